```python
import math
import jax, jax.numpy as jnp
from jax import lax
import numpy as np

D_MODEL = 2048
BATCH = 8
SEQ = 8192
DEPTH = 1

MIX_WIDTH = D_MODEL
SSM_WIDTH = MIX_WIDTH // 2
SSM_GROUP = 16
SSM_GROUPS = SSM_WIDTH // SSM_GROUP
SSM_STATE = 64
SGU_WIDTH = MIX_WIDTH - SSM_WIDTH
SGU_HEADS = 8
SGU_HEAD_DIM = SGU_WIDTH // SGU_HEADS
SGU_CHUNK = 128
IN_WIDTH = SSM_WIDTH + 2 * SGU_WIDTH
D_FF = 4 * D_MODEL
EPS = 1e-6
DT_MIN = 1e-3
DT_MAX = 1e-1

kernel_name = "hymba_style_s5_sgu_hybrid_layer"


def rmsnorm(x, g):
    xf = x.astype(jnp.float32)
    xf = xf * lax.rsqrt(jnp.mean(xf * xf, axis=-1, keepdims=True) + EPS)
    return (xf * g.astype(jnp.float32)).astype(x.dtype)


def layernorm(x, g, b):
    xf = x.astype(jnp.float32)
    mu = jnp.mean(xf, axis=-1, keepdims=True)
    var = jnp.mean(jnp.square(xf - mu), axis=-1, keepdims=True)
    y = (xf - mu) * lax.rsqrt(var + EPS) * g.astype(jnp.float32) + b.astype(jnp.float32)
    return y.astype(x.dtype)


def _scan_combine(left, right):
    a_l, b_l = left
    a_r, b_r = right
    return a_l * a_r, a_r * b_l + b_r


def s5_mixer(u, a_re, a_im, b_re, b_im, c_re, c_im, d, log_dt, glu_w, glu_b):
    dtype = u.dtype
    bsz, seq, _ = u.shape
    uf = u.astype(jnp.float32).reshape(bsz, seq, SSM_GROUPS, SSM_GROUP)
    lam = lax.complex(a_re.astype(jnp.float32), a_im.astype(jnp.float32))
    dt = jnp.exp(log_dt.astype(jnp.float32))[:, None]
    a_bar = jnp.exp(lam * dt)
    b_mat = lax.complex(b_re.astype(jnp.float32), b_im.astype(jnp.float32))
    b_bar = ((a_bar - 1.0) / lam)[..., None] * b_mat
    bu = jnp.einsum('bsgh,gph->bsgp', uf.astype(jnp.complex64), b_bar)
    a_seq = jnp.broadcast_to(a_bar, bu.shape)
    _, states = lax.associative_scan(_scan_combine, (a_seq, bu), axis=1)
    c_mat = lax.complex(c_re.astype(jnp.float32), c_im.astype(jnp.float32))
    y = jnp.einsum('bsgp,ghp->bsgh', states, c_mat).real + d.astype(jnp.float32) * uf
    y = jax.nn.gelu(y.reshape(bsz, seq, SSM_WIDTH))
    gate = jax.nn.sigmoid(y @ glu_w.astype(jnp.float32) + glu_b.astype(jnp.float32))
    return (y * gate).astype(dtype)


def sgu_mixer(z, ln_g, ln_b, w_s, b_s):
    bsz, seq, _ = z.shape
    u, v = jnp.split(jax.nn.gelu(z), 2, axis=-1)
    v = layernorm(v, ln_g, ln_b)
    v = v.reshape(bsz, seq // SGU_CHUNK, SGU_CHUNK, SGU_HEADS, SGU_HEAD_DIM)
    causal = jnp.tril(jnp.ones((SGU_CHUNK, SGU_CHUNK), dtype=bool))
    w = jnp.where(causal[None], w_s, jnp.zeros_like(w_s))
    mixed = jnp.einsum('hts,bcshd->bcthd', w, v) + b_s.T[:, :, None]
    return u * mixed.reshape(bsz, seq, SGU_WIDTH)


def _fwd_setup_inputs(seed: int = 0) -> dict:
    key = jax.random.key(seed)
    ks = jax.random.split(key, 24)
    L = DEPTH
    G, P, H = SSM_GROUPS, SSM_STATE, SSM_GROUP
    f32 = jnp.float32
    nrm = lambda k, shape: jax.random.normal(k, shape, f32)
    x = nrm(ks[0], (BATCH, SEQ, D_MODEL))
    norm_mix_g = 1.0 + 0.02 * nrm(ks[1], (L, D_MODEL))
    w_in = nrm(ks[2], (L, D_MODEL, IN_WIDTH)) * D_MODEL ** -0.5
    n = jnp.arange(P, dtype=f32)
    ssm_a_re = -0.5 + 0.01 * nrm(ks[3], (L, G, P))
    ssm_a_im = math.pi * n + 0.01 * nrm(ks[4], (L, G, P))
    ssm_b_re = nrm(ks[5], (L, G, P, H)) * (2.0 * H) ** -0.5
    ssm_b_im = nrm(ks[6], (L, G, P, H)) * (2.0 * H) ** -0.5
    ssm_c_re = nrm(ks[7], (L, G, H, P)) * (2.0 * P) ** -0.5
    ssm_c_im = nrm(ks[8], (L, G, H, P)) * (2.0 * P) ** -0.5
    ssm_d = nrm(ks[9], (L, G, H))
    ssm_log_dt = jax.random.uniform(ks[10], (L, G), f32, math.log(DT_MIN), math.log(DT_MAX))
    ssm_glu_w = nrm(ks[11], (L, SSM_WIDTH, SSM_WIDTH)) * SSM_WIDTH ** -0.5
    ssm_glu_b = 0.01 * nrm(ks[12], (L, SSM_WIDTH))
    sgu_ln_g = 1.0 + 0.02 * nrm(ks[13], (L, SGU_WIDTH))
    sgu_ln_b = 0.01 * nrm(ks[14], (L, SGU_WIDTH))
    sgu_w = nrm(ks[15], (L, SGU_HEADS, SGU_CHUNK, SGU_CHUNK)) * 0.5 * SGU_CHUNK ** -0.5
    sgu_b = 1.0 + 0.01 * nrm(ks[16], (L, SGU_HEADS, SGU_CHUNK))
    out_norm_ssm_g = 1.0 + 0.02 * nrm(ks[17], (L, SSM_WIDTH))
    out_norm_sgu_g = 1.0 + 0.02 * nrm(ks[18], (L, SGU_WIDTH))
    w_out = nrm(ks[19], (L, MIX_WIDTH, D_MODEL)) * MIX_WIDTH ** -0.5
    norm_mlp_g = 1.0 + 0.02 * nrm(ks[20], (L, D_MODEL))
    w_up = nrm(ks[21], (L, D_MODEL, D_FF)) * D_MODEL ** -0.5
    w_down = nrm(ks[22], (L, D_FF, D_MODEL)) * D_FF ** -0.5
    norm_final_g = 1.0 + 0.02 * nrm(ks[23], (D_MODEL,))
    return {"x": x, "norm_mix_g": norm_mix_g, "w_in": w_in,
            "ssm_a_re": ssm_a_re, "ssm_a_im": ssm_a_im,
            "ssm_b_re": ssm_b_re, "ssm_b_im": ssm_b_im,
            "ssm_c_re": ssm_c_re, "ssm_c_im": ssm_c_im,
            "ssm_d": ssm_d, "ssm_log_dt": ssm_log_dt,
            "ssm_glu_w": ssm_glu_w, "ssm_glu_b": ssm_glu_b,
            "sgu_ln_g": sgu_ln_g, "sgu_ln_b": sgu_ln_b,
            "sgu_w": sgu_w, "sgu_b": sgu_b,
            "out_norm_ssm_g": out_norm_ssm_g, "out_norm_sgu_g": out_norm_sgu_g,
            "w_out": w_out, "norm_mlp_g": norm_mlp_g,
            "w_up": w_up, "w_down": w_down, "norm_final_g": norm_final_g}


def _fwd_reference(x, norm_mix_g, w_in, ssm_a_re, ssm_a_im, ssm_b_re, ssm_b_im,
              ssm_c_re, ssm_c_im, ssm_d, ssm_log_dt, ssm_glu_w, ssm_glu_b,
              sgu_ln_g, sgu_ln_b, sgu_w, sgu_b, out_norm_ssm_g, out_norm_sgu_g,
              w_out, norm_mlp_g, w_up, w_down, norm_final_g):
    for l in range(DEPTH):
        h = rmsnorm(x, norm_mix_g[l])
        z = h @ w_in[l]
        z_ssm = z[..., :SSM_WIDTH]
        z_sgu = z[..., SSM_WIDTH:]
        y_ssm = s5_mixer(z_ssm, ssm_a_re[l], ssm_a_im[l], ssm_b_re[l], ssm_b_im[l],
                         ssm_c_re[l], ssm_c_im[l], ssm_d[l], ssm_log_dt[l],
                         ssm_glu_w[l], ssm_glu_b[l])
        y_sgu = sgu_mixer(z_sgu, sgu_ln_g[l], sgu_ln_b[l], sgu_w[l], sgu_b[l])
        mixed = jnp.concatenate([rmsnorm(y_ssm, out_norm_ssm_g[l]),
                                 rmsnorm(y_sgu, out_norm_sgu_g[l])], axis=-1)
        x = x + mixed @ w_out[l]
        h = rmsnorm(x, norm_mlp_g[l])
        x = x + jnp.square(jax.nn.relu(h @ w_up[l])) @ w_down[l]
    return rmsnorm(x, norm_final_g)


import jax as _jax
import jax.numpy as _jnp

TWIN_FORMAT = 'train_step'
FWD_PARAMS = ['x', 'norm_mix_g', 'w_in', 'ssm_a_re', 'ssm_a_im', 'ssm_b_re', 'ssm_b_im', 'ssm_c_re', 'ssm_c_im', 'ssm_d', 'ssm_log_dt', 'ssm_glu_w', 'ssm_glu_b', 'sgu_ln_g', 'sgu_ln_b', 'sgu_w', 'sgu_b', 'out_norm_ssm_g', 'out_norm_sgu_g', 'w_out', 'norm_mlp_g', 'w_up', 'w_down', 'norm_final_g']
TWIN_WEIGHTS = ['norm_mix_g', 'w_in', 'ssm_a_re', 'ssm_a_im', 'ssm_b_re', 'ssm_b_im', 'ssm_c_re', 'ssm_c_im', 'ssm_d', 'ssm_log_dt', 'ssm_glu_w', 'ssm_glu_b', 'sgu_ln_g', 'sgu_ln_b', 'sgu_w', 'sgu_b', 'out_norm_ssm_g', 'out_norm_sgu_g', 'w_out', 'norm_mlp_g', 'w_up', 'w_down', 'norm_final_g']
TWIN_DIFF_INPUT = 'x'
TWIN_INPUTS = ['x', 'norm_mix_g', 'w_in', 'ssm_a_re', 'ssm_a_im', 'ssm_b_re', 'ssm_b_im', 'ssm_c_re', 'ssm_c_im', 'ssm_d', 'ssm_log_dt', 'ssm_glu_w', 'ssm_glu_b', 'sgu_ln_g', 'sgu_ln_b', 'sgu_w', 'sgu_b', 'out_norm_ssm_g', 'out_norm_sgu_g', 'w_out', 'norm_mlp_g', 'w_up', 'w_down', 'norm_final_g', 'loss_target', 'm_norm_mix_g', 'm_w_in', 'm_ssm_a_re', 'm_ssm_a_im', 'm_ssm_b_re', 'm_ssm_b_im', 'm_ssm_c_re', 'm_ssm_c_im', 'm_ssm_d', 'm_ssm_log_dt', 'm_ssm_glu_w', 'm_ssm_glu_b', 'm_sgu_ln_g', 'm_sgu_ln_b', 'm_sgu_w', 'm_sgu_b', 'm_out_norm_ssm_g', 'm_out_norm_sgu_g', 'm_w_out', 'm_norm_mlp_g', 'm_w_up', 'm_w_down', 'm_norm_final_g', 'v_norm_mix_g', 'v_w_in', 'v_ssm_a_re', 'v_ssm_a_im', 'v_ssm_b_re', 'v_ssm_b_im', 'v_ssm_c_re', 'v_ssm_c_im', 'v_ssm_d', 'v_ssm_log_dt', 'v_ssm_glu_w', 'v_ssm_glu_b', 'v_sgu_ln_g', 'v_sgu_ln_b', 'v_sgu_w', 'v_sgu_b', 'v_out_norm_ssm_g', 'v_out_norm_sgu_g', 'v_w_out', 'v_norm_mlp_g', 'v_w_up', 'v_w_down', 'v_norm_final_g']
TWIN_OUTPUTS = ['loss', 'grad_x', 'grad_norm_mix_g', 'grad_w_in', 'grad_ssm_a_re', 'grad_ssm_a_im', 'grad_ssm_b_re', 'grad_ssm_b_im', 'grad_ssm_c_re', 'grad_ssm_c_im', 'grad_ssm_d', 'grad_ssm_log_dt', 'grad_ssm_glu_w', 'grad_ssm_glu_b', 'grad_sgu_ln_g', 'grad_sgu_ln_b', 'grad_sgu_w', 'grad_sgu_b', 'grad_out_norm_ssm_g', 'grad_out_norm_sgu_g', 'grad_w_out', 'grad_norm_mlp_g', 'grad_w_up', 'grad_w_down', 'grad_norm_final_g', 'delta_norm_mix_g', 'delta_w_in', 'delta_ssm_a_re', 'delta_ssm_a_im', 'delta_ssm_b_re', 'delta_ssm_b_im', 'delta_ssm_c_re', 'delta_ssm_c_im', 'delta_ssm_d', 'delta_ssm_log_dt', 'delta_ssm_glu_w', 'delta_ssm_glu_b', 'delta_sgu_ln_g', 'delta_sgu_ln_b', 'delta_sgu_w', 'delta_sgu_b', 'delta_out_norm_ssm_g', 'delta_out_norm_sgu_g', 'delta_w_out', 'delta_norm_mlp_g', 'delta_w_up', 'delta_w_down', 'delta_norm_final_g', 'new_m_norm_mix_g', 'new_m_w_in', 'new_m_ssm_a_re', 'new_m_ssm_a_im', 'new_m_ssm_b_re', 'new_m_ssm_b_im', 'new_m_ssm_c_re', 'new_m_ssm_c_im', 'new_m_ssm_d', 'new_m_ssm_log_dt', 'new_m_ssm_glu_w', 'new_m_ssm_glu_b', 'new_m_sgu_ln_g', 'new_m_sgu_ln_b', 'new_m_sgu_w', 'new_m_sgu_b', 'new_m_out_norm_ssm_g', 'new_m_out_norm_sgu_g', 'new_m_w_out', 'new_m_norm_mlp_g', 'new_m_w_up', 'new_m_w_down', 'new_m_norm_final_g', 'new_v_norm_mix_g', 'new_v_w_in', 'new_v_ssm_a_re', 'new_v_ssm_a_im', 'new_v_ssm_b_re', 'new_v_ssm_b_im', 'new_v_ssm_c_re', 'new_v_ssm_c_im', 'new_v_ssm_d', 'new_v_ssm_log_dt', 'new_v_ssm_glu_w', 'new_v_ssm_glu_b', 'new_v_sgu_ln_g', 'new_v_sgu_ln_b', 'new_v_sgu_w', 'new_v_sgu_b', 'new_v_out_norm_ssm_g', 'new_v_out_norm_sgu_g', 'new_v_w_out', 'new_v_norm_mlp_g', 'new_v_w_up', 'new_v_w_down', 'new_v_norm_final_g']
TWIN_LEAF_KINDS = {'loss': 'loss', 'grad_x': 'grad_x', 'grad_norm_mix_g': 'grad_w', 'grad_w_in': 'grad_w', 'grad_ssm_a_re': 'grad_w', 'grad_ssm_a_im': 'grad_w', 'grad_ssm_b_re': 'grad_w', 'grad_ssm_b_im': 'grad_w', 'grad_ssm_c_re': 'grad_w', 'grad_ssm_c_im': 'grad_w', 'grad_ssm_d': 'grad_w', 'grad_ssm_log_dt': 'grad_w', 'grad_ssm_glu_w': 'grad_w', 'grad_ssm_glu_b': 'grad_w', 'grad_sgu_ln_g': 'grad_w', 'grad_sgu_ln_b': 'grad_w', 'grad_sgu_w': 'grad_w', 'grad_sgu_b': 'grad_w', 'grad_out_norm_ssm_g': 'grad_w', 'grad_out_norm_sgu_g': 'grad_w', 'grad_w_out': 'grad_w', 'grad_norm_mlp_g': 'grad_w', 'grad_w_up': 'grad_w', 'grad_w_down': 'grad_w', 'grad_norm_final_g': 'grad_w', 'delta_norm_mix_g': 'delta_w', 'delta_w_in': 'delta_w', 'delta_ssm_a_re': 'delta_w', 'delta_ssm_a_im': 'delta_w', 'delta_ssm_b_re': 'delta_w', 'delta_ssm_b_im': 'delta_w', 'delta_ssm_c_re': 'delta_w', 'delta_ssm_c_im': 'delta_w', 'delta_ssm_d': 'delta_w', 'delta_ssm_log_dt': 'delta_w', 'delta_ssm_glu_w': 'delta_w', 'delta_ssm_glu_b': 'delta_w', 'delta_sgu_ln_g': 'delta_w', 'delta_sgu_ln_b': 'delta_w', 'delta_sgu_w': 'delta_w', 'delta_sgu_b': 'delta_w', 'delta_out_norm_ssm_g': 'delta_w', 'delta_out_norm_sgu_g': 'delta_w', 'delta_w_out': 'delta_w', 'delta_norm_mlp_g': 'delta_w', 'delta_w_up': 'delta_w', 'delta_w_down': 'delta_w', 'delta_norm_final_g': 'delta_w', 'new_m_norm_mix_g': 'new_m', 'new_m_w_in': 'new_m', 'new_m_ssm_a_re': 'new_m', 'new_m_ssm_a_im': 'new_m', 'new_m_ssm_b_re': 'new_m', 'new_m_ssm_b_im': 'new_m', 'new_m_ssm_c_re': 'new_m', 'new_m_ssm_c_im': 'new_m', 'new_m_ssm_d': 'new_m', 'new_m_ssm_log_dt': 'new_m', 'new_m_ssm_glu_w': 'new_m', 'new_m_ssm_glu_b': 'new_m', 'new_m_sgu_ln_g': 'new_m', 'new_m_sgu_ln_b': 'new_m', 'new_m_sgu_w': 'new_m', 'new_m_sgu_b': 'new_m', 'new_m_out_norm_ssm_g': 'new_m', 'new_m_out_norm_sgu_g': 'new_m', 'new_m_w_out': 'new_m', 'new_m_norm_mlp_g': 'new_m', 'new_m_w_up': 'new_m', 'new_m_w_down': 'new_m', 'new_m_norm_final_g': 'new_m', 'new_v_norm_mix_g': 'new_v', 'new_v_w_in': 'new_v', 'new_v_ssm_a_re': 'new_v', 'new_v_ssm_a_im': 'new_v', 'new_v_ssm_b_re': 'new_v', 'new_v_ssm_b_im': 'new_v', 'new_v_ssm_c_re': 'new_v', 'new_v_ssm_c_im': 'new_v', 'new_v_ssm_d': 'new_v', 'new_v_ssm_log_dt': 'new_v', 'new_v_ssm_glu_w': 'new_v', 'new_v_ssm_glu_b': 'new_v', 'new_v_sgu_ln_g': 'new_v', 'new_v_sgu_ln_b': 'new_v', 'new_v_sgu_w': 'new_v', 'new_v_sgu_b': 'new_v', 'new_v_out_norm_ssm_g': 'new_v', 'new_v_out_norm_sgu_g': 'new_v', 'new_v_w_out': 'new_v', 'new_v_norm_mlp_g': 'new_v', 'new_v_w_up': 'new_v', 'new_v_w_down': 'new_v', 'new_v_norm_final_g': 'new_v'}


def _forward(args):
    return _fwd_reference(*[args[k] for k in FWD_PARAMS])


def _output_shape():
    def fwd():
        inp = _fwd_setup_inputs(0)
        return _fwd_reference(*[inp[k] for k in FWD_PARAMS])
    out = _jax.eval_shape(fwd)
    return out.shape, out.dtype

N_MICROBATCH = 1
ADAM_LR = 0.001
ADAM_B1 = 0.9
ADAM_B2 = 0.999
ADAM_EPS = 1e-08
ADAM_WD = 0.01
ADAM_STEP = 10
PER_EXAMPLE_BATCH_AXIS = {'x': 0, 'loss_target': 0}
SHARED_INPUTS = []
_WEIGHT_DTYPES = {'norm_mix_g': _jnp.float32, 'w_in': _jnp.float32, 'ssm_a_re': _jnp.float32, 'ssm_a_im': _jnp.float32, 'ssm_b_re': _jnp.float32, 'ssm_b_im': _jnp.float32, 'ssm_c_re': _jnp.float32, 'ssm_c_im': _jnp.float32, 'ssm_d': _jnp.float32, 'ssm_log_dt': _jnp.float32, 'ssm_glu_w': _jnp.float32, 'ssm_glu_b': _jnp.float32, 'sgu_ln_g': _jnp.float32, 'sgu_ln_b': _jnp.float32, 'sgu_w': _jnp.float32, 'sgu_b': _jnp.float32, 'out_norm_ssm_g': _jnp.float32, 'out_norm_sgu_g': _jnp.float32, 'w_out': _jnp.float32, 'norm_mlp_g': _jnp.float32, 'w_up': _jnp.float32, 'w_down': _jnp.float32, 'norm_final_g': _jnp.float32}
MOMENT_SCALE = {'norm_mix_g': 1.050999e-01, 'w_in': 8.447992e-02, 'ssm_a_re': 4.901754e-03, 'ssm_a_im': 4.965527e-03, 'ssm_b_re': 3.311534e-03, 'ssm_b_im': 3.375628e-03, 'ssm_c_re': 6.723710e-03, 'ssm_c_im': 6.557203e-03, 'ssm_d': 1.348681e-01, 'ssm_log_dt': 3.844053e+00, 'ssm_glu_w': 2.961098e-02, 'ssm_glu_b': 5.562666e-02, 'sgu_ln_g': 3.048552e-02, 'sgu_ln_b': 3.109219e-02, 'sgu_w': 6.214653e-02, 'sgu_b': 8.258287e-02, 'out_norm_ssm_g': 1.082335e-01, 'out_norm_sgu_g': 1.226938e-01, 'w_out': 1.195666e-01, 'norm_mlp_g': 1.043248e-01, 'w_up': 5.112953e-02, 'w_down': 1.163991e-01, 'norm_final_g': 3.259292e+01}


def _to_microbatches(a, axis):
    t = _jnp.moveaxis(a, axis, 0)
    t = t.reshape((N_MICROBATCH, t.shape[0] // N_MICROBATCH) + t.shape[1:])
    return _jnp.moveaxis(t, 1, axis + 1)


def setup_inputs(seed: int = 0) -> dict:
    inp = _fwd_setup_inputs(seed)
    key = _jax.random.fold_in(_jax.random.key(seed), 7919)
    shape, _ = _output_shape()
    out = dict(inp)
    out["loss_target"] = _jax.random.normal(_jax.random.fold_in(key, 0), shape, _jnp.float32)
    for i, name in enumerate(TWIN_WEIGHTS):
        w = inp[name].astype(_jnp.float32)
        if MOMENT_SCALE is None:
            s = _jnp.sqrt(_jnp.mean(_jnp.square(w)) + 1e-30)
        else:
            s = MOMENT_SCALE[name]
        km, kv = _jax.random.split(_jax.random.fold_in(key, i + 1))
        out[name] = w
        out["m_" + name] = s * _jax.random.normal(km, w.shape, _jnp.float32)
        out["v_" + name] = (s * s) * _jax.random.uniform(kv, w.shape, _jnp.float32, 0.5, 1.5)
    if N_MICROBATCH > 1:
        for name, axis in PER_EXAMPLE_BATCH_AXIS.items():
            out[name] = _to_microbatches(out[name], axis)
    return {'x': out['x'], 'norm_mix_g': out['norm_mix_g'], 'w_in': out['w_in'], 'ssm_a_re': out['ssm_a_re'], 'ssm_a_im': out['ssm_a_im'], 'ssm_b_re': out['ssm_b_re'], 'ssm_b_im': out['ssm_b_im'], 'ssm_c_re': out['ssm_c_re'], 'ssm_c_im': out['ssm_c_im'], 'ssm_d': out['ssm_d'], 'ssm_log_dt': out['ssm_log_dt'], 'ssm_glu_w': out['ssm_glu_w'], 'ssm_glu_b': out['ssm_glu_b'], 'sgu_ln_g': out['sgu_ln_g'], 'sgu_ln_b': out['sgu_ln_b'], 'sgu_w': out['sgu_w'], 'sgu_b': out['sgu_b'], 'out_norm_ssm_g': out['out_norm_ssm_g'], 'out_norm_sgu_g': out['out_norm_sgu_g'], 'w_out': out['w_out'], 'norm_mlp_g': out['norm_mlp_g'], 'w_up': out['w_up'], 'w_down': out['w_down'], 'norm_final_g': out['norm_final_g'], 'loss_target': out['loss_target'], 'm_norm_mix_g': out['m_norm_mix_g'], 'm_w_in': out['m_w_in'], 'm_ssm_a_re': out['m_ssm_a_re'], 'm_ssm_a_im': out['m_ssm_a_im'], 'm_ssm_b_re': out['m_ssm_b_re'], 'm_ssm_b_im': out['m_ssm_b_im'], 'm_ssm_c_re': out['m_ssm_c_re'], 'm_ssm_c_im': out['m_ssm_c_im'], 'm_ssm_d': out['m_ssm_d'], 'm_ssm_log_dt': out['m_ssm_log_dt'], 'm_ssm_glu_w': out['m_ssm_glu_w'], 'm_ssm_glu_b': out['m_ssm_glu_b'], 'm_sgu_ln_g': out['m_sgu_ln_g'], 'm_sgu_ln_b': out['m_sgu_ln_b'], 'm_sgu_w': out['m_sgu_w'], 'm_sgu_b': out['m_sgu_b'], 'm_out_norm_ssm_g': out['m_out_norm_ssm_g'], 'm_out_norm_sgu_g': out['m_out_norm_sgu_g'], 'm_w_out': out['m_w_out'], 'm_norm_mlp_g': out['m_norm_mlp_g'], 'm_w_up': out['m_w_up'], 'm_w_down': out['m_w_down'], 'm_norm_final_g': out['m_norm_final_g'], 'v_norm_mix_g': out['v_norm_mix_g'], 'v_w_in': out['v_w_in'], 'v_ssm_a_re': out['v_ssm_a_re'], 'v_ssm_a_im': out['v_ssm_a_im'], 'v_ssm_b_re': out['v_ssm_b_re'], 'v_ssm_b_im': out['v_ssm_b_im'], 'v_ssm_c_re': out['v_ssm_c_re'], 'v_ssm_c_im': out['v_ssm_c_im'], 'v_ssm_d': out['v_ssm_d'], 'v_ssm_log_dt': out['v_ssm_log_dt'], 'v_ssm_glu_w': out['v_ssm_glu_w'], 'v_ssm_glu_b': out['v_ssm_glu_b'], 'v_sgu_ln_g': out['v_sgu_ln_g'], 'v_sgu_ln_b': out['v_sgu_ln_b'], 'v_sgu_w': out['v_sgu_w'], 'v_sgu_b': out['v_sgu_b'], 'v_out_norm_ssm_g': out['v_out_norm_ssm_g'], 'v_out_norm_sgu_g': out['v_out_norm_sgu_g'], 'v_w_out': out['v_w_out'], 'v_norm_mlp_g': out['v_norm_mlp_g'], 'v_w_up': out['v_w_up'], 'v_w_down': out['v_w_down'], 'v_norm_final_g': out['v_norm_final_g']}


def _loss(weights, diff, rest, loss_target):
    with _jax.named_scope("forward"):
        args = {**rest, TWIN_DIFF_INPUT: diff, **{k: w.astype(_WEIGHT_DTYPES[k]) for k, w in weights.items()}}
        y = _forward(args)
    with _jax.named_scope("loss_head"):
        err = _jnp.square(y.astype(_jnp.float32) - loss_target)
        return 0.5 * _jnp.sum(_jnp.mean(err, axis=-1)) if err.ndim else 0.5 * err


def _adamw(w, g, m, v):
    m = ADAM_B1 * m + (1.0 - ADAM_B1) * g
    v = ADAM_B2 * v + (1.0 - ADAM_B2) * _jnp.square(g)
    m_hat = m / (1.0 - ADAM_B1 ** ADAM_STEP)
    v_hat = v / (1.0 - ADAM_B2 ** ADAM_STEP)
    delta = -ADAM_LR * (m_hat / (_jnp.sqrt(v_hat) + ADAM_EPS) + ADAM_WD * w)
    return delta, m, v


def reference(x, norm_mix_g, w_in, ssm_a_re, ssm_a_im, ssm_b_re, ssm_b_im, ssm_c_re, ssm_c_im, ssm_d, ssm_log_dt, ssm_glu_w, ssm_glu_b, sgu_ln_g, sgu_ln_b, sgu_w, sgu_b, out_norm_ssm_g, out_norm_sgu_g, w_out, norm_mlp_g, w_up, w_down, norm_final_g, loss_target, m_norm_mix_g, m_w_in, m_ssm_a_re, m_ssm_a_im, m_ssm_b_re, m_ssm_b_im, m_ssm_c_re, m_ssm_c_im, m_ssm_d, m_ssm_log_dt, m_ssm_glu_w, m_ssm_glu_b, m_sgu_ln_g, m_sgu_ln_b, m_sgu_w, m_sgu_b, m_out_norm_ssm_g, m_out_norm_sgu_g, m_w_out, m_norm_mlp_g, m_w_up, m_w_down, m_norm_final_g, v_norm_mix_g, v_w_in, v_ssm_a_re, v_ssm_a_im, v_ssm_b_re, v_ssm_b_im, v_ssm_c_re, v_ssm_c_im, v_ssm_d, v_ssm_log_dt, v_ssm_glu_w, v_ssm_glu_b, v_sgu_ln_g, v_sgu_ln_b, v_sgu_w, v_sgu_b, v_out_norm_ssm_g, v_out_norm_sgu_g, v_w_out, v_norm_mlp_g, v_w_up, v_w_down, v_norm_final_g):
    given = dict(x=x, norm_mix_g=norm_mix_g, w_in=w_in, ssm_a_re=ssm_a_re, ssm_a_im=ssm_a_im, ssm_b_re=ssm_b_re, ssm_b_im=ssm_b_im, ssm_c_re=ssm_c_re, ssm_c_im=ssm_c_im, ssm_d=ssm_d, ssm_log_dt=ssm_log_dt, ssm_glu_w=ssm_glu_w, ssm_glu_b=ssm_glu_b, sgu_ln_g=sgu_ln_g, sgu_ln_b=sgu_ln_b, sgu_w=sgu_w, sgu_b=sgu_b, out_norm_ssm_g=out_norm_ssm_g, out_norm_sgu_g=out_norm_sgu_g, w_out=w_out, norm_mlp_g=norm_mlp_g, w_up=w_up, w_down=w_down, norm_final_g=norm_final_g, loss_target=loss_target, m_norm_mix_g=m_norm_mix_g, m_w_in=m_w_in, m_ssm_a_re=m_ssm_a_re, m_ssm_a_im=m_ssm_a_im, m_ssm_b_re=m_ssm_b_re, m_ssm_b_im=m_ssm_b_im, m_ssm_c_re=m_ssm_c_re, m_ssm_c_im=m_ssm_c_im, m_ssm_d=m_ssm_d, m_ssm_log_dt=m_ssm_log_dt, m_ssm_glu_w=m_ssm_glu_w, m_ssm_glu_b=m_ssm_glu_b, m_sgu_ln_g=m_sgu_ln_g, m_sgu_ln_b=m_sgu_ln_b, m_sgu_w=m_sgu_w, m_sgu_b=m_sgu_b, m_out_norm_ssm_g=m_out_norm_ssm_g, m_out_norm_sgu_g=m_out_norm_sgu_g, m_w_out=m_w_out, m_norm_mlp_g=m_norm_mlp_g, m_w_up=m_w_up, m_w_down=m_w_down, m_norm_final_g=m_norm_final_g, v_norm_mix_g=v_norm_mix_g, v_w_in=v_w_in, v_ssm_a_re=v_ssm_a_re, v_ssm_a_im=v_ssm_a_im, v_ssm_b_re=v_ssm_b_re, v_ssm_b_im=v_ssm_b_im, v_ssm_c_re=v_ssm_c_re, v_ssm_c_im=v_ssm_c_im, v_ssm_d=v_ssm_d, v_ssm_log_dt=v_ssm_log_dt, v_ssm_glu_w=v_ssm_glu_w, v_ssm_glu_b=v_ssm_glu_b, v_sgu_ln_g=v_sgu_ln_g, v_sgu_ln_b=v_sgu_ln_b, v_sgu_w=v_sgu_w, v_sgu_b=v_sgu_b, v_out_norm_ssm_g=v_out_norm_ssm_g, v_out_norm_sgu_g=v_out_norm_sgu_g, v_w_out=v_w_out, v_norm_mlp_g=v_norm_mlp_g, v_w_up=v_w_up, v_w_down=v_w_down, v_norm_final_g=v_norm_final_g)
    weights = {n: given[n] for n in TWIN_WEIGHTS}
    shared = {n: given[n] for n in SHARED_INPUTS}
    per_example = {n: given[n] for n in ['x']}
    grad_fn = _jax.value_and_grad(_loss, argnums=(0, 1))

    def one_microbatch(ex, loss_target):
        ex = dict(ex)
        diff = ex.pop(TWIN_DIFF_INPUT)
        return grad_fn(weights, diff, {**shared, **ex}, loss_target)

    if N_MICROBATCH == 1:
        loss, (grad_w, grad_x) = one_microbatch(per_example, given["loss_target"])
    else:
        def body(carry, xs):
            loss_sum, grad_sum = carry
            l_k, (gw_k, gx_k) = one_microbatch(xs[0], xs[1])
            with _jax.named_scope("update"):
                return (loss_sum + l_k, _jax.tree.map(_jnp.add, grad_sum, gw_k)), gx_k

        init = (_jnp.zeros((), _jnp.float32), _jax.tree.map(_jnp.zeros_like, weights))
        (loss, grad_w), grad_x = _jax.lax.scan(body, init, (per_example, given["loss_target"]))
    with _jax.named_scope("update"):
        delta_w, new_m, new_v = {}, {}, {}
        for n in TWIN_WEIGHTS:
            delta_w[n], new_m[n], new_v[n] = _adamw(weights[n], grad_w[n], given["m_" + n], given["v_" + n])
    return (loss, grad_x, *[grad_w[n] for n in TWIN_WEIGHTS], *[delta_w[n] for n in TWIN_WEIGHTS],
            *[new_m[n] for n in TWIN_WEIGHTS], *[new_v[n] for n in TWIN_WEIGHTS])
```

```python
import functools
import math

import jax
import jax.numpy as jnp
from jax import lax
from jax.experimental import pallas as pl
from jax.experimental.pallas import tpu as pltpu

F32 = jnp.float32
BF16 = jnp.bfloat16
MESH = pl.DeviceIdType.MESH
HIGHEST = lax.Precision.HIGHEST

EPS = 1e-6
ADAM_LR = 0.001
ADAM_B1 = 0.9
ADAM_B2 = 0.999
ADAM_EPS = 1e-08
ADAM_WD = 0.01
ADAM_STEP = 10

N_CHIPS = 4
LANES = 128
SSM_GROUP = 16
SSM_STATE = 64
GROUPS_PER_TILE = LANES // SSM_GROUP
CHUNK = 16
PAIRS = CHUNK // 2
SGU_CHUNK = 128
VMEM_LIMIT = 56 * 2**20


def _params(n_axes, vmem=VMEM_LIMIT):
    return pltpu.CompilerParams(dimension_semantics=("arbitrary",) * n_axes, vmem_limit_bytes=vmem)


def _call(body, **kw):
    return pl.pallas_call(body, **kw)


def _dot(a, b):
    return jnp.dot(a, b, preferred_element_type=F32)


def _dot_nt(a, b):
    return lax.dot_general(a, b, (((1,), (1,)), ((), ())), preferred_element_type=F32)


def _dot_tn(a, b):
    return lax.dot_general(a, b, (((0,), (0,)), ((), ())), preferred_element_type=F32)


_GELU_K = math.sqrt(2.0 / math.pi)
_GELU_C = 0.044715


def _gelu(x):
    return 0.5 * x * (1.0 + jnp.tanh(_GELU_K * (x + _GELU_C * x * x * x)))


def _gelu_grad(x):
    t = jnp.tanh(_GELU_K * (x + _GELU_C * x * x * x))
    return 0.5 * (1.0 + t) + 0.5 * x * (1.0 - t * t) * (_GELU_K * (1.0 + 3.0 * _GELU_C * x * x))


def _sigmoid(x):
    return 1.0 / (1.0 + jnp.exp(-x))


def _rms(x):
    return lax.rsqrt(jnp.mean(x * x, axis=-1, keepdims=True) + EPS)


def _rms_bwd(dy, x, r, g):
    a = dy * g
    dx = r * a - x * (r * r * r) * jnp.mean(a * x, axis=-1, keepdims=True)
    return dx, dy * x * r


def _row_tile(rows, target, mult=16):
    for t in range(min(rows, target), 0, -1):
        if rows % t == 0 and t % mult == 0:
            return t
    return rows


def _ssm_mats(a_re, a_im, b_re, b_im, c_re, c_im, d, log_dt):
    g, p = a_re.shape
    h = b_re.shape[-1]
    nt = g // GROUPS_PER_TILE
    dt = jnp.exp(log_dt)[:, None]
    lr, li = a_re * dt, a_im * dt

    def apow(l):
        mag = jnp.exp(lr * l)
        return mag * jnp.cos(li * l), mag * jnp.sin(li * l)

    ar, ai = apow(1.0)
    den = a_re * a_re + a_im * a_im
    qr = ((ar - 1.0) * a_re + ai * a_im) / den
    qi = (ai * a_re - (ar - 1.0) * a_im) / den
    bbr = qr[..., None] * b_re - qi[..., None] * b_im
    bbi = qr[..., None] * b_im + qi[..., None] * b_re
    ls = jnp.arange(CHUNK + 1, dtype=F32)[:, None, None]
    pr, pi = apow(ls)
    tr = pr[:CHUNK, :, :, None] * bbr[None] - pi[:CHUNK, :, :, None] * bbi[None]
    ti = pr[:CHUNK, :, :, None] * bbi[None] + pi[:CHUNK, :, :, None] * bbr[None]
    k = jnp.einsum("gop,lgpi->lgio", c_re, tr, precision=HIGHEST) - jnp.einsum(
        "gop,lgpi->lgio", c_im, ti, precision=HIGHEST
    )
    k = k.at[0].add(d[:, :, None] * jnp.eye(h, dtype=F32))
    eye = jnp.eye(GROUPS_PER_TILE, dtype=F32)

    kb = k.reshape(CHUNK, nt, GROUPS_PER_TILE, h, 1, h) * eye[None, None, :, None, :, None]
    kb = kb.reshape(CHUNK, nt, LANES, LANES)
    wj = jnp.stack([tr[::-1], ti[::-1]], axis=2)
    wj = jnp.transpose(wj, (0, 1, 4, 2, 3))
    wb = wj.reshape(CHUNK, nt, GROUPS_PER_TILE, h, 2, 1, p) * eye[None, None, :, None, None, :, None]
    wb = wb.reshape(CHUNK, nt, LANES, 2 * GROUPS_PER_TILE * p)
    car = c_re[None] * pr[1:, :, None, :] - c_im[None] * pi[1:, :, None, :]
    cai = c_re[None] * pi[1:, :, None, :] + c_im[None] * pr[1:, :, None, :]
    vi = jnp.stack([car, -cai], axis=2)
    vi = jnp.transpose(vi, (0, 2, 1, 4, 3))
    vb = vi.reshape(CHUNK, 2, nt, GROUPS_PER_TILE, p, 1, h) * eye[None, None, None, :, None, :, None]
    vb = jnp.transpose(vb, (0, 2, 1, 3, 4, 5, 6)).reshape(CHUNK, nt, 2 * GROUPS_PER_TILE * p, LANES)

    kz = jnp.concatenate([jnp.zeros_like(kb[:1]), kb], axis=0)
    top = jnp.concatenate([kz[1::2], kz[2::2]], axis=-1)
    bot = jnp.concatenate([kz[0:-1:2], kz[1::2]], axis=-1)
    k2 = jnp.transpose(jnp.concatenate([top, bot], axis=-2), (1, 0, 2, 3))
    w2 = jnp.transpose(jnp.concatenate([wb[0::2], wb[1::2]], axis=-2), (1, 0, 2, 3))
    v2 = jnp.transpose(jnp.concatenate([vb[0::2], vb[1::2]], axis=-1), (1, 0, 2, 3))
    al = jnp.stack([pr[CHUNK].reshape(nt, -1), pi[CHUNK].reshape(nt, -1)], axis=1)
    return k2, w2, v2, al


def _in_proj(x, g, w4, tm):
    s, d = x.shape
    nj, _, cw = w4.shape

    def body(x_ref, g_ref, w_ref, z_ref, h_ref):
        @pl.when(pl.program_id(1) == 0)
        def _():
            xv = x_ref[...]
            h_ref[...] = (xv * _rms(xv) * g_ref[...]).astype(BF16)

        z_ref[...] = _dot(h_ref[...], w_ref[...])

    return _call(
        body,
        grid=(s // tm, nj),
        in_specs=[
            pl.BlockSpec((tm, d), lambda i, j: (i, 0)),
            pl.BlockSpec((1, d), lambda i, j: (0, 0)),
            pl.BlockSpec((None, d, cw), lambda i, j: (j, 0, 0)),
        ],
        out_specs=[pl.BlockSpec((tm, cw), lambda i, j: (i, j)), pl.BlockSpec((tm, d), lambda i, j: (i, 0))],
        out_shape=[jax.ShapeDtypeStruct((s, nj * cw), F32), jax.ShapeDtypeStruct((s, d), BF16)],
        compiler_params=_params(2),
        name="in_proj",
    )(x, g, w4)


def _ssm_fwd(z, k2, w2, v2, al):
    s = z.shape[0]
    nt = k2.shape[0]
    nc = s // CHUNK
    ns = w2.shape[-1]
    hs = ns // 2

    def body(u_ref, k_ref, w_ref, v_ref, al_ref, y_ref, xp_ref, us_ref, xloc):
        for q in range(PAIRS):
            us_ref[q, :, 0:LANES] = u_ref[pl.ds(2 * q, nc, stride=CHUNK), :].astype(BF16)
            us_ref[q, :, LANES : 2 * LANES] = u_ref[pl.ds(2 * q + 1, nc, stride=CHUNK), :].astype(BF16)
        acc = _dot(us_ref[0], w_ref[0])
        for q in range(1, PAIRS):
            acc = acc + _dot(us_ref[q], w_ref[q])
        xloc[...] = acc
        ar = al_ref[0:1, :]
        ai = al_ref[1:2, :]

        def step(c, carry):
            xr, xi = carry
            xp_ref[pl.ds(c, 1), 0:hs] = xr
            xp_ref[pl.ds(c, 1), hs:ns] = xi
            lr = xloc[pl.ds(c, 1), 0:hs]
            li = xloc[pl.ds(c, 1), hs:ns]
            return ar * xr - ai * xi + lr, ar * xi + ai * xr + li

        zero = jnp.zeros((1, hs), F32)
        lax.fori_loop(0, nc, step, (zero, zero))
        xpb = xp_ref[...].astype(BF16)
        for r in range(PAIRS):
            acc = _dot(xpb, v_ref[r])
            for q in range(r + 1):
                acc = acc + _dot(us_ref[q], k_ref[r - q])
            y_ref[pl.ds(2 * r, nc, stride=CHUNK), :] = acc[:, 0:LANES]
            y_ref[pl.ds(2 * r + 1, nc, stride=CHUNK), :] = acc[:, LANES : 2 * LANES]

    return _call(
        body,
        grid=(nt,),
        in_specs=[
            pl.BlockSpec((s, LANES), lambda t: (0, t)),
            pl.BlockSpec((None, PAIRS, 2 * LANES, 2 * LANES), lambda t: (t, 0, 0, 0)),
            pl.BlockSpec((None, PAIRS, 2 * LANES, ns), lambda t: (t, 0, 0, 0)),
            pl.BlockSpec((None, PAIRS, ns, 2 * LANES), lambda t: (t, 0, 0, 0)),
            pl.BlockSpec((None, 2, hs), lambda t: (t, 0, 0)),
        ],
        out_specs=[
            pl.BlockSpec((s, LANES), lambda t: (0, t)),
            pl.BlockSpec((None, nc, ns), lambda t: (t, 0, 0)),
            pl.BlockSpec((None, PAIRS, nc, 2 * LANES), lambda t: (t, 0, 0, 0)),
        ],
        out_shape=[
            jax.ShapeDtypeStruct((s, nt * LANES), F32),
            jax.ShapeDtypeStruct((nt, nc, ns), F32),
            jax.ShapeDtypeStruct((nt, PAIRS, nc, 2 * LANES), BF16),
        ],
        scratch_shapes=[pltpu.VMEM((nc, ns), F32)],
        compiler_params=_params(1),
        name="ssm_fwd",
    )(z, k2, w2, v2, al)


def _glu_fwd(y_pre, wg, bg, go, d_model, tm):
    s, w = y_pre.shape

    def body(y_ref, wg_ref, bg_ref, go_ref, o_ref):
        yg = _gelu(y_ref[...])
        gate = _sigmoid(_dot(yg.astype(BF16), wg_ref[...]) + bg_ref[...])
        ys = yg * gate
        o_ref[...] = (ys * _rms(ys) * go_ref[...]).astype(BF16)

    return _call(
        body,
        grid=(s // tm,),
        in_specs=[
            pl.BlockSpec((tm, w), lambda i: (i, 0)),
            pl.BlockSpec((w, w), lambda i: (0, 0)),
            pl.BlockSpec((1, w), lambda i: (0, 0)),
            pl.BlockSpec((1, w), lambda i: (0, 0)),
        ],
        out_specs=pl.BlockSpec((tm, w), lambda i: (i, 0)),
        out_shape=jax.ShapeDtypeStruct((s, d_model), BF16),
        compiler_params=_params(1),
        name="glu_fwd",
    )(y_pre, wg, bg, go)


def _sgu_parts(zu, zv, lng, lnb, wt_ref, bias_ref):
    u = _gelu(zu)
    v = _gelu(zv)
    mu = jnp.mean(v, axis=-1, keepdims=True)
    vc = v - mu
    rstd = lax.rsqrt(jnp.mean(vc * vc, axis=-1, keepdims=True) + EPS)
    vhat = vc * rstd
    vb = (vhat * lng + lnb).astype(BF16)
    heads = wt_ref.shape[0]
    mix = jnp.concatenate(
        [_dot(wt_ref[h], vb[:, h * LANES : (h + 1) * LANES]) + bias_ref[h] for h in range(heads)], axis=1
    )
    return u, vhat, rstd, vb, mix


def _sgu_fwd(z, mixed, lng, lnb, wt, biasb, go, rb):
    s = z.shape[0]
    w = lng.shape[-1]
    heads = wt.shape[0]

    def body(zu_ref, zv_ref, m_any, lng_ref, lnb_ref, wt_ref, b_ref, go_ref, o_ref):
        del m_any
        for ck in range(rb // SGU_CHUNK):
            rows = slice(ck * SGU_CHUNK, (ck + 1) * SGU_CHUNK)
            u, _, _, _, mix = _sgu_parts(zu_ref[rows, :], zv_ref[rows, :], lng_ref[...], lnb_ref[...], wt_ref, b_ref)
            y = u * mix
            o_ref[rows, :] = (y * _rms(y) * go_ref[...]).astype(BF16)

    return _call(
        body,
        grid=(s // rb,),
        in_specs=[
            pl.BlockSpec((rb, w), lambda i: (i, 1)),
            pl.BlockSpec((rb, w), lambda i: (i, 2)),
            pl.BlockSpec(memory_space=pl.ANY),
            pl.BlockSpec((1, w), lambda i: (0, 0)),
            pl.BlockSpec((1, w), lambda i: (0, 0)),
            pl.BlockSpec((heads, SGU_CHUNK, SGU_CHUNK), lambda i: (0, 0, 0)),
            pl.BlockSpec((heads, SGU_CHUNK, LANES), lambda i: (0, 0, 0)),
            pl.BlockSpec((1, w), lambda i: (0, 0)),
        ],
        out_specs=pl.BlockSpec((rb, w), lambda i: (i, 1)),
        out_shape=jax.ShapeDtypeStruct(mixed.shape, mixed.dtype),
        input_output_aliases={2: 0},
        compiler_params=_params(1),
        name="sgu_fwd",
    )(z, z, mixed, lng, lnb, wt, biasb, go)


def _out_proj(x, mixed, w_out, tm):
    s, d = x.shape

    def body(x_ref, m_ref, w_ref, o_ref):
        o_ref[...] = x_ref[...] + _dot(m_ref[...], w_ref[...])

    return _call(
        body,
        grid=(s // tm,),
        in_specs=[
            pl.BlockSpec((tm, d), lambda i: (i, 0)),
            pl.BlockSpec((tm, d), lambda i: (i, 0)),
            pl.BlockSpec((d, d), lambda i: (0, 0)),
        ],
        out_specs=pl.BlockSpec((tm, d), lambda i: (i, 0)),
        out_shape=jax.ShapeDtypeStruct((s, d), F32),
        compiler_params=_params(1),
        name="out_proj",
    )(x, mixed, w_out)


def _mlp_fwd(x1, g, w_up4, w_down, tm, tf):
    s, d = x1.shape
    ff = w_down.shape[0]
    cw = w_up4.shape[-1]
    per = cw // tf
    nf = ff // tf

    def body(x_ref, g_ref, wu_ref, wd_ref, x2_ref, up_ref, h_ref):
        @pl.when(pl.program_id(1) == 0)
        def _():
            xv = x_ref[...]
            h_ref[...] = (xv * _rms(xv) * g_ref[...]).astype(BF16)
            x2_ref[...] = xv

        upb = _dot(h_ref[...], wu_ref[...]).astype(BF16)
        up_ref[...] = upb
        a = jnp.maximum(upb.astype(F32), 0.0)
        x2_ref[...] += _dot((a * a).astype(BF16), wd_ref[...])

    return _call(
        body,
        grid=(s // tm, nf),
        in_specs=[
            pl.BlockSpec((tm, d), lambda i, f: (i, 0)),
            pl.BlockSpec((1, d), lambda i, f: (0, 0)),
            pl.BlockSpec((None, d, tf), lambda i, f: (f // per, 0, f % per)),
            pl.BlockSpec((tf, d), lambda i, f: (f, 0)),
        ],
        out_specs=[
            pl.BlockSpec((tm, d), lambda i, f: (i, 0)),
            pl.BlockSpec((tm, tf), lambda i, f: (i, f)),
            pl.BlockSpec((tm, d), lambda i, f: (i, 0)),
        ],
        out_shape=[
            jax.ShapeDtypeStruct((s, d), F32),
            jax.ShapeDtypeStruct((s, ff), BF16),
            jax.ShapeDtypeStruct((s, d), BF16),
        ],
        compiler_params=_params(2),
        name="mlp_fwd",
    )(x1, g, w_up4, w_down)


def _final(x2, target, g, tm):
    s, d = x2.shape

    def body(x_ref, t_ref, g_ref, loss_ref, dx_ref, dxb_ref, dg_ref):
        @pl.when(pl.program_id(0) == 0)
        def _():
            loss_ref[...] = jnp.zeros_like(loss_ref)
            dg_ref[...] = jnp.zeros_like(dg_ref)

        xv = x_ref[...]
        r = _rms(xv)
        gv = g_ref[...]
        diff = xv * r * gv - t_ref[...]
        loss_ref[...] += 0.5 * jnp.sum(jnp.mean(diff * diff, axis=-1, keepdims=True), axis=0, keepdims=True)
        dx, dgt = _rms_bwd(diff * (1.0 / d), xv, r, gv)
        dx_ref[...] = dx
        dxb_ref[...] = dx.astype(BF16)
        dg_ref[...] += jnp.sum(dgt, axis=0, keepdims=True)

    return _call(
        body,
        grid=(s // tm,),
        in_specs=[
            pl.BlockSpec((tm, d), lambda i: (i, 0)),
            pl.BlockSpec((tm, d), lambda i: (i, 0)),
            pl.BlockSpec((1, d), lambda i: (0, 0)),
        ],
        out_specs=[
            pl.BlockSpec((1, 1), lambda i: (0, 0)),
            pl.BlockSpec((tm, d), lambda i: (i, 0)),
            pl.BlockSpec((tm, d), lambda i: (i, 0)),
            pl.BlockSpec((1, d), lambda i: (0, 0)),
        ],
        out_shape=[
            jax.ShapeDtypeStruct((1, 1), F32),
            jax.ShapeDtypeStruct((s, d), F32),
            jax.ShapeDtypeStruct((s, d), BF16),
            jax.ShapeDtypeStruct((1, d), F32),
        ],
        compiler_params=_params(1),
        name="final_loss",
    )(x2, target, g)


def _mlp_bwd(dx2b, up, w_up4, w_down, tm, tf):
    s, d = dx2b.shape
    ff = w_down.shape[0]
    cw = w_up4.shape[-1]
    per = cw // tf

    def body(dx_ref, up_ref, wu_ref, wd_ref, dup_ref, dh_ref):
        @pl.when(pl.program_id(1) == 0)
        def _():
            dh_ref[...] = jnp.zeros_like(dh_ref)

        dact = _dot_nt(dx_ref[...], wd_ref[...])
        dupb = (dact * (2.0 * jnp.maximum(up_ref[...].astype(F32), 0.0))).astype(BF16)
        dup_ref[...] = dupb
        dh_ref[...] += _dot_nt(dupb, wu_ref[...])

    return _call(
        body,
        grid=(s // tm, ff // tf),
        in_specs=[
            pl.BlockSpec((tm, d), lambda i, f: (i, 0)),
            pl.BlockSpec((tm, tf), lambda i, f: (i, f)),
            pl.BlockSpec((None, d, tf), lambda i, f: (f // per, 0, f % per)),
            pl.BlockSpec((tf, d), lambda i, f: (f, 0)),
        ],
        out_specs=[pl.BlockSpec((tm, tf), lambda i, f: (i, f)), pl.BlockSpec((tm, d), lambda i, f: (i, 0))],
        out_shape=[jax.ShapeDtypeStruct((s, ff), BF16), jax.ShapeDtypeStruct((s, d), F32)],
        compiler_params=_params(2),
        name="mlp_bwd",
    )(dx2b, up, w_up4, w_down)


def _norm_bwd(dh, x, dres, g, tm, name):
    s, d = x.shape

    def body(dh_ref, x_ref, dr_ref, g_ref, dx_ref, dxb_ref, dg_ref):
        @pl.when(pl.program_id(0) == 0)
        def _():
            dg_ref[...] = jnp.zeros_like(dg_ref)

        xv = x_ref[...]
        dx, dgt = _rms_bwd(dh_ref[...], xv, _rms(xv), g_ref[...])
        tot = dr_ref[...] + dx
        dx_ref[...] = tot
        dxb_ref[...] = tot.astype(BF16)
        dg_ref[...] += jnp.sum(dgt, axis=0, keepdims=True)

    row = pl.BlockSpec((tm, d), lambda i: (i, 0))
    vec = pl.BlockSpec((1, d), lambda i: (0, 0))
    return _call(
        body,
        grid=(s // tm,),
        in_specs=[row, row, row, vec],
        out_specs=[row, row, vec],
        out_shape=[
            jax.ShapeDtypeStruct((s, d), F32),
            jax.ShapeDtypeStruct((s, d), BF16),
            jax.ShapeDtypeStruct((1, d), F32),
        ],
        compiler_params=_params(1),
        name=name,
    )(dh, x, dres, g)


def _grad_w(a, b, out_dims, index_map, tm, tn, tk, relu2, name):
    t, m = a.shape
    n = b.shape[1]
    nk = t // tk

    def body(a_ref, b_ref, o_ref, acc):
        k = pl.program_id(2)

        @pl.when(k == 0)
        def _():
            acc[...] = jnp.zeros_like(acc)

        av = a_ref[...]
        if relu2:
            r = jnp.maximum(av.astype(F32), 0.0)
            av = (r * r).astype(BF16)
        acc[...] += _dot_tn(av, b_ref[...])

        @pl.when(k == nk - 1)
        def _():
            o_ref[...] = acc[...].astype(BF16)

    return _call(
        body,
        grid=(m // tm, n // tn, nk),
        in_specs=[pl.BlockSpec((tk, tm), lambda i, j, k: (k, i)), pl.BlockSpec((tk, tn), lambda i, j, k: (k, j))],
        out_specs=pl.BlockSpec((None, None, tm, tn), lambda i, j, k: index_map(i, j)),
        out_shape=jax.ShapeDtypeStruct(out_dims, BF16),
        scratch_shapes=[pltpu.VMEM((tm, tn), F32)],
        compiler_params=_params(3),
        name=name,
    )(a, b)


def _out_proj_bwd(dx1b, w_out, tm):
    s, d = dx1b.shape

    def body(dx_ref, w_ref, o_ref):
        o_ref[...] = _dot_nt(dx_ref[...], w_ref[...])

    return _call(
        body,
        grid=(s // tm,),
        in_specs=[pl.BlockSpec((tm, d), lambda i: (i, 0)), pl.BlockSpec((d, d), lambda i: (0, 0))],
        out_specs=pl.BlockSpec((tm, d), lambda i: (i, 0)),
        out_shape=jax.ShapeDtypeStruct((s, d), F32),
        compiler_params=_params(1),
        name="out_proj_bwd",
    )(dx1b, w_out)


def _glu_bwd(y_pre, dmix, wg, bg, go, tm):
    s, w = y_pre.shape
    n = s // tm

    def body(y_ref, dm_ref, wg_ref, bg_ref, go_ref, dy_ref, dwg_ref, dbg_ref, dgo_ref, acc):
        i = pl.program_id(0)

        @pl.when(i == 0)
        def _():
            acc[...] = jnp.zeros_like(acc)
            dbg_ref[...] = jnp.zeros_like(dbg_ref)
            dgo_ref[...] = jnp.zeros_like(dgo_ref)

        yp = y_ref[...]
        yg = _gelu(yp)
        ygb = yg.astype(BF16)
        gate = _sigmoid(_dot(ygb, wg_ref[...]) + bg_ref[...])
        ys = yg * gate
        dys, dgt = _rms_bwd(dm_ref[...], ys, _rms(ys), go_ref[...])
        dgo_ref[...] += jnp.sum(dgt, axis=0, keepdims=True)
        dpre = dys * yg * gate * (1.0 - gate)
        dbg_ref[...] += jnp.sum(dpre, axis=0, keepdims=True)
        dpb = dpre.astype(BF16)
        dyg = dys * gate + _dot_nt(dpb, wg_ref[...])
        dy_ref[...] = dyg * _gelu_grad(yp)
        acc[...] += _dot_tn(ygb, dpb)

        @pl.when(i == n - 1)
        def _():
            dwg_ref[...] = acc[...].astype(BF16)

    return _call(
        body,
        grid=(n,),
        in_specs=[
            pl.BlockSpec((tm, w), lambda i: (i, 0)),
            pl.BlockSpec((tm, w), lambda i: (i, 0)),
            pl.BlockSpec((w, w), lambda i: (0, 0)),
            pl.BlockSpec((1, w), lambda i: (0, 0)),
            pl.BlockSpec((1, w), lambda i: (0, 0)),
        ],
        out_specs=[
            pl.BlockSpec((tm, w), lambda i: (i, 0)),
            pl.BlockSpec((w, w), lambda i: (0, 0)),
            pl.BlockSpec((1, w), lambda i: (0, 0)),
            pl.BlockSpec((1, w), lambda i: (0, 0)),
        ],
        out_shape=[
            jax.ShapeDtypeStruct((s, w), F32),
            jax.ShapeDtypeStruct((w, w), BF16),
            jax.ShapeDtypeStruct((1, w), F32),
            jax.ShapeDtypeStruct((1, w), F32),
        ],
        scratch_shapes=[pltpu.VMEM((w, w), F32)],
        compiler_params=_params(1),
        name="glu_bwd",
    )(y_pre, dmix, wg, bg, go)


def _ssm_bwd_state(dy, v2, al, xprev):
    s = dy.shape[0]
    nt, nc, ns = xprev.shape
    hs = ns // 2

    def body(dy_ref, v_ref, al_ref, xp_ref, dys_ref, g_ref, dal_ref, dxp, gs):
        for q in range(PAIRS):
            dys_ref[q, :, 0:LANES] = dy_ref[pl.ds(2 * q, nc, stride=CHUNK), :].astype(BF16)
            dys_ref[q, :, LANES : 2 * LANES] = dy_ref[pl.ds(2 * q + 1, nc, stride=CHUNK), :].astype(BF16)
        acc = _dot_nt(dys_ref[0], v_ref[0])
        for r in range(1, PAIRS):
            acc = acc + _dot_nt(dys_ref[r], v_ref[r])
        dxp[...] = acc
        ar = al_ref[0:1, :]
        ai = al_ref[1:2, :]
        zero = jnp.zeros((1, hs), F32)
        gs[pl.ds(nc - 1, 1), 0:hs] = zero
        gs[pl.ds(nc - 1, 1), hs:ns] = zero

        def step(n, carry):
            gr, gi = carry
            c = nc - 2 - n
            nr = dxp[pl.ds(c + 1, 1), 0:hs] + ar * gr + ai * gi
            ni = dxp[pl.ds(c + 1, 1), hs:ns] + ar * gi - ai * gr
            gs[pl.ds(c, 1), 0:hs] = nr
            gs[pl.ds(c, 1), hs:ns] = ni
            return nr, ni

        lax.fori_loop(0, nc - 1, step, (zero, zero))
        gv = gs[...]
        xv = xp_ref[...]
        gr, gi = gv[:, 0:hs], gv[:, hs:ns]
        xr, xi = xv[:, 0:hs], xv[:, hs:ns]
        dal_ref[0:1, :] = jnp.sum(gr * xr + gi * xi, axis=0, keepdims=True)
        dal_ref[1:2, :] = jnp.sum(gi * xr - gr * xi, axis=0, keepdims=True)
        g_ref[...] = gv.astype(BF16)

    return _call(
        body,
        grid=(nt,),
        in_specs=[
            pl.BlockSpec((s, LANES), lambda t: (0, t)),
            pl.BlockSpec((None, PAIRS, ns, 2 * LANES), lambda t: (t, 0, 0, 0)),
            pl.BlockSpec((None, 2, hs), lambda t: (t, 0, 0)),
            pl.BlockSpec((None, nc, ns), lambda t: (t, 0, 0)),
        ],
        out_specs=[
            pl.BlockSpec((None, PAIRS, nc, 2 * LANES), lambda t: (t, 0, 0, 0)),
            pl.BlockSpec((None, nc, ns), lambda t: (t, 0, 0)),
            pl.BlockSpec((None, 2, hs), lambda t: (t, 0, 0)),
        ],
        out_shape=[
            jax.ShapeDtypeStruct((nt, PAIRS, nc, 2 * LANES), BF16),
            jax.ShapeDtypeStruct((nt, nc, ns), BF16),
            jax.ShapeDtypeStruct((nt, 2, hs), F32),
        ],
        scratch_shapes=[pltpu.VMEM((nc, ns), F32), pltpu.VMEM((nc, ns), F32)],
        compiler_params=_params(1),
        name="ssm_bwd_state",
    )(dy, v2, al, xprev)


def _ssm_bwd_main(us, dys, k2, w2, xprev, gst, s, width):
    nt, _, nc, _ = us.shape
    ns = xprev.shape[-1]

    def body(us_ref, dys_ref, k_ref, w_ref, xp_ref, g_ref, du_ref, dk_ref, dw_ref, dv_ref, acc, du2, duf):
        q = pl.program_id(1)

        @pl.when(q == 0)
        def _():
            dk_ref[...] = jnp.zeros_like(dk_ref)

        usq = us_ref[q]
        gb = g_ref[...]
        dw_ref[...] = _dot_tn(usq, gb)
        dv_ref[...] = _dot_tn(xp_ref[...].astype(BF16), dys_ref[q])
        acc[...] = _dot_nt(gb, w_ref[...])
        for m in range(PAIRS):

            @pl.when(q + m < PAIRS)
            def _():
                dyr = dys_ref[jnp.minimum(q + m, PAIRS - 1)]
                acc[...] += _dot_nt(dyr, k_ref[m])
                dk_ref[m] += _dot_tn(usq, dyr)

        du2[q] = acc[...]

        @pl.when(q == PAIRS - 1)
        def _():
            for p in range(PAIRS):
                duf[pl.ds(2 * p, nc, stride=CHUNK), :] = du2[p, :, 0:LANES]
                duf[pl.ds(2 * p + 1, nc, stride=CHUNK), :] = du2[p, :, LANES : 2 * LANES]
            du_ref[...] = duf[...].astype(BF16)

    return _call(
        body,
        grid=(nt, PAIRS),
        in_specs=[
            pl.BlockSpec((None, PAIRS, nc, 2 * LANES), lambda t, q: (t, 0, 0, 0)),
            pl.BlockSpec((None, PAIRS, nc, 2 * LANES), lambda t, q: (t, 0, 0, 0)),
            pl.BlockSpec((None, PAIRS, 2 * LANES, 2 * LANES), lambda t, q: (t, 0, 0, 0)),
            pl.BlockSpec((None, None, 2 * LANES, ns), lambda t, q: (t, q, 0, 0)),
            pl.BlockSpec((None, nc, ns), lambda t, q: (t, 0, 0)),
            pl.BlockSpec((None, nc, ns), lambda t, q: (t, 0, 0)),
        ],
        out_specs=[
            pl.BlockSpec((s, LANES), lambda t, q: (0, t)),
            pl.BlockSpec((None, PAIRS, 2 * LANES, 2 * LANES), lambda t, q: (t, 0, 0, 0)),
            pl.BlockSpec((None, None, 2 * LANES, ns), lambda t, q: (t, q, 0, 0)),
            pl.BlockSpec((None, None, ns, 2 * LANES), lambda t, q: (t, q, 0, 0)),
        ],
        out_shape=[
            jax.ShapeDtypeStruct((s, width), BF16),
            jax.ShapeDtypeStruct((nt, PAIRS, 2 * LANES, 2 * LANES), F32),
            jax.ShapeDtypeStruct((nt, PAIRS, 2 * LANES, ns), F32),
            jax.ShapeDtypeStruct((nt, PAIRS, ns, 2 * LANES), F32),
        ],
        scratch_shapes=[
            pltpu.VMEM((nc, 2 * LANES), F32),
            pltpu.VMEM((PAIRS, nc, 2 * LANES), F32),
            pltpu.VMEM((s, LANES), F32),
        ],
        compiler_params=_params(2),
        name="ssm_bwd_main",
    )(us, dys, k2, w2, xprev, gst)


def _sgu_bwd(z, dmix, dz, lng, lnb, wt, wtt, biasb, go, rb):
    s = z.shape[0]
    w = lng.shape[-1]
    heads = wt.shape[0]
    ncol = (LANES, LANES)

    def body(zu_ref, zv_ref, dm_ref, dz_any, lng_ref, lnb_ref, wt_ref, wtt_ref, b_ref, go_ref,
             dz_ref, dw_ref, db_ref, dlg_ref, dlb_ref, dgo_ref, dzv):
        del dz_any
        i = pl.program_id(0)
        p = pl.program_id(1)

        @pl.when(jnp.logical_and(i == 0, p == 0))
        def _():
            dw_ref[...] = jnp.zeros_like(dw_ref)
            db_ref[...] = jnp.zeros_like(db_ref)
            dlg_ref[...] = jnp.zeros_like(dlg_ref)
            dlb_ref[...] = jnp.zeros_like(dlb_ref)
            dgo_ref[...] = jnp.zeros_like(dgo_ref)

        @pl.when(p == 0)
        def _():
            lng_v = lng_ref[...]
            for ck in range(rb // SGU_CHUNK):
                rows = slice(ck * SGU_CHUNK, (ck + 1) * SGU_CHUNK)
                zu = zu_ref[rows, :]
                zv = zv_ref[rows, :]
                u, vhat, rstd, vb, mix = _sgu_parts(zu, zv, lng_v, lnb_ref[...], wt_ref, b_ref)
                y = u * mix
                dy, dgt = _rms_bwd(dm_ref[rows, :], y, _rms(y), go_ref[...])
                dgo_ref[...] += jnp.sum(dgt, axis=0, keepdims=True)
                du = dy * mix
                dmx = dy * u
                dmb = dmx.astype(BF16)
                dvl = []
                for h in range(heads):
                    cols = slice(h * LANES, (h + 1) * LANES)
                    db_ref[h] += jnp.broadcast_to(jnp.sum(dmx[:, cols], axis=-1, keepdims=True), ncol)
                    dw_ref[h] += _dot_nt(dmb[:, cols], vb[:, cols])
                    dvl.append(_dot(wtt_ref[h], dmb[:, cols]))
                dvln = jnp.concatenate(dvl, axis=1)
                dlg_ref[...] += jnp.sum(dvln * vhat, axis=0, keepdims=True)
                dlb_ref[...] += jnp.sum(dvln, axis=0, keepdims=True)
                dvh = dvln * lng_v
                dv = rstd * (
                    dvh
                    - jnp.mean(dvh, axis=-1, keepdims=True)
                    - vhat * jnp.mean(dvh * vhat, axis=-1, keepdims=True)
                )
                dz_ref[rows, :] = (du * _gelu_grad(zu)).astype(BF16)
                dzv[rows, :] = (dv * _gelu_grad(zv)).astype(BF16)

        @pl.when(p == 1)
        def _():
            dz_ref[...] = dzv[...]

    return _call(
        body,
        grid=(s // rb, 2),
        in_specs=[
            pl.BlockSpec((rb, w), lambda i, p: (i, 1)),
            pl.BlockSpec((rb, w), lambda i, p: (i, 2)),
            pl.BlockSpec((rb, w), lambda i, p: (i, 1)),
            pl.BlockSpec(memory_space=pl.ANY),
            pl.BlockSpec((1, w), lambda i, p: (0, 0)),
            pl.BlockSpec((1, w), lambda i, p: (0, 0)),
            pl.BlockSpec((heads, SGU_CHUNK, SGU_CHUNK), lambda i, p: (0, 0, 0)),
            pl.BlockSpec((heads, SGU_CHUNK, SGU_CHUNK), lambda i, p: (0, 0, 0)),
            pl.BlockSpec((heads, SGU_CHUNK, LANES), lambda i, p: (0, 0, 0)),
            pl.BlockSpec((1, w), lambda i, p: (0, 0)),
        ],
        out_specs=[
            pl.BlockSpec((rb, w), lambda i, p: (i, 1 + p)),
            pl.BlockSpec((heads, SGU_CHUNK, SGU_CHUNK), lambda i, p: (0, 0, 0)),
            pl.BlockSpec((heads, SGU_CHUNK, LANES), lambda i, p: (0, 0, 0)),
            pl.BlockSpec((1, w), lambda i, p: (0, 0)),
            pl.BlockSpec((1, w), lambda i, p: (0, 0)),
            pl.BlockSpec((1, w), lambda i, p: (0, 0)),
        ],
        out_shape=[
            jax.ShapeDtypeStruct(dz.shape, dz.dtype),
            jax.ShapeDtypeStruct((heads, SGU_CHUNK, SGU_CHUNK), F32),
            jax.ShapeDtypeStruct((heads, SGU_CHUNK, LANES), F32),
            jax.ShapeDtypeStruct((1, w), F32),
            jax.ShapeDtypeStruct((1, w), F32),
            jax.ShapeDtypeStruct((1, w), F32),
        ],
        scratch_shapes=[pltpu.VMEM((rb, w), BF16)],
        input_output_aliases={3: 0},
        compiler_params=_params(2),
        name="sgu_bwd",
    )(z, z, dmix, dz, lng, lnb, wt, wtt, biasb, go)


def _in_proj_bwd(dz, w4, tm):
    s = dz.shape[0]
    nj, d, cw = w4.shape

    def body(dz_ref, w_ref, dh_ref):
        @pl.when(pl.program_id(1) == 0)
        def _():
            dh_ref[...] = jnp.zeros_like(dh_ref)

        dh_ref[...] += _dot_nt(dz_ref[...], w_ref[...])

    return _call(
        body,
        grid=(s // tm, nj),
        in_specs=[pl.BlockSpec((tm, cw), lambda i, j: (i, j)), pl.BlockSpec((None, d, cw), lambda i, j: (j, 0, 0))],
        out_specs=pl.BlockSpec((tm, d), lambda i, j: (i, 0)),
        out_shape=jax.ShapeDtypeStruct((s, d), F32),
        compiler_params=_params(2),
        name="in_proj_bwd",
    )(dz, w4)


_ANY = pl.BlockSpec(memory_space=pl.ANY)


def _position():
    x, y, c = lax.axis_index("x"), lax.axis_index("y"), lax.axis_index("c")
    return x, y, c, [(1 - x, y), (x, 1 - y), (1 - x, 1 - y)]


def _remote(src, dst, send_sems, recv_sems, k, to):
    return pltpu.make_async_remote_copy(
        src_ref=src, dst_ref=dst, send_sem=send_sems.at[k], recv_sem=recv_sems.at[k], device_id=to, device_id_type=MESH
    )


def _gather_chips(arrs, name):
    n = len(arrs)

    def body(*refs):
        ins, outs = refs[:n], refs[n : 2 * n]
        send_sems, recv_sems, local_sems = refs[2 * n :]
        x, y, c, chips = _position()
        me = 2 * x + y
        local = [pltpu.make_async_copy(ins[i], outs[i].at[me], local_sems.at[i]) for i in range(n)]
        for cp in local:
            cp.start()
        sends = []
        for k, (px, py) in enumerate(chips):
            for i in range(n):
                sends.append(_remote(ins[i], outs[i].at[me], send_sems, recv_sems, k * n + i, (px, py, c)))
                sends[-1].start()
        for k, (px, py) in enumerate(chips):
            for i in range(n):
                _remote(ins[i], outs[i].at[2 * px + py], send_sems, recv_sems, k * n + i, (px, py, c)).wait_recv()
        for cp in sends:
            cp.wait_send()
        for cp in local:
            cp.wait()

    return _call(
        body,
        in_specs=[_ANY] * n,
        out_specs=[_ANY] * n,
        out_shape=[jax.ShapeDtypeStruct((N_CHIPS,) + a.shape, a.dtype) for a in arrs],
        scratch_shapes=[
            pltpu.SemaphoreType.DMA((3 * n,)),
            pltpu.SemaphoreType.DMA((3 * n,)),
            pltpu.SemaphoreType.DMA((n,)),
        ],
        name=name,
    )(*arrs)


def _pair_exchange(gs, name):
    n = len(gs)

    def body(*refs):
        ins, outs = refs[:n], refs[n : 2 * n]
        send_sems, recv_sems = refs[2 * n :]
        x, y, c, _ = _position()
        sib = (x, y, 1 - c)
        cps = []
        for i in range(n):
            for j in range(N_CHIPS):
                cps.append(_remote(ins[i].at[j, 1 - c], outs[i].at[j], send_sems, recv_sems, i * N_CHIPS + j, sib))
                cps[-1].start()
        for cp in cps:
            cp.wait_recv()
        for cp in cps:
            cp.wait_send()

    return _call(
        body,
        in_specs=[_ANY] * n,
        out_specs=[_ANY] * n,
        out_shape=[jax.ShapeDtypeStruct((g.shape[0],) + g.shape[2:], g.dtype) for g in gs],
        scratch_shapes=[pltpu.SemaphoreType.DMA((n * N_CHIPS,)), pltpu.SemaphoreType.DMA((n * N_CHIPS,))],
        name=name,
    )(*gs)


def _chip_exchange(ps, name):
    n = len(ps)

    def body(*refs):
        ins, outs = refs[:n], refs[n : 2 * n]
        send_sems, recv_sems, local_sems = refs[2 * n :]
        x, y, c, chips = _position()
        me = 2 * x + y
        local = [pltpu.make_async_copy(ins[i].at[me], outs[i].at[me], local_sems.at[i]) for i in range(n)]
        for cp in local:
            cp.start()
        sends = []
        for k, (px, py) in enumerate(chips):
            for i in range(n):
                sends.append(
                    _remote(ins[i].at[2 * px + py], outs[i].at[me], send_sems, recv_sems, k * n + i, (px, py, c))
                )
                sends[-1].start()
        for k, (px, py) in enumerate(chips):
            for i in range(n):
                _remote(ins[i].at[me], outs[i].at[2 * px + py], send_sems, recv_sems, k * n + i, (px, py, c)).wait_recv()
        for cp in sends:
            cp.wait_send()
        for cp in local:
            cp.wait()

    return _call(
        body,
        in_specs=[_ANY] * n,
        out_specs=[_ANY] * n,
        out_shape=[jax.ShapeDtypeStruct(p.shape, p.dtype) for p in ps],
        scratch_shapes=[
            pltpu.SemaphoreType.DMA((3 * n,)),
            pltpu.SemaphoreType.DMA((3 * n,)),
            pltpu.SemaphoreType.DMA((n,)),
        ],
        name=name,
    )(*ps)


def _pair_share(rs, name):
    n = len(rs)

    def body(*refs):
        ins, outs = refs[:n], refs[n : 2 * n]
        send_sems, recv_sems, local_sems = refs[2 * n :]
        x, y, c, _ = _position()
        sib = (x, y, 1 - c)
        local = [pltpu.make_async_copy(ins[i], outs[i].at[c], local_sems.at[i]) for i in range(n)]
        for cp in local:
            cp.start()
        sends = [_remote(ins[i], outs[i].at[c], send_sems, recv_sems, i, sib) for i in range(n)]
        for cp in sends:
            cp.start()
        for i in range(n):
            _remote(ins[i], outs[i].at[1 - c], send_sems, recv_sems, i, sib).wait_recv()
        for cp in sends:
            cp.wait_send()
        for cp in local:
            cp.wait()

    return _call(
        body,
        in_specs=[_ANY] * n,
        out_specs=[_ANY] * n,
        out_shape=[jax.ShapeDtypeStruct((2,) + r.shape, r.dtype) for r in rs],
        scratch_shapes=[
            pltpu.SemaphoreType.DMA((n,)),
            pltpu.SemaphoreType.DMA((n,)),
            pltpu.SemaphoreType.DMA((n,)),
        ],
        name=name,
    )(*rs)


def _pair_sum(g, r, core, name):
    nj, _, hr, cols = g.shape
    tr = _row_tile(hr, 256)

    def body(c_ref, g_ref, r_ref, o_ref):
        del c_ref
        o_ref[...] = (g_ref[...].astype(F32) + r_ref[...].astype(F32)).astype(o_ref.dtype)

    return _call(
        body,
        grid_spec=pltpu.PrefetchScalarGridSpec(
            num_scalar_prefetch=1,
            grid=(nj, hr // tr),
            in_specs=[
                pl.BlockSpec((None, None, tr, cols), lambda j, i, c: (j, c[0], i, 0)),
                pl.BlockSpec((None, tr, cols), lambda j, i, c: (j, i, 0)),
            ],
            out_specs=pl.BlockSpec((None, tr, cols), lambda j, i, c: (j, i, 0)),
        ),
        out_shape=jax.ShapeDtypeStruct((nj, hr, cols), g.dtype),
        compiler_params=_params(2),
        name=name,
    )(core, g, r)


def _chip_sum(q, name):
    nj, hr, cols = q.shape
    tr = _row_tile(hr, 256)

    def body(*refs):
        o_ref = refs[nj]
        tot = refs[0][...].astype(F32)
        for k in range(1, nj):
            tot = tot + refs[k][...].astype(F32)
        o_ref[...] = tot

    return _call(
        body,
        grid=(hr // tr,),
        in_specs=[pl.BlockSpec((None, tr, cols), functools.partial(lambda k, i: (k, i, 0), k)) for k in range(nj)],
        out_specs=pl.BlockSpec((tr, cols), lambda i: (i, 0)),
        out_shape=jax.ShapeDtypeStruct((hr, cols), F32),
        compiler_params=_params(1),
        name=name,
    )(*([q] * nj))


def _adamw(w, g, m, v, name):
    rows, cols = w.shape
    tr = _row_tile(rows, max(8, (2**18) // cols), mult=8)
    c1 = 1.0 - ADAM_B1**ADAM_STEP
    c2 = 1.0 - ADAM_B2**ADAM_STEP

    def body(w_ref, g_ref, m_ref, v_ref, d_ref, nm_ref, nv_ref):
        gv = g_ref[...]
        nm = ADAM_B1 * m_ref[...] + (1.0 - ADAM_B1) * gv
        nv = ADAM_B2 * v_ref[...] + (1.0 - ADAM_B2) * (gv * gv)
        nm_ref[...] = nm
        nv_ref[...] = nv
        d_ref[...] = -ADAM_LR * ((nm / c1) / (jnp.sqrt(nv / c2) + ADAM_EPS) + ADAM_WD * w_ref[...])

    spec = pl.BlockSpec((tr, cols), lambda i: (i, 0))
    sds = jax.ShapeDtypeStruct((rows, cols), F32)
    return _call(
        body,
        grid=(rows // tr,),
        in_specs=[spec] * 4,
        out_specs=[spec] * 3,
        out_shape=[sds] * 3,
        compiler_params=_params(1),
        name=name,
    )(w, g, m, v)


_TILE_ELEMS = 8 * LANES
_FLAT_ROW_MULT = 8 * 2 * N_CHIPS


def _flat_rows(shape):
    n = math.prod(shape)
    return (n + _TILE_ELEMS - 1) // _TILE_ELEMS * 8


def _pack(arrs):
    parts = []
    for a in arrs:
        rows = _flat_rows(a.shape)
        flat = a.reshape(-1).astype(F32)
        flat = jnp.pad(flat, (0, rows * LANES - flat.shape[0]))
        parts.append(flat.reshape(rows, LANES))
    total = sum(p.shape[0] for p in parts)
    pad = -total % _FLAT_ROW_MULT
    if pad:
        parts.append(jnp.zeros((pad, LANES), F32))
    return jnp.concatenate(parts, axis=0)


def _unpack(flat, shapes):
    out, row = [], 0
    for shp in shapes:
        rows = _flat_rows(shp)
        out.append(flat[row : row + rows].reshape(-1)[: math.prod(shp)].reshape(shp))
        row += rows
    return out


def kernel(x, norm_mix_g, w_in, ssm_a_re, ssm_a_im, ssm_b_re, ssm_b_im, ssm_c_re, ssm_c_im, ssm_d, ssm_log_dt, ssm_glu_w, ssm_glu_b, sgu_ln_g, sgu_ln_b, sgu_w, sgu_b, out_norm_ssm_g, out_norm_sgu_g, w_out, norm_mlp_g, w_up, w_down, norm_final_g, loss_target, m_norm_mix_g, m_w_in, m_ssm_a_re, m_ssm_a_im, m_ssm_b_re, m_ssm_b_im, m_ssm_c_re, m_ssm_c_im, m_ssm_d, m_ssm_log_dt, m_ssm_glu_w, m_ssm_glu_b, m_sgu_ln_g, m_sgu_ln_b, m_sgu_w, m_sgu_b, m_out_norm_ssm_g, m_out_norm_sgu_g, m_w_out, m_norm_mlp_g, m_w_up, m_w_down, m_norm_final_g, v_norm_mix_g, v_w_in, v_ssm_a_re, v_ssm_a_im, v_ssm_b_re, v_ssm_b_im, v_ssm_c_re, v_ssm_c_im, v_ssm_d, v_ssm_log_dt, v_ssm_glu_w, v_ssm_glu_b, v_sgu_ln_g, v_sgu_ln_b, v_sgu_w, v_sgu_b, v_out_norm_ssm_g, v_out_norm_sgu_g, v_w_out, v_norm_mlp_g, v_w_up, v_w_down, v_norm_final_g):
    weights = dict(norm_mix_g=norm_mix_g, w_in=w_in, ssm_a_re=ssm_a_re, ssm_a_im=ssm_a_im, ssm_b_re=ssm_b_re, ssm_b_im=ssm_b_im, ssm_c_re=ssm_c_re, ssm_c_im=ssm_c_im, ssm_d=ssm_d, ssm_log_dt=ssm_log_dt, ssm_glu_w=ssm_glu_w, ssm_glu_b=ssm_glu_b, sgu_ln_g=sgu_ln_g, sgu_ln_b=sgu_ln_b, sgu_w=sgu_w, sgu_b=sgu_b, out_norm_ssm_g=out_norm_ssm_g, out_norm_sgu_g=out_norm_sgu_g, w_out=w_out, norm_mlp_g=norm_mlp_g, w_up=w_up, w_down=w_down, norm_final_g=norm_final_g)
    mom_m = dict(norm_mix_g=m_norm_mix_g, w_in=m_w_in, ssm_a_re=m_ssm_a_re, ssm_a_im=m_ssm_a_im, ssm_b_re=m_ssm_b_re, ssm_b_im=m_ssm_b_im, ssm_c_re=m_ssm_c_re, ssm_c_im=m_ssm_c_im, ssm_d=m_ssm_d, ssm_log_dt=m_ssm_log_dt, ssm_glu_w=m_ssm_glu_w, ssm_glu_b=m_ssm_glu_b, sgu_ln_g=m_sgu_ln_g, sgu_ln_b=m_sgu_ln_b, sgu_w=m_sgu_w, sgu_b=m_sgu_b, out_norm_ssm_g=m_out_norm_ssm_g, out_norm_sgu_g=m_out_norm_sgu_g, w_out=m_w_out, norm_mlp_g=m_norm_mlp_g, w_up=m_w_up, w_down=m_w_down, norm_final_g=m_norm_final_g)
    mom_v = dict(norm_mix_g=v_norm_mix_g, w_in=v_w_in, ssm_a_re=v_ssm_a_re, ssm_a_im=v_ssm_a_im, ssm_b_re=v_ssm_b_re, ssm_b_im=v_ssm_b_im, ssm_c_re=v_ssm_c_re, ssm_c_im=v_ssm_c_im, ssm_d=v_ssm_d, ssm_log_dt=v_ssm_log_dt, ssm_glu_w=v_ssm_glu_w, ssm_glu_b=v_ssm_glu_b, sgu_ln_g=v_sgu_ln_g, sgu_ln_b=v_sgu_ln_b, sgu_w=v_sgu_w, sgu_b=v_sgu_b, out_norm_ssm_g=v_out_norm_ssm_g, out_norm_sgu_g=v_out_norm_sgu_g, w_out=v_w_out, norm_mlp_g=v_norm_mlp_g, w_up=v_w_up, w_down=v_w_down, norm_final_g=v_norm_final_g)
    names = list(weights)
    large = ["w_in", "ssm_glu_w", "w_out", "w_up", "w_down"]
    small = [n for n in names if n not in large]

    s, d = x.shape[1], x.shape[2]
    xs = x.reshape(s, d)
    target = loss_target.reshape(s, d)
    width = ssm_glu_w.shape[-1]
    ff = w_down.shape[1] * N_CHIPS
    tm = min(512, s)
    core = lax.axis_index("c").astype(jnp.int32).reshape(1)

    w_in4, glu4, out4, up4, down4 = _gather_chips(
        [w_in[0].astype(BF16), ssm_glu_w[0].astype(BF16), w_out[0].astype(BF16), w_up[0].astype(BF16),
         w_down[0].astype(BF16)], "gather_weights")
    wg_full = glu4.reshape(width, width)
    w_out_full = out4.reshape(d, d)
    w_down_full = down4.reshape(ff, d)

    ssm_args = (ssm_a_re[0], ssm_a_im[0], ssm_b_re[0], ssm_b_im[0], ssm_c_re[0], ssm_c_im[0], ssm_d[0], ssm_log_dt[0])
    (k2, w2, v2, al), ssm_vjp = jax.vjp(_ssm_mats, *ssm_args)
    k2b, w2b, v2b = k2.astype(BF16), w2.astype(BF16), v2.astype(BF16)
    causal = jnp.tril(jnp.ones((SGU_CHUNK, SGU_CHUNK), dtype=bool))
    wt = jnp.where(causal[None], sgu_w[0], 0.0)
    wtb = wt.astype(BF16)
    wttb = jnp.swapaxes(wt, 1, 2).astype(BF16)
    heads = sgu_w.shape[1]
    biasb = jnp.broadcast_to(sgu_b[0][:, :, None], (heads, SGU_CHUNK, LANES))

    z, h1b = _in_proj(xs, norm_mix_g, w_in4, tm)
    y_pre, xprev, us = _ssm_fwd(z, k2b, w2b, v2b, al)
    mixed = _glu_fwd(y_pre, wg_full, ssm_glu_b, out_norm_ssm_g, d, tm)
    mixed = _sgu_fwd(z, mixed, sgu_ln_g, sgu_ln_b, wtb, biasb, out_norm_sgu_g, tm)
    x1 = _out_proj(xs, mixed, w_out_full, tm)
    x2, up, h2b = _mlp_fwd(x1, norm_mlp_g, up4, w_down_full, tm, min(1024, up4.shape[-1]))
    loss_part, dx2, dx2b, d_norm_final = _final(x2, target, norm_final_g.reshape(1, d), tm)

    dup, dh2 = _mlp_bwd(dx2b, up, up4, w_down_full, tm, min(1024, up4.shape[-1]))
    dx1, dx1b, d_norm_mlp = _norm_bwd(dh2, x1, dx2, norm_mlp_g, min(256, s), "norm_mlp_bwd")
    hr_big = d // 2
    tkk = min(512, s)
    g_down = _grad_w(
        up, dx2b, (N_CHIPS, 2, hr_big, d),
        lambda i, j: (i // (2 * (hr_big // min(1024, hr_big))), (i // (hr_big // min(1024, hr_big))) % 2,
                      i % (hr_big // min(1024, hr_big)), j),
        min(1024, hr_big), d, tkk, True, "grad_w_down")
    tn_up = min(2048, up4.shape[-1])
    per_up = up4.shape[-1] // tn_up
    g_up = _grad_w(
        h2b, dup, (N_CHIPS, 2, hr_big, up4.shape[-1]),
        lambda i, j: (j // per_up, i // (hr_big // min(1024, hr_big)), i % (hr_big // min(1024, hr_big)), j % per_up),
        min(1024, hr_big), tn_up, tkk, False, "grad_w_up")
    dmix = _out_proj_bwd(dx1b, w_out_full, tm)
    hr_out = d // (2 * N_CHIPS)
    g_out = _grad_w(
        mixed, dx1b, (N_CHIPS, 2, hr_out, d),
        lambda i, j: (i // 2, i % 2, 0, j), hr_out, d, tkk, False, "grad_w_out")
    dy_pre, g_glu, d_glu_b, d_norm_ssm = _glu_bwd(y_pre, dmix, wg_full, ssm_glu_b, out_norm_ssm_g, tm)
    dys, gst, d_al = _ssm_bwd_state(dy_pre, v2b, al, xprev)
    dz, d_k2, d_w2, d_v2 = _ssm_bwd_main(us, dys, k2b, w2b, xprev, gst, s, z.shape[1])
    dz, d_wt, d_bias, d_ln_g, d_ln_b, d_norm_sgu = _sgu_bwd(
        z, dmix, dz, sgu_ln_g, sgu_ln_b, wtb, wttb, biasb, out_norm_sgu_g, tm)
    dh1 = _in_proj_bwd(dz, w_in4, tm)
    grad_x, _, d_norm_mix = _norm_bwd(dh1, xs, dx1, norm_mix_g, min(256, s), "norm_mix_bwd")
    cw_in = w_in4.shape[-1]
    g_in = _grad_w(
        h1b, dz, (N_CHIPS, 2, hr_big, cw_in),
        lambda i, j: (j, i // (hr_big // min(1024, hr_big)), i % (hr_big // min(1024, hr_big)), 0),
        min(1024, hr_big), cw_in, tkk, False, "grad_w_in")

    d_ssm = ssm_vjp((d_k2, d_w2, d_v2, d_al))
    small_grads = dict(
        norm_mix_g=d_norm_mix, ssm_a_re=d_ssm[0], ssm_a_im=d_ssm[1], ssm_b_re=d_ssm[2], ssm_b_im=d_ssm[3],
        ssm_c_re=d_ssm[4], ssm_c_im=d_ssm[5], ssm_d=d_ssm[6], ssm_log_dt=d_ssm[7], ssm_glu_b=d_glu_b,
        sgu_ln_g=d_ln_g, sgu_ln_b=d_ln_b, sgu_w=jnp.where(causal[None], d_wt, 0.0), sgu_b=d_bias[:, :, 0],
        out_norm_ssm_g=d_norm_ssm, out_norm_sgu_g=d_norm_sgu, norm_mlp_g=d_norm_mlp, norm_final_g=d_norm_final)
    flat = _pack([small_grads[n] for n in small])
    hr_small = flat.shape[0] // (2 * N_CHIPS)
    g_small = flat.reshape(N_CHIPS, 2, hr_small, LANES)

    hr_glu = width // (2 * N_CHIPS)
    grads = [g_in, g_glu.reshape(N_CHIPS, 2, hr_glu, width), g_out, g_up, g_down, g_small]
    tags = ["w_in", "glu_w", "w_out", "w_up", "w_down", "small"]
    from_sib = _pair_exchange(grads, "grads_to_sibling")
    pair = [_pair_sum(g, r, core, "pair_sum_" + t) for g, r, t in zip(grads, from_sib, tags)]
    from_chips = _chip_exchange(pair, "grads_to_owner")
    halves = [_chip_sum(q, "chip_sum_" + t) for q, t in zip(from_chips, tags)]
    owned = _pair_share(halves, "grads_to_both_cores")
    (small_all,) = _gather_chips([owned[5]], "gather_small_grads")
    small_flat = small_all.reshape(flat.shape)

    grad_out, delta_out, m_out, v_out = {}, {}, {}, {}
    for n, g in zip(large, owned[:5]):
        shp = weights[n].shape
        g2 = g.reshape(shp[1], shp[2])
        dl, nm, nv = _adamw(weights[n][0], g2, mom_m[n][0], mom_v[n][0], "adamw_" + n)
        grad_out[n], delta_out[n], m_out[n], v_out[n] = g2.reshape(shp), dl.reshape(shp), nm.reshape(shp), nv.reshape(shp)
    shapes = [weights[n].shape for n in small]
    dl, nm, nv = _adamw(
        _pack([weights[n] for n in small]), small_flat, _pack([mom_m[n] for n in small]),
        _pack([mom_v[n] for n in small]), "adamw_small")
    for n, g, a, b, c in zip(small, _unpack(small_flat, shapes), _unpack(dl, shapes), _unpack(nm, shapes), _unpack(nv, shapes)):
        grad_out[n], delta_out[n], m_out[n], v_out[n] = g, a, b, c

    loss = lax.psum(loss_part[0, 0], ("x", "y", "c"))
    return (loss, grad_x.reshape(x.shape), *[grad_out[n] for n in names], *[delta_out[n] for n in names],
            *[m_out[n] for n in names], *[v_out[n] for n in names])
```

```python
import functools
import math

import numpy as np
import jax
import jax.numpy as jnp
from jax import lax
from jax.experimental import pallas as pl
from jax.experimental.pallas import tpu as pltpu

F32 = jnp.float32
BF16 = jnp.bfloat16
MESH = pl.DeviceIdType.MESH
HIGHEST = lax.Precision.HIGHEST

EPS = 1e-6
ADAM_LR = 0.001
ADAM_B1 = 0.9
ADAM_B2 = 0.999
ADAM_EPS = 1e-08
ADAM_WD = 0.01
ADAM_STEP = 10

N_CHIPS = 4
LANES = 128
SSM_GROUP = 16
SSM_STATE = 64
GROUPS_PER_TILE = LANES // SSM_GROUP
CHUNK = 16
PAIRS = CHUNK // 2
SGU_CHUNK = 128
VMEM_LIMIT = 56 * 2**20


def _params(n_axes, vmem=VMEM_LIMIT):
    return pltpu.CompilerParams(dimension_semantics=("arbitrary",) * n_axes, vmem_limit_bytes=vmem)


def _call(body, **kw):
    return pl.pallas_call(body, **kw)


def _dot(a, b):
    return jnp.dot(a, b, preferred_element_type=F32)


def _dot_nt(a, b):
    return lax.dot_general(a, b, (((1,), (1,)), ((), ())), preferred_element_type=F32)


def _dot_tn(a, b):
    return lax.dot_general(a, b, (((0,), (0,)), ((), ())), preferred_element_type=F32)


_GELU_K = math.sqrt(2.0 / math.pi)
_GELU_C = 0.044715


def _gelu(x):
    return 0.5 * x * (1.0 + jnp.tanh(_GELU_K * (x + _GELU_C * x * x * x)))


def _gelu_grad(x):
    t = jnp.tanh(_GELU_K * (x + _GELU_C * x * x * x))
    return 0.5 * (1.0 + t) + 0.5 * x * (1.0 - t * t) * (_GELU_K * (1.0 + 3.0 * _GELU_C * x * x))


def _sigmoid(x):
    return 1.0 / (1.0 + jnp.exp(-x))


def _rms(x):
    return lax.rsqrt(jnp.mean(x * x, axis=-1, keepdims=True) + EPS)


def _rms_bwd(dy, x, r, g):
    a = dy * g
    dx = r * a - x * (r * r * r) * jnp.mean(a * x, axis=-1, keepdims=True)
    return dx, dy * x * r


def _row_tile(rows, target, mult=16):
    for t in range(min(rows, target), 0, -1):
        if rows % t == 0 and t % mult == 0:
            return t
    return rows


def _ssm_mats(a_re, a_im, b_re, b_im, c_re, c_im, d, log_dt):
    g, p = a_re.shape
    h = b_re.shape[-1]
    nt = g // GROUPS_PER_TILE
    dt = jnp.exp(log_dt)[:, None]
    lr, li = a_re * dt, a_im * dt

    def apow(l):
        mag = jnp.exp(lr * l)
        return mag * jnp.cos(li * l), mag * jnp.sin(li * l)

    ar, ai = apow(1.0)
    den = a_re * a_re + a_im * a_im
    qr = ((ar - 1.0) * a_re + ai * a_im) / den
    qi = (ai * a_re - (ar - 1.0) * a_im) / den
    bbr = qr[..., None] * b_re - qi[..., None] * b_im
    bbi = qr[..., None] * b_im + qi[..., None] * b_re
    ls = jnp.arange(CHUNK + 1, dtype=F32)[:, None, None]
    pr, pi = apow(ls)
    tr = pr[:CHUNK, :, :, None] * bbr[None] - pi[:CHUNK, :, :, None] * bbi[None]
    ti = pr[:CHUNK, :, :, None] * bbi[None] + pi[:CHUNK, :, :, None] * bbr[None]
    k = jnp.einsum("gop,lgpi->lgio", c_re, tr, precision=HIGHEST) - jnp.einsum(
        "gop,lgpi->lgio", c_im, ti, precision=HIGHEST
    )
    k = k.at[0].add(d[:, :, None] * jnp.eye(h, dtype=F32))
    gg = GROUPS_PER_TILE

    def tiles(a, tail):
        a = a.reshape((PAIRS, nt, gg) + a.shape[2:])
        a = jnp.moveaxis(a, 3, 2)
        return jnp.moveaxis(a, 1, 0).reshape(nt, PAIRS, 2 * gg * h, tail)

    kz = jnp.concatenate([jnp.zeros_like(k[:1]), k], axis=0)
    row0 = jnp.stack([kz[1::2], kz[2::2]], axis=3)
    row1 = jnp.stack([kz[0:-1:2], kz[1::2]], axis=3)
    k2c = tiles(jnp.stack([row0, row1], axis=2), 2 * h)
    k2c = jnp.pad(k2c, ((0, 0), (0, 0), (0, 0), (0, LANES - 2 * h)))
    wj = jnp.transpose(jnp.stack([tr[::-1], ti[::-1]], axis=2), (0, 1, 4, 2, 3))
    w2c = tiles(wj.reshape(PAIRS, 2, g, h, 2, p).swapaxes(1, 2), 2 * p)
    car = c_re[None] * pr[1:, :, None, :] - c_im[None] * pi[1:, :, None, :]
    cai = c_re[None] * pi[1:, :, None, :] + c_im[None] * pr[1:, :, None, :]
    vi = jnp.stack([car, -cai], axis=3)
    v2c = tiles(vi.reshape(PAIRS, 2, g, h, 2, p).swapaxes(1, 2), 2 * p)
    al = jnp.stack([pr[CHUNK].reshape(nt, -1), pi[CHUNK].reshape(nt, -1)], axis=1)
    return k2c, w2c, v2c, al


def _spread_consts(h, p):
    gg = GROUPS_PER_TILE
    row_g = (np.arange(2 * gg * h) // h) % gg
    colk = np.arange(2 * gg * h)
    rep_k = np.zeros((LANES, 2 * gg * h), np.float32)
    rep_k[(colk // (gg * h)) * h + colk % h, colk] = 1.0
    mask_k = (row_g[:, None] == ((colk // h) % gg)[None, :]).astype(np.float32)
    cols = np.arange(2 * gg * p)
    rep_s = np.zeros((2 * p, 2 * gg * p), np.float32)
    rep_s[(cols // (gg * p)) * p + cols % p, cols] = 1.0
    mask_s = (row_g[:, None] == ((cols // p) % gg)[None, :]).astype(np.float32)
    return tuple(jnp.asarray(a, BF16) for a in (rep_k, mask_k, rep_s, mask_s))


def _ssm_spread(k2c, w2c, v2c, consts):
    nt = k2c.shape[0]
    rep_k, mask_k, rep_s, mask_s = consts
    nk, ns = rep_k.shape[1], rep_s.shape[1]

    def body(k_ref, w_ref, v_ref, rk_ref, mk_ref, rs_ref, ms_ref, ko_ref, wo_ref, vo_ref):
        for q in range(PAIRS):
            ko_ref[q] = (_dot(k_ref[q].astype(BF16), rk_ref[...]) * mk_ref[...]).astype(BF16)
            wo_ref[q] = (_dot(w_ref[q].astype(BF16), rs_ref[...]) * ms_ref[...]).astype(BF16)
            vo_ref[q] = (_dot(v_ref[q].astype(BF16), rs_ref[...]) * ms_ref[...]).astype(BF16)

    def tile(cols):
        return pl.BlockSpec((None, PAIRS, nk, cols), lambda t: (t, 0, 0, 0))

    def whole(a):
        return pl.BlockSpec(a.shape, lambda t: (0, 0))

    return _call(
        body,
        grid=(nt,),
        in_specs=[tile(LANES), tile(LANES), tile(LANES), whole(rep_k), whole(mask_k), whole(rep_s), whole(mask_s)],
        out_specs=[tile(nk), tile(ns), tile(ns)],
        out_shape=[
            jax.ShapeDtypeStruct((nt, PAIRS, nk, nk), BF16),
            jax.ShapeDtypeStruct((nt, PAIRS, nk, ns), BF16),
            jax.ShapeDtypeStruct((nt, PAIRS, nk, ns), BF16),
        ],
        compiler_params=_params(1),
        name="ssm_spread",
    )(k2c, w2c, v2c, rep_k, mask_k, rep_s, mask_s)


def _gather_blocks(full, mask, rep):
    m = full * mask
    hi = m.astype(BF16)
    lo = (m - hi.astype(F32)).astype(BF16)
    return _dot_nt(hi, rep) + _dot_nt(lo, rep)


def _in_proj(x, g, w4, tm):
    s, d = x.shape
    nj, _, cw = w4.shape

    def body(x_ref, g_ref, w_ref, z_ref, h_ref):
        @pl.when(pl.program_id(1) == 0)
        def _():
            xv = x_ref[...]
            h_ref[...] = (xv * _rms(xv) * g_ref[...]).astype(BF16)

        z_ref[...] = _dot(h_ref[...], w_ref[...])

    return _call(
        body,
        grid=(s // tm, nj),
        in_specs=[
            pl.BlockSpec((tm, d), lambda i, j: (i, 0)),
            pl.BlockSpec((1, d), lambda i, j: (0, 0)),
            pl.BlockSpec((None, d, cw), lambda i, j: (j, 0, 0)),
        ],
        out_specs=[pl.BlockSpec((tm, cw), lambda i, j: (i, j)), pl.BlockSpec((tm, d), lambda i, j: (i, 0))],
        out_shape=[jax.ShapeDtypeStruct((s, nj * cw), F32), jax.ShapeDtypeStruct((s, d), BF16)],
        compiler_params=_params(2),
        name="in_proj",
    )(x, g, w4)


def _ssm_fwd(z, k2, w2, v2, al):
    s = z.shape[0]
    nt = k2.shape[0]
    nc = s // CHUNK
    ns = w2.shape[-1]
    hs = ns // 2

    def body(u_ref, k_ref, w_ref, v_ref, al_ref, y_ref, xp_ref, us_ref, xloc):
        for q in range(PAIRS):
            us_ref[q, :, 0:LANES] = u_ref[pl.ds(2 * q, nc, stride=CHUNK), :].astype(BF16)
            us_ref[q, :, LANES : 2 * LANES] = u_ref[pl.ds(2 * q + 1, nc, stride=CHUNK), :].astype(BF16)
        acc = _dot(us_ref[0], w_ref[0])
        for q in range(1, PAIRS):
            acc = acc + _dot(us_ref[q], w_ref[q])
        xloc[...] = acc
        ar = al_ref[0:1, :]
        ai = al_ref[1:2, :]

        def step(c, carry):
            xr, xi = carry
            xp_ref[pl.ds(c, 1), 0:hs] = xr
            xp_ref[pl.ds(c, 1), hs:ns] = xi
            lr = xloc[pl.ds(c, 1), 0:hs]
            li = xloc[pl.ds(c, 1), hs:ns]
            return ar * xr - ai * xi + lr, ar * xi + ai * xr + li

        zero = jnp.zeros((1, hs), F32)
        lax.fori_loop(0, nc, step, (zero, zero))
        xpb = xp_ref[...].astype(BF16)
        for r in range(PAIRS):
            acc = _dot_nt(xpb, v_ref[r])
            for q in range(r + 1):
                acc = acc + _dot(us_ref[q], k_ref[r - q])
            y_ref[pl.ds(2 * r, nc, stride=CHUNK), :] = acc[:, 0:LANES]
            y_ref[pl.ds(2 * r + 1, nc, stride=CHUNK), :] = acc[:, LANES : 2 * LANES]

    return _call(
        body,
        grid=(nt,),
        in_specs=[
            pl.BlockSpec((s, LANES), lambda t: (0, t)),
            pl.BlockSpec((None, PAIRS, 2 * LANES, 2 * LANES), lambda t: (t, 0, 0, 0)),
            pl.BlockSpec((None, PAIRS, 2 * LANES, ns), lambda t: (t, 0, 0, 0)),
            pl.BlockSpec((None, PAIRS, 2 * LANES, ns), lambda t: (t, 0, 0, 0)),
            pl.BlockSpec((None, 2, hs), lambda t: (t, 0, 0)),
        ],
        out_specs=[
            pl.BlockSpec((s, LANES), lambda t: (0, t)),
            pl.BlockSpec((None, nc, ns), lambda t: (t, 0, 0)),
            pl.BlockSpec((None, PAIRS, nc, 2 * LANES), lambda t: (t, 0, 0, 0)),
        ],
        out_shape=[
            jax.ShapeDtypeStruct((s, nt * LANES), F32),
            jax.ShapeDtypeStruct((nt, nc, ns), F32),
            jax.ShapeDtypeStruct((nt, PAIRS, nc, 2 * LANES), BF16),
        ],
        scratch_shapes=[pltpu.VMEM((nc, ns), F32)],
        compiler_params=_params(1),
        name="ssm_fwd",
    )(z, k2, w2, v2, al)


def _glu_fwd(y_pre, wg, bg, go, d_model, tm):
    s, w = y_pre.shape

    def body(y_ref, wg_ref, bg_ref, go_ref, o_ref):
        yg = _gelu(y_ref[...])
        gate = _sigmoid(_dot(yg.astype(BF16), wg_ref[...]) + bg_ref[...])
        ys = yg * gate
        o_ref[...] = (ys * _rms(ys) * go_ref[...]).astype(BF16)

    return _call(
        body,
        grid=(s // tm,),
        in_specs=[
            pl.BlockSpec((tm, w), lambda i: (i, 0)),
            pl.BlockSpec((w, w), lambda i: (0, 0)),
            pl.BlockSpec((1, w), lambda i: (0, 0)),
            pl.BlockSpec((1, w), lambda i: (0, 0)),
        ],
        out_specs=pl.BlockSpec((tm, w), lambda i: (i, 0)),
        out_shape=jax.ShapeDtypeStruct((s, d_model), BF16),
        compiler_params=_params(1),
        name="glu_fwd",
    )(y_pre, wg, bg, go)


def _sgu_parts(zu, zv, lng, lnb, wt_ref, bias_ref):
    u = _gelu(zu)
    v = _gelu(zv)
    mu = jnp.mean(v, axis=-1, keepdims=True)
    vc = v - mu
    rstd = lax.rsqrt(jnp.mean(vc * vc, axis=-1, keepdims=True) + EPS)
    vhat = vc * rstd
    vb = (vhat * lng + lnb).astype(BF16)
    heads = wt_ref.shape[0]
    mix = jnp.concatenate(
        [_dot(wt_ref[h], vb[:, h * LANES : (h + 1) * LANES]) + bias_ref[h] for h in range(heads)], axis=1
    )
    return u, vhat, rstd, vb, mix


def _sgu_fwd(z, mixed, lng, lnb, wt, biasb, go, rb):
    s = z.shape[0]
    w = lng.shape[-1]
    heads = wt.shape[0]

    def body(zu_ref, zv_ref, m_any, lng_ref, lnb_ref, wt_ref, b_ref, go_ref, o_ref):
        del m_any
        for ck in range(rb // SGU_CHUNK):
            rows = slice(ck * SGU_CHUNK, (ck + 1) * SGU_CHUNK)
            u, _, _, _, mix = _sgu_parts(zu_ref[rows, :], zv_ref[rows, :], lng_ref[...], lnb_ref[...], wt_ref, b_ref)
            y = u * mix
            o_ref[rows, :] = (y * _rms(y) * go_ref[...]).astype(BF16)

    return _call(
        body,
        grid=(s // rb,),
        in_specs=[
            pl.BlockSpec((rb, w), lambda i: (i, 1)),
            pl.BlockSpec((rb, w), lambda i: (i, 2)),
            pl.BlockSpec(memory_space=pl.ANY),
            pl.BlockSpec((1, w), lambda i: (0, 0)),
            pl.BlockSpec((1, w), lambda i: (0, 0)),
            pl.BlockSpec((heads, SGU_CHUNK, SGU_CHUNK), lambda i: (0, 0, 0)),
            pl.BlockSpec((heads, SGU_CHUNK, LANES), lambda i: (0, 0, 0)),
            pl.BlockSpec((1, w), lambda i: (0, 0)),
        ],
        out_specs=pl.BlockSpec((rb, w), lambda i: (i, 1)),
        out_shape=jax.ShapeDtypeStruct(mixed.shape, mixed.dtype),
        input_output_aliases={2: 0},
        compiler_params=_params(1),
        name="sgu_fwd",
    )(z, z, mixed, lng, lnb, wt, biasb, go)


def _out_proj(x, mixed, w_out, tm):
    s, d = x.shape

    def body(x_ref, m_ref, w_ref, o_ref):
        o_ref[...] = x_ref[...] + _dot(m_ref[...], w_ref[...])

    return _call(
        body,
        grid=(s // tm,),
        in_specs=[
            pl.BlockSpec((tm, d), lambda i: (i, 0)),
            pl.BlockSpec((tm, d), lambda i: (i, 0)),
            pl.BlockSpec((d, d), lambda i: (0, 0)),
        ],
        out_specs=pl.BlockSpec((tm, d), lambda i: (i, 0)),
        out_shape=jax.ShapeDtypeStruct((s, d), F32),
        compiler_params=_params(1),
        name="out_proj",
    )(x, mixed, w_out)


def _mlp_fwd(x1, g, w_up4, w_down, tm, tf):
    s, d = x1.shape
    ff = w_down.shape[0]
    cw = w_up4.shape[-1]
    per = cw // tf
    nf = ff // tf

    def body(x_ref, g_ref, wu_ref, wd_ref, x2_ref, up_ref, h_ref):
        @pl.when(pl.program_id(1) == 0)
        def _():
            xv = x_ref[...]
            h_ref[...] = (xv * _rms(xv) * g_ref[...]).astype(BF16)
            x2_ref[...] = xv

        upb = _dot(h_ref[...], wu_ref[...]).astype(BF16)
        up_ref[...] = upb
        a = jnp.maximum(upb.astype(F32), 0.0)
        x2_ref[...] += _dot((a * a).astype(BF16), wd_ref[...])

    return _call(
        body,
        grid=(s // tm, nf),
        in_specs=[
            pl.BlockSpec((tm, d), lambda i, f: (i, 0)),
            pl.BlockSpec((1, d), lambda i, f: (0, 0)),
            pl.BlockSpec((None, d, tf), lambda i, f: (f // per, 0, f % per)),
            pl.BlockSpec((tf, d), lambda i, f: (f, 0)),
        ],
        out_specs=[
            pl.BlockSpec((tm, d), lambda i, f: (i, 0)),
            pl.BlockSpec((tm, tf), lambda i, f: (i, f)),
            pl.BlockSpec((tm, d), lambda i, f: (i, 0)),
        ],
        out_shape=[
            jax.ShapeDtypeStruct((s, d), F32),
            jax.ShapeDtypeStruct((s, ff), BF16),
            jax.ShapeDtypeStruct((s, d), BF16),
        ],
        compiler_params=_params(2),
        name="mlp_fwd",
    )(x1, g, w_up4, w_down)


def _final(x2, target, g, tm):
    s, d = x2.shape

    def body(x_ref, t_ref, g_ref, loss_ref, dx_ref, dxb_ref, dg_ref):
        @pl.when(pl.program_id(0) == 0)
        def _():
            loss_ref[...] = jnp.zeros_like(loss_ref)
            dg_ref[...] = jnp.zeros_like(dg_ref)

        xv = x_ref[...]
        r = _rms(xv)
        gv = g_ref[...]
        diff = xv * r * gv - t_ref[...]
        loss_ref[...] += 0.5 * jnp.sum(jnp.mean(diff * diff, axis=-1, keepdims=True), axis=0, keepdims=True)
        dx, dgt = _rms_bwd(diff * (1.0 / d), xv, r, gv)
        dx_ref[...] = dx
        dxb_ref[...] = dx.astype(BF16)
        dg_ref[...] += jnp.sum(dgt, axis=0, keepdims=True)

    return _call(
        body,
        grid=(s // tm,),
        in_specs=[
            pl.BlockSpec((tm, d), lambda i: (i, 0)),
            pl.BlockSpec((tm, d), lambda i: (i, 0)),
            pl.BlockSpec((1, d), lambda i: (0, 0)),
        ],
        out_specs=[
            pl.BlockSpec((1, 1), lambda i: (0, 0)),
            pl.BlockSpec((tm, d), lambda i: (i, 0)),
            pl.BlockSpec((tm, d), lambda i: (i, 0)),
            pl.BlockSpec((1, d), lambda i: (0, 0)),
        ],
        out_shape=[
            jax.ShapeDtypeStruct((1, 1), F32),
            jax.ShapeDtypeStruct((s, d), F32),
            jax.ShapeDtypeStruct((s, d), BF16),
            jax.ShapeDtypeStruct((1, d), F32),
        ],
        compiler_params=_params(1),
        name="final_loss",
    )(x2, target, g)


def _mlp_bwd(dx2b, up, w_up4, w_down, tm, tf):
    s, d = dx2b.shape
    ff = w_down.shape[0]
    cw = w_up4.shape[-1]
    per = cw // tf

    def body(dx_ref, up_ref, wu_ref, wd_ref, dup_ref, dh_ref):
        @pl.when(pl.program_id(1) == 0)
        def _():
            dh_ref[...] = jnp.zeros_like(dh_ref)

        dact = _dot_nt(dx_ref[...], wd_ref[...])
        dupb = (dact * (2.0 * jnp.maximum(up_ref[...].astype(F32), 0.0))).astype(BF16)
        dup_ref[...] = dupb
        dh_ref[...] += _dot_nt(dupb, wu_ref[...])

    return _call(
        body,
        grid=(s // tm, ff // tf),
        in_specs=[
            pl.BlockSpec((tm, d), lambda i, f: (i, 0)),
            pl.BlockSpec((tm, tf), lambda i, f: (i, f)),
            pl.BlockSpec((None, d, tf), lambda i, f: (f // per, 0, f % per)),
            pl.BlockSpec((tf, d), lambda i, f: (f, 0)),
        ],
        out_specs=[pl.BlockSpec((tm, tf), lambda i, f: (i, f)), pl.BlockSpec((tm, d), lambda i, f: (i, 0))],
        out_shape=[jax.ShapeDtypeStruct((s, ff), BF16), jax.ShapeDtypeStruct((s, d), F32)],
        compiler_params=_params(2),
        name="mlp_bwd",
    )(dx2b, up, w_up4, w_down)


def _norm_bwd(dh, x, dres, g, tm, name):
    s, d = x.shape

    def body(dh_ref, x_ref, dr_ref, g_ref, dx_ref, dxb_ref, dg_ref):
        @pl.when(pl.program_id(0) == 0)
        def _():
            dg_ref[...] = jnp.zeros_like(dg_ref)

        xv = x_ref[...]
        dx, dgt = _rms_bwd(dh_ref[...], xv, _rms(xv), g_ref[...])
        tot = dr_ref[...] + dx
        dx_ref[...] = tot
        dxb_ref[...] = tot.astype(BF16)
        dg_ref[...] += jnp.sum(dgt, axis=0, keepdims=True)

    row = pl.BlockSpec((tm, d), lambda i: (i, 0))
    vec = pl.BlockSpec((1, d), lambda i: (0, 0))
    return _call(
        body,
        grid=(s // tm,),
        in_specs=[row, row, row, vec],
        out_specs=[row, row, vec],
        out_shape=[
            jax.ShapeDtypeStruct((s, d), F32),
            jax.ShapeDtypeStruct((s, d), BF16),
            jax.ShapeDtypeStruct((1, d), F32),
        ],
        compiler_params=_params(1),
        name=name,
    )(dh, x, dres, g)


def _grad_w(a, b, out_dims, index_map, tm, tn, tk, relu2, name):
    t, m = a.shape
    n = b.shape[1]
    nk = t // tk

    def body(a_ref, b_ref, o_ref, acc):
        k = pl.program_id(2)

        @pl.when(k == 0)
        def _():
            acc[...] = jnp.zeros_like(acc)

        av = a_ref[...]
        if relu2:
            r = jnp.maximum(av.astype(F32), 0.0)
            av = (r * r).astype(BF16)
        acc[...] += _dot_tn(av, b_ref[...])

        @pl.when(k == nk - 1)
        def _():
            o_ref[...] = acc[...].astype(BF16)

    return _call(
        body,
        grid=(m // tm, n // tn, nk),
        in_specs=[pl.BlockSpec((tk, tm), lambda i, j, k: (k, i)), pl.BlockSpec((tk, tn), lambda i, j, k: (k, j))],
        out_specs=pl.BlockSpec((None, None, tm, tn), lambda i, j, k: index_map(i, j)),
        out_shape=jax.ShapeDtypeStruct(out_dims, BF16),
        scratch_shapes=[pltpu.VMEM((tm, tn), F32)],
        compiler_params=_params(3),
        name=name,
    )(a, b)


def _out_proj_bwd(dx1b, w_out, tm):
    s, d = dx1b.shape

    def body(dx_ref, w_ref, o_ref):
        o_ref[...] = _dot_nt(dx_ref[...], w_ref[...])

    return _call(
        body,
        grid=(s // tm,),
        in_specs=[pl.BlockSpec((tm, d), lambda i: (i, 0)), pl.BlockSpec((d, d), lambda i: (0, 0))],
        out_specs=pl.BlockSpec((tm, d), lambda i: (i, 0)),
        out_shape=jax.ShapeDtypeStruct((s, d), F32),
        compiler_params=_params(1),
        name="out_proj_bwd",
    )(dx1b, w_out)


def _glu_bwd(y_pre, dmix, wg, bg, go, tm):
    s, w = y_pre.shape
    n = s // tm

    def body(y_ref, dm_ref, wg_ref, bg_ref, go_ref, dy_ref, dwg_ref, dbg_ref, dgo_ref, acc):
        i = pl.program_id(0)

        @pl.when(i == 0)
        def _():
            acc[...] = jnp.zeros_like(acc)
            dbg_ref[...] = jnp.zeros_like(dbg_ref)
            dgo_ref[...] = jnp.zeros_like(dgo_ref)

        yp = y_ref[...]
        yg = _gelu(yp)
        ygb = yg.astype(BF16)
        gate = _sigmoid(_dot(ygb, wg_ref[...]) + bg_ref[...])
        ys = yg * gate
        dys, dgt = _rms_bwd(dm_ref[...], ys, _rms(ys), go_ref[...])
        dgo_ref[...] += jnp.sum(dgt, axis=0, keepdims=True)
        dpre = dys * yg * gate * (1.0 - gate)
        dbg_ref[...] += jnp.sum(dpre, axis=0, keepdims=True)
        dpb = dpre.astype(BF16)
        dyg = dys * gate + _dot_nt(dpb, wg_ref[...])
        dy_ref[...] = dyg * _gelu_grad(yp)
        acc[...] += _dot_tn(ygb, dpb)

        @pl.when(i == n - 1)
        def _():
            dwg_ref[...] = acc[...].astype(BF16)

    return _call(
        body,
        grid=(n,),
        in_specs=[
            pl.BlockSpec((tm, w), lambda i: (i, 0)),
            pl.BlockSpec((tm, w), lambda i: (i, 0)),
            pl.BlockSpec((w, w), lambda i: (0, 0)),
            pl.BlockSpec((1, w), lambda i: (0, 0)),
            pl.BlockSpec((1, w), lambda i: (0, 0)),
        ],
        out_specs=[
            pl.BlockSpec((tm, w), lambda i: (i, 0)),
            pl.BlockSpec((w, w), lambda i: (0, 0)),
            pl.BlockSpec((1, w), lambda i: (0, 0)),
            pl.BlockSpec((1, w), lambda i: (0, 0)),
        ],
        out_shape=[
            jax.ShapeDtypeStruct((s, w), F32),
            jax.ShapeDtypeStruct((w, w), BF16),
            jax.ShapeDtypeStruct((1, w), F32),
            jax.ShapeDtypeStruct((1, w), F32),
        ],
        scratch_shapes=[pltpu.VMEM((w, w), F32)],
        compiler_params=_params(1),
        name="glu_bwd",
    )(y_pre, dmix, wg, bg, go)


def _ssm_bwd_state(dy, v2, al, xprev):
    s = dy.shape[0]
    nt, nc, ns = xprev.shape
    hs = ns // 2

    def body(dy_ref, v_ref, al_ref, xp_ref, dys_ref, g_ref, dal_ref, dxp, gs):
        for q in range(PAIRS):
            dys_ref[q, :, 0:LANES] = dy_ref[pl.ds(2 * q, nc, stride=CHUNK), :].astype(BF16)
            dys_ref[q, :, LANES : 2 * LANES] = dy_ref[pl.ds(2 * q + 1, nc, stride=CHUNK), :].astype(BF16)
        acc = _dot(dys_ref[0], v_ref[0])
        for r in range(1, PAIRS):
            acc = acc + _dot(dys_ref[r], v_ref[r])
        dxp[...] = acc
        ar = al_ref[0:1, :]
        ai = al_ref[1:2, :]
        zero = jnp.zeros((1, hs), F32)
        gs[pl.ds(nc - 1, 1), 0:hs] = zero
        gs[pl.ds(nc - 1, 1), hs:ns] = zero

        def step(n, carry):
            gr, gi = carry
            c = nc - 2 - n
            nr = dxp[pl.ds(c + 1, 1), 0:hs] + ar * gr + ai * gi
            ni = dxp[pl.ds(c + 1, 1), hs:ns] + ar * gi - ai * gr
            gs[pl.ds(c, 1), 0:hs] = nr
            gs[pl.ds(c, 1), hs:ns] = ni
            return nr, ni

        lax.fori_loop(0, nc - 1, step, (zero, zero))
        gv = gs[...]
        xv = xp_ref[...]
        gr, gi = gv[:, 0:hs], gv[:, hs:ns]
        xr, xi = xv[:, 0:hs], xv[:, hs:ns]
        dal_ref[0:1, :] = jnp.sum(gr * xr + gi * xi, axis=0, keepdims=True)
        dal_ref[1:2, :] = jnp.sum(gi * xr - gr * xi, axis=0, keepdims=True)
        g_ref[...] = gv.astype(BF16)

    return _call(
        body,
        grid=(nt,),
        in_specs=[
            pl.BlockSpec((s, LANES), lambda t: (0, t)),
            pl.BlockSpec((None, PAIRS, 2 * LANES, ns), lambda t: (t, 0, 0, 0)),
            pl.BlockSpec((None, 2, hs), lambda t: (t, 0, 0)),
            pl.BlockSpec((None, nc, ns), lambda t: (t, 0, 0)),
        ],
        out_specs=[
            pl.BlockSpec((None, PAIRS, nc, 2 * LANES), lambda t: (t, 0, 0, 0)),
            pl.BlockSpec((None, nc, ns), lambda t: (t, 0, 0)),
            pl.BlockSpec((None, 2, hs), lambda t: (t, 0, 0)),
        ],
        out_shape=[
            jax.ShapeDtypeStruct((nt, PAIRS, nc, 2 * LANES), BF16),
            jax.ShapeDtypeStruct((nt, nc, ns), BF16),
            jax.ShapeDtypeStruct((nt, 2, hs), F32),
        ],
        scratch_shapes=[pltpu.VMEM((nc, ns), F32), pltpu.VMEM((nc, ns), F32)],
        compiler_params=_params(1),
        name="ssm_bwd_state",
    )(dy, v2, al, xprev)


def _ssm_bwd_main(us, dys, k2, w2, xprev, gst, consts, s, width):
    nt, _, nc, _ = us.shape
    ns = xprev.shape[-1]
    rep_k, mask_k, rep_s, mask_s = consts
    nk = 2 * LANES

    def body(us_ref, dys_ref, k_ref, w_ref, xp_ref, g_ref, rk_ref, mk_ref, rs_ref, ms_ref,
             du_ref, dk_ref, dw_ref, dv_ref, acc, du2, duf, dkf):
        q = pl.program_id(1)

        @pl.when(q == 0)
        def _():
            dkf[...] = jnp.zeros_like(dkf)

        usq = us_ref[q]
        gb = g_ref[...]
        dw_ref[...] = _gather_blocks(_dot_tn(usq, gb), ms_ref[...], rs_ref[...])
        dv_ref[...] = _gather_blocks(_dot_tn(dys_ref[q], xp_ref[...].astype(BF16)), ms_ref[...], rs_ref[...])
        acc[...] = _dot_nt(gb, w_ref[...])
        for m in range(PAIRS):

            @pl.when(q + m < PAIRS)
            def _():
                dyr = dys_ref[jnp.minimum(q + m, PAIRS - 1)]
                acc[...] += _dot_nt(dyr, k_ref[m])
                dkf[m] += _dot_tn(usq, dyr)

        du2[q] = acc[...]

        @pl.when(q == PAIRS - 1)
        def _():
            for p in range(PAIRS):
                duf[pl.ds(2 * p, nc, stride=CHUNK), :] = du2[p, :, 0:LANES]
                duf[pl.ds(2 * p + 1, nc, stride=CHUNK), :] = du2[p, :, LANES : 2 * LANES]
                dk_ref[p] = _gather_blocks(dkf[p], mk_ref[...], rk_ref[...])
            du_ref[...] = duf[...].astype(BF16)

    def per_tile(rows, cols):
        return pl.BlockSpec((None, PAIRS, rows, cols), lambda t, q: (t, 0, 0, 0))

    def per_pair(rows, cols):
        return pl.BlockSpec((None, None, rows, cols), lambda t, q: (t, q, 0, 0))

    def whole(a):
        return pl.BlockSpec(a.shape, lambda t, q: (0, 0))

    return _call(
        body,
        grid=(nt, PAIRS),
        in_specs=[
            per_tile(nc, nk),
            per_tile(nc, nk),
            per_tile(nk, nk),
            per_pair(nk, ns),
            pl.BlockSpec((None, nc, ns), lambda t, q: (t, 0, 0)),
            pl.BlockSpec((None, nc, ns), lambda t, q: (t, 0, 0)),
            whole(rep_k),
            whole(mask_k),
            whole(rep_s),
            whole(mask_s),
        ],
        out_specs=[
            pl.BlockSpec((s, LANES), lambda t, q: (0, t)),
            per_tile(nk, LANES),
            per_pair(nk, LANES),
            per_pair(nk, LANES),
        ],
        out_shape=[
            jax.ShapeDtypeStruct((s, width), BF16),
            jax.ShapeDtypeStruct((nt, PAIRS, nk, LANES), F32),
            jax.ShapeDtypeStruct((nt, PAIRS, nk, LANES), F32),
            jax.ShapeDtypeStruct((nt, PAIRS, nk, LANES), F32),
        ],
        scratch_shapes=[
            pltpu.VMEM((nc, nk), F32),
            pltpu.VMEM((PAIRS, nc, nk), F32),
            pltpu.VMEM((s, LANES), F32),
            pltpu.VMEM((PAIRS, nk, nk), F32),
        ],
        compiler_params=_params(2),
        name="ssm_bwd_main",
    )(us, dys, k2, w2, xprev, gst, rep_k, mask_k, rep_s, mask_s)


def _sgu_bwd(z, dmix, dz, lng, lnb, wt, wtt, biasb, go, rb):
    s = z.shape[0]
    w = lng.shape[-1]
    heads = wt.shape[0]
    ncol = (LANES, LANES)

    def body(zu_ref, zv_ref, dm_ref, dz_any, lng_ref, lnb_ref, wt_ref, wtt_ref, b_ref, go_ref,
             dz_ref, dw_ref, db_ref, dlg_ref, dlb_ref, dgo_ref, dzv):
        del dz_any
        i = pl.program_id(0)
        p = pl.program_id(1)

        @pl.when(jnp.logical_and(i == 0, p == 0))
        def _():
            dw_ref[...] = jnp.zeros_like(dw_ref)
            db_ref[...] = jnp.zeros_like(db_ref)
            dlg_ref[...] = jnp.zeros_like(dlg_ref)
            dlb_ref[...] = jnp.zeros_like(dlb_ref)
            dgo_ref[...] = jnp.zeros_like(dgo_ref)

        @pl.when(p == 0)
        def _():
            lng_v = lng_ref[...]
            for ck in range(rb // SGU_CHUNK):
                rows = slice(ck * SGU_CHUNK, (ck + 1) * SGU_CHUNK)
                zu = zu_ref[rows, :]
                zv = zv_ref[rows, :]
                u, vhat, rstd, vb, mix = _sgu_parts(zu, zv, lng_v, lnb_ref[...], wt_ref, b_ref)
                y = u * mix
                dy, dgt = _rms_bwd(dm_ref[rows, :], y, _rms(y), go_ref[...])
                dgo_ref[...] += jnp.sum(dgt, axis=0, keepdims=True)
                du = dy * mix
                dmx = dy * u
                dmb = dmx.astype(BF16)
                dvl = []
                for h in range(heads):
                    cols = slice(h * LANES, (h + 1) * LANES)
                    db_ref[h] += jnp.broadcast_to(jnp.sum(dmx[:, cols], axis=-1, keepdims=True), ncol)
                    dw_ref[h] += _dot_nt(dmb[:, cols], vb[:, cols])
                    dvl.append(_dot(wtt_ref[h], dmb[:, cols]))
                dvln = jnp.concatenate(dvl, axis=1)
                dlg_ref[...] += jnp.sum(dvln * vhat, axis=0, keepdims=True)
                dlb_ref[...] += jnp.sum(dvln, axis=0, keepdims=True)
                dvh = dvln * lng_v
                dv = rstd * (
                    dvh
                    - jnp.mean(dvh, axis=-1, keepdims=True)
                    - vhat * jnp.mean(dvh * vhat, axis=-1, keepdims=True)
                )
                dz_ref[rows, :] = (du * _gelu_grad(zu)).astype(BF16)
                dzv[rows, :] = (dv * _gelu_grad(zv)).astype(BF16)

        @pl.when(p == 1)
        def _():
            dz_ref[...] = dzv[...]

    return _call(
        body,
        grid=(s // rb, 2),
        in_specs=[
            pl.BlockSpec((rb, w), lambda i, p: (i, 1)),
            pl.BlockSpec((rb, w), lambda i, p: (i, 2)),
            pl.BlockSpec((rb, w), lambda i, p: (i, 1)),
            pl.BlockSpec(memory_space=pl.ANY),
            pl.BlockSpec((1, w), lambda i, p: (0, 0)),
            pl.BlockSpec((1, w), lambda i, p: (0, 0)),
            pl.BlockSpec((heads, SGU_CHUNK, SGU_CHUNK), lambda i, p: (0, 0, 0)),
            pl.BlockSpec((heads, SGU_CHUNK, SGU_CHUNK), lambda i, p: (0, 0, 0)),
            pl.BlockSpec((heads, SGU_CHUNK, LANES), lambda i, p: (0, 0, 0)),
            pl.BlockSpec((1, w), lambda i, p: (0, 0)),
        ],
        out_specs=[
            pl.BlockSpec((rb, w), lambda i, p: (i, 1 + p)),
            pl.BlockSpec((heads, SGU_CHUNK, SGU_CHUNK), lambda i, p: (0, 0, 0)),
            pl.BlockSpec((heads, SGU_CHUNK, LANES), lambda i, p: (0, 0, 0)),
            pl.BlockSpec((1, w), lambda i, p: (0, 0)),
            pl.BlockSpec((1, w), lambda i, p: (0, 0)),
            pl.BlockSpec((1, w), lambda i, p: (0, 0)),
        ],
        out_shape=[
            jax.ShapeDtypeStruct(dz.shape, dz.dtype),
            jax.ShapeDtypeStruct((heads, SGU_CHUNK, SGU_CHUNK), F32),
            jax.ShapeDtypeStruct((heads, SGU_CHUNK, LANES), F32),
            jax.ShapeDtypeStruct((1, w), F32),
            jax.ShapeDtypeStruct((1, w), F32),
            jax.ShapeDtypeStruct((1, w), F32),
        ],
        scratch_shapes=[pltpu.VMEM((rb, w), BF16)],
        input_output_aliases={3: 0},
        compiler_params=_params(2),
        name="sgu_bwd",
    )(z, z, dmix, dz, lng, lnb, wt, wtt, biasb, go)


def _in_proj_bwd(dz, w4, tm):
    s = dz.shape[0]
    nj, d, cw = w4.shape

    def body(dz_ref, w_ref, dh_ref):
        @pl.when(pl.program_id(1) == 0)
        def _():
            dh_ref[...] = jnp.zeros_like(dh_ref)

        dh_ref[...] += _dot_nt(dz_ref[...], w_ref[...])

    return _call(
        body,
        grid=(s // tm, nj),
        in_specs=[pl.BlockSpec((tm, cw), lambda i, j: (i, j)), pl.BlockSpec((None, d, cw), lambda i, j: (j, 0, 0))],
        out_specs=pl.BlockSpec((tm, d), lambda i, j: (i, 0)),
        out_shape=jax.ShapeDtypeStruct((s, d), F32),
        compiler_params=_params(2),
        name="in_proj_bwd",
    )(dz, w4)


_ANY = pl.BlockSpec(memory_space=pl.ANY)


def _position():
    x, y, c = lax.axis_index("x"), lax.axis_index("y"), lax.axis_index("c")
    return x, y, c, [(1 - x, y), (x, 1 - y), (1 - x, 1 - y)]


def _remote(src, dst, send_sems, recv_sems, k, to):
    return pltpu.make_async_remote_copy(
        src_ref=src, dst_ref=dst, send_sem=send_sems.at[k], recv_sem=recv_sems.at[k], device_id=to, device_id_type=MESH
    )


def _gather_chips(arrs, name):
    n = len(arrs)

    def body(*refs):
        ins, outs = refs[:n], refs[n : 2 * n]
        send_sems, recv_sems, local_sems = refs[2 * n :]
        x, y, c, chips = _position()
        me = 2 * x + y
        local = [pltpu.make_async_copy(ins[i], outs[i].at[me], local_sems.at[i]) for i in range(n)]
        for cp in local:
            cp.start()
        sends = []
        for k, (px, py) in enumerate(chips):
            for i in range(n):
                sends.append(_remote(ins[i], outs[i].at[me], send_sems, recv_sems, k * n + i, (px, py, c)))
                sends[-1].start()
        for k, (px, py) in enumerate(chips):
            for i in range(n):
                _remote(ins[i], outs[i].at[2 * px + py], send_sems, recv_sems, k * n + i, (px, py, c)).wait_recv()
        for cp in sends:
            cp.wait_send()
        for cp in local:
            cp.wait()

    return _call(
        body,
        in_specs=[_ANY] * n,
        out_specs=[_ANY] * n,
        out_shape=[jax.ShapeDtypeStruct((N_CHIPS,) + a.shape, a.dtype) for a in arrs],
        scratch_shapes=[
            pltpu.SemaphoreType.DMA((3 * n,)),
            pltpu.SemaphoreType.DMA((3 * n,)),
            pltpu.SemaphoreType.DMA((n,)),
        ],
        name=name,
    )(*arrs)


def _pair_exchange(gs, name):
    n = len(gs)

    def body(*refs):
        ins, outs = refs[:n], refs[n : 2 * n]
        send_sems, recv_sems = refs[2 * n :]
        x, y, c, _ = _position()
        sib = (x, y, 1 - c)
        cps = []
        for i in range(n):
            for j in range(N_CHIPS):
                cps.append(_remote(ins[i].at[j, 1 - c], outs[i].at[j], send_sems, recv_sems, i * N_CHIPS + j, sib))
                cps[-1].start()
        for cp in cps:
            cp.wait_recv()
        for cp in cps:
            cp.wait_send()

    return _call(
        body,
        in_specs=[_ANY] * n,
        out_specs=[_ANY] * n,
        out_shape=[jax.ShapeDtypeStruct((g.shape[0],) + g.shape[2:], g.dtype) for g in gs],
        scratch_shapes=[pltpu.SemaphoreType.DMA((n * N_CHIPS,)), pltpu.SemaphoreType.DMA((n * N_CHIPS,))],
        name=name,
    )(*gs)


def _gather_halves(shards, name):
    n = len(shards)

    def body(*refs):
        ins, outs = refs[:n], refs[n : 2 * n]
        ici_send, ici_recv, d2d_send, d2d_recv = refs[2 * n :]
        x, y, c, chips = _position()
        me = 2 * x + y
        sib = (x, y, 1 - c)
        started = []
        for i in range(n):
            started.append(_remote(ins[i].at[c], outs[i].at[me, c], d2d_send, d2d_recv, i, sib))
            started[-1].start()
        for k, (px, py) in enumerate(chips):
            for i in range(n):
                started.append(_remote(ins[i].at[c], outs[i].at[me, c], ici_send, ici_recv, k * n + i, (px, py, c)))
                started[-1].start()
        for k, (px, py) in enumerate(chips):
            for i in range(n):
                landed = outs[i].at[2 * px + py, c]
                _remote(ins[i].at[c], landed, ici_send, ici_recv, k * n + i, (px, py, c)).wait_recv()
                started.append(_remote(landed, landed, d2d_send, d2d_recv, (k + 1) * n + i, sib))
                started[-1].start()
        for i in range(n):
            _remote(ins[i].at[1 - c], outs[i].at[me, 1 - c], d2d_send, d2d_recv, i, sib).wait_recv()
        for k, (px, py) in enumerate(chips):
            for i in range(n):
                theirs = outs[i].at[2 * px + py, 1 - c]
                _remote(theirs, theirs, d2d_send, d2d_recv, (k + 1) * n + i, sib).wait_recv()
        for cp in started:
            cp.wait_send()

    return _call(
        body,
        in_specs=[_ANY] * n,
        out_specs=[_ANY] * n,
        out_shape=[jax.ShapeDtypeStruct((N_CHIPS,) + a.shape, a.dtype) for a in shards],
        scratch_shapes=[
            pltpu.SemaphoreType.DMA((3 * n,)),
            pltpu.SemaphoreType.DMA((3 * n,)),
            pltpu.SemaphoreType.DMA((4 * n,)),
            pltpu.SemaphoreType.DMA((4 * n,)),
        ],
        name=name,
    )(*shards)


def _place_own(shard, gathered, pos, name):
    _, hr, cols = shard.shape
    tr = _row_tile(hr, 512)

    def body(pos_ref, s_ref, g_any, o_ref):
        del pos_ref, g_any
        o_ref[...] = s_ref[...]

    return _call(
        body,
        grid_spec=pltpu.PrefetchScalarGridSpec(
            num_scalar_prefetch=1,
            grid=(hr // tr,),
            in_specs=[pl.BlockSpec((None, tr, cols), lambda i, p: (p[1], i, 0)), _ANY],
            out_specs=pl.BlockSpec((None, None, tr, cols), lambda i, p: (p[0], p[1], i, 0)),
        ),
        out_shape=jax.ShapeDtypeStruct(gathered.shape, gathered.dtype),
        input_output_aliases={2: 0},
        compiler_params=_params(1),
        name=name,
    )(pos, shard, gathered)


def _chip_exchange(ps, name):
    n = len(ps)

    def body(*refs):
        ins, outs = refs[:n], refs[n : 2 * n]
        send_sems, recv_sems = refs[2 * n :]
        x, y, c, chips = _position()
        sends = []
        for k, (px, py) in enumerate(chips):
            for i in range(n):
                sends.append(_remote(ins[i].at[2 * px + py], outs[i].at[k], send_sems, recv_sems, k * n + i, (px, py, c)))
                sends[-1].start()
        for cp in sends:
            cp.wait_recv()
        for cp in sends:
            cp.wait_send()

    return _call(
        body,
        in_specs=[_ANY] * n,
        out_specs=[_ANY] * n,
        out_shape=[jax.ShapeDtypeStruct((3,) + p.shape[1:], p.dtype) for p in ps],
        scratch_shapes=[pltpu.SemaphoreType.DMA((3 * n,)), pltpu.SemaphoreType.DMA((3 * n,))],
        name=name,
    )(*ps)


def _pair_share(fs, name):
    n = len(fs)

    def body(*refs):
        ins, outs = refs[:n], refs[n : 2 * n]
        send_sems, recv_sems = refs[2 * n :]
        x, y, c, _ = _position()
        sib = (x, y, 1 - c)
        sends = [_remote(ins[i].at[c], outs[i].at[c], send_sems, recv_sems, i, sib) for i in range(n)]
        for cp in sends:
            cp.start()
        for i in range(n):
            _remote(ins[i].at[1 - c], outs[i].at[1 - c], send_sems, recv_sems, i, sib).wait_recv()
        for cp in sends:
            cp.wait_send()

    return _call(
        body,
        in_specs=[_ANY] * n,
        out_specs=[_ANY] * n,
        out_shape=[jax.ShapeDtypeStruct(f.shape, f.dtype) for f in fs],
        input_output_aliases={i: i for i in range(n)},
        scratch_shapes=[pltpu.SemaphoreType.DMA((n,)), pltpu.SemaphoreType.DMA((n,))],
        name=name,
    )(*fs)


def _pair_sum(g, r, core, name):
    nj, _, hr, cols = g.shape
    tr = _row_tile(hr, 256)

    def body(c_ref, g_ref, r_ref, o_ref):
        del c_ref
        o_ref[...] = (g_ref[...].astype(F32) + r_ref[...].astype(F32)).astype(o_ref.dtype)

    return _call(
        body,
        grid_spec=pltpu.PrefetchScalarGridSpec(
            num_scalar_prefetch=1,
            grid=(nj, hr // tr),
            in_specs=[
                pl.BlockSpec((None, None, tr, cols), lambda j, i, c: (j, c[0], i, 0)),
                pl.BlockSpec((None, tr, cols), lambda j, i, c: (j, i, 0)),
            ],
            out_specs=pl.BlockSpec((None, tr, cols), lambda j, i, c: (j, i, 0)),
        ),
        out_shape=jax.ShapeDtypeStruct((nj, hr, cols), g.dtype),
        compiler_params=_params(2),
        name=name,
    )(core, g, r)


def _chip_sum(p, q, pos, name):
    nq, hr, cols = q.shape
    tr = _row_tile(hr, 256)

    def body(pos_ref, p_ref, *refs):
        del pos_ref
        o_ref = refs[nq]
        tot = p_ref[...].astype(F32)
        for k in range(nq):
            tot = tot + refs[k][...].astype(F32)
        o_ref[...] = tot

    return _call(
        body,
        grid_spec=pltpu.PrefetchScalarGridSpec(
            num_scalar_prefetch=1,
            grid=(hr // tr,),
            in_specs=[pl.BlockSpec((None, tr, cols), lambda i, s: (s[0], i, 0))]
            + [pl.BlockSpec((None, tr, cols), functools.partial(lambda k, i, s: (k, i, 0), k)) for k in range(nq)],
            out_specs=pl.BlockSpec((None, tr, cols), lambda i, s: (s[1], i, 0)),
        ),
        out_shape=jax.ShapeDtypeStruct((2, hr, cols), F32),
        compiler_params=_params(1),
        name=name,
    )(pos, p, *([q] * nq))


def _adamw(w, g, m, v, name):
    rows, cols = w.shape
    tr = _row_tile(rows, max(8, (2**18) // cols), mult=8)
    c1 = 1.0 - ADAM_B1**ADAM_STEP
    c2 = 1.0 - ADAM_B2**ADAM_STEP

    def body(w_ref, g_ref, m_ref, v_ref, d_ref, nm_ref, nv_ref):
        gv = g_ref[...]
        nm = ADAM_B1 * m_ref[...] + (1.0 - ADAM_B1) * gv
        nv = ADAM_B2 * v_ref[...] + (1.0 - ADAM_B2) * (gv * gv)
        nm_ref[...] = nm
        nv_ref[...] = nv
        d_ref[...] = -ADAM_LR * ((nm / c1) / (jnp.sqrt(nv / c2) + ADAM_EPS) + ADAM_WD * w_ref[...])

    spec = pl.BlockSpec((tr, cols), lambda i: (i, 0))
    sds = jax.ShapeDtypeStruct((rows, cols), F32)
    return _call(
        body,
        grid=(rows // tr,),
        in_specs=[spec] * 4,
        out_specs=[spec] * 3,
        out_shape=[sds] * 3,
        compiler_params=_params(1),
        name=name,
    )(w, g, m, v)


_TILE_ELEMS = 8 * LANES
_FLAT_ROW_MULT = 8 * 2 * N_CHIPS


def _flat_rows(shape):
    n = math.prod(shape)
    return (n + _TILE_ELEMS - 1) // _TILE_ELEMS * 8


def _pack(arrs):
    parts = []
    for a in arrs:
        rows = _flat_rows(a.shape)
        flat = a.reshape(-1).astype(F32)
        flat = jnp.pad(flat, (0, rows * LANES - flat.shape[0]))
        parts.append(flat.reshape(rows, LANES))
    total = sum(p.shape[0] for p in parts)
    pad = -total % _FLAT_ROW_MULT
    if pad:
        parts.append(jnp.zeros((pad, LANES), F32))
    return jnp.concatenate(parts, axis=0)


def _unpack(flat, shapes):
    out, row = [], 0
    for shp in shapes:
        rows = _flat_rows(shp)
        out.append(flat[row : row + rows].reshape(-1)[: math.prod(shp)].reshape(shp))
        row += rows
    return out


def kernel(x, norm_mix_g, w_in, ssm_a_re, ssm_a_im, ssm_b_re, ssm_b_im, ssm_c_re, ssm_c_im, ssm_d, ssm_log_dt, ssm_glu_w, ssm_glu_b, sgu_ln_g, sgu_ln_b, sgu_w, sgu_b, out_norm_ssm_g, out_norm_sgu_g, w_out, norm_mlp_g, w_up, w_down, norm_final_g, loss_target, m_norm_mix_g, m_w_in, m_ssm_a_re, m_ssm_a_im, m_ssm_b_re, m_ssm_b_im, m_ssm_c_re, m_ssm_c_im, m_ssm_d, m_ssm_log_dt, m_ssm_glu_w, m_ssm_glu_b, m_sgu_ln_g, m_sgu_ln_b, m_sgu_w, m_sgu_b, m_out_norm_ssm_g, m_out_norm_sgu_g, m_w_out, m_norm_mlp_g, m_w_up, m_w_down, m_norm_final_g, v_norm_mix_g, v_w_in, v_ssm_a_re, v_ssm_a_im, v_ssm_b_re, v_ssm_b_im, v_ssm_c_re, v_ssm_c_im, v_ssm_d, v_ssm_log_dt, v_ssm_glu_w, v_ssm_glu_b, v_sgu_ln_g, v_sgu_ln_b, v_sgu_w, v_sgu_b, v_out_norm_ssm_g, v_out_norm_sgu_g, v_w_out, v_norm_mlp_g, v_w_up, v_w_down, v_norm_final_g):
    weights = dict(norm_mix_g=norm_mix_g, w_in=w_in, ssm_a_re=ssm_a_re, ssm_a_im=ssm_a_im, ssm_b_re=ssm_b_re, ssm_b_im=ssm_b_im, ssm_c_re=ssm_c_re, ssm_c_im=ssm_c_im, ssm_d=ssm_d, ssm_log_dt=ssm_log_dt, ssm_glu_w=ssm_glu_w, ssm_glu_b=ssm_glu_b, sgu_ln_g=sgu_ln_g, sgu_ln_b=sgu_ln_b, sgu_w=sgu_w, sgu_b=sgu_b, out_norm_ssm_g=out_norm_ssm_g, out_norm_sgu_g=out_norm_sgu_g, w_out=w_out, norm_mlp_g=norm_mlp_g, w_up=w_up, w_down=w_down, norm_final_g=norm_final_g)
    mom_m = dict(norm_mix_g=m_norm_mix_g, w_in=m_w_in, ssm_a_re=m_ssm_a_re, ssm_a_im=m_ssm_a_im, ssm_b_re=m_ssm_b_re, ssm_b_im=m_ssm_b_im, ssm_c_re=m_ssm_c_re, ssm_c_im=m_ssm_c_im, ssm_d=m_ssm_d, ssm_log_dt=m_ssm_log_dt, ssm_glu_w=m_ssm_glu_w, ssm_glu_b=m_ssm_glu_b, sgu_ln_g=m_sgu_ln_g, sgu_ln_b=m_sgu_ln_b, sgu_w=m_sgu_w, sgu_b=m_sgu_b, out_norm_ssm_g=m_out_norm_ssm_g, out_norm_sgu_g=m_out_norm_sgu_g, w_out=m_w_out, norm_mlp_g=m_norm_mlp_g, w_up=m_w_up, w_down=m_w_down, norm_final_g=m_norm_final_g)
    mom_v = dict(norm_mix_g=v_norm_mix_g, w_in=v_w_in, ssm_a_re=v_ssm_a_re, ssm_a_im=v_ssm_a_im, ssm_b_re=v_ssm_b_re, ssm_b_im=v_ssm_b_im, ssm_c_re=v_ssm_c_re, ssm_c_im=v_ssm_c_im, ssm_d=v_ssm_d, ssm_log_dt=v_ssm_log_dt, ssm_glu_w=v_ssm_glu_w, ssm_glu_b=v_ssm_glu_b, sgu_ln_g=v_sgu_ln_g, sgu_ln_b=v_sgu_ln_b, sgu_w=v_sgu_w, sgu_b=v_sgu_b, out_norm_ssm_g=v_out_norm_ssm_g, out_norm_sgu_g=v_out_norm_sgu_g, w_out=v_w_out, norm_mlp_g=v_norm_mlp_g, w_up=v_w_up, w_down=v_w_down, norm_final_g=v_norm_final_g)
    names = list(weights)
    large = ["w_in", "ssm_glu_w", "w_out", "w_up", "w_down"]
    small = [n for n in names if n not in large]

    s, d = x.shape[1], x.shape[2]
    xs = x.reshape(s, d)
    target = loss_target.reshape(s, d)
    width = ssm_glu_w.shape[-1]
    ff = w_down.shape[1] * N_CHIPS
    tm = min(512, s)
    core = lax.axis_index("c").astype(jnp.int32).reshape(1)
    chip = (2 * lax.axis_index("x") + lax.axis_index("y")).astype(jnp.int32).reshape(1)
    pos = jnp.concatenate([chip, core])

    shards = [w[0].astype(BF16).reshape(2, w.shape[1] // 2, w.shape[2]) for w in (w_in, ssm_glu_w, w_out, w_up, w_down)]
    gathered = _gather_halves(shards, "gather_weights")
    gathered = [_place_own(sh, g, pos, "place_" + n) for sh, g, n in zip(shards, gathered, large)]
    w_in4 = gathered[0].reshape(N_CHIPS, d, w_in.shape[2])
    wg_full = gathered[1].reshape(width, width)
    w_out_full = gathered[2].reshape(d, d)
    up4 = gathered[3].reshape(N_CHIPS, d, w_up.shape[2])
    w_down_full = gathered[4].reshape(ff, d)

    ssm_args = (ssm_a_re[0], ssm_a_im[0], ssm_b_re[0], ssm_b_im[0], ssm_c_re[0], ssm_c_im[0], ssm_d[0], ssm_log_dt[0])
    (k2c, w2c, v2c, al), ssm_vjp = jax.vjp(_ssm_mats, *ssm_args)
    consts = _spread_consts(ssm_b_re.shape[-1], ssm_a_re.shape[-1])
    k2b, w2b, v2b = _ssm_spread(k2c, w2c, v2c, consts)
    causal = jnp.tril(jnp.ones((SGU_CHUNK, SGU_CHUNK), dtype=bool))
    wt = jnp.where(causal[None], sgu_w[0], 0.0)
    wtb = wt.astype(BF16)
    wttb = jnp.swapaxes(wt, 1, 2).astype(BF16)
    heads = sgu_w.shape[1]
    biasb = jnp.broadcast_to(sgu_b[0][:, :, None], (heads, SGU_CHUNK, LANES))

    z, h1b = _in_proj(xs, norm_mix_g, w_in4, tm)
    y_pre, xprev, us = _ssm_fwd(z, k2b, w2b, v2b, al)
    mixed = _glu_fwd(y_pre, wg_full, ssm_glu_b, out_norm_ssm_g, d, tm)
    mixed = _sgu_fwd(z, mixed, sgu_ln_g, sgu_ln_b, wtb, biasb, out_norm_sgu_g, tm)
    x1 = _out_proj(xs, mixed, w_out_full, tm)
    x2, up, h2b = _mlp_fwd(x1, norm_mlp_g, up4, w_down_full, tm, min(1024, up4.shape[-1]))
    loss_part, dx2, dx2b, d_norm_final = _final(x2, target, norm_final_g.reshape(1, d), tm)

    dup, dh2 = _mlp_bwd(dx2b, up, up4, w_down_full, tm, min(1024, up4.shape[-1]))
    dx1, dx1b, d_norm_mlp = _norm_bwd(dh2, x1, dx2, norm_mlp_g, min(256, s), "norm_mlp_bwd")
    hr_big = d // 2
    tkk = min(512, s)
    g_down = _grad_w(
        up, dx2b, (N_CHIPS, 2, hr_big, d),
        lambda i, j: (i // (2 * (hr_big // min(1024, hr_big))), (i // (hr_big // min(1024, hr_big))) % 2,
                      i % (hr_big // min(1024, hr_big)), j),
        min(1024, hr_big), d, tkk, True, "grad_w_down")
    tn_up = min(2048, up4.shape[-1])
    per_up = up4.shape[-1] // tn_up
    g_up = _grad_w(
        h2b, dup, (N_CHIPS, 2, hr_big, up4.shape[-1]),
        lambda i, j: (j // per_up, i // (hr_big // min(1024, hr_big)), i % (hr_big // min(1024, hr_big)), j % per_up),
        min(1024, hr_big), tn_up, tkk, False, "grad_w_up")
    dmix = _out_proj_bwd(dx1b, w_out_full, tm)
    hr_out = d // (2 * N_CHIPS)
    g_out = _grad_w(
        mixed, dx1b, (N_CHIPS, 2, hr_out, d),
        lambda i, j: (i // 2, i % 2, 0, j), hr_out, d, tkk, False, "grad_w_out")
    dy_pre, g_glu, d_glu_b, d_norm_ssm = _glu_bwd(y_pre, dmix, wg_full, ssm_glu_b, out_norm_ssm_g, tm)
    dys, gst, d_al = _ssm_bwd_state(dy_pre, v2b, al, xprev)
    dz, d_k2, d_w2, d_v2 = _ssm_bwd_main(us, dys, k2b, w2b, xprev, gst, consts, s, z.shape[1])
    dz, d_wt, d_bias, d_ln_g, d_ln_b, d_norm_sgu = _sgu_bwd(
        z, dmix, dz, sgu_ln_g, sgu_ln_b, wtb, wttb, biasb, out_norm_sgu_g, tm)
    dh1 = _in_proj_bwd(dz, w_in4, tm)
    grad_x, _, d_norm_mix = _norm_bwd(dh1, xs, dx1, norm_mix_g, min(256, s), "norm_mix_bwd")
    cw_in = w_in4.shape[-1]
    g_in = _grad_w(
        h1b, dz, (N_CHIPS, 2, hr_big, cw_in),
        lambda i, j: (j, i // (hr_big // min(1024, hr_big)), i % (hr_big // min(1024, hr_big)), 0),
        min(1024, hr_big), cw_in, tkk, False, "grad_w_in")

    d_ssm = ssm_vjp((d_k2, d_w2, d_v2, d_al))
    small_grads = dict(
        norm_mix_g=d_norm_mix, ssm_a_re=d_ssm[0], ssm_a_im=d_ssm[1], ssm_b_re=d_ssm[2], ssm_b_im=d_ssm[3],
        ssm_c_re=d_ssm[4], ssm_c_im=d_ssm[5], ssm_d=d_ssm[6], ssm_log_dt=d_ssm[7], ssm_glu_b=d_glu_b,
        sgu_ln_g=d_ln_g, sgu_ln_b=d_ln_b, sgu_w=jnp.where(causal[None], d_wt, 0.0), sgu_b=d_bias[:, :, 0],
        out_norm_ssm_g=d_norm_ssm, out_norm_sgu_g=d_norm_sgu, norm_mlp_g=d_norm_mlp, norm_final_g=d_norm_final)
    flat = _pack([small_grads[n] for n in small])
    hr_small = flat.shape[0] // (2 * N_CHIPS)
    g_small = flat.reshape(N_CHIPS, 2, hr_small, LANES)

    hr_glu = width // (2 * N_CHIPS)
    grads = [g_in, g_glu.reshape(N_CHIPS, 2, hr_glu, width), g_out, g_up, g_down, g_small]
    tags = ["w_in", "glu_w", "w_out", "w_up", "w_down", "small"]
    from_sib = _pair_exchange(grads, "grads_to_sibling")
    pair = [_pair_sum(g, r, core, "pair_sum_" + t) for g, r, t in zip(grads, from_sib, tags)]
    from_chips = _chip_exchange(pair, "grads_to_owner")
    halves = [_chip_sum(p, q, pos, "chip_sum_" + t) for p, q, t in zip(pair, from_chips, tags)]
    owned = _pair_share(halves, "grads_to_both_cores")
    (small_all,) = _gather_chips([owned[5]], "gather_small_grads")
    small_flat = small_all.reshape(flat.shape)

    grad_out, delta_out, m_out, v_out = {}, {}, {}, {}
    for n, g in zip(large, owned[:5]):
        shp = weights[n].shape
        g2 = g.reshape(shp[1], shp[2])
        dl, nm, nv = _adamw(weights[n][0], g2, mom_m[n][0], mom_v[n][0], "adamw_" + n)
        grad_out[n], delta_out[n], m_out[n], v_out[n] = g2.reshape(shp), dl.reshape(shp), nm.reshape(shp), nv.reshape(shp)
    shapes = [weights[n].shape for n in small]
    dl, nm, nv = _adamw(
        _pack([weights[n] for n in small]), small_flat, _pack([mom_m[n] for n in small]),
        _pack([mom_v[n] for n in small]), "adamw_small")
    for n, g, a, b, c in zip(small, _unpack(small_flat, shapes), _unpack(dl, shapes), _unpack(nm, shapes), _unpack(nv, shapes)):
        grad_out[n], delta_out[n], m_out[n], v_out[n] = g, a, b, c

    loss = lax.psum(loss_part[0, 0], ("x", "y", "c"))
    return (loss, grad_x.reshape(x.shape), *[grad_out[n] for n in names], *[delta_out[n] for n in names],
            *[m_out[n] for n in names], *[v_out[n] for n in names])
```

```python
import functools
import math

import numpy as np
import jax
import jax.numpy as jnp
from jax import lax
from jax.experimental import pallas as pl
from jax.experimental.pallas import tpu as pltpu

F32 = jnp.float32
BF16 = jnp.bfloat16
MESH = pl.DeviceIdType.MESH
HIGHEST = lax.Precision.HIGHEST

EPS = 1e-6
ADAM_LR = 0.001
ADAM_B1 = 0.9
ADAM_B2 = 0.999
ADAM_EPS = 1e-08
ADAM_WD = 0.01
ADAM_STEP = 10

N_CHIPS = 4
LANES = 128
SSM_GROUP = 16
SSM_STATE = 64
GROUPS_PER_TILE = LANES // SSM_GROUP
CHUNK = 16
PAIRS = CHUNK // 2
SGU_CHUNK = 128
VMEM_LIMIT = 56 * 2**20


def _params(n_axes, vmem=VMEM_LIMIT):
    return pltpu.CompilerParams(dimension_semantics=("arbitrary",) * n_axes, vmem_limit_bytes=vmem)


def _call(body, **kw):
    return pl.pallas_call(body, **kw)


def _dot(a, b):
    return jnp.dot(a, b, preferred_element_type=F32)


def _dot_nt(a, b):
    return lax.dot_general(a, b, (((1,), (1,)), ((), ())), preferred_element_type=F32)


def _dot_tn(a, b):
    return lax.dot_general(a, b, (((0,), (0,)), ((), ())), preferred_element_type=F32)


_GELU_K = math.sqrt(2.0 / math.pi)
_GELU_C = 0.044715


def _gelu(x):
    return 0.5 * x * (1.0 + jnp.tanh(_GELU_K * (x + _GELU_C * x * x * x)))


def _gelu_grad(x):
    t = jnp.tanh(_GELU_K * (x + _GELU_C * x * x * x))
    return 0.5 * (1.0 + t) + 0.5 * x * (1.0 - t * t) * (_GELU_K * (1.0 + 3.0 * _GELU_C * x * x))


def _sigmoid(x):
    return 1.0 / (1.0 + jnp.exp(-x))


def _rms(x):
    return lax.rsqrt(jnp.mean(x * x, axis=-1, keepdims=True) + EPS)


def _rms_bwd(dy, x, r, g):
    a = dy * g
    dx = r * a - x * (r * r * r) * jnp.mean(a * x, axis=-1, keepdims=True)
    return dx, dy * x * r


def _row_tile(rows, target, mult=16):
    for t in range(min(rows, target), 0, -1):
        if rows % t == 0 and t % mult == 0:
            return t
    return rows


def _ssm_mats(a_re, a_im, b_re, b_im, c_re, c_im, d, log_dt):
    g, p = a_re.shape
    h = b_re.shape[-1]
    nt = g // GROUPS_PER_TILE
    dt = jnp.exp(log_dt)[:, None]
    lr, li = a_re * dt, a_im * dt

    def apow(l):
        mag = jnp.exp(lr * l)
        return mag * jnp.cos(li * l), mag * jnp.sin(li * l)

    ar, ai = apow(1.0)
    den = a_re * a_re + a_im * a_im
    qr = ((ar - 1.0) * a_re + ai * a_im) / den
    qi = (ai * a_re - (ar - 1.0) * a_im) / den
    bt_re, bt_im = jnp.swapaxes(b_re, 1, 2), jnp.swapaxes(b_im, 1, 2)
    bbr = qr[:, None, :] * bt_re - qi[:, None, :] * bt_im
    bbi = qr[:, None, :] * bt_im + qi[:, None, :] * bt_re
    ls = jnp.arange(CHUNK + 1, dtype=F32)[:, None, None]
    pr, pi = apow(ls)
    tr = pr[:CHUNK, :, None, :] * bbr[None] - pi[:CHUNK, :, None, :] * bbi[None]
    ti = pr[:CHUNK, :, None, :] * bbi[None] + pi[:CHUNK, :, None, :] * bbr[None]
    k = jnp.einsum("gop,lgip->lgio", c_re, tr, precision=HIGHEST) - jnp.einsum(
        "gop,lgip->lgio", c_im, ti, precision=HIGHEST
    )
    k = k.at[0].add(d[:, :, None] * jnp.eye(h, dtype=F32))

    def tiles(a):
        a = a.reshape(PAIRS, 2, nt, GROUPS_PER_TILE, h, a.shape[-1])
        return jnp.transpose(a, (2, 0, 1, 3, 4, 5)).reshape(nt, PAIRS, 2 * GROUPS_PER_TILE * h, a.shape[-1])

    zero = jnp.zeros_like(k[:1])
    kz = jnp.concatenate([zero, zero, k], axis=0).reshape(PAIRS + 1, 2, g, h, h)
    even, odd = kz[:, 0], kz[:, 1]
    fill = jnp.zeros((PAIRS, g, h, LANES - 2 * h), F32)
    row0 = jnp.concatenate([even[1:], odd[1:], fill], axis=-1)
    row1 = jnp.concatenate([odd[:-1], even[1:], fill], axis=-1)
    k2c = tiles(jnp.stack([row0, row1], axis=1).reshape(2 * PAIRS, g, h, LANES))
    w2c = tiles(jnp.concatenate([tr[::-1], ti[::-1]], axis=-1))
    car = c_re[None] * pr[1:, :, None, :] - c_im[None] * pi[1:, :, None, :]
    cai = c_re[None] * pi[1:, :, None, :] + c_im[None] * pr[1:, :, None, :]
    v2c = tiles(jnp.concatenate([car, -cai], axis=-1))
    al = jnp.stack([pr[CHUNK].reshape(nt, -1), pi[CHUNK].reshape(nt, -1)], axis=1)
    return k2c, w2c, v2c, al


def _spread_consts(h, p):
    gg = GROUPS_PER_TILE
    row_g = (np.arange(2 * gg * h) // h) % gg
    colk = np.arange(2 * gg * h)
    rep_k = np.zeros((LANES, 2 * gg * h), np.float32)
    rep_k[(colk // (gg * h)) * h + colk % h, colk] = 1.0
    mask_k = (row_g[:, None] == ((colk // h) % gg)[None, :]).astype(np.float32)
    cols = np.arange(2 * gg * p)
    rep_s = np.zeros((2 * p, 2 * gg * p), np.float32)
    rep_s[(cols // (gg * p)) * p + cols % p, cols] = 1.0
    mask_s = (row_g[:, None] == ((cols // p) % gg)[None, :]).astype(np.float32)
    return tuple(jnp.asarray(a, BF16) for a in (rep_k, mask_k, rep_s, mask_s))


def _ssm_spread(k2c, w2c, v2c, consts):
    nt = k2c.shape[0]
    rep_k, mask_k, rep_s, mask_s = consts
    nk, ns = rep_k.shape[1], rep_s.shape[1]

    def body(k_ref, w_ref, v_ref, rk_ref, mk_ref, rs_ref, ms_ref, ko_ref, wo_ref, vo_ref):
        for q in range(PAIRS):
            ko_ref[q] = (_dot(k_ref[q].astype(BF16), rk_ref[...]) * mk_ref[...]).astype(BF16)
            wo_ref[q] = (_dot(w_ref[q].astype(BF16), rs_ref[...]) * ms_ref[...]).astype(BF16)
            vo_ref[q] = (_dot(v_ref[q].astype(BF16), rs_ref[...]) * ms_ref[...]).astype(BF16)

    def tile(cols):
        return pl.BlockSpec((None, PAIRS, nk, cols), lambda t: (t, 0, 0, 0))

    def whole(a):
        return pl.BlockSpec(a.shape, lambda t: (0, 0))

    return _call(
        body,
        grid=(nt,),
        in_specs=[tile(LANES), tile(LANES), tile(LANES), whole(rep_k), whole(mask_k), whole(rep_s), whole(mask_s)],
        out_specs=[tile(nk), tile(ns), tile(ns)],
        out_shape=[
            jax.ShapeDtypeStruct((nt, PAIRS, nk, nk), BF16),
            jax.ShapeDtypeStruct((nt, PAIRS, nk, ns), BF16),
            jax.ShapeDtypeStruct((nt, PAIRS, nk, ns), BF16),
        ],
        compiler_params=_params(1),
        name="ssm_spread",
    )(k2c, w2c, v2c, rep_k, mask_k, rep_s, mask_s)


def _gather_blocks(full, mask, rep):
    m = full * mask
    hi = m.astype(BF16)
    lo = (m - hi.astype(F32)).astype(BF16)
    return _dot_nt(hi, rep) + _dot_nt(lo, rep)


def _in_proj(x, g, w4, tm, ex):
    s, d = x.shape
    nj, _, cw = w4.shape

    def body(x_ref, g_ref, w_ref, z_ref, h_ref):
        @pl.when(pl.program_id(1) == 0)
        def _():
            xv = x_ref[...]
            h_ref[...] = (xv * _rms(xv) * g_ref[...]).astype(BF16)

        z_ref[...] = _dot(h_ref[...], w_ref[...])

    return _call_riding(
        body,
        ex,
        scratch_shapes=[],
        args=(x, g, w4),
        grid=(s // tm, nj),
        in_specs=[
            pl.BlockSpec((tm, d), lambda i, j: (i, 0)),
            pl.BlockSpec((1, d), lambda i, j: (0, 0)),
            pl.BlockSpec((None, d, cw), lambda i, j: (j, 0, 0)),
        ],
        out_specs=[pl.BlockSpec((tm, cw), lambda i, j: (i, j)), pl.BlockSpec((tm, d), lambda i, j: (i, 0))],
        out_shape=[jax.ShapeDtypeStruct((s, nj * cw), F32), jax.ShapeDtypeStruct((s, d), BF16)],
        compiler_params=_params(2),
        name="in_proj",
    )


def _ssm_fwd(z, k2, w2, v2, al):
    s = z.shape[0]
    nt = k2.shape[0]
    nc = s // CHUNK
    ns = w2.shape[-1]
    hs = ns // 2

    def body(u_ref, k_ref, w_ref, v_ref, al_ref, y_ref, xp_ref, us_ref, xloc):
        for q in range(PAIRS):
            us_ref[q, :, 0:LANES] = u_ref[pl.ds(2 * q, nc, stride=CHUNK), :].astype(BF16)
            us_ref[q, :, LANES : 2 * LANES] = u_ref[pl.ds(2 * q + 1, nc, stride=CHUNK), :].astype(BF16)
        acc = _dot(us_ref[0], w_ref[0])
        for q in range(1, PAIRS):
            acc = acc + _dot(us_ref[q], w_ref[q])
        xloc[...] = acc
        ar = al_ref[0:1, :]
        ai = al_ref[1:2, :]

        def step(c, carry):
            xr, xi = carry
            xp_ref[pl.ds(c, 1), 0:hs] = xr
            xp_ref[pl.ds(c, 1), hs:ns] = xi
            lr = xloc[pl.ds(c, 1), 0:hs]
            li = xloc[pl.ds(c, 1), hs:ns]
            return ar * xr - ai * xi + lr, ar * xi + ai * xr + li

        zero = jnp.zeros((1, hs), F32)
        lax.fori_loop(0, nc, step, (zero, zero))
        xpb = xp_ref[...].astype(BF16)
        for r in range(PAIRS):
            acc = _dot_nt(xpb, v_ref[r])
            for q in range(r + 1):
                acc = acc + _dot(us_ref[q], k_ref[r - q])
            y_ref[pl.ds(2 * r, nc, stride=CHUNK), :] = acc[:, 0:LANES]
            y_ref[pl.ds(2 * r + 1, nc, stride=CHUNK), :] = acc[:, LANES : 2 * LANES]

    return _call(
        body,
        grid=(nt,),
        in_specs=[
            pl.BlockSpec((s, LANES), lambda t: (0, t)),
            pl.BlockSpec((None, PAIRS, 2 * LANES, 2 * LANES), lambda t: (t, 0, 0, 0)),
            pl.BlockSpec((None, PAIRS, 2 * LANES, ns), lambda t: (t, 0, 0, 0)),
            pl.BlockSpec((None, PAIRS, 2 * LANES, ns), lambda t: (t, 0, 0, 0)),
            pl.BlockSpec((None, 2, hs), lambda t: (t, 0, 0)),
        ],
        out_specs=[
            pl.BlockSpec((s, LANES), lambda t: (0, t)),
            pl.BlockSpec((None, nc, ns), lambda t: (t, 0, 0)),
            pl.BlockSpec((None, PAIRS, nc, 2 * LANES), lambda t: (t, 0, 0, 0)),
        ],
        out_shape=[
            jax.ShapeDtypeStruct((s, nt * LANES), F32),
            jax.ShapeDtypeStruct((nt, nc, ns), F32),
            jax.ShapeDtypeStruct((nt, PAIRS, nc, 2 * LANES), BF16),
        ],
        scratch_shapes=[pltpu.VMEM((nc, ns), F32)],
        compiler_params=_params(1),
        name="ssm_fwd",
    )(z, k2, w2, v2, al)


def _glu_fwd(y_pre, wg, bg, go, d_model, tm):
    s, w = y_pre.shape

    def body(y_ref, wg_ref, bg_ref, go_ref, o_ref):
        yg = _gelu(y_ref[...])
        gate = _sigmoid(_dot(yg.astype(BF16), wg_ref[...]) + bg_ref[...])
        ys = yg * gate
        o_ref[...] = (ys * _rms(ys) * go_ref[...]).astype(BF16)

    return _call(
        body,
        grid=(s // tm,),
        in_specs=[
            pl.BlockSpec((tm, w), lambda i: (i, 0)),
            pl.BlockSpec((w, w), lambda i: (0, 0)),
            pl.BlockSpec((1, w), lambda i: (0, 0)),
            pl.BlockSpec((1, w), lambda i: (0, 0)),
        ],
        out_specs=pl.BlockSpec((tm, w), lambda i: (i, 0)),
        out_shape=jax.ShapeDtypeStruct((s, d_model), BF16),
        compiler_params=_params(1),
        name="glu_fwd",
    )(y_pre, wg, bg, go)


def _sgu_parts(zu, zv, lng, lnb, wt_ref, bias_ref):
    u = _gelu(zu)
    v = _gelu(zv)
    mu = jnp.mean(v, axis=-1, keepdims=True)
    vc = v - mu
    rstd = lax.rsqrt(jnp.mean(vc * vc, axis=-1, keepdims=True) + EPS)
    vhat = vc * rstd
    vb = (vhat * lng + lnb).astype(BF16)
    heads = wt_ref.shape[0]
    mix = jnp.concatenate(
        [_dot(wt_ref[h], vb[:, h * LANES : (h + 1) * LANES]) + bias_ref[h] for h in range(heads)], axis=1
    )
    return u, vhat, rstd, vb, mix


def _sgu_fwd(z, mixed, lng, lnb, wt, biasb, go, rb):
    s = z.shape[0]
    w = lng.shape[-1]
    heads = wt.shape[0]

    def body(zu_ref, zv_ref, m_any, lng_ref, lnb_ref, wt_ref, b_ref, go_ref, o_ref):
        del m_any
        for ck in range(rb // SGU_CHUNK):
            rows = slice(ck * SGU_CHUNK, (ck + 1) * SGU_CHUNK)
            u, _, _, _, mix = _sgu_parts(zu_ref[rows, :], zv_ref[rows, :], lng_ref[...], lnb_ref[...], wt_ref, b_ref)
            y = u * mix
            o_ref[rows, :] = (y * _rms(y) * go_ref[...]).astype(BF16)

    return _call(
        body,
        grid=(s // rb,),
        in_specs=[
            pl.BlockSpec((rb, w), lambda i: (i, 1)),
            pl.BlockSpec((rb, w), lambda i: (i, 2)),
            pl.BlockSpec(memory_space=pl.ANY),
            pl.BlockSpec((1, w), lambda i: (0, 0)),
            pl.BlockSpec((1, w), lambda i: (0, 0)),
            pl.BlockSpec((heads, SGU_CHUNK, SGU_CHUNK), lambda i: (0, 0, 0)),
            pl.BlockSpec((heads, SGU_CHUNK, LANES), lambda i: (0, 0, 0)),
            pl.BlockSpec((1, w), lambda i: (0, 0)),
        ],
        out_specs=pl.BlockSpec((rb, w), lambda i: (i, 1)),
        out_shape=jax.ShapeDtypeStruct(mixed.shape, mixed.dtype),
        input_output_aliases={2: 0},
        compiler_params=_params(1),
        name="sgu_fwd",
    )(z, z, mixed, lng, lnb, wt, biasb, go)


def _out_proj(x, mixed, w_out, tm):
    s, d = x.shape

    def body(x_ref, m_ref, w_ref, o_ref):
        o_ref[...] = x_ref[...] + _dot(m_ref[...], w_ref[...])

    return _call(
        body,
        grid=(s // tm,),
        in_specs=[
            pl.BlockSpec((tm, d), lambda i: (i, 0)),
            pl.BlockSpec((tm, d), lambda i: (i, 0)),
            pl.BlockSpec((d, d), lambda i: (0, 0)),
        ],
        out_specs=pl.BlockSpec((tm, d), lambda i: (i, 0)),
        out_shape=jax.ShapeDtypeStruct((s, d), F32),
        compiler_params=_params(1),
        name="out_proj",
    )(x, mixed, w_out)


def _mlp_up(x1, g, w_up4, tm, tf, ex):
    s, d = x1.shape
    nj, _, cw = w_up4.shape
    per = cw // tf

    def body(x_ref, g_ref, wu_ref, up_ref, h_ref):
        @pl.when(pl.program_id(1) == 0)
        def _():
            xv = x_ref[...]
            h_ref[...] = (xv * _rms(xv) * g_ref[...]).astype(BF16)

        up_ref[...] = _dot(h_ref[...], wu_ref[...]).astype(BF16)

    return _call_riding(
        body,
        ex,
        scratch_shapes=[],
        args=(x1, g, w_up4),
        grid=(s // tm, nj * per),
        in_specs=[
            pl.BlockSpec((tm, d), lambda i, f: (i, 0)),
            pl.BlockSpec((1, d), lambda i, f: (0, 0)),
            pl.BlockSpec((None, d, tf), lambda i, f: (f // per, 0, f % per)),
        ],
        out_specs=[pl.BlockSpec((tm, tf), lambda i, f: (i, f)), pl.BlockSpec((tm, d), lambda i, f: (i, 0))],
        out_shape=[jax.ShapeDtypeStruct((s, nj * cw), BF16), jax.ShapeDtypeStruct((s, d), BF16)],
        compiler_params=_params(2),
        name="mlp_up",
    )


def _mlp_down(x1, up, w_down, tm, tf):
    s, d = x1.shape
    ff = w_down.shape[0]

    def body(x_ref, up_ref, wd_ref, x2_ref):
        @pl.when(pl.program_id(1) == 0)
        def _():
            x2_ref[...] = x_ref[...]

        a = jnp.maximum(up_ref[...].astype(F32), 0.0)
        x2_ref[...] += _dot((a * a).astype(BF16), wd_ref[...])

    return _call(
        body,
        grid=(s // tm, ff // tf),
        in_specs=[
            pl.BlockSpec((tm, d), lambda i, f: (i, 0)),
            pl.BlockSpec((tm, tf), lambda i, f: (i, f)),
            pl.BlockSpec((tf, d), lambda i, f: (f, 0)),
        ],
        out_specs=pl.BlockSpec((tm, d), lambda i, f: (i, 0)),
        out_shape=jax.ShapeDtypeStruct((s, d), F32),
        compiler_params=_params(2),
        name="mlp_down",
    )(x1, up, w_down)


def _final(x2, target, g, tm):
    s, d = x2.shape

    def body(x_ref, t_ref, g_ref, loss_ref, dx_ref, dxb_ref, dg_ref):
        @pl.when(pl.program_id(0) == 0)
        def _():
            loss_ref[...] = jnp.zeros_like(loss_ref)
            dg_ref[...] = jnp.zeros_like(dg_ref)

        xv = x_ref[...]
        r = _rms(xv)
        gv = g_ref[...]
        diff = xv * r * gv - t_ref[...]
        loss_ref[...] += 0.5 * jnp.sum(jnp.mean(diff * diff, axis=-1, keepdims=True), axis=0, keepdims=True)
        dx, dgt = _rms_bwd(diff * (1.0 / d), xv, r, gv)
        dx_ref[...] = dx
        dxb_ref[...] = dx.astype(BF16)
        dg_ref[...] += jnp.sum(dgt, axis=0, keepdims=True)

    return _call(
        body,
        grid=(s // tm,),
        in_specs=[
            pl.BlockSpec((tm, d), lambda i: (i, 0)),
            pl.BlockSpec((tm, d), lambda i: (i, 0)),
            pl.BlockSpec((1, d), lambda i: (0, 0)),
        ],
        out_specs=[
            pl.BlockSpec((1, 1), lambda i: (0, 0)),
            pl.BlockSpec((tm, d), lambda i: (i, 0)),
            pl.BlockSpec((tm, d), lambda i: (i, 0)),
            pl.BlockSpec((1, d), lambda i: (0, 0)),
        ],
        out_shape=[
            jax.ShapeDtypeStruct((1, 1), F32),
            jax.ShapeDtypeStruct((s, d), F32),
            jax.ShapeDtypeStruct((s, d), BF16),
            jax.ShapeDtypeStruct((1, d), F32),
        ],
        compiler_params=_params(1),
        name="final_loss",
    )(x2, target, g)


def _mlp_bwd(dx2b, up, w_up4, w_down, tm, tf):
    s, d = dx2b.shape
    ff = w_down.shape[0]
    cw = w_up4.shape[-1]
    per = cw // tf

    def body(dx_ref, up_ref, wu_ref, wd_ref, dup_ref, dh_ref):
        @pl.when(pl.program_id(1) == 0)
        def _():
            dh_ref[...] = jnp.zeros_like(dh_ref)

        dact = _dot_nt(dx_ref[...], wd_ref[...])
        dupb = (dact * (2.0 * jnp.maximum(up_ref[...].astype(F32), 0.0))).astype(BF16)
        dup_ref[...] = dupb
        dh_ref[...] += _dot_nt(dupb, wu_ref[...])

    return _call(
        body,
        grid=(s // tm, ff // tf),
        in_specs=[
            pl.BlockSpec((tm, d), lambda i, f: (i, 0)),
            pl.BlockSpec((tm, tf), lambda i, f: (i, f)),
            pl.BlockSpec((None, d, tf), lambda i, f: (f // per, 0, f % per)),
            pl.BlockSpec((tf, d), lambda i, f: (f, 0)),
        ],
        out_specs=[pl.BlockSpec((tm, tf), lambda i, f: (i, f)), pl.BlockSpec((tm, d), lambda i, f: (i, 0))],
        out_shape=[jax.ShapeDtypeStruct((s, ff), BF16), jax.ShapeDtypeStruct((s, d), F32)],
        compiler_params=_params(2),
        name="mlp_bwd",
    )(dx2b, up, w_up4, w_down)


def _norm_bwd(dh, x, dres, g, tm, name):
    s, d = x.shape

    def body(dh_ref, x_ref, dr_ref, g_ref, dx_ref, dxb_ref, dg_ref):
        @pl.when(pl.program_id(0) == 0)
        def _():
            dg_ref[...] = jnp.zeros_like(dg_ref)

        xv = x_ref[...]
        dx, dgt = _rms_bwd(dh_ref[...], xv, _rms(xv), g_ref[...])
        tot = dr_ref[...] + dx
        dx_ref[...] = tot
        dxb_ref[...] = tot.astype(BF16)
        dg_ref[...] += jnp.sum(dgt, axis=0, keepdims=True)

    row = pl.BlockSpec((tm, d), lambda i: (i, 0))
    vec = pl.BlockSpec((1, d), lambda i: (0, 0))
    return _call(
        body,
        grid=(s // tm,),
        in_specs=[row, row, row, vec],
        out_specs=[row, row, vec],
        out_shape=[
            jax.ShapeDtypeStruct((s, d), F32),
            jax.ShapeDtypeStruct((s, d), BF16),
            jax.ShapeDtypeStruct((1, d), F32),
        ],
        compiler_params=_params(1),
        name=name,
    )(dh, x, dres, g)


def _grad_w(a, b, out_dims, index_map, tm, tn, tk, relu2, name, ex=None):
    t, m = a.shape
    n = b.shape[1]
    nk = t // tk

    def body(a_ref, b_ref, o_ref, acc):
        k = pl.program_id(2)

        @pl.when(k == 0)
        def _():
            acc[...] = jnp.zeros_like(acc)

        av = a_ref[...]
        if relu2:
            r = jnp.maximum(av.astype(F32), 0.0)
            av = (r * r).astype(BF16)
        acc[...] += _dot_tn(av, b_ref[...])

        @pl.when(k == nk - 1)
        def _():
            o_ref[...] = acc[...].astype(BF16)

    spec = dict(
        grid=(m // tm, n // tn, nk),
        in_specs=[pl.BlockSpec((tk, tm), lambda i, j, k: (k, i)), pl.BlockSpec((tk, tn), lambda i, j, k: (k, j))],
        out_specs=[pl.BlockSpec((None, None, tm, tn), lambda i, j, k: index_map(i, j))],
        out_shape=[jax.ShapeDtypeStruct(out_dims, BF16)],
        scratch_shapes=[pltpu.VMEM((tm, tn), F32)],
        compiler_params=_params(3),
        name=name,
    )
    if ex is None:
        return _call(body, **spec)(a, b)[0]
    (g,), (landed,) = _call_riding(body, ex, args=(a, b), **spec)
    return g, landed


def _out_proj_bwd(dx1b, w_out, tm):
    s, d = dx1b.shape

    def body(dx_ref, w_ref, o_ref):
        o_ref[...] = _dot_nt(dx_ref[...], w_ref[...])

    return _call(
        body,
        grid=(s // tm,),
        in_specs=[pl.BlockSpec((tm, d), lambda i: (i, 0)), pl.BlockSpec((d, d), lambda i: (0, 0))],
        out_specs=pl.BlockSpec((tm, d), lambda i: (i, 0)),
        out_shape=jax.ShapeDtypeStruct((s, d), F32),
        compiler_params=_params(1),
        name="out_proj_bwd",
    )(dx1b, w_out)


def _glu_bwd(y_pre, dmix, wg, bg, go, tm):
    s, w = y_pre.shape
    n = s // tm

    def body(y_ref, dm_ref, wg_ref, bg_ref, go_ref, dy_ref, dwg_ref, dbg_ref, dgo_ref, acc):
        i = pl.program_id(0)

        @pl.when(i == 0)
        def _():
            acc[...] = jnp.zeros_like(acc)
            dbg_ref[...] = jnp.zeros_like(dbg_ref)
            dgo_ref[...] = jnp.zeros_like(dgo_ref)

        yp = y_ref[...]
        yg = _gelu(yp)
        ygb = yg.astype(BF16)
        gate = _sigmoid(_dot(ygb, wg_ref[...]) + bg_ref[...])
        ys = yg * gate
        dys, dgt = _rms_bwd(dm_ref[...], ys, _rms(ys), go_ref[...])
        dgo_ref[...] += jnp.sum(dgt, axis=0, keepdims=True)
        dpre = dys * yg * gate * (1.0 - gate)
        dbg_ref[...] += jnp.sum(dpre, axis=0, keepdims=True)
        dpb = dpre.astype(BF16)
        dyg = dys * gate + _dot_nt(dpb, wg_ref[...])
        dy_ref[...] = dyg * _gelu_grad(yp)
        acc[...] += _dot_tn(ygb, dpb)

        @pl.when(i == n - 1)
        def _():
            dwg_ref[...] = acc[...].astype(BF16)

    return _call(
        body,
        grid=(n,),
        in_specs=[
            pl.BlockSpec((tm, w), lambda i: (i, 0)),
            pl.BlockSpec((tm, w), lambda i: (i, 0)),
            pl.BlockSpec((w, w), lambda i: (0, 0)),
            pl.BlockSpec((1, w), lambda i: (0, 0)),
            pl.BlockSpec((1, w), lambda i: (0, 0)),
        ],
        out_specs=[
            pl.BlockSpec((tm, w), lambda i: (i, 0)),
            pl.BlockSpec((w, w), lambda i: (0, 0)),
            pl.BlockSpec((1, w), lambda i: (0, 0)),
            pl.BlockSpec((1, w), lambda i: (0, 0)),
        ],
        out_shape=[
            jax.ShapeDtypeStruct((s, w), F32),
            jax.ShapeDtypeStruct((w, w), BF16),
            jax.ShapeDtypeStruct((1, w), F32),
            jax.ShapeDtypeStruct((1, w), F32),
        ],
        scratch_shapes=[pltpu.VMEM((w, w), F32)],
        compiler_params=_params(1),
        name="glu_bwd",
    )(y_pre, dmix, wg, bg, go)


def _ssm_bwd_state(dy, v2, al, xprev):
    s = dy.shape[0]
    nt, nc, ns = xprev.shape
    hs = ns // 2

    def body(dy_ref, v_ref, al_ref, xp_ref, dys_ref, g_ref, dal_ref, dxp, gs):
        for q in range(PAIRS):
            dys_ref[q, :, 0:LANES] = dy_ref[pl.ds(2 * q, nc, stride=CHUNK), :].astype(BF16)
            dys_ref[q, :, LANES : 2 * LANES] = dy_ref[pl.ds(2 * q + 1, nc, stride=CHUNK), :].astype(BF16)
        acc = _dot(dys_ref[0], v_ref[0])
        for r in range(1, PAIRS):
            acc = acc + _dot(dys_ref[r], v_ref[r])
        dxp[...] = acc
        ar = al_ref[0:1, :]
        ai = al_ref[1:2, :]
        zero = jnp.zeros((1, hs), F32)
        gs[pl.ds(nc - 1, 1), 0:hs] = zero
        gs[pl.ds(nc - 1, 1), hs:ns] = zero

        def step(n, carry):
            gr, gi = carry
            c = nc - 2 - n
            nr = dxp[pl.ds(c + 1, 1), 0:hs] + ar * gr + ai * gi
            ni = dxp[pl.ds(c + 1, 1), hs:ns] + ar * gi - ai * gr
            gs[pl.ds(c, 1), 0:hs] = nr
            gs[pl.ds(c, 1), hs:ns] = ni
            return nr, ni

        lax.fori_loop(0, nc - 1, step, (zero, zero))
        gv = gs[...]
        xv = xp_ref[...]
        gr, gi = gv[:, 0:hs], gv[:, hs:ns]
        xr, xi = xv[:, 0:hs], xv[:, hs:ns]
        dal_ref[0:1, :] = jnp.sum(gr * xr + gi * xi, axis=0, keepdims=True)
        dal_ref[1:2, :] = jnp.sum(gi * xr - gr * xi, axis=0, keepdims=True)
        g_ref[...] = gv.astype(BF16)

    return _call(
        body,
        grid=(nt,),
        in_specs=[
            pl.BlockSpec((s, LANES), lambda t: (0, t)),
            pl.BlockSpec((None, PAIRS, 2 * LANES, ns), lambda t: (t, 0, 0, 0)),
            pl.BlockSpec((None, 2, hs), lambda t: (t, 0, 0)),
            pl.BlockSpec((None, nc, ns), lambda t: (t, 0, 0)),
        ],
        out_specs=[
            pl.BlockSpec((None, PAIRS, nc, 2 * LANES), lambda t: (t, 0, 0, 0)),
            pl.BlockSpec((None, nc, ns), lambda t: (t, 0, 0)),
            pl.BlockSpec((None, 2, hs), lambda t: (t, 0, 0)),
        ],
        out_shape=[
            jax.ShapeDtypeStruct((nt, PAIRS, nc, 2 * LANES), BF16),
            jax.ShapeDtypeStruct((nt, nc, ns), BF16),
            jax.ShapeDtypeStruct((nt, 2, hs), F32),
        ],
        scratch_shapes=[pltpu.VMEM((nc, ns), F32), pltpu.VMEM((nc, ns), F32)],
        compiler_params=_params(1),
        name="ssm_bwd_state",
    )(dy, v2, al, xprev)


def _ssm_bwd_main(us, dys, k2, w2, xprev, gst, consts, s, width, ex):
    nt, _, nc, _ = us.shape
    ns = xprev.shape[-1]
    rep_k, mask_k, rep_s, mask_s = consts
    nk = 2 * LANES

    def body(us_ref, dys_ref, k_ref, w_ref, xp_ref, g_ref, rk_ref, mk_ref, rs_ref, ms_ref,
             du_ref, dk_ref, dw_ref, dv_ref, acc, du2, duf, dkf):
        q = pl.program_id(1)

        @pl.when(q == 0)
        def _():
            dkf[...] = jnp.zeros_like(dkf)

        usq = us_ref[q]
        gb = g_ref[...]
        dw_ref[...] = _gather_blocks(_dot_tn(usq, gb), ms_ref[...], rs_ref[...])
        dv_ref[...] = _gather_blocks(_dot_tn(dys_ref[q], xp_ref[...].astype(BF16)), ms_ref[...], rs_ref[...])
        acc[...] = _dot_nt(gb, w_ref[...])
        for m in range(PAIRS):

            @pl.when(q + m < PAIRS)
            def _():
                dyr = dys_ref[jnp.minimum(q + m, PAIRS - 1)]
                acc[...] += _dot_nt(dyr, k_ref[m])
                dkf[m] += _dot_tn(usq, dyr)

        du2[q] = acc[...]

        @pl.when(q == PAIRS - 1)
        def _():
            for p in range(PAIRS):
                duf[pl.ds(2 * p, nc, stride=CHUNK), :] = du2[p, :, 0:LANES]
                duf[pl.ds(2 * p + 1, nc, stride=CHUNK), :] = du2[p, :, LANES : 2 * LANES]
                dk_ref[p] = _gather_blocks(dkf[p], mk_ref[...], rk_ref[...])
            du_ref[...] = duf[...].astype(BF16)

    def per_tile(rows, cols):
        return pl.BlockSpec((None, PAIRS, rows, cols), lambda t, q: (t, 0, 0, 0))

    def per_pair(rows, cols):
        return pl.BlockSpec((None, None, rows, cols), lambda t, q: (t, q, 0, 0))

    def whole(a):
        return pl.BlockSpec(a.shape, lambda t, q: (0, 0))

    return _call_riding(
        body,
        ex,
        args=(us, dys, k2, w2, xprev, gst, rep_k, mask_k, rep_s, mask_s),
        grid=(nt, PAIRS),
        in_specs=[
            per_tile(nc, nk),
            per_tile(nc, nk),
            per_tile(nk, nk),
            per_pair(nk, ns),
            pl.BlockSpec((None, nc, ns), lambda t, q: (t, 0, 0)),
            pl.BlockSpec((None, nc, ns), lambda t, q: (t, 0, 0)),
            whole(rep_k),
            whole(mask_k),
            whole(rep_s),
            whole(mask_s),
        ],
        out_specs=[
            pl.BlockSpec((s, LANES), lambda t, q: (0, t)),
            per_tile(nk, LANES),
            per_pair(nk, LANES),
            per_pair(nk, LANES),
        ],
        out_shape=[
            jax.ShapeDtypeStruct((s, width), BF16),
            jax.ShapeDtypeStruct((nt, PAIRS, nk, LANES), F32),
            jax.ShapeDtypeStruct((nt, PAIRS, nk, LANES), F32),
            jax.ShapeDtypeStruct((nt, PAIRS, nk, LANES), F32),
        ],
        scratch_shapes=[
            pltpu.VMEM((nc, nk), F32),
            pltpu.VMEM((PAIRS, nc, nk), F32),
            pltpu.VMEM((s, LANES), F32),
            pltpu.VMEM((PAIRS, nk, nk), F32),
        ],
        compiler_params=_params(2),
        name="ssm_bwd_main",
    )


def _sgu_bwd(z, dmix, dz, lng, lnb, wt, wtt, biasb, go, rb):
    s = z.shape[0]
    w = lng.shape[-1]
    heads = wt.shape[0]
    ncol = (LANES, LANES)

    def body(zu_ref, zv_ref, dm_ref, dz_any, lng_ref, lnb_ref, wt_ref, wtt_ref, b_ref, go_ref,
             dz_ref, dw_ref, db_ref, dlg_ref, dlb_ref, dgo_ref, dzv):
        del dz_any
        i = pl.program_id(0)
        p = pl.program_id(1)

        @pl.when(jnp.logical_and(i == 0, p == 0))
        def _():
            dw_ref[...] = jnp.zeros_like(dw_ref)
            db_ref[...] = jnp.zeros_like(db_ref)
            dlg_ref[...] = jnp.zeros_like(dlg_ref)
            dlb_ref[...] = jnp.zeros_like(dlb_ref)
            dgo_ref[...] = jnp.zeros_like(dgo_ref)

        @pl.when(p == 0)
        def _():
            lng_v = lng_ref[...]
            for ck in range(rb // SGU_CHUNK):
                rows = slice(ck * SGU_CHUNK, (ck + 1) * SGU_CHUNK)
                zu = zu_ref[rows, :]
                zv = zv_ref[rows, :]
                u, vhat, rstd, vb, mix = _sgu_parts(zu, zv, lng_v, lnb_ref[...], wt_ref, b_ref)
                y = u * mix
                dy, dgt = _rms_bwd(dm_ref[rows, :], y, _rms(y), go_ref[...])
                dgo_ref[...] += jnp.sum(dgt, axis=0, keepdims=True)
                du = dy * mix
                dmx = dy * u
                dmb = dmx.astype(BF16)
                dvl = []
                for h in range(heads):
                    cols = slice(h * LANES, (h + 1) * LANES)
                    db_ref[h] += jnp.broadcast_to(jnp.sum(dmx[:, cols], axis=-1, keepdims=True), ncol)
                    dw_ref[h] += _dot_nt(dmb[:, cols], vb[:, cols])
                    dvl.append(_dot(wtt_ref[h], dmb[:, cols]))
                dvln = jnp.concatenate(dvl, axis=1)
                dlg_ref[...] += jnp.sum(dvln * vhat, axis=0, keepdims=True)
                dlb_ref[...] += jnp.sum(dvln, axis=0, keepdims=True)
                dvh = dvln * lng_v
                dv = rstd * (
                    dvh
                    - jnp.mean(dvh, axis=-1, keepdims=True)
                    - vhat * jnp.mean(dvh * vhat, axis=-1, keepdims=True)
                )
                dz_ref[rows, :] = (du * _gelu_grad(zu)).astype(BF16)
                dzv[rows, :] = (dv * _gelu_grad(zv)).astype(BF16)

        @pl.when(p == 1)
        def _():
            dz_ref[...] = dzv[...]

    return _call(
        body,
        grid=(s // rb, 2),
        in_specs=[
            pl.BlockSpec((rb, w), lambda i, p: (i, 1)),
            pl.BlockSpec((rb, w), lambda i, p: (i, 2)),
            pl.BlockSpec((rb, w), lambda i, p: (i, 1)),
            pl.BlockSpec(memory_space=pl.ANY),
            pl.BlockSpec((1, w), lambda i, p: (0, 0)),
            pl.BlockSpec((1, w), lambda i, p: (0, 0)),
            pl.BlockSpec((heads, SGU_CHUNK, SGU_CHUNK), lambda i, p: (0, 0, 0)),
            pl.BlockSpec((heads, SGU_CHUNK, SGU_CHUNK), lambda i, p: (0, 0, 0)),
            pl.BlockSpec((heads, SGU_CHUNK, LANES), lambda i, p: (0, 0, 0)),
            pl.BlockSpec((1, w), lambda i, p: (0, 0)),
        ],
        out_specs=[
            pl.BlockSpec((rb, w), lambda i, p: (i, 1 + p)),
            pl.BlockSpec((heads, SGU_CHUNK, SGU_CHUNK), lambda i, p: (0, 0, 0)),
            pl.BlockSpec((heads, SGU_CHUNK, LANES), lambda i, p: (0, 0, 0)),
            pl.BlockSpec((1, w), lambda i, p: (0, 0)),
            pl.BlockSpec((1, w), lambda i, p: (0, 0)),
            pl.BlockSpec((1, w), lambda i, p: (0, 0)),
        ],
        out_shape=[
            jax.ShapeDtypeStruct(dz.shape, dz.dtype),
            jax.ShapeDtypeStruct((heads, SGU_CHUNK, SGU_CHUNK), F32),
            jax.ShapeDtypeStruct((heads, SGU_CHUNK, LANES), F32),
            jax.ShapeDtypeStruct((1, w), F32),
            jax.ShapeDtypeStruct((1, w), F32),
            jax.ShapeDtypeStruct((1, w), F32),
        ],
        scratch_shapes=[pltpu.VMEM((rb, w), BF16)],
        input_output_aliases={3: 0},
        compiler_params=_params(2),
        name="sgu_bwd",
    )(z, z, dmix, dz, lng, lnb, wt, wtt, biasb, go)


def _in_proj_bwd(dz, w4, tm):
    s = dz.shape[0]
    nj, d, cw = w4.shape

    def body(dz_ref, w_ref, dh_ref):
        @pl.when(pl.program_id(1) == 0)
        def _():
            dh_ref[...] = jnp.zeros_like(dh_ref)

        dh_ref[...] += _dot_nt(dz_ref[...], w_ref[...])

    return _call(
        body,
        grid=(s // tm, nj),
        in_specs=[pl.BlockSpec((tm, cw), lambda i, j: (i, j)), pl.BlockSpec((None, d, cw), lambda i, j: (j, 0, 0))],
        out_specs=pl.BlockSpec((tm, d), lambda i, j: (i, 0)),
        out_shape=jax.ShapeDtypeStruct((s, d), F32),
        compiler_params=_params(2),
        name="in_proj_bwd",
    )(dz, w4)


_ANY = pl.BlockSpec(memory_space=pl.ANY)


def _position():
    x, y, c = lax.axis_index("x"), lax.axis_index("y"), lax.axis_index("c")
    return x, y, c, [(1 - x, y), (x, 1 - y), (1 - x, 1 - y)]


def _remote(src, dst, send_sems, recv_sems, k, to):
    return pltpu.make_async_remote_copy(
        src_ref=src, dst_ref=dst, send_sem=send_sems.at[k], recv_sem=recv_sems.at[k], device_id=to, device_id_type=MESH
    )


class _Riding:
    def __init__(self, srcs, out_shapes, n_sems, start, finish):
        self.srcs, self.out_shapes, self.n_sems, self.start, self.finish = srcs, out_shapes, n_sems, start, finish


def _call_riding(body, ex, *, grid, in_specs, out_specs, out_shape, scratch_shapes, args, **kw):
    n_in, n_out, n_scr = len(in_specs), len(out_specs), len(scratch_shapes)
    xi, xo = len(ex.srcs), len(ex.out_shapes)

    def fused(*refs):
        cin, xin = refs[:n_in], refs[n_in : n_in + xi]
        o = n_in + xi
        cout, xout = refs[o : o + n_out], refs[o + n_out : o + n_out + xo]
        scr = refs[o + n_out + xo :]
        ids = [pl.program_id(a) for a in range(len(grid))]
        first = functools.reduce(jnp.logical_and, [i == 0 for i in ids])
        last = functools.reduce(jnp.logical_and, [i == n - 1 for i, n in zip(ids, grid)])

        @pl.when(first)
        def _():
            ex.start(xin, xout, *scr[n_scr:])

        body(*cin, *cout, *scr[:n_scr])

        @pl.when(last)
        def _():
            ex.finish(xin, xout, *scr[n_scr:])

    sems = [pltpu.SemaphoreType.DMA((ex.n_sems,)), pltpu.SemaphoreType.DMA((ex.n_sems,))]
    outs = _call(
        fused,
        grid=grid,
        in_specs=list(in_specs) + [_ANY] * xi,
        out_specs=list(out_specs) + [_ANY] * xo,
        out_shape=list(out_shape) + list(ex.out_shapes),
        scratch_shapes=list(scratch_shapes) + sems,
        **kw,
    )(*args, *ex.srcs)
    return outs[:n_out], outs[n_out:]


def _gather_one(shard):
    def own_copies(ins, outs, send, recv):
        x, y, c, chips = _position()
        me = 2 * x + y
        sib = (x, y, 1 - c)
        cps = [_remote(ins[0].at[c], outs[0].at[me, c], send, recv, 0, sib)]
        cps += [_remote(ins[0].at[c], outs[0].at[me, c], send, recv, 1 + k, (px, py, c)) for k, (px, py) in enumerate(chips)]
        return cps, (c, me, sib, chips)

    def start(ins, outs, send, recv):
        for cp in own_copies(ins, outs, send, recv)[0]:
            cp.start()

    def finish(ins, outs, send, recv):
        cps, (c, me, sib, chips) = own_copies(ins, outs, send, recv)
        for k, (px, py) in enumerate(chips):
            landed = outs[0].at[2 * px + py, c]
            _remote(ins[0].at[c], landed, send, recv, 1 + k, (px, py, c)).wait_recv()
            cps.append(_remote(landed, landed, send, recv, 4 + k, sib))
            cps[-1].start()
        _remote(ins[0].at[1 - c], outs[0].at[me, 1 - c], send, recv, 0, sib).wait_recv()
        for k, (px, py) in enumerate(chips):
            theirs = outs[0].at[2 * px + py, 1 - c]
            _remote(theirs, theirs, send, recv, 4 + k, sib).wait_recv()
        for cp in cps:
            cp.wait_send()

    return _Riding([shard], [jax.ShapeDtypeStruct((N_CHIPS,) + shard.shape, shard.dtype)], 7, start, finish)


def _to_owner_one(p):
    def copies(ins, outs, send, recv):
        x, y, c, chips = _position()
        return [_remote(ins[0].at[2 * px + py], outs[0].at[k], send, recv, k, (px, py, c)) for k, (px, py) in enumerate(chips)]

    def start(ins, outs, send, recv):
        for cp in copies(ins, outs, send, recv):
            cp.start()

    def finish(ins, outs, send, recv):
        cps = copies(ins, outs, send, recv)
        for cp in cps:
            cp.wait_recv()
        for cp in cps:
            cp.wait_send()

    return _Riding([p], [jax.ShapeDtypeStruct((3,) + p.shape[1:], p.dtype)], 3, start, finish)


def _gather_chips(arrs, name):
    n = len(arrs)

    def body(*refs):
        ins, outs = refs[:n], refs[n : 2 * n]
        send_sems, recv_sems, local_sems = refs[2 * n :]
        x, y, c, chips = _position()
        me = 2 * x + y
        local = [pltpu.make_async_copy(ins[i], outs[i].at[me], local_sems.at[i]) for i in range(n)]
        for cp in local:
            cp.start()
        sends = []
        for k, (px, py) in enumerate(chips):
            for i in range(n):
                sends.append(_remote(ins[i], outs[i].at[me], send_sems, recv_sems, k * n + i, (px, py, c)))
                sends[-1].start()
        for k, (px, py) in enumerate(chips):
            for i in range(n):
                _remote(ins[i], outs[i].at[2 * px + py], send_sems, recv_sems, k * n + i, (px, py, c)).wait_recv()
        for cp in sends:
            cp.wait_send()
        for cp in local:
            cp.wait()

    return _call(
        body,
        in_specs=[_ANY] * n,
        out_specs=[_ANY] * n,
        out_shape=[jax.ShapeDtypeStruct((N_CHIPS,) + a.shape, a.dtype) for a in arrs],
        scratch_shapes=[
            pltpu.SemaphoreType.DMA((3 * n,)),
            pltpu.SemaphoreType.DMA((3 * n,)),
            pltpu.SemaphoreType.DMA((n,)),
        ],
        name=name,
    )(*arrs)


def _pair_exchange(gs, name):
    n = len(gs)

    def body(*refs):
        ins, outs = refs[:n], refs[n : 2 * n]
        send_sems, recv_sems = refs[2 * n :]
        x, y, c, _ = _position()
        sib = (x, y, 1 - c)
        cps = []
        for i in range(n):
            for j in range(N_CHIPS):
                cps.append(_remote(ins[i].at[j, 1 - c], outs[i].at[j], send_sems, recv_sems, i * N_CHIPS + j, sib))
                cps[-1].start()
        for cp in cps:
            cp.wait_recv()
        for cp in cps:
            cp.wait_send()

    return _call(
        body,
        in_specs=[_ANY] * n,
        out_specs=[_ANY] * n,
        out_shape=[jax.ShapeDtypeStruct((g.shape[0],) + g.shape[2:], g.dtype) for g in gs],
        scratch_shapes=[pltpu.SemaphoreType.DMA((n * N_CHIPS,)), pltpu.SemaphoreType.DMA((n * N_CHIPS,))],
        name=name,
    )(*gs)


def _gather_halves(shards, name):
    n = len(shards)

    def body(*refs):
        ins, outs = refs[:n], refs[n : 2 * n]
        ici_send, ici_recv, d2d_send, d2d_recv = refs[2 * n :]
        x, y, c, chips = _position()
        me = 2 * x + y
        sib = (x, y, 1 - c)
        started = []
        for i in range(n):
            started.append(_remote(ins[i].at[c], outs[i].at[me, c], d2d_send, d2d_recv, i, sib))
            started[-1].start()
        for k, (px, py) in enumerate(chips):
            for i in range(n):
                started.append(_remote(ins[i].at[c], outs[i].at[me, c], ici_send, ici_recv, k * n + i, (px, py, c)))
                started[-1].start()
        for k, (px, py) in enumerate(chips):
            for i in range(n):
                landed = outs[i].at[2 * px + py, c]
                _remote(ins[i].at[c], landed, ici_send, ici_recv, k * n + i, (px, py, c)).wait_recv()
                started.append(_remote(landed, landed, d2d_send, d2d_recv, (k + 1) * n + i, sib))
                started[-1].start()
        for i in range(n):
            _remote(ins[i].at[1 - c], outs[i].at[me, 1 - c], d2d_send, d2d_recv, i, sib).wait_recv()
        for k, (px, py) in enumerate(chips):
            for i in range(n):
                theirs = outs[i].at[2 * px + py, 1 - c]
                _remote(theirs, theirs, d2d_send, d2d_recv, (k + 1) * n + i, sib).wait_recv()
        for cp in started:
            cp.wait_send()

    return _call(
        body,
        in_specs=[_ANY] * n,
        out_specs=[_ANY] * n,
        out_shape=[jax.ShapeDtypeStruct((N_CHIPS,) + a.shape, a.dtype) for a in shards],
        scratch_shapes=[
            pltpu.SemaphoreType.DMA((3 * n,)),
            pltpu.SemaphoreType.DMA((3 * n,)),
            pltpu.SemaphoreType.DMA((4 * n,)),
            pltpu.SemaphoreType.DMA((4 * n,)),
        ],
        name=name,
    )(*shards)


def _place_own(shard, gathered, pos, name):
    _, hr, cols = shard.shape
    tr = _row_tile(hr, 512)

    def body(pos_ref, s_ref, g_any, o_ref):
        del pos_ref, g_any
        o_ref[...] = s_ref[...]

    return _call(
        body,
        grid_spec=pltpu.PrefetchScalarGridSpec(
            num_scalar_prefetch=1,
            grid=(hr // tr,),
            in_specs=[pl.BlockSpec((None, tr, cols), lambda i, p: (p[1], i, 0)), _ANY],
            out_specs=pl.BlockSpec((None, None, tr, cols), lambda i, p: (p[0], p[1], i, 0)),
        ),
        out_shape=jax.ShapeDtypeStruct(gathered.shape, gathered.dtype),
        input_output_aliases={2: 0},
        compiler_params=_params(1),
        name=name,
    )(pos, shard, gathered)


def _chip_exchange(ps, name):
    n = len(ps)

    def body(*refs):
        ins, outs = refs[:n], refs[n : 2 * n]
        send_sems, recv_sems = refs[2 * n :]
        x, y, c, chips = _position()
        sends = []
        for k, (px, py) in enumerate(chips):
            for i in range(n):
                sends.append(_remote(ins[i].at[2 * px + py], outs[i].at[k], send_sems, recv_sems, k * n + i, (px, py, c)))
                sends[-1].start()
        for cp in sends:
            cp.wait_recv()
        for cp in sends:
            cp.wait_send()

    return _call(
        body,
        in_specs=[_ANY] * n,
        out_specs=[_ANY] * n,
        out_shape=[jax.ShapeDtypeStruct((3,) + p.shape[1:], p.dtype) for p in ps],
        scratch_shapes=[pltpu.SemaphoreType.DMA((3 * n,)), pltpu.SemaphoreType.DMA((3 * n,))],
        name=name,
    )(*ps)


def _pair_share(fs, name):
    n = len(fs)

    def body(*refs):
        ins, outs = refs[:n], refs[n : 2 * n]
        send_sems, recv_sems = refs[2 * n :]
        x, y, c, _ = _position()
        sib = (x, y, 1 - c)
        sends = [_remote(ins[i].at[c], outs[i].at[c], send_sems, recv_sems, i, sib) for i in range(n)]
        for cp in sends:
            cp.start()
        for i in range(n):
            _remote(ins[i].at[1 - c], outs[i].at[1 - c], send_sems, recv_sems, i, sib).wait_recv()
        for cp in sends:
            cp.wait_send()

    return _call(
        body,
        in_specs=[_ANY] * n,
        out_specs=[_ANY] * n,
        out_shape=[jax.ShapeDtypeStruct(f.shape, f.dtype) for f in fs],
        input_output_aliases={i: i for i in range(n)},
        scratch_shapes=[pltpu.SemaphoreType.DMA((n,)), pltpu.SemaphoreType.DMA((n,))],
        name=name,
    )(*fs)


def _pair_sum(g, r, core, name):
    nj, _, hr, cols = g.shape
    tr = _row_tile(hr, 256)

    def body(c_ref, g_ref, r_ref, o_ref):
        del c_ref
        o_ref[...] = (g_ref[...].astype(F32) + r_ref[...].astype(F32)).astype(o_ref.dtype)

    return _call(
        body,
        grid_spec=pltpu.PrefetchScalarGridSpec(
            num_scalar_prefetch=1,
            grid=(nj, hr // tr),
            in_specs=[
                pl.BlockSpec((None, None, tr, cols), lambda j, i, c: (j, c[0], i, 0)),
                pl.BlockSpec((None, tr, cols), lambda j, i, c: (j, i, 0)),
            ],
            out_specs=pl.BlockSpec((None, tr, cols), lambda j, i, c: (j, i, 0)),
        ),
        out_shape=jax.ShapeDtypeStruct((nj, hr, cols), g.dtype),
        compiler_params=_params(2),
        name=name,
    )(core, g, r)


def _chip_sum(p, q, pos, name):
    nq, hr, cols = q.shape
    tr = _row_tile(hr, 256)

    def body(pos_ref, p_ref, *refs):
        del pos_ref
        o_ref = refs[nq]
        tot = p_ref[...].astype(F32)
        for k in range(nq):
            tot = tot + refs[k][...].astype(F32)
        o_ref[...] = tot

    return _call(
        body,
        grid_spec=pltpu.PrefetchScalarGridSpec(
            num_scalar_prefetch=1,
            grid=(hr // tr,),
            in_specs=[pl.BlockSpec((None, tr, cols), lambda i, s: (s[0], i, 0))]
            + [pl.BlockSpec((None, tr, cols), functools.partial(lambda k, i, s: (k, i, 0), k)) for k in range(nq)],
            out_specs=pl.BlockSpec((None, tr, cols), lambda i, s: (s[1], i, 0)),
        ),
        out_shape=jax.ShapeDtypeStruct((2, hr, cols), F32),
        compiler_params=_params(1),
        name=name,
    )(pos, p, *([q] * nq))


def _adamw(w, g, m, v, name):
    rows, cols = w.shape
    tr = _row_tile(rows, max(8, (2**18) // cols), mult=8)
    c1 = 1.0 - ADAM_B1**ADAM_STEP
    c2 = 1.0 - ADAM_B2**ADAM_STEP

    def body(w_ref, g_ref, m_ref, v_ref, d_ref, nm_ref, nv_ref):
        gv = g_ref[...]
        nm = ADAM_B1 * m_ref[...] + (1.0 - ADAM_B1) * gv
        nv = ADAM_B2 * v_ref[...] + (1.0 - ADAM_B2) * (gv * gv)
        nm_ref[...] = nm
        nv_ref[...] = nv
        d_ref[...] = -ADAM_LR * ((nm / c1) / (jnp.sqrt(nv / c2) + ADAM_EPS) + ADAM_WD * w_ref[...])

    spec = pl.BlockSpec((tr, cols), lambda i: (i, 0))
    sds = jax.ShapeDtypeStruct((rows, cols), F32)
    return _call(
        body,
        grid=(rows // tr,),
        in_specs=[spec] * 4,
        out_specs=[spec] * 3,
        out_shape=[sds] * 3,
        compiler_params=_params(1),
        name=name,
    )(w, g, m, v)


_TILE_ELEMS = 8 * LANES
_FLAT_ROW_MULT = 8 * 2 * N_CHIPS


def _flat_rows(shape):
    n = math.prod(shape)
    return (n + _TILE_ELEMS - 1) // _TILE_ELEMS * 8


def _pack(arrs):
    parts = []
    for a in arrs:
        rows = _flat_rows(a.shape)
        flat = a.reshape(-1).astype(F32)
        flat = jnp.pad(flat, (0, rows * LANES - flat.shape[0]))
        parts.append(flat.reshape(rows, LANES))
    total = sum(p.shape[0] for p in parts)
    pad = -total % _FLAT_ROW_MULT
    if pad:
        parts.append(jnp.zeros((pad, LANES), F32))
    return jnp.concatenate(parts, axis=0)


def _unpack(flat, shapes):
    out, row = [], 0
    for shp in shapes:
        rows = _flat_rows(shp)
        out.append(flat[row : row + rows].reshape(-1)[: math.prod(shp)].reshape(shp))
        row += rows
    return out


def kernel(x, norm_mix_g, w_in, ssm_a_re, ssm_a_im, ssm_b_re, ssm_b_im, ssm_c_re, ssm_c_im, ssm_d, ssm_log_dt, ssm_glu_w, ssm_glu_b, sgu_ln_g, sgu_ln_b, sgu_w, sgu_b, out_norm_ssm_g, out_norm_sgu_g, w_out, norm_mlp_g, w_up, w_down, norm_final_g, loss_target, m_norm_mix_g, m_w_in, m_ssm_a_re, m_ssm_a_im, m_ssm_b_re, m_ssm_b_im, m_ssm_c_re, m_ssm_c_im, m_ssm_d, m_ssm_log_dt, m_ssm_glu_w, m_ssm_glu_b, m_sgu_ln_g, m_sgu_ln_b, m_sgu_w, m_sgu_b, m_out_norm_ssm_g, m_out_norm_sgu_g, m_w_out, m_norm_mlp_g, m_w_up, m_w_down, m_norm_final_g, v_norm_mix_g, v_w_in, v_ssm_a_re, v_ssm_a_im, v_ssm_b_re, v_ssm_b_im, v_ssm_c_re, v_ssm_c_im, v_ssm_d, v_ssm_log_dt, v_ssm_glu_w, v_ssm_glu_b, v_sgu_ln_g, v_sgu_ln_b, v_sgu_w, v_sgu_b, v_out_norm_ssm_g, v_out_norm_sgu_g, v_w_out, v_norm_mlp_g, v_w_up, v_w_down, v_norm_final_g):
    weights = dict(norm_mix_g=norm_mix_g, w_in=w_in, ssm_a_re=ssm_a_re, ssm_a_im=ssm_a_im, ssm_b_re=ssm_b_re, ssm_b_im=ssm_b_im, ssm_c_re=ssm_c_re, ssm_c_im=ssm_c_im, ssm_d=ssm_d, ssm_log_dt=ssm_log_dt, ssm_glu_w=ssm_glu_w, ssm_glu_b=ssm_glu_b, sgu_ln_g=sgu_ln_g, sgu_ln_b=sgu_ln_b, sgu_w=sgu_w, sgu_b=sgu_b, out_norm_ssm_g=out_norm_ssm_g, out_norm_sgu_g=out_norm_sgu_g, w_out=w_out, norm_mlp_g=norm_mlp_g, w_up=w_up, w_down=w_down, norm_final_g=norm_final_g)
    mom_m = dict(norm_mix_g=m_norm_mix_g, w_in=m_w_in, ssm_a_re=m_ssm_a_re, ssm_a_im=m_ssm_a_im, ssm_b_re=m_ssm_b_re, ssm_b_im=m_ssm_b_im, ssm_c_re=m_ssm_c_re, ssm_c_im=m_ssm_c_im, ssm_d=m_ssm_d, ssm_log_dt=m_ssm_log_dt, ssm_glu_w=m_ssm_glu_w, ssm_glu_b=m_ssm_glu_b, sgu_ln_g=m_sgu_ln_g, sgu_ln_b=m_sgu_ln_b, sgu_w=m_sgu_w, sgu_b=m_sgu_b, out_norm_ssm_g=m_out_norm_ssm_g, out_norm_sgu_g=m_out_norm_sgu_g, w_out=m_w_out, norm_mlp_g=m_norm_mlp_g, w_up=m_w_up, w_down=m_w_down, norm_final_g=m_norm_final_g)
    mom_v = dict(norm_mix_g=v_norm_mix_g, w_in=v_w_in, ssm_a_re=v_ssm_a_re, ssm_a_im=v_ssm_a_im, ssm_b_re=v_ssm_b_re, ssm_b_im=v_ssm_b_im, ssm_c_re=v_ssm_c_re, ssm_c_im=v_ssm_c_im, ssm_d=v_ssm_d, ssm_log_dt=v_ssm_log_dt, ssm_glu_w=v_ssm_glu_w, ssm_glu_b=v_ssm_glu_b, sgu_ln_g=v_sgu_ln_g, sgu_ln_b=v_sgu_ln_b, sgu_w=v_sgu_w, sgu_b=v_sgu_b, out_norm_ssm_g=v_out_norm_ssm_g, out_norm_sgu_g=v_out_norm_sgu_g, w_out=v_w_out, norm_mlp_g=v_norm_mlp_g, w_up=v_w_up, w_down=v_w_down, norm_final_g=v_norm_final_g)
    names = list(weights)
    large = ["w_in", "ssm_glu_w", "w_out", "w_up", "w_down"]
    small = [n for n in names if n not in large]

    s, d = x.shape[1], x.shape[2]
    xs = x.reshape(s, d)
    target = loss_target.reshape(s, d)
    width = ssm_glu_w.shape[-1]
    ff = w_down.shape[1] * N_CHIPS
    tm = min(512, s)
    core = lax.axis_index("c").astype(jnp.int32).reshape(1)
    chip = (2 * lax.axis_index("x") + lax.axis_index("y")).astype(jnp.int32).reshape(1)
    pos = jnp.concatenate([chip, core])

    shards = [w[0].astype(BF16).reshape(2, w.shape[1] // 2, w.shape[2]) for w in (w_in, ssm_glu_w, w_out, w_up, w_down)]
    gathered = _gather_halves(shards[:3], "gather_weights")
    gathered = [_place_own(sh, g, pos, "place_" + n) for sh, g, n in zip(shards, gathered, large)]
    w_in4 = gathered[0].reshape(N_CHIPS, d, w_in.shape[2])
    wg_full = gathered[1].reshape(width, width)
    w_out_full = gathered[2].reshape(d, d)

    ssm_args = (ssm_a_re[0], ssm_a_im[0], ssm_b_re[0], ssm_b_im[0], ssm_c_re[0], ssm_c_im[0], ssm_d[0], ssm_log_dt[0])
    (k2c, w2c, v2c, al), ssm_vjp = jax.vjp(_ssm_mats, *ssm_args)
    consts = _spread_consts(ssm_b_re.shape[-1], ssm_a_re.shape[-1])
    k2b, w2b, v2b = _ssm_spread(k2c, w2c, v2c, consts)
    causal = jnp.tril(jnp.ones((SGU_CHUNK, SGU_CHUNK), dtype=bool))
    wt = jnp.where(causal[None], sgu_w[0], 0.0)
    wtb = wt.astype(BF16)
    wttb = jnp.swapaxes(wt, 1, 2).astype(BF16)
    heads = sgu_w.shape[1]
    biasb = jnp.broadcast_to(sgu_b[0][:, :, None], (heads, SGU_CHUNK, LANES))

    (z, h1b), (up_g,) = _in_proj(xs, norm_mix_g, w_in4, tm, _gather_one(shards[3]))
    up4 = _place_own(shards[3], up_g, pos, "place_w_up").reshape(N_CHIPS, d, w_up.shape[2])
    y_pre, xprev, us = _ssm_fwd(z, k2b, w2b, v2b, al)
    mixed = _glu_fwd(y_pre, wg_full, ssm_glu_b, out_norm_ssm_g, d, tm)
    mixed = _sgu_fwd(z, mixed, sgu_ln_g, sgu_ln_b, wtb, biasb, out_norm_sgu_g, tm)
    x1 = _out_proj(xs, mixed, w_out_full, tm)
    tf = min(2048, up4.shape[-1])
    (up, h2b), (down_g,) = _mlp_up(x1, norm_mlp_g, up4, tm, tf, _gather_one(shards[4]))
    w_down_full = _place_own(shards[4], down_g, pos, "place_w_down").reshape(ff, d)
    x2 = _mlp_down(x1, up, w_down_full, tm, tf)
    loss_part, dx2, dx2b, d_norm_final = _final(x2, target, norm_final_g.reshape(1, d), tm)

    dup, dh2 = _mlp_bwd(dx2b, up, up4, w_down_full, tm, min(1024, up4.shape[-1]))
    dx1, dx1b, d_norm_mlp = _norm_bwd(dh2, x1, dx2, norm_mlp_g, min(256, s), "norm_mlp_bwd")
    hr_big = d // 2
    tkk = min(512, s)
    g_down = _grad_w(
        up, dx2b, (N_CHIPS, 2, hr_big, d),
        lambda i, j: (i // (2 * (hr_big // min(1024, hr_big))), (i // (hr_big // min(1024, hr_big))) % 2,
                      i % (hr_big // min(1024, hr_big)), j),
        min(1024, hr_big), d, tkk, True, "grad_w_down")
    (sib_down,) = _pair_exchange([g_down], "w_down_to_sibling")
    pair_down = _pair_sum(g_down, sib_down, core, "pair_sum_w_down")
    tn_up = min(2048, up4.shape[-1])
    per_up = up4.shape[-1] // tn_up
    g_up, chips_down = _grad_w(
        h2b, dup, (N_CHIPS, 2, hr_big, up4.shape[-1]),
        lambda i, j: (j // per_up, i // (hr_big // min(1024, hr_big)), i % (hr_big // min(1024, hr_big)), j % per_up),
        min(1024, hr_big), tn_up, tkk, False, "grad_w_up", ex=_to_owner_one(pair_down))
    (sib_up,) = _pair_exchange([g_up], "w_up_to_sibling")
    pair_up = _pair_sum(g_up, sib_up, core, "pair_sum_w_up")
    dmix = _out_proj_bwd(dx1b, w_out_full, tm)
    hr_out = d // (2 * N_CHIPS)
    g_out = _grad_w(
        mixed, dx1b, (N_CHIPS, 2, hr_out, d),
        lambda i, j: (i // 2, i % 2, 0, j), hr_out, d, tkk, False, "grad_w_out")
    dy_pre, g_glu, d_glu_b, d_norm_ssm = _glu_bwd(y_pre, dmix, wg_full, ssm_glu_b, out_norm_ssm_g, tm)
    dys, gst, d_al = _ssm_bwd_state(dy_pre, v2b, al, xprev)
    (dz, d_k2, d_w2, d_v2), (chips_up,) = _ssm_bwd_main(
        us, dys, k2b, w2b, xprev, gst, consts, s, z.shape[1], _to_owner_one(pair_up))
    dz, d_wt, d_bias, d_ln_g, d_ln_b, d_norm_sgu = _sgu_bwd(
        z, dmix, dz, sgu_ln_g, sgu_ln_b, wtb, wttb, biasb, out_norm_sgu_g, tm)
    dh1 = _in_proj_bwd(dz, w_in4, tm)
    grad_x, _, d_norm_mix = _norm_bwd(dh1, xs, dx1, norm_mix_g, min(256, s), "norm_mix_bwd")
    cw_in = w_in4.shape[-1]
    g_in = _grad_w(
        h1b, dz, (N_CHIPS, 2, hr_big, cw_in),
        lambda i, j: (j, i // (hr_big // min(1024, hr_big)), i % (hr_big // min(1024, hr_big)), 0),
        min(1024, hr_big), cw_in, tkk, False, "grad_w_in")

    d_ssm = ssm_vjp((d_k2, d_w2, d_v2, d_al))
    small_grads = dict(
        norm_mix_g=d_norm_mix, ssm_a_re=d_ssm[0], ssm_a_im=d_ssm[1], ssm_b_re=d_ssm[2], ssm_b_im=d_ssm[3],
        ssm_c_re=d_ssm[4], ssm_c_im=d_ssm[5], ssm_d=d_ssm[6], ssm_log_dt=d_ssm[7], ssm_glu_b=d_glu_b,
        sgu_ln_g=d_ln_g, sgu_ln_b=d_ln_b, sgu_w=jnp.where(causal[None], d_wt, 0.0), sgu_b=d_bias[:, :, 0],
        out_norm_ssm_g=d_norm_ssm, out_norm_sgu_g=d_norm_sgu, norm_mlp_g=d_norm_mlp, norm_final_g=d_norm_final)
    flat = _pack([small_grads[n] for n in small])
    hr_small = flat.shape[0] // (2 * N_CHIPS)
    g_small = flat.reshape(N_CHIPS, 2, hr_small, LANES)

    hr_glu = width // (2 * N_CHIPS)
    grads = [g_in, g_glu.reshape(N_CHIPS, 2, hr_glu, width), g_out, g_small]
    tags = ["w_in", "glu_w", "w_out", "small"]
    from_sib = _pair_exchange(grads, "grads_to_sibling")
    pair = [_pair_sum(g, r, core, "pair_sum_" + t) for g, r, t in zip(grads, from_sib, tags)]
    from_chips = list(_chip_exchange(pair, "grads_to_owner"))
    pair = pair[:3] + [pair_up, pair_down, pair[3]]
    from_chips = from_chips[:3] + [chips_up, chips_down, from_chips[3]]
    tags = tags[:3] + ["w_up", "w_down", "small"]
    halves = [_chip_sum(p, q, pos, "chip_sum_" + t) for p, q, t in zip(pair, from_chips, tags)]
    owned = _pair_share(halves, "grads_to_both_cores")
    (small_all,) = _gather_chips([owned[5]], "gather_small_grads")
    small_flat = small_all.reshape(flat.shape)

    grad_out, delta_out, m_out, v_out = {}, {}, {}, {}
    for n, g in zip(large, owned[:5]):
        shp = weights[n].shape
        g2 = g.reshape(shp[1], shp[2])
        dl, nm, nv = _adamw(weights[n][0], g2, mom_m[n][0], mom_v[n][0], "adamw_" + n)
        grad_out[n], delta_out[n], m_out[n], v_out[n] = g2.reshape(shp), dl.reshape(shp), nm.reshape(shp), nv.reshape(shp)
    shapes = [weights[n].shape for n in small]
    dl, nm, nv = _adamw(
        _pack([weights[n] for n in small]), small_flat, _pack([mom_m[n] for n in small]),
        _pack([mom_v[n] for n in small]), "adamw_small")
    for n, g, a, b, c in zip(small, _unpack(small_flat, shapes), _unpack(dl, shapes), _unpack(nm, shapes), _unpack(nv, shapes)):
        grad_out[n], delta_out[n], m_out[n], v_out[n] = g, a, b, c

    loss = lax.psum(loss_part[0, 0], ("x", "y", "c"))
    return (loss, grad_x.reshape(x.shape), *[grad_out[n] for n in names], *[delta_out[n] for n in names],
            *[m_out[n] for n in names], *[v_out[n] for n in names])
```

```python
import functools
import math

import numpy as np
import jax
import jax.numpy as jnp
from jax import lax
from jax.experimental import pallas as pl
from jax.experimental.pallas import tpu as pltpu

F32 = jnp.float32
BF16 = jnp.bfloat16
MESH = pl.DeviceIdType.MESH
HIGHEST = lax.Precision.HIGHEST

EPS = 1e-6
ADAM_LR = 0.001
ADAM_B1 = 0.9
ADAM_B2 = 0.999
ADAM_EPS = 1e-08
ADAM_WD = 0.01
ADAM_STEP = 10

N_CHIPS = 4
LANES = 128
SSM_GROUP = 16
SSM_STATE = 64
GROUPS_PER_TILE = LANES // SSM_GROUP
CHUNK = 16
PAIRS = CHUNK // 2
SGU_CHUNK = 128
VMEM_LIMIT = 56 * 2**20


def _params(n_axes, vmem=VMEM_LIMIT):
    return pltpu.CompilerParams(dimension_semantics=("arbitrary",) * n_axes, vmem_limit_bytes=vmem)


def _call(body, **kw):
    return pl.pallas_call(body, **kw)


def _dot(a, b):
    return jnp.dot(a, b, preferred_element_type=F32)


def _dot_nt(a, b):
    return lax.dot_general(a, b, (((1,), (1,)), ((), ())), preferred_element_type=F32)


def _dot_tn(a, b):
    return lax.dot_general(a, b, (((0,), (0,)), ((), ())), preferred_element_type=F32)


_GELU_K = math.sqrt(2.0 / math.pi)
_GELU_C = 0.044715


def _gelu_both(x):
    x2 = x * x
    t = jnp.tanh(x * (_GELU_K + (_GELU_K * _GELU_C) * x2))
    hx = 0.5 * x
    onep = 1.0 + t
    return hx * onep, 0.5 * onep + hx * (1.0 - t * t) * (_GELU_K + (3.0 * _GELU_K * _GELU_C) * x2)


def _gelu(x):
    return _gelu_both(x)[0]


def _sigmoid(x):
    return 1.0 / (1.0 + jnp.exp(-x))


def _rms(x):
    return lax.rsqrt(jnp.mean(x * x, axis=-1, keepdims=True) + EPS)


def _rms_bwd(dy, x, r, g):
    a = dy * g
    dx = r * a - x * (r * r * r) * jnp.mean(a * x, axis=-1, keepdims=True)
    return dx, dy * x * r


def _row_tile(rows, target, mult=16):
    for t in range(min(rows, target), 0, -1):
        if rows % t == 0 and t % mult == 0:
            return t
    return rows


def _ssm_mats(a_re, a_im, b_re, b_im, c_re, c_im, d, log_dt):
    g, p = a_re.shape
    h = b_re.shape[-1]
    nt = g // GROUPS_PER_TILE
    dt = jnp.exp(log_dt)[:, None]
    lr, li = a_re * dt, a_im * dt

    def apow(l):
        mag = jnp.exp(lr * l)
        return mag * jnp.cos(li * l), mag * jnp.sin(li * l)

    ar, ai = apow(1.0)
    den = a_re * a_re + a_im * a_im
    qr = ((ar - 1.0) * a_re + ai * a_im) / den
    qi = (ai * a_re - (ar - 1.0) * a_im) / den
    bt_re, bt_im = jnp.swapaxes(b_re, 1, 2), jnp.swapaxes(b_im, 1, 2)
    bbr = qr[:, None, :] * bt_re - qi[:, None, :] * bt_im
    bbi = qr[:, None, :] * bt_im + qi[:, None, :] * bt_re
    ls = jnp.arange(CHUNK + 1, dtype=F32)[:, None, None]
    pr, pi = apow(ls)
    tr = pr[:CHUNK, :, None, :] * bbr[None] - pi[:CHUNK, :, None, :] * bbi[None]
    ti = pr[:CHUNK, :, None, :] * bbi[None] + pi[:CHUNK, :, None, :] * bbr[None]
    k = jnp.einsum("gop,lgip->lgio", c_re, tr, precision=HIGHEST) - jnp.einsum(
        "gop,lgip->lgio", c_im, ti, precision=HIGHEST
    )
    k = k.at[0].add(d[:, :, None] * jnp.eye(h, dtype=F32))

    def tiles(a):
        a = a.reshape(PAIRS, 2, nt, GROUPS_PER_TILE, h, a.shape[-1])
        return jnp.transpose(a, (2, 0, 1, 3, 4, 5)).reshape(nt, PAIRS, 2 * GROUPS_PER_TILE * h, a.shape[-1])

    zero = jnp.zeros_like(k[:1])
    kz = jnp.concatenate([zero, zero, k], axis=0).reshape(PAIRS + 1, 2, g, h, h)
    even, odd = kz[:, 0], kz[:, 1]
    fill = jnp.zeros((PAIRS, g, h, LANES - 2 * h), F32)
    row0 = jnp.concatenate([even[1:], odd[1:], fill], axis=-1)
    row1 = jnp.concatenate([odd[:-1], even[1:], fill], axis=-1)
    k2c = tiles(jnp.stack([row0, row1], axis=1).reshape(2 * PAIRS, g, h, LANES))
    w2c = tiles(jnp.concatenate([tr[::-1], ti[::-1]], axis=-1))
    car = c_re[None] * pr[1:, :, None, :] - c_im[None] * pi[1:, :, None, :]
    cai = c_re[None] * pi[1:, :, None, :] + c_im[None] * pr[1:, :, None, :]
    v2c = tiles(jnp.concatenate([car, -cai], axis=-1))
    al = jnp.stack([pr[CHUNK].reshape(nt, -1), pi[CHUNK].reshape(nt, -1)], axis=1)
    return k2c, w2c, v2c, al


def _spread_consts(h, p):
    gg = GROUPS_PER_TILE
    row_g = (np.arange(2 * gg * h) // h) % gg
    colk = np.arange(2 * gg * h)
    rep_k = np.zeros((LANES, 2 * gg * h), np.float32)
    rep_k[(colk // (gg * h)) * h + colk % h, colk] = 1.0
    mask_k = (row_g[:, None] == ((colk // h) % gg)[None, :]).astype(np.float32)
    cols = np.arange(2 * gg * p)
    rep_s = np.zeros((2 * p, 2 * gg * p), np.float32)
    rep_s[(cols // (gg * p)) * p + cols % p, cols] = 1.0
    mask_s = (row_g[:, None] == ((cols // p) % gg)[None, :]).astype(np.float32)
    return tuple(jnp.asarray(a, BF16) for a in (rep_k, mask_k, rep_s, mask_s))


def _ssm_spread(k2c, w2c, v2c, consts):
    nt = k2c.shape[0]
    rep_k, mask_k, rep_s, mask_s = consts
    nk, ns = rep_k.shape[1], rep_s.shape[1]

    def body(k_ref, w_ref, v_ref, rk_ref, mk_ref, rs_ref, ms_ref, ko_ref, wo_ref, vo_ref):
        for q in range(PAIRS):
            ko_ref[q] = (_dot(k_ref[q].astype(BF16), rk_ref[...]) * mk_ref[...]).astype(BF16)
            wo_ref[q] = (_dot(w_ref[q].astype(BF16), rs_ref[...]) * ms_ref[...]).astype(BF16)
            vo_ref[q] = (_dot(v_ref[q].astype(BF16), rs_ref[...]) * ms_ref[...]).astype(BF16)

    def tile(cols):
        return pl.BlockSpec((None, PAIRS, nk, cols), lambda t: (t, 0, 0, 0))

    def whole(a):
        return pl.BlockSpec(a.shape, lambda t: (0, 0))

    return _call(
        body,
        grid=(nt,),
        in_specs=[tile(LANES), tile(LANES), tile(LANES), whole(rep_k), whole(mask_k), whole(rep_s), whole(mask_s)],
        out_specs=[tile(nk), tile(ns), tile(ns)],
        out_shape=[
            jax.ShapeDtypeStruct((nt, PAIRS, nk, nk), BF16),
            jax.ShapeDtypeStruct((nt, PAIRS, nk, ns), BF16),
            jax.ShapeDtypeStruct((nt, PAIRS, nk, ns), BF16),
        ],
        compiler_params=_params(1),
        name="ssm_spread",
    )(k2c, w2c, v2c, rep_k, mask_k, rep_s, mask_s)


def _gather_blocks(full, mask, rep):
    m = full * mask
    hi = m.astype(BF16)
    lo = (m - hi.astype(F32)).astype(BF16)
    return _dot_nt(hi, rep) + _dot_nt(lo, rep)


def _in_proj(x, g, w4, tm, exs):
    s, d = x.shape
    nj, _, cw = w4.shape

    def body(x_ref, g_ref, w_ref, z_ref, h_ref):
        @pl.when(pl.program_id(1) == 0)
        def _():
            xv = x_ref[...]
            h_ref[...] = (xv * _rms(xv) * g_ref[...]).astype(BF16)

        z_ref[...] = _dot(h_ref[...], w_ref[...])

    return _call_riding(
        body,
        exs,
        scratch_shapes=[],
        args=(x, g, w4),
        grid=(s // tm, nj),
        in_specs=[
            pl.BlockSpec((tm, d), lambda i, j: (i, 0)),
            pl.BlockSpec((1, d), lambda i, j: (0, 0)),
            pl.BlockSpec((None, d, cw), lambda i, j: (j, 0, 0)),
        ],
        out_specs=[pl.BlockSpec((tm, cw), lambda i, j: (i, j)), pl.BlockSpec((tm, d), lambda i, j: (i, 0))],
        out_shape=[jax.ShapeDtypeStruct((s, nj * cw), F32), jax.ShapeDtypeStruct((s, d), BF16)],
        compiler_params=_params(2),
        name="in_proj",
    )


def _ssm_fwd(z, k2, w2, v2, al, exs):
    s = z.shape[0]
    nt = k2.shape[0]
    nc = s // CHUNK
    ns = w2.shape[-1]
    hs = ns // 2

    def body(u_ref, k_ref, w_ref, v_ref, al_ref, y_ref, xp_ref, us_ref, xloc):
        for q in range(PAIRS):
            us_ref[q, :, 0:LANES] = u_ref[pl.ds(2 * q, nc, stride=CHUNK), :].astype(BF16)
            us_ref[q, :, LANES : 2 * LANES] = u_ref[pl.ds(2 * q + 1, nc, stride=CHUNK), :].astype(BF16)
        acc = _dot(us_ref[0], w_ref[0])
        for q in range(1, PAIRS):
            acc = acc + _dot(us_ref[q], w_ref[q])
        xloc[...] = acc
        ar = al_ref[0:1, :]
        ai = al_ref[1:2, :]

        def step(c, carry):
            xr, xi = carry
            xp_ref[pl.ds(c, 1), 0:hs] = xr
            xp_ref[pl.ds(c, 1), hs:ns] = xi
            lr = xloc[pl.ds(c, 1), 0:hs]
            li = xloc[pl.ds(c, 1), hs:ns]
            return ar * xr - ai * xi + lr, ar * xi + ai * xr + li

        zero = jnp.zeros((1, hs), F32)
        lax.fori_loop(0, nc, step, (zero, zero))
        xpb = xp_ref[...].astype(BF16)
        for r in range(PAIRS):
            acc = _dot_nt(xpb, v_ref[r])
            for q in range(r + 1):
                acc = acc + _dot(us_ref[q], k_ref[r - q])
            y_ref[pl.ds(2 * r, nc, stride=CHUNK), :] = acc[:, 0:LANES]
            y_ref[pl.ds(2 * r + 1, nc, stride=CHUNK), :] = acc[:, LANES : 2 * LANES]

    return _call_riding(
        body,
        exs,
        args=(z, k2, w2, v2, al),
        grid=(nt,),
        in_specs=[
            pl.BlockSpec((s, LANES), lambda t: (0, t)),
            pl.BlockSpec((None, PAIRS, 2 * LANES, 2 * LANES), lambda t: (t, 0, 0, 0)),
            pl.BlockSpec((None, PAIRS, 2 * LANES, ns), lambda t: (t, 0, 0, 0)),
            pl.BlockSpec((None, PAIRS, 2 * LANES, ns), lambda t: (t, 0, 0, 0)),
            pl.BlockSpec((None, 2, hs), lambda t: (t, 0, 0)),
        ],
        out_specs=[
            pl.BlockSpec((s, LANES), lambda t: (0, t)),
            pl.BlockSpec((None, nc, ns), lambda t: (t, 0, 0)),
            pl.BlockSpec((None, PAIRS, nc, 2 * LANES), lambda t: (t, 0, 0, 0)),
        ],
        out_shape=[
            jax.ShapeDtypeStruct((s, nt * LANES), F32),
            jax.ShapeDtypeStruct((nt, nc, ns), F32),
            jax.ShapeDtypeStruct((nt, PAIRS, nc, 2 * LANES), BF16),
        ],
        scratch_shapes=[pltpu.VMEM((nc, ns), F32)],
        compiler_params=_params(1),
        name="ssm_fwd",
    )


def _glu_fwd(y_pre, wg, bg, go, d_model, tm):
    s, w = y_pre.shape

    def body(y_ref, wg_ref, bg_ref, go_ref, o_ref):
        yg = _gelu(y_ref[...])
        gate = _sigmoid(_dot(yg.astype(BF16), wg_ref[...]) + bg_ref[...])
        ys = yg * gate
        o_ref[...] = (ys * _rms(ys) * go_ref[...]).astype(BF16)

    return _call(
        body,
        grid=(s // tm,),
        in_specs=[
            pl.BlockSpec((tm, w), lambda i: (i, 0)),
            pl.BlockSpec((w, w), lambda i: (0, 0)),
            pl.BlockSpec((1, w), lambda i: (0, 0)),
            pl.BlockSpec((1, w), lambda i: (0, 0)),
        ],
        out_specs=pl.BlockSpec((tm, w), lambda i: (i, 0)),
        out_shape=jax.ShapeDtypeStruct((s, d_model), BF16),
        compiler_params=_params(1),
        name="glu_fwd",
    )(y_pre, wg, bg, go)


def _sgu_parts(zu, zv, lng, lnb, wt_ref, bias_ref):
    u, u_grad = _gelu_both(zu)
    v, v_grad = _gelu_both(zv)
    mu = jnp.mean(v, axis=-1, keepdims=True)
    vc = v - mu
    rstd = lax.rsqrt(jnp.mean(vc * vc, axis=-1, keepdims=True) + EPS)
    vhat = vc * rstd
    vb = (vhat * lng + lnb).astype(BF16)
    heads = wt_ref.shape[0]
    mix = jnp.concatenate(
        [_dot(wt_ref[h], vb[:, h * LANES : (h + 1) * LANES]) + bias_ref[h] for h in range(heads)], axis=1
    )
    return u, vhat, rstd, vb, mix, u_grad, v_grad


def _sgu_fwd(z, mixed, lng, lnb, wt, biasb, go, rb):
    s = z.shape[0]
    w = lng.shape[-1]
    heads = wt.shape[0]

    def body(zu_ref, zv_ref, m_any, lng_ref, lnb_ref, wt_ref, b_ref, go_ref, o_ref):
        del m_any
        for ck in range(rb // SGU_CHUNK):
            rows = slice(ck * SGU_CHUNK, (ck + 1) * SGU_CHUNK)
            u, _, _, _, mix, _, _ = _sgu_parts(zu_ref[rows, :], zv_ref[rows, :], lng_ref[...], lnb_ref[...], wt_ref, b_ref)
            y = u * mix
            o_ref[rows, :] = (y * _rms(y) * go_ref[...]).astype(BF16)

    return _call(
        body,
        grid=(s // rb,),
        in_specs=[
            pl.BlockSpec((rb, w), lambda i: (i, 1)),
            pl.BlockSpec((rb, w), lambda i: (i, 2)),
            pl.BlockSpec(memory_space=pl.ANY),
            pl.BlockSpec((1, w), lambda i: (0, 0)),
            pl.BlockSpec((1, w), lambda i: (0, 0)),
            pl.BlockSpec((heads, SGU_CHUNK, SGU_CHUNK), lambda i: (0, 0, 0)),
            pl.BlockSpec((heads, SGU_CHUNK, LANES), lambda i: (0, 0, 0)),
            pl.BlockSpec((1, w), lambda i: (0, 0)),
        ],
        out_specs=pl.BlockSpec((rb, w), lambda i: (i, 1)),
        out_shape=jax.ShapeDtypeStruct(mixed.shape, mixed.dtype),
        input_output_aliases={2: 0},
        compiler_params=_params(1),
        name="sgu_fwd",
    )(z, z, mixed, lng, lnb, wt, biasb, go)


def _out_proj(x, mixed, w_out, tm):
    s, d = x.shape

    def body(x_ref, m_ref, w_ref, o_ref):
        o_ref[...] = x_ref[...] + _dot(m_ref[...], w_ref[...])

    return _call(
        body,
        grid=(s // tm,),
        in_specs=[
            pl.BlockSpec((tm, d), lambda i: (i, 0)),
            pl.BlockSpec((tm, d), lambda i: (i, 0)),
            pl.BlockSpec((d, d), lambda i: (0, 0)),
        ],
        out_specs=pl.BlockSpec((tm, d), lambda i: (i, 0)),
        out_shape=jax.ShapeDtypeStruct((s, d), F32),
        compiler_params=_params(1),
        name="out_proj",
    )(x, mixed, w_out)


def _mlp_up(x1, g, w_up4, tm, tf, exs):
    s, d = x1.shape
    nj, _, cw = w_up4.shape
    per = cw // tf

    def body(x_ref, g_ref, wu_ref, up_ref, h_ref):
        @pl.when(pl.program_id(1) == 0)
        def _():
            xv = x_ref[...]
            h_ref[...] = (xv * _rms(xv) * g_ref[...]).astype(BF16)

        up_ref[...] = _dot(h_ref[...], wu_ref[...]).astype(BF16)

    return _call_riding(
        body,
        exs,
        scratch_shapes=[],
        args=(x1, g, w_up4),
        grid=(s // tm, nj * per),
        in_specs=[
            pl.BlockSpec((tm, d), lambda i, f: (i, 0)),
            pl.BlockSpec((1, d), lambda i, f: (0, 0)),
            pl.BlockSpec((None, d, tf), lambda i, f: (f // per, 0, f % per)),
        ],
        out_specs=[pl.BlockSpec((tm, tf), lambda i, f: (i, f)), pl.BlockSpec((tm, d), lambda i, f: (i, 0))],
        out_shape=[jax.ShapeDtypeStruct((s, nj * cw), BF16), jax.ShapeDtypeStruct((s, d), BF16)],
        compiler_params=_params(2),
        name="mlp_up",
    )


def _mlp_down(x1, up, w_down, tm, tf):
    s, d = x1.shape
    ff = w_down.shape[0]

    def body(x_ref, up_ref, wd_ref, x2_ref):
        @pl.when(pl.program_id(1) == 0)
        def _():
            x2_ref[...] = x_ref[...]

        a = jnp.maximum(up_ref[...].astype(F32), 0.0)
        x2_ref[...] += _dot((a * a).astype(BF16), wd_ref[...])

    return _call(
        body,
        grid=(s // tm, ff // tf),
        in_specs=[
            pl.BlockSpec((tm, d), lambda i, f: (i, 0)),
            pl.BlockSpec((tm, tf), lambda i, f: (i, f)),
            pl.BlockSpec((tf, d), lambda i, f: (f, 0)),
        ],
        out_specs=pl.BlockSpec((tm, d), lambda i, f: (i, 0)),
        out_shape=jax.ShapeDtypeStruct((s, d), F32),
        compiler_params=_params(2),
        name="mlp_down",
    )(x1, up, w_down)


def _final(x2, target, g, tm):
    s, d = x2.shape

    def body(x_ref, t_ref, g_ref, loss_ref, dx_ref, dxb_ref, dg_ref):
        @pl.when(pl.program_id(0) == 0)
        def _():
            loss_ref[...] = jnp.zeros_like(loss_ref)
            dg_ref[...] = jnp.zeros_like(dg_ref)

        xv = x_ref[...]
        r = _rms(xv)
        gv = g_ref[...]
        diff = xv * r * gv - t_ref[...]
        loss_ref[...] += 0.5 * jnp.sum(jnp.mean(diff * diff, axis=-1, keepdims=True), axis=0, keepdims=True)
        dx, dgt = _rms_bwd(diff * (1.0 / d), xv, r, gv)
        dx_ref[...] = dx
        dxb_ref[...] = dx.astype(BF16)
        dg_ref[...] += jnp.sum(dgt, axis=0, keepdims=True)

    return _call(
        body,
        grid=(s // tm,),
        in_specs=[
            pl.BlockSpec((tm, d), lambda i: (i, 0)),
            pl.BlockSpec((tm, d), lambda i: (i, 0)),
            pl.BlockSpec((1, d), lambda i: (0, 0)),
        ],
        out_specs=[
            pl.BlockSpec((1, 1), lambda i: (0, 0)),
            pl.BlockSpec((tm, d), lambda i: (i, 0)),
            pl.BlockSpec((tm, d), lambda i: (i, 0)),
            pl.BlockSpec((1, d), lambda i: (0, 0)),
        ],
        out_shape=[
            jax.ShapeDtypeStruct((1, 1), F32),
            jax.ShapeDtypeStruct((s, d), F32),
            jax.ShapeDtypeStruct((s, d), BF16),
            jax.ShapeDtypeStruct((1, d), F32),
        ],
        compiler_params=_params(1),
        name="final_loss",
    )(x2, target, g)


def _mlp_bwd(dx2b, up, w_up4, w_down, tm, tf):
    s, d = dx2b.shape
    ff = w_down.shape[0]
    cw = w_up4.shape[-1]
    per = cw // tf

    def body(dx_ref, up_ref, wu_ref, wd_ref, dup_ref, dh_ref):
        @pl.when(pl.program_id(1) == 0)
        def _():
            dh_ref[...] = jnp.zeros_like(dh_ref)

        dact = _dot_nt(dx_ref[...], wd_ref[...])
        dupb = (dact * (2.0 * jnp.maximum(up_ref[...].astype(F32), 0.0))).astype(BF16)
        dup_ref[...] = dupb
        dh_ref[...] += _dot_nt(dupb, wu_ref[...])

    return _call(
        body,
        grid=(s // tm, ff // tf),
        in_specs=[
            pl.BlockSpec((tm, d), lambda i, f: (i, 0)),
            pl.BlockSpec((tm, tf), lambda i, f: (i, f)),
            pl.BlockSpec((None, d, tf), lambda i, f: (f // per, 0, f % per)),
            pl.BlockSpec((tf, d), lambda i, f: (f, 0)),
        ],
        out_specs=[pl.BlockSpec((tm, tf), lambda i, f: (i, f)), pl.BlockSpec((tm, d), lambda i, f: (i, 0))],
        out_shape=[jax.ShapeDtypeStruct((s, ff), BF16), jax.ShapeDtypeStruct((s, d), F32)],
        compiler_params=_params(2),
        name="mlp_bwd",
    )(dx2b, up, w_up4, w_down)


def _norm_bwd(dh, x, dres, g, tm, name):
    s, d = x.shape

    def body(dh_ref, x_ref, dr_ref, g_ref, dx_ref, dxb_ref, dg_ref):
        @pl.when(pl.program_id(0) == 0)
        def _():
            dg_ref[...] = jnp.zeros_like(dg_ref)

        xv = x_ref[...]
        dx, dgt = _rms_bwd(dh_ref[...], xv, _rms(xv), g_ref[...])
        tot = dr_ref[...] + dx
        dx_ref[...] = tot
        dxb_ref[...] = tot.astype(BF16)
        dg_ref[...] += jnp.sum(dgt, axis=0, keepdims=True)

    row = pl.BlockSpec((tm, d), lambda i: (i, 0))
    vec = pl.BlockSpec((1, d), lambda i: (0, 0))
    return _call(
        body,
        grid=(s // tm,),
        in_specs=[row, row, row, vec],
        out_specs=[row, row, vec],
        out_shape=[
            jax.ShapeDtypeStruct((s, d), F32),
            jax.ShapeDtypeStruct((s, d), BF16),
            jax.ShapeDtypeStruct((1, d), F32),
        ],
        compiler_params=_params(1),
        name=name,
    )(dh, x, dres, g)


def _grad_w(a, b, out_dims, index_map, tm, tn, tk, relu2, name, exs=()):
    t, m = a.shape
    n = b.shape[1]
    nk = t // tk

    def body(a_ref, b_ref, o_ref, acc):
        k = pl.program_id(2)

        @pl.when(k == 0)
        def _():
            acc[...] = jnp.zeros_like(acc)

        av = a_ref[...]
        if relu2:
            r = jnp.maximum(av.astype(F32), 0.0)
            av = (r * r).astype(BF16)
        acc[...] += _dot_tn(av, b_ref[...])

        @pl.when(k == nk - 1)
        def _():
            o_ref[...] = acc[...].astype(BF16)

    spec = dict(
        grid=(m // tm, n // tn, nk),
        in_specs=[pl.BlockSpec((tk, tm), lambda i, j, k: (k, i)), pl.BlockSpec((tk, tn), lambda i, j, k: (k, j))],
        out_specs=[pl.BlockSpec((None, None, tm, tn), lambda i, j, k: index_map(i, j))],
        out_shape=[jax.ShapeDtypeStruct(out_dims, BF16)],
        scratch_shapes=[pltpu.VMEM((tm, tn), F32)],
        compiler_params=_params(3),
        name=name,
    )
    (g,), landed = _call_riding(body, exs, args=(a, b), **spec)
    return (g, *landed) if exs else g


def _out_proj_bwd(dx1b, w_out, tm):
    s, d = dx1b.shape

    def body(dx_ref, w_ref, o_ref):
        o_ref[...] = _dot_nt(dx_ref[...], w_ref[...])

    return _call(
        body,
        grid=(s // tm,),
        in_specs=[pl.BlockSpec((tm, d), lambda i: (i, 0)), pl.BlockSpec((d, d), lambda i: (0, 0))],
        out_specs=pl.BlockSpec((tm, d), lambda i: (i, 0)),
        out_shape=jax.ShapeDtypeStruct((s, d), F32),
        compiler_params=_params(1),
        name="out_proj_bwd",
    )(dx1b, w_out)


def _glu_bwd(y_pre, dmix, wg, bg, go, tm, exs):
    s, w = y_pre.shape
    n = s // tm

    def body(y_ref, dm_ref, wg_ref, bg_ref, go_ref, dy_ref, dwg_ref, dbg_ref, dgo_ref, acc):
        i = pl.program_id(0)

        @pl.when(i == 0)
        def _():
            acc[...] = jnp.zeros_like(acc)
            dbg_ref[...] = jnp.zeros_like(dbg_ref)
            dgo_ref[...] = jnp.zeros_like(dgo_ref)

        yp = y_ref[...]
        yg, yg_grad = _gelu_both(yp)
        ygb = yg.astype(BF16)
        gate = _sigmoid(_dot(ygb, wg_ref[...]) + bg_ref[...])
        ys = yg * gate
        dys, dgt = _rms_bwd(dm_ref[...], ys, _rms(ys), go_ref[...])
        dgo_ref[...] += jnp.sum(dgt, axis=0, keepdims=True)
        dpre = dys * yg * gate * (1.0 - gate)
        dbg_ref[...] += jnp.sum(dpre, axis=0, keepdims=True)
        dpb = dpre.astype(BF16)
        dyg = dys * gate + _dot_nt(dpb, wg_ref[...])
        dy_ref[...] = dyg * yg_grad
        acc[...] += _dot_tn(ygb, dpb)

        @pl.when(i == n - 1)
        def _():
            dwg_ref[...] = acc[...].astype(BF16)

    return _call_riding(
        body,
        exs,
        args=(y_pre, dmix, wg, bg, go),
        grid=(n,),
        in_specs=[
            pl.BlockSpec((tm, w), lambda i: (i, 0)),
            pl.BlockSpec((tm, w), lambda i: (i, 0)),
            pl.BlockSpec((w, w), lambda i: (0, 0)),
            pl.BlockSpec((1, w), lambda i: (0, 0)),
            pl.BlockSpec((1, w), lambda i: (0, 0)),
        ],
        out_specs=[
            pl.BlockSpec((tm, w), lambda i: (i, 0)),
            pl.BlockSpec((w, w), lambda i: (0, 0)),
            pl.BlockSpec((1, w), lambda i: (0, 0)),
            pl.BlockSpec((1, w), lambda i: (0, 0)),
        ],
        out_shape=[
            jax.ShapeDtypeStruct((s, w), F32),
            jax.ShapeDtypeStruct((w, w), BF16),
            jax.ShapeDtypeStruct((1, w), F32),
            jax.ShapeDtypeStruct((1, w), F32),
        ],
        scratch_shapes=[pltpu.VMEM((w, w), F32)],
        compiler_params=_params(1),
        name="glu_bwd",
    )


def _ssm_bwd_state(dy, v2, al, xprev):
    s = dy.shape[0]
    nt, nc, ns = xprev.shape
    hs = ns // 2

    def body(dy_ref, v_ref, al_ref, xp_ref, dys_ref, g_ref, dal_ref, dxp, gs):
        for q in range(PAIRS):
            dys_ref[q, :, 0:LANES] = dy_ref[pl.ds(2 * q, nc, stride=CHUNK), :].astype(BF16)
            dys_ref[q, :, LANES : 2 * LANES] = dy_ref[pl.ds(2 * q + 1, nc, stride=CHUNK), :].astype(BF16)
        acc = _dot(dys_ref[0], v_ref[0])
        for r in range(1, PAIRS):
            acc = acc + _dot(dys_ref[r], v_ref[r])
        dxp[...] = acc
        ar = al_ref[0:1, :]
        ai = al_ref[1:2, :]
        zero = jnp.zeros((1, hs), F32)
        gs[pl.ds(nc - 1, 1), 0:hs] = zero
        gs[pl.ds(nc - 1, 1), hs:ns] = zero

        def step(n, carry):
            gr, gi = carry
            c = nc - 2 - n
            nr = dxp[pl.ds(c + 1, 1), 0:hs] + ar * gr + ai * gi
            ni = dxp[pl.ds(c + 1, 1), hs:ns] + ar * gi - ai * gr
            gs[pl.ds(c, 1), 0:hs] = nr
            gs[pl.ds(c, 1), hs:ns] = ni
            return nr, ni

        lax.fori_loop(0, nc - 1, step, (zero, zero))
        gv = gs[...]
        xv = xp_ref[...]
        gr, gi = gv[:, 0:hs], gv[:, hs:ns]
        xr, xi = xv[:, 0:hs], xv[:, hs:ns]
        dal_ref[0:1, :] = jnp.sum(gr * xr + gi * xi, axis=0, keepdims=True)
        dal_ref[1:2, :] = jnp.sum(gi * xr - gr * xi, axis=0, keepdims=True)
        g_ref[...] = gv.astype(BF16)

    return _call(
        body,
        grid=(nt,),
        in_specs=[
            pl.BlockSpec((s, LANES), lambda t: (0, t)),
            pl.BlockSpec((None, PAIRS, 2 * LANES, ns), lambda t: (t, 0, 0, 0)),
            pl.BlockSpec((None, 2, hs), lambda t: (t, 0, 0)),
            pl.BlockSpec((None, nc, ns), lambda t: (t, 0, 0)),
        ],
        out_specs=[
            pl.BlockSpec((None, PAIRS, nc, 2 * LANES), lambda t: (t, 0, 0, 0)),
            pl.BlockSpec((None, nc, ns), lambda t: (t, 0, 0)),
            pl.BlockSpec((None, 2, hs), lambda t: (t, 0, 0)),
        ],
        out_shape=[
            jax.ShapeDtypeStruct((nt, PAIRS, nc, 2 * LANES), BF16),
            jax.ShapeDtypeStruct((nt, nc, ns), BF16),
            jax.ShapeDtypeStruct((nt, 2, hs), F32),
        ],
        scratch_shapes=[pltpu.VMEM((nc, ns), F32), pltpu.VMEM((nc, ns), F32)],
        compiler_params=_params(1),
        name="ssm_bwd_state",
    )(dy, v2, al, xprev)


def _ssm_bwd_main(us, dys, k2, w2, xprev, gst, consts, s, width, exs):
    nt, _, nc, _ = us.shape
    ns = xprev.shape[-1]
    rep_k, mask_k, rep_s, mask_s = consts
    nk = 2 * LANES

    def body(us_ref, dys_ref, k_ref, w_ref, xp_ref, g_ref, rk_ref, mk_ref, rs_ref, ms_ref,
             du_ref, dk_ref, dw_ref, dv_ref, acc, du2, duf, dkf):
        q = pl.program_id(1)

        @pl.when(q == 0)
        def _():
            dkf[...] = jnp.zeros_like(dkf)

        usq = us_ref[q]
        gb = g_ref[...]
        dw_ref[...] = _gather_blocks(_dot_tn(usq, gb), ms_ref[...], rs_ref[...])
        dv_ref[...] = _gather_blocks(_dot_tn(dys_ref[q], xp_ref[...].astype(BF16)), ms_ref[...], rs_ref[...])
        acc[...] = _dot_nt(gb, w_ref[...])
        for m in range(PAIRS):

            @pl.when(q + m < PAIRS)
            def _():
                dyr = dys_ref[jnp.minimum(q + m, PAIRS - 1)]
                acc[...] += _dot_nt(dyr, k_ref[m])
                dkf[m] += _dot_tn(usq, dyr)

        du2[q] = acc[...]

        @pl.when(q == PAIRS - 1)
        def _():
            for p in range(PAIRS):
                duf[pl.ds(2 * p, nc, stride=CHUNK), :] = du2[p, :, 0:LANES]
                duf[pl.ds(2 * p + 1, nc, stride=CHUNK), :] = du2[p, :, LANES : 2 * LANES]
                dk_ref[p] = _gather_blocks(dkf[p], mk_ref[...], rk_ref[...])
            du_ref[...] = duf[...].astype(BF16)

    def per_tile(rows, cols):
        return pl.BlockSpec((None, PAIRS, rows, cols), lambda t, q: (t, 0, 0, 0))

    def per_pair(rows, cols):
        return pl.BlockSpec((None, None, rows, cols), lambda t, q: (t, q, 0, 0))

    def whole(a):
        return pl.BlockSpec(a.shape, lambda t, q: (0, 0))

    return _call_riding(
        body,
        exs,
        args=(us, dys, k2, w2, xprev, gst, rep_k, mask_k, rep_s, mask_s),
        grid=(nt, PAIRS),
        in_specs=[
            per_tile(nc, nk),
            per_tile(nc, nk),
            per_tile(nk, nk),
            per_pair(nk, ns),
            pl.BlockSpec((None, nc, ns), lambda t, q: (t, 0, 0)),
            pl.BlockSpec((None, nc, ns), lambda t, q: (t, 0, 0)),
            whole(rep_k),
            whole(mask_k),
            whole(rep_s),
            whole(mask_s),
        ],
        out_specs=[
            pl.BlockSpec((s, LANES), lambda t, q: (0, t)),
            per_tile(nk, LANES),
            per_pair(nk, LANES),
            per_pair(nk, LANES),
        ],
        out_shape=[
            jax.ShapeDtypeStruct((s, width), BF16),
            jax.ShapeDtypeStruct((nt, PAIRS, nk, LANES), F32),
            jax.ShapeDtypeStruct((nt, PAIRS, nk, LANES), F32),
            jax.ShapeDtypeStruct((nt, PAIRS, nk, LANES), F32),
        ],
        scratch_shapes=[
            pltpu.VMEM((nc, nk), F32),
            pltpu.VMEM((PAIRS, nc, nk), F32),
            pltpu.VMEM((s, LANES), F32),
            pltpu.VMEM((PAIRS, nk, nk), F32),
        ],
        compiler_params=_params(2),
        name="ssm_bwd_main",
    )


def _sgu_bwd(z, dmix, dz, lng, lnb, wt, wtt, biasb, go, rb):
    s = z.shape[0]
    w = lng.shape[-1]
    heads = wt.shape[0]
    ncol = (LANES, LANES)

    def body(zu_ref, zv_ref, dm_ref, dz_any, lng_ref, lnb_ref, wt_ref, wtt_ref, b_ref, go_ref,
             dz_ref, dw_ref, db_ref, dlg_ref, dlb_ref, dgo_ref, dzv):
        del dz_any
        i = pl.program_id(0)
        p = pl.program_id(1)

        @pl.when(jnp.logical_and(i == 0, p == 0))
        def _():
            dw_ref[...] = jnp.zeros_like(dw_ref)
            db_ref[...] = jnp.zeros_like(db_ref)
            dlg_ref[...] = jnp.zeros_like(dlg_ref)
            dlb_ref[...] = jnp.zeros_like(dlb_ref)
            dgo_ref[...] = jnp.zeros_like(dgo_ref)

        @pl.when(p == 0)
        def _():
            lng_v = lng_ref[...]
            for ck in range(rb // SGU_CHUNK):
                rows = slice(ck * SGU_CHUNK, (ck + 1) * SGU_CHUNK)
                zu = zu_ref[rows, :]
                zv = zv_ref[rows, :]
                u, vhat, rstd, vb, mix, u_grad, v_grad = _sgu_parts(zu, zv, lng_v, lnb_ref[...], wt_ref, b_ref)
                y = u * mix
                dy, dgt = _rms_bwd(dm_ref[rows, :], y, _rms(y), go_ref[...])
                dgo_ref[...] += jnp.sum(dgt, axis=0, keepdims=True)
                du = dy * mix
                dmx = dy * u
                dmb = dmx.astype(BF16)
                dvl = []
                for h in range(heads):
                    cols = slice(h * LANES, (h + 1) * LANES)
                    db_ref[h] += jnp.broadcast_to(jnp.sum(dmx[:, cols], axis=-1, keepdims=True), ncol)
                    dw_ref[h] += _dot_nt(dmb[:, cols], vb[:, cols])
                    dvl.append(_dot(wtt_ref[h], dmb[:, cols]))
                dvln = jnp.concatenate(dvl, axis=1)
                dlg_ref[...] += jnp.sum(dvln * vhat, axis=0, keepdims=True)
                dlb_ref[...] += jnp.sum(dvln, axis=0, keepdims=True)
                dvh = dvln * lng_v
                dv = rstd * (
                    dvh
                    - jnp.mean(dvh, axis=-1, keepdims=True)
                    - vhat * jnp.mean(dvh * vhat, axis=-1, keepdims=True)
                )
                dz_ref[rows, :] = (du * u_grad).astype(BF16)
                dzv[rows, :] = (dv * v_grad).astype(BF16)

        @pl.when(p == 1)
        def _():
            dz_ref[...] = dzv[...]

    return _call(
        body,
        grid=(s // rb, 2),
        in_specs=[
            pl.BlockSpec((rb, w), lambda i, p: (i, 1)),
            pl.BlockSpec((rb, w), lambda i, p: (i, 2)),
            pl.BlockSpec((rb, w), lambda i, p: (i, 1)),
            pl.BlockSpec(memory_space=pl.ANY),
            pl.BlockSpec((1, w), lambda i, p: (0, 0)),
            pl.BlockSpec((1, w), lambda i, p: (0, 0)),
            pl.BlockSpec((heads, SGU_CHUNK, SGU_CHUNK), lambda i, p: (0, 0, 0)),
            pl.BlockSpec((heads, SGU_CHUNK, SGU_CHUNK), lambda i, p: (0, 0, 0)),
            pl.BlockSpec((heads, SGU_CHUNK, LANES), lambda i, p: (0, 0, 0)),
            pl.BlockSpec((1, w), lambda i, p: (0, 0)),
        ],
        out_specs=[
            pl.BlockSpec((rb, w), lambda i, p: (i, 1 + p)),
            pl.BlockSpec((heads, SGU_CHUNK, SGU_CHUNK), lambda i, p: (0, 0, 0)),
            pl.BlockSpec((heads, SGU_CHUNK, LANES), lambda i, p: (0, 0, 0)),
            pl.BlockSpec((1, w), lambda i, p: (0, 0)),
            pl.BlockSpec((1, w), lambda i, p: (0, 0)),
            pl.BlockSpec((1, w), lambda i, p: (0, 0)),
        ],
        out_shape=[
            jax.ShapeDtypeStruct(dz.shape, dz.dtype),
            jax.ShapeDtypeStruct((heads, SGU_CHUNK, SGU_CHUNK), F32),
            jax.ShapeDtypeStruct((heads, SGU_CHUNK, LANES), F32),
            jax.ShapeDtypeStruct((1, w), F32),
            jax.ShapeDtypeStruct((1, w), F32),
            jax.ShapeDtypeStruct((1, w), F32),
        ],
        scratch_shapes=[pltpu.VMEM((rb, w), BF16)],
        input_output_aliases={3: 0},
        compiler_params=_params(2),
        name="sgu_bwd",
    )(z, z, dmix, dz, lng, lnb, wt, wtt, biasb, go)


def _in_proj_bwd(dz, w4, x, dres, g, tm, exs):
    s, d = x.shape
    nj, _, cw = w4.shape

    def body(dz_ref, w_ref, x_ref, dr_ref, g_ref, dx_ref, dg_ref, acc):
        i = pl.program_id(0)
        j = pl.program_id(1)

        @pl.when(j == 0)
        def _():
            acc[...] = jnp.zeros_like(acc)

        @pl.when(jnp.logical_and(i == 0, j == 0))
        def _():
            dg_ref[...] = jnp.zeros_like(dg_ref)

        acc[...] += _dot_nt(dz_ref[...], w_ref[...])

        @pl.when(j == nj - 1)
        def _():
            xv = x_ref[...]
            dx, dgt = _rms_bwd(acc[...], xv, _rms(xv), g_ref[...])
            dx_ref[...] = dr_ref[...] + dx
            dg_ref[...] += jnp.sum(dgt, axis=0, keepdims=True)

    row = pl.BlockSpec((tm, d), lambda i, j: (i, 0))
    vec = pl.BlockSpec((1, d), lambda i, j: (0, 0))
    return _call_riding(
        body,
        exs,
        args=(dz, w4, x, dres, g),
        scratch_shapes=[pltpu.VMEM((tm, d), F32)],
        grid=(s // tm, nj),
        in_specs=[
            pl.BlockSpec((tm, cw), lambda i, j: (i, j)),
            pl.BlockSpec((None, d, cw), lambda i, j: (j, 0, 0)),
            row,
            row,
            vec,
        ],
        out_specs=[row, vec],
        out_shape=[jax.ShapeDtypeStruct((s, d), F32), jax.ShapeDtypeStruct((1, d), F32)],
        compiler_params=_params(2),
        name="in_proj_bwd",
    )


_ANY = pl.BlockSpec(memory_space=pl.ANY)


def _position():
    x, y, c = lax.axis_index("x"), lax.axis_index("y"), lax.axis_index("c")
    return x, y, c, [(1 - x, y), (x, 1 - y), (1 - x, 1 - y)]


def _remote(src, dst, send_sems, recv_sems, k, to):
    return pltpu.make_async_remote_copy(
        src_ref=src, dst_ref=dst, send_sem=send_sems.at[k], recv_sem=recv_sems.at[k], device_id=to, device_id_type=MESH
    )


class _Riding:
    def __init__(self, srcs, out_shapes, n_sems, start, finish):
        self.srcs, self.out_shapes, self.n_sems, self.start, self.finish = srcs, out_shapes, n_sems, start, finish


def _call_riding(body, exs, *, grid, in_specs, out_specs, out_shape, scratch_shapes, args, **kw):
    n_in, n_out, n_scr = len(in_specs), len(out_specs), len(scratch_shapes)
    spec = dict(grid=grid, in_specs=list(in_specs), out_specs=list(out_specs), out_shape=list(out_shape))
    if not exs:
        return _call(body, scratch_shapes=list(scratch_shapes), **spec, **kw)(*args), []
    srcs = [a for ex in exs for a in ex.srcs]
    lands = [a for ex in exs for a in ex.out_shapes]
    xi, xo = len(srcs), len(lands)

    def fused(*refs):
        cin, xin = refs[:n_in], refs[n_in : n_in + xi]
        o = n_in + xi
        cout, xout = refs[o : o + n_out], refs[o + n_out : o + n_out + xo]
        scr = refs[o + n_out + xo :]
        ids = [pl.program_id(a) for a in range(len(grid))]
        first = functools.reduce(jnp.logical_and, [i == 0 for i in ids])
        last = functools.reduce(jnp.logical_and, [i == n - 1 for i, n in zip(ids, grid)])

        def each(half):
            si = so = 0
            for e, ex in enumerate(exs):
                ni, no = len(ex.srcs), len(ex.out_shapes)
                sems = scr[n_scr + 2 * e : n_scr + 2 * e + 2]
                getattr(ex, half)(xin[si : si + ni], xout[so : so + no], *sems)
                si, so = si + ni, so + no

        @pl.when(first)
        def _():
            each("start")

        body(*cin, *cout, *scr[:n_scr])

        @pl.when(last)
        def _():
            each("finish")

    sems = [pltpu.SemaphoreType.DMA((ex.n_sems,)) for ex in exs for _ in range(2)]
    spec["in_specs"] += [_ANY] * xi
    spec["out_specs"] += [_ANY] * xo
    spec["out_shape"] += lands
    outs = _call(fused, scratch_shapes=list(scratch_shapes) + sems, **spec, **kw)(*args, *srcs)
    return outs[:n_out], outs[n_out:]


def _gather_one(shard):
    def own_copies(ins, outs, send, recv):
        x, y, c, chips = _position()
        me = 2 * x + y
        sib = (x, y, 1 - c)
        cps = [_remote(ins[0].at[c], outs[0].at[me, c], send, recv, 0, sib)]
        cps += [_remote(ins[0].at[c], outs[0].at[me, c], send, recv, 1 + k, (px, py, c)) for k, (px, py) in enumerate(chips)]
        return cps, (c, me, sib, chips)

    def start(ins, outs, send, recv):
        for cp in own_copies(ins, outs, send, recv)[0]:
            cp.start()

    def finish(ins, outs, send, recv):
        cps, (c, me, sib, chips) = own_copies(ins, outs, send, recv)
        for k, (px, py) in enumerate(chips):
            landed = outs[0].at[2 * px + py, c]
            _remote(ins[0].at[c], landed, send, recv, 1 + k, (px, py, c)).wait_recv()
            cps.append(_remote(landed, landed, send, recv, 4 + k, sib))
            cps[-1].start()
        _remote(ins[0].at[1 - c], outs[0].at[me, 1 - c], send, recv, 0, sib).wait_recv()
        for k, (px, py) in enumerate(chips):
            theirs = outs[0].at[2 * px + py, 1 - c]
            _remote(theirs, theirs, send, recv, 4 + k, sib).wait_recv()
        for cp in cps:
            cp.wait_send()

    return _Riding([shard], [jax.ShapeDtypeStruct((N_CHIPS,) + shard.shape, shard.dtype)], 7, start, finish)


def _to_owner_one(p):
    def copies(ins, outs, send, recv):
        x, y, c, chips = _position()
        return [_remote(ins[0].at[2 * px + py], outs[0].at[k], send, recv, k, (px, py, c)) for k, (px, py) in enumerate(chips)]

    def start(ins, outs, send, recv):
        for cp in copies(ins, outs, send, recv):
            cp.start()

    def finish(ins, outs, send, recv):
        cps = copies(ins, outs, send, recv)
        for cp in cps:
            cp.wait_recv()
        for cp in cps:
            cp.wait_send()

    return _Riding([p], [jax.ShapeDtypeStruct((3,) + p.shape[1:], p.dtype)], 3, start, finish)


def _gather_chips(arrs, name):
    n = len(arrs)

    def body(*refs):
        ins, outs = refs[:n], refs[n : 2 * n]
        send_sems, recv_sems, local_sems = refs[2 * n :]
        x, y, c, chips = _position()
        me = 2 * x + y
        local = [pltpu.make_async_copy(ins[i], outs[i].at[me], local_sems.at[i]) for i in range(n)]
        for cp in local:
            cp.start()
        sends = []
        for k, (px, py) in enumerate(chips):
            for i in range(n):
                sends.append(_remote(ins[i], outs[i].at[me], send_sems, recv_sems, k * n + i, (px, py, c)))
                sends[-1].start()
        for k, (px, py) in enumerate(chips):
            for i in range(n):
                _remote(ins[i], outs[i].at[2 * px + py], send_sems, recv_sems, k * n + i, (px, py, c)).wait_recv()
        for cp in sends:
            cp.wait_send()
        for cp in local:
            cp.wait()

    return _call(
        body,
        in_specs=[_ANY] * n,
        out_specs=[_ANY] * n,
        out_shape=[jax.ShapeDtypeStruct((N_CHIPS,) + a.shape, a.dtype) for a in arrs],
        scratch_shapes=[
            pltpu.SemaphoreType.DMA((3 * n,)),
            pltpu.SemaphoreType.DMA((3 * n,)),
            pltpu.SemaphoreType.DMA((n,)),
        ],
        name=name,
    )(*arrs)


def _pair_exchange(gs, name):
    n = len(gs)

    def body(*refs):
        ins, outs = refs[:n], refs[n : 2 * n]
        send_sems, recv_sems = refs[2 * n :]
        x, y, c, _ = _position()
        sib = (x, y, 1 - c)
        cps = []
        for i in range(n):
            for j in range(N_CHIPS):
                cps.append(_remote(ins[i].at[j, 1 - c], outs[i].at[j], send_sems, recv_sems, i * N_CHIPS + j, sib))
                cps[-1].start()
        for cp in cps:
            cp.wait_recv()
        for cp in cps:
            cp.wait_send()

    return _call(
        body,
        in_specs=[_ANY] * n,
        out_specs=[_ANY] * n,
        out_shape=[jax.ShapeDtypeStruct((g.shape[0],) + g.shape[2:], g.dtype) for g in gs],
        scratch_shapes=[pltpu.SemaphoreType.DMA((n * N_CHIPS,)), pltpu.SemaphoreType.DMA((n * N_CHIPS,))],
        name=name,
    )(*gs)


def _gather_halves(shards, name):
    n = len(shards)

    def body(*refs):
        ins, outs = refs[:n], refs[n : 2 * n]
        ici_send, ici_recv, d2d_send, d2d_recv = refs[2 * n :]
        x, y, c, chips = _position()
        me = 2 * x + y
        sib = (x, y, 1 - c)
        started = []
        for i in range(n):
            started.append(_remote(ins[i].at[c], outs[i].at[me, c], d2d_send, d2d_recv, i, sib))
            started[-1].start()
        for k, (px, py) in enumerate(chips):
            for i in range(n):
                started.append(_remote(ins[i].at[c], outs[i].at[me, c], ici_send, ici_recv, k * n + i, (px, py, c)))
                started[-1].start()
        for k, (px, py) in enumerate(chips):
            for i in range(n):
                landed = outs[i].at[2 * px + py, c]
                _remote(ins[i].at[c], landed, ici_send, ici_recv, k * n + i, (px, py, c)).wait_recv()
                started.append(_remote(landed, landed, d2d_send, d2d_recv, (k + 1) * n + i, sib))
                started[-1].start()
        for i in range(n):
            _remote(ins[i].at[1 - c], outs[i].at[me, 1 - c], d2d_send, d2d_recv, i, sib).wait_recv()
        for k, (px, py) in enumerate(chips):
            for i in range(n):
                theirs = outs[i].at[2 * px + py, 1 - c]
                _remote(theirs, theirs, d2d_send, d2d_recv, (k + 1) * n + i, sib).wait_recv()
        for cp in started:
            cp.wait_send()

    return _call(
        body,
        in_specs=[_ANY] * n,
        out_specs=[_ANY] * n,
        out_shape=[jax.ShapeDtypeStruct((N_CHIPS,) + a.shape, a.dtype) for a in shards],
        scratch_shapes=[
            pltpu.SemaphoreType.DMA((3 * n,)),
            pltpu.SemaphoreType.DMA((3 * n,)),
            pltpu.SemaphoreType.DMA((4 * n,)),
            pltpu.SemaphoreType.DMA((4 * n,)),
        ],
        name=name,
    )(*shards)


def _place_own(shard, gathered, pos, name):
    _, hr, cols = shard.shape
    tr = _row_tile(hr, 512)

    def body(pos_ref, s_ref, g_any, o_ref):
        del pos_ref, g_any
        o_ref[...] = s_ref[...]

    return _call(
        body,
        grid_spec=pltpu.PrefetchScalarGridSpec(
            num_scalar_prefetch=1,
            grid=(hr // tr,),
            in_specs=[pl.BlockSpec((None, tr, cols), lambda i, p: (p[1], i, 0)), _ANY],
            out_specs=pl.BlockSpec((None, None, tr, cols), lambda i, p: (p[0], p[1], i, 0)),
        ),
        out_shape=jax.ShapeDtypeStruct(gathered.shape, gathered.dtype),
        input_output_aliases={2: 0},
        compiler_params=_params(1),
        name=name,
    )(pos, shard, gathered)


def _chip_exchange(ps, name):
    n = len(ps)

    def body(*refs):
        ins, outs = refs[:n], refs[n : 2 * n]
        send_sems, recv_sems = refs[2 * n :]
        x, y, c, chips = _position()
        sends = []
        for k, (px, py) in enumerate(chips):
            for i in range(n):
                sends.append(_remote(ins[i].at[2 * px + py], outs[i].at[k], send_sems, recv_sems, k * n + i, (px, py, c)))
                sends[-1].start()
        for cp in sends:
            cp.wait_recv()
        for cp in sends:
            cp.wait_send()

    return _call(
        body,
        in_specs=[_ANY] * n,
        out_specs=[_ANY] * n,
        out_shape=[jax.ShapeDtypeStruct((3,) + p.shape[1:], p.dtype) for p in ps],
        scratch_shapes=[pltpu.SemaphoreType.DMA((3 * n,)), pltpu.SemaphoreType.DMA((3 * n,))],
        name=name,
    )(*ps)


def _pair_share(fs, name):
    n = len(fs)

    def body(*refs):
        ins, outs = refs[:n], refs[n : 2 * n]
        send_sems, recv_sems = refs[2 * n :]
        x, y, c, _ = _position()
        sib = (x, y, 1 - c)
        sends = [_remote(ins[i].at[c], outs[i].at[c], send_sems, recv_sems, i, sib) for i in range(n)]
        for cp in sends:
            cp.start()
        for i in range(n):
            _remote(ins[i].at[1 - c], outs[i].at[1 - c], send_sems, recv_sems, i, sib).wait_recv()
        for cp in sends:
            cp.wait_send()

    return _call(
        body,
        in_specs=[_ANY] * n,
        out_specs=[_ANY] * n,
        out_shape=[jax.ShapeDtypeStruct(f.shape, f.dtype) for f in fs],
        input_output_aliases={i: i for i in range(n)},
        scratch_shapes=[pltpu.SemaphoreType.DMA((n,)), pltpu.SemaphoreType.DMA((n,))],
        name=name,
    )(*fs)


def _pair_sum(g, r, core, name):
    nj, _, hr, cols = g.shape
    tr = _row_tile(hr, 256)

    def body(c_ref, g_ref, r_ref, o_ref):
        del c_ref
        o_ref[...] = (g_ref[...].astype(F32) + r_ref[...].astype(F32)).astype(o_ref.dtype)

    return _call(
        body,
        grid_spec=pltpu.PrefetchScalarGridSpec(
            num_scalar_prefetch=1,
            grid=(nj, hr // tr),
            in_specs=[
                pl.BlockSpec((None, None, tr, cols), lambda j, i, c: (j, c[0], i, 0)),
                pl.BlockSpec((None, tr, cols), lambda j, i, c: (j, i, 0)),
            ],
            out_specs=pl.BlockSpec((None, tr, cols), lambda j, i, c: (j, i, 0)),
        ),
        out_shape=jax.ShapeDtypeStruct((nj, hr, cols), g.dtype),
        compiler_params=_params(2),
        name=name,
    )(core, g, r)


def _chip_sum(p, q, pos, name):
    nq, hr, cols = q.shape
    tr = _row_tile(hr, 256)

    def body(pos_ref, p_ref, *refs):
        del pos_ref
        o_ref = refs[nq]
        tot = p_ref[...].astype(F32)
        for k in range(nq):
            tot = tot + refs[k][...].astype(F32)
        o_ref[...] = tot

    return _call(
        body,
        grid_spec=pltpu.PrefetchScalarGridSpec(
            num_scalar_prefetch=1,
            grid=(hr // tr,),
            in_specs=[pl.BlockSpec((None, tr, cols), lambda i, s: (s[0], i, 0))]
            + [pl.BlockSpec((None, tr, cols), functools.partial(lambda k, i, s: (k, i, 0), k)) for k in range(nq)],
            out_specs=pl.BlockSpec((None, tr, cols), lambda i, s: (s[1], i, 0)),
        ),
        out_shape=jax.ShapeDtypeStruct((2, hr, cols), F32),
        compiler_params=_params(1),
        name=name,
    )(pos, p, *([q] * nq))


def _adamw(w, g, m, v, name):
    rows, cols = w.shape
    tr = _row_tile(rows, max(8, (2**18) // cols), mult=8)
    c1 = 1.0 - ADAM_B1**ADAM_STEP
    c2 = 1.0 - ADAM_B2**ADAM_STEP

    def body(w_ref, g_ref, m_ref, v_ref, d_ref, nm_ref, nv_ref):
        gv = g_ref[...]
        nm = ADAM_B1 * m_ref[...] + (1.0 - ADAM_B1) * gv
        nv = ADAM_B2 * v_ref[...] + (1.0 - ADAM_B2) * (gv * gv)
        nm_ref[...] = nm
        nv_ref[...] = nv
        d_ref[...] = -ADAM_LR * ((nm / c1) / (jnp.sqrt(nv / c2) + ADAM_EPS) + ADAM_WD * w_ref[...])

    spec = pl.BlockSpec((tr, cols), lambda i: (i, 0))
    sds = jax.ShapeDtypeStruct((rows, cols), F32)
    return _call(
        body,
        grid=(rows // tr,),
        in_specs=[spec] * 4,
        out_specs=[spec] * 3,
        out_shape=[sds] * 3,
        compiler_params=_params(1),
        name=name,
    )(w, g, m, v)


_TILE_ELEMS = 8 * LANES
_FLAT_ROW_MULT = 8 * 2 * N_CHIPS


def _flat_rows(shape):
    n = math.prod(shape)
    return (n + _TILE_ELEMS - 1) // _TILE_ELEMS * 8


def _pack(arrs):
    parts = []
    for a in arrs:
        rows = _flat_rows(a.shape)
        flat = a.reshape(-1).astype(F32)
        flat = jnp.pad(flat, (0, rows * LANES - flat.shape[0]))
        parts.append(flat.reshape(rows, LANES))
    total = sum(p.shape[0] for p in parts)
    pad = -total % _FLAT_ROW_MULT
    if pad:
        parts.append(jnp.zeros((pad, LANES), F32))
    return jnp.concatenate(parts, axis=0)


def _unpack(flat, shapes):
    out, row = [], 0
    for shp in shapes:
        rows = _flat_rows(shp)
        out.append(flat[row : row + rows].reshape(-1)[: math.prod(shp)].reshape(shp))
        row += rows
    return out


def kernel(x, norm_mix_g, w_in, ssm_a_re, ssm_a_im, ssm_b_re, ssm_b_im, ssm_c_re, ssm_c_im, ssm_d, ssm_log_dt, ssm_glu_w, ssm_glu_b, sgu_ln_g, sgu_ln_b, sgu_w, sgu_b, out_norm_ssm_g, out_norm_sgu_g, w_out, norm_mlp_g, w_up, w_down, norm_final_g, loss_target, m_norm_mix_g, m_w_in, m_ssm_a_re, m_ssm_a_im, m_ssm_b_re, m_ssm_b_im, m_ssm_c_re, m_ssm_c_im, m_ssm_d, m_ssm_log_dt, m_ssm_glu_w, m_ssm_glu_b, m_sgu_ln_g, m_sgu_ln_b, m_sgu_w, m_sgu_b, m_out_norm_ssm_g, m_out_norm_sgu_g, m_w_out, m_norm_mlp_g, m_w_up, m_w_down, m_norm_final_g, v_norm_mix_g, v_w_in, v_ssm_a_re, v_ssm_a_im, v_ssm_b_re, v_ssm_b_im, v_ssm_c_re, v_ssm_c_im, v_ssm_d, v_ssm_log_dt, v_ssm_glu_w, v_ssm_glu_b, v_sgu_ln_g, v_sgu_ln_b, v_sgu_w, v_sgu_b, v_out_norm_ssm_g, v_out_norm_sgu_g, v_w_out, v_norm_mlp_g, v_w_up, v_w_down, v_norm_final_g):
    weights = dict(norm_mix_g=norm_mix_g, w_in=w_in, ssm_a_re=ssm_a_re, ssm_a_im=ssm_a_im, ssm_b_re=ssm_b_re, ssm_b_im=ssm_b_im, ssm_c_re=ssm_c_re, ssm_c_im=ssm_c_im, ssm_d=ssm_d, ssm_log_dt=ssm_log_dt, ssm_glu_w=ssm_glu_w, ssm_glu_b=ssm_glu_b, sgu_ln_g=sgu_ln_g, sgu_ln_b=sgu_ln_b, sgu_w=sgu_w, sgu_b=sgu_b, out_norm_ssm_g=out_norm_ssm_g, out_norm_sgu_g=out_norm_sgu_g, w_out=w_out, norm_mlp_g=norm_mlp_g, w_up=w_up, w_down=w_down, norm_final_g=norm_final_g)
    mom_m = dict(norm_mix_g=m_norm_mix_g, w_in=m_w_in, ssm_a_re=m_ssm_a_re, ssm_a_im=m_ssm_a_im, ssm_b_re=m_ssm_b_re, ssm_b_im=m_ssm_b_im, ssm_c_re=m_ssm_c_re, ssm_c_im=m_ssm_c_im, ssm_d=m_ssm_d, ssm_log_dt=m_ssm_log_dt, ssm_glu_w=m_ssm_glu_w, ssm_glu_b=m_ssm_glu_b, sgu_ln_g=m_sgu_ln_g, sgu_ln_b=m_sgu_ln_b, sgu_w=m_sgu_w, sgu_b=m_sgu_b, out_norm_ssm_g=m_out_norm_ssm_g, out_norm_sgu_g=m_out_norm_sgu_g, w_out=m_w_out, norm_mlp_g=m_norm_mlp_g, w_up=m_w_up, w_down=m_w_down, norm_final_g=m_norm_final_g)
    mom_v = dict(norm_mix_g=v_norm_mix_g, w_in=v_w_in, ssm_a_re=v_ssm_a_re, ssm_a_im=v_ssm_a_im, ssm_b_re=v_ssm_b_re, ssm_b_im=v_ssm_b_im, ssm_c_re=v_ssm_c_re, ssm_c_im=v_ssm_c_im, ssm_d=v_ssm_d, ssm_log_dt=v_ssm_log_dt, ssm_glu_w=v_ssm_glu_w, ssm_glu_b=v_ssm_glu_b, sgu_ln_g=v_sgu_ln_g, sgu_ln_b=v_sgu_ln_b, sgu_w=v_sgu_w, sgu_b=v_sgu_b, out_norm_ssm_g=v_out_norm_ssm_g, out_norm_sgu_g=v_out_norm_sgu_g, w_out=v_w_out, norm_mlp_g=v_norm_mlp_g, w_up=v_w_up, w_down=v_w_down, norm_final_g=v_norm_final_g)
    names = list(weights)
    large = ["w_in", "ssm_glu_w", "w_out", "w_up", "w_down"]
    small = [n for n in names if n not in large]

    s, d = x.shape[1], x.shape[2]
    xs = x.reshape(s, d)
    target = loss_target.reshape(s, d)
    width = ssm_glu_w.shape[-1]
    ff = w_down.shape[1] * N_CHIPS
    tm = min(512, s)
    core = lax.axis_index("c").astype(jnp.int32).reshape(1)
    chip = (2 * lax.axis_index("x") + lax.axis_index("y")).astype(jnp.int32).reshape(1)
    pos = jnp.concatenate([chip, core])

    shards = [w[0].astype(BF16).reshape(2, w.shape[1] // 2, w.shape[2]) for w in (w_in, ssm_glu_w, w_out, w_up, w_down)]
    (w_in_g,) = _gather_halves(shards[:1], "gather_w_in")
    w_in4 = _place_own(shards[0], w_in_g, pos, "place_w_in").reshape(N_CHIPS, d, w_in.shape[2])

    ssm_args = (ssm_a_re[0], ssm_a_im[0], ssm_b_re[0], ssm_b_im[0], ssm_c_re[0], ssm_c_im[0], ssm_d[0], ssm_log_dt[0])
    (k2c, w2c, v2c, al), ssm_vjp = jax.vjp(_ssm_mats, *ssm_args)
    consts = _spread_consts(ssm_b_re.shape[-1], ssm_a_re.shape[-1])
    k2b, w2b, v2b = _ssm_spread(k2c, w2c, v2c, consts)
    causal = jnp.tril(jnp.ones((SGU_CHUNK, SGU_CHUNK), dtype=bool))
    wt = jnp.where(causal[None], sgu_w[0], 0.0)
    wtb = wt.astype(BF16)
    wttb = jnp.swapaxes(wt, 1, 2).astype(BF16)
    heads = sgu_w.shape[1]
    biasb = jnp.broadcast_to(sgu_b[0][:, :, None], (heads, SGU_CHUNK, LANES))

    tm_big = min(1024, s)
    (z, h1b), (glu_g, up_g) = _in_proj(
        xs, norm_mix_g, w_in4, tm_big, [_gather_one(shards[1]), _gather_one(shards[3])])
    wg_full = _place_own(shards[1], glu_g, pos, "place_glu_w").reshape(width, width)
    up4 = _place_own(shards[3], up_g, pos, "place_w_up").reshape(N_CHIPS, d, w_up.shape[2])
    (y_pre, xprev, us), (out_g,) = _ssm_fwd(z, k2b, w2b, v2b, al, [_gather_one(shards[2])])
    w_out_full = _place_own(shards[2], out_g, pos, "place_w_out").reshape(d, d)
    mixed = _glu_fwd(y_pre, wg_full, ssm_glu_b, out_norm_ssm_g, d, tm)
    mixed = _sgu_fwd(z, mixed, sgu_ln_g, sgu_ln_b, wtb, biasb, out_norm_sgu_g, tm)
    x1 = _out_proj(xs, mixed, w_out_full, tm)
    tf = min(1024, up4.shape[-1])
    (up, h2b), (down_g,) = _mlp_up(x1, norm_mlp_g, up4, tm_big, tf, [_gather_one(shards[4])])
    w_down_full = _place_own(shards[4], down_g, pos, "place_w_down").reshape(ff, d)
    x2 = _mlp_down(x1, up, w_down_full, tm_big, tf)
    loss_part, dx2, dx2b, d_norm_final = _final(x2, target, norm_final_g.reshape(1, d), tm)

    dup, dh2 = _mlp_bwd(dx2b, up, up4, w_down_full, tm_big, min(512, up4.shape[-1]))
    dx1, dx1b, d_norm_mlp = _norm_bwd(dh2, x1, dx2, norm_mlp_g, min(256, s), "norm_mlp_bwd")
    hr_big = d // 2
    tkk = min(512, s)
    g_down = _grad_w(
        up, dx2b, (N_CHIPS, 2, hr_big, d),
        lambda i, j: (i // (2 * (hr_big // min(1024, hr_big))), (i // (hr_big // min(1024, hr_big))) % 2,
                      i % (hr_big // min(1024, hr_big)), j),
        min(1024, hr_big), d, tkk, True, "grad_w_down")
    (sib_down,) = _pair_exchange([g_down], "w_down_to_sibling")
    pair_down = _pair_sum(g_down, sib_down, core, "pair_sum_w_down")
    tn_up = min(2048, up4.shape[-1])
    per_up = up4.shape[-1] // tn_up
    g_up, chips_down = _grad_w(
        h2b, dup, (N_CHIPS, 2, hr_big, up4.shape[-1]),
        lambda i, j: (j // per_up, i // (hr_big // min(1024, hr_big)), i % (hr_big // min(1024, hr_big)), j % per_up),
        min(1024, hr_big), tn_up, tkk, False, "grad_w_up", exs=[_to_owner_one(pair_down)])
    (sib_up,) = _pair_exchange([g_up], "w_up_to_sibling")
    pair_up = _pair_sum(g_up, sib_up, core, "pair_sum_w_up")
    dmix = _out_proj_bwd(dx1b, w_out_full, tm)
    hr_out = d // (2 * N_CHIPS)
    g_out = _grad_w(
        mixed, dx1b, (N_CHIPS, 2, hr_out, d),
        lambda i, j: (i // 2, i % 2, 0, j), hr_out, d, tkk, False, "grad_w_out")
    (sib_out,) = _pair_exchange([g_out], "w_out_to_sibling")
    pair_out = _pair_sum(g_out, sib_out, core, "pair_sum_w_out")
    (dy_pre, g_glu, d_glu_b, d_norm_ssm), (chips_out,) = _glu_bwd(
        y_pre, dmix, wg_full, ssm_glu_b, out_norm_ssm_g, tm, [_to_owner_one(pair_out)])
    dys, gst, d_al = _ssm_bwd_state(dy_pre, v2b, al, xprev)
    (dz, d_k2, d_w2, d_v2), (chips_up,) = _ssm_bwd_main(
        us, dys, k2b, w2b, xprev, gst, consts, s, z.shape[1], [_to_owner_one(pair_up)])
    dz, d_wt, d_bias, d_ln_g, d_ln_b, d_norm_sgu = _sgu_bwd(
        z, dmix, dz, sgu_ln_g, sgu_ln_b, wtb, wttb, biasb, out_norm_sgu_g, tm)
    cw_in = w_in4.shape[-1]
    g_in = _grad_w(
        h1b, dz, (N_CHIPS, 2, hr_big, cw_in),
        lambda i, j: (j, i // (hr_big // min(1024, hr_big)), i % (hr_big // min(1024, hr_big)), 0),
        min(1024, hr_big), cw_in, tkk, False, "grad_w_in")
    (sib_in,) = _pair_exchange([g_in], "w_in_to_sibling")
    pair_in = _pair_sum(g_in, sib_in, core, "pair_sum_w_in")
    (grad_x, d_norm_mix), (chips_in,) = _in_proj_bwd(dz, w_in4, xs, dx1, norm_mix_g, tm, [_to_owner_one(pair_in)])

    d_ssm = ssm_vjp((d_k2, d_w2, d_v2, d_al))
    small_grads = dict(
        norm_mix_g=d_norm_mix, ssm_a_re=d_ssm[0], ssm_a_im=d_ssm[1], ssm_b_re=d_ssm[2], ssm_b_im=d_ssm[3],
        ssm_c_re=d_ssm[4], ssm_c_im=d_ssm[5], ssm_d=d_ssm[6], ssm_log_dt=d_ssm[7], ssm_glu_b=d_glu_b,
        sgu_ln_g=d_ln_g, sgu_ln_b=d_ln_b, sgu_w=jnp.where(causal[None], d_wt, 0.0), sgu_b=d_bias[:, :, 0],
        out_norm_ssm_g=d_norm_ssm, out_norm_sgu_g=d_norm_sgu, norm_mlp_g=d_norm_mlp, norm_final_g=d_norm_final)
    flat = _pack([small_grads[n] for n in small])
    hr_small = flat.shape[0] // (2 * N_CHIPS)
    g_small = flat.reshape(N_CHIPS, 2, hr_small, LANES)

    hr_glu = width // (2 * N_CHIPS)
    grads = [g_glu.reshape(N_CHIPS, 2, hr_glu, width), g_small]
    tags = ["glu_w", "small"]
    from_sib = _pair_exchange(grads, "grads_to_sibling")
    pair = [_pair_sum(g, r, core, "pair_sum_" + t) for g, r, t in zip(grads, from_sib, tags)]
    from_chips = _chip_exchange(pair, "grads_to_owner")
    pair = [pair_in, pair[0], pair_out, pair_up, pair_down, pair[1]]
    from_chips = [chips_in, from_chips[0], chips_out, chips_up, chips_down, from_chips[1]]
    tags = ["w_in", "glu_w", "w_out", "w_up", "w_down", "small"]
    halves = [_chip_sum(p, q, pos, "chip_sum_" + t) for p, q, t in zip(pair, from_chips, tags)]
    owned = _pair_share(halves, "grads_to_both_cores")
    (small_all,) = _gather_chips([owned[5]], "gather_small_grads")
    small_flat = small_all.reshape(flat.shape)

    grad_out, delta_out, m_out, v_out = {}, {}, {}, {}
    for n, g in zip(large, owned[:5]):
        shp = weights[n].shape
        g2 = g.reshape(shp[1], shp[2])
        dl, nm, nv = _adamw(weights[n][0], g2, mom_m[n][0], mom_v[n][0], "adamw_" + n)
        grad_out[n], delta_out[n], m_out[n], v_out[n] = g2.reshape(shp), dl.reshape(shp), nm.reshape(shp), nv.reshape(shp)
    shapes = [weights[n].shape for n in small]
    dl, nm, nv = _adamw(
        _pack([weights[n] for n in small]), small_flat, _pack([mom_m[n] for n in small]),
        _pack([mom_v[n] for n in small]), "adamw_small")
    for n, g, a, b, c in zip(small, _unpack(small_flat, shapes), _unpack(dl, shapes), _unpack(nm, shapes), _unpack(nv, shapes)):
        grad_out[n], delta_out[n], m_out[n], v_out[n] = g, a, b, c

    loss = lax.psum(loss_part[0, 0], ("x", "y", "c"))
    return (loss, grad_x.reshape(x.shape), *[grad_out[n] for n in names], *[delta_out[n] for n in names],
            *[m_out[n] for n in names], *[v_out[n] for n in names])
```

```python
import functools
import math

import numpy as np
import jax
import jax.numpy as jnp
from jax import lax
from jax.experimental import pallas as pl
from jax.experimental.pallas import tpu as pltpu

F32 = jnp.float32
BF16 = jnp.bfloat16
MESH = pl.DeviceIdType.MESH
HIGHEST = lax.Precision.HIGHEST

EPS = 1e-6
ADAM_LR = 0.001
ADAM_B1 = 0.9
ADAM_B2 = 0.999
ADAM_EPS = 1e-08
ADAM_WD = 0.01
ADAM_STEP = 10

N_CHIPS = 4
LANES = 128
SSM_GROUP = 16
SSM_STATE = 64
GROUPS_PER_TILE = LANES // SSM_GROUP
CHUNK = 16
PAIRS = CHUNK // 2
SGU_CHUNK = 128
VMEM_LIMIT = 56 * 2**20


def _params(n_axes, vmem=VMEM_LIMIT):
    return pltpu.CompilerParams(dimension_semantics=("arbitrary",) * n_axes, vmem_limit_bytes=vmem)


def _call(body, **kw):
    return pl.pallas_call(body, **kw)


def _dot(a, b):
    return jnp.dot(a, b, preferred_element_type=F32)


def _dot_nt(a, b):
    return lax.dot_general(a, b, (((1,), (1,)), ((), ())), preferred_element_type=F32)


def _dot_tn(a, b):
    return lax.dot_general(a, b, (((0,), (0,)), ((), ())), preferred_element_type=F32)


_GELU_K = math.sqrt(2.0 / math.pi)
_GELU_C = 0.044715


def _gelu_both(x):
    x2 = x * x
    t = jnp.tanh(x * (_GELU_K + (_GELU_K * _GELU_C) * x2))
    hx = 0.5 * x
    onep = 1.0 + t
    return hx * onep, 0.5 * onep + hx * (1.0 - t * t) * (_GELU_K + (3.0 * _GELU_K * _GELU_C) * x2)


def _gelu(x):
    return _gelu_both(x)[0]


def _sigmoid(x):
    return 1.0 / (1.0 + jnp.exp(-x))


def _rms(x):
    return lax.rsqrt(jnp.mean(x * x, axis=-1, keepdims=True) + EPS)


def _rms_bwd(dy, x, r, g):
    a = dy * g
    dx = r * a - x * (r * r * r) * jnp.mean(a * x, axis=-1, keepdims=True)
    return dx, dy * x * r


def _row_tile(rows, target, mult=16):
    for t in range(min(rows, target), 0, -1):
        if rows % t == 0 and t % mult == 0:
            return t
    return rows


def _ssm_mats(a_re, a_im, b_re, b_im, c_re, c_im, d, log_dt):
    g, p = a_re.shape
    h = b_re.shape[-1]
    nt = g // GROUPS_PER_TILE
    dt = jnp.exp(log_dt)[:, None]
    lr, li = a_re * dt, a_im * dt

    def apow(l):
        mag = jnp.exp(lr * l)
        return mag * jnp.cos(li * l), mag * jnp.sin(li * l)

    ar, ai = apow(1.0)
    den = a_re * a_re + a_im * a_im
    qr = ((ar - 1.0) * a_re + ai * a_im) / den
    qi = (ai * a_re - (ar - 1.0) * a_im) / den
    bt_re, bt_im = jnp.swapaxes(b_re, 1, 2), jnp.swapaxes(b_im, 1, 2)
    bbr = qr[:, None, :] * bt_re - qi[:, None, :] * bt_im
    bbi = qr[:, None, :] * bt_im + qi[:, None, :] * bt_re
    ls = jnp.arange(CHUNK + 1, dtype=F32)[:, None, None]
    pr, pi = apow(ls)
    tr = pr[:CHUNK, :, None, :] * bbr[None] - pi[:CHUNK, :, None, :] * bbi[None]
    ti = pr[:CHUNK, :, None, :] * bbi[None] + pi[:CHUNK, :, None, :] * bbr[None]
    k = jnp.einsum("gop,lgip->lgio", c_re, tr, precision=HIGHEST) - jnp.einsum(
        "gop,lgip->lgio", c_im, ti, precision=HIGHEST
    )
    k = k.at[0].add(d[:, :, None] * jnp.eye(h, dtype=F32))

    def tiles(a):
        a = a.reshape(PAIRS, 2, nt, GROUPS_PER_TILE, h, a.shape[-1])
        return jnp.transpose(a, (2, 0, 1, 3, 4, 5)).reshape(nt, PAIRS, 2 * GROUPS_PER_TILE * h, a.shape[-1])

    zero = jnp.zeros_like(k[:1])
    kz = jnp.concatenate([zero, zero, k], axis=0).reshape(PAIRS + 1, 2, g, h, h)
    even, odd = kz[:, 0], kz[:, 1]
    fill = jnp.zeros((PAIRS, g, h, LANES - 2 * h), F32)
    row0 = jnp.concatenate([even[1:], odd[1:], fill], axis=-1)
    row1 = jnp.concatenate([odd[:-1], even[1:], fill], axis=-1)
    k2c = tiles(jnp.stack([row0, row1], axis=1).reshape(2 * PAIRS, g, h, LANES))
    w2c = tiles(jnp.concatenate([tr[::-1], ti[::-1]], axis=-1))
    car = c_re[None] * pr[1:, :, None, :] - c_im[None] * pi[1:, :, None, :]
    cai = c_re[None] * pi[1:, :, None, :] + c_im[None] * pr[1:, :, None, :]
    v2c = tiles(jnp.concatenate([car, -cai], axis=-1))
    al = jnp.stack([pr[CHUNK].reshape(nt, -1), pi[CHUNK].reshape(nt, -1)], axis=1)
    return k2c, w2c, v2c, al


def _spread_consts(h, p):
    gg = GROUPS_PER_TILE
    row_g = (np.arange(2 * gg * h) // h) % gg
    colk = np.arange(2 * gg * h)
    rep_k = np.zeros((LANES, 2 * gg * h), np.float32)
    rep_k[(colk // (gg * h)) * h + colk % h, colk] = 1.0
    mask_k = (row_g[:, None] == ((colk // h) % gg)[None, :]).astype(np.float32)
    cols = np.arange(2 * gg * p)
    rep_s = np.zeros((2 * p, 2 * gg * p), np.float32)
    rep_s[(cols // (gg * p)) * p + cols % p, cols] = 1.0
    mask_s = (row_g[:, None] == ((cols // p) % gg)[None, :]).astype(np.float32)
    return tuple(jnp.asarray(a, BF16) for a in (rep_k, mask_k, rep_s, mask_s))


def _ssm_spread(k2c, w2c, v2c, consts, exs):
    nt = k2c.shape[0]
    rep_k, mask_k, rep_s, mask_s = consts
    nk, ns = rep_k.shape[1], rep_s.shape[1]

    def body(k_ref, w_ref, v_ref, rk_ref, mk_ref, rs_ref, ms_ref, ko_ref, wo_ref, vo_ref):
        for q in range(PAIRS):
            ko_ref[q] = (_dot(k_ref[q].astype(BF16), rk_ref[...]) * mk_ref[...]).astype(BF16)
            wo_ref[q] = (_dot(w_ref[q].astype(BF16), rs_ref[...]) * ms_ref[...]).astype(BF16)
            vo_ref[q] = (_dot(v_ref[q].astype(BF16), rs_ref[...]) * ms_ref[...]).astype(BF16)

    def tile(cols):
        return pl.BlockSpec((None, PAIRS, nk, cols), lambda t: (t, 0, 0, 0))

    def whole(a):
        return pl.BlockSpec(a.shape, lambda t: (0, 0))

    return _call_riding(
        body,
        exs,
        args=(k2c, w2c, v2c, rep_k, mask_k, rep_s, mask_s),
        scratch_shapes=[],
        grid=(nt,),
        in_specs=[tile(LANES), tile(LANES), tile(LANES), whole(rep_k), whole(mask_k), whole(rep_s), whole(mask_s)],
        out_specs=[tile(nk), tile(ns), tile(ns)],
        out_shape=[
            jax.ShapeDtypeStruct((nt, PAIRS, nk, nk), BF16),
            jax.ShapeDtypeStruct((nt, PAIRS, nk, ns), BF16),
            jax.ShapeDtypeStruct((nt, PAIRS, nk, ns), BF16),
        ],
        compiler_params=_params(1),
        name="ssm_spread",
    )


def _gather_blocks(full, mask, rep):
    return _dot_nt((full * mask).astype(BF16), rep)


def _in_proj(x, g, w4, tm, exs):
    s, d = x.shape
    nj, _, cw = w4.shape

    def body(x_ref, g_ref, w_ref, z_ref, h_ref):
        @pl.when(pl.program_id(1) == 0)
        def _():
            xv = x_ref[...]
            h_ref[...] = (xv * _rms(xv) * g_ref[...]).astype(BF16)

        z_ref[...] = _dot(h_ref[...], w_ref[...])

    return _call_riding(
        body,
        exs,
        scratch_shapes=[],
        args=(x, g, w4),
        grid=(s // tm, nj),
        in_specs=[
            pl.BlockSpec((tm, d), lambda i, j: (i, 0)),
            pl.BlockSpec((1, d), lambda i, j: (0, 0)),
            pl.BlockSpec((None, d, cw), lambda i, j: (j, 0, 0)),
        ],
        out_specs=[pl.BlockSpec((tm, cw), lambda i, j: (i, j)), pl.BlockSpec((tm, d), lambda i, j: (i, 0))],
        out_shape=[jax.ShapeDtypeStruct((s, nj * cw), F32), jax.ShapeDtypeStruct((s, d), BF16)],
        compiler_params=_params(2),
        name="in_proj",
    )


def _ssm_fwd(z, k2, w2, v2, al, exs):
    s = z.shape[0]
    nt = k2.shape[0]
    nc = s // CHUNK
    ns = w2.shape[-1]
    hs = ns // 2

    def body(u_ref, k_ref, w_ref, v_ref, al_ref, y_ref, xp_ref, us_ref, xloc):
        for q in range(PAIRS):
            us_ref[q, :, 0:LANES] = u_ref[pl.ds(2 * q, nc, stride=CHUNK), :].astype(BF16)
            us_ref[q, :, LANES : 2 * LANES] = u_ref[pl.ds(2 * q + 1, nc, stride=CHUNK), :].astype(BF16)
        acc = _dot(us_ref[0], w_ref[0])
        for q in range(1, PAIRS):
            acc = acc + _dot(us_ref[q], w_ref[q])
        xloc[...] = acc
        ar = al_ref[0:1, :]
        ai = al_ref[1:2, :]

        def step(c, carry):
            xr, xi = carry
            xp_ref[pl.ds(c, 1), 0:hs] = xr
            xp_ref[pl.ds(c, 1), hs:ns] = xi
            lr = xloc[pl.ds(c, 1), 0:hs]
            li = xloc[pl.ds(c, 1), hs:ns]
            return ar * xr - ai * xi + lr, ar * xi + ai * xr + li

        zero = jnp.zeros((1, hs), F32)
        lax.fori_loop(0, nc, step, (zero, zero))
        xpb = xp_ref[...].astype(BF16)
        for r in range(PAIRS):
            acc = _dot_nt(xpb, v_ref[r])
            for q in range(r + 1):
                acc = acc + _dot(us_ref[q], k_ref[r - q])
            y_ref[pl.ds(2 * r, nc, stride=CHUNK), :] = acc[:, 0:LANES]
            y_ref[pl.ds(2 * r + 1, nc, stride=CHUNK), :] = acc[:, LANES : 2 * LANES]

    return _call_riding(
        body,
        exs,
        args=(z, k2, w2, v2, al),
        grid=(nt,),
        in_specs=[
            pl.BlockSpec((s, LANES), lambda t: (0, t)),
            pl.BlockSpec((None, PAIRS, 2 * LANES, 2 * LANES), lambda t: (t, 0, 0, 0)),
            pl.BlockSpec((None, PAIRS, 2 * LANES, ns), lambda t: (t, 0, 0, 0)),
            pl.BlockSpec((None, PAIRS, 2 * LANES, ns), lambda t: (t, 0, 0, 0)),
            pl.BlockSpec((None, 2, hs), lambda t: (t, 0, 0)),
        ],
        out_specs=[
            pl.BlockSpec((s, LANES), lambda t: (0, t)),
            pl.BlockSpec((None, nc, ns), lambda t: (t, 0, 0)),
            pl.BlockSpec((None, PAIRS, nc, 2 * LANES), lambda t: (t, 0, 0, 0)),
        ],
        out_shape=[
            jax.ShapeDtypeStruct((s, nt * LANES), F32),
            jax.ShapeDtypeStruct((nt, nc, ns), F32),
            jax.ShapeDtypeStruct((nt, PAIRS, nc, 2 * LANES), BF16),
        ],
        scratch_shapes=[pltpu.VMEM((nc, ns), F32)],
        compiler_params=_params(1),
        name="ssm_fwd",
    )


def _glu_fwd(y_pre, wg, bg, go, d_model, tm):
    s, w = y_pre.shape

    def body(y_ref, wg_ref, bg_ref, go_ref, o_ref):
        yg = _gelu(y_ref[...])
        gate = _sigmoid(_dot(yg.astype(BF16), wg_ref[...]) + bg_ref[...])
        ys = yg * gate
        o_ref[...] = (ys * _rms(ys) * go_ref[...]).astype(BF16)

    return _call(
        body,
        grid=(s // tm,),
        in_specs=[
            pl.BlockSpec((tm, w), lambda i: (i, 0)),
            pl.BlockSpec((w, w), lambda i: (0, 0)),
            pl.BlockSpec((1, w), lambda i: (0, 0)),
            pl.BlockSpec((1, w), lambda i: (0, 0)),
        ],
        out_specs=pl.BlockSpec((tm, w), lambda i: (i, 0)),
        out_shape=jax.ShapeDtypeStruct((s, d_model), BF16),
        compiler_params=_params(1),
        name="glu_fwd",
    )(y_pre, wg, bg, go)


def _sgu_parts(zu, zv, lng, lnb, wt_ref, bias_ref):
    u, u_grad = _gelu_both(zu)
    v, v_grad = _gelu_both(zv)
    mu = jnp.mean(v, axis=-1, keepdims=True)
    vc = v - mu
    rstd = lax.rsqrt(jnp.mean(vc * vc, axis=-1, keepdims=True) + EPS)
    vhat = vc * rstd
    vb = (vhat * lng + lnb).astype(BF16)
    heads = wt_ref.shape[0]
    mix = jnp.concatenate(
        [_dot(wt_ref[h], vb[:, h * LANES : (h + 1) * LANES]) + bias_ref[h] for h in range(heads)], axis=1
    )
    return u, vhat, rstd, vb, mix, u_grad, v_grad


def _sgu_fwd(z, mixed, lng, lnb, wt, biasb, go, rb):
    s = z.shape[0]
    w = lng.shape[-1]
    heads = wt.shape[0]

    def body(zu_ref, zv_ref, m_any, lng_ref, lnb_ref, wt_ref, b_ref, go_ref, o_ref):
        del m_any
        for ck in range(rb // SGU_CHUNK):
            rows = slice(ck * SGU_CHUNK, (ck + 1) * SGU_CHUNK)
            u, _, _, _, mix, _, _ = _sgu_parts(zu_ref[rows, :], zv_ref[rows, :], lng_ref[...], lnb_ref[...], wt_ref, b_ref)
            y = u * mix
            o_ref[rows, :] = (y * _rms(y) * go_ref[...]).astype(BF16)

    return _call(
        body,
        grid=(s // rb,),
        in_specs=[
            pl.BlockSpec((rb, w), lambda i: (i, 1)),
            pl.BlockSpec((rb, w), lambda i: (i, 2)),
            pl.BlockSpec(memory_space=pl.ANY),
            pl.BlockSpec((1, w), lambda i: (0, 0)),
            pl.BlockSpec((1, w), lambda i: (0, 0)),
            pl.BlockSpec((heads, SGU_CHUNK, SGU_CHUNK), lambda i: (0, 0, 0)),
            pl.BlockSpec((heads, SGU_CHUNK, LANES), lambda i: (0, 0, 0)),
            pl.BlockSpec((1, w), lambda i: (0, 0)),
        ],
        out_specs=pl.BlockSpec((rb, w), lambda i: (i, 1)),
        out_shape=jax.ShapeDtypeStruct(mixed.shape, mixed.dtype),
        input_output_aliases={2: 0},
        compiler_params=_params(1),
        name="sgu_fwd",
    )(z, z, mixed, lng, lnb, wt, biasb, go)


def _out_proj(x, mixed, w_out, tm):
    s, d = x.shape

    def body(x_ref, m_ref, w_ref, o_ref):
        o_ref[...] = x_ref[...] + _dot(m_ref[...], w_ref[...])

    return _call(
        body,
        grid=(s // tm,),
        in_specs=[
            pl.BlockSpec((tm, d), lambda i: (i, 0)),
            pl.BlockSpec((tm, d), lambda i: (i, 0)),
            pl.BlockSpec((d, d), lambda i: (0, 0)),
        ],
        out_specs=pl.BlockSpec((tm, d), lambda i: (i, 0)),
        out_shape=jax.ShapeDtypeStruct((s, d), F32),
        compiler_params=_params(1),
        name="out_proj",
    )(x, mixed, w_out)


def _mlp_up(x1, g, w_up4, tm, tf, exs):
    s, d = x1.shape
    nj, _, cw = w_up4.shape
    per = cw // tf

    def body(x_ref, g_ref, wu_ref, up_ref, h_ref):
        @pl.when(pl.program_id(1) == 0)
        def _():
            xv = x_ref[...]
            h_ref[...] = (xv * _rms(xv) * g_ref[...]).astype(BF16)

        up_ref[...] = _dot(h_ref[...], wu_ref[...]).astype(BF16)

    return _call_riding(
        body,
        exs,
        scratch_shapes=[],
        args=(x1, g, w_up4),
        grid=(s // tm, nj * per),
        in_specs=[
            pl.BlockSpec((tm, d), lambda i, f: (i, 0)),
            pl.BlockSpec((1, d), lambda i, f: (0, 0)),
            pl.BlockSpec((None, d, tf), lambda i, f: (f // per, 0, f % per)),
        ],
        out_specs=[pl.BlockSpec((tm, tf), lambda i, f: (i, f)), pl.BlockSpec((tm, d), lambda i, f: (i, 0))],
        out_shape=[jax.ShapeDtypeStruct((s, nj * cw), BF16), jax.ShapeDtypeStruct((s, d), BF16)],
        compiler_params=_params(2),
        name="mlp_up",
    )


def _mlp_down_loss(x1, up, w_down, target, g, tm, tf):
    s, d = x1.shape
    ff = w_down.shape[0]
    nf = ff // tf

    def body(x_ref, up_ref, wd_ref, t_ref, g_ref, loss_ref, dx_ref, dxb_ref, dg_ref, acc):
        i = pl.program_id(0)
        f = pl.program_id(1)

        @pl.when(f == 0)
        def _():
            acc[...] = x_ref[...]

        @pl.when(jnp.logical_and(i == 0, f == 0))
        def _():
            loss_ref[...] = jnp.zeros_like(loss_ref)
            dg_ref[...] = jnp.zeros_like(dg_ref)

        a = jnp.maximum(up_ref[...].astype(F32), 0.0)
        acc[...] += _dot((a * a).astype(BF16), wd_ref[...])

        @pl.when(f == nf - 1)
        def _():
            xv = acc[...]
            r = _rms(xv)
            gv = g_ref[...]
            diff = xv * r * gv - t_ref[...]
            loss_ref[...] += 0.5 * jnp.sum(jnp.mean(diff * diff, axis=-1, keepdims=True), axis=0, keepdims=True)
            dx, dgt = _rms_bwd(diff * (1.0 / d), xv, r, gv)
            dx_ref[...] = dx
            dxb_ref[...] = dx.astype(BF16)
            dg_ref[...] += jnp.sum(dgt, axis=0, keepdims=True)

    row = pl.BlockSpec((tm, d), lambda i, f: (i, 0))
    vec = pl.BlockSpec((1, d), lambda i, f: (0, 0))
    return _call(
        body,
        grid=(s // tm, nf),
        in_specs=[
            row,
            pl.BlockSpec((tm, tf), lambda i, f: (i, f)),
            pl.BlockSpec((tf, d), lambda i, f: (f, 0)),
            row,
            vec,
        ],
        out_specs=[pl.BlockSpec((1, 1), lambda i, f: (0, 0)), row, row, vec],
        out_shape=[
            jax.ShapeDtypeStruct((1, 1), F32),
            jax.ShapeDtypeStruct((s, d), F32),
            jax.ShapeDtypeStruct((s, d), BF16),
            jax.ShapeDtypeStruct((1, d), F32),
        ],
        scratch_shapes=[pltpu.VMEM((tm, d), F32)],
        compiler_params=_params(2),
        name="mlp_down_loss",
    )(x1, up, w_down, target, g)


def _mlp_bwd(dx2b, up, w_up4, w_down, tm, tf):
    s, d = dx2b.shape
    ff = w_down.shape[0]
    cw = w_up4.shape[-1]
    per = cw // tf

    def body(dx_ref, up_ref, wu_ref, wd_ref, dup_ref, dh_ref):
        @pl.when(pl.program_id(1) == 0)
        def _():
            dh_ref[...] = jnp.zeros_like(dh_ref)

        dact = _dot_nt(dx_ref[...], wd_ref[...])
        dupb = (dact * (2.0 * jnp.maximum(up_ref[...].astype(F32), 0.0))).astype(BF16)
        dup_ref[...] = dupb
        dh_ref[...] += _dot_nt(dupb, wu_ref[...])

    return _call(
        body,
        grid=(s // tm, ff // tf),
        in_specs=[
            pl.BlockSpec((tm, d), lambda i, f: (i, 0)),
            pl.BlockSpec((tm, tf), lambda i, f: (i, f)),
            pl.BlockSpec((None, d, tf), lambda i, f: (f // per, 0, f % per)),
            pl.BlockSpec((tf, d), lambda i, f: (f, 0)),
        ],
        out_specs=[pl.BlockSpec((tm, tf), lambda i, f: (i, f)), pl.BlockSpec((tm, d), lambda i, f: (i, 0))],
        out_shape=[jax.ShapeDtypeStruct((s, ff), BF16), jax.ShapeDtypeStruct((s, d), F32)],
        compiler_params=_params(2),
        name="mlp_bwd",
    )(dx2b, up, w_up4, w_down)


def _norm_bwd(dh, x, dres, g, tm, name):
    s, d = x.shape

    def body(dh_ref, x_ref, dr_ref, g_ref, dx_ref, dxb_ref, dg_ref):
        @pl.when(pl.program_id(0) == 0)
        def _():
            dg_ref[...] = jnp.zeros_like(dg_ref)

        xv = x_ref[...]
        dx, dgt = _rms_bwd(dh_ref[...], xv, _rms(xv), g_ref[...])
        tot = dr_ref[...] + dx
        dx_ref[...] = tot
        dxb_ref[...] = tot.astype(BF16)
        dg_ref[...] += jnp.sum(dgt, axis=0, keepdims=True)

    row = pl.BlockSpec((tm, d), lambda i: (i, 0))
    vec = pl.BlockSpec((1, d), lambda i: (0, 0))
    return _call(
        body,
        grid=(s // tm,),
        in_specs=[row, row, row, vec],
        out_specs=[row, row, vec],
        out_shape=[
            jax.ShapeDtypeStruct((s, d), F32),
            jax.ShapeDtypeStruct((s, d), BF16),
            jax.ShapeDtypeStruct((1, d), F32),
        ],
        compiler_params=_params(1),
        name=name,
    )(dh, x, dres, g)


def _grad_w(a, b, out_dims, index_map, tm, tn, tk, relu2, name, exs=(), chunks=1):
    t, m = a.shape
    n = b.shape[1]
    nk = t // tk
    cw = tn // chunks

    def body(a_ref, b_ref, o_ref, acc):
        k = pl.program_id(2)

        @pl.when(k == 0)
        def _():
            acc[...] = jnp.zeros_like(acc)

        av = a_ref[...]
        if relu2:
            r = jnp.maximum(av.astype(F32), 0.0)
            av = (r * r).astype(BF16)
        acc[...] += _dot_tn(av, b_ref[...])

        @pl.when(k == nk - 1)
        def _():
            for c in range(chunks):
                o_ref[c] = acc[:, c * cw : (c + 1) * cw].astype(BF16)

    spec = dict(
        grid=(m // tm, n // tn, nk),
        in_specs=[pl.BlockSpec((tk, tm), lambda i, j, k: (k, i)), pl.BlockSpec((tk, tn), lambda i, j, k: (k, j))],
        out_specs=[pl.BlockSpec((chunks, None, tm, cw), lambda i, j, k: index_map(i, j))],
        out_shape=[jax.ShapeDtypeStruct(out_dims, BF16)],
        scratch_shapes=[pltpu.VMEM((tm, tn), F32)],
        compiler_params=_params(3),
        name=name,
    )
    (g,), landed = _call_riding(body, exs, args=(a, b), **spec)
    return (g, *landed) if exs else g


def _out_proj_bwd(dx1b, w_out, tm):
    s, d = dx1b.shape

    def body(dx_ref, w_ref, o_ref):
        o_ref[...] = _dot_nt(dx_ref[...], w_ref[...])

    return _call(
        body,
        grid=(s // tm,),
        in_specs=[pl.BlockSpec((tm, d), lambda i: (i, 0)), pl.BlockSpec((d, d), lambda i: (0, 0))],
        out_specs=pl.BlockSpec((tm, d), lambda i: (i, 0)),
        out_shape=jax.ShapeDtypeStruct((s, d), F32),
        compiler_params=_params(1),
        name="out_proj_bwd",
    )(dx1b, w_out)


def _glu_bwd(y_pre, dmix, wg, bg, go, tm, exs):
    s, w = y_pre.shape
    n = s // tm

    def body(y_ref, dm_ref, wg_ref, bg_ref, go_ref, dy_ref, dwg_ref, dbg_ref, dgo_ref, acc):
        i = pl.program_id(0)

        @pl.when(i == 0)
        def _():
            acc[...] = jnp.zeros_like(acc)
            dbg_ref[...] = jnp.zeros_like(dbg_ref)
            dgo_ref[...] = jnp.zeros_like(dgo_ref)

        yp = y_ref[...]
        yg, yg_grad = _gelu_both(yp)
        ygb = yg.astype(BF16)
        gate = _sigmoid(_dot(ygb, wg_ref[...]) + bg_ref[...])
        ys = yg * gate
        dys, dgt = _rms_bwd(dm_ref[...], ys, _rms(ys), go_ref[...])
        dgo_ref[...] += jnp.sum(dgt, axis=0, keepdims=True)
        dpre = dys * yg * gate * (1.0 - gate)
        dbg_ref[...] += jnp.sum(dpre, axis=0, keepdims=True)
        dpb = dpre.astype(BF16)
        dyg = dys * gate + _dot_nt(dpb, wg_ref[...])
        dy_ref[...] = dyg * yg_grad
        acc[...] += _dot_tn(ygb, dpb)

        @pl.when(i == n - 1)
        def _():
            dwg_ref[...] = acc[...].astype(BF16)

    return _call_riding(
        body,
        exs,
        args=(y_pre, dmix, wg, bg, go),
        grid=(n,),
        in_specs=[
            pl.BlockSpec((tm, w), lambda i: (i, 0)),
            pl.BlockSpec((tm, w), lambda i: (i, 0)),
            pl.BlockSpec((w, w), lambda i: (0, 0)),
            pl.BlockSpec((1, w), lambda i: (0, 0)),
            pl.BlockSpec((1, w), lambda i: (0, 0)),
        ],
        out_specs=[
            pl.BlockSpec((tm, w), lambda i: (i, 0)),
            pl.BlockSpec((w, w), lambda i: (0, 0)),
            pl.BlockSpec((1, w), lambda i: (0, 0)),
            pl.BlockSpec((1, w), lambda i: (0, 0)),
        ],
        out_shape=[
            jax.ShapeDtypeStruct((s, w), F32),
            jax.ShapeDtypeStruct((w, w), BF16),
            jax.ShapeDtypeStruct((1, w), F32),
            jax.ShapeDtypeStruct((1, w), F32),
        ],
        scratch_shapes=[pltpu.VMEM((w, w), F32)],
        compiler_params=_params(1),
        name="glu_bwd",
    )


def _ssm_bwd_state(dy, v2, al, xprev):
    s = dy.shape[0]
    nt, nc, ns = xprev.shape
    hs = ns // 2

    def body(dy_ref, v_ref, al_ref, xp_ref, dys_ref, g_ref, dal_ref, dxp, gs):
        for q in range(PAIRS):
            dys_ref[q, :, 0:LANES] = dy_ref[pl.ds(2 * q, nc, stride=CHUNK), :].astype(BF16)
            dys_ref[q, :, LANES : 2 * LANES] = dy_ref[pl.ds(2 * q + 1, nc, stride=CHUNK), :].astype(BF16)
        acc = _dot(dys_ref[0], v_ref[0])
        for r in range(1, PAIRS):
            acc = acc + _dot(dys_ref[r], v_ref[r])
        dxp[...] = acc
        ar = al_ref[0:1, :]
        ai = al_ref[1:2, :]
        zero = jnp.zeros((1, hs), F32)
        gs[pl.ds(nc - 1, 1), 0:hs] = zero
        gs[pl.ds(nc - 1, 1), hs:ns] = zero

        def step(n, carry):
            gr, gi = carry
            c = nc - 2 - n
            nr = dxp[pl.ds(c + 1, 1), 0:hs] + ar * gr + ai * gi
            ni = dxp[pl.ds(c + 1, 1), hs:ns] + ar * gi - ai * gr
            gs[pl.ds(c, 1), 0:hs] = nr
            gs[pl.ds(c, 1), hs:ns] = ni
            return nr, ni

        lax.fori_loop(0, nc - 1, step, (zero, zero))
        gv = gs[...]
        xv = xp_ref[...]
        gr, gi = gv[:, 0:hs], gv[:, hs:ns]
        xr, xi = xv[:, 0:hs], xv[:, hs:ns]
        dal_ref[0:1, :] = jnp.sum(gr * xr + gi * xi, axis=0, keepdims=True)
        dal_ref[1:2, :] = jnp.sum(gi * xr - gr * xi, axis=0, keepdims=True)
        g_ref[...] = gv.astype(BF16)

    return _call(
        body,
        grid=(nt,),
        in_specs=[
            pl.BlockSpec((s, LANES), lambda t: (0, t)),
            pl.BlockSpec((None, PAIRS, 2 * LANES, ns), lambda t: (t, 0, 0, 0)),
            pl.BlockSpec((None, 2, hs), lambda t: (t, 0, 0)),
            pl.BlockSpec((None, nc, ns), lambda t: (t, 0, 0)),
        ],
        out_specs=[
            pl.BlockSpec((None, PAIRS, nc, 2 * LANES), lambda t: (t, 0, 0, 0)),
            pl.BlockSpec((None, nc, ns), lambda t: (t, 0, 0)),
            pl.BlockSpec((None, 2, hs), lambda t: (t, 0, 0)),
        ],
        out_shape=[
            jax.ShapeDtypeStruct((nt, PAIRS, nc, 2 * LANES), BF16),
            jax.ShapeDtypeStruct((nt, nc, ns), BF16),
            jax.ShapeDtypeStruct((nt, 2, hs), F32),
        ],
        scratch_shapes=[pltpu.VMEM((nc, ns), F32), pltpu.VMEM((nc, ns), F32)],
        compiler_params=_params(1),
        name="ssm_bwd_state",
    )(dy, v2, al, xprev)


def _ssm_bwd_main(us, dys, k2, w2, xprev, gst, consts, s, width, exs):
    nt, _, nc, _ = us.shape
    ns = xprev.shape[-1]
    rep_k, mask_k, rep_s, mask_s = consts
    nk = 2 * LANES

    def body(us_ref, dys_ref, k_ref, w_ref, xp_ref, g_ref, rk_ref, mk_ref, rs_ref, ms_ref,
             du_ref, dk_ref, dw_ref, dv_ref, acc, du2, duf, dkf):
        q = pl.program_id(1)

        @pl.when(q == 0)
        def _():
            dkf[...] = jnp.zeros_like(dkf)

        usq = us_ref[q]
        gb = g_ref[...]
        dw_ref[...] = _gather_blocks(_dot_tn(usq, gb), ms_ref[...], rs_ref[...])
        dv_ref[...] = _gather_blocks(_dot_tn(dys_ref[q], xp_ref[...].astype(BF16)), ms_ref[...], rs_ref[...])
        acc[...] = _dot_nt(gb, w_ref[...])
        for m in range(PAIRS):

            @pl.when(q + m < PAIRS)
            def _():
                dyr = dys_ref[jnp.minimum(q + m, PAIRS - 1)]
                acc[...] += _dot_nt(dyr, k_ref[m])
                dkf[m] += _dot_tn(usq, dyr)

        du2[q] = acc[...]

        @pl.when(q == PAIRS - 1)
        def _():
            for p in range(PAIRS):
                duf[pl.ds(2 * p, nc, stride=CHUNK), :] = du2[p, :, 0:LANES]
                duf[pl.ds(2 * p + 1, nc, stride=CHUNK), :] = du2[p, :, LANES : 2 * LANES]
                dk_ref[p] = _gather_blocks(dkf[p], mk_ref[...], rk_ref[...])
            du_ref[...] = duf[...].astype(BF16)

    def per_tile(rows, cols):
        return pl.BlockSpec((None, PAIRS, rows, cols), lambda t, q: (t, 0, 0, 0))

    def per_pair(rows, cols):
        return pl.BlockSpec((None, None, rows, cols), lambda t, q: (t, q, 0, 0))

    def whole(a):
        return pl.BlockSpec(a.shape, lambda t, q: (0, 0))

    return _call_riding(
        body,
        exs,
        args=(us, dys, k2, w2, xprev, gst, rep_k, mask_k, rep_s, mask_s),
        grid=(nt, PAIRS),
        in_specs=[
            per_tile(nc, nk),
            per_tile(nc, nk),
            per_tile(nk, nk),
            per_pair(nk, ns),
            pl.BlockSpec((None, nc, ns), lambda t, q: (t, 0, 0)),
            pl.BlockSpec((None, nc, ns), lambda t, q: (t, 0, 0)),
            whole(rep_k),
            whole(mask_k),
            whole(rep_s),
            whole(mask_s),
        ],
        out_specs=[
            pl.BlockSpec((s, LANES), lambda t, q: (0, t)),
            per_tile(nk, LANES),
            per_pair(nk, LANES),
            per_pair(nk, LANES),
        ],
        out_shape=[
            jax.ShapeDtypeStruct((s, width), BF16),
            jax.ShapeDtypeStruct((nt, PAIRS, nk, LANES), F32),
            jax.ShapeDtypeStruct((nt, PAIRS, nk, LANES), F32),
            jax.ShapeDtypeStruct((nt, PAIRS, nk, LANES), F32),
        ],
        scratch_shapes=[
            pltpu.VMEM((nc, nk), F32),
            pltpu.VMEM((PAIRS, nc, nk), F32),
            pltpu.VMEM((s, LANES), F32),
            pltpu.VMEM((PAIRS, nk, nk), F32),
        ],
        compiler_params=_params(2),
        name="ssm_bwd_main",
    )


def _sgu_bwd(z, dmix, dz, lng, lnb, wt, wtt, biasb, go, rb):
    s = z.shape[0]
    w = lng.shape[-1]
    heads = wt.shape[0]
    ncol = (LANES, LANES)

    def body(zu_ref, zv_ref, dm_ref, dz_any, lng_ref, lnb_ref, wt_ref, wtt_ref, b_ref, go_ref,
             dz_ref, dw_ref, db_ref, dlg_ref, dlb_ref, dgo_ref, dzv):
        del dz_any
        i = pl.program_id(0)
        p = pl.program_id(1)

        @pl.when(jnp.logical_and(i == 0, p == 0))
        def _():
            dw_ref[...] = jnp.zeros_like(dw_ref)
            db_ref[...] = jnp.zeros_like(db_ref)
            dlg_ref[...] = jnp.zeros_like(dlg_ref)
            dlb_ref[...] = jnp.zeros_like(dlb_ref)
            dgo_ref[...] = jnp.zeros_like(dgo_ref)

        @pl.when(p == 0)
        def _():
            lng_v = lng_ref[...]
            for ck in range(rb // SGU_CHUNK):
                rows = slice(ck * SGU_CHUNK, (ck + 1) * SGU_CHUNK)
                zu = zu_ref[rows, :]
                zv = zv_ref[rows, :]
                u, vhat, rstd, vb, mix, u_grad, v_grad = _sgu_parts(zu, zv, lng_v, lnb_ref[...], wt_ref, b_ref)
                y = u * mix
                dy, dgt = _rms_bwd(dm_ref[rows, :], y, _rms(y), go_ref[...])
                dgo_ref[...] += jnp.sum(dgt, axis=0, keepdims=True)
                du = dy * mix
                dmx = dy * u
                dmb = dmx.astype(BF16)
                dvl = []
                for h in range(heads):
                    cols = slice(h * LANES, (h + 1) * LANES)
                    db_ref[h] += jnp.broadcast_to(jnp.sum(dmx[:, cols], axis=-1, keepdims=True), ncol)
                    dw_ref[h] += _dot_nt(dmb[:, cols], vb[:, cols])
                    dvl.append(_dot(wtt_ref[h], dmb[:, cols]))
                dvln = jnp.concatenate(dvl, axis=1)
                dlg_ref[...] += jnp.sum(dvln * vhat, axis=0, keepdims=True)
                dlb_ref[...] += jnp.sum(dvln, axis=0, keepdims=True)
                dvh = dvln * lng_v
                dv = rstd * (
                    dvh
                    - jnp.mean(dvh, axis=-1, keepdims=True)
                    - vhat * jnp.mean(dvh * vhat, axis=-1, keepdims=True)
                )
                dz_ref[rows, :] = (du * u_grad).astype(BF16)
                dzv[rows, :] = (dv * v_grad).astype(BF16)

        @pl.when(p == 1)
        def _():
            dz_ref[...] = dzv[...]

    return _call(
        body,
        grid=(s // rb, 2),
        in_specs=[
            pl.BlockSpec((rb, w), lambda i, p: (i, 1)),
            pl.BlockSpec((rb, w), lambda i, p: (i, 2)),
            pl.BlockSpec((rb, w), lambda i, p: (i, 1)),
            pl.BlockSpec(memory_space=pl.ANY),
            pl.BlockSpec((1, w), lambda i, p: (0, 0)),
            pl.BlockSpec((1, w), lambda i, p: (0, 0)),
            pl.BlockSpec((heads, SGU_CHUNK, SGU_CHUNK), lambda i, p: (0, 0, 0)),
            pl.BlockSpec((heads, SGU_CHUNK, SGU_CHUNK), lambda i, p: (0, 0, 0)),
            pl.BlockSpec((heads, SGU_CHUNK, LANES), lambda i, p: (0, 0, 0)),
            pl.BlockSpec((1, w), lambda i, p: (0, 0)),
        ],
        out_specs=[
            pl.BlockSpec((rb, w), lambda i, p: (i, 1 + p)),
            pl.BlockSpec((heads, SGU_CHUNK, SGU_CHUNK), lambda i, p: (0, 0, 0)),
            pl.BlockSpec((heads, SGU_CHUNK, LANES), lambda i, p: (0, 0, 0)),
            pl.BlockSpec((1, w), lambda i, p: (0, 0)),
            pl.BlockSpec((1, w), lambda i, p: (0, 0)),
            pl.BlockSpec((1, w), lambda i, p: (0, 0)),
        ],
        out_shape=[
            jax.ShapeDtypeStruct(dz.shape, dz.dtype),
            jax.ShapeDtypeStruct((heads, SGU_CHUNK, SGU_CHUNK), F32),
            jax.ShapeDtypeStruct((heads, SGU_CHUNK, LANES), F32),
            jax.ShapeDtypeStruct((1, w), F32),
            jax.ShapeDtypeStruct((1, w), F32),
            jax.ShapeDtypeStruct((1, w), F32),
        ],
        scratch_shapes=[pltpu.VMEM((rb, w), BF16)],
        input_output_aliases={3: 0},
        compiler_params=_params(2),
        name="sgu_bwd",
    )(z, z, dmix, dz, lng, lnb, wt, wtt, biasb, go)


def _in_proj_bwd(dz, w4, x, dres, g, tm, exs):
    s, d = x.shape
    nj, _, cw = w4.shape

    def body(dz_ref, w_ref, x_ref, dr_ref, g_ref, dx_ref, dg_ref, acc):
        i = pl.program_id(0)
        j = pl.program_id(1)

        @pl.when(j == 0)
        def _():
            acc[...] = jnp.zeros_like(acc)

        @pl.when(jnp.logical_and(i == 0, j == 0))
        def _():
            dg_ref[...] = jnp.zeros_like(dg_ref)

        acc[...] += _dot_nt(dz_ref[...], w_ref[...])

        @pl.when(j == nj - 1)
        def _():
            xv = x_ref[...]
            dx, dgt = _rms_bwd(acc[...], xv, _rms(xv), g_ref[...])
            dx_ref[...] = dr_ref[...] + dx
            dg_ref[...] += jnp.sum(dgt, axis=0, keepdims=True)

    row = pl.BlockSpec((tm, d), lambda i, j: (i, 0))
    vec = pl.BlockSpec((1, d), lambda i, j: (0, 0))
    return _call_riding(
        body,
        exs,
        args=(dz, w4, x, dres, g),
        scratch_shapes=[pltpu.VMEM((tm, d), F32)],
        grid=(s // tm, nj),
        in_specs=[
            pl.BlockSpec((tm, cw), lambda i, j: (i, j)),
            pl.BlockSpec((None, d, cw), lambda i, j: (j, 0, 0)),
            row,
            row,
            vec,
        ],
        out_specs=[row, vec],
        out_shape=[jax.ShapeDtypeStruct((s, d), F32), jax.ShapeDtypeStruct((1, d), F32)],
        compiler_params=_params(2),
        name="in_proj_bwd",
    )


_ANY = pl.BlockSpec(memory_space=pl.ANY)


def _position():
    x, y, c = lax.axis_index("x"), lax.axis_index("y"), lax.axis_index("c")
    return x, y, c, [(1 - x, y), (x, 1 - y), (1 - x, 1 - y)]


def _remote(src, dst, send_sems, recv_sems, k, to):
    return pltpu.make_async_remote_copy(
        src_ref=src, dst_ref=dst, send_sem=send_sems.at[k], recv_sem=recv_sems.at[k], device_id=to, device_id_type=MESH
    )


class _Riding:
    def __init__(self, srcs, out_shapes, n_sems, start, finish):
        self.srcs, self.out_shapes, self.n_sems, self.start, self.finish = srcs, out_shapes, n_sems, start, finish


def _call_riding(body, exs, *, grid, in_specs, out_specs, out_shape, scratch_shapes, args, **kw):
    n_in, n_out, n_scr = len(in_specs), len(out_specs), len(scratch_shapes)
    spec = dict(grid=grid, in_specs=list(in_specs), out_specs=list(out_specs), out_shape=list(out_shape))
    if not exs:
        return _call(body, scratch_shapes=list(scratch_shapes), **spec, **kw)(*args), []
    srcs = [a for ex in exs for a in ex.srcs]
    lands = [a for ex in exs for a in ex.out_shapes]
    xi, xo = len(srcs), len(lands)

    def fused(*refs):
        cin, xin = refs[:n_in], refs[n_in : n_in + xi]
        o = n_in + xi
        cout, xout = refs[o : o + n_out], refs[o + n_out : o + n_out + xo]
        scr = refs[o + n_out + xo :]
        ids = [pl.program_id(a) for a in range(len(grid))]
        first = functools.reduce(jnp.logical_and, [i == 0 for i in ids])
        last = functools.reduce(jnp.logical_and, [i == n - 1 for i, n in zip(ids, grid)])

        def each(half):
            si = so = 0
            for e, ex in enumerate(exs):
                ni, no = len(ex.srcs), len(ex.out_shapes)
                sems = scr[n_scr + 2 * e : n_scr + 2 * e + 2]
                getattr(ex, half)(xin[si : si + ni], xout[so : so + no], *sems)
                si, so = si + ni, so + no

        @pl.when(first)
        def _():
            each("start")

        body(*cin, *cout, *scr[:n_scr])

        @pl.when(last)
        def _():
            each("finish")

    sems = [pltpu.SemaphoreType.DMA((ex.n_sems,)) for ex in exs for _ in range(2)]
    spec["in_specs"] += [_ANY] * xi
    spec["out_specs"] += [_ANY] * xo
    spec["out_shape"] += lands
    outs = _call(fused, scratch_shapes=list(scratch_shapes) + sems, **spec, **kw)(*args, *srcs)
    return outs[:n_out], outs[n_out:]


def _gather_one(shard):
    def own_copies(ins, outs, send, recv):
        x, y, c, chips = _position()
        me = 2 * x + y
        sib = (x, y, 1 - c)
        cps = [_remote(ins[0].at[c], outs[0].at[me, c], send, recv, 0, sib)]
        cps += [_remote(ins[0].at[c], outs[0].at[me, c], send, recv, 1 + k, (px, py, c)) for k, (px, py) in enumerate(chips)]
        return cps, (c, me, sib, chips)

    def start(ins, outs, send, recv):
        for cp in own_copies(ins, outs, send, recv)[0]:
            cp.start()

    def finish(ins, outs, send, recv):
        cps, (c, me, sib, chips) = own_copies(ins, outs, send, recv)
        for k, (px, py) in enumerate(chips):
            landed = outs[0].at[2 * px + py, c]
            _remote(ins[0].at[c], landed, send, recv, 1 + k, (px, py, c)).wait_recv()
            cps.append(_remote(landed, landed, send, recv, 4 + k, sib))
            cps[-1].start()
        _remote(ins[0].at[1 - c], outs[0].at[me, 1 - c], send, recv, 0, sib).wait_recv()
        for k, (px, py) in enumerate(chips):
            theirs = outs[0].at[2 * px + py, 1 - c]
            _remote(theirs, theirs, send, recv, 4 + k, sib).wait_recv()
        for cp in cps:
            cp.wait_send()

    return _Riding([shard], [jax.ShapeDtypeStruct((N_CHIPS,) + shard.shape, shard.dtype)], 7, start, finish)


def _to_owner_one(p):
    def copies(ins, outs, send, recv):
        x, y, c, chips = _position()
        return [_remote(ins[0].at[2 * px + py], outs[0].at[k], send, recv, k, (px, py, c)) for k, (px, py) in enumerate(chips)]

    def start(ins, outs, send, recv):
        for cp in copies(ins, outs, send, recv):
            cp.start()

    def finish(ins, outs, send, recv):
        cps = copies(ins, outs, send, recv)
        for cp in cps:
            cp.wait_recv()
        for cp in cps:
            cp.wait_send()

    return _Riding([p], [jax.ShapeDtypeStruct((3,) + p.shape[1:], p.dtype)], 3, start, finish)


def _gather_chips(arrs, name):
    n = len(arrs)

    def body(*refs):
        ins, outs = refs[:n], refs[n : 2 * n]
        send_sems, recv_sems, local_sems = refs[2 * n :]
        x, y, c, chips = _position()
        me = 2 * x + y
        local = [pltpu.make_async_copy(ins[i], outs[i].at[me], local_sems.at[i]) for i in range(n)]
        for cp in local:
            cp.start()
        sends = []
        for k, (px, py) in enumerate(chips):
            for i in range(n):
                sends.append(_remote(ins[i], outs[i].at[me], send_sems, recv_sems, k * n + i, (px, py, c)))
                sends[-1].start()
        for k, (px, py) in enumerate(chips):
            for i in range(n):
                _remote(ins[i], outs[i].at[2 * px + py], send_sems, recv_sems, k * n + i, (px, py, c)).wait_recv()
        for cp in sends:
            cp.wait_send()
        for cp in local:
            cp.wait()

    return _call(
        body,
        in_specs=[_ANY] * n,
        out_specs=[_ANY] * n,
        out_shape=[jax.ShapeDtypeStruct((N_CHIPS,) + a.shape, a.dtype) for a in arrs],
        scratch_shapes=[
            pltpu.SemaphoreType.DMA((3 * n,)),
            pltpu.SemaphoreType.DMA((3 * n,)),
            pltpu.SemaphoreType.DMA((n,)),
        ],
        name=name,
    )(*arrs)


def _pair_exchange(gs, name):
    n = len(gs)

    def body(*refs):
        ins, outs = refs[:n], refs[n : 2 * n]
        send_sems, recv_sems = refs[2 * n :]
        x, y, c, _ = _position()
        sib = (x, y, 1 - c)
        cps = []
        for i in range(n):
            for j in range(N_CHIPS):
                cps.append(_remote(ins[i].at[j, 1 - c], outs[i].at[j], send_sems, recv_sems, i * N_CHIPS + j, sib))
                cps[-1].start()
        for cp in cps:
            cp.wait_recv()
        for cp in cps:
            cp.wait_send()

    return _call(
        body,
        in_specs=[_ANY] * n,
        out_specs=[_ANY] * n,
        out_shape=[jax.ShapeDtypeStruct((g.shape[0],) + g.shape[2:], g.dtype) for g in gs],
        scratch_shapes=[pltpu.SemaphoreType.DMA((n * N_CHIPS,)), pltpu.SemaphoreType.DMA((n * N_CHIPS,))],
        name=name,
    )(*gs)


def _gather_halves(shards, name):
    n = len(shards)

    def body(*refs):
        ins, outs = refs[:n], refs[n : 2 * n]
        ici_send, ici_recv, d2d_send, d2d_recv = refs[2 * n :]
        x, y, c, chips = _position()
        me = 2 * x + y
        sib = (x, y, 1 - c)
        started = []
        for i in range(n):
            started.append(_remote(ins[i].at[c], outs[i].at[me, c], d2d_send, d2d_recv, i, sib))
            started[-1].start()
        for k, (px, py) in enumerate(chips):
            for i in range(n):
                started.append(_remote(ins[i].at[c], outs[i].at[me, c], ici_send, ici_recv, k * n + i, (px, py, c)))
                started[-1].start()
        for k, (px, py) in enumerate(chips):
            for i in range(n):
                landed = outs[i].at[2 * px + py, c]
                _remote(ins[i].at[c], landed, ici_send, ici_recv, k * n + i, (px, py, c)).wait_recv()
                started.append(_remote(landed, landed, d2d_send, d2d_recv, (k + 1) * n + i, sib))
                started[-1].start()
        for i in range(n):
            _remote(ins[i].at[1 - c], outs[i].at[me, 1 - c], d2d_send, d2d_recv, i, sib).wait_recv()
        for k, (px, py) in enumerate(chips):
            for i in range(n):
                theirs = outs[i].at[2 * px + py, 1 - c]
                _remote(theirs, theirs, d2d_send, d2d_recv, (k + 1) * n + i, sib).wait_recv()
        for cp in started:
            cp.wait_send()

    return _call(
        body,
        in_specs=[_ANY] * n,
        out_specs=[_ANY] * n,
        out_shape=[jax.ShapeDtypeStruct((N_CHIPS,) + a.shape, a.dtype) for a in shards],
        scratch_shapes=[
            pltpu.SemaphoreType.DMA((3 * n,)),
            pltpu.SemaphoreType.DMA((3 * n,)),
            pltpu.SemaphoreType.DMA((4 * n,)),
            pltpu.SemaphoreType.DMA((4 * n,)),
        ],
        name=name,
    )(*shards)


def _place_own(shard, gathered, pos, name):
    _, hr, cols = shard.shape
    tr = _row_tile(hr, 512)

    def body(pos_ref, s_ref, g_any, o_ref):
        del pos_ref, g_any
        o_ref[...] = s_ref[...]

    return _call(
        body,
        grid_spec=pltpu.PrefetchScalarGridSpec(
            num_scalar_prefetch=1,
            grid=(hr // tr,),
            in_specs=[pl.BlockSpec((None, tr, cols), lambda i, p: (p[1], i, 0)), _ANY],
            out_specs=pl.BlockSpec((None, None, tr, cols), lambda i, p: (p[0], p[1], i, 0)),
        ),
        out_shape=jax.ShapeDtypeStruct(gathered.shape, gathered.dtype),
        input_output_aliases={2: 0},
        compiler_params=_params(1),
        name=name,
    )(pos, shard, gathered)


def _chip_exchange(ps, name):
    n = len(ps)

    def body(*refs):
        ins, outs = refs[:n], refs[n : 2 * n]
        send_sems, recv_sems = refs[2 * n :]
        x, y, c, chips = _position()
        sends = []
        for k, (px, py) in enumerate(chips):
            for i in range(n):
                sends.append(_remote(ins[i].at[2 * px + py], outs[i].at[k], send_sems, recv_sems, k * n + i, (px, py, c)))
                sends[-1].start()
        for cp in sends:
            cp.wait_recv()
        for cp in sends:
            cp.wait_send()

    return _call(
        body,
        in_specs=[_ANY] * n,
        out_specs=[_ANY] * n,
        out_shape=[jax.ShapeDtypeStruct((3,) + p.shape[1:], p.dtype) for p in ps],
        scratch_shapes=[pltpu.SemaphoreType.DMA((3 * n,)), pltpu.SemaphoreType.DMA((3 * n,))],
        name=name,
    )(*ps)


def _pair_share(fs, name):
    n = len(fs)

    def body(*refs):
        ins, outs = refs[:n], refs[n : 2 * n]
        send_sems, recv_sems = refs[2 * n :]
        x, y, c, _ = _position()
        sib = (x, y, 1 - c)
        sends = [_remote(ins[i].at[c], outs[i].at[c], send_sems, recv_sems, i, sib) for i in range(n)]
        for cp in sends:
            cp.start()
        for i in range(n):
            _remote(ins[i].at[1 - c], outs[i].at[1 - c], send_sems, recv_sems, i, sib).wait_recv()
        for cp in sends:
            cp.wait_send()

    return _call(
        body,
        in_specs=[_ANY] * n,
        out_specs=[_ANY] * n,
        out_shape=[jax.ShapeDtypeStruct(f.shape, f.dtype) for f in fs],
        input_output_aliases={i: i for i in range(n)},
        scratch_shapes=[pltpu.SemaphoreType.DMA((n,)), pltpu.SemaphoreType.DMA((n,))],
        name=name,
    )(*fs)


def _pair_sum(g, r, core, name):
    nj, _, hr, cols = g.shape
    tr = _row_tile(hr, 256)

    def body(c_ref, g_ref, r_ref, o_ref):
        del c_ref
        o_ref[...] = (g_ref[...].astype(F32) + r_ref[...].astype(F32)).astype(o_ref.dtype)

    return _call(
        body,
        grid_spec=pltpu.PrefetchScalarGridSpec(
            num_scalar_prefetch=1,
            grid=(nj, hr // tr),
            in_specs=[
                pl.BlockSpec((None, None, tr, cols), lambda j, i, c: (j, c[0], i, 0)),
                pl.BlockSpec((None, tr, cols), lambda j, i, c: (j, i, 0)),
            ],
            out_specs=pl.BlockSpec((None, tr, cols), lambda j, i, c: (j, i, 0)),
        ),
        out_shape=jax.ShapeDtypeStruct((nj, hr, cols), g.dtype),
        compiler_params=_params(2),
        name=name,
    )(core, g, r)


def _chip_sum(p, q, pos, name):
    nq, hr, cols = q.shape
    tr = _row_tile(hr, 256)

    def body(pos_ref, p_ref, *refs):
        del pos_ref
        o_ref = refs[nq]
        tot = p_ref[...].astype(F32)
        for k in range(nq):
            tot = tot + refs[k][...].astype(F32)
        o_ref[...] = tot

    return _call(
        body,
        grid_spec=pltpu.PrefetchScalarGridSpec(
            num_scalar_prefetch=1,
            grid=(hr // tr,),
            in_specs=[pl.BlockSpec((None, tr, cols), lambda i, s: (s[0], i, 0))]
            + [pl.BlockSpec((None, tr, cols), functools.partial(lambda k, i, s: (k, i, 0), k)) for k in range(nq)],
            out_specs=pl.BlockSpec((None, tr, cols), lambda i, s: (s[1], i, 0)),
        ),
        out_shape=jax.ShapeDtypeStruct((2, hr, cols), F32),
        compiler_params=_params(1),
        name=name,
    )(pos, p, *([q] * nq))


def _adamw(w, g, m, v, name):
    rows, cols = w.shape
    tr = _row_tile(rows, max(8, (2**18) // cols), mult=8)
    c1 = 1.0 - ADAM_B1**ADAM_STEP
    c2 = 1.0 - ADAM_B2**ADAM_STEP

    def body(w_ref, g_ref, m_ref, v_ref, d_ref, nm_ref, nv_ref):
        gv = g_ref[...]
        nm = ADAM_B1 * m_ref[...] + (1.0 - ADAM_B1) * gv
        nv = ADAM_B2 * v_ref[...] + (1.0 - ADAM_B2) * (gv * gv)
        nm_ref[...] = nm
        nv_ref[...] = nv
        d_ref[...] = -ADAM_LR * ((nm / c1) / (jnp.sqrt(nv / c2) + ADAM_EPS) + ADAM_WD * w_ref[...])

    spec = pl.BlockSpec((tr, cols), lambda i: (i, 0))
    sds = jax.ShapeDtypeStruct((rows, cols), F32)
    return _call(
        body,
        grid=(rows // tr,),
        in_specs=[spec] * 4,
        out_specs=[spec] * 3,
        out_shape=[sds] * 3,
        compiler_params=_params(1),
        name=name,
    )(w, g, m, v)


_TILE_ELEMS = 8 * LANES
_FLAT_ROW_MULT = 8 * 2 * N_CHIPS


def _flat_rows(shape):
    n = math.prod(shape)
    return (n + _TILE_ELEMS - 1) // _TILE_ELEMS * 8


def _pack(arrs):
    parts = []
    for a in arrs:
        rows = _flat_rows(a.shape)
        flat = a.reshape(-1).astype(F32)
        flat = jnp.pad(flat, (0, rows * LANES - flat.shape[0]))
        parts.append(flat.reshape(rows, LANES))
    total = sum(p.shape[0] for p in parts)
    pad = -total % _FLAT_ROW_MULT
    if pad:
        parts.append(jnp.zeros((pad, LANES), F32))
    return jnp.concatenate(parts, axis=0)


def _unpack(flat, shapes):
    out, row = [], 0
    for shp in shapes:
        rows = _flat_rows(shp)
        out.append(flat[row : row + rows].reshape(-1)[: math.prod(shp)].reshape(shp))
        row += rows
    return out


def kernel(x, norm_mix_g, w_in, ssm_a_re, ssm_a_im, ssm_b_re, ssm_b_im, ssm_c_re, ssm_c_im, ssm_d, ssm_log_dt, ssm_glu_w, ssm_glu_b, sgu_ln_g, sgu_ln_b, sgu_w, sgu_b, out_norm_ssm_g, out_norm_sgu_g, w_out, norm_mlp_g, w_up, w_down, norm_final_g, loss_target, m_norm_mix_g, m_w_in, m_ssm_a_re, m_ssm_a_im, m_ssm_b_re, m_ssm_b_im, m_ssm_c_re, m_ssm_c_im, m_ssm_d, m_ssm_log_dt, m_ssm_glu_w, m_ssm_glu_b, m_sgu_ln_g, m_sgu_ln_b, m_sgu_w, m_sgu_b, m_out_norm_ssm_g, m_out_norm_sgu_g, m_w_out, m_norm_mlp_g, m_w_up, m_w_down, m_norm_final_g, v_norm_mix_g, v_w_in, v_ssm_a_re, v_ssm_a_im, v_ssm_b_re, v_ssm_b_im, v_ssm_c_re, v_ssm_c_im, v_ssm_d, v_ssm_log_dt, v_ssm_glu_w, v_ssm_glu_b, v_sgu_ln_g, v_sgu_ln_b, v_sgu_w, v_sgu_b, v_out_norm_ssm_g, v_out_norm_sgu_g, v_w_out, v_norm_mlp_g, v_w_up, v_w_down, v_norm_final_g):
    weights = dict(norm_mix_g=norm_mix_g, w_in=w_in, ssm_a_re=ssm_a_re, ssm_a_im=ssm_a_im, ssm_b_re=ssm_b_re, ssm_b_im=ssm_b_im, ssm_c_re=ssm_c_re, ssm_c_im=ssm_c_im, ssm_d=ssm_d, ssm_log_dt=ssm_log_dt, ssm_glu_w=ssm_glu_w, ssm_glu_b=ssm_glu_b, sgu_ln_g=sgu_ln_g, sgu_ln_b=sgu_ln_b, sgu_w=sgu_w, sgu_b=sgu_b, out_norm_ssm_g=out_norm_ssm_g, out_norm_sgu_g=out_norm_sgu_g, w_out=w_out, norm_mlp_g=norm_mlp_g, w_up=w_up, w_down=w_down, norm_final_g=norm_final_g)
    mom_m = dict(norm_mix_g=m_norm_mix_g, w_in=m_w_in, ssm_a_re=m_ssm_a_re, ssm_a_im=m_ssm_a_im, ssm_b_re=m_ssm_b_re, ssm_b_im=m_ssm_b_im, ssm_c_re=m_ssm_c_re, ssm_c_im=m_ssm_c_im, ssm_d=m_ssm_d, ssm_log_dt=m_ssm_log_dt, ssm_glu_w=m_ssm_glu_w, ssm_glu_b=m_ssm_glu_b, sgu_ln_g=m_sgu_ln_g, sgu_ln_b=m_sgu_ln_b, sgu_w=m_sgu_w, sgu_b=m_sgu_b, out_norm_ssm_g=m_out_norm_ssm_g, out_norm_sgu_g=m_out_norm_sgu_g, w_out=m_w_out, norm_mlp_g=m_norm_mlp_g, w_up=m_w_up, w_down=m_w_down, norm_final_g=m_norm_final_g)
    mom_v = dict(norm_mix_g=v_norm_mix_g, w_in=v_w_in, ssm_a_re=v_ssm_a_re, ssm_a_im=v_ssm_a_im, ssm_b_re=v_ssm_b_re, ssm_b_im=v_ssm_b_im, ssm_c_re=v_ssm_c_re, ssm_c_im=v_ssm_c_im, ssm_d=v_ssm_d, ssm_log_dt=v_ssm_log_dt, ssm_glu_w=v_ssm_glu_w, ssm_glu_b=v_ssm_glu_b, sgu_ln_g=v_sgu_ln_g, sgu_ln_b=v_sgu_ln_b, sgu_w=v_sgu_w, sgu_b=v_sgu_b, out_norm_ssm_g=v_out_norm_ssm_g, out_norm_sgu_g=v_out_norm_sgu_g, w_out=v_w_out, norm_mlp_g=v_norm_mlp_g, w_up=v_w_up, w_down=v_w_down, norm_final_g=v_norm_final_g)
    names = list(weights)
    large = ["w_in", "ssm_glu_w", "w_out", "w_up", "w_down"]
    small = [n for n in names if n not in large]

    s, d = x.shape[1], x.shape[2]
    xs = x.reshape(s, d)
    target = loss_target.reshape(s, d)
    width = ssm_glu_w.shape[-1]
    ff = w_down.shape[1] * N_CHIPS
    tm = min(512, s)
    core = lax.axis_index("c").astype(jnp.int32).reshape(1)
    chip = (2 * lax.axis_index("x") + lax.axis_index("y")).astype(jnp.int32).reshape(1)
    pos = jnp.concatenate([chip, core])

    shards = [w[0].astype(BF16).reshape(2, w.shape[1] // 2, w.shape[2]) for w in (w_in, ssm_glu_w, w_out, w_up, w_down)]
    ssm_args = (ssm_a_re[0], ssm_a_im[0], ssm_b_re[0], ssm_b_im[0], ssm_c_re[0], ssm_c_im[0], ssm_d[0], ssm_log_dt[0])
    (k2c, w2c, v2c, al), ssm_vjp = jax.vjp(_ssm_mats, *ssm_args)
    consts = _spread_consts(ssm_b_re.shape[-1], ssm_a_re.shape[-1])
    (k2b, w2b, v2b), (w_in_g,) = _ssm_spread(k2c, w2c, v2c, consts, [_gather_one(shards[0])])
    w_in4 = _place_own(shards[0], w_in_g, pos, "place_w_in").reshape(N_CHIPS, d, w_in.shape[2])
    causal = jnp.tril(jnp.ones((SGU_CHUNK, SGU_CHUNK), dtype=bool))
    wt = jnp.where(causal[None], sgu_w[0], 0.0)
    wtb = wt.astype(BF16)
    wttb = jnp.swapaxes(wt, 1, 2).astype(BF16)
    heads = sgu_w.shape[1]
    biasb = jnp.broadcast_to(sgu_b[0][:, :, None], (heads, SGU_CHUNK, LANES))

    tm_big = min(1024, s)
    (z, h1b), (glu_g, up_g) = _in_proj(
        xs, norm_mix_g, w_in4, tm_big, [_gather_one(shards[1]), _gather_one(shards[3])])
    wg_full = _place_own(shards[1], glu_g, pos, "place_glu_w").reshape(width, width)
    up4 = _place_own(shards[3], up_g, pos, "place_w_up").reshape(N_CHIPS, d, w_up.shape[2])
    (y_pre, xprev, us), (out_g,) = _ssm_fwd(z, k2b, w2b, v2b, al, [_gather_one(shards[2])])
    w_out_full = _place_own(shards[2], out_g, pos, "place_w_out").reshape(d, d)
    mixed = _glu_fwd(y_pre, wg_full, ssm_glu_b, out_norm_ssm_g, d, tm)
    mixed = _sgu_fwd(z, mixed, sgu_ln_g, sgu_ln_b, wtb, biasb, out_norm_sgu_g, tm)
    x1 = _out_proj(xs, mixed, w_out_full, tm)
    tf = min(1024, up4.shape[-1])
    (up, h2b), (down_g,) = _mlp_up(x1, norm_mlp_g, up4, tm_big, tf, [_gather_one(shards[4])])
    w_down_full = _place_own(shards[4], down_g, pos, "place_w_down").reshape(ff, d)
    loss_part, dx2, dx2b, d_norm_final = _mlp_down_loss(
        x1, up, w_down_full, target, norm_final_g.reshape(1, d), tm, tf)

    dup, dh2 = _mlp_bwd(dx2b, up, up4, w_down_full, tm_big, min(512, up4.shape[-1]))
    dx1, dx1b, d_norm_mlp = _norm_bwd(dh2, x1, dx2, norm_mlp_g, min(256, s), "norm_mlp_bwd")
    hr_big = d // 2
    tkk = min(512, s)
    g_down = _grad_w(
        up, dx2b, (N_CHIPS, 2, hr_big, d),
        lambda i, j: (i // (2 * (hr_big // min(1024, hr_big))), (i // (hr_big // min(1024, hr_big))) % 2,
                      i % (hr_big // min(1024, hr_big)), j),
        min(1024, hr_big), d, tkk, True, "grad_w_down")
    (sib_down,) = _pair_exchange([g_down], "w_down_to_sibling")
    pair_down = _pair_sum(g_down, sib_down, core, "pair_sum_w_down")
    tn_up = min(2048, up4.shape[-1])
    per_up = up4.shape[-1] // tn_up
    g_up, chips_down = _grad_w(
        h2b, dup, (N_CHIPS, 2, hr_big, up4.shape[-1]),
        lambda i, j: (j // per_up, i // (hr_big // min(1024, hr_big)), i % (hr_big // min(1024, hr_big)), j % per_up),
        min(1024, hr_big), tn_up, tkk, False, "grad_w_up", exs=[_to_owner_one(pair_down)])
    (sib_up,) = _pair_exchange([g_up], "w_up_to_sibling")
    pair_up = _pair_sum(g_up, sib_up, core, "pair_sum_w_up")
    dmix = _out_proj_bwd(dx1b, w_out_full, tm)
    hr_out = d // (2 * N_CHIPS)
    g_out = _grad_w(
        mixed, dx1b, (1, 1, d, d), lambda i, j: (0, 0, i, j), min(1024, d), d, tkk, False, "grad_w_out")
    g_out = g_out.reshape(N_CHIPS, 2, hr_out, d)
    (sib_out,) = _pair_exchange([g_out], "w_out_to_sibling")
    pair_out = _pair_sum(g_out, sib_out, core, "pair_sum_w_out")
    (dy_pre, g_glu, d_glu_b, d_norm_ssm), (chips_out,) = _glu_bwd(
        y_pre, dmix, wg_full, ssm_glu_b, out_norm_ssm_g, tm, [_to_owner_one(pair_out)])
    dys, gst, d_al = _ssm_bwd_state(dy_pre, v2b, al, xprev)
    (dz, d_k2, d_w2, d_v2), (chips_up,) = _ssm_bwd_main(
        us, dys, k2b, w2b, xprev, gst, consts, s, z.shape[1], [_to_owner_one(pair_up)])
    dz, d_wt, d_bias, d_ln_g, d_ln_b, d_norm_sgu = _sgu_bwd(
        z, dmix, dz, sgu_ln_g, sgu_ln_b, wtb, wttb, biasb, out_norm_sgu_g, tm)
    cw_in = w_in4.shape[-1]
    g_in = _grad_w(
        h1b, dz, (N_CHIPS, 2, hr_big, cw_in),
        lambda i, j: (j, i // (hr_big // min(1024, hr_big)), i % (hr_big // min(1024, hr_big)), 0),
        min(1024, hr_big), 2 * cw_in, tkk, False, "grad_w_in", chunks=2)
    (sib_in,) = _pair_exchange([g_in], "w_in_to_sibling")
    pair_in = _pair_sum(g_in, sib_in, core, "pair_sum_w_in")
    (grad_x, d_norm_mix), (chips_in,) = _in_proj_bwd(dz, w_in4, xs, dx1, norm_mix_g, tm, [_to_owner_one(pair_in)])

    d_ssm = ssm_vjp((d_k2, d_w2, d_v2, d_al))
    small_grads = dict(
        norm_mix_g=d_norm_mix, ssm_a_re=d_ssm[0], ssm_a_im=d_ssm[1], ssm_b_re=d_ssm[2], ssm_b_im=d_ssm[3],
        ssm_c_re=d_ssm[4], ssm_c_im=d_ssm[5], ssm_d=d_ssm[6], ssm_log_dt=d_ssm[7], ssm_glu_b=d_glu_b,
        sgu_ln_g=d_ln_g, sgu_ln_b=d_ln_b, sgu_w=jnp.where(causal[None], d_wt, 0.0), sgu_b=d_bias[:, :, 0],
        out_norm_ssm_g=d_norm_ssm, out_norm_sgu_g=d_norm_sgu, norm_mlp_g=d_norm_mlp, norm_final_g=d_norm_final)
    flat = _pack([small_grads[n] for n in small])
    hr_small = flat.shape[0] // (2 * N_CHIPS)
    g_small = flat.reshape(N_CHIPS, 2, hr_small, LANES)

    hr_glu = width // (2 * N_CHIPS)
    grads = [g_glu.reshape(N_CHIPS, 2, hr_glu, width), g_small]
    tags = ["glu_w", "small"]
    from_sib = _pair_exchange(grads, "grads_to_sibling")
    pair = [_pair_sum(g, r, core, "pair_sum_" + t) for g, r, t in zip(grads, from_sib, tags)]
    from_chips = _chip_exchange(pair, "grads_to_owner")
    pair = [pair_in, pair[0], pair_out, pair_up, pair_down, pair[1]]
    from_chips = [chips_in, from_chips[0], chips_out, chips_up, chips_down, from_chips[1]]
    tags = ["w_in", "glu_w", "w_out", "w_up", "w_down", "small"]
    halves = [_chip_sum(p, q, pos, "chip_sum_" + t) for p, q, t in zip(pair, from_chips, tags)]
    owned = _pair_share(halves, "grads_to_both_cores")
    (small_all,) = _gather_chips([owned[5]], "gather_small_grads")
    small_flat = small_all.reshape(flat.shape)

    grad_out, delta_out, m_out, v_out = {}, {}, {}, {}
    for n, g in zip(large, owned[:5]):
        shp = weights[n].shape
        g2 = g.reshape(shp[1], shp[2])
        dl, nm, nv = _adamw(weights[n][0], g2, mom_m[n][0], mom_v[n][0], "adamw_" + n)
        grad_out[n], delta_out[n], m_out[n], v_out[n] = g2.reshape(shp), dl.reshape(shp), nm.reshape(shp), nv.reshape(shp)
    shapes = [weights[n].shape for n in small]
    dl, nm, nv = _adamw(
        _pack([weights[n] for n in small]), small_flat, _pack([mom_m[n] for n in small]),
        _pack([mom_v[n] for n in small]), "adamw_small")
    for n, g, a, b, c in zip(small, _unpack(small_flat, shapes), _unpack(dl, shapes), _unpack(nm, shapes), _unpack(nv, shapes)):
        grad_out[n], delta_out[n], m_out[n], v_out[n] = g, a, b, c

    loss = lax.psum(loss_part[0, 0], ("x", "y", "c"))
    return (loss, grad_x.reshape(x.shape), *[grad_out[n] for n in names], *[delta_out[n] for n in names],
            *[m_out[n] for n in names], *[v_out[n] for n in names])
```

```python
import functools
import math

import numpy as np
import jax
import jax.numpy as jnp
from jax import lax
from jax.experimental import pallas as pl
from jax.experimental.pallas import tpu as pltpu

F32 = jnp.float32
BF16 = jnp.bfloat16
MESH = pl.DeviceIdType.MESH
HIGHEST = lax.Precision.HIGHEST

EPS = 1e-6
ADAM_LR = 0.001
ADAM_B1 = 0.9
ADAM_B2 = 0.999
ADAM_EPS = 1e-08
ADAM_WD = 0.01
ADAM_STEP = 10

N_CHIPS = 4
LANES = 128
SSM_GROUP = 16
SSM_STATE = 64
GROUPS_PER_TILE = LANES // SSM_GROUP
CHUNK = 16
PAIRS = CHUNK // 2
SGU_CHUNK = 128
VMEM_LIMIT = 56 * 2**20


def _params(n_axes, vmem=VMEM_LIMIT):
    return pltpu.CompilerParams(dimension_semantics=("arbitrary",) * n_axes, vmem_limit_bytes=vmem)


def _call(body, **kw):
    return pl.pallas_call(body, **kw)


def _dot(a, b):
    return jnp.dot(a, b, preferred_element_type=F32)


def _dot_nt(a, b):
    return lax.dot_general(a, b, (((1,), (1,)), ((), ())), preferred_element_type=F32)


def _dot_tn(a, b):
    return lax.dot_general(a, b, (((0,), (0,)), ((), ())), preferred_element_type=F32)


_GELU_K = math.sqrt(2.0 / math.pi)
_GELU_C = 0.044715


def _gelu_both(x):
    x2 = x * x
    t = jnp.tanh(x * (_GELU_K + (_GELU_K * _GELU_C) * x2))
    hx = 0.5 * x
    onep = 1.0 + t
    return hx * onep, 0.5 * onep + hx * (1.0 - t * t) * (_GELU_K + (3.0 * _GELU_K * _GELU_C) * x2)


def _gelu(x):
    return _gelu_both(x)[0]


def _sigmoid(x):
    return 1.0 / (1.0 + jnp.exp(-x))


def _rms(x):
    return lax.rsqrt(jnp.mean(x * x, axis=-1, keepdims=True) + EPS)


def _rms_bwd(dy, x, r, g):
    a = dy * g
    dx = r * a - x * (r * r * r) * jnp.mean(a * x, axis=-1, keepdims=True)
    return dx, dy * x * r


def _row_tile(rows, target, mult=16):
    for t in range(min(rows, target), 0, -1):
        if rows % t == 0 and t % mult == 0:
            return t
    return rows


def _ssm_mats(a_re, a_im, b_re, b_im, c_re, c_im, d, log_dt):
    g, p = a_re.shape
    h = b_re.shape[-1]
    nt = g // GROUPS_PER_TILE
    dt = jnp.exp(log_dt)[:, None]
    lr, li = a_re * dt, a_im * dt

    def apow(l):
        mag = jnp.exp(lr * l)
        return mag * jnp.cos(li * l), mag * jnp.sin(li * l)

    ar, ai = apow(1.0)
    den = a_re * a_re + a_im * a_im
    qr = ((ar - 1.0) * a_re + ai * a_im) / den
    qi = (ai * a_re - (ar - 1.0) * a_im) / den
    bt_re, bt_im = jnp.swapaxes(b_re, 1, 2), jnp.swapaxes(b_im, 1, 2)
    bbr = qr[:, None, :] * bt_re - qi[:, None, :] * bt_im
    bbi = qr[:, None, :] * bt_im + qi[:, None, :] * bt_re
    ls = jnp.arange(CHUNK + 1, dtype=F32)[:, None, None]
    pr, pi = apow(ls)
    tr = pr[:CHUNK, :, None, :] * bbr[None] - pi[:CHUNK, :, None, :] * bbi[None]
    ti = pr[:CHUNK, :, None, :] * bbi[None] + pi[:CHUNK, :, None, :] * bbr[None]
    k = jnp.einsum("gop,lgip->lgio", c_re, tr, precision=HIGHEST) - jnp.einsum(
        "gop,lgip->lgio", c_im, ti, precision=HIGHEST
    )
    k = k.at[0].add(d[:, :, None] * jnp.eye(h, dtype=F32))

    def tiles(a):
        a = a.reshape(PAIRS, 2, nt, GROUPS_PER_TILE, h, a.shape[-1])
        return jnp.transpose(a, (2, 0, 1, 3, 4, 5)).reshape(nt, PAIRS, 2 * GROUPS_PER_TILE * h, a.shape[-1])

    zero = jnp.zeros_like(k[:1])
    kz = jnp.concatenate([zero, zero, k], axis=0).reshape(PAIRS + 1, 2, g, h, h)
    even, odd = kz[:, 0], kz[:, 1]
    fill = jnp.zeros((PAIRS, g, h, LANES - 2 * h), F32)
    row0 = jnp.concatenate([even[1:], odd[1:], fill], axis=-1)
    row1 = jnp.concatenate([odd[:-1], even[1:], fill], axis=-1)
    k2c = tiles(jnp.stack([row0, row1], axis=1).reshape(2 * PAIRS, g, h, LANES))
    w2c = tiles(jnp.concatenate([tr[::-1], ti[::-1]], axis=-1))
    car = c_re[None] * pr[1:, :, None, :] - c_im[None] * pi[1:, :, None, :]
    cai = c_re[None] * pi[1:, :, None, :] + c_im[None] * pr[1:, :, None, :]
    v2c = tiles(jnp.concatenate([car, -cai], axis=-1))
    al = jnp.stack([pr[CHUNK].reshape(nt, -1), pi[CHUNK].reshape(nt, -1)], axis=1)
    return k2c, w2c, v2c, al


def _spread_consts(h, p):
    gg = GROUPS_PER_TILE
    row_g = (np.arange(2 * gg * h) // h) % gg
    colk = np.arange(2 * gg * h)
    rep_k = np.zeros((LANES, 2 * gg * h), np.float32)
    rep_k[(colk // (gg * h)) * h + colk % h, colk] = 1.0
    mask_k = (row_g[:, None] == ((colk // h) % gg)[None, :]).astype(np.float32)
    cols = np.arange(2 * gg * p)
    rep_s = np.zeros((2 * p, 2 * gg * p), np.float32)
    rep_s[(cols // (gg * p)) * p + cols % p, cols] = 1.0
    mask_s = (row_g[:, None] == ((cols // p) % gg)[None, :]).astype(np.float32)
    return tuple(jnp.asarray(a, BF16) for a in (rep_k, mask_k, rep_s, mask_s))


def _ssm_spread(k2c, w2c, v2c, consts, exs):
    nt = k2c.shape[0]
    rep_k, mask_k, rep_s, mask_s = consts
    nk, ns = rep_k.shape[1], rep_s.shape[1]

    def body(k_ref, w_ref, v_ref, rk_ref, mk_ref, rs_ref, ms_ref, ko_ref, wo_ref, vo_ref):
        for q in range(PAIRS):
            ko_ref[q] = (_dot(k_ref[q].astype(BF16), rk_ref[...]) * mk_ref[...]).astype(BF16)
            wo_ref[q] = (_dot(w_ref[q].astype(BF16), rs_ref[...]) * ms_ref[...]).astype(BF16)
            vo_ref[q] = (_dot(v_ref[q].astype(BF16), rs_ref[...]) * ms_ref[...]).astype(BF16)

    def tile(cols):
        return pl.BlockSpec((None, PAIRS, nk, cols), lambda t: (t, 0, 0, 0))

    def whole(a):
        return pl.BlockSpec(a.shape, lambda t: (0, 0))

    return _call_riding(
        body,
        exs,
        args=(k2c, w2c, v2c, rep_k, mask_k, rep_s, mask_s),
        scratch_shapes=[],
        grid=(nt,),
        in_specs=[tile(LANES), tile(LANES), tile(LANES), whole(rep_k), whole(mask_k), whole(rep_s), whole(mask_s)],
        out_specs=[tile(nk), tile(ns), tile(ns)],
        out_shape=[
            jax.ShapeDtypeStruct((nt, PAIRS, nk, nk), BF16),
            jax.ShapeDtypeStruct((nt, PAIRS, nk, ns), BF16),
            jax.ShapeDtypeStruct((nt, PAIRS, nk, ns), BF16),
        ],
        compiler_params=_params(1),
        name="ssm_spread",
    )


def _gather_blocks(full, mask, rep):
    return _dot_nt((full * mask).astype(BF16), rep)


def _in_proj(x, g, w4, tm, exs):
    s, d = x.shape
    nj, _, cw = w4.shape

    def body(x_ref, g_ref, w_ref, z_ref, h_ref):
        @pl.when(pl.program_id(1) == 0)
        def _():
            xv = x_ref[...]
            h_ref[...] = (xv * _rms(xv) * g_ref[...]).astype(BF16)

        z_ref[...] = _dot(h_ref[...], w_ref[...])

    return _call_riding(
        body,
        exs,
        scratch_shapes=[],
        args=(x, g, w4),
        grid=(s // tm, nj),
        in_specs=[
            pl.BlockSpec((tm, d), lambda i, j: (i, 0)),
            pl.BlockSpec((1, d), lambda i, j: (0, 0)),
            pl.BlockSpec((None, d, cw), lambda i, j: (j, 0, 0)),
        ],
        out_specs=[pl.BlockSpec((tm, cw), lambda i, j: (i, j)), pl.BlockSpec((tm, d), lambda i, j: (i, 0))],
        out_shape=[jax.ShapeDtypeStruct((s, nj * cw), F32), jax.ShapeDtypeStruct((s, d), BF16)],
        compiler_params=_params(2),
        name="in_proj",
    )


def _ssm_fwd(z, k2, w2, v2, al, exs):
    s = z.shape[0]
    nt = k2.shape[0]
    nc = s // CHUNK
    ns = w2.shape[-1]
    hs = ns // 2

    def body(u_ref, k_ref, w_ref, v_ref, al_ref, y_ref, xp_ref, us_ref, xloc):
        for q in range(PAIRS):
            us_ref[q, :, 0:LANES] = u_ref[pl.ds(2 * q, nc, stride=CHUNK), :].astype(BF16)
            us_ref[q, :, LANES : 2 * LANES] = u_ref[pl.ds(2 * q + 1, nc, stride=CHUNK), :].astype(BF16)
        acc = _dot(us_ref[0], w_ref[0])
        for q in range(1, PAIRS):
            acc = acc + _dot(us_ref[q], w_ref[q])
        xloc[...] = acc
        ar = al_ref[0:1, :]
        ai = al_ref[1:2, :]

        def step(c, carry):
            xr, xi = carry
            xp_ref[pl.ds(c, 1), 0:hs] = xr
            xp_ref[pl.ds(c, 1), hs:ns] = xi
            lr = xloc[pl.ds(c, 1), 0:hs]
            li = xloc[pl.ds(c, 1), hs:ns]
            return ar * xr - ai * xi + lr, ar * xi + ai * xr + li

        zero = jnp.zeros((1, hs), F32)
        lax.fori_loop(0, nc, step, (zero, zero))
        xpb = xp_ref[...].astype(BF16)
        for r in range(PAIRS):
            acc = _dot_nt(xpb, v_ref[r])
            for q in range(r + 1):
                acc = acc + _dot(us_ref[q], k_ref[r - q])
            y_ref[pl.ds(2 * r, nc, stride=CHUNK), :] = acc[:, 0:LANES]
            y_ref[pl.ds(2 * r + 1, nc, stride=CHUNK), :] = acc[:, LANES : 2 * LANES]

    return _call_riding(
        body,
        exs,
        args=(z, k2, w2, v2, al),
        grid=(nt,),
        in_specs=[
            pl.BlockSpec((s, LANES), lambda t: (0, t)),
            pl.BlockSpec((None, PAIRS, 2 * LANES, 2 * LANES), lambda t: (t, 0, 0, 0)),
            pl.BlockSpec((None, PAIRS, 2 * LANES, ns), lambda t: (t, 0, 0, 0)),
            pl.BlockSpec((None, PAIRS, 2 * LANES, ns), lambda t: (t, 0, 0, 0)),
            pl.BlockSpec((None, 2, hs), lambda t: (t, 0, 0)),
        ],
        out_specs=[
            pl.BlockSpec((s, LANES), lambda t: (0, t)),
            pl.BlockSpec((None, nc, ns), lambda t: (t, 0, 0)),
            pl.BlockSpec((None, PAIRS, nc, 2 * LANES), lambda t: (t, 0, 0, 0)),
        ],
        out_shape=[
            jax.ShapeDtypeStruct((s, nt * LANES), F32),
            jax.ShapeDtypeStruct((nt, nc, ns), F32),
            jax.ShapeDtypeStruct((nt, PAIRS, nc, 2 * LANES), BF16),
        ],
        scratch_shapes=[pltpu.VMEM((nc, ns), F32)],
        compiler_params=_params(1),
        name="ssm_fwd",
    )


def _glu_fwd(y_pre, wg, bg, go, d_model, tm):
    s, w = y_pre.shape

    def body(y_ref, wg_ref, bg_ref, go_ref, o_ref):
        yg = _gelu(y_ref[...])
        gate = _sigmoid(_dot(yg.astype(BF16), wg_ref[...]) + bg_ref[...])
        ys = yg * gate
        o_ref[...] = (ys * _rms(ys) * go_ref[...]).astype(BF16)

    return _call(
        body,
        grid=(s // tm,),
        in_specs=[
            pl.BlockSpec((tm, w), lambda i: (i, 0)),
            pl.BlockSpec((w, w), lambda i: (0, 0)),
            pl.BlockSpec((1, w), lambda i: (0, 0)),
            pl.BlockSpec((1, w), lambda i: (0, 0)),
        ],
        out_specs=pl.BlockSpec((tm, w), lambda i: (i, 0)),
        out_shape=jax.ShapeDtypeStruct((s, d_model), BF16),
        compiler_params=_params(1),
        name="glu_fwd",
    )(y_pre, wg, bg, go)


def _sgu_parts(zu, zv, lng, lnb, wt_ref, bias_ref):
    u, u_grad = _gelu_both(zu)
    v, v_grad = _gelu_both(zv)
    mu = jnp.mean(v, axis=-1, keepdims=True)
    vc = v - mu
    rstd = lax.rsqrt(jnp.mean(vc * vc, axis=-1, keepdims=True) + EPS)
    vhat = vc * rstd
    vb = (vhat * lng + lnb).astype(BF16)
    heads = wt_ref.shape[0]
    mix = jnp.concatenate(
        [_dot(wt_ref[h], vb[:, h * LANES : (h + 1) * LANES]) + bias_ref[h] for h in range(heads)], axis=1
    )
    return u, vhat, rstd, vb, mix, u_grad, v_grad


def _sgu_fwd(z, mixed, lng, lnb, wt, biasb, go, rb):
    s = z.shape[0]
    w = lng.shape[-1]
    heads = wt.shape[0]

    def body(zu_ref, zv_ref, m_any, lng_ref, lnb_ref, wt_ref, b_ref, go_ref, o_ref):
        del m_any
        for ck in range(rb // SGU_CHUNK):
            rows = slice(ck * SGU_CHUNK, (ck + 1) * SGU_CHUNK)
            u, _, _, _, mix, _, _ = _sgu_parts(zu_ref[rows, :], zv_ref[rows, :], lng_ref[...], lnb_ref[...], wt_ref, b_ref)
            y = u * mix
            o_ref[rows, :] = (y * _rms(y) * go_ref[...]).astype(BF16)

    return _call(
        body,
        grid=(s // rb,),
        in_specs=[
            pl.BlockSpec((rb, w), lambda i: (i, 1)),
            pl.BlockSpec((rb, w), lambda i: (i, 2)),
            pl.BlockSpec(memory_space=pl.ANY),
            pl.BlockSpec((1, w), lambda i: (0, 0)),
            pl.BlockSpec((1, w), lambda i: (0, 0)),
            pl.BlockSpec((heads, SGU_CHUNK, SGU_CHUNK), lambda i: (0, 0, 0)),
            pl.BlockSpec((heads, SGU_CHUNK, LANES), lambda i: (0, 0, 0)),
            pl.BlockSpec((1, w), lambda i: (0, 0)),
        ],
        out_specs=pl.BlockSpec((rb, w), lambda i: (i, 1)),
        out_shape=jax.ShapeDtypeStruct(mixed.shape, mixed.dtype),
        input_output_aliases={2: 0},
        compiler_params=_params(1),
        name="sgu_fwd",
    )(z, z, mixed, lng, lnb, wt, biasb, go)


def _out_proj(x, mixed, w_out, tm):
    s, d = x.shape

    def body(x_ref, m_ref, w_ref, o_ref):
        o_ref[...] = x_ref[...] + _dot(m_ref[...], w_ref[...])

    return _call(
        body,
        grid=(s // tm,),
        in_specs=[
            pl.BlockSpec((tm, d), lambda i: (i, 0)),
            pl.BlockSpec((tm, d), lambda i: (i, 0)),
            pl.BlockSpec((d, d), lambda i: (0, 0)),
        ],
        out_specs=pl.BlockSpec((tm, d), lambda i: (i, 0)),
        out_shape=jax.ShapeDtypeStruct((s, d), F32),
        compiler_params=_params(1),
        name="out_proj",
    )(x, mixed, w_out)


def _mlp_up(x1, g, w_up4, tm, tf, exs):
    s, d = x1.shape
    nj, _, cw = w_up4.shape
    per = cw // tf

    def body(x_ref, g_ref, wu_ref, up_ref, h_ref):
        @pl.when(pl.program_id(1) == 0)
        def _():
            xv = x_ref[...]
            h_ref[...] = (xv * _rms(xv) * g_ref[...]).astype(BF16)

        up_ref[...] = _dot(h_ref[...], wu_ref[...]).astype(BF16)

    return _call_riding(
        body,
        exs,
        scratch_shapes=[],
        args=(x1, g, w_up4),
        grid=(s // tm, nj * per),
        in_specs=[
            pl.BlockSpec((tm, d), lambda i, f: (i, 0)),
            pl.BlockSpec((1, d), lambda i, f: (0, 0)),
            pl.BlockSpec((None, d, tf), lambda i, f: (f // per, 0, f % per)),
        ],
        out_specs=[pl.BlockSpec((tm, tf), lambda i, f: (i, f)), pl.BlockSpec((tm, d), lambda i, f: (i, 0))],
        out_shape=[jax.ShapeDtypeStruct((s, nj * cw), BF16), jax.ShapeDtypeStruct((s, d), BF16)],
        compiler_params=_params(2),
        name="mlp_up",
    )


def _mlp_down_loss(x1, up, w_down, target, g, tm, tf):
    s, d = x1.shape
    ff = w_down.shape[0]
    nf = ff // tf

    def body(x_ref, up_ref, wd_ref, t_ref, g_ref, loss_ref, dx_ref, dxb_ref, dg_ref, acc):
        i = pl.program_id(0)
        f = pl.program_id(1)

        @pl.when(f == 0)
        def _():
            acc[...] = x_ref[...]

        @pl.when(jnp.logical_and(i == 0, f == 0))
        def _():
            loss_ref[...] = jnp.zeros_like(loss_ref)
            dg_ref[...] = jnp.zeros_like(dg_ref)

        a = jnp.maximum(up_ref[...].astype(F32), 0.0)
        acc[...] += _dot((a * a).astype(BF16), wd_ref[...])

        @pl.when(f == nf - 1)
        def _():
            xv = acc[...]
            r = _rms(xv)
            gv = g_ref[...]
            diff = xv * r * gv - t_ref[...]
            loss_ref[...] += 0.5 * jnp.sum(jnp.mean(diff * diff, axis=-1, keepdims=True), axis=0, keepdims=True)
            dx, dgt = _rms_bwd(diff * (1.0 / d), xv, r, gv)
            dx_ref[...] = dx
            dxb_ref[...] = dx.astype(BF16)
            dg_ref[...] += jnp.sum(dgt, axis=0, keepdims=True)

    row = pl.BlockSpec((tm, d), lambda i, f: (i, 0))
    vec = pl.BlockSpec((1, d), lambda i, f: (0, 0))
    return _call(
        body,
        grid=(s // tm, nf),
        in_specs=[
            row,
            pl.BlockSpec((tm, tf), lambda i, f: (i, f)),
            pl.BlockSpec((tf, d), lambda i, f: (f, 0)),
            row,
            vec,
        ],
        out_specs=[pl.BlockSpec((1, 1), lambda i, f: (0, 0)), row, row, vec],
        out_shape=[
            jax.ShapeDtypeStruct((1, 1), F32),
            jax.ShapeDtypeStruct((s, d), F32),
            jax.ShapeDtypeStruct((s, d), BF16),
            jax.ShapeDtypeStruct((1, d), F32),
        ],
        scratch_shapes=[pltpu.VMEM((tm, d), F32)],
        compiler_params=_params(2),
        name="mlp_down_loss",
    )(x1, up, w_down, target, g)


def _mlp_bwd(dx2b, up, w_up4, w_down, tm, tf):
    s, d = dx2b.shape
    ff = w_down.shape[0]
    cw = w_up4.shape[-1]
    per = cw // tf

    def body(dx_ref, up_ref, wu_ref, wd_ref, dup_ref, dh_ref):
        @pl.when(pl.program_id(1) == 0)
        def _():
            dh_ref[...] = jnp.zeros_like(dh_ref)

        dact = _dot_nt(dx_ref[...], wd_ref[...])
        dupb = (dact * (2.0 * jnp.maximum(up_ref[...].astype(F32), 0.0))).astype(BF16)
        dup_ref[...] = dupb
        dh_ref[...] += _dot_nt(dupb, wu_ref[...])

    return _call(
        body,
        grid=(s // tm, ff // tf),
        in_specs=[
            pl.BlockSpec((tm, d), lambda i, f: (i, 0)),
            pl.BlockSpec((tm, tf), lambda i, f: (i, f)),
            pl.BlockSpec((None, d, tf), lambda i, f: (f // per, 0, f % per)),
            pl.BlockSpec((tf, d), lambda i, f: (f, 0)),
        ],
        out_specs=[pl.BlockSpec((tm, tf), lambda i, f: (i, f)), pl.BlockSpec((tm, d), lambda i, f: (i, 0))],
        out_shape=[jax.ShapeDtypeStruct((s, ff), BF16), jax.ShapeDtypeStruct((s, d), F32)],
        compiler_params=_params(2),
        name="mlp_bwd",
    )(dx2b, up, w_up4, w_down)


def _norm_bwd(dh, x, dres, g, tm, name):
    s, d = x.shape

    def body(dh_ref, x_ref, dr_ref, g_ref, dx_ref, dxb_ref, dg_ref):
        @pl.when(pl.program_id(0) == 0)
        def _():
            dg_ref[...] = jnp.zeros_like(dg_ref)

        xv = x_ref[...]
        dx, dgt = _rms_bwd(dh_ref[...], xv, _rms(xv), g_ref[...])
        tot = dr_ref[...] + dx
        dx_ref[...] = tot
        dxb_ref[...] = tot.astype(BF16)
        dg_ref[...] += jnp.sum(dgt, axis=0, keepdims=True)

    row = pl.BlockSpec((tm, d), lambda i: (i, 0))
    vec = pl.BlockSpec((1, d), lambda i: (0, 0))
    return _call(
        body,
        grid=(s // tm,),
        in_specs=[row, row, row, vec],
        out_specs=[row, row, vec],
        out_shape=[
            jax.ShapeDtypeStruct((s, d), F32),
            jax.ShapeDtypeStruct((s, d), BF16),
            jax.ShapeDtypeStruct((1, d), F32),
        ],
        compiler_params=_params(1),
        name=name,
    )(dh, x, dres, g)


def _grad_w(a, b, out_dims, index_map, tm, tn, tk, relu2, name, exs=(), chunks=1):
    t, m = a.shape
    n = b.shape[1]
    nk = t // tk
    cw = tn // chunks

    def body(a_ref, b_ref, o_ref, acc):
        k = pl.program_id(2)

        @pl.when(k == 0)
        def _():
            acc[...] = jnp.zeros_like(acc)

        av = a_ref[...]
        if relu2:
            r = jnp.maximum(av.astype(F32), 0.0)
            av = (r * r).astype(BF16)
        acc[...] += _dot_tn(av, b_ref[...])

        @pl.when(k == nk - 1)
        def _():
            for c in range(chunks):
                o_ref[c] = acc[:, c * cw : (c + 1) * cw].astype(BF16)

    spec = dict(
        grid=(m // tm, n // tn, nk),
        in_specs=[pl.BlockSpec((tk, tm), lambda i, j, k: (k, i)), pl.BlockSpec((tk, tn), lambda i, j, k: (k, j))],
        out_specs=[pl.BlockSpec((chunks, None, tm, cw), lambda i, j, k: index_map(i, j))],
        out_shape=[jax.ShapeDtypeStruct(out_dims, BF16)],
        scratch_shapes=[pltpu.VMEM((tm, tn), F32)],
        compiler_params=_params(3),
        name=name,
    )
    (g,), landed = _call_riding(body, exs, args=(a, b), **spec)
    return (g, *landed) if exs else g


def _out_proj_bwd(dx1b, w_out, tm):
    s, d = dx1b.shape

    def body(dx_ref, w_ref, o_ref):
        o_ref[...] = _dot_nt(dx_ref[...], w_ref[...])

    return _call(
        body,
        grid=(s // tm,),
        in_specs=[pl.BlockSpec((tm, d), lambda i: (i, 0)), pl.BlockSpec((d, d), lambda i: (0, 0))],
        out_specs=pl.BlockSpec((tm, d), lambda i: (i, 0)),
        out_shape=jax.ShapeDtypeStruct((s, d), F32),
        compiler_params=_params(1),
        name="out_proj_bwd",
    )(dx1b, w_out)


def _glu_bwd(y_pre, dmix, wg, bg, go, tm, exs):
    s, w = y_pre.shape
    n = s // tm

    def body(y_ref, dm_ref, wg_ref, bg_ref, go_ref, dy_ref, dwg_ref, dbg_ref, dgo_ref, acc):
        i = pl.program_id(0)

        @pl.when(i == 0)
        def _():
            acc[...] = jnp.zeros_like(acc)
            dbg_ref[...] = jnp.zeros_like(dbg_ref)
            dgo_ref[...] = jnp.zeros_like(dgo_ref)

        yp = y_ref[...]
        yg, yg_grad = _gelu_both(yp)
        ygb = yg.astype(BF16)
        gate = _sigmoid(_dot(ygb, wg_ref[...]) + bg_ref[...])
        ys = yg * gate
        dys, dgt = _rms_bwd(dm_ref[...], ys, _rms(ys), go_ref[...])
        dgo_ref[...] += jnp.sum(dgt, axis=0, keepdims=True)
        dpre = dys * yg * gate * (1.0 - gate)
        dbg_ref[...] += jnp.sum(dpre, axis=0, keepdims=True)
        dpb = dpre.astype(BF16)
        dyg = dys * gate + _dot_nt(dpb, wg_ref[...])
        dy_ref[...] = dyg * yg_grad
        acc[...] += _dot_tn(ygb, dpb)

        @pl.when(i == n - 1)
        def _():
            dwg_ref[...] = acc[...].astype(BF16)

    return _call_riding(
        body,
        exs,
        args=(y_pre, dmix, wg, bg, go),
        grid=(n,),
        in_specs=[
            pl.BlockSpec((tm, w), lambda i: (i, 0)),
            pl.BlockSpec((tm, w), lambda i: (i, 0)),
            pl.BlockSpec((w, w), lambda i: (0, 0)),
            pl.BlockSpec((1, w), lambda i: (0, 0)),
            pl.BlockSpec((1, w), lambda i: (0, 0)),
        ],
        out_specs=[
            pl.BlockSpec((tm, w), lambda i: (i, 0)),
            pl.BlockSpec((w, w), lambda i: (0, 0)),
            pl.BlockSpec((1, w), lambda i: (0, 0)),
            pl.BlockSpec((1, w), lambda i: (0, 0)),
        ],
        out_shape=[
            jax.ShapeDtypeStruct((s, w), F32),
            jax.ShapeDtypeStruct((w, w), BF16),
            jax.ShapeDtypeStruct((1, w), F32),
            jax.ShapeDtypeStruct((1, w), F32),
        ],
        scratch_shapes=[pltpu.VMEM((w, w), F32)],
        compiler_params=_params(1),
        name="glu_bwd",
    )


def _ssm_bwd_state(dy, v2, al, xprev):
    s = dy.shape[0]
    nt, nc, ns = xprev.shape
    hs = ns // 2

    def body(dy_ref, v_ref, al_ref, xp_ref, dys_ref, g_ref, dal_ref, dxp, gs):
        for q in range(PAIRS):
            dys_ref[q, :, 0:LANES] = dy_ref[pl.ds(2 * q, nc, stride=CHUNK), :].astype(BF16)
            dys_ref[q, :, LANES : 2 * LANES] = dy_ref[pl.ds(2 * q + 1, nc, stride=CHUNK), :].astype(BF16)
        acc = _dot(dys_ref[0], v_ref[0])
        for r in range(1, PAIRS):
            acc = acc + _dot(dys_ref[r], v_ref[r])
        dxp[...] = acc
        ar = al_ref[0:1, :]
        ai = al_ref[1:2, :]
        zero = jnp.zeros((1, hs), F32)
        gs[pl.ds(nc - 1, 1), 0:hs] = zero
        gs[pl.ds(nc - 1, 1), hs:ns] = zero

        def step(n, carry):
            gr, gi = carry
            c = nc - 2 - n
            nr = dxp[pl.ds(c + 1, 1), 0:hs] + ar * gr + ai * gi
            ni = dxp[pl.ds(c + 1, 1), hs:ns] + ar * gi - ai * gr
            gs[pl.ds(c, 1), 0:hs] = nr
            gs[pl.ds(c, 1), hs:ns] = ni
            return nr, ni

        lax.fori_loop(0, nc - 1, step, (zero, zero))
        gv = gs[...]
        xv = xp_ref[...]
        gr, gi = gv[:, 0:hs], gv[:, hs:ns]
        xr, xi = xv[:, 0:hs], xv[:, hs:ns]
        dal_ref[0:1, :] = jnp.sum(gr * xr + gi * xi, axis=0, keepdims=True)
        dal_ref[1:2, :] = jnp.sum(gi * xr - gr * xi, axis=0, keepdims=True)
        g_ref[...] = gv.astype(BF16)

    return _call(
        body,
        grid=(nt,),
        in_specs=[
            pl.BlockSpec((s, LANES), lambda t: (0, t)),
            pl.BlockSpec((None, PAIRS, 2 * LANES, ns), lambda t: (t, 0, 0, 0)),
            pl.BlockSpec((None, 2, hs), lambda t: (t, 0, 0)),
            pl.BlockSpec((None, nc, ns), lambda t: (t, 0, 0)),
        ],
        out_specs=[
            pl.BlockSpec((None, PAIRS, nc, 2 * LANES), lambda t: (t, 0, 0, 0)),
            pl.BlockSpec((None, nc, ns), lambda t: (t, 0, 0)),
            pl.BlockSpec((None, 2, hs), lambda t: (t, 0, 0)),
        ],
        out_shape=[
            jax.ShapeDtypeStruct((nt, PAIRS, nc, 2 * LANES), BF16),
            jax.ShapeDtypeStruct((nt, nc, ns), BF16),
            jax.ShapeDtypeStruct((nt, 2, hs), F32),
        ],
        scratch_shapes=[pltpu.VMEM((nc, ns), F32), pltpu.VMEM((nc, ns), F32)],
        compiler_params=_params(1),
        name="ssm_bwd_state",
    )(dy, v2, al, xprev)


def _ssm_bwd_main(us, dys, k2, w2, xprev, gst, consts, s, width, exs):
    nt, _, nc, _ = us.shape
    ns = xprev.shape[-1]
    rep_k, mask_k, rep_s, mask_s = consts
    nk = 2 * LANES

    def body(us_ref, dys_ref, k_ref, w_ref, xp_ref, g_ref, rk_ref, mk_ref, rs_ref, ms_ref,
             du_ref, dk_ref, dw_ref, dv_ref, acc, du2, duf, dkf):
        q = pl.program_id(1)

        @pl.when(q == 0)
        def _():
            dkf[...] = jnp.zeros_like(dkf)

        usq = us_ref[q]
        gb = g_ref[...]
        dw_ref[...] = _gather_blocks(_dot_tn(usq, gb), ms_ref[...], rs_ref[...])
        dv_ref[...] = _gather_blocks(_dot_tn(dys_ref[q], xp_ref[...].astype(BF16)), ms_ref[...], rs_ref[...])
        acc[...] = _dot_nt(gb, w_ref[...])
        for m in range(PAIRS):

            @pl.when(q + m < PAIRS)
            def _():
                dyr = dys_ref[jnp.minimum(q + m, PAIRS - 1)]
                acc[...] += _dot_nt(dyr, k_ref[m])
                dkf[m] += _dot_tn(usq, dyr)

        du2[q] = acc[...]

        @pl.when(q == PAIRS - 1)
        def _():
            for p in range(PAIRS):
                duf[pl.ds(2 * p, nc, stride=CHUNK), :] = du2[p, :, 0:LANES]
                duf[pl.ds(2 * p + 1, nc, stride=CHUNK), :] = du2[p, :, LANES : 2 * LANES]
                dk_ref[p] = _gather_blocks(dkf[p], mk_ref[...], rk_ref[...])
            du_ref[...] = duf[...].astype(BF16)

    def per_tile(rows, cols):
        return pl.BlockSpec((None, PAIRS, rows, cols), lambda t, q: (t, 0, 0, 0))

    def per_pair(rows, cols):
        return pl.BlockSpec((None, None, rows, cols), lambda t, q: (t, q, 0, 0))

    def whole(a):
        return pl.BlockSpec(a.shape, lambda t, q: (0, 0))

    return _call_riding(
        body,
        exs,
        args=(us, dys, k2, w2, xprev, gst, rep_k, mask_k, rep_s, mask_s),
        grid=(nt, PAIRS),
        in_specs=[
            per_tile(nc, nk),
            per_tile(nc, nk),
            per_tile(nk, nk),
            per_pair(nk, ns),
            pl.BlockSpec((None, nc, ns), lambda t, q: (t, 0, 0)),
            pl.BlockSpec((None, nc, ns), lambda t, q: (t, 0, 0)),
            whole(rep_k),
            whole(mask_k),
            whole(rep_s),
            whole(mask_s),
        ],
        out_specs=[
            pl.BlockSpec((s, LANES), lambda t, q: (0, t)),
            per_tile(nk, LANES),
            per_pair(nk, LANES),
            per_pair(nk, LANES),
        ],
        out_shape=[
            jax.ShapeDtypeStruct((s, width), BF16),
            jax.ShapeDtypeStruct((nt, PAIRS, nk, LANES), F32),
            jax.ShapeDtypeStruct((nt, PAIRS, nk, LANES), F32),
            jax.ShapeDtypeStruct((nt, PAIRS, nk, LANES), F32),
        ],
        scratch_shapes=[
            pltpu.VMEM((nc, nk), F32),
            pltpu.VMEM((PAIRS, nc, nk), F32),
            pltpu.VMEM((s, LANES), F32),
            pltpu.VMEM((PAIRS, nk, nk), F32),
        ],
        compiler_params=_params(2),
        name="ssm_bwd_main",
    )


def _sgu_bwd(z, dmix, dz, lng, lnb, wt, wtt, biasb, go, rb, exs):
    s = z.shape[0]
    w = lng.shape[-1]
    heads = wt.shape[0]
    ncol = (LANES, LANES)

    def body(zu_ref, zv_ref, dm_ref, dz_any, lng_ref, lnb_ref, wt_ref, wtt_ref, b_ref, go_ref,
             dz_ref, dw_ref, db_ref, dlg_ref, dlb_ref, dgo_ref, dzv):
        del dz_any
        i = pl.program_id(0)
        p = pl.program_id(1)

        @pl.when(jnp.logical_and(i == 0, p == 0))
        def _():
            dw_ref[...] = jnp.zeros_like(dw_ref)
            db_ref[...] = jnp.zeros_like(db_ref)
            dlg_ref[...] = jnp.zeros_like(dlg_ref)
            dlb_ref[...] = jnp.zeros_like(dlb_ref)
            dgo_ref[...] = jnp.zeros_like(dgo_ref)

        @pl.when(p == 0)
        def _():
            lng_v = lng_ref[...]
            for ck in range(rb // SGU_CHUNK):
                rows = slice(ck * SGU_CHUNK, (ck + 1) * SGU_CHUNK)
                zu = zu_ref[rows, :]
                zv = zv_ref[rows, :]
                u, vhat, rstd, vb, mix, u_grad, v_grad = _sgu_parts(zu, zv, lng_v, lnb_ref[...], wt_ref, b_ref)
                y = u * mix
                dy, dgt = _rms_bwd(dm_ref[rows, :], y, _rms(y), go_ref[...])
                dgo_ref[...] += jnp.sum(dgt, axis=0, keepdims=True)
                du = dy * mix
                dmx = dy * u
                dmb = dmx.astype(BF16)
                dvl = []
                for h in range(heads):
                    cols = slice(h * LANES, (h + 1) * LANES)
                    db_ref[h] += jnp.broadcast_to(jnp.sum(dmx[:, cols], axis=-1, keepdims=True), ncol)
                    dw_ref[h] += _dot_nt(dmb[:, cols], vb[:, cols])
                    dvl.append(_dot(wtt_ref[h], dmb[:, cols]))
                dvln = jnp.concatenate(dvl, axis=1)
                dlg_ref[...] += jnp.sum(dvln * vhat, axis=0, keepdims=True)
                dlb_ref[...] += jnp.sum(dvln, axis=0, keepdims=True)
                dvh = dvln * lng_v
                dv = rstd * (
                    dvh
                    - jnp.mean(dvh, axis=-1, keepdims=True)
                    - vhat * jnp.mean(dvh * vhat, axis=-1, keepdims=True)
                )
                dz_ref[rows, :] = (du * u_grad).astype(BF16)
                dzv[rows, :] = (dv * v_grad).astype(BF16)

        @pl.when(p == 1)
        def _():
            dz_ref[...] = dzv[...]

    return _call_riding(
        body,
        exs,
        args=(z, z, dmix, dz, lng, lnb, wt, wtt, biasb, go),
        grid=(s // rb, 2),
        in_specs=[
            pl.BlockSpec((rb, w), lambda i, p: (i, 1)),
            pl.BlockSpec((rb, w), lambda i, p: (i, 2)),
            pl.BlockSpec((rb, w), lambda i, p: (i, 1)),
            pl.BlockSpec(memory_space=pl.ANY),
            pl.BlockSpec((1, w), lambda i, p: (0, 0)),
            pl.BlockSpec((1, w), lambda i, p: (0, 0)),
            pl.BlockSpec((heads, SGU_CHUNK, SGU_CHUNK), lambda i, p: (0, 0, 0)),
            pl.BlockSpec((heads, SGU_CHUNK, SGU_CHUNK), lambda i, p: (0, 0, 0)),
            pl.BlockSpec((heads, SGU_CHUNK, LANES), lambda i, p: (0, 0, 0)),
            pl.BlockSpec((1, w), lambda i, p: (0, 0)),
        ],
        out_specs=[
            pl.BlockSpec((rb, w), lambda i, p: (i, 1 + p)),
            pl.BlockSpec((heads, SGU_CHUNK, SGU_CHUNK), lambda i, p: (0, 0, 0)),
            pl.BlockSpec((heads, SGU_CHUNK, LANES), lambda i, p: (0, 0, 0)),
            pl.BlockSpec((1, w), lambda i, p: (0, 0)),
            pl.BlockSpec((1, w), lambda i, p: (0, 0)),
            pl.BlockSpec((1, w), lambda i, p: (0, 0)),
        ],
        out_shape=[
            jax.ShapeDtypeStruct(dz.shape, dz.dtype),
            jax.ShapeDtypeStruct((heads, SGU_CHUNK, SGU_CHUNK), F32),
            jax.ShapeDtypeStruct((heads, SGU_CHUNK, LANES), F32),
            jax.ShapeDtypeStruct((1, w), F32),
            jax.ShapeDtypeStruct((1, w), F32),
            jax.ShapeDtypeStruct((1, w), F32),
        ],
        scratch_shapes=[pltpu.VMEM((rb, w), BF16)],
        input_output_aliases={3: 0},
        compiler_params=_params(2),
        name="sgu_bwd",
    )


def _in_proj_bwd(dz, w4, x, dres, g, tm, exs):
    s, d = x.shape
    nj, _, cw = w4.shape

    def body(dz_ref, w_ref, x_ref, dr_ref, g_ref, dx_ref, dg_ref, acc):
        i = pl.program_id(0)
        j = pl.program_id(1)

        @pl.when(j == 0)
        def _():
            acc[...] = jnp.zeros_like(acc)

        @pl.when(jnp.logical_and(i == 0, j == 0))
        def _():
            dg_ref[...] = jnp.zeros_like(dg_ref)

        acc[...] += _dot_nt(dz_ref[...], w_ref[...])

        @pl.when(j == nj - 1)
        def _():
            xv = x_ref[...]
            dx, dgt = _rms_bwd(acc[...], xv, _rms(xv), g_ref[...])
            dx_ref[...] = dr_ref[...] + dx
            dg_ref[...] += jnp.sum(dgt, axis=0, keepdims=True)

    row = pl.BlockSpec((tm, d), lambda i, j: (i, 0))
    vec = pl.BlockSpec((1, d), lambda i, j: (0, 0))
    return _call_riding(
        body,
        exs,
        args=(dz, w4, x, dres, g),
        scratch_shapes=[pltpu.VMEM((tm, d), F32)],
        grid=(s // tm, nj),
        in_specs=[
            pl.BlockSpec((tm, cw), lambda i, j: (i, j)),
            pl.BlockSpec((None, d, cw), lambda i, j: (j, 0, 0)),
            row,
            row,
            vec,
        ],
        out_specs=[row, vec],
        out_shape=[jax.ShapeDtypeStruct((s, d), F32), jax.ShapeDtypeStruct((1, d), F32)],
        compiler_params=_params(2),
        name="in_proj_bwd",
    )


_ANY = pl.BlockSpec(memory_space=pl.ANY)


def _position():
    x, y, c = lax.axis_index("x"), lax.axis_index("y"), lax.axis_index("c")
    return x, y, c, [(1 - x, y), (x, 1 - y), (1 - x, 1 - y)]


def _remote(src, dst, send_sems, recv_sems, k, to):
    return pltpu.make_async_remote_copy(
        src_ref=src, dst_ref=dst, send_sem=send_sems.at[k], recv_sem=recv_sems.at[k], device_id=to, device_id_type=MESH
    )


class _Riding:
    def __init__(self, srcs, out_shapes, n_sems, start, finish):
        self.srcs, self.out_shapes, self.n_sems, self.start, self.finish = srcs, out_shapes, n_sems, start, finish


def _call_riding(body, exs, *, grid, in_specs, out_specs, out_shape, scratch_shapes, args, **kw):
    n_in, n_out, n_scr = len(in_specs), len(out_specs), len(scratch_shapes)
    spec = dict(grid=grid, in_specs=list(in_specs), out_specs=list(out_specs), out_shape=list(out_shape))
    if not exs:
        return _call(body, scratch_shapes=list(scratch_shapes), **spec, **kw)(*args), []
    srcs = [a for ex in exs for a in ex.srcs]
    lands = [a for ex in exs for a in ex.out_shapes]
    xi, xo = len(srcs), len(lands)

    def fused(*refs):
        cin, xin = refs[:n_in], refs[n_in : n_in + xi]
        o = n_in + xi
        cout, xout = refs[o : o + n_out], refs[o + n_out : o + n_out + xo]
        scr = refs[o + n_out + xo :]
        ids = [pl.program_id(a) for a in range(len(grid))]
        first = functools.reduce(jnp.logical_and, [i == 0 for i in ids])
        last = functools.reduce(jnp.logical_and, [i == n - 1 for i, n in zip(ids, grid)])

        def each(half):
            si = so = 0
            for e, ex in enumerate(exs):
                ni, no = len(ex.srcs), len(ex.out_shapes)
                sems = scr[n_scr + 2 * e : n_scr + 2 * e + 2]
                getattr(ex, half)(xin[si : si + ni], xout[so : so + no], *sems)
                si, so = si + ni, so + no

        @pl.when(first)
        def _():
            each("start")

        body(*cin, *cout, *scr[:n_scr])

        @pl.when(last)
        def _():
            each("finish")

    sems = [pltpu.SemaphoreType.DMA((ex.n_sems,)) for ex in exs for _ in range(2)]
    spec["in_specs"] += [_ANY] * xi
    spec["out_specs"] += [_ANY] * xo
    spec["out_shape"] += lands
    outs = _call(fused, scratch_shapes=list(scratch_shapes) + sems, **spec, **kw)(*args, *srcs)
    return outs[:n_out], outs[n_out:]


def _gather_one(shard):
    def own_copies(ins, outs, send, recv):
        x, y, c, chips = _position()
        me = 2 * x + y
        sib = (x, y, 1 - c)
        cps = [_remote(ins[0].at[c], outs[0].at[me, c], send, recv, 0, sib)]
        cps += [_remote(ins[0].at[c], outs[0].at[me, c], send, recv, 1 + k, (px, py, c)) for k, (px, py) in enumerate(chips)]
        return cps, (c, me, sib, chips)

    def start(ins, outs, send, recv):
        for cp in own_copies(ins, outs, send, recv)[0]:
            cp.start()

    def finish(ins, outs, send, recv):
        cps, (c, me, sib, chips) = own_copies(ins, outs, send, recv)
        for k, (px, py) in enumerate(chips):
            landed = outs[0].at[2 * px + py, c]
            _remote(ins[0].at[c], landed, send, recv, 1 + k, (px, py, c)).wait_recv()
            cps.append(_remote(landed, landed, send, recv, 4 + k, sib))
            cps[-1].start()
        _remote(ins[0].at[1 - c], outs[0].at[me, 1 - c], send, recv, 0, sib).wait_recv()
        for k, (px, py) in enumerate(chips):
            theirs = outs[0].at[2 * px + py, 1 - c]
            _remote(theirs, theirs, send, recv, 4 + k, sib).wait_recv()
        for cp in cps:
            cp.wait_send()

    return _Riding([shard], [jax.ShapeDtypeStruct((N_CHIPS,) + shard.shape, shard.dtype)], 7, start, finish)


def _to_owner_one(p):
    def copies(ins, outs, send, recv):
        x, y, c, chips = _position()
        return [_remote(ins[0].at[2 * px + py], outs[0].at[k], send, recv, k, (px, py, c)) for k, (px, py) in enumerate(chips)]

    def start(ins, outs, send, recv):
        for cp in copies(ins, outs, send, recv):
            cp.start()

    def finish(ins, outs, send, recv):
        cps = copies(ins, outs, send, recv)
        for cp in cps:
            cp.wait_recv()
        for cp in cps:
            cp.wait_send()

    return _Riding([p], [jax.ShapeDtypeStruct((3,) + p.shape[1:], p.dtype)], 3, start, finish)


def _to_sibling_one(g):
    def copies(ins, outs, send, recv):
        x, y, c, _ = _position()
        return [_remote(ins[0].at[j, 1 - c], outs[0].at[j], send, recv, j, (x, y, 1 - c)) for j in range(N_CHIPS)]

    def start(ins, outs, send, recv):
        for cp in copies(ins, outs, send, recv):
            cp.start()

    def finish(ins, outs, send, recv):
        cps = copies(ins, outs, send, recv)
        for cp in cps:
            cp.wait_recv()
        for cp in cps:
            cp.wait_send()

    return _Riding([g], [jax.ShapeDtypeStruct((g.shape[0],) + g.shape[2:], g.dtype)], N_CHIPS, start, finish)


def _gather_chips(arrs, name):
    n = len(arrs)

    def body(*refs):
        ins, outs = refs[:n], refs[n : 2 * n]
        send_sems, recv_sems, local_sems = refs[2 * n :]
        x, y, c, chips = _position()
        me = 2 * x + y
        local = [pltpu.make_async_copy(ins[i], outs[i].at[me], local_sems.at[i]) for i in range(n)]
        for cp in local:
            cp.start()
        sends = []
        for k, (px, py) in enumerate(chips):
            for i in range(n):
                sends.append(_remote(ins[i], outs[i].at[me], send_sems, recv_sems, k * n + i, (px, py, c)))
                sends[-1].start()
        for k, (px, py) in enumerate(chips):
            for i in range(n):
                _remote(ins[i], outs[i].at[2 * px + py], send_sems, recv_sems, k * n + i, (px, py, c)).wait_recv()
        for cp in sends:
            cp.wait_send()
        for cp in local:
            cp.wait()

    return _call(
        body,
        in_specs=[_ANY] * n,
        out_specs=[_ANY] * n,
        out_shape=[jax.ShapeDtypeStruct((N_CHIPS,) + a.shape, a.dtype) for a in arrs],
        scratch_shapes=[
            pltpu.SemaphoreType.DMA((3 * n,)),
            pltpu.SemaphoreType.DMA((3 * n,)),
            pltpu.SemaphoreType.DMA((n,)),
        ],
        name=name,
    )(*arrs)


def _pair_exchange(gs, name):
    n = len(gs)

    def body(*refs):
        ins, outs = refs[:n], refs[n : 2 * n]
        send_sems, recv_sems = refs[2 * n :]
        x, y, c, _ = _position()
        sib = (x, y, 1 - c)
        cps = []
        for i in range(n):
            for j in range(N_CHIPS):
                cps.append(_remote(ins[i].at[j, 1 - c], outs[i].at[j], send_sems, recv_sems, i * N_CHIPS + j, sib))
                cps[-1].start()
        for cp in cps:
            cp.wait_recv()
        for cp in cps:
            cp.wait_send()

    return _call(
        body,
        in_specs=[_ANY] * n,
        out_specs=[_ANY] * n,
        out_shape=[jax.ShapeDtypeStruct((g.shape[0],) + g.shape[2:], g.dtype) for g in gs],
        scratch_shapes=[pltpu.SemaphoreType.DMA((n * N_CHIPS,)), pltpu.SemaphoreType.DMA((n * N_CHIPS,))],
        name=name,
    )(*gs)


def _gather_halves(shards, name):
    n = len(shards)

    def body(*refs):
        ins, outs = refs[:n], refs[n : 2 * n]
        ici_send, ici_recv, d2d_send, d2d_recv = refs[2 * n :]
        x, y, c, chips = _position()
        me = 2 * x + y
        sib = (x, y, 1 - c)
        started = []
        for i in range(n):
            started.append(_remote(ins[i].at[c], outs[i].at[me, c], d2d_send, d2d_recv, i, sib))
            started[-1].start()
        for k, (px, py) in enumerate(chips):
            for i in range(n):
                started.append(_remote(ins[i].at[c], outs[i].at[me, c], ici_send, ici_recv, k * n + i, (px, py, c)))
                started[-1].start()
        for k, (px, py) in enumerate(chips):
            for i in range(n):
                landed = outs[i].at[2 * px + py, c]
                _remote(ins[i].at[c], landed, ici_send, ici_recv, k * n + i, (px, py, c)).wait_recv()
                started.append(_remote(landed, landed, d2d_send, d2d_recv, (k + 1) * n + i, sib))
                started[-1].start()
        for i in range(n):
            _remote(ins[i].at[1 - c], outs[i].at[me, 1 - c], d2d_send, d2d_recv, i, sib).wait_recv()
        for k, (px, py) in enumerate(chips):
            for i in range(n):
                theirs = outs[i].at[2 * px + py, 1 - c]
                _remote(theirs, theirs, d2d_send, d2d_recv, (k + 1) * n + i, sib).wait_recv()
        for cp in started:
            cp.wait_send()

    return _call(
        body,
        in_specs=[_ANY] * n,
        out_specs=[_ANY] * n,
        out_shape=[jax.ShapeDtypeStruct((N_CHIPS,) + a.shape, a.dtype) for a in shards],
        scratch_shapes=[
            pltpu.SemaphoreType.DMA((3 * n,)),
            pltpu.SemaphoreType.DMA((3 * n,)),
            pltpu.SemaphoreType.DMA((4 * n,)),
            pltpu.SemaphoreType.DMA((4 * n,)),
        ],
        name=name,
    )(*shards)


def _place_own(shard, gathered, pos, name):
    _, hr, cols = shard.shape
    tr = _row_tile(hr, 512)

    def body(pos_ref, s_ref, g_any, o_ref):
        del pos_ref, g_any
        o_ref[...] = s_ref[...]

    return _call(
        body,
        grid_spec=pltpu.PrefetchScalarGridSpec(
            num_scalar_prefetch=1,
            grid=(hr // tr,),
            in_specs=[pl.BlockSpec((None, tr, cols), lambda i, p: (p[1], i, 0)), _ANY],
            out_specs=pl.BlockSpec((None, None, tr, cols), lambda i, p: (p[0], p[1], i, 0)),
        ),
        out_shape=jax.ShapeDtypeStruct(gathered.shape, gathered.dtype),
        input_output_aliases={2: 0},
        compiler_params=_params(1),
        name=name,
    )(pos, shard, gathered)


def _chip_exchange(ps, name):
    n = len(ps)

    def body(*refs):
        ins, outs = refs[:n], refs[n : 2 * n]
        send_sems, recv_sems = refs[2 * n :]
        x, y, c, chips = _position()
        sends = []
        for k, (px, py) in enumerate(chips):
            for i in range(n):
                sends.append(_remote(ins[i].at[2 * px + py], outs[i].at[k], send_sems, recv_sems, k * n + i, (px, py, c)))
                sends[-1].start()
        for cp in sends:
            cp.wait_recv()
        for cp in sends:
            cp.wait_send()

    return _call(
        body,
        in_specs=[_ANY] * n,
        out_specs=[_ANY] * n,
        out_shape=[jax.ShapeDtypeStruct((3,) + p.shape[1:], p.dtype) for p in ps],
        scratch_shapes=[pltpu.SemaphoreType.DMA((3 * n,)), pltpu.SemaphoreType.DMA((3 * n,))],
        name=name,
    )(*ps)


def _pair_share(fs, name):
    n = len(fs)

    def body(*refs):
        ins, outs = refs[:n], refs[n : 2 * n]
        send_sems, recv_sems = refs[2 * n :]
        x, y, c, _ = _position()
        sib = (x, y, 1 - c)
        sends = [_remote(ins[i].at[c], outs[i].at[c], send_sems, recv_sems, i, sib) for i in range(n)]
        for cp in sends:
            cp.start()
        for i in range(n):
            _remote(ins[i].at[1 - c], outs[i].at[1 - c], send_sems, recv_sems, i, sib).wait_recv()
        for cp in sends:
            cp.wait_send()

    return _call(
        body,
        in_specs=[_ANY] * n,
        out_specs=[_ANY] * n,
        out_shape=[jax.ShapeDtypeStruct(f.shape, f.dtype) for f in fs],
        input_output_aliases={i: i for i in range(n)},
        scratch_shapes=[pltpu.SemaphoreType.DMA((n,)), pltpu.SemaphoreType.DMA((n,))],
        name=name,
    )(*fs)


def _pair_sum(g, r, core, name):
    nj, _, hr, cols = g.shape
    tr = _row_tile(hr, 256)

    def body(c_ref, g_ref, r_ref, o_ref):
        del c_ref
        o_ref[...] = (g_ref[...].astype(F32) + r_ref[...].astype(F32)).astype(o_ref.dtype)

    return _call(
        body,
        grid_spec=pltpu.PrefetchScalarGridSpec(
            num_scalar_prefetch=1,
            grid=(nj, hr // tr),
            in_specs=[
                pl.BlockSpec((None, None, tr, cols), lambda j, i, c: (j, c[0], i, 0)),
                pl.BlockSpec((None, tr, cols), lambda j, i, c: (j, i, 0)),
            ],
            out_specs=pl.BlockSpec((None, tr, cols), lambda j, i, c: (j, i, 0)),
        ),
        out_shape=jax.ShapeDtypeStruct((nj, hr, cols), g.dtype),
        compiler_params=_params(2),
        name=name,
    )(core, g, r)


def _chip_sum(p, q, pos, name):
    nq, hr, cols = q.shape
    tr = _row_tile(hr, 256)

    def body(pos_ref, p_ref, *refs):
        del pos_ref
        o_ref = refs[nq]
        tot = p_ref[...].astype(F32)
        for k in range(nq):
            tot = tot + refs[k][...].astype(F32)
        o_ref[...] = tot

    return _call(
        body,
        grid_spec=pltpu.PrefetchScalarGridSpec(
            num_scalar_prefetch=1,
            grid=(hr // tr,),
            in_specs=[pl.BlockSpec((None, tr, cols), lambda i, s: (s[0], i, 0))]
            + [pl.BlockSpec((None, tr, cols), functools.partial(lambda k, i, s: (k, i, 0), k)) for k in range(nq)],
            out_specs=pl.BlockSpec((None, tr, cols), lambda i, s: (s[1], i, 0)),
        ),
        out_shape=jax.ShapeDtypeStruct((2, hr, cols), F32),
        compiler_params=_params(1),
        name=name,
    )(pos, p, *([q] * nq))


def _adamw(w, g, m, v, name):
    rows, cols = w.shape
    tr = _row_tile(rows, max(8, (2**18) // cols), mult=8)
    c1 = 1.0 - ADAM_B1**ADAM_STEP
    c2 = 1.0 - ADAM_B2**ADAM_STEP

    def body(w_ref, g_ref, m_ref, v_ref, d_ref, nm_ref, nv_ref):
        gv = g_ref[...]
        nm = ADAM_B1 * m_ref[...] + (1.0 - ADAM_B1) * gv
        nv = ADAM_B2 * v_ref[...] + (1.0 - ADAM_B2) * (gv * gv)
        nm_ref[...] = nm
        nv_ref[...] = nv
        d_ref[...] = -ADAM_LR * ((nm / c1) / (jnp.sqrt(nv / c2) + ADAM_EPS) + ADAM_WD * w_ref[...])

    spec = pl.BlockSpec((tr, cols), lambda i: (i, 0))
    sds = jax.ShapeDtypeStruct((rows, cols), F32)
    return _call(
        body,
        grid=(rows // tr,),
        in_specs=[spec] * 4,
        out_specs=[spec] * 3,
        out_shape=[sds] * 3,
        compiler_params=_params(1),
        name=name,
    )(w, g, m, v)


_TILE_ELEMS = 8 * LANES
_FLAT_ROW_MULT = 8 * 2 * N_CHIPS


def _flat_rows(shape):
    n = math.prod(shape)
    return (n + _TILE_ELEMS - 1) // _TILE_ELEMS * 8


def _pack(arrs):
    parts = []
    for a in arrs:
        rows = _flat_rows(a.shape)
        flat = a.reshape(-1).astype(F32)
        flat = jnp.pad(flat, (0, rows * LANES - flat.shape[0]))
        parts.append(flat.reshape(rows, LANES))
    total = sum(p.shape[0] for p in parts)
    pad = -total % _FLAT_ROW_MULT
    if pad:
        parts.append(jnp.zeros((pad, LANES), F32))
    return jnp.concatenate(parts, axis=0)


def _unpack(flat, shapes):
    out, row = [], 0
    for shp in shapes:
        rows = _flat_rows(shp)
        out.append(flat[row : row + rows].reshape(-1)[: math.prod(shp)].reshape(shp))
        row += rows
    return out


def kernel(x, norm_mix_g, w_in, ssm_a_re, ssm_a_im, ssm_b_re, ssm_b_im, ssm_c_re, ssm_c_im, ssm_d, ssm_log_dt, ssm_glu_w, ssm_glu_b, sgu_ln_g, sgu_ln_b, sgu_w, sgu_b, out_norm_ssm_g, out_norm_sgu_g, w_out, norm_mlp_g, w_up, w_down, norm_final_g, loss_target, m_norm_mix_g, m_w_in, m_ssm_a_re, m_ssm_a_im, m_ssm_b_re, m_ssm_b_im, m_ssm_c_re, m_ssm_c_im, m_ssm_d, m_ssm_log_dt, m_ssm_glu_w, m_ssm_glu_b, m_sgu_ln_g, m_sgu_ln_b, m_sgu_w, m_sgu_b, m_out_norm_ssm_g, m_out_norm_sgu_g, m_w_out, m_norm_mlp_g, m_w_up, m_w_down, m_norm_final_g, v_norm_mix_g, v_w_in, v_ssm_a_re, v_ssm_a_im, v_ssm_b_re, v_ssm_b_im, v_ssm_c_re, v_ssm_c_im, v_ssm_d, v_ssm_log_dt, v_ssm_glu_w, v_ssm_glu_b, v_sgu_ln_g, v_sgu_ln_b, v_sgu_w, v_sgu_b, v_out_norm_ssm_g, v_out_norm_sgu_g, v_w_out, v_norm_mlp_g, v_w_up, v_w_down, v_norm_final_g):
    weights = dict(norm_mix_g=norm_mix_g, w_in=w_in, ssm_a_re=ssm_a_re, ssm_a_im=ssm_a_im, ssm_b_re=ssm_b_re, ssm_b_im=ssm_b_im, ssm_c_re=ssm_c_re, ssm_c_im=ssm_c_im, ssm_d=ssm_d, ssm_log_dt=ssm_log_dt, ssm_glu_w=ssm_glu_w, ssm_glu_b=ssm_glu_b, sgu_ln_g=sgu_ln_g, sgu_ln_b=sgu_ln_b, sgu_w=sgu_w, sgu_b=sgu_b, out_norm_ssm_g=out_norm_ssm_g, out_norm_sgu_g=out_norm_sgu_g, w_out=w_out, norm_mlp_g=norm_mlp_g, w_up=w_up, w_down=w_down, norm_final_g=norm_final_g)
    mom_m = dict(norm_mix_g=m_norm_mix_g, w_in=m_w_in, ssm_a_re=m_ssm_a_re, ssm_a_im=m_ssm_a_im, ssm_b_re=m_ssm_b_re, ssm_b_im=m_ssm_b_im, ssm_c_re=m_ssm_c_re, ssm_c_im=m_ssm_c_im, ssm_d=m_ssm_d, ssm_log_dt=m_ssm_log_dt, ssm_glu_w=m_ssm_glu_w, ssm_glu_b=m_ssm_glu_b, sgu_ln_g=m_sgu_ln_g, sgu_ln_b=m_sgu_ln_b, sgu_w=m_sgu_w, sgu_b=m_sgu_b, out_norm_ssm_g=m_out_norm_ssm_g, out_norm_sgu_g=m_out_norm_sgu_g, w_out=m_w_out, norm_mlp_g=m_norm_mlp_g, w_up=m_w_up, w_down=m_w_down, norm_final_g=m_norm_final_g)
    mom_v = dict(norm_mix_g=v_norm_mix_g, w_in=v_w_in, ssm_a_re=v_ssm_a_re, ssm_a_im=v_ssm_a_im, ssm_b_re=v_ssm_b_re, ssm_b_im=v_ssm_b_im, ssm_c_re=v_ssm_c_re, ssm_c_im=v_ssm_c_im, ssm_d=v_ssm_d, ssm_log_dt=v_ssm_log_dt, ssm_glu_w=v_ssm_glu_w, ssm_glu_b=v_ssm_glu_b, sgu_ln_g=v_sgu_ln_g, sgu_ln_b=v_sgu_ln_b, sgu_w=v_sgu_w, sgu_b=v_sgu_b, out_norm_ssm_g=v_out_norm_ssm_g, out_norm_sgu_g=v_out_norm_sgu_g, w_out=v_w_out, norm_mlp_g=v_norm_mlp_g, w_up=v_w_up, w_down=v_w_down, norm_final_g=v_norm_final_g)
    names = list(weights)
    large = ["w_in", "ssm_glu_w", "w_out", "w_up", "w_down"]
    small = [n for n in names if n not in large]

    s, d = x.shape[1], x.shape[2]
    xs = x.reshape(s, d)
    target = loss_target.reshape(s, d)
    width = ssm_glu_w.shape[-1]
    ff = w_down.shape[1] * N_CHIPS
    tm = min(512, s)
    core = lax.axis_index("c").astype(jnp.int32).reshape(1)
    chip = (2 * lax.axis_index("x") + lax.axis_index("y")).astype(jnp.int32).reshape(1)
    pos = jnp.concatenate([chip, core])

    shards = [w[0].astype(BF16).reshape(2, w.shape[1] // 2, w.shape[2]) for w in (w_in, ssm_glu_w, w_out, w_up, w_down)]
    ssm_args = (ssm_a_re[0], ssm_a_im[0], ssm_b_re[0], ssm_b_im[0], ssm_c_re[0], ssm_c_im[0], ssm_d[0], ssm_log_dt[0])
    (k2c, w2c, v2c, al), ssm_vjp = jax.vjp(_ssm_mats, *ssm_args)
    consts = _spread_consts(ssm_b_re.shape[-1], ssm_a_re.shape[-1])
    (k2b, w2b, v2b), (w_in_g,) = _ssm_spread(k2c, w2c, v2c, consts, [_gather_one(shards[0])])
    w_in4 = _place_own(shards[0], w_in_g, pos, "place_w_in").reshape(N_CHIPS, d, w_in.shape[2])
    causal = jnp.tril(jnp.ones((SGU_CHUNK, SGU_CHUNK), dtype=bool))
    wt = jnp.where(causal[None], sgu_w[0], 0.0)
    wtb = wt.astype(BF16)
    wttb = jnp.swapaxes(wt, 1, 2).astype(BF16)
    heads = sgu_w.shape[1]
    biasb = jnp.broadcast_to(sgu_b[0][:, :, None], (heads, SGU_CHUNK, LANES))

    tm_big = min(1024, s)
    (z, h1b), (glu_g, up_g) = _in_proj(
        xs, norm_mix_g, w_in4, tm_big, [_gather_one(shards[1]), _gather_one(shards[3])])
    wg_full = _place_own(shards[1], glu_g, pos, "place_glu_w").reshape(width, width)
    up4 = _place_own(shards[3], up_g, pos, "place_w_up").reshape(N_CHIPS, d, w_up.shape[2])
    (y_pre, xprev, us), (out_g,) = _ssm_fwd(z, k2b, w2b, v2b, al, [_gather_one(shards[2])])
    w_out_full = _place_own(shards[2], out_g, pos, "place_w_out").reshape(d, d)
    mixed = _glu_fwd(y_pre, wg_full, ssm_glu_b, out_norm_ssm_g, d, tm)
    mixed = _sgu_fwd(z, mixed, sgu_ln_g, sgu_ln_b, wtb, biasb, out_norm_sgu_g, tm)
    x1 = _out_proj(xs, mixed, w_out_full, tm)
    tf = min(1024, up4.shape[-1])
    (up, h2b), (down_g,) = _mlp_up(x1, norm_mlp_g, up4, tm_big, tf, [_gather_one(shards[4])])
    w_down_full = _place_own(shards[4], down_g, pos, "place_w_down").reshape(ff, d)
    loss_part, dx2, dx2b, d_norm_final = _mlp_down_loss(
        x1, up, w_down_full, target, norm_final_g.reshape(1, d), tm, tf)

    dup, dh2 = _mlp_bwd(dx2b, up, up4, w_down_full, tm_big, min(512, up4.shape[-1]))
    dx1, dx1b, d_norm_mlp = _norm_bwd(dh2, x1, dx2, norm_mlp_g, min(256, s), "norm_mlp_bwd")
    hr_big = d // 2
    tkk = min(1024, s)
    g_down = _grad_w(
        up, dx2b, (N_CHIPS, 2, hr_big, d),
        lambda i, j: (i // (2 * (hr_big // min(1024, hr_big))), (i // (hr_big // min(1024, hr_big))) % 2,
                      i % (hr_big // min(1024, hr_big)), j),
        min(1024, hr_big), d, tkk, True, "grad_w_down")
    tn_up = min(2048, up4.shape[-1])
    per_up = up4.shape[-1] // tn_up
    g_up, sib_down = _grad_w(
        h2b, dup, (N_CHIPS, 2, hr_big, up4.shape[-1]),
        lambda i, j: (j // per_up, i // (hr_big // min(1024, hr_big)), i % (hr_big // min(1024, hr_big)), j % per_up),
        min(1024, hr_big), tn_up, tkk, False, "grad_w_up", exs=[_to_sibling_one(g_down)])
    pair_down = _pair_sum(g_down, sib_down, core, "pair_sum_w_down")
    dmix = _out_proj_bwd(dx1b, w_out_full, tm)
    hr_out = d // (2 * N_CHIPS)
    g_out, sib_up = _grad_w(
        mixed, dx1b, (1, 1, d, d), lambda i, j: (0, 0, i, j), min(1024, d), d, tkk, False, "grad_w_out",
        exs=[_to_sibling_one(g_up)])
    g_out = g_out.reshape(N_CHIPS, 2, hr_out, d)
    pair_up = _pair_sum(g_up, sib_up, core, "pair_sum_w_up")
    (dy_pre, g_glu, d_glu_b, d_norm_ssm), (sib_out,) = _glu_bwd(
        y_pre, dmix, wg_full, ssm_glu_b, out_norm_ssm_g, tm, [_to_sibling_one(g_out)])
    pair_out = _pair_sum(g_out, sib_out, core, "pair_sum_w_out")
    dys, gst, d_al = _ssm_bwd_state(dy_pre, v2b, al, xprev)
    (dz, d_k2, d_w2, d_v2), (chips_down,) = _ssm_bwd_main(
        us, dys, k2b, w2b, xprev, gst, consts, s, z.shape[1], [_to_owner_one(pair_down)])
    (dz, d_wt, d_bias, d_ln_g, d_ln_b, d_norm_sgu), (chips_up,) = _sgu_bwd(
        z, dmix, dz, sgu_ln_g, sgu_ln_b, wtb, wttb, biasb, out_norm_sgu_g, tm, [_to_owner_one(pair_up)])
    cw_in = w_in4.shape[-1]
    g_in, chips_out = _grad_w(
        h1b, dz, (N_CHIPS, 2, hr_big, cw_in),
        lambda i, j: (j, i // (hr_big // min(1024, hr_big)), i % (hr_big // min(1024, hr_big)), 0),
        min(1024, hr_big), 2 * cw_in, tkk, False, "grad_w_in", exs=[_to_owner_one(pair_out)], chunks=2)
    (sib_in,) = _pair_exchange([g_in], "w_in_to_sibling")
    pair_in = _pair_sum(g_in, sib_in, core, "pair_sum_w_in")
    (grad_x, d_norm_mix), (chips_in,) = _in_proj_bwd(dz, w_in4, xs, dx1, norm_mix_g, tm, [_to_owner_one(pair_in)])

    d_ssm = ssm_vjp((d_k2, d_w2, d_v2, d_al))
    small_grads = dict(
        norm_mix_g=d_norm_mix, ssm_a_re=d_ssm[0], ssm_a_im=d_ssm[1], ssm_b_re=d_ssm[2], ssm_b_im=d_ssm[3],
        ssm_c_re=d_ssm[4], ssm_c_im=d_ssm[5], ssm_d=d_ssm[6], ssm_log_dt=d_ssm[7], ssm_glu_b=d_glu_b,
        sgu_ln_g=d_ln_g, sgu_ln_b=d_ln_b, sgu_w=jnp.where(causal[None], d_wt, 0.0), sgu_b=d_bias[:, :, 0],
        out_norm_ssm_g=d_norm_ssm, out_norm_sgu_g=d_norm_sgu, norm_mlp_g=d_norm_mlp, norm_final_g=d_norm_final)
    flat = _pack([small_grads[n] for n in small])
    hr_small = flat.shape[0] // (2 * N_CHIPS)
    g_small = flat.reshape(N_CHIPS, 2, hr_small, LANES)

    hr_glu = width // (2 * N_CHIPS)
    grads = [g_glu.reshape(N_CHIPS, 2, hr_glu, width), g_small]
    tags = ["glu_w", "small"]
    from_sib = _pair_exchange(grads, "grads_to_sibling")
    pair = [_pair_sum(g, r, core, "pair_sum_" + t) for g, r, t in zip(grads, from_sib, tags)]
    from_chips = _chip_exchange(pair, "grads_to_owner")
    pair = [pair_in, pair[0], pair_out, pair_up, pair_down, pair[1]]
    from_chips = [chips_in, from_chips[0], chips_out, chips_up, chips_down, from_chips[1]]
    tags = ["w_in", "glu_w", "w_out", "w_up", "w_down", "small"]
    halves = [_chip_sum(p, q, pos, "chip_sum_" + t) for p, q, t in zip(pair, from_chips, tags)]
    owned = _pair_share(halves, "grads_to_both_cores")
    (small_all,) = _gather_chips([owned[5]], "gather_small_grads")
    small_flat = small_all.reshape(flat.shape)

    grad_out, delta_out, m_out, v_out = {}, {}, {}, {}
    for n, g in zip(large, owned[:5]):
        shp = weights[n].shape
        g2 = g.reshape(shp[1], shp[2])
        dl, nm, nv = _adamw(weights[n][0], g2, mom_m[n][0], mom_v[n][0], "adamw_" + n)
        grad_out[n], delta_out[n], m_out[n], v_out[n] = g2.reshape(shp), dl.reshape(shp), nm.reshape(shp), nv.reshape(shp)
    shapes = [weights[n].shape for n in small]
    dl, nm, nv = _adamw(
        _pack([weights[n] for n in small]), small_flat, _pack([mom_m[n] for n in small]),
        _pack([mom_v[n] for n in small]), "adamw_small")
    for n, g, a, b, c in zip(small, _unpack(small_flat, shapes), _unpack(dl, shapes), _unpack(nm, shapes), _unpack(nv, shapes)):
        grad_out[n], delta_out[n], m_out[n], v_out[n] = g, a, b, c

    loss = lax.psum(loss_part[0, 0], ("x", "y", "c"))
    return (loss, grad_x.reshape(x.shape), *[grad_out[n] for n in names], *[delta_out[n] for n in names],
            *[m_out[n] for n in names], *[v_out[n] for n in names])
```

```python
import functools
import math

import numpy as np
import jax
import jax.numpy as jnp
from jax import lax
from jax.experimental import pallas as pl
from jax.experimental.pallas import tpu as pltpu

F32 = jnp.float32
BF16 = jnp.bfloat16
MESH = pl.DeviceIdType.MESH
HIGHEST = lax.Precision.HIGHEST

EPS = 1e-6
ADAM_LR = 0.001
ADAM_B1 = 0.9
ADAM_B2 = 0.999
ADAM_EPS = 1e-08
ADAM_WD = 0.01
ADAM_STEP = 10

N_CHIPS = 4
LANES = 128
SSM_GROUP = 16
SSM_STATE = 64
GROUPS_PER_TILE = LANES // SSM_GROUP
CHUNK = 16
PAIRS = CHUNK // 2
SGU_CHUNK = 128
VMEM_LIMIT = 56 * 2**20


def _params(n_axes, vmem=VMEM_LIMIT):
    return pltpu.CompilerParams(dimension_semantics=("arbitrary",) * n_axes, vmem_limit_bytes=vmem)


def _call(body, **kw):
    return pl.pallas_call(body, **kw)


def _dot(a, b):
    return jnp.dot(a, b, preferred_element_type=F32)


def _dot_nt(a, b):
    return lax.dot_general(a, b, (((1,), (1,)), ((), ())), preferred_element_type=F32)


def _dot_tn(a, b):
    return lax.dot_general(a, b, (((0,), (0,)), ((), ())), preferred_element_type=F32)


_GELU_K = math.sqrt(2.0 / math.pi)
_GELU_C = 0.044715


def _gelu_both(x):
    x2 = x * x
    t = jnp.tanh(x * (_GELU_K + (_GELU_K * _GELU_C) * x2))
    hx = 0.5 * x
    onep = 1.0 + t
    return hx * onep, 0.5 * onep + hx * (1.0 - t * t) * (_GELU_K + (3.0 * _GELU_K * _GELU_C) * x2)


def _gelu(x):
    return _gelu_both(x)[0]


def _sigmoid(x):
    return 1.0 / (1.0 + jnp.exp(-x))


def _rms(x):
    return lax.rsqrt(jnp.mean(x * x, axis=-1, keepdims=True) + EPS)


def _rms_bwd(dy, x, r, g):
    a = dy * g
    dx = r * a - x * (r * r * r) * jnp.mean(a * x, axis=-1, keepdims=True)
    return dx, dy * x * r


def _row_tile(rows, target, mult=16):
    for t in range(min(rows, target), 0, -1):
        if rows % t == 0 and t % mult == 0:
            return t
    return rows


def _ssm_mats(a_re, a_im, b_re, b_im, c_re, c_im, d, log_dt):
    g, p = a_re.shape
    h = b_re.shape[-1]
    nt = g // GROUPS_PER_TILE
    dt = jnp.exp(log_dt)[:, None]
    lr, li = a_re * dt, a_im * dt

    def apow(l):
        mag = jnp.exp(lr * l)
        return mag * jnp.cos(li * l), mag * jnp.sin(li * l)

    ar, ai = apow(1.0)
    den = a_re * a_re + a_im * a_im
    qr = ((ar - 1.0) * a_re + ai * a_im) / den
    qi = (ai * a_re - (ar - 1.0) * a_im) / den
    bt_re, bt_im = jnp.swapaxes(b_re, 1, 2), jnp.swapaxes(b_im, 1, 2)
    bbr = qr[:, None, :] * bt_re - qi[:, None, :] * bt_im
    bbi = qr[:, None, :] * bt_im + qi[:, None, :] * bt_re
    b_same = jnp.concatenate([bbr, bbi], axis=-1)
    b_swap = jnp.concatenate([bbi, bbr], axis=-1)
    ls = jnp.arange(CHUNK + 1, dtype=F32)[:, None, None]
    pr, pi = apow(ls)
    p_same = jnp.concatenate([pr, pr], axis=-1)[:, :, None, :]
    p_sign = jnp.concatenate([-pi, pi], axis=-1)[:, :, None, :]
    tj = p_same[:CHUNK] * b_same[None] + p_sign[:CHUNK] * b_swap[None]
    c_conj = jnp.concatenate([c_re, -c_im], axis=-1)
    k = jnp.einsum("goq,lgiq->lgio", c_conj, tj, precision=HIGHEST)
    k = k.at[0].add(d[:, :, None] * jnp.eye(h, dtype=F32))

    def tiles(a):
        return a.reshape(PAIRS, 2, g, h, a.shape[-1])

    zero = jnp.zeros_like(k[:1])
    kz = jnp.concatenate([zero, zero, k], axis=0).reshape(PAIRS + 1, 2, g, h, h)
    even, odd = kz[:, 0], kz[:, 1]
    fill = jnp.zeros((PAIRS, g, h, LANES - 2 * h), F32)
    row0 = jnp.concatenate([even[1:], odd[1:], fill], axis=-1)
    row1 = jnp.concatenate([odd[:-1], even[1:], fill], axis=-1)
    k2c = tiles(jnp.stack([row0, row1], axis=1).reshape(2 * PAIRS, g, h, LANES))
    w2c = tiles(tj[::-1])
    c_cross = jnp.concatenate([-c_im, -c_re], axis=-1)
    p_imag = jnp.concatenate([pi, pi], axis=-1)[:, :, None, :]
    v2c = tiles(c_conj[None] * p_same[1:] + c_cross[None] * p_imag[1:])
    al = jnp.stack([pr[CHUNK].reshape(nt, -1), pi[CHUNK].reshape(nt, -1)], axis=1)
    return k2c, w2c, v2c, al


def _spread_consts(h, p):
    gg = GROUPS_PER_TILE
    row_g = (np.arange(2 * gg * h) // h) % gg
    colk = np.arange(2 * gg * h)
    rep_k = np.zeros((LANES, 2 * gg * h), np.float32)
    rep_k[(colk // (gg * h)) * h + colk % h, colk] = 1.0
    mask_k = (row_g[:, None] == ((colk // h) % gg)[None, :]).astype(np.float32)
    cols = np.arange(2 * gg * p)
    rep_s = np.zeros((2 * p, 2 * gg * p), np.float32)
    rep_s[(cols // (gg * p)) * p + cols % p, cols] = 1.0
    mask_s = (row_g[:, None] == ((cols // p) % gg)[None, :]).astype(np.float32)
    return tuple(jnp.asarray(a, BF16) for a in (rep_k, mask_k, rep_s, mask_s))


def _ssm_spread(k2c, w2c, v2c, consts, exs):
    h = k2c.shape[3]
    nt = k2c.shape[2] // GROUPS_PER_TILE
    rep_k, mask_k, rep_s, mask_s = consts
    nk, ns = rep_k.shape[1], rep_s.shape[1]

    def body(k_ref, w_ref, v_ref, rk_ref, mk_ref, rs_ref, ms_ref, ko_ref, wo_ref, vo_ref):
        for q in range(PAIRS):
            for c_ref, o_ref, r_ref, m_ref in ((k_ref, ko_ref, rk_ref, mk_ref), (w_ref, wo_ref, rs_ref, ms_ref),
                                               (v_ref, vo_ref, rs_ref, ms_ref)):
                rows = c_ref[q].reshape(nk, LANES).astype(BF16)
                o_ref[q] = (_dot(rows, r_ref[...]) * m_ref[...]).astype(BF16)

    compact = pl.BlockSpec((PAIRS, 2, GROUPS_PER_TILE, h, LANES), lambda t: (0, 0, t, 0, 0))

    def tile(cols):
        return pl.BlockSpec((None, PAIRS, nk, cols), lambda t: (t, 0, 0, 0))

    def whole(a):
        return pl.BlockSpec(a.shape, lambda t: (0, 0))

    return _call_riding(
        body,
        exs,
        args=(k2c, w2c, v2c, rep_k, mask_k, rep_s, mask_s),
        scratch_shapes=[],
        grid=(nt,),
        in_specs=[compact, compact, compact, whole(rep_k), whole(mask_k), whole(rep_s), whole(mask_s)],
        out_specs=[tile(nk), tile(ns), tile(ns)],
        out_shape=[
            jax.ShapeDtypeStruct((nt, PAIRS, nk, nk), BF16),
            jax.ShapeDtypeStruct((nt, PAIRS, nk, ns), BF16),
            jax.ShapeDtypeStruct((nt, PAIRS, nk, ns), BF16),
        ],
        compiler_params=_params(1),
        name="ssm_spread",
    )


def _gather_blocks(full, mask, rep):
    return _dot_nt((full * mask).astype(BF16), rep)


def _in_proj(x, g, w4, tm, exs):
    s, d = x.shape
    nj, _, cw = w4.shape

    def body(x_ref, g_ref, w_ref, z_ref, h_ref):
        @pl.when(pl.program_id(1) == 0)
        def _():
            xv = x_ref[...]
            h_ref[...] = (xv * _rms(xv) * g_ref[...]).astype(BF16)

        z_ref[...] = _dot(h_ref[...], w_ref[...])

    return _call_riding(
        body,
        exs,
        scratch_shapes=[],
        args=(x, g, w4),
        grid=(s // tm, nj),
        in_specs=[
            pl.BlockSpec((tm, d), lambda i, j: (i, 0)),
            pl.BlockSpec((1, d), lambda i, j: (0, 0)),
            pl.BlockSpec((None, d, cw), lambda i, j: (j, 0, 0)),
        ],
        out_specs=[pl.BlockSpec((tm, cw), lambda i, j: (i, j)), pl.BlockSpec((tm, d), lambda i, j: (i, 0))],
        out_shape=[jax.ShapeDtypeStruct((s, nj * cw), F32), jax.ShapeDtypeStruct((s, d), BF16)],
        compiler_params=_params(2),
        name="in_proj",
    )


def _ssm_fwd(z, k2, w2, v2, al, exs):
    s = z.shape[0]
    nt = k2.shape[0]
    nc = s // CHUNK
    ns = w2.shape[-1]
    hs = ns // 2

    def body(u_ref, k_ref, w_ref, v_ref, al_ref, y_ref, xp_ref, us_ref, xloc):
        for q in range(PAIRS):
            us_ref[q, :, 0:LANES] = u_ref[pl.ds(2 * q, nc, stride=CHUNK), :].astype(BF16)
            us_ref[q, :, LANES : 2 * LANES] = u_ref[pl.ds(2 * q + 1, nc, stride=CHUNK), :].astype(BF16)
        acc = _dot(us_ref[0], w_ref[0])
        for q in range(1, PAIRS):
            acc = acc + _dot(us_ref[q], w_ref[q])
        xloc[...] = acc
        ar = al_ref[0:1, :]
        ai = al_ref[1:2, :]

        def step(c, carry):
            xr, xi = carry
            xp_ref[pl.ds(c, 1), 0:hs] = xr
            xp_ref[pl.ds(c, 1), hs:ns] = xi
            lr = xloc[pl.ds(c, 1), 0:hs]
            li = xloc[pl.ds(c, 1), hs:ns]
            return ar * xr - ai * xi + lr, ar * xi + ai * xr + li

        zero = jnp.zeros((1, hs), F32)
        lax.fori_loop(0, nc, step, (zero, zero))
        xpb = xp_ref[...].astype(BF16)
        for r in range(PAIRS):
            acc = _dot_nt(xpb, v_ref[r])
            for q in range(r + 1):
                acc = acc + _dot(us_ref[q], k_ref[r - q])
            y_ref[pl.ds(2 * r, nc, stride=CHUNK), :] = acc[:, 0:LANES]
            y_ref[pl.ds(2 * r + 1, nc, stride=CHUNK), :] = acc[:, LANES : 2 * LANES]

    return _call_riding(
        body,
        exs,
        args=(z, k2, w2, v2, al),
        grid=(nt,),
        in_specs=[
            pl.BlockSpec((s, LANES), lambda t: (0, t)),
            pl.BlockSpec((None, PAIRS, 2 * LANES, 2 * LANES), lambda t: (t, 0, 0, 0)),
            pl.BlockSpec((None, PAIRS, 2 * LANES, ns), lambda t: (t, 0, 0, 0)),
            pl.BlockSpec((None, PAIRS, 2 * LANES, ns), lambda t: (t, 0, 0, 0)),
            pl.BlockSpec((None, 2, hs), lambda t: (t, 0, 0)),
        ],
        out_specs=[
            pl.BlockSpec((s, LANES), lambda t: (0, t)),
            pl.BlockSpec((None, nc, ns), lambda t: (t, 0, 0)),
            pl.BlockSpec((None, PAIRS, nc, 2 * LANES), lambda t: (t, 0, 0, 0)),
        ],
        out_shape=[
            jax.ShapeDtypeStruct((s, nt * LANES), F32),
            jax.ShapeDtypeStruct((nt, nc, ns), F32),
            jax.ShapeDtypeStruct((nt, PAIRS, nc, 2 * LANES), BF16),
        ],
        scratch_shapes=[pltpu.VMEM((nc, ns), F32)],
        compiler_params=_params(1),
        name="ssm_fwd",
    )


def _glu_fwd(y_pre, wg, bg, go, d_model, tm):
    s, w = y_pre.shape

    def body(y_ref, wg_ref, bg_ref, go_ref, o_ref):
        yg = _gelu(y_ref[...])
        gate = _sigmoid(_dot(yg.astype(BF16), wg_ref[...]) + bg_ref[...])
        ys = yg * gate
        o_ref[...] = (ys * _rms(ys) * go_ref[...]).astype(BF16)

    return _call(
        body,
        grid=(s // tm,),
        in_specs=[
            pl.BlockSpec((tm, w), lambda i: (i, 0)),
            pl.BlockSpec((w, w), lambda i: (0, 0)),
            pl.BlockSpec((1, w), lambda i: (0, 0)),
            pl.BlockSpec((1, w), lambda i: (0, 0)),
        ],
        out_specs=pl.BlockSpec((tm, w), lambda i: (i, 0)),
        out_shape=jax.ShapeDtypeStruct((s, d_model), BF16),
        compiler_params=_params(1),
        name="glu_fwd",
    )(y_pre, wg, bg, go)


def _sgu_parts(zu, zv, lng, lnb, wt_ref, bias_ref):
    u, u_grad = _gelu_both(zu)
    v, v_grad = _gelu_both(zv)
    mu = jnp.mean(v, axis=-1, keepdims=True)
    vc = v - mu
    rstd = lax.rsqrt(jnp.mean(vc * vc, axis=-1, keepdims=True) + EPS)
    vhat = vc * rstd
    vb = (vhat * lng + lnb).astype(BF16)
    heads = wt_ref.shape[0]
    mix = jnp.concatenate(
        [_dot(wt_ref[h], vb[:, h * LANES : (h + 1) * LANES]) + bias_ref[h] for h in range(heads)], axis=1
    )
    return u, vhat, rstd, vb, mix, u_grad, v_grad


def _sgu_fwd(z, mixed, lng, lnb, wt, biasb, go, rb):
    s = z.shape[0]
    w = lng.shape[-1]
    heads = wt.shape[0]

    def body(zu_ref, zv_ref, m_any, lng_ref, lnb_ref, wt_ref, b_ref, go_ref, o_ref):
        del m_any
        for ck in range(rb // SGU_CHUNK):
            rows = slice(ck * SGU_CHUNK, (ck + 1) * SGU_CHUNK)
            u, _, _, _, mix, _, _ = _sgu_parts(zu_ref[rows, :], zv_ref[rows, :], lng_ref[...], lnb_ref[...], wt_ref, b_ref)
            y = u * mix
            o_ref[rows, :] = (y * _rms(y) * go_ref[...]).astype(BF16)

    return _call(
        body,
        grid=(s // rb,),
        in_specs=[
            pl.BlockSpec((rb, w), lambda i: (i, 1)),
            pl.BlockSpec((rb, w), lambda i: (i, 2)),
            pl.BlockSpec(memory_space=pl.ANY),
            pl.BlockSpec((1, w), lambda i: (0, 0)),
            pl.BlockSpec((1, w), lambda i: (0, 0)),
            pl.BlockSpec((heads, SGU_CHUNK, SGU_CHUNK), lambda i: (0, 0, 0)),
            pl.BlockSpec((heads, SGU_CHUNK, LANES), lambda i: (0, 0, 0)),
            pl.BlockSpec((1, w), lambda i: (0, 0)),
        ],
        out_specs=pl.BlockSpec((rb, w), lambda i: (i, 1)),
        out_shape=jax.ShapeDtypeStruct(mixed.shape, mixed.dtype),
        input_output_aliases={2: 0},
        compiler_params=_params(1),
        name="sgu_fwd",
    )(z, z, mixed, lng, lnb, wt, biasb, go)


def _out_proj(x, mixed, w_out, tm):
    s, d = x.shape

    def body(x_ref, m_ref, w_ref, o_ref):
        o_ref[...] = x_ref[...] + _dot(m_ref[...], w_ref[...])

    return _call(
        body,
        grid=(s // tm,),
        in_specs=[
            pl.BlockSpec((tm, d), lambda i: (i, 0)),
            pl.BlockSpec((tm, d), lambda i: (i, 0)),
            pl.BlockSpec((d, d), lambda i: (0, 0)),
        ],
        out_specs=pl.BlockSpec((tm, d), lambda i: (i, 0)),
        out_shape=jax.ShapeDtypeStruct((s, d), F32),
        compiler_params=_params(1),
        name="out_proj",
    )(x, mixed, w_out)


def _mlp_up(x1, g, w_up4, tm, tf, exs):
    s, d = x1.shape
    nj, _, cw = w_up4.shape
    per = cw // tf

    def body(x_ref, g_ref, wu_ref, up_ref, h_ref):
        @pl.when(pl.program_id(1) == 0)
        def _():
            xv = x_ref[...]
            h_ref[...] = (xv * _rms(xv) * g_ref[...]).astype(BF16)

        up_ref[...] = _dot(h_ref[...], wu_ref[...]).astype(BF16)

    return _call_riding(
        body,
        exs,
        scratch_shapes=[],
        args=(x1, g, w_up4),
        grid=(s // tm, nj * per),
        in_specs=[
            pl.BlockSpec((tm, d), lambda i, f: (i, 0)),
            pl.BlockSpec((1, d), lambda i, f: (0, 0)),
            pl.BlockSpec((None, d, tf), lambda i, f: (f // per, 0, f % per)),
        ],
        out_specs=[pl.BlockSpec((tm, tf), lambda i, f: (i, f)), pl.BlockSpec((tm, d), lambda i, f: (i, 0))],
        out_shape=[jax.ShapeDtypeStruct((s, nj * cw), BF16), jax.ShapeDtypeStruct((s, d), BF16)],
        compiler_params=_params(2),
        name="mlp_up",
    )


def _mlp_down_loss(x1, up, w_down, target, g, tm, tf):
    s, d = x1.shape
    ff = w_down.shape[0]
    nf = ff // tf

    def body(x_ref, up_ref, wd_ref, t_ref, g_ref, loss_ref, dx_ref, dxb_ref, dg_ref, acc):
        i = pl.program_id(0)
        f = pl.program_id(1)

        @pl.when(f == 0)
        def _():
            acc[...] = x_ref[...]

        @pl.when(jnp.logical_and(i == 0, f == 0))
        def _():
            loss_ref[...] = jnp.zeros_like(loss_ref)
            dg_ref[...] = jnp.zeros_like(dg_ref)

        a = jnp.maximum(up_ref[...].astype(F32), 0.0)
        acc[...] += _dot((a * a).astype(BF16), wd_ref[...])

        @pl.when(f == nf - 1)
        def _():
            xv = acc[...]
            r = _rms(xv)
            gv = g_ref[...]
            diff = xv * r * gv - t_ref[...]
            loss_ref[...] += 0.5 * jnp.sum(jnp.mean(diff * diff, axis=-1, keepdims=True), axis=0, keepdims=True)
            dx, dgt = _rms_bwd(diff * (1.0 / d), xv, r, gv)
            dx_ref[...] = dx
            dxb_ref[...] = dx.astype(BF16)
            dg_ref[...] += jnp.sum(dgt, axis=0, keepdims=True)

    row = pl.BlockSpec((tm, d), lambda i, f: (i, 0))
    vec = pl.BlockSpec((1, d), lambda i, f: (0, 0))
    return _call(
        body,
        grid=(s // tm, nf),
        in_specs=[
            row,
            pl.BlockSpec((tm, tf), lambda i, f: (i, f)),
            pl.BlockSpec((tf, d), lambda i, f: (f, 0)),
            row,
            vec,
        ],
        out_specs=[pl.BlockSpec((1, 1), lambda i, f: (0, 0)), row, row, vec],
        out_shape=[
            jax.ShapeDtypeStruct((1, 1), F32),
            jax.ShapeDtypeStruct((s, d), F32),
            jax.ShapeDtypeStruct((s, d), BF16),
            jax.ShapeDtypeStruct((1, d), F32),
        ],
        scratch_shapes=[pltpu.VMEM((tm, d), F32)],
        compiler_params=_params(2),
        name="mlp_down_loss",
    )(x1, up, w_down, target, g)


def _mlp_bwd(dx2b, up, w_up4, w_down, tm, tf):
    s, d = dx2b.shape
    ff = w_down.shape[0]
    cw = w_up4.shape[-1]
    per = cw // tf

    def body(dx_ref, up_ref, wu_ref, wd_ref, dup_ref, dh_ref):
        @pl.when(pl.program_id(1) == 0)
        def _():
            dh_ref[...] = jnp.zeros_like(dh_ref)

        dact = _dot_nt(dx_ref[...], wd_ref[...])
        dupb = (dact * (2.0 * jnp.maximum(up_ref[...].astype(F32), 0.0))).astype(BF16)
        dup_ref[...] = dupb
        dh_ref[...] += _dot_nt(dupb, wu_ref[...])

    return _call(
        body,
        grid=(s // tm, ff // tf),
        in_specs=[
            pl.BlockSpec((tm, d), lambda i, f: (i, 0)),
            pl.BlockSpec((tm, tf), lambda i, f: (i, f)),
            pl.BlockSpec((None, d, tf), lambda i, f: (f // per, 0, f % per)),
            pl.BlockSpec((tf, d), lambda i, f: (f, 0)),
        ],
        out_specs=[pl.BlockSpec((tm, tf), lambda i, f: (i, f)), pl.BlockSpec((tm, d), lambda i, f: (i, 0))],
        out_shape=[jax.ShapeDtypeStruct((s, ff), BF16), jax.ShapeDtypeStruct((s, d), F32)],
        compiler_params=_params(2),
        name="mlp_bwd",
    )(dx2b, up, w_up4, w_down)


def _norm_bwd(dh, x, dres, g, tm, name):
    s, d = x.shape

    def body(dh_ref, x_ref, dr_ref, g_ref, dx_ref, dxb_ref, dg_ref):
        @pl.when(pl.program_id(0) == 0)
        def _():
            dg_ref[...] = jnp.zeros_like(dg_ref)

        xv = x_ref[...]
        dx, dgt = _rms_bwd(dh_ref[...], xv, _rms(xv), g_ref[...])
        tot = dr_ref[...] + dx
        dx_ref[...] = tot
        dxb_ref[...] = tot.astype(BF16)
        dg_ref[...] += jnp.sum(dgt, axis=0, keepdims=True)

    row = pl.BlockSpec((tm, d), lambda i: (i, 0))
    vec = pl.BlockSpec((1, d), lambda i: (0, 0))
    return _call(
        body,
        grid=(s // tm,),
        in_specs=[row, row, row, vec],
        out_specs=[row, row, vec],
        out_shape=[
            jax.ShapeDtypeStruct((s, d), F32),
            jax.ShapeDtypeStruct((s, d), BF16),
            jax.ShapeDtypeStruct((1, d), F32),
        ],
        compiler_params=_params(1),
        name=name,
    )(dh, x, dres, g)


def _grad_w(a, b, out_dims, index_map, tm, tn, tk, relu2, name, exs=(), chunks=1):
    t, m = a.shape
    n = b.shape[1]
    nk = t // tk
    cw = tn // chunks

    def body(a_ref, b_ref, o_ref, acc):
        k = pl.program_id(2)

        @pl.when(k == 0)
        def _():
            acc[...] = jnp.zeros_like(acc)

        av = a_ref[...]
        if relu2:
            r = jnp.maximum(av.astype(F32), 0.0)
            av = (r * r).astype(BF16)
        acc[...] += _dot_tn(av, b_ref[...])

        @pl.when(k == nk - 1)
        def _():
            for c in range(chunks):
                o_ref[c] = acc[:, c * cw : (c + 1) * cw].astype(BF16)

    spec = dict(
        grid=(m // tm, n // tn, nk),
        in_specs=[pl.BlockSpec((tk, tm), lambda i, j, k: (k, i)), pl.BlockSpec((tk, tn), lambda i, j, k: (k, j))],
        out_specs=[pl.BlockSpec((chunks, None, tm, cw), lambda i, j, k: index_map(i, j))],
        out_shape=[jax.ShapeDtypeStruct(out_dims, BF16)],
        scratch_shapes=[pltpu.VMEM((tm, tn), F32)],
        compiler_params=_params(3),
        name=name,
    )
    (g,), landed = _call_riding(body, exs, args=(a, b), **spec)
    return (g, *landed) if exs else g


def _out_proj_bwd(dx1b, w_out, tm):
    s, d = dx1b.shape

    def body(dx_ref, w_ref, o_ref):
        o_ref[...] = _dot_nt(dx_ref[...], w_ref[...])

    return _call(
        body,
        grid=(s // tm,),
        in_specs=[pl.BlockSpec((tm, d), lambda i: (i, 0)), pl.BlockSpec((d, d), lambda i: (0, 0))],
        out_specs=pl.BlockSpec((tm, d), lambda i: (i, 0)),
        out_shape=jax.ShapeDtypeStruct((s, d), F32),
        compiler_params=_params(1),
        name="out_proj_bwd",
    )(dx1b, w_out)


def _glu_bwd(y_pre, dmix, wg, bg, go, tm, exs):
    s, w = y_pre.shape
    n = s // tm

    def body(y_ref, dm_ref, wg_ref, bg_ref, go_ref, dy_ref, dwg_ref, dbg_ref, dgo_ref, acc):
        i = pl.program_id(0)

        @pl.when(i == 0)
        def _():
            acc[...] = jnp.zeros_like(acc)
            dbg_ref[...] = jnp.zeros_like(dbg_ref)
            dgo_ref[...] = jnp.zeros_like(dgo_ref)

        yp = y_ref[...]
        yg, yg_grad = _gelu_both(yp)
        ygb = yg.astype(BF16)
        gate = _sigmoid(_dot(ygb, wg_ref[...]) + bg_ref[...])
        ys = yg * gate
        dys, dgt = _rms_bwd(dm_ref[...], ys, _rms(ys), go_ref[...])
        dgo_ref[...] += jnp.sum(dgt, axis=0, keepdims=True)
        dpre = dys * yg * gate * (1.0 - gate)
        dbg_ref[...] += jnp.sum(dpre, axis=0, keepdims=True)
        dpb = dpre.astype(BF16)
        dyg = dys * gate + _dot_nt(dpb, wg_ref[...])
        dy_ref[...] = dyg * yg_grad
        acc[...] += _dot_tn(ygb, dpb)

        @pl.when(i == n - 1)
        def _():
            dwg_ref[...] = acc[...].astype(BF16)

    return _call_riding(
        body,
        exs,
        args=(y_pre, dmix, wg, bg, go),
        grid=(n,),
        in_specs=[
            pl.BlockSpec((tm, w), lambda i: (i, 0)),
            pl.BlockSpec((tm, w), lambda i: (i, 0)),
            pl.BlockSpec((w, w), lambda i: (0, 0)),
            pl.BlockSpec((1, w), lambda i: (0, 0)),
            pl.BlockSpec((1, w), lambda i: (0, 0)),
        ],
        out_specs=[
            pl.BlockSpec((tm, w), lambda i: (i, 0)),
            pl.BlockSpec((w, w), lambda i: (0, 0)),
            pl.BlockSpec((1, w), lambda i: (0, 0)),
            pl.BlockSpec((1, w), lambda i: (0, 0)),
        ],
        out_shape=[
            jax.ShapeDtypeStruct((s, w), F32),
            jax.ShapeDtypeStruct((w, w), BF16),
            jax.ShapeDtypeStruct((1, w), F32),
            jax.ShapeDtypeStruct((1, w), F32),
        ],
        scratch_shapes=[pltpu.VMEM((w, w), F32)],
        compiler_params=_params(1),
        name="glu_bwd",
    )


def _ssm_bwd_state(dy, v2, al, xprev):
    s = dy.shape[0]
    nt, nc, ns = xprev.shape
    hs = ns // 2

    def body(dy_ref, v_ref, al_ref, xp_ref, dys_ref, g_ref, dal_ref, dxp, gs):
        for q in range(PAIRS):
            dys_ref[q, :, 0:LANES] = dy_ref[pl.ds(2 * q, nc, stride=CHUNK), :].astype(BF16)
            dys_ref[q, :, LANES : 2 * LANES] = dy_ref[pl.ds(2 * q + 1, nc, stride=CHUNK), :].astype(BF16)
        acc = _dot(dys_ref[0], v_ref[0])
        for r in range(1, PAIRS):
            acc = acc + _dot(dys_ref[r], v_ref[r])
        dxp[...] = acc
        ar = al_ref[0:1, :]
        ai = al_ref[1:2, :]
        zero = jnp.zeros((1, hs), F32)
        gs[pl.ds(nc - 1, 1), 0:hs] = zero
        gs[pl.ds(nc - 1, 1), hs:ns] = zero

        def step(n, carry):
            gr, gi = carry
            c = nc - 2 - n
            nr = dxp[pl.ds(c + 1, 1), 0:hs] + ar * gr + ai * gi
            ni = dxp[pl.ds(c + 1, 1), hs:ns] + ar * gi - ai * gr
            gs[pl.ds(c, 1), 0:hs] = nr
            gs[pl.ds(c, 1), hs:ns] = ni
            return nr, ni

        lax.fori_loop(0, nc - 1, step, (zero, zero))
        gv = gs[...]
        xv = xp_ref[...]
        gr, gi = gv[:, 0:hs], gv[:, hs:ns]
        xr, xi = xv[:, 0:hs], xv[:, hs:ns]
        dal_ref[0:1, :] = jnp.sum(gr * xr + gi * xi, axis=0, keepdims=True)
        dal_ref[1:2, :] = jnp.sum(gi * xr - gr * xi, axis=0, keepdims=True)
        g_ref[...] = gv.astype(BF16)

    return _call(
        body,
        grid=(nt,),
        in_specs=[
            pl.BlockSpec((s, LANES), lambda t: (0, t)),
            pl.BlockSpec((None, PAIRS, 2 * LANES, ns), lambda t: (t, 0, 0, 0)),
            pl.BlockSpec((None, 2, hs), lambda t: (t, 0, 0)),
            pl.BlockSpec((None, nc, ns), lambda t: (t, 0, 0)),
        ],
        out_specs=[
            pl.BlockSpec((None, PAIRS, nc, 2 * LANES), lambda t: (t, 0, 0, 0)),
            pl.BlockSpec((None, nc, ns), lambda t: (t, 0, 0)),
            pl.BlockSpec((None, 2, hs), lambda t: (t, 0, 0)),
        ],
        out_shape=[
            jax.ShapeDtypeStruct((nt, PAIRS, nc, 2 * LANES), BF16),
            jax.ShapeDtypeStruct((nt, nc, ns), BF16),
            jax.ShapeDtypeStruct((nt, 2, hs), F32),
        ],
        scratch_shapes=[pltpu.VMEM((nc, ns), F32), pltpu.VMEM((nc, ns), F32)],
        compiler_params=_params(1),
        name="ssm_bwd_state",
    )(dy, v2, al, xprev)


def _ssm_bwd_main(us, dys, k2, w2, xprev, gst, consts, s, width, exs):
    nt, _, nc, _ = us.shape
    ns = xprev.shape[-1]
    rep_k, mask_k, rep_s, mask_s = consts
    nk = 2 * LANES
    compact = (2, GROUPS_PER_TILE, SSM_GROUP, LANES)

    def body(us_ref, dys_ref, k_ref, w_ref, xp_ref, g_ref, rk_ref, mk_ref, rs_ref, ms_ref,
             du_ref, dk_ref, dw_ref, dv_ref, acc, du2, duf, dkf):
        q = pl.program_id(1)

        @pl.when(q == 0)
        def _():
            dkf[...] = jnp.zeros_like(dkf)

        usq = us_ref[q]
        gb = g_ref[...]
        dw_ref[...] = _gather_blocks(_dot_tn(usq, gb), ms_ref[...], rs_ref[...]).reshape(compact)
        dv = _gather_blocks(_dot_tn(dys_ref[q], xp_ref[...].astype(BF16)), ms_ref[...], rs_ref[...])
        dv_ref[...] = dv.reshape(compact)
        acc[...] = _dot_nt(gb, w_ref[...])
        for m in range(PAIRS):

            @pl.when(q + m < PAIRS)
            def _():
                dyr = dys_ref[jnp.minimum(q + m, PAIRS - 1)]
                acc[...] += _dot_nt(dyr, k_ref[m])
                dkf[m] += _dot_tn(usq, dyr)

        du2[q] = acc[...]

        @pl.when(q == PAIRS - 1)
        def _():
            for p in range(PAIRS):
                duf[pl.ds(2 * p, nc, stride=CHUNK), :] = du2[p, :, 0:LANES]
                duf[pl.ds(2 * p + 1, nc, stride=CHUNK), :] = du2[p, :, LANES : 2 * LANES]
                dk_ref[p] = _gather_blocks(dkf[p], mk_ref[...], rk_ref[...]).reshape(compact)
            du_ref[...] = duf[...].astype(BF16)

    def per_tile(rows, cols):
        return pl.BlockSpec((None, PAIRS, rows, cols), lambda t, q: (t, 0, 0, 0))

    def per_pair(rows, cols):
        return pl.BlockSpec((None, None, rows, cols), lambda t, q: (t, q, 0, 0))

    def whole(a):
        return pl.BlockSpec(a.shape, lambda t, q: (0, 0))

    return _call_riding(
        body,
        exs,
        args=(us, dys, k2, w2, xprev, gst, rep_k, mask_k, rep_s, mask_s),
        grid=(nt, PAIRS),
        in_specs=[
            per_tile(nc, nk),
            per_tile(nc, nk),
            per_tile(nk, nk),
            per_pair(nk, ns),
            pl.BlockSpec((None, nc, ns), lambda t, q: (t, 0, 0)),
            pl.BlockSpec((None, nc, ns), lambda t, q: (t, 0, 0)),
            whole(rep_k),
            whole(mask_k),
            whole(rep_s),
            whole(mask_s),
        ],
        out_specs=[
            pl.BlockSpec((s, LANES), lambda t, q: (0, t)),
            pl.BlockSpec((PAIRS,) + compact, lambda t, q: (0, 0, t, 0, 0)),
            pl.BlockSpec((None,) + compact, lambda t, q: (q, 0, t, 0, 0)),
            pl.BlockSpec((None,) + compact, lambda t, q: (q, 0, t, 0, 0)),
        ],
        out_shape=[
            jax.ShapeDtypeStruct((s, width), BF16),
            jax.ShapeDtypeStruct((PAIRS, 2, nt * GROUPS_PER_TILE, SSM_GROUP, LANES), F32),
            jax.ShapeDtypeStruct((PAIRS, 2, nt * GROUPS_PER_TILE, SSM_GROUP, LANES), F32),
            jax.ShapeDtypeStruct((PAIRS, 2, nt * GROUPS_PER_TILE, SSM_GROUP, LANES), F32),
        ],
        scratch_shapes=[
            pltpu.VMEM((nc, nk), F32),
            pltpu.VMEM((PAIRS, nc, nk), F32),
            pltpu.VMEM((s, LANES), F32),
            pltpu.VMEM((PAIRS, nk, nk), F32),
        ],
        compiler_params=_params(2),
        name="ssm_bwd_main",
    )


def _sgu_bwd(z, dmix, dz, lng, lnb, wt, wtt, biasb, go, rb, exs):
    s = z.shape[0]
    w = lng.shape[-1]
    heads = wt.shape[0]
    ncol = (LANES, LANES)

    def body(zu_ref, zv_ref, dm_ref, dz_any, lng_ref, lnb_ref, wt_ref, wtt_ref, b_ref, go_ref,
             dz_ref, dw_ref, db_ref, dlg_ref, dlb_ref, dgo_ref, dzv):
        del dz_any
        i = pl.program_id(0)
        p = pl.program_id(1)

        @pl.when(jnp.logical_and(i == 0, p == 0))
        def _():
            dw_ref[...] = jnp.zeros_like(dw_ref)
            db_ref[...] = jnp.zeros_like(db_ref)
            dlg_ref[...] = jnp.zeros_like(dlg_ref)
            dlb_ref[...] = jnp.zeros_like(dlb_ref)
            dgo_ref[...] = jnp.zeros_like(dgo_ref)

        @pl.when(p == 0)
        def _():
            lng_v = lng_ref[...]
            for ck in range(rb // SGU_CHUNK):
                rows = slice(ck * SGU_CHUNK, (ck + 1) * SGU_CHUNK)
                zu = zu_ref[rows, :]
                zv = zv_ref[rows, :]
                u, vhat, rstd, vb, mix, u_grad, v_grad = _sgu_parts(zu, zv, lng_v, lnb_ref[...], wt_ref, b_ref)
                y = u * mix
                dy, dgt = _rms_bwd(dm_ref[rows, :], y, _rms(y), go_ref[...])
                dgo_ref[...] += jnp.sum(dgt, axis=0, keepdims=True)
                du = dy * mix
                dmx = dy * u
                dmb = dmx.astype(BF16)
                dvl = []
                for h in range(heads):
                    cols = slice(h * LANES, (h + 1) * LANES)
                    db_ref[h] += jnp.broadcast_to(jnp.sum(dmx[:, cols], axis=-1, keepdims=True), ncol)
                    dw_ref[h] += _dot_nt(dmb[:, cols], vb[:, cols])
                    dvl.append(_dot(wtt_ref[h], dmb[:, cols]))
                dvln = jnp.concatenate(dvl, axis=1)
                dlg_ref[...] += jnp.sum(dvln * vhat, axis=0, keepdims=True)
                dlb_ref[...] += jnp.sum(dvln, axis=0, keepdims=True)
                dvh = dvln * lng_v
                dv = rstd * (
                    dvh
                    - jnp.mean(dvh, axis=-1, keepdims=True)
                    - vhat * jnp.mean(dvh * vhat, axis=-1, keepdims=True)
                )
                dz_ref[rows, :] = (du * u_grad).astype(BF16)
                dzv[rows, :] = (dv * v_grad).astype(BF16)

        @pl.when(p == 1)
        def _():
            dz_ref[...] = dzv[...]

    return _call_riding(
        body,
        exs,
        args=(z, z, dmix, dz, lng, lnb, wt, wtt, biasb, go),
        grid=(s // rb, 2),
        in_specs=[
            pl.BlockSpec((rb, w), lambda i, p: (i, 1)),
            pl.BlockSpec((rb, w), lambda i, p: (i, 2)),
            pl.BlockSpec((rb, w), lambda i, p: (i, 1)),
            pl.BlockSpec(memory_space=pl.ANY),
            pl.BlockSpec((1, w), lambda i, p: (0, 0)),
            pl.BlockSpec((1, w), lambda i, p: (0, 0)),
            pl.BlockSpec((heads, SGU_CHUNK, SGU_CHUNK), lambda i, p: (0, 0, 0)),
            pl.BlockSpec((heads, SGU_CHUNK, SGU_CHUNK), lambda i, p: (0, 0, 0)),
            pl.BlockSpec((heads, SGU_CHUNK, LANES), lambda i, p: (0, 0, 0)),
            pl.BlockSpec((1, w), lambda i, p: (0, 0)),
        ],
        out_specs=[
            pl.BlockSpec((rb, w), lambda i, p: (i, 1 + p)),
            pl.BlockSpec((heads, SGU_CHUNK, SGU_CHUNK), lambda i, p: (0, 0, 0)),
            pl.BlockSpec((heads, SGU_CHUNK, LANES), lambda i, p: (0, 0, 0)),
            pl.BlockSpec((1, w), lambda i, p: (0, 0)),
            pl.BlockSpec((1, w), lambda i, p: (0, 0)),
            pl.BlockSpec((1, w), lambda i, p: (0, 0)),
        ],
        out_shape=[
            jax.ShapeDtypeStruct(dz.shape, dz.dtype),
            jax.ShapeDtypeStruct((heads, SGU_CHUNK, SGU_CHUNK), F32),
            jax.ShapeDtypeStruct((heads, SGU_CHUNK, LANES), F32),
            jax.ShapeDtypeStruct((1, w), F32),
            jax.ShapeDtypeStruct((1, w), F32),
            jax.ShapeDtypeStruct((1, w), F32),
        ],
        scratch_shapes=[pltpu.VMEM((rb, w), BF16)],
        input_output_aliases={3: 0},
        compiler_params=_params(2),
        name="sgu_bwd",
    )


def _in_proj_bwd(dz, w4, x, dres, g, tm, exs):
    s, d = x.shape
    nj, _, cw = w4.shape

    def body(dz_ref, w_ref, x_ref, dr_ref, g_ref, dx_ref, dg_ref, acc):
        i = pl.program_id(0)
        j = pl.program_id(1)

        @pl.when(j == 0)
        def _():
            acc[...] = jnp.zeros_like(acc)

        @pl.when(jnp.logical_and(i == 0, j == 0))
        def _():
            dg_ref[...] = jnp.zeros_like(dg_ref)

        acc[...] += _dot_nt(dz_ref[...], w_ref[...])

        @pl.when(j == nj - 1)
        def _():
            xv = x_ref[...]
            dx, dgt = _rms_bwd(acc[...], xv, _rms(xv), g_ref[...])
            dx_ref[...] = dr_ref[...] + dx
            dg_ref[...] += jnp.sum(dgt, axis=0, keepdims=True)

    row = pl.BlockSpec((tm, d), lambda i, j: (i, 0))
    vec = pl.BlockSpec((1, d), lambda i, j: (0, 0))
    return _call_riding(
        body,
        exs,
        args=(dz, w4, x, dres, g),
        scratch_shapes=[pltpu.VMEM((tm, d), F32)],
        grid=(s // tm, nj),
        in_specs=[
            pl.BlockSpec((tm, cw), lambda i, j: (i, j)),
            pl.BlockSpec((None, d, cw), lambda i, j: (j, 0, 0)),
            row,
            row,
            vec,
        ],
        out_specs=[row, vec],
        out_shape=[jax.ShapeDtypeStruct((s, d), F32), jax.ShapeDtypeStruct((1, d), F32)],
        compiler_params=_params(2),
        name="in_proj_bwd",
    )


_ANY = pl.BlockSpec(memory_space=pl.ANY)


def _position():
    x, y, c = lax.axis_index("x"), lax.axis_index("y"), lax.axis_index("c")
    return x, y, c, [(1 - x, y), (x, 1 - y), (1 - x, 1 - y)]


def _remote(src, dst, send_sems, recv_sems, k, to):
    return pltpu.make_async_remote_copy(
        src_ref=src, dst_ref=dst, send_sem=send_sems.at[k], recv_sem=recv_sems.at[k], device_id=to, device_id_type=MESH
    )


class _Riding:
    def __init__(self, srcs, out_shapes, n_sems, start, finish):
        self.srcs, self.out_shapes, self.n_sems, self.start, self.finish = srcs, out_shapes, n_sems, start, finish


def _call_riding(body, exs, *, grid, in_specs, out_specs, out_shape, scratch_shapes, args, **kw):
    n_in, n_out, n_scr = len(in_specs), len(out_specs), len(scratch_shapes)
    spec = dict(grid=grid, in_specs=list(in_specs), out_specs=list(out_specs), out_shape=list(out_shape))
    if not exs:
        return _call(body, scratch_shapes=list(scratch_shapes), **spec, **kw)(*args), []
    srcs = [a for ex in exs for a in ex.srcs]
    lands = [a for ex in exs for a in ex.out_shapes]
    xi, xo = len(srcs), len(lands)

    def fused(*refs):
        cin, xin = refs[:n_in], refs[n_in : n_in + xi]
        o = n_in + xi
        cout, xout = refs[o : o + n_out], refs[o + n_out : o + n_out + xo]
        scr = refs[o + n_out + xo :]
        ids = [pl.program_id(a) for a in range(len(grid))]
        first = functools.reduce(jnp.logical_and, [i == 0 for i in ids])
        last = functools.reduce(jnp.logical_and, [i == n - 1 for i, n in zip(ids, grid)])

        def each(half):
            si = so = 0
            for e, ex in enumerate(exs):
                ni, no = len(ex.srcs), len(ex.out_shapes)
                sems = scr[n_scr + 2 * e : n_scr + 2 * e + 2]
                getattr(ex, half)(xin[si : si + ni], xout[so : so + no], *sems)
                si, so = si + ni, so + no

        @pl.when(first)
        def _():
            each("start")

        body(*cin, *cout, *scr[:n_scr])

        @pl.when(last)
        def _():
            each("finish")

    sems = [pltpu.SemaphoreType.DMA((ex.n_sems,)) for ex in exs for _ in range(2)]
    spec["in_specs"] += [_ANY] * xi
    spec["out_specs"] += [_ANY] * xo
    spec["out_shape"] += lands
    outs = _call(fused, scratch_shapes=list(scratch_shapes) + sems, **spec, **kw)(*args, *srcs)
    return outs[:n_out], outs[n_out:]


def _gather_one(shard):
    def own_copies(ins, outs, send, recv):
        x, y, c, chips = _position()
        me = 2 * x + y
        sib = (x, y, 1 - c)
        cps = [_remote(ins[0].at[c], outs[0].at[me, c], send, recv, 0, sib)]
        cps += [_remote(ins[0].at[c], outs[0].at[me, c], send, recv, 1 + k, (px, py, c)) for k, (px, py) in enumerate(chips)]
        return cps, (c, me, sib, chips)

    def start(ins, outs, send, recv):
        for cp in own_copies(ins, outs, send, recv)[0]:
            cp.start()

    def finish(ins, outs, send, recv):
        cps, (c, me, sib, chips) = own_copies(ins, outs, send, recv)
        for k, (px, py) in enumerate(chips):
            landed = outs[0].at[2 * px + py, c]
            _remote(ins[0].at[c], landed, send, recv, 1 + k, (px, py, c)).wait_recv()
            cps.append(_remote(landed, landed, send, recv, 4 + k, sib))
            cps[-1].start()
        _remote(ins[0].at[1 - c], outs[0].at[me, 1 - c], send, recv, 0, sib).wait_recv()
        for k, (px, py) in enumerate(chips):
            theirs = outs[0].at[2 * px + py, 1 - c]
            _remote(theirs, theirs, send, recv, 4 + k, sib).wait_recv()
        for cp in cps:
            cp.wait_send()

    return _Riding([shard], [jax.ShapeDtypeStruct((N_CHIPS,) + shard.shape, shard.dtype)], 7, start, finish)


def _to_owner_one(p):
    def copies(ins, outs, send, recv):
        x, y, c, chips = _position()
        return [_remote(ins[0].at[2 * px + py], outs[0].at[k], send, recv, k, (px, py, c)) for k, (px, py) in enumerate(chips)]

    def start(ins, outs, send, recv):
        for cp in copies(ins, outs, send, recv):
            cp.start()

    def finish(ins, outs, send, recv):
        cps = copies(ins, outs, send, recv)
        for cp in cps:
            cp.wait_recv()
        for cp in cps:
            cp.wait_send()

    return _Riding([p], [jax.ShapeDtypeStruct((3,) + p.shape[1:], p.dtype)], 3, start, finish)


def _to_sibling_one(g):
    def copies(ins, outs, send, recv):
        x, y, c, _ = _position()
        return [_remote(ins[0].at[j, 1 - c], outs[0].at[j], send, recv, j, (x, y, 1 - c)) for j in range(N_CHIPS)]

    def start(ins, outs, send, recv):
        for cp in copies(ins, outs, send, recv):
            cp.start()

    def finish(ins, outs, send, recv):
        cps = copies(ins, outs, send, recv)
        for cp in cps:
            cp.wait_recv()
        for cp in cps:
            cp.wait_send()

    return _Riding([g], [jax.ShapeDtypeStruct((g.shape[0],) + g.shape[2:], g.dtype)], N_CHIPS, start, finish)


def _gather_chips(arrs, name):
    n = len(arrs)

    def body(*refs):
        ins, outs = refs[:n], refs[n : 2 * n]
        send_sems, recv_sems, local_sems = refs[2 * n :]
        x, y, c, chips = _position()
        me = 2 * x + y
        local = [pltpu.make_async_copy(ins[i], outs[i].at[me], local_sems.at[i]) for i in range(n)]
        for cp in local:
            cp.start()
        sends = []
        for k, (px, py) in enumerate(chips):
            for i in range(n):
                sends.append(_remote(ins[i], outs[i].at[me], send_sems, recv_sems, k * n + i, (px, py, c)))
                sends[-1].start()
        for k, (px, py) in enumerate(chips):
            for i in range(n):
                _remote(ins[i], outs[i].at[2 * px + py], send_sems, recv_sems, k * n + i, (px, py, c)).wait_recv()
        for cp in sends:
            cp.wait_send()
        for cp in local:
            cp.wait()

    return _call(
        body,
        in_specs=[_ANY] * n,
        out_specs=[_ANY] * n,
        out_shape=[jax.ShapeDtypeStruct((N_CHIPS,) + a.shape, a.dtype) for a in arrs],
        scratch_shapes=[
            pltpu.SemaphoreType.DMA((3 * n,)),
            pltpu.SemaphoreType.DMA((3 * n,)),
            pltpu.SemaphoreType.DMA((n,)),
        ],
        name=name,
    )(*arrs)


def _pair_exchange(gs, name):
    n = len(gs)

    def body(*refs):
        ins, outs = refs[:n], refs[n : 2 * n]
        send_sems, recv_sems = refs[2 * n :]
        x, y, c, _ = _position()
        sib = (x, y, 1 - c)
        cps = []
        for i in range(n):
            for j in range(N_CHIPS):
                cps.append(_remote(ins[i].at[j, 1 - c], outs[i].at[j], send_sems, recv_sems, i * N_CHIPS + j, sib))
                cps[-1].start()
        for cp in cps:
            cp.wait_recv()
        for cp in cps:
            cp.wait_send()

    return _call(
        body,
        in_specs=[_ANY] * n,
        out_specs=[_ANY] * n,
        out_shape=[jax.ShapeDtypeStruct((g.shape[0],) + g.shape[2:], g.dtype) for g in gs],
        scratch_shapes=[pltpu.SemaphoreType.DMA((n * N_CHIPS,)), pltpu.SemaphoreType.DMA((n * N_CHIPS,))],
        name=name,
    )(*gs)


def _gather_halves(shards, name):
    n = len(shards)

    def body(*refs):
        ins, outs = refs[:n], refs[n : 2 * n]
        ici_send, ici_recv, d2d_send, d2d_recv = refs[2 * n :]
        x, y, c, chips = _position()
        me = 2 * x + y
        sib = (x, y, 1 - c)
        started = []
        for i in range(n):
            started.append(_remote(ins[i].at[c], outs[i].at[me, c], d2d_send, d2d_recv, i, sib))
            started[-1].start()
        for k, (px, py) in enumerate(chips):
            for i in range(n):
                started.append(_remote(ins[i].at[c], outs[i].at[me, c], ici_send, ici_recv, k * n + i, (px, py, c)))
                started[-1].start()
        for k, (px, py) in enumerate(chips):
            for i in range(n):
                landed = outs[i].at[2 * px + py, c]
                _remote(ins[i].at[c], landed, ici_send, ici_recv, k * n + i, (px, py, c)).wait_recv()
                started.append(_remote(landed, landed, d2d_send, d2d_recv, (k + 1) * n + i, sib))
                started[-1].start()
        for i in range(n):
            _remote(ins[i].at[1 - c], outs[i].at[me, 1 - c], d2d_send, d2d_recv, i, sib).wait_recv()
        for k, (px, py) in enumerate(chips):
            for i in range(n):
                theirs = outs[i].at[2 * px + py, 1 - c]
                _remote(theirs, theirs, d2d_send, d2d_recv, (k + 1) * n + i, sib).wait_recv()
        for cp in started:
            cp.wait_send()

    return _call(
        body,
        in_specs=[_ANY] * n,
        out_specs=[_ANY] * n,
        out_shape=[jax.ShapeDtypeStruct((N_CHIPS,) + a.shape, a.dtype) for a in shards],
        scratch_shapes=[
            pltpu.SemaphoreType.DMA((3 * n,)),
            pltpu.SemaphoreType.DMA((3 * n,)),
            pltpu.SemaphoreType.DMA((4 * n,)),
            pltpu.SemaphoreType.DMA((4 * n,)),
        ],
        name=name,
    )(*shards)


def _place_own(shard, gathered, pos, name):
    _, hr, cols = shard.shape
    tr = _row_tile(hr, 512)

    def body(pos_ref, s_ref, g_any, o_ref):
        del pos_ref, g_any
        o_ref[...] = s_ref[...]

    return _call(
        body,
        grid_spec=pltpu.PrefetchScalarGridSpec(
            num_scalar_prefetch=1,
            grid=(hr // tr,),
            in_specs=[pl.BlockSpec((None, tr, cols), lambda i, p: (p[1], i, 0)), _ANY],
            out_specs=pl.BlockSpec((None, None, tr, cols), lambda i, p: (p[0], p[1], i, 0)),
        ),
        out_shape=jax.ShapeDtypeStruct(gathered.shape, gathered.dtype),
        input_output_aliases={2: 0},
        compiler_params=_params(1),
        name=name,
    )(pos, shard, gathered)


def _chip_exchange(ps, name):
    n = len(ps)

    def body(*refs):
        ins, outs = refs[:n], refs[n : 2 * n]
        send_sems, recv_sems = refs[2 * n :]
        x, y, c, chips = _position()
        sends = []
        for k, (px, py) in enumerate(chips):
            for i in range(n):
                sends.append(_remote(ins[i].at[2 * px + py], outs[i].at[k], send_sems, recv_sems, k * n + i, (px, py, c)))
                sends[-1].start()
        for cp in sends:
            cp.wait_recv()
        for cp in sends:
            cp.wait_send()

    return _call(
        body,
        in_specs=[_ANY] * n,
        out_specs=[_ANY] * n,
        out_shape=[jax.ShapeDtypeStruct((3,) + p.shape[1:], p.dtype) for p in ps],
        scratch_shapes=[pltpu.SemaphoreType.DMA((3 * n,)), pltpu.SemaphoreType.DMA((3 * n,))],
        name=name,
    )(*ps)


def _pair_share(fs, name):
    n = len(fs)

    def body(*refs):
        ins, outs = refs[:n], refs[n : 2 * n]
        send_sems, recv_sems = refs[2 * n :]
        x, y, c, _ = _position()
        sib = (x, y, 1 - c)
        sends = [_remote(ins[i].at[c], outs[i].at[c], send_sems, recv_sems, i, sib) for i in range(n)]
        for cp in sends:
            cp.start()
        for i in range(n):
            _remote(ins[i].at[1 - c], outs[i].at[1 - c], send_sems, recv_sems, i, sib).wait_recv()
        for cp in sends:
            cp.wait_send()

    return _call(
        body,
        in_specs=[_ANY] * n,
        out_specs=[_ANY] * n,
        out_shape=[jax.ShapeDtypeStruct(f.shape, f.dtype) for f in fs],
        input_output_aliases={i: i for i in range(n)},
        scratch_shapes=[pltpu.SemaphoreType.DMA((n,)), pltpu.SemaphoreType.DMA((n,))],
        name=name,
    )(*fs)


def _pair_sum(g, r, core, name):
    nj, _, hr, cols = g.shape
    tr = _row_tile(hr, 256)

    def body(c_ref, g_ref, r_ref, o_ref):
        del c_ref
        o_ref[...] = (g_ref[...].astype(F32) + r_ref[...].astype(F32)).astype(o_ref.dtype)

    return _call(
        body,
        grid_spec=pltpu.PrefetchScalarGridSpec(
            num_scalar_prefetch=1,
            grid=(nj, hr // tr),
            in_specs=[
                pl.BlockSpec((None, None, tr, cols), lambda j, i, c: (j, c[0], i, 0)),
                pl.BlockSpec((None, tr, cols), lambda j, i, c: (j, i, 0)),
            ],
            out_specs=pl.BlockSpec((None, tr, cols), lambda j, i, c: (j, i, 0)),
        ),
        out_shape=jax.ShapeDtypeStruct((nj, hr, cols), g.dtype),
        compiler_params=_params(2),
        name=name,
    )(core, g, r)


def _chip_sum(p, q, pos, name):
    nq, hr, cols = q.shape
    tr = _row_tile(hr, 256)

    def body(pos_ref, p_ref, *refs):
        del pos_ref
        o_ref = refs[nq]
        tot = p_ref[...].astype(F32)
        for k in range(nq):
            tot = tot + refs[k][...].astype(F32)
        o_ref[...] = tot

    return _call(
        body,
        grid_spec=pltpu.PrefetchScalarGridSpec(
            num_scalar_prefetch=1,
            grid=(hr // tr,),
            in_specs=[pl.BlockSpec((None, tr, cols), lambda i, s: (s[0], i, 0))]
            + [pl.BlockSpec((None, tr, cols), functools.partial(lambda k, i, s: (k, i, 0), k)) for k in range(nq)],
            out_specs=pl.BlockSpec((None, tr, cols), lambda i, s: (s[1], i, 0)),
        ),
        out_shape=jax.ShapeDtypeStruct((2, hr, cols), F32),
        compiler_params=_params(1),
        name=name,
    )(pos, p, *([q] * nq))


def _adamw(w, g, m, v, name):
    rows, cols = w.shape
    tr = _row_tile(rows, max(8, (2**18) // cols), mult=8)
    c1 = 1.0 - ADAM_B1**ADAM_STEP
    c2 = 1.0 - ADAM_B2**ADAM_STEP

    def body(w_ref, g_ref, m_ref, v_ref, d_ref, nm_ref, nv_ref):
        gv = g_ref[...]
        nm = ADAM_B1 * m_ref[...] + (1.0 - ADAM_B1) * gv
        nv = ADAM_B2 * v_ref[...] + (1.0 - ADAM_B2) * (gv * gv)
        nm_ref[...] = nm
        nv_ref[...] = nv
        d_ref[...] = -ADAM_LR * ((nm / c1) / (jnp.sqrt(nv / c2) + ADAM_EPS) + ADAM_WD * w_ref[...])

    spec = pl.BlockSpec((tr, cols), lambda i: (i, 0))
    sds = jax.ShapeDtypeStruct((rows, cols), F32)
    return _call(
        body,
        grid=(rows // tr,),
        in_specs=[spec] * 4,
        out_specs=[spec] * 3,
        out_shape=[sds] * 3,
        compiler_params=_params(1),
        name=name,
    )(w, g, m, v)


_TILE_ELEMS = 8 * LANES
_FLAT_ROW_MULT = 8 * 2 * N_CHIPS


def _flat_rows(shape):
    n = math.prod(shape)
    return (n + _TILE_ELEMS - 1) // _TILE_ELEMS * 8


def _pack(arrs):
    parts = []
    for a in arrs:
        rows = _flat_rows(a.shape)
        flat = a.reshape(-1).astype(F32)
        flat = jnp.pad(flat, (0, rows * LANES - flat.shape[0]))
        parts.append(flat.reshape(rows, LANES))
    total = sum(p.shape[0] for p in parts)
    pad = -total % _FLAT_ROW_MULT
    if pad:
        parts.append(jnp.zeros((pad, LANES), F32))
    return jnp.concatenate(parts, axis=0)


def _unpack(flat, shapes):
    out, row = [], 0
    for shp in shapes:
        rows = _flat_rows(shp)
        out.append(flat[row : row + rows].reshape(-1)[: math.prod(shp)].reshape(shp))
        row += rows
    return out


def kernel(x, norm_mix_g, w_in, ssm_a_re, ssm_a_im, ssm_b_re, ssm_b_im, ssm_c_re, ssm_c_im, ssm_d, ssm_log_dt, ssm_glu_w, ssm_glu_b, sgu_ln_g, sgu_ln_b, sgu_w, sgu_b, out_norm_ssm_g, out_norm_sgu_g, w_out, norm_mlp_g, w_up, w_down, norm_final_g, loss_target, m_norm_mix_g, m_w_in, m_ssm_a_re, m_ssm_a_im, m_ssm_b_re, m_ssm_b_im, m_ssm_c_re, m_ssm_c_im, m_ssm_d, m_ssm_log_dt, m_ssm_glu_w, m_ssm_glu_b, m_sgu_ln_g, m_sgu_ln_b, m_sgu_w, m_sgu_b, m_out_norm_ssm_g, m_out_norm_sgu_g, m_w_out, m_norm_mlp_g, m_w_up, m_w_down, m_norm_final_g, v_norm_mix_g, v_w_in, v_ssm_a_re, v_ssm_a_im, v_ssm_b_re, v_ssm_b_im, v_ssm_c_re, v_ssm_c_im, v_ssm_d, v_ssm_log_dt, v_ssm_glu_w, v_ssm_glu_b, v_sgu_ln_g, v_sgu_ln_b, v_sgu_w, v_sgu_b, v_out_norm_ssm_g, v_out_norm_sgu_g, v_w_out, v_norm_mlp_g, v_w_up, v_w_down, v_norm_final_g):
    weights = dict(norm_mix_g=norm_mix_g, w_in=w_in, ssm_a_re=ssm_a_re, ssm_a_im=ssm_a_im, ssm_b_re=ssm_b_re, ssm_b_im=ssm_b_im, ssm_c_re=ssm_c_re, ssm_c_im=ssm_c_im, ssm_d=ssm_d, ssm_log_dt=ssm_log_dt, ssm_glu_w=ssm_glu_w, ssm_glu_b=ssm_glu_b, sgu_ln_g=sgu_ln_g, sgu_ln_b=sgu_ln_b, sgu_w=sgu_w, sgu_b=sgu_b, out_norm_ssm_g=out_norm_ssm_g, out_norm_sgu_g=out_norm_sgu_g, w_out=w_out, norm_mlp_g=norm_mlp_g, w_up=w_up, w_down=w_down, norm_final_g=norm_final_g)
    mom_m = dict(norm_mix_g=m_norm_mix_g, w_in=m_w_in, ssm_a_re=m_ssm_a_re, ssm_a_im=m_ssm_a_im, ssm_b_re=m_ssm_b_re, ssm_b_im=m_ssm_b_im, ssm_c_re=m_ssm_c_re, ssm_c_im=m_ssm_c_im, ssm_d=m_ssm_d, ssm_log_dt=m_ssm_log_dt, ssm_glu_w=m_ssm_glu_w, ssm_glu_b=m_ssm_glu_b, sgu_ln_g=m_sgu_ln_g, sgu_ln_b=m_sgu_ln_b, sgu_w=m_sgu_w, sgu_b=m_sgu_b, out_norm_ssm_g=m_out_norm_ssm_g, out_norm_sgu_g=m_out_norm_sgu_g, w_out=m_w_out, norm_mlp_g=m_norm_mlp_g, w_up=m_w_up, w_down=m_w_down, norm_final_g=m_norm_final_g)
    mom_v = dict(norm_mix_g=v_norm_mix_g, w_in=v_w_in, ssm_a_re=v_ssm_a_re, ssm_a_im=v_ssm_a_im, ssm_b_re=v_ssm_b_re, ssm_b_im=v_ssm_b_im, ssm_c_re=v_ssm_c_re, ssm_c_im=v_ssm_c_im, ssm_d=v_ssm_d, ssm_log_dt=v_ssm_log_dt, ssm_glu_w=v_ssm_glu_w, ssm_glu_b=v_ssm_glu_b, sgu_ln_g=v_sgu_ln_g, sgu_ln_b=v_sgu_ln_b, sgu_w=v_sgu_w, sgu_b=v_sgu_b, out_norm_ssm_g=v_out_norm_ssm_g, out_norm_sgu_g=v_out_norm_sgu_g, w_out=v_w_out, norm_mlp_g=v_norm_mlp_g, w_up=v_w_up, w_down=v_w_down, norm_final_g=v_norm_final_g)
    names = list(weights)
    large = ["w_in", "ssm_glu_w", "w_out", "w_up", "w_down"]
    small = [n for n in names if n not in large]

    s, d = x.shape[1], x.shape[2]
    xs = x.reshape(s, d)
    target = loss_target.reshape(s, d)
    width = ssm_glu_w.shape[-1]
    ff = w_down.shape[1] * N_CHIPS
    tm = min(512, s)
    core = lax.axis_index("c").astype(jnp.int32).reshape(1)
    chip = (2 * lax.axis_index("x") + lax.axis_index("y")).astype(jnp.int32).reshape(1)
    pos = jnp.concatenate([chip, core])

    shards = [w[0].astype(BF16).reshape(2, w.shape[1] // 2, w.shape[2]) for w in (w_in, ssm_glu_w, w_out, w_up, w_down)]
    ssm_args = (ssm_a_re[0], ssm_a_im[0], ssm_b_re[0], ssm_b_im[0], ssm_c_re[0], ssm_c_im[0], ssm_d[0], ssm_log_dt[0])
    (k2c, w2c, v2c, al), ssm_vjp = jax.vjp(_ssm_mats, *ssm_args)
    consts = _spread_consts(ssm_b_re.shape[-1], ssm_a_re.shape[-1])
    (k2b, w2b, v2b), (w_in_g,) = _ssm_spread(k2c, w2c, v2c, consts, [_gather_one(shards[0])])
    w_in4 = _place_own(shards[0], w_in_g, pos, "place_w_in").reshape(N_CHIPS, d, w_in.shape[2])
    causal = jnp.tril(jnp.ones((SGU_CHUNK, SGU_CHUNK), dtype=bool))
    wt = jnp.where(causal[None], sgu_w[0], 0.0)
    wtb = wt.astype(BF16)
    wttb = jnp.swapaxes(wt, 1, 2).astype(BF16)
    heads = sgu_w.shape[1]
    biasb = jnp.broadcast_to(sgu_b[0][:, :, None], (heads, SGU_CHUNK, LANES))

    tm_big = min(1024, s)
    (z, h1b), (glu_g, up_g) = _in_proj(
        xs, norm_mix_g, w_in4, tm_big, [_gather_one(shards[1]), _gather_one(shards[3])])
    wg_full = _place_own(shards[1], glu_g, pos, "place_glu_w").reshape(width, width)
    up4 = _place_own(shards[3], up_g, pos, "place_w_up").reshape(N_CHIPS, d, w_up.shape[2])
    (y_pre, xprev, us), (out_g,) = _ssm_fwd(z, k2b, w2b, v2b, al, [_gather_one(shards[2])])
    w_out_full = _place_own(shards[2], out_g, pos, "place_w_out").reshape(d, d)
    mixed = _glu_fwd(y_pre, wg_full, ssm_glu_b, out_norm_ssm_g, d, tm)
    mixed = _sgu_fwd(z, mixed, sgu_ln_g, sgu_ln_b, wtb, biasb, out_norm_sgu_g, tm)
    x1 = _out_proj(xs, mixed, w_out_full, tm)
    tf = min(1024, up4.shape[-1])
    (up, h2b), (down_g,) = _mlp_up(x1, norm_mlp_g, up4, tm_big, tf, [_gather_one(shards[4])])
    w_down_full = _place_own(shards[4], down_g, pos, "place_w_down").reshape(ff, d)
    loss_part, dx2, dx2b, d_norm_final = _mlp_down_loss(
        x1, up, w_down_full, target, norm_final_g.reshape(1, d), tm, tf)

    dup, dh2 = _mlp_bwd(dx2b, up, up4, w_down_full, tm_big, min(512, up4.shape[-1]))
    dx1, dx1b, d_norm_mlp = _norm_bwd(dh2, x1, dx2, norm_mlp_g, min(256, s), "norm_mlp_bwd")
    hr_big = d // 2
    tkk = min(1024, s)
    g_down = _grad_w(
        up, dx2b, (N_CHIPS, 2, hr_big, d),
        lambda i, j: (i // (2 * (hr_big // min(1024, hr_big))), (i // (hr_big // min(1024, hr_big))) % 2,
                      i % (hr_big // min(1024, hr_big)), j),
        min(1024, hr_big), d, tkk, True, "grad_w_down")
    tn_up = min(2048, up4.shape[-1])
    per_up = up4.shape[-1] // tn_up
    g_up, sib_down = _grad_w(
        h2b, dup, (N_CHIPS, 2, hr_big, up4.shape[-1]),
        lambda i, j: (j // per_up, i // (hr_big // min(1024, hr_big)), i % (hr_big // min(1024, hr_big)), j % per_up),
        min(1024, hr_big), tn_up, tkk, False, "grad_w_up", exs=[_to_sibling_one(g_down)])
    pair_down = _pair_sum(g_down, sib_down, core, "pair_sum_w_down")
    dmix = _out_proj_bwd(dx1b, w_out_full, tm)
    hr_out = d // (2 * N_CHIPS)
    g_out, sib_up = _grad_w(
        mixed, dx1b, (1, 1, d, d), lambda i, j: (0, 0, i, j), min(1024, d), d, tkk, False, "grad_w_out",
        exs=[_to_sibling_one(g_up)])
    g_out = g_out.reshape(N_CHIPS, 2, hr_out, d)
    pair_up = _pair_sum(g_up, sib_up, core, "pair_sum_w_up")
    (dy_pre, g_glu, d_glu_b, d_norm_ssm), (sib_out,) = _glu_bwd(
        y_pre, dmix, wg_full, ssm_glu_b, out_norm_ssm_g, tm, [_to_sibling_one(g_out)])
    pair_out = _pair_sum(g_out, sib_out, core, "pair_sum_w_out")
    dys, gst, d_al = _ssm_bwd_state(dy_pre, v2b, al, xprev)
    (dz, d_k2, d_w2, d_v2), (chips_down,) = _ssm_bwd_main(
        us, dys, k2b, w2b, xprev, gst, consts, s, z.shape[1], [_to_owner_one(pair_down)])
    (dz, d_wt, d_bias, d_ln_g, d_ln_b, d_norm_sgu), (chips_up,) = _sgu_bwd(
        z, dmix, dz, sgu_ln_g, sgu_ln_b, wtb, wttb, biasb, out_norm_sgu_g, tm, [_to_owner_one(pair_up)])
    cw_in = w_in4.shape[-1]
    g_in, chips_out = _grad_w(
        h1b, dz, (N_CHIPS, 2, hr_big, cw_in),
        lambda i, j: (j, i // (hr_big // min(1024, hr_big)), i % (hr_big // min(1024, hr_big)), 0),
        min(1024, hr_big), 2 * cw_in, tkk, False, "grad_w_in", exs=[_to_owner_one(pair_out)], chunks=2)
    (sib_in,) = _pair_exchange([g_in], "w_in_to_sibling")
    pair_in = _pair_sum(g_in, sib_in, core, "pair_sum_w_in")
    (grad_x, d_norm_mix), (chips_in,) = _in_proj_bwd(dz, w_in4, xs, dx1, norm_mix_g, tm, [_to_owner_one(pair_in)])

    d_ssm = ssm_vjp((d_k2, d_w2, d_v2, d_al))
    small_grads = dict(
        norm_mix_g=d_norm_mix, ssm_a_re=d_ssm[0], ssm_a_im=d_ssm[1], ssm_b_re=d_ssm[2], ssm_b_im=d_ssm[3],
        ssm_c_re=d_ssm[4], ssm_c_im=d_ssm[5], ssm_d=d_ssm[6], ssm_log_dt=d_ssm[7], ssm_glu_b=d_glu_b,
        sgu_ln_g=d_ln_g, sgu_ln_b=d_ln_b, sgu_w=jnp.where(causal[None], d_wt, 0.0), sgu_b=d_bias[:, :, 0],
        out_norm_ssm_g=d_norm_ssm, out_norm_sgu_g=d_norm_sgu, norm_mlp_g=d_norm_mlp, norm_final_g=d_norm_final)
    flat = _pack([small_grads[n] for n in small])
    hr_small = flat.shape[0] // (2 * N_CHIPS)
    g_small = flat.reshape(N_CHIPS, 2, hr_small, LANES)

    hr_glu = width // (2 * N_CHIPS)
    grads = [g_glu.reshape(N_CHIPS, 2, hr_glu, width), g_small]
    tags = ["glu_w", "small"]
    from_sib = _pair_exchange(grads, "grads_to_sibling")
    pair = [_pair_sum(g, r, core, "pair_sum_" + t) for g, r, t in zip(grads, from_sib, tags)]
    from_chips = _chip_exchange(pair, "grads_to_owner")
    pair = [pair_in, pair[0], pair_out, pair_up, pair_down, pair[1]]
    from_chips = [chips_in, from_chips[0], chips_out, chips_up, chips_down, from_chips[1]]
    tags = ["w_in", "glu_w", "w_out", "w_up", "w_down", "small"]
    halves = [_chip_sum(p, q, pos, "chip_sum_" + t) for p, q, t in zip(pair, from_chips, tags)]
    owned = _pair_share(halves, "grads_to_both_cores")
    (small_all,) = _gather_chips([owned[5]], "gather_small_grads")
    small_flat = small_all.reshape(flat.shape)

    grad_out, delta_out, m_out, v_out = {}, {}, {}, {}
    for n, g in zip(large, owned[:5]):
        shp = weights[n].shape
        g2 = g.reshape(shp[1], shp[2])
        dl, nm, nv = _adamw(weights[n][0], g2, mom_m[n][0], mom_v[n][0], "adamw_" + n)
        grad_out[n], delta_out[n], m_out[n], v_out[n] = g2.reshape(shp), dl.reshape(shp), nm.reshape(shp), nv.reshape(shp)
    shapes = [weights[n].shape for n in small]
    dl, nm, nv = _adamw(
        _pack([weights[n] for n in small]), small_flat, _pack([mom_m[n] for n in small]),
        _pack([mom_v[n] for n in small]), "adamw_small")
    for n, g, a, b, c in zip(small, _unpack(small_flat, shapes), _unpack(dl, shapes), _unpack(nm, shapes), _unpack(nv, shapes)):
        grad_out[n], delta_out[n], m_out[n], v_out[n] = g, a, b, c

    loss = lax.psum(loss_part[0, 0], ("x", "y", "c"))
    return (loss, grad_x.reshape(x.shape), *[grad_out[n] for n in names], *[delta_out[n] for n in names],
            *[m_out[n] for n in names], *[v_out[n] for n in names])
```

```python
import functools
import math

import numpy as np
import jax
import jax.numpy as jnp
from jax import lax
from jax.experimental import pallas as pl
from jax.experimental.pallas import tpu as pltpu

F32 = jnp.float32
BF16 = jnp.bfloat16
MESH = pl.DeviceIdType.MESH
HIGHEST = lax.Precision.HIGHEST

EPS = 1e-6
ADAM_LR = 0.001
ADAM_B1 = 0.9
ADAM_B2 = 0.999
ADAM_EPS = 1e-08
ADAM_WD = 0.01
ADAM_STEP = 10

N_CHIPS = 4
LANES = 128
SSM_GROUP = 16
SSM_STATE = 64
GROUPS_PER_TILE = LANES // SSM_GROUP
CHUNK = 16
PAIRS = CHUNK // 2
SGU_CHUNK = 128
VMEM_LIMIT = 56 * 2**20


def _params(n_axes, vmem=VMEM_LIMIT):
    return pltpu.CompilerParams(dimension_semantics=("arbitrary",) * n_axes, vmem_limit_bytes=vmem)


def _call(body, **kw):
    return pl.pallas_call(body, **kw)


def _dot(a, b):
    return jnp.dot(a, b, preferred_element_type=F32)


def _dot_nt(a, b):
    return lax.dot_general(a, b, (((1,), (1,)), ((), ())), preferred_element_type=F32)


def _dot_tn(a, b):
    return lax.dot_general(a, b, (((0,), (0,)), ((), ())), preferred_element_type=F32)


_GELU_K = math.sqrt(2.0 / math.pi)
_GELU_C = 0.044715


def _gelu_both(x):
    x2 = x * x
    t = jnp.tanh(x * (_GELU_K + (_GELU_K * _GELU_C) * x2))
    hx = 0.5 * x
    onep = 1.0 + t
    return hx * onep, 0.5 * onep + hx * (1.0 - t * t) * (_GELU_K + (3.0 * _GELU_K * _GELU_C) * x2)


def _gelu(x):
    return _gelu_both(x)[0]


def _sigmoid(x):
    return 1.0 / (1.0 + jnp.exp(-x))


def _rms(x):
    return lax.rsqrt(jnp.mean(x * x, axis=-1, keepdims=True) + EPS)


def _rms_bwd(dy, x, r, g):
    a = dy * g
    dx = r * a - x * (r * r * r) * jnp.mean(a * x, axis=-1, keepdims=True)
    return dx, dy * x * r


def _row_tile(rows, target, mult=16):
    for t in range(min(rows, target), 0, -1):
        if rows % t == 0 and t % mult == 0:
            return t
    return rows


def _ssm_mats(a_re, a_im, b_re, b_im, c_re, c_im, d, log_dt):
    g, p = a_re.shape
    h = b_re.shape[-1]
    nt = g // GROUPS_PER_TILE
    dt = jnp.exp(log_dt)[:, None]
    lr, li = a_re * dt, a_im * dt

    def apow(l, lr, li):
        mag = jnp.exp(lr * l)
        return mag * jnp.cos(li * l), mag * jnp.sin(li * l)

    ar, ai = apow(1.0, lr, li)
    den = a_re * a_re + a_im * a_im
    qr = ((ar - 1.0) * a_re + ai * a_im) / den
    qi = (ai * a_re - (ar - 1.0) * a_im) / den
    bt_re, bt_im = jnp.swapaxes(b_re, 1, 2), jnp.swapaxes(b_im, 1, 2)
    bbr = qr[:, None, :] * bt_re - qi[:, None, :] * bt_im
    bbi = qr[:, None, :] * bt_im + qi[:, None, :] * bt_re
    b_same = jnp.concatenate([bbr, bbi], axis=-1)[:, None]
    b_swap = jnp.concatenate([bbi, bbr], axis=-1)[:, None]
    ls = jnp.arange(CHUNK + 1, dtype=F32)[None, :, None]
    pr, pi = apow(ls, lr[:, None, :], li[:, None, :])
    p_same = jnp.concatenate([pr, pr], axis=-1)[:, :, None, :]
    p_sign = jnp.concatenate([-pi, pi], axis=-1)[:, :, None, :]
    tj = p_same[:, :CHUNK] * b_same + p_sign[:, :CHUNK] * b_swap
    c_conj = jnp.concatenate([c_re, -c_im], axis=-1)
    k = jnp.einsum("goq,gliq->glio", c_conj, tj, precision=HIGHEST)
    k = k.at[:, 0].add(d[:, :, None] * jnp.eye(h, dtype=F32))

    zero = jnp.zeros_like(k[:, :1])
    kz = jnp.concatenate([zero, zero, k], axis=1).reshape(g, PAIRS + 1, 2, h, h)
    even, odd = kz[:, :, 0], kz[:, :, 1]
    fill = jnp.zeros((g, PAIRS, h, LANES - 2 * h), F32)
    row0 = jnp.concatenate([even[:, 1:], odd[:, 1:], fill], axis=-1)
    row1 = jnp.concatenate([odd[:, :-1], even[:, 1:], fill], axis=-1)
    k2c = jnp.stack([row0, row1], axis=2)
    w2c = tj[:, ::-1].reshape(g, PAIRS, 2, h, 2 * p)
    c_cross = jnp.concatenate([-c_im, -c_re], axis=-1)
    p_imag = jnp.concatenate([pi, pi], axis=-1)[:, :, None, :]
    v2c = (c_conj[:, None] * p_same[:, 1:] + c_cross[:, None] * p_imag[:, 1:]).reshape(g, PAIRS, 2, h, 2 * p)
    al = jnp.stack([pr[:, CHUNK].reshape(nt, -1), pi[:, CHUNK].reshape(nt, -1)], axis=1)
    return k2c, w2c, v2c, al


def _spread_consts(h, p):
    gg = GROUPS_PER_TILE
    row_g = (np.arange(2 * gg * h) // h) % gg
    colk = np.arange(2 * gg * h)
    rep_k = np.zeros((LANES, 2 * gg * h), np.float32)
    rep_k[(colk // (gg * h)) * h + colk % h, colk] = 1.0
    mask_k = (row_g[:, None] == ((colk // h) % gg)[None, :]).astype(np.float32)
    cols = np.arange(2 * gg * p)
    rep_s = np.zeros((2 * p, 2 * gg * p), np.float32)
    rep_s[(cols // (gg * p)) * p + cols % p, cols] = 1.0
    mask_s = (row_g[:, None] == ((cols // p) % gg)[None, :]).astype(np.float32)
    return tuple(jnp.asarray(a, BF16) for a in (rep_k, mask_k, rep_s, mask_s))


def _ssm_spread(k2c, w2c, v2c, consts, exs):
    h = k2c.shape[3]
    nt = k2c.shape[0] // GROUPS_PER_TILE
    rep_k, mask_k, rep_s, mask_s = consts
    nk, ns = rep_k.shape[1], rep_s.shape[1]

    def body(k_ref, w_ref, v_ref, rk_ref, mk_ref, rs_ref, ms_ref, ko_ref, wo_ref, vo_ref):
        for q in range(PAIRS):
            for c_ref, o_ref, r_ref, m_ref in ((k_ref, ko_ref, rk_ref, mk_ref), (w_ref, wo_ref, rs_ref, ms_ref),
                                               (v_ref, vo_ref, rs_ref, ms_ref)):
                rows = jnp.concatenate(
                    [c_ref[gi, q, st] for st in range(2) for gi in range(GROUPS_PER_TILE)], axis=0)
                o_ref[q] = (_dot(rows.astype(BF16), r_ref[...]) * m_ref[...]).astype(BF16)

    compact = pl.BlockSpec((GROUPS_PER_TILE, PAIRS, 2, h, LANES), lambda t: (t, 0, 0, 0, 0))

    def tile(cols):
        return pl.BlockSpec((None, PAIRS, nk, cols), lambda t: (t, 0, 0, 0))

    def whole(a):
        return pl.BlockSpec(a.shape, lambda t: (0, 0))

    return _call_riding(
        body,
        exs,
        args=(k2c, w2c, v2c, rep_k, mask_k, rep_s, mask_s),
        scratch_shapes=[],
        grid=(nt,),
        in_specs=[compact, compact, compact, whole(rep_k), whole(mask_k), whole(rep_s), whole(mask_s)],
        out_specs=[tile(nk), tile(ns), tile(ns)],
        out_shape=[
            jax.ShapeDtypeStruct((nt, PAIRS, nk, nk), BF16),
            jax.ShapeDtypeStruct((nt, PAIRS, nk, ns), BF16),
            jax.ShapeDtypeStruct((nt, PAIRS, nk, ns), BF16),
        ],
        compiler_params=_params(1),
        name="ssm_spread",
    )


def _gather_blocks(full, mask, rep):
    return _dot_nt((full * mask).astype(BF16), rep)


def _in_proj(x, g, w4, tm, exs):
    s, d = x.shape
    nj, _, cw = w4.shape

    def body(x_ref, g_ref, w_ref, z_ref, h_ref):
        @pl.when(pl.program_id(1) == 0)
        def _():
            xv = x_ref[...]
            h_ref[...] = (xv * _rms(xv) * g_ref[...]).astype(BF16)

        z_ref[...] = _dot(h_ref[...], w_ref[...])

    return _call_riding(
        body,
        exs,
        scratch_shapes=[],
        args=(x, g, w4),
        grid=(s // tm, nj),
        in_specs=[
            pl.BlockSpec((tm, d), lambda i, j: (i, 0)),
            pl.BlockSpec((1, d), lambda i, j: (0, 0)),
            pl.BlockSpec((None, d, cw), lambda i, j: (j, 0, 0)),
        ],
        out_specs=[pl.BlockSpec((tm, cw), lambda i, j: (i, j)), pl.BlockSpec((tm, d), lambda i, j: (i, 0))],
        out_shape=[jax.ShapeDtypeStruct((s, nj * cw), F32), jax.ShapeDtypeStruct((s, d), BF16)],
        compiler_params=_params(2),
        name="in_proj",
    )


def _ssm_fwd(z, k2, w2, v2, al, exs):
    s = z.shape[0]
    nt = k2.shape[0]
    nc = s // CHUNK
    ns = w2.shape[-1]
    hs = ns // 2

    def body(u_ref, k_ref, w_ref, v_ref, al_ref, y_ref, xp_ref, us_ref, xloc):
        for q in range(PAIRS):
            us_ref[q, :, 0:LANES] = u_ref[pl.ds(2 * q, nc, stride=CHUNK), :].astype(BF16)
            us_ref[q, :, LANES : 2 * LANES] = u_ref[pl.ds(2 * q + 1, nc, stride=CHUNK), :].astype(BF16)
        acc = _dot(us_ref[0], w_ref[0])
        for q in range(1, PAIRS):
            acc = acc + _dot(us_ref[q], w_ref[q])
        xloc[...] = acc
        ar = al_ref[0:1, :]
        ai = al_ref[1:2, :]

        def step(c, carry):
            xr, xi = carry
            xp_ref[pl.ds(c, 1), 0:hs] = xr
            xp_ref[pl.ds(c, 1), hs:ns] = xi
            lr = xloc[pl.ds(c, 1), 0:hs]
            li = xloc[pl.ds(c, 1), hs:ns]
            return ar * xr - ai * xi + lr, ar * xi + ai * xr + li

        zero = jnp.zeros((1, hs), F32)
        lax.fori_loop(0, nc, step, (zero, zero))
        xpb = xp_ref[...].astype(BF16)
        for r in range(PAIRS):
            acc = _dot_nt(xpb, v_ref[r])
            for q in range(r + 1):
                acc = acc + _dot(us_ref[q], k_ref[r - q])
            y_ref[pl.ds(2 * r, nc, stride=CHUNK), :] = acc[:, 0:LANES]
            y_ref[pl.ds(2 * r + 1, nc, stride=CHUNK), :] = acc[:, LANES : 2 * LANES]

    return _call_riding(
        body,
        exs,
        args=(z, k2, w2, v2, al),
        grid=(nt,),
        in_specs=[
            pl.BlockSpec((s, LANES), lambda t: (0, t)),
            pl.BlockSpec((None, PAIRS, 2 * LANES, 2 * LANES), lambda t: (t, 0, 0, 0)),
            pl.BlockSpec((None, PAIRS, 2 * LANES, ns), lambda t: (t, 0, 0, 0)),
            pl.BlockSpec((None, PAIRS, 2 * LANES, ns), lambda t: (t, 0, 0, 0)),
            pl.BlockSpec((None, 2, hs), lambda t: (t, 0, 0)),
        ],
        out_specs=[
            pl.BlockSpec((s, LANES), lambda t: (0, t)),
            pl.BlockSpec((None, nc, ns), lambda t: (t, 0, 0)),
            pl.BlockSpec((None, PAIRS, nc, 2 * LANES), lambda t: (t, 0, 0, 0)),
        ],
        out_shape=[
            jax.ShapeDtypeStruct((s, nt * LANES), F32),
            jax.ShapeDtypeStruct((nt, nc, ns), F32),
            jax.ShapeDtypeStruct((nt, PAIRS, nc, 2 * LANES), BF16),
        ],
        scratch_shapes=[pltpu.VMEM((nc, ns), F32)],
        compiler_params=_params(1),
        name="ssm_fwd",
    )


def _glu_fwd(y_pre, wg, bg, go, d_model, tm):
    s, w = y_pre.shape

    def body(y_ref, wg_ref, bg_ref, go_ref, o_ref):
        yg = _gelu(y_ref[...])
        gate = _sigmoid(_dot(yg.astype(BF16), wg_ref[...]) + bg_ref[...])
        ys = yg * gate
        o_ref[...] = (ys * _rms(ys) * go_ref[...]).astype(BF16)

    return _call(
        body,
        grid=(s // tm,),
        in_specs=[
            pl.BlockSpec((tm, w), lambda i: (i, 0)),
            pl.BlockSpec((w, w), lambda i: (0, 0)),
            pl.BlockSpec((1, w), lambda i: (0, 0)),
            pl.BlockSpec((1, w), lambda i: (0, 0)),
        ],
        out_specs=pl.BlockSpec((tm, w), lambda i: (i, 0)),
        out_shape=jax.ShapeDtypeStruct((s, d_model), BF16),
        compiler_params=_params(1),
        name="glu_fwd",
    )(y_pre, wg, bg, go)


def _sgu_parts(zu, zv, lng, lnb, wt_ref, bias_ref):
    u, u_grad = _gelu_both(zu)
    v, v_grad = _gelu_both(zv)
    mu = jnp.mean(v, axis=-1, keepdims=True)
    vc = v - mu
    rstd = lax.rsqrt(jnp.mean(vc * vc, axis=-1, keepdims=True) + EPS)
    vhat = vc * rstd
    vb = (vhat * lng + lnb).astype(BF16)
    heads = wt_ref.shape[0]
    mix = jnp.concatenate(
        [_dot(wt_ref[h], vb[:, h * LANES : (h + 1) * LANES]) + bias_ref[h] for h in range(heads)], axis=1
    )
    return u, vhat, rstd, vb, mix, u_grad, v_grad


def _sgu_fwd(z, mixed, lng, lnb, wt, biasb, go, rb):
    s = z.shape[0]
    w = lng.shape[-1]
    heads = wt.shape[0]

    def body(zu_ref, zv_ref, m_any, lng_ref, lnb_ref, wt_ref, b_ref, go_ref, o_ref):
        del m_any
        for ck in range(rb // SGU_CHUNK):
            rows = slice(ck * SGU_CHUNK, (ck + 1) * SGU_CHUNK)
            u, _, _, _, mix, _, _ = _sgu_parts(zu_ref[rows, :], zv_ref[rows, :], lng_ref[...], lnb_ref[...], wt_ref, b_ref)
            y = u * mix
            o_ref[rows, :] = (y * _rms(y) * go_ref[...]).astype(BF16)

    return _call(
        body,
        grid=(s // rb,),
        in_specs=[
            pl.BlockSpec((rb, w), lambda i: (i, 1)),
            pl.BlockSpec((rb, w), lambda i: (i, 2)),
            pl.BlockSpec(memory_space=pl.ANY),
            pl.BlockSpec((1, w), lambda i: (0, 0)),
            pl.BlockSpec((1, w), lambda i: (0, 0)),
            pl.BlockSpec((heads, SGU_CHUNK, SGU_CHUNK), lambda i: (0, 0, 0)),
            pl.BlockSpec((heads, SGU_CHUNK, LANES), lambda i: (0, 0, 0)),
            pl.BlockSpec((1, w), lambda i: (0, 0)),
        ],
        out_specs=pl.BlockSpec((rb, w), lambda i: (i, 1)),
        out_shape=jax.ShapeDtypeStruct(mixed.shape, mixed.dtype),
        input_output_aliases={2: 0},
        compiler_params=_params(1),
        name="sgu_fwd",
    )(z, z, mixed, lng, lnb, wt, biasb, go)


def _out_proj(x, mixed, w_out, tm):
    s, d = x.shape

    def body(x_ref, m_ref, w_ref, o_ref):
        o_ref[...] = x_ref[...] + _dot(m_ref[...], w_ref[...])

    return _call(
        body,
        grid=(s // tm,),
        in_specs=[
            pl.BlockSpec((tm, d), lambda i: (i, 0)),
            pl.BlockSpec((tm, d), lambda i: (i, 0)),
            pl.BlockSpec((d, d), lambda i: (0, 0)),
        ],
        out_specs=pl.BlockSpec((tm, d), lambda i: (i, 0)),
        out_shape=jax.ShapeDtypeStruct((s, d), F32),
        compiler_params=_params(1),
        name="out_proj",
    )(x, mixed, w_out)


def _mlp_up(x1, g, w_up4, tm, tf, exs):
    s, d = x1.shape
    nj, _, cw = w_up4.shape
    per = cw // tf

    def body(x_ref, g_ref, wu_ref, up_ref, h_ref):
        @pl.when(pl.program_id(1) == 0)
        def _():
            xv = x_ref[...]
            h_ref[...] = (xv * _rms(xv) * g_ref[...]).astype(BF16)

        up_ref[...] = _dot(h_ref[...], wu_ref[...]).astype(BF16)

    return _call_riding(
        body,
        exs,
        scratch_shapes=[],
        args=(x1, g, w_up4),
        grid=(s // tm, nj * per),
        in_specs=[
            pl.BlockSpec((tm, d), lambda i, f: (i, 0)),
            pl.BlockSpec((1, d), lambda i, f: (0, 0)),
            pl.BlockSpec((None, d, tf), lambda i, f: (f // per, 0, f % per)),
        ],
        out_specs=[pl.BlockSpec((tm, tf), lambda i, f: (i, f)), pl.BlockSpec((tm, d), lambda i, f: (i, 0))],
        out_shape=[jax.ShapeDtypeStruct((s, nj * cw), BF16), jax.ShapeDtypeStruct((s, d), BF16)],
        compiler_params=_params(2),
        name="mlp_up",
    )


def _mlp_down_loss(x1, up, w_down, target, g, tm, tf):
    s, d = x1.shape
    ff = w_down.shape[0]
    nf = ff // tf

    def body(x_ref, up_ref, wd_ref, t_ref, g_ref, loss_ref, dx_ref, dxb_ref, dg_ref, acc):
        i = pl.program_id(0)
        f = pl.program_id(1)

        @pl.when(f == 0)
        def _():
            acc[...] = x_ref[...]

        @pl.when(jnp.logical_and(i == 0, f == 0))
        def _():
            loss_ref[...] = jnp.zeros_like(loss_ref)
            dg_ref[...] = jnp.zeros_like(dg_ref)

        a = jnp.maximum(up_ref[...].astype(F32), 0.0)
        acc[...] += _dot((a * a).astype(BF16), wd_ref[...])

        @pl.when(f == nf - 1)
        def _():
            xv = acc[...]
            r = _rms(xv)
            gv = g_ref[...]
            diff = xv * r * gv - t_ref[...]
            loss_ref[...] += 0.5 * jnp.sum(jnp.mean(diff * diff, axis=-1, keepdims=True), axis=0, keepdims=True)
            dx, dgt = _rms_bwd(diff * (1.0 / d), xv, r, gv)
            dx_ref[...] = dx
            dxb_ref[...] = dx.astype(BF16)
            dg_ref[...] += jnp.sum(dgt, axis=0, keepdims=True)

    row = pl.BlockSpec((tm, d), lambda i, f: (i, 0))
    vec = pl.BlockSpec((1, d), lambda i, f: (0, 0))
    return _call(
        body,
        grid=(s // tm, nf),
        in_specs=[
            row,
            pl.BlockSpec((tm, tf), lambda i, f: (i, f)),
            pl.BlockSpec((tf, d), lambda i, f: (f, 0)),
            row,
            vec,
        ],
        out_specs=[pl.BlockSpec((1, 1), lambda i, f: (0, 0)), row, row, vec],
        out_shape=[
            jax.ShapeDtypeStruct((1, 1), F32),
            jax.ShapeDtypeStruct((s, d), F32),
            jax.ShapeDtypeStruct((s, d), BF16),
            jax.ShapeDtypeStruct((1, d), F32),
        ],
        scratch_shapes=[pltpu.VMEM((tm, d), F32)],
        compiler_params=_params(2),
        name="mlp_down_loss",
    )(x1, up, w_down, target, g)


def _mlp_bwd(dx2b, up, w_up4, w_down, tm, tf):
    s, d = dx2b.shape
    ff = w_down.shape[0]
    cw = w_up4.shape[-1]
    per = cw // tf

    def body(dx_ref, up_ref, wu_ref, wd_ref, dup_ref, dh_ref):
        @pl.when(pl.program_id(1) == 0)
        def _():
            dh_ref[...] = jnp.zeros_like(dh_ref)

        dact = _dot_nt(dx_ref[...], wd_ref[...])
        dupb = (dact * (2.0 * jnp.maximum(up_ref[...].astype(F32), 0.0))).astype(BF16)
        dup_ref[...] = dupb
        dh_ref[...] += _dot_nt(dupb, wu_ref[...])

    return _call(
        body,
        grid=(s // tm, ff // tf),
        in_specs=[
            pl.BlockSpec((tm, d), lambda i, f: (i, 0)),
            pl.BlockSpec((tm, tf), lambda i, f: (i, f)),
            pl.BlockSpec((None, d, tf), lambda i, f: (f // per, 0, f % per)),
            pl.BlockSpec((tf, d), lambda i, f: (f, 0)),
        ],
        out_specs=[pl.BlockSpec((tm, tf), lambda i, f: (i, f)), pl.BlockSpec((tm, d), lambda i, f: (i, 0))],
        out_shape=[jax.ShapeDtypeStruct((s, ff), BF16), jax.ShapeDtypeStruct((s, d), F32)],
        compiler_params=_params(2),
        name="mlp_bwd",
    )(dx2b, up, w_up4, w_down)


def _norm_bwd(dh, x, dres, g, tm, name):
    s, d = x.shape

    def body(dh_ref, x_ref, dr_ref, g_ref, dx_ref, dxb_ref, dg_ref):
        @pl.when(pl.program_id(0) == 0)
        def _():
            dg_ref[...] = jnp.zeros_like(dg_ref)

        xv = x_ref[...]
        dx, dgt = _rms_bwd(dh_ref[...], xv, _rms(xv), g_ref[...])
        tot = dr_ref[...] + dx
        dx_ref[...] = tot
        dxb_ref[...] = tot.astype(BF16)
        dg_ref[...] += jnp.sum(dgt, axis=0, keepdims=True)

    row = pl.BlockSpec((tm, d), lambda i: (i, 0))
    vec = pl.BlockSpec((1, d), lambda i: (0, 0))
    return _call(
        body,
        grid=(s // tm,),
        in_specs=[row, row, row, vec],
        out_specs=[row, row, vec],
        out_shape=[
            jax.ShapeDtypeStruct((s, d), F32),
            jax.ShapeDtypeStruct((s, d), BF16),
            jax.ShapeDtypeStruct((1, d), F32),
        ],
        compiler_params=_params(1),
        name=name,
    )(dh, x, dres, g)


def _grad_w(a, b, out_dims, index_map, tm, tn, tk, relu2, name, exs=(), chunks=1):
    t, m = a.shape
    n = b.shape[1]
    nk = t // tk
    cw = tn // chunks

    def body(a_ref, b_ref, o_ref, acc):
        k = pl.program_id(2)

        @pl.when(k == 0)
        def _():
            acc[...] = jnp.zeros_like(acc)

        av = a_ref[...]
        if relu2:
            r = jnp.maximum(av.astype(F32), 0.0)
            av = (r * r).astype(BF16)
        acc[...] += _dot_tn(av, b_ref[...])

        @pl.when(k == nk - 1)
        def _():
            for c in range(chunks):
                o_ref[c] = acc[:, c * cw : (c + 1) * cw].astype(BF16)

    spec = dict(
        grid=(m // tm, n // tn, nk),
        in_specs=[pl.BlockSpec((tk, tm), lambda i, j, k: (k, i)), pl.BlockSpec((tk, tn), lambda i, j, k: (k, j))],
        out_specs=[pl.BlockSpec((chunks, None, tm, cw), lambda i, j, k: index_map(i, j))],
        out_shape=[jax.ShapeDtypeStruct(out_dims, BF16)],
        scratch_shapes=[pltpu.VMEM((tm, tn), F32)],
        compiler_params=_params(3),
        name=name,
    )
    (g,), landed = _call_riding(body, exs, args=(a, b), **spec)
    return (g, *landed) if exs else g


def _out_proj_bwd(dx1b, w_out, tm):
    s, d = dx1b.shape

    def body(dx_ref, w_ref, o_ref):
        o_ref[...] = _dot_nt(dx_ref[...], w_ref[...])

    return _call(
        body,
        grid=(s // tm,),
        in_specs=[pl.BlockSpec((tm, d), lambda i: (i, 0)), pl.BlockSpec((d, d), lambda i: (0, 0))],
        out_specs=pl.BlockSpec((tm, d), lambda i: (i, 0)),
        out_shape=jax.ShapeDtypeStruct((s, d), F32),
        compiler_params=_params(1),
        name="out_proj_bwd",
    )(dx1b, w_out)


def _glu_bwd(y_pre, dmix, wg, bg, go, tm, exs):
    s, w = y_pre.shape
    n = s // tm

    def body(y_ref, dm_ref, wg_ref, bg_ref, go_ref, dy_ref, dwg_ref, dbg_ref, dgo_ref, acc):
        i = pl.program_id(0)

        @pl.when(i == 0)
        def _():
            acc[...] = jnp.zeros_like(acc)
            dbg_ref[...] = jnp.zeros_like(dbg_ref)
            dgo_ref[...] = jnp.zeros_like(dgo_ref)

        yp = y_ref[...]
        yg, yg_grad = _gelu_both(yp)
        ygb = yg.astype(BF16)
        gate = _sigmoid(_dot(ygb, wg_ref[...]) + bg_ref[...])
        ys = yg * gate
        dys, dgt = _rms_bwd(dm_ref[...], ys, _rms(ys), go_ref[...])
        dgo_ref[...] += jnp.sum(dgt, axis=0, keepdims=True)
        dpre = dys * yg * gate * (1.0 - gate)
        dbg_ref[...] += jnp.sum(dpre, axis=0, keepdims=True)
        dpb = dpre.astype(BF16)
        dyg = dys * gate + _dot_nt(dpb, wg_ref[...])
        dy_ref[...] = dyg * yg_grad
        acc[...] += _dot_tn(ygb, dpb)

        @pl.when(i == n - 1)
        def _():
            dwg_ref[...] = acc[...].astype(BF16)

    return _call_riding(
        body,
        exs,
        args=(y_pre, dmix, wg, bg, go),
        grid=(n,),
        in_specs=[
            pl.BlockSpec((tm, w), lambda i: (i, 0)),
            pl.BlockSpec((tm, w), lambda i: (i, 0)),
            pl.BlockSpec((w, w), lambda i: (0, 0)),
            pl.BlockSpec((1, w), lambda i: (0, 0)),
            pl.BlockSpec((1, w), lambda i: (0, 0)),
        ],
        out_specs=[
            pl.BlockSpec((tm, w), lambda i: (i, 0)),
            pl.BlockSpec((w, w), lambda i: (0, 0)),
            pl.BlockSpec((1, w), lambda i: (0, 0)),
            pl.BlockSpec((1, w), lambda i: (0, 0)),
        ],
        out_shape=[
            jax.ShapeDtypeStruct((s, w), F32),
            jax.ShapeDtypeStruct((w, w), BF16),
            jax.ShapeDtypeStruct((1, w), F32),
            jax.ShapeDtypeStruct((1, w), F32),
        ],
        scratch_shapes=[pltpu.VMEM((w, w), F32)],
        compiler_params=_params(1),
        name="glu_bwd",
    )


def _ssm_bwd_state(dy, v2, al, xprev):
    s = dy.shape[0]
    nt, nc, ns = xprev.shape
    hs = ns // 2

    def body(dy_ref, v_ref, al_ref, xp_ref, dys_ref, g_ref, dal_ref, dxp, gs):
        for q in range(PAIRS):
            dys_ref[q, :, 0:LANES] = dy_ref[pl.ds(2 * q, nc, stride=CHUNK), :].astype(BF16)
            dys_ref[q, :, LANES : 2 * LANES] = dy_ref[pl.ds(2 * q + 1, nc, stride=CHUNK), :].astype(BF16)
        acc = _dot(dys_ref[0], v_ref[0])
        for r in range(1, PAIRS):
            acc = acc + _dot(dys_ref[r], v_ref[r])
        dxp[...] = acc
        ar = al_ref[0:1, :]
        ai = al_ref[1:2, :]
        zero = jnp.zeros((1, hs), F32)
        gs[pl.ds(nc - 1, 1), 0:hs] = zero
        gs[pl.ds(nc - 1, 1), hs:ns] = zero

        def step(n, carry):
            gr, gi = carry
            c = nc - 2 - n
            nr = dxp[pl.ds(c + 1, 1), 0:hs] + ar * gr + ai * gi
            ni = dxp[pl.ds(c + 1, 1), hs:ns] + ar * gi - ai * gr
            gs[pl.ds(c, 1), 0:hs] = nr
            gs[pl.ds(c, 1), hs:ns] = ni
            return nr, ni

        lax.fori_loop(0, nc - 1, step, (zero, zero))
        gv = gs[...]
        xv = xp_ref[...]
        gr, gi = gv[:, 0:hs], gv[:, hs:ns]
        xr, xi = xv[:, 0:hs], xv[:, hs:ns]
        dal_ref[0:1, :] = jnp.sum(gr * xr + gi * xi, axis=0, keepdims=True)
        dal_ref[1:2, :] = jnp.sum(gi * xr - gr * xi, axis=0, keepdims=True)
        g_ref[...] = gv.astype(BF16)

    return _call(
        body,
        grid=(nt,),
        in_specs=[
            pl.BlockSpec((s, LANES), lambda t: (0, t)),
            pl.BlockSpec((None, PAIRS, 2 * LANES, ns), lambda t: (t, 0, 0, 0)),
            pl.BlockSpec((None, 2, hs), lambda t: (t, 0, 0)),
            pl.BlockSpec((None, nc, ns), lambda t: (t, 0, 0)),
        ],
        out_specs=[
            pl.BlockSpec((None, PAIRS, nc, 2 * LANES), lambda t: (t, 0, 0, 0)),
            pl.BlockSpec((None, nc, ns), lambda t: (t, 0, 0)),
            pl.BlockSpec((None, 2, hs), lambda t: (t, 0, 0)),
        ],
        out_shape=[
            jax.ShapeDtypeStruct((nt, PAIRS, nc, 2 * LANES), BF16),
            jax.ShapeDtypeStruct((nt, nc, ns), BF16),
            jax.ShapeDtypeStruct((nt, 2, hs), F32),
        ],
        scratch_shapes=[pltpu.VMEM((nc, ns), F32), pltpu.VMEM((nc, ns), F32)],
        compiler_params=_params(1),
        name="ssm_bwd_state",
    )(dy, v2, al, xprev)


def _ssm_bwd_main(us, dys, k2, w2, xprev, gst, consts, s, width, exs):
    nt, _, nc, _ = us.shape
    ns = xprev.shape[-1]
    rep_k, mask_k, rep_s, mask_s = consts
    nk = 2 * LANES
    gg = GROUPS_PER_TILE
    rows_of = [(gi, st, slice((st * gg + gi) * SSM_GROUP, (st * gg + gi + 1) * SSM_GROUP))
               for st in range(2) for gi in range(gg)]

    def body(us_ref, dys_ref, k_ref, w_ref, xp_ref, g_ref, rk_ref, mk_ref, rs_ref, ms_ref,
             du_ref, dk_ref, dw_ref, dv_ref, acc, du2, duf, dkf):
        q = pl.program_id(1)

        @pl.when(q == 0)
        def _():
            dkf[...] = jnp.zeros_like(dkf)

        usq = us_ref[q]
        gb = g_ref[...]
        dw = _gather_blocks(_dot_tn(usq, gb), ms_ref[...], rs_ref[...])
        dv = _gather_blocks(_dot_tn(dys_ref[q], xp_ref[...].astype(BF16)), ms_ref[...], rs_ref[...])
        for gi, st, rows in rows_of:
            dw_ref[gi, st] = dw[rows]
            dv_ref[gi, st] = dv[rows]
        acc[...] = _dot_nt(gb, w_ref[...])
        for m in range(PAIRS):

            @pl.when(q + m < PAIRS)
            def _():
                dyr = dys_ref[jnp.minimum(q + m, PAIRS - 1)]
                acc[...] += _dot_nt(dyr, k_ref[m])
                dkf[m] += _dot_tn(usq, dyr)

        du2[q] = acc[...]

        @pl.when(q == PAIRS - 1)
        def _():
            for p in range(PAIRS):
                duf[pl.ds(2 * p, nc, stride=CHUNK), :] = du2[p, :, 0:LANES]
                duf[pl.ds(2 * p + 1, nc, stride=CHUNK), :] = du2[p, :, LANES : 2 * LANES]
                dk = _gather_blocks(dkf[p], mk_ref[...], rk_ref[...])
                for gi, st, rows in rows_of:
                    dk_ref[gi, p, st] = dk[rows]
            du_ref[...] = duf[...].astype(BF16)

    def per_tile(rows, cols):
        return pl.BlockSpec((None, PAIRS, rows, cols), lambda t, q: (t, 0, 0, 0))

    def per_pair(rows, cols):
        return pl.BlockSpec((None, None, rows, cols), lambda t, q: (t, q, 0, 0))

    def whole(a):
        return pl.BlockSpec(a.shape, lambda t, q: (0, 0))

    return _call_riding(
        body,
        exs,
        args=(us, dys, k2, w2, xprev, gst, rep_k, mask_k, rep_s, mask_s),
        grid=(nt, PAIRS),
        in_specs=[
            per_tile(nc, nk),
            per_tile(nc, nk),
            per_tile(nk, nk),
            per_pair(nk, ns),
            pl.BlockSpec((None, nc, ns), lambda t, q: (t, 0, 0)),
            pl.BlockSpec((None, nc, ns), lambda t, q: (t, 0, 0)),
            whole(rep_k),
            whole(mask_k),
            whole(rep_s),
            whole(mask_s),
        ],
        out_specs=[
            pl.BlockSpec((s, LANES), lambda t, q: (0, t)),
            pl.BlockSpec((gg, PAIRS, 2, SSM_GROUP, LANES), lambda t, q: (t, 0, 0, 0, 0)),
            pl.BlockSpec((gg, None, 2, SSM_GROUP, LANES), lambda t, q: (t, q, 0, 0, 0)),
            pl.BlockSpec((gg, None, 2, SSM_GROUP, LANES), lambda t, q: (t, q, 0, 0, 0)),
        ],
        out_shape=[
            jax.ShapeDtypeStruct((s, width), BF16),
            jax.ShapeDtypeStruct((nt * gg, PAIRS, 2, SSM_GROUP, LANES), F32),
            jax.ShapeDtypeStruct((nt * gg, PAIRS, 2, SSM_GROUP, LANES), F32),
            jax.ShapeDtypeStruct((nt * gg, PAIRS, 2, SSM_GROUP, LANES), F32),
        ],
        scratch_shapes=[
            pltpu.VMEM((nc, nk), F32),
            pltpu.VMEM((PAIRS, nc, nk), F32),
            pltpu.VMEM((s, LANES), F32),
            pltpu.VMEM((PAIRS, nk, nk), F32),
        ],
        compiler_params=_params(2),
        name="ssm_bwd_main",
    )


def _sgu_bwd(z, dmix, dz, lng, lnb, wt, wtt, biasb, go, rb, exs):
    s = z.shape[0]
    w = lng.shape[-1]
    heads = wt.shape[0]
    ncol = (LANES, LANES)

    def body(zu_ref, zv_ref, dm_ref, dz_any, lng_ref, lnb_ref, wt_ref, wtt_ref, b_ref, go_ref,
             dz_ref, dw_ref, db_ref, dlg_ref, dlb_ref, dgo_ref, dzv):
        del dz_any
        i = pl.program_id(0)
        p = pl.program_id(1)

        @pl.when(jnp.logical_and(i == 0, p == 0))
        def _():
            dw_ref[...] = jnp.zeros_like(dw_ref)
            db_ref[...] = jnp.zeros_like(db_ref)
            dlg_ref[...] = jnp.zeros_like(dlg_ref)
            dlb_ref[...] = jnp.zeros_like(dlb_ref)
            dgo_ref[...] = jnp.zeros_like(dgo_ref)

        @pl.when(p == 0)
        def _():
            lng_v = lng_ref[...]
            for ck in range(rb // SGU_CHUNK):
                rows = slice(ck * SGU_CHUNK, (ck + 1) * SGU_CHUNK)
                zu = zu_ref[rows, :]
                zv = zv_ref[rows, :]
                u, vhat, rstd, vb, mix, u_grad, v_grad = _sgu_parts(zu, zv, lng_v, lnb_ref[...], wt_ref, b_ref)
                y = u * mix
                dy, dgt = _rms_bwd(dm_ref[rows, :], y, _rms(y), go_ref[...])
                dgo_ref[...] += jnp.sum(dgt, axis=0, keepdims=True)
                du = dy * mix
                dmx = dy * u
                dmb = dmx.astype(BF16)
                dvl = []
                for h in range(heads):
                    cols = slice(h * LANES, (h + 1) * LANES)
                    db_ref[h] += jnp.broadcast_to(jnp.sum(dmx[:, cols], axis=-1, keepdims=True), ncol)
                    dw_ref[h] += _dot_nt(dmb[:, cols], vb[:, cols])
                    dvl.append(_dot(wtt_ref[h], dmb[:, cols]))
                dvln = jnp.concatenate(dvl, axis=1)
                dlg_ref[...] += jnp.sum(dvln * vhat, axis=0, keepdims=True)
                dlb_ref[...] += jnp.sum(dvln, axis=0, keepdims=True)
                dvh = dvln * lng_v
                dv = rstd * (
                    dvh
                    - jnp.mean(dvh, axis=-1, keepdims=True)
                    - vhat * jnp.mean(dvh * vhat, axis=-1, keepdims=True)
                )
                dz_ref[rows, :] = (du * u_grad).astype(BF16)
                dzv[rows, :] = (dv * v_grad).astype(BF16)

        @pl.when(p == 1)
        def _():
            dz_ref[...] = dzv[...]

    return _call_riding(
        body,
        exs,
        args=(z, z, dmix, dz, lng, lnb, wt, wtt, biasb, go),
        grid=(s // rb, 2),
        in_specs=[
            pl.BlockSpec((rb, w), lambda i, p: (i, 1)),
            pl.BlockSpec((rb, w), lambda i, p: (i, 2)),
            pl.BlockSpec((rb, w), lambda i, p: (i, 1)),
            pl.BlockSpec(memory_space=pl.ANY),
            pl.BlockSpec((1, w), lambda i, p: (0, 0)),
            pl.BlockSpec((1, w), lambda i, p: (0, 0)),
            pl.BlockSpec((heads, SGU_CHUNK, SGU_CHUNK), lambda i, p: (0, 0, 0)),
            pl.BlockSpec((heads, SGU_CHUNK, SGU_CHUNK), lambda i, p: (0, 0, 0)),
            pl.BlockSpec((heads, SGU_CHUNK, LANES), lambda i, p: (0, 0, 0)),
            pl.BlockSpec((1, w), lambda i, p: (0, 0)),
        ],
        out_specs=[
            pl.BlockSpec((rb, w), lambda i, p: (i, 1 + p)),
            pl.BlockSpec((heads, SGU_CHUNK, SGU_CHUNK), lambda i, p: (0, 0, 0)),
            pl.BlockSpec((heads, SGU_CHUNK, LANES), lambda i, p: (0, 0, 0)),
            pl.BlockSpec((1, w), lambda i, p: (0, 0)),
            pl.BlockSpec((1, w), lambda i, p: (0, 0)),
            pl.BlockSpec((1, w), lambda i, p: (0, 0)),
        ],
        out_shape=[
            jax.ShapeDtypeStruct(dz.shape, dz.dtype),
            jax.ShapeDtypeStruct((heads, SGU_CHUNK, SGU_CHUNK), F32),
            jax.ShapeDtypeStruct((heads, SGU_CHUNK, LANES), F32),
            jax.ShapeDtypeStruct((1, w), F32),
            jax.ShapeDtypeStruct((1, w), F32),
            jax.ShapeDtypeStruct((1, w), F32),
        ],
        scratch_shapes=[pltpu.VMEM((rb, w), BF16)],
        input_output_aliases={3: 0},
        compiler_params=_params(2),
        name="sgu_bwd",
    )


def _in_proj_bwd(dz, w4, x, dres, g, tm, exs):
    s, d = x.shape
    nj, _, cw = w4.shape

    def body(dz_ref, w_ref, x_ref, dr_ref, g_ref, dx_ref, dg_ref, acc):
        i = pl.program_id(0)
        j = pl.program_id(1)

        @pl.when(j == 0)
        def _():
            acc[...] = jnp.zeros_like(acc)

        @pl.when(jnp.logical_and(i == 0, j == 0))
        def _():
            dg_ref[...] = jnp.zeros_like(dg_ref)

        acc[...] += _dot_nt(dz_ref[...], w_ref[...])

        @pl.when(j == nj - 1)
        def _():
            xv = x_ref[...]
            dx, dgt = _rms_bwd(acc[...], xv, _rms(xv), g_ref[...])
            dx_ref[...] = dr_ref[...] + dx
            dg_ref[...] += jnp.sum(dgt, axis=0, keepdims=True)

    row = pl.BlockSpec((tm, d), lambda i, j: (i, 0))
    vec = pl.BlockSpec((1, d), lambda i, j: (0, 0))
    return _call_riding(
        body,
        exs,
        args=(dz, w4, x, dres, g),
        scratch_shapes=[pltpu.VMEM((tm, d), F32)],
        grid=(s // tm, nj),
        in_specs=[
            pl.BlockSpec((tm, cw), lambda i, j: (i, j)),
            pl.BlockSpec((None, d, cw), lambda i, j: (j, 0, 0)),
            row,
            row,
            vec,
        ],
        out_specs=[row, vec],
        out_shape=[jax.ShapeDtypeStruct((s, d), F32), jax.ShapeDtypeStruct((1, d), F32)],
        compiler_params=_params(2),
        name="in_proj_bwd",
    )


_ANY = pl.BlockSpec(memory_space=pl.ANY)


def _position():
    x, y, c = lax.axis_index("x"), lax.axis_index("y"), lax.axis_index("c")
    return x, y, c, [(1 - x, y), (x, 1 - y), (1 - x, 1 - y)]


def _remote(src, dst, send_sems, recv_sems, k, to):
    return pltpu.make_async_remote_copy(
        src_ref=src, dst_ref=dst, send_sem=send_sems.at[k], recv_sem=recv_sems.at[k], device_id=to, device_id_type=MESH
    )


class _Riding:
    def __init__(self, srcs, out_shapes, n_sems, start, finish, continues=False):
        self.srcs, self.out_shapes, self.n_sems, self.start, self.finish = srcs, out_shapes, n_sems, start, finish
        self.continues = continues


def _call_riding(body, exs, *, grid, in_specs, out_specs, out_shape, scratch_shapes, args, **kw):
    n_in, n_out, n_scr = len(in_specs), len(out_specs), len(scratch_shapes)
    spec = dict(grid=grid, in_specs=list(in_specs), out_specs=list(out_specs), out_shape=list(out_shape))
    if not exs:
        return _call(body, scratch_shapes=list(scratch_shapes), **spec, **kw)(*args), []
    srcs = [a for ex in exs for a in ex.srcs]
    lands = [a for ex in exs for a in ex.out_shapes]
    xi, xo = len(srcs), len(lands)

    def fused(*refs):
        cin, xin = refs[:n_in], refs[n_in : n_in + xi]
        o = n_in + xi
        cout, xout = refs[o : o + n_out], refs[o + n_out : o + n_out + xo]
        scr = refs[o + n_out + xo :]
        ids = [pl.program_id(a) for a in range(len(grid))]
        first = functools.reduce(jnp.logical_and, [i == 0 for i in ids])
        last = functools.reduce(jnp.logical_and, [i == n - 1 for i, n in zip(ids, grid)])

        def each(half):
            si = so = 0
            for e, ex in enumerate(exs):
                ni, no = len(ex.srcs), len(ex.out_shapes)
                sems = scr[n_scr + 2 * e : n_scr + 2 * e + 2]
                getattr(ex, half)(xin[si : si + ni], xout[so : so + no], *sems)
                si, so = si + ni, so + no

        @pl.when(first)
        def _():
            each("start")

        body(*cin, *cout, *scr[:n_scr])

        @pl.when(last)
        def _():
            each("finish")

    sems = [pltpu.SemaphoreType.DMA((ex.n_sems,)) for ex in exs for _ in range(2)]
    aliases = dict(kw.pop("input_output_aliases", {}))
    si = so = 0
    for ex in exs:
        if ex.continues:
            aliases[n_in + si + len(ex.srcs) - 1] = n_out + so
        si, so = si + len(ex.srcs), so + len(ex.out_shapes)
    kw["input_output_aliases"] = aliases
    spec["in_specs"] += [_ANY] * xi
    spec["out_specs"] += [_ANY] * xo
    spec["out_shape"] += lands
    outs = _call(fused, scratch_shapes=list(scratch_shapes) + sems, **spec, **kw)(*args, *srcs)
    return outs[:n_out], outs[n_out:]


def _gather_one(shard, peers=(0, 1, 2), into=None):
    def own_copies(ins, outs, send, recv):
        x, y, c, chips = _position()
        me = 2 * x + y
        sib = (x, y, 1 - c)
        cps = [] if into is not None else [_remote(ins[0].at[c], outs[0].at[me, c], send, recv, 0, sib)]
        cps += [_remote(ins[0].at[c], outs[0].at[me, c], send, recv, 1 + k, (*chips[k], c)) for k in peers]
        return cps, (c, me, sib, chips)

    def start(ins, outs, send, recv):
        for cp in own_copies(ins, outs, send, recv)[0]:
            cp.start()

    def finish(ins, outs, send, recv):
        cps, (c, me, sib, chips) = own_copies(ins, outs, send, recv)
        for k in peers:
            px, py = chips[k]
            landed = outs[0].at[2 * px + py, c]
            _remote(ins[0].at[c], landed, send, recv, 1 + k, (px, py, c)).wait_recv()
            cps.append(_remote(landed, landed, send, recv, 4 + k, sib))
            cps[-1].start()
        if into is None:
            _remote(ins[0].at[1 - c], outs[0].at[me, 1 - c], send, recv, 0, sib).wait_recv()
        for k in peers:
            px, py = chips[k]
            theirs = outs[0].at[2 * px + py, 1 - c]
            _remote(theirs, theirs, send, recv, 4 + k, sib).wait_recv()
        for cp in cps:
            cp.wait_send()

    land = jax.ShapeDtypeStruct((N_CHIPS,) + shard.shape, shard.dtype)
    return _Riding([shard] + ([] if into is None else [into]), [land], 7, start, finish, continues=into is not None)


def _to_owner_one(p, peers=(0, 1, 2), into=None):
    def copies(ins, outs, send, recv):
        x, y, c, chips = _position()
        return [_remote(ins[0].at[2 * chips[k][0] + chips[k][1]], outs[0].at[k], send, recv, k, (*chips[k], c)) for k in peers]

    def start(ins, outs, send, recv):
        for cp in copies(ins, outs, send, recv):
            cp.start()

    def finish(ins, outs, send, recv):
        cps = copies(ins, outs, send, recv)
        for cp in cps:
            cp.wait_recv()
        for cp in cps:
            cp.wait_send()

    land = jax.ShapeDtypeStruct((3,) + p.shape[1:], p.dtype)
    return _Riding([p] + ([] if into is None else [into]), [land], 3, start, finish, continues=into is not None)


def _to_sibling_one(g):
    def copies(ins, outs, send, recv):
        x, y, c, _ = _position()
        return [_remote(ins[0].at[j, 1 - c], outs[0].at[j], send, recv, j, (x, y, 1 - c)) for j in range(N_CHIPS)]

    def start(ins, outs, send, recv):
        for cp in copies(ins, outs, send, recv):
            cp.start()

    def finish(ins, outs, send, recv):
        cps = copies(ins, outs, send, recv)
        for cp in cps:
            cp.wait_recv()
        for cp in cps:
            cp.wait_send()

    return _Riding([g], [jax.ShapeDtypeStruct((g.shape[0],) + g.shape[2:], g.dtype)], N_CHIPS, start, finish)


def _gather_chips(arrs, name):
    n = len(arrs)

    def body(*refs):
        ins, outs = refs[:n], refs[n : 2 * n]
        send_sems, recv_sems, local_sems = refs[2 * n :]
        x, y, c, chips = _position()
        me = 2 * x + y
        local = [pltpu.make_async_copy(ins[i], outs[i].at[me], local_sems.at[i]) for i in range(n)]
        for cp in local:
            cp.start()
        sends = []
        for k, (px, py) in enumerate(chips):
            for i in range(n):
                sends.append(_remote(ins[i], outs[i].at[me], send_sems, recv_sems, k * n + i, (px, py, c)))
                sends[-1].start()
        for k, (px, py) in enumerate(chips):
            for i in range(n):
                _remote(ins[i], outs[i].at[2 * px + py], send_sems, recv_sems, k * n + i, (px, py, c)).wait_recv()
        for cp in sends:
            cp.wait_send()
        for cp in local:
            cp.wait()

    return _call(
        body,
        in_specs=[_ANY] * n,
        out_specs=[_ANY] * n,
        out_shape=[jax.ShapeDtypeStruct((N_CHIPS,) + a.shape, a.dtype) for a in arrs],
        scratch_shapes=[
            pltpu.SemaphoreType.DMA((3 * n,)),
            pltpu.SemaphoreType.DMA((3 * n,)),
            pltpu.SemaphoreType.DMA((n,)),
        ],
        name=name,
    )(*arrs)


def _pair_exchange(gs, name):
    n = len(gs)

    def body(*refs):
        ins, outs = refs[:n], refs[n : 2 * n]
        send_sems, recv_sems = refs[2 * n :]
        x, y, c, _ = _position()
        sib = (x, y, 1 - c)
        cps = []
        for i in range(n):
            for j in range(N_CHIPS):
                cps.append(_remote(ins[i].at[j, 1 - c], outs[i].at[j], send_sems, recv_sems, i * N_CHIPS + j, sib))
                cps[-1].start()
        for cp in cps:
            cp.wait_recv()
        for cp in cps:
            cp.wait_send()

    return _call(
        body,
        in_specs=[_ANY] * n,
        out_specs=[_ANY] * n,
        out_shape=[jax.ShapeDtypeStruct((g.shape[0],) + g.shape[2:], g.dtype) for g in gs],
        scratch_shapes=[pltpu.SemaphoreType.DMA((n * N_CHIPS,)), pltpu.SemaphoreType.DMA((n * N_CHIPS,))],
        name=name,
    )(*gs)


def _gather_halves(shards, name):
    n = len(shards)

    def body(*refs):
        ins, outs = refs[:n], refs[n : 2 * n]
        ici_send, ici_recv, d2d_send, d2d_recv = refs[2 * n :]
        x, y, c, chips = _position()
        me = 2 * x + y
        sib = (x, y, 1 - c)
        started = []
        for i in range(n):
            started.append(_remote(ins[i].at[c], outs[i].at[me, c], d2d_send, d2d_recv, i, sib))
            started[-1].start()
        for k, (px, py) in enumerate(chips):
            for i in range(n):
                started.append(_remote(ins[i].at[c], outs[i].at[me, c], ici_send, ici_recv, k * n + i, (px, py, c)))
                started[-1].start()
        for k, (px, py) in enumerate(chips):
            for i in range(n):
                landed = outs[i].at[2 * px + py, c]
                _remote(ins[i].at[c], landed, ici_send, ici_recv, k * n + i, (px, py, c)).wait_recv()
                started.append(_remote(landed, landed, d2d_send, d2d_recv, (k + 1) * n + i, sib))
                started[-1].start()
        for i in range(n):
            _remote(ins[i].at[1 - c], outs[i].at[me, 1 - c], d2d_send, d2d_recv, i, sib).wait_recv()
        for k, (px, py) in enumerate(chips):
            for i in range(n):
                theirs = outs[i].at[2 * px + py, 1 - c]
                _remote(theirs, theirs, d2d_send, d2d_recv, (k + 1) * n + i, sib).wait_recv()
        for cp in started:
            cp.wait_send()

    return _call(
        body,
        in_specs=[_ANY] * n,
        out_specs=[_ANY] * n,
        out_shape=[jax.ShapeDtypeStruct((N_CHIPS,) + a.shape, a.dtype) for a in shards],
        scratch_shapes=[
            pltpu.SemaphoreType.DMA((3 * n,)),
            pltpu.SemaphoreType.DMA((3 * n,)),
            pltpu.SemaphoreType.DMA((4 * n,)),
            pltpu.SemaphoreType.DMA((4 * n,)),
        ],
        name=name,
    )(*shards)


def _place_own(shard, gathered, pos, name):
    _, hr, cols = shard.shape
    tr = _row_tile(hr, 512)

    def body(pos_ref, s_ref, g_any, o_ref):
        del pos_ref, g_any
        o_ref[...] = s_ref[...]

    return _call(
        body,
        grid_spec=pltpu.PrefetchScalarGridSpec(
            num_scalar_prefetch=1,
            grid=(hr // tr,),
            in_specs=[pl.BlockSpec((None, tr, cols), lambda i, p: (p[1], i, 0)), _ANY],
            out_specs=pl.BlockSpec((None, None, tr, cols), lambda i, p: (p[0], p[1], i, 0)),
        ),
        out_shape=jax.ShapeDtypeStruct(gathered.shape, gathered.dtype),
        input_output_aliases={2: 0},
        compiler_params=_params(1),
        name=name,
    )(pos, shard, gathered)


def _chip_exchange(ps, name):
    n = len(ps)

    def body(*refs):
        ins, outs = refs[:n], refs[n : 2 * n]
        send_sems, recv_sems = refs[2 * n :]
        x, y, c, chips = _position()
        sends = []
        for k, (px, py) in enumerate(chips):
            for i in range(n):
                sends.append(_remote(ins[i].at[2 * px + py], outs[i].at[k], send_sems, recv_sems, k * n + i, (px, py, c)))
                sends[-1].start()
        for cp in sends:
            cp.wait_recv()
        for cp in sends:
            cp.wait_send()

    return _call(
        body,
        in_specs=[_ANY] * n,
        out_specs=[_ANY] * n,
        out_shape=[jax.ShapeDtypeStruct((3,) + p.shape[1:], p.dtype) for p in ps],
        scratch_shapes=[pltpu.SemaphoreType.DMA((3 * n,)), pltpu.SemaphoreType.DMA((3 * n,))],
        name=name,
    )(*ps)


def _pair_share(fs, name):
    n = len(fs)

    def body(*refs):
        ins, outs = refs[:n], refs[n : 2 * n]
        send_sems, recv_sems = refs[2 * n :]
        x, y, c, _ = _position()
        sib = (x, y, 1 - c)
        sends = [_remote(ins[i].at[c], outs[i].at[c], send_sems, recv_sems, i, sib) for i in range(n)]
        for cp in sends:
            cp.start()
        for i in range(n):
            _remote(ins[i].at[1 - c], outs[i].at[1 - c], send_sems, recv_sems, i, sib).wait_recv()
        for cp in sends:
            cp.wait_send()

    return _call(
        body,
        in_specs=[_ANY] * n,
        out_specs=[_ANY] * n,
        out_shape=[jax.ShapeDtypeStruct(f.shape, f.dtype) for f in fs],
        input_output_aliases={i: i for i in range(n)},
        scratch_shapes=[pltpu.SemaphoreType.DMA((n,)), pltpu.SemaphoreType.DMA((n,))],
        name=name,
    )(*fs)


def _pair_sum(g, r, core, name):
    nj, _, hr, cols = g.shape
    tr = _row_tile(hr, 256)

    def body(c_ref, g_ref, r_ref, o_ref):
        del c_ref
        o_ref[...] = (g_ref[...].astype(F32) + r_ref[...].astype(F32)).astype(o_ref.dtype)

    return _call(
        body,
        grid_spec=pltpu.PrefetchScalarGridSpec(
            num_scalar_prefetch=1,
            grid=(nj, hr // tr),
            in_specs=[
                pl.BlockSpec((None, None, tr, cols), lambda j, i, c: (j, c[0], i, 0)),
                pl.BlockSpec((None, tr, cols), lambda j, i, c: (j, i, 0)),
            ],
            out_specs=pl.BlockSpec((None, tr, cols), lambda j, i, c: (j, i, 0)),
        ),
        out_shape=jax.ShapeDtypeStruct((nj, hr, cols), g.dtype),
        compiler_params=_params(2),
        name=name,
    )(core, g, r)


def _chip_sum(p, q, pos, name):
    nq, hr, cols = q.shape
    tr = _row_tile(hr, 256)

    def body(pos_ref, p_ref, *refs):
        del pos_ref
        o_ref = refs[nq]
        tot = p_ref[...].astype(F32)
        for k in range(nq):
            tot = tot + refs[k][...].astype(F32)
        o_ref[...] = tot

    return _call(
        body,
        grid_spec=pltpu.PrefetchScalarGridSpec(
            num_scalar_prefetch=1,
            grid=(hr // tr,),
            in_specs=[pl.BlockSpec((None, tr, cols), lambda i, s: (s[0], i, 0))]
            + [pl.BlockSpec((None, tr, cols), functools.partial(lambda k, i, s: (k, i, 0), k)) for k in range(nq)],
            out_specs=pl.BlockSpec((None, tr, cols), lambda i, s: (s[1], i, 0)),
        ),
        out_shape=jax.ShapeDtypeStruct((2, hr, cols), F32),
        compiler_params=_params(1),
        name=name,
    )(pos, p, *([q] * nq))


def _adamw(w, g, m, v, name):
    rows, cols = w.shape
    tr = _row_tile(rows, max(8, (2**18) // cols), mult=8)
    c1 = 1.0 - ADAM_B1**ADAM_STEP
    c2 = 1.0 - ADAM_B2**ADAM_STEP

    def body(w_ref, g_ref, m_ref, v_ref, d_ref, nm_ref, nv_ref):
        gv = g_ref[...]
        nm = ADAM_B1 * m_ref[...] + (1.0 - ADAM_B1) * gv
        nv = ADAM_B2 * v_ref[...] + (1.0 - ADAM_B2) * (gv * gv)
        nm_ref[...] = nm
        nv_ref[...] = nv
        d_ref[...] = -ADAM_LR * ((nm / c1) / (jnp.sqrt(nv / c2) + ADAM_EPS) + ADAM_WD * w_ref[...])

    spec = pl.BlockSpec((tr, cols), lambda i: (i, 0))
    sds = jax.ShapeDtypeStruct((rows, cols), F32)
    return _call(
        body,
        grid=(rows // tr,),
        in_specs=[spec] * 4,
        out_specs=[spec] * 3,
        out_shape=[sds] * 3,
        compiler_params=_params(1),
        name=name,
    )(w, g, m, v)


_TILE_ELEMS = 8 * LANES
_FLAT_ROW_MULT = 8 * 2 * N_CHIPS


def _flat_rows(shape):
    n = math.prod(shape)
    return (n + _TILE_ELEMS - 1) // _TILE_ELEMS * 8


def _pack(arrs):
    parts = []
    for a in arrs:
        rows = _flat_rows(a.shape)
        flat = a.reshape(-1).astype(F32)
        flat = jnp.pad(flat, (0, rows * LANES - flat.shape[0]))
        parts.append(flat.reshape(rows, LANES))
    total = sum(p.shape[0] for p in parts)
    pad = -total % _FLAT_ROW_MULT
    if pad:
        parts.append(jnp.zeros((pad, LANES), F32))
    return jnp.concatenate(parts, axis=0)


def _unpack(flat, shapes):
    out, row = [], 0
    for shp in shapes:
        rows = _flat_rows(shp)
        out.append(flat[row : row + rows].reshape(-1)[: math.prod(shp)].reshape(shp))
        row += rows
    return out


def kernel(x, norm_mix_g, w_in, ssm_a_re, ssm_a_im, ssm_b_re, ssm_b_im, ssm_c_re, ssm_c_im, ssm_d, ssm_log_dt, ssm_glu_w, ssm_glu_b, sgu_ln_g, sgu_ln_b, sgu_w, sgu_b, out_norm_ssm_g, out_norm_sgu_g, w_out, norm_mlp_g, w_up, w_down, norm_final_g, loss_target, m_norm_mix_g, m_w_in, m_ssm_a_re, m_ssm_a_im, m_ssm_b_re, m_ssm_b_im, m_ssm_c_re, m_ssm_c_im, m_ssm_d, m_ssm_log_dt, m_ssm_glu_w, m_ssm_glu_b, m_sgu_ln_g, m_sgu_ln_b, m_sgu_w, m_sgu_b, m_out_norm_ssm_g, m_out_norm_sgu_g, m_w_out, m_norm_mlp_g, m_w_up, m_w_down, m_norm_final_g, v_norm_mix_g, v_w_in, v_ssm_a_re, v_ssm_a_im, v_ssm_b_re, v_ssm_b_im, v_ssm_c_re, v_ssm_c_im, v_ssm_d, v_ssm_log_dt, v_ssm_glu_w, v_ssm_glu_b, v_sgu_ln_g, v_sgu_ln_b, v_sgu_w, v_sgu_b, v_out_norm_ssm_g, v_out_norm_sgu_g, v_w_out, v_norm_mlp_g, v_w_up, v_w_down, v_norm_final_g):
    weights = dict(norm_mix_g=norm_mix_g, w_in=w_in, ssm_a_re=ssm_a_re, ssm_a_im=ssm_a_im, ssm_b_re=ssm_b_re, ssm_b_im=ssm_b_im, ssm_c_re=ssm_c_re, ssm_c_im=ssm_c_im, ssm_d=ssm_d, ssm_log_dt=ssm_log_dt, ssm_glu_w=ssm_glu_w, ssm_glu_b=ssm_glu_b, sgu_ln_g=sgu_ln_g, sgu_ln_b=sgu_ln_b, sgu_w=sgu_w, sgu_b=sgu_b, out_norm_ssm_g=out_norm_ssm_g, out_norm_sgu_g=out_norm_sgu_g, w_out=w_out, norm_mlp_g=norm_mlp_g, w_up=w_up, w_down=w_down, norm_final_g=norm_final_g)
    mom_m = dict(norm_mix_g=m_norm_mix_g, w_in=m_w_in, ssm_a_re=m_ssm_a_re, ssm_a_im=m_ssm_a_im, ssm_b_re=m_ssm_b_re, ssm_b_im=m_ssm_b_im, ssm_c_re=m_ssm_c_re, ssm_c_im=m_ssm_c_im, ssm_d=m_ssm_d, ssm_log_dt=m_ssm_log_dt, ssm_glu_w=m_ssm_glu_w, ssm_glu_b=m_ssm_glu_b, sgu_ln_g=m_sgu_ln_g, sgu_ln_b=m_sgu_ln_b, sgu_w=m_sgu_w, sgu_b=m_sgu_b, out_norm_ssm_g=m_out_norm_ssm_g, out_norm_sgu_g=m_out_norm_sgu_g, w_out=m_w_out, norm_mlp_g=m_norm_mlp_g, w_up=m_w_up, w_down=m_w_down, norm_final_g=m_norm_final_g)
    mom_v = dict(norm_mix_g=v_norm_mix_g, w_in=v_w_in, ssm_a_re=v_ssm_a_re, ssm_a_im=v_ssm_a_im, ssm_b_re=v_ssm_b_re, ssm_b_im=v_ssm_b_im, ssm_c_re=v_ssm_c_re, ssm_c_im=v_ssm_c_im, ssm_d=v_ssm_d, ssm_log_dt=v_ssm_log_dt, ssm_glu_w=v_ssm_glu_w, ssm_glu_b=v_ssm_glu_b, sgu_ln_g=v_sgu_ln_g, sgu_ln_b=v_sgu_ln_b, sgu_w=v_sgu_w, sgu_b=v_sgu_b, out_norm_ssm_g=v_out_norm_ssm_g, out_norm_sgu_g=v_out_norm_sgu_g, w_out=v_w_out, norm_mlp_g=v_norm_mlp_g, w_up=v_w_up, w_down=v_w_down, norm_final_g=v_norm_final_g)
    names = list(weights)
    large = ["w_in", "ssm_glu_w", "w_out", "w_up", "w_down"]
    small = [n for n in names if n not in large]

    s, d = x.shape[1], x.shape[2]
    xs = x.reshape(s, d)
    target = loss_target.reshape(s, d)
    width = ssm_glu_w.shape[-1]
    ff = w_down.shape[1] * N_CHIPS
    tm = min(512, s)
    core = lax.axis_index("c").astype(jnp.int32).reshape(1)
    chip = (2 * lax.axis_index("x") + lax.axis_index("y")).astype(jnp.int32).reshape(1)
    pos = jnp.concatenate([chip, core])

    shards = [w[0].astype(BF16).reshape(2, w.shape[1] // 2, w.shape[2]) for w in (w_in, ssm_glu_w, w_out, w_up, w_down)]
    ssm_args = (ssm_a_re[0], ssm_a_im[0], ssm_b_re[0], ssm_b_im[0], ssm_c_re[0], ssm_c_im[0], ssm_d[0], ssm_log_dt[0])
    (k2c, w2c, v2c, al), ssm_vjp = jax.vjp(_ssm_mats, *ssm_args)
    consts = _spread_consts(ssm_b_re.shape[-1], ssm_a_re.shape[-1])
    (k2b, w2b, v2b), (w_in_g,) = _ssm_spread(k2c, w2c, v2c, consts, [_gather_one(shards[0])])
    w_in4 = _place_own(shards[0], w_in_g, pos, "place_w_in").reshape(N_CHIPS, d, w_in.shape[2])
    causal = jnp.tril(jnp.ones((SGU_CHUNK, SGU_CHUNK), dtype=bool))
    wt = jnp.where(causal[None], sgu_w[0], 0.0)
    wtb = wt.astype(BF16)
    wttb = jnp.swapaxes(wt, 1, 2).astype(BF16)
    heads = sgu_w.shape[1]
    biasb = jnp.broadcast_to(sgu_b[0][:, :, None], (heads, SGU_CHUNK, LANES))

    tm_big = min(1024, s)
    (z, h1b), (glu_g, out_g, up_g) = _in_proj(
        xs, norm_mix_g, w_in4, tm_big,
        [_gather_one(shards[1]), _gather_one(shards[2]), _gather_one(shards[3], peers=(0, 1))])
    wg_full = _place_own(shards[1], glu_g, pos, "place_glu_w").reshape(width, width)
    w_out_full = _place_own(shards[2], out_g, pos, "place_w_out").reshape(d, d)
    (y_pre, xprev, us), (up_g,) = _ssm_fwd(z, k2b, w2b, v2b, al, [_gather_one(shards[3], peers=(2,), into=up_g)])
    up4 = _place_own(shards[3], up_g, pos, "place_w_up").reshape(N_CHIPS, d, w_up.shape[2])
    mixed = _glu_fwd(y_pre, wg_full, ssm_glu_b, out_norm_ssm_g, d, tm)
    mixed = _sgu_fwd(z, mixed, sgu_ln_g, sgu_ln_b, wtb, biasb, out_norm_sgu_g, tm)
    x1 = _out_proj(xs, mixed, w_out_full, tm)
    tf = min(1024, up4.shape[-1])
    (up, h2b), (down_g,) = _mlp_up(x1, norm_mlp_g, up4, tm_big, tf, [_gather_one(shards[4])])
    w_down_full = _place_own(shards[4], down_g, pos, "place_w_down").reshape(ff, d)
    loss_part, dx2, dx2b, d_norm_final = _mlp_down_loss(
        x1, up, w_down_full, target, norm_final_g.reshape(1, d), tm, tf)

    dup, dh2 = _mlp_bwd(dx2b, up, up4, w_down_full, tm_big, min(512, up4.shape[-1]))
    dx1, dx1b, d_norm_mlp = _norm_bwd(dh2, x1, dx2, norm_mlp_g, min(256, s), "norm_mlp_bwd")
    hr_big = d // 2
    tkk = min(1024, s)
    g_down = _grad_w(
        up, dx2b, (N_CHIPS, 2, hr_big, d),
        lambda i, j: (i // (2 * (hr_big // min(1024, hr_big))), (i // (hr_big // min(1024, hr_big))) % 2,
                      i % (hr_big // min(1024, hr_big)), j),
        min(1024, hr_big), d, tkk, True, "grad_w_down")
    tn_up = min(2048, up4.shape[-1])
    per_up = up4.shape[-1] // tn_up
    g_up, sib_down = _grad_w(
        h2b, dup, (N_CHIPS, 2, hr_big, up4.shape[-1]),
        lambda i, j: (j // per_up, i // (hr_big // min(1024, hr_big)), i % (hr_big // min(1024, hr_big)), j % per_up),
        min(1024, hr_big), tn_up, tkk, False, "grad_w_up", exs=[_to_sibling_one(g_down)])
    pair_down = _pair_sum(g_down, sib_down, core, "pair_sum_w_down")
    dmix = _out_proj_bwd(dx1b, w_out_full, tm)
    hr_out = d // (2 * N_CHIPS)
    g_out, sib_up = _grad_w(
        mixed, dx1b, (1, 1, d, d), lambda i, j: (0, 0, i, j), min(1024, d), d, tkk, False, "grad_w_out",
        exs=[_to_sibling_one(g_up)])
    g_out = g_out.reshape(N_CHIPS, 2, hr_out, d)
    pair_up = _pair_sum(g_up, sib_up, core, "pair_sum_w_up")
    (dy_pre, g_glu, d_glu_b, d_norm_ssm), (sib_out,) = _glu_bwd(
        y_pre, dmix, wg_full, ssm_glu_b, out_norm_ssm_g, tm, [_to_sibling_one(g_out)])
    pair_out = _pair_sum(g_out, sib_out, core, "pair_sum_w_out")
    dys, gst, d_al = _ssm_bwd_state(dy_pre, v2b, al, xprev)
    (dz, d_k2, d_w2, d_v2), (chips_down,) = _ssm_bwd_main(
        us, dys, k2b, w2b, xprev, gst, consts, s, z.shape[1], [_to_owner_one(pair_down)])
    (dz, d_wt, d_bias, d_ln_g, d_ln_b, d_norm_sgu), (chips_up,) = _sgu_bwd(
        z, dmix, dz, sgu_ln_g, sgu_ln_b, wtb, wttb, biasb, out_norm_sgu_g, tm,
        [_to_owner_one(pair_up, peers=(0, 1))])
    cw_in = w_in4.shape[-1]
    g_in, chips_out, chips_up = _grad_w(
        h1b, dz, (N_CHIPS, 2, hr_big, cw_in),
        lambda i, j: (j, i // (hr_big // min(1024, hr_big)), i % (hr_big // min(1024, hr_big)), 0),
        min(1024, hr_big), 2 * cw_in, tkk, False, "grad_w_in",
        exs=[_to_owner_one(pair_out), _to_owner_one(pair_up, peers=(2,), into=chips_up)], chunks=2)
    (sib_in,) = _pair_exchange([g_in], "w_in_to_sibling")
    pair_in = _pair_sum(g_in, sib_in, core, "pair_sum_w_in")
    (grad_x, d_norm_mix), (chips_in,) = _in_proj_bwd(dz, w_in4, xs, dx1, norm_mix_g, tm, [_to_owner_one(pair_in)])

    d_ssm = ssm_vjp((d_k2, d_w2, d_v2, d_al))
    small_grads = dict(
        norm_mix_g=d_norm_mix, ssm_a_re=d_ssm[0], ssm_a_im=d_ssm[1], ssm_b_re=d_ssm[2], ssm_b_im=d_ssm[3],
        ssm_c_re=d_ssm[4], ssm_c_im=d_ssm[5], ssm_d=d_ssm[6], ssm_log_dt=d_ssm[7], ssm_glu_b=d_glu_b,
        sgu_ln_g=d_ln_g, sgu_ln_b=d_ln_b, sgu_w=jnp.where(causal[None], d_wt, 0.0), sgu_b=d_bias[:, :, 0],
        out_norm_ssm_g=d_norm_ssm, out_norm_sgu_g=d_norm_sgu, norm_mlp_g=d_norm_mlp, norm_final_g=d_norm_final)
    flat = _pack([small_grads[n] for n in small])
    hr_small = flat.shape[0] // (2 * N_CHIPS)
    g_small = flat.reshape(N_CHIPS, 2, hr_small, LANES)

    hr_glu = width // (2 * N_CHIPS)
    grads = [g_glu.reshape(N_CHIPS, 2, hr_glu, width), g_small]
    tags = ["glu_w", "small"]
    from_sib = _pair_exchange(grads, "grads_to_sibling")
    pair = [_pair_sum(g, r, core, "pair_sum_" + t) for g, r, t in zip(grads, from_sib, tags)]
    from_chips = _chip_exchange(pair, "grads_to_owner")
    pair = [pair_in, pair[0], pair_out, pair_up, pair_down, pair[1]]
    from_chips = [chips_in, from_chips[0], chips_out, chips_up, chips_down, from_chips[1]]
    tags = ["w_in", "glu_w", "w_out", "w_up", "w_down", "small"]
    halves = [_chip_sum(p, q, pos, "chip_sum_" + t) for p, q, t in zip(pair, from_chips, tags)]
    owned = _pair_share(halves, "grads_to_both_cores")
    (small_all,) = _gather_chips([owned[5]], "gather_small_grads")
    small_flat = small_all.reshape(flat.shape)

    grad_out, delta_out, m_out, v_out = {}, {}, {}, {}
    for n, g in zip(large, owned[:5]):
        shp = weights[n].shape
        g2 = g.reshape(shp[1], shp[2])
        dl, nm, nv = _adamw(weights[n][0], g2, mom_m[n][0], mom_v[n][0], "adamw_" + n)
        grad_out[n], delta_out[n], m_out[n], v_out[n] = g2.reshape(shp), dl.reshape(shp), nm.reshape(shp), nv.reshape(shp)
    shapes = [weights[n].shape for n in small]
    dl, nm, nv = _adamw(
        _pack([weights[n] for n in small]), small_flat, _pack([mom_m[n] for n in small]),
        _pack([mom_v[n] for n in small]), "adamw_small")
    for n, g, a, b, c in zip(small, _unpack(small_flat, shapes), _unpack(dl, shapes), _unpack(nm, shapes), _unpack(nv, shapes)):
        grad_out[n], delta_out[n], m_out[n], v_out[n] = g, a, b, c

    loss = lax.psum(loss_part[0, 0], ("x", "y", "c"))
    return (loss, grad_x.reshape(x.shape), *[grad_out[n] for n in names], *[delta_out[n] for n in names],
            *[m_out[n] for n in names], *[v_out[n] for n in names])
```

```python
import functools
import math

import numpy as np
import jax
import jax.numpy as jnp
from jax import lax
from jax.experimental import pallas as pl
from jax.experimental.pallas import tpu as pltpu

F32 = jnp.float32
BF16 = jnp.bfloat16
MESH = pl.DeviceIdType.MESH
HIGHEST = lax.Precision.HIGHEST

EPS = 1e-6
ADAM_LR = 0.001
ADAM_B1 = 0.9
ADAM_B2 = 0.999
ADAM_EPS = 1e-08
ADAM_WD = 0.01
ADAM_STEP = 10

N_CHIPS = 4
LANES = 128
SSM_GROUP = 16
SSM_STATE = 64
GROUPS_PER_TILE = LANES // SSM_GROUP
CHUNK = 16
PAIRS = CHUNK // 2
SGU_CHUNK = 128
VMEM_LIMIT = 56 * 2**20


def _params(n_axes, vmem=VMEM_LIMIT):
    return pltpu.CompilerParams(dimension_semantics=("arbitrary",) * n_axes, vmem_limit_bytes=vmem)


def _call(body, **kw):
    return pl.pallas_call(body, **kw)


def _dot(a, b):
    return jnp.dot(a, b, preferred_element_type=F32)


def _dot_nt(a, b):
    return lax.dot_general(a, b, (((1,), (1,)), ((), ())), preferred_element_type=F32)


def _dot_tn(a, b):
    return lax.dot_general(a, b, (((0,), (0,)), ((), ())), preferred_element_type=F32)


_GELU_K = math.sqrt(2.0 / math.pi)
_GELU_C = 0.044715


def _gelu_both(x):
    x2 = x * x
    t = jnp.tanh(x * (_GELU_K + (_GELU_K * _GELU_C) * x2))
    hx = 0.5 * x
    onep = 1.0 + t
    return hx * onep, 0.5 * onep + hx * (1.0 - t * t) * (_GELU_K + (3.0 * _GELU_K * _GELU_C) * x2)


def _gelu(x):
    return _gelu_both(x)[0]


def _sigmoid(x):
    return 1.0 / (1.0 + jnp.exp(-x))


def _rms(x):
    return lax.rsqrt(jnp.mean(x * x, axis=-1, keepdims=True) + EPS)


def _rms_bwd(dy, x, r, g):
    a = dy * g
    dx = r * a - x * (r * r * r) * jnp.mean(a * x, axis=-1, keepdims=True)
    return dx, dy * x * r


def _row_tile(rows, target, mult=16):
    for t in range(min(rows, target), 0, -1):
        if rows % t == 0 and t % mult == 0:
            return t
    return rows


def _ssm_mats(a_re, a_im, b_re, b_im, c_re, c_im, d, log_dt):
    g, p = a_re.shape
    h = b_re.shape[-1]
    nt = g // GROUPS_PER_TILE
    dt = jnp.exp(log_dt)[:, None]
    lr, li = a_re * dt, a_im * dt

    def apow(l, lr, li):
        mag = jnp.exp(lr * l)
        return mag * jnp.cos(li * l), mag * jnp.sin(li * l)

    ar, ai = apow(1.0, lr, li)
    den = a_re * a_re + a_im * a_im
    qr = ((ar - 1.0) * a_re + ai * a_im) / den
    qi = (ai * a_re - (ar - 1.0) * a_im) / den
    bt_re, bt_im = jnp.swapaxes(b_re, 1, 2), jnp.swapaxes(b_im, 1, 2)
    bbr = qr[:, None, :] * bt_re - qi[:, None, :] * bt_im
    bbi = qr[:, None, :] * bt_im + qi[:, None, :] * bt_re
    b_same = jnp.concatenate([bbr, bbi], axis=-1)[:, None]
    b_swap = jnp.concatenate([bbi, bbr], axis=-1)[:, None]
    ls = jnp.arange(CHUNK + 1, dtype=F32)[None, :, None]
    pr, pi = apow(ls, lr[:, None, :], li[:, None, :])
    p_same = jnp.concatenate([pr, pr], axis=-1)[:, :, None, :]
    p_sign = jnp.concatenate([-pi, pi], axis=-1)[:, :, None, :]
    tj = p_same[:, :CHUNK] * b_same + p_sign[:, :CHUNK] * b_swap
    c_conj = jnp.concatenate([c_re, -c_im], axis=-1)
    k = jnp.einsum("goq,gliq->glio", c_conj, tj, precision=HIGHEST)
    k = k.at[:, 0].add(d[:, :, None] * jnp.eye(h, dtype=F32))

    zero = jnp.zeros_like(k[:, :1])
    kz = jnp.concatenate([zero, zero, k], axis=1).reshape(g, PAIRS + 1, 2, h, h)
    even, odd = kz[:, :, 0], kz[:, :, 1]
    fill = jnp.zeros((g, PAIRS, h, LANES - 2 * h), F32)
    row0 = jnp.concatenate([even[:, 1:], odd[:, 1:], fill], axis=-1)
    row1 = jnp.concatenate([odd[:, :-1], even[:, 1:], fill], axis=-1)
    k2c = jnp.stack([row0, row1], axis=2)
    w2c = tj[:, ::-1].reshape(g, PAIRS, 2, h, 2 * p)
    c_cross = jnp.concatenate([-c_im, -c_re], axis=-1)
    p_imag = jnp.concatenate([pi, pi], axis=-1)[:, :, None, :]
    v2c = (c_conj[:, None] * p_same[:, 1:] + c_cross[:, None] * p_imag[:, 1:]).reshape(g, PAIRS, 2, h, 2 * p)
    al = jnp.stack([pr[:, CHUNK].reshape(nt, -1), pi[:, CHUNK].reshape(nt, -1)], axis=1)
    return k2c, w2c, v2c, al


def _spread_consts(h, p):
    gg = GROUPS_PER_TILE
    row_g = (np.arange(2 * gg * h) // h) % gg
    colk = np.arange(2 * gg * h)
    rep_k = np.zeros((LANES, 2 * gg * h), np.float32)
    rep_k[(colk // (gg * h)) * h + colk % h, colk] = 1.0
    mask_k = (row_g[:, None] == ((colk // h) % gg)[None, :]).astype(np.float32)
    cols = np.arange(2 * gg * p)
    rep_s = np.zeros((2 * p, 2 * gg * p), np.float32)
    rep_s[(cols // (gg * p)) * p + cols % p, cols] = 1.0
    mask_s = (row_g[:, None] == ((cols // p) % gg)[None, :]).astype(np.float32)
    return tuple(jnp.asarray(a, BF16) for a in (rep_k, mask_k, rep_s, mask_s))


def _ssm_spread(k2c, w2c, v2c, consts, exs):
    h = k2c.shape[3]
    nt = k2c.shape[0] // GROUPS_PER_TILE
    rep_k, mask_k, rep_s, mask_s = consts
    nk, ns = rep_k.shape[1], rep_s.shape[1]

    def body(k_ref, w_ref, v_ref, rk_ref, mk_ref, rs_ref, ms_ref, ko_ref, wo_ref, vo_ref):
        for q in range(PAIRS):
            for c_ref, o_ref, r_ref, m_ref in ((k_ref, ko_ref, rk_ref, mk_ref), (w_ref, wo_ref, rs_ref, ms_ref),
                                               (v_ref, vo_ref, rs_ref, ms_ref)):
                rows = jnp.concatenate(
                    [c_ref[gi, q, st] for st in range(2) for gi in range(GROUPS_PER_TILE)], axis=0)
                o_ref[q] = (_dot(rows.astype(BF16), r_ref[...]) * m_ref[...]).astype(BF16)

    compact = pl.BlockSpec((GROUPS_PER_TILE, PAIRS, 2, h, LANES), lambda t: (t, 0, 0, 0, 0))

    def tile(cols):
        return pl.BlockSpec((None, PAIRS, nk, cols), lambda t: (t, 0, 0, 0))

    def whole(a):
        return pl.BlockSpec(a.shape, lambda t: (0, 0))

    return _call_riding(
        body,
        exs,
        args=(k2c, w2c, v2c, rep_k, mask_k, rep_s, mask_s),
        scratch_shapes=[],
        grid=(nt,),
        in_specs=[compact, compact, compact, whole(rep_k), whole(mask_k), whole(rep_s), whole(mask_s)],
        out_specs=[tile(nk), tile(ns), tile(ns)],
        out_shape=[
            jax.ShapeDtypeStruct((nt, PAIRS, nk, nk), BF16),
            jax.ShapeDtypeStruct((nt, PAIRS, nk, ns), BF16),
            jax.ShapeDtypeStruct((nt, PAIRS, nk, ns), BF16),
        ],
        compiler_params=_params(1),
        name="ssm_spread",
    )


def _gather_blocks(full, mask, rep):
    return _dot_nt((full * mask).astype(BF16), rep)


def _in_proj(x, g, w4, tm, exs):
    s, d = x.shape
    nj, _, cw = w4.shape

    def body(x_ref, g_ref, w_ref, z_ref, h_ref):
        @pl.when(pl.program_id(1) == 0)
        def _():
            xv = x_ref[...]
            h_ref[...] = (xv * _rms(xv) * g_ref[...]).astype(BF16)

        z_ref[...] = _dot(h_ref[...], w_ref[...])

    return _call_riding(
        body,
        exs,
        scratch_shapes=[],
        args=(x, g, w4),
        grid=(s // tm, nj),
        in_specs=[
            pl.BlockSpec((tm, d), lambda i, j: (i, 0)),
            pl.BlockSpec((1, d), lambda i, j: (0, 0)),
            pl.BlockSpec((None, d, cw), lambda i, j: (j, 0, 0)),
        ],
        out_specs=[pl.BlockSpec((tm, cw), lambda i, j: (i, j)), pl.BlockSpec((tm, d), lambda i, j: (i, 0))],
        out_shape=[jax.ShapeDtypeStruct((s, nj * cw), F32), jax.ShapeDtypeStruct((s, d), BF16)],
        compiler_params=_params(2),
        name="in_proj",
    )


def _ssm_fwd(z, k2, w2, v2, al, exs):
    s = z.shape[0]
    nt = k2.shape[0]
    nc = s // CHUNK
    ns = w2.shape[-1]
    hs = ns // 2

    def body(u_ref, k_ref, w_ref, v_ref, al_ref, y_ref, xp_ref, us_ref, xloc):
        for q in range(PAIRS):
            us_ref[q, :, 0:LANES] = u_ref[pl.ds(2 * q, nc, stride=CHUNK), :].astype(BF16)
            us_ref[q, :, LANES : 2 * LANES] = u_ref[pl.ds(2 * q + 1, nc, stride=CHUNK), :].astype(BF16)
        acc = _dot(us_ref[0], w_ref[0])
        for q in range(1, PAIRS):
            acc = acc + _dot(us_ref[q], w_ref[q])
        xloc[...] = acc
        ar = al_ref[0:1, :]
        ai = al_ref[1:2, :]

        def step(c, carry):
            xr, xi = carry
            xp_ref[pl.ds(c, 1), 0:hs] = xr
            xp_ref[pl.ds(c, 1), hs:ns] = xi
            lr = xloc[pl.ds(c, 1), 0:hs]
            li = xloc[pl.ds(c, 1), hs:ns]
            return ar * xr - ai * xi + lr, ar * xi + ai * xr + li

        zero = jnp.zeros((1, hs), F32)
        lax.fori_loop(0, nc, step, (zero, zero))
        xpb = xp_ref[...].astype(BF16)
        for r in range(PAIRS):
            acc = _dot_nt(xpb, v_ref[r])
            for q in range(r + 1):
                acc = acc + _dot(us_ref[q], k_ref[r - q])
            y_ref[pl.ds(2 * r, nc, stride=CHUNK), :] = acc[:, 0:LANES]
            y_ref[pl.ds(2 * r + 1, nc, stride=CHUNK), :] = acc[:, LANES : 2 * LANES]

    return _call_riding(
        body,
        exs,
        args=(z, k2, w2, v2, al),
        grid=(nt,),
        in_specs=[
            pl.BlockSpec((s, LANES), lambda t: (0, t)),
            pl.BlockSpec((None, PAIRS, 2 * LANES, 2 * LANES), lambda t: (t, 0, 0, 0)),
            pl.BlockSpec((None, PAIRS, 2 * LANES, ns), lambda t: (t, 0, 0, 0)),
            pl.BlockSpec((None, PAIRS, 2 * LANES, ns), lambda t: (t, 0, 0, 0)),
            pl.BlockSpec((None, 2, hs), lambda t: (t, 0, 0)),
        ],
        out_specs=[
            pl.BlockSpec((s, LANES), lambda t: (0, t)),
            pl.BlockSpec((None, nc, ns), lambda t: (t, 0, 0)),
            pl.BlockSpec((None, PAIRS, nc, 2 * LANES), lambda t: (t, 0, 0, 0)),
        ],
        out_shape=[
            jax.ShapeDtypeStruct((s, nt * LANES), F32),
            jax.ShapeDtypeStruct((nt, nc, ns), F32),
            jax.ShapeDtypeStruct((nt, PAIRS, nc, 2 * LANES), BF16),
        ],
        scratch_shapes=[pltpu.VMEM((nc, ns), F32)],
        compiler_params=_params(1),
        name="ssm_fwd",
    )


def _glu_fwd(y_pre, wg, bg, go, d_model, tm):
    s, w = y_pre.shape

    def body(y_ref, wg_ref, bg_ref, go_ref, o_ref):
        yg = _gelu(y_ref[...])
        gate = _sigmoid(_dot(yg.astype(BF16), wg_ref[...]) + bg_ref[...])
        ys = yg * gate
        o_ref[...] = (ys * _rms(ys) * go_ref[...]).astype(BF16)

    return _call(
        body,
        grid=(s // tm,),
        in_specs=[
            pl.BlockSpec((tm, w), lambda i: (i, 0)),
            pl.BlockSpec((w, w), lambda i: (0, 0)),
            pl.BlockSpec((1, w), lambda i: (0, 0)),
            pl.BlockSpec((1, w), lambda i: (0, 0)),
        ],
        out_specs=pl.BlockSpec((tm, w), lambda i: (i, 0)),
        out_shape=jax.ShapeDtypeStruct((s, d_model), BF16),
        compiler_params=_params(1),
        name="glu_fwd",
    )(y_pre, wg, bg, go)


def _sgu_parts(zu, zv, lng, lnb, wt_ref, bias_ref):
    u, u_grad = _gelu_both(zu)
    v, v_grad = _gelu_both(zv)
    mu = jnp.mean(v, axis=-1, keepdims=True)
    vc = v - mu
    rstd = lax.rsqrt(jnp.mean(vc * vc, axis=-1, keepdims=True) + EPS)
    vhat = vc * rstd
    vb = (vhat * lng + lnb).astype(BF16)
    heads = wt_ref.shape[0]
    mix = jnp.concatenate(
        [_dot(wt_ref[h], vb[:, h * LANES : (h + 1) * LANES]) + bias_ref[h] for h in range(heads)], axis=1
    )
    return u, vhat, rstd, vb, mix, u_grad, v_grad


def _sgu_fwd(z, mixed, lng, lnb, wt, biasb, go, rb):
    s = z.shape[0]
    w = lng.shape[-1]
    heads = wt.shape[0]

    def body(zu_ref, zv_ref, m_any, lng_ref, lnb_ref, wt_ref, b_ref, go_ref, o_ref):
        del m_any
        for ck in range(rb // SGU_CHUNK):
            rows = slice(ck * SGU_CHUNK, (ck + 1) * SGU_CHUNK)
            u, _, _, _, mix, _, _ = _sgu_parts(zu_ref[rows, :], zv_ref[rows, :], lng_ref[...], lnb_ref[...], wt_ref, b_ref)
            y = u * mix
            o_ref[rows, :] = (y * _rms(y) * go_ref[...]).astype(BF16)

    return _call(
        body,
        grid=(s // rb,),
        in_specs=[
            pl.BlockSpec((rb, w), lambda i: (i, 1)),
            pl.BlockSpec((rb, w), lambda i: (i, 2)),
            pl.BlockSpec(memory_space=pl.ANY),
            pl.BlockSpec((1, w), lambda i: (0, 0)),
            pl.BlockSpec((1, w), lambda i: (0, 0)),
            pl.BlockSpec((heads, SGU_CHUNK, SGU_CHUNK), lambda i: (0, 0, 0)),
            pl.BlockSpec((heads, SGU_CHUNK, LANES), lambda i: (0, 0, 0)),
            pl.BlockSpec((1, w), lambda i: (0, 0)),
        ],
        out_specs=pl.BlockSpec((rb, w), lambda i: (i, 1)),
        out_shape=jax.ShapeDtypeStruct(mixed.shape, mixed.dtype),
        input_output_aliases={2: 0},
        compiler_params=_params(1),
        name="sgu_fwd",
    )(z, z, mixed, lng, lnb, wt, biasb, go)


def _out_proj(x, mixed, w_out, tm):
    s, d = x.shape

    def body(x_ref, m_ref, w_ref, o_ref):
        o_ref[...] = x_ref[...] + _dot(m_ref[...], w_ref[...])

    return _call(
        body,
        grid=(s // tm,),
        in_specs=[
            pl.BlockSpec((tm, d), lambda i: (i, 0)),
            pl.BlockSpec((tm, d), lambda i: (i, 0)),
            pl.BlockSpec((d, d), lambda i: (0, 0)),
        ],
        out_specs=pl.BlockSpec((tm, d), lambda i: (i, 0)),
        out_shape=jax.ShapeDtypeStruct((s, d), F32),
        compiler_params=_params(1),
        name="out_proj",
    )(x, mixed, w_out)


def _mlp_up(x1, g, w_up4, tm, tf, exs):
    s, d = x1.shape
    nj, _, cw = w_up4.shape
    per = cw // tf

    def body(x_ref, g_ref, wu_ref, up_ref, h_ref):
        @pl.when(pl.program_id(1) == 0)
        def _():
            xv = x_ref[...]
            h_ref[...] = (xv * _rms(xv) * g_ref[...]).astype(BF16)

        up_ref[...] = _dot(h_ref[...], wu_ref[...]).astype(BF16)

    return _call_riding(
        body,
        exs,
        scratch_shapes=[],
        args=(x1, g, w_up4),
        grid=(s // tm, nj * per),
        in_specs=[
            pl.BlockSpec((tm, d), lambda i, f: (i, 0)),
            pl.BlockSpec((1, d), lambda i, f: (0, 0)),
            pl.BlockSpec((None, d, tf), lambda i, f: (f // per, 0, f % per)),
        ],
        out_specs=[pl.BlockSpec((tm, tf), lambda i, f: (i, f)), pl.BlockSpec((tm, d), lambda i, f: (i, 0))],
        out_shape=[jax.ShapeDtypeStruct((s, nj * cw), BF16), jax.ShapeDtypeStruct((s, d), BF16)],
        compiler_params=_params(2),
        name="mlp_up",
    )


def _mlp_down_loss(x1, up, w_down, target, g, tm, tf):
    s, d = x1.shape
    ff = w_down.shape[0]
    nf = ff // tf

    def body(x_ref, up_ref, wd_ref, t_ref, g_ref, loss_ref, dx_ref, dxb_ref, dg_ref, acc):
        i = pl.program_id(0)
        f = pl.program_id(1)

        @pl.when(f == 0)
        def _():
            acc[...] = x_ref[...]

        @pl.when(jnp.logical_and(i == 0, f == 0))
        def _():
            loss_ref[...] = jnp.zeros_like(loss_ref)
            dg_ref[...] = jnp.zeros_like(dg_ref)

        a = jnp.maximum(up_ref[...].astype(F32), 0.0)
        acc[...] += _dot((a * a).astype(BF16), wd_ref[...])

        @pl.when(f == nf - 1)
        def _():
            xv = acc[...]
            r = _rms(xv)
            gv = g_ref[...]
            diff = xv * r * gv - t_ref[...]
            loss_ref[...] += 0.5 * jnp.sum(jnp.mean(diff * diff, axis=-1, keepdims=True), axis=0, keepdims=True)
            dx, dgt = _rms_bwd(diff * (1.0 / d), xv, r, gv)
            dx_ref[...] = dx
            dxb_ref[...] = dx.astype(BF16)
            dg_ref[...] += jnp.sum(dgt, axis=0, keepdims=True)

    row = pl.BlockSpec((tm, d), lambda i, f: (i, 0))
    vec = pl.BlockSpec((1, d), lambda i, f: (0, 0))
    return _call(
        body,
        grid=(s // tm, nf),
        in_specs=[
            row,
            pl.BlockSpec((tm, tf), lambda i, f: (i, f)),
            pl.BlockSpec((tf, d), lambda i, f: (f, 0)),
            row,
            vec,
        ],
        out_specs=[pl.BlockSpec((1, 1), lambda i, f: (0, 0)), row, row, vec],
        out_shape=[
            jax.ShapeDtypeStruct((1, 1), F32),
            jax.ShapeDtypeStruct((s, d), F32),
            jax.ShapeDtypeStruct((s, d), BF16),
            jax.ShapeDtypeStruct((1, d), F32),
        ],
        scratch_shapes=[pltpu.VMEM((tm, d), F32)],
        compiler_params=_params(2),
        name="mlp_down_loss",
    )(x1, up, w_down, target, g)


def _mlp_bwd(dx2b, up, w_up4, w_down, tm, tf):
    s, d = dx2b.shape
    ff = w_down.shape[0]
    cw = w_up4.shape[-1]
    per = cw // tf

    def body(dx_ref, up_ref, wu_ref, wd_ref, dup_ref, dh_ref):
        @pl.when(pl.program_id(1) == 0)
        def _():
            dh_ref[...] = jnp.zeros_like(dh_ref)

        dact = _dot_nt(dx_ref[...], wd_ref[...])
        dupb = (dact * (2.0 * jnp.maximum(up_ref[...].astype(F32), 0.0))).astype(BF16)
        dup_ref[...] = dupb
        dh_ref[...] += _dot_nt(dupb, wu_ref[...])

    return _call(
        body,
        grid=(s // tm, ff // tf),
        in_specs=[
            pl.BlockSpec((tm, d), lambda i, f: (i, 0)),
            pl.BlockSpec((tm, tf), lambda i, f: (i, f)),
            pl.BlockSpec((None, d, tf), lambda i, f: (f // per, 0, f % per)),
            pl.BlockSpec((tf, d), lambda i, f: (f, 0)),
        ],
        out_specs=[pl.BlockSpec((tm, tf), lambda i, f: (i, f)), pl.BlockSpec((tm, d), lambda i, f: (i, 0))],
        out_shape=[jax.ShapeDtypeStruct((s, ff), BF16), jax.ShapeDtypeStruct((s, d), F32)],
        compiler_params=_params(2),
        name="mlp_bwd",
    )(dx2b, up, w_up4, w_down)


def _norm_bwd(dh, x, dres, g, tm, name):
    s, d = x.shape

    def body(dh_ref, x_ref, dr_ref, g_ref, dx_ref, dxb_ref, dg_ref):
        @pl.when(pl.program_id(0) == 0)
        def _():
            dg_ref[...] = jnp.zeros_like(dg_ref)

        xv = x_ref[...]
        dx, dgt = _rms_bwd(dh_ref[...], xv, _rms(xv), g_ref[...])
        tot = dr_ref[...] + dx
        dx_ref[...] = tot
        dxb_ref[...] = tot.astype(BF16)
        dg_ref[...] += jnp.sum(dgt, axis=0, keepdims=True)

    row = pl.BlockSpec((tm, d), lambda i: (i, 0))
    vec = pl.BlockSpec((1, d), lambda i: (0, 0))
    return _call(
        body,
        grid=(s // tm,),
        in_specs=[row, row, row, vec],
        out_specs=[row, row, vec],
        out_shape=[
            jax.ShapeDtypeStruct((s, d), F32),
            jax.ShapeDtypeStruct((s, d), BF16),
            jax.ShapeDtypeStruct((1, d), F32),
        ],
        compiler_params=_params(1),
        name=name,
    )(dh, x, dres, g)


def _grad_w(a, b, out_dims, index_map, tm, tn, tk, relu2, name, exs=(), chunks=1):
    t, m = a.shape
    n = b.shape[1]
    nk = t // tk
    cw = tn // chunks

    def body(a_ref, b_ref, o_ref, acc):
        k = pl.program_id(2)

        @pl.when(k == 0)
        def _():
            acc[...] = jnp.zeros_like(acc)

        av = a_ref[...]
        if relu2:
            r = jnp.maximum(av.astype(F32), 0.0)
            av = (r * r).astype(BF16)
        acc[...] += _dot_tn(av, b_ref[...])

        @pl.when(k == nk - 1)
        def _():
            for c in range(chunks):
                o_ref[c] = acc[:, c * cw : (c + 1) * cw].astype(BF16)

    spec = dict(
        grid=(m // tm, n // tn, nk),
        in_specs=[pl.BlockSpec((tk, tm), lambda i, j, k: (k, i)), pl.BlockSpec((tk, tn), lambda i, j, k: (k, j))],
        out_specs=[pl.BlockSpec((chunks, None, tm, cw), lambda i, j, k: index_map(i, j))],
        out_shape=[jax.ShapeDtypeStruct(out_dims, BF16)],
        scratch_shapes=[pltpu.VMEM((tm, tn), F32)],
        compiler_params=_params(3),
        name=name,
    )
    (g,), landed = _call_riding(body, exs, args=(a, b), **spec)
    return (g, *landed) if exs else g


def _out_proj_bwd(dx1b, w_out, tm):
    s, d = dx1b.shape

    def body(dx_ref, w_ref, o_ref):
        o_ref[...] = _dot_nt(dx_ref[...], w_ref[...])

    return _call(
        body,
        grid=(s // tm,),
        in_specs=[pl.BlockSpec((tm, d), lambda i: (i, 0)), pl.BlockSpec((d, d), lambda i: (0, 0))],
        out_specs=pl.BlockSpec((tm, d), lambda i: (i, 0)),
        out_shape=jax.ShapeDtypeStruct((s, d), F32),
        compiler_params=_params(1),
        name="out_proj_bwd",
    )(dx1b, w_out)


def _glu_bwd(y_pre, dmix, wg, bg, go, tm, exs):
    s, w = y_pre.shape
    n = s // tm

    def body(y_ref, dm_ref, wg_ref, bg_ref, go_ref, dy_ref, dwg_ref, dbg_ref, dgo_ref, acc):
        i = pl.program_id(0)

        @pl.when(i == 0)
        def _():
            acc[...] = jnp.zeros_like(acc)
            dbg_ref[...] = jnp.zeros_like(dbg_ref)
            dgo_ref[...] = jnp.zeros_like(dgo_ref)

        yp = y_ref[...]
        yg, yg_grad = _gelu_both(yp)
        ygb = yg.astype(BF16)
        gate = _sigmoid(_dot(ygb, wg_ref[...]) + bg_ref[...])
        ys = yg * gate
        dys, dgt = _rms_bwd(dm_ref[...], ys, _rms(ys), go_ref[...])
        dgo_ref[...] += jnp.sum(dgt, axis=0, keepdims=True)
        dpre = dys * yg * gate * (1.0 - gate)
        dbg_ref[...] += jnp.sum(dpre, axis=0, keepdims=True)
        dpb = dpre.astype(BF16)
        dyg = dys * gate + _dot_nt(dpb, wg_ref[...])
        dy_ref[...] = dyg * yg_grad
        acc[...] += _dot_tn(ygb, dpb)

        @pl.when(i == n - 1)
        def _():
            dwg_ref[...] = acc[...].astype(BF16)

    return _call_riding(
        body,
        exs,
        args=(y_pre, dmix, wg, bg, go),
        grid=(n,),
        in_specs=[
            pl.BlockSpec((tm, w), lambda i: (i, 0)),
            pl.BlockSpec((tm, w), lambda i: (i, 0)),
            pl.BlockSpec((w, w), lambda i: (0, 0)),
            pl.BlockSpec((1, w), lambda i: (0, 0)),
            pl.BlockSpec((1, w), lambda i: (0, 0)),
        ],
        out_specs=[
            pl.BlockSpec((tm, w), lambda i: (i, 0)),
            pl.BlockSpec((w, w), lambda i: (0, 0)),
            pl.BlockSpec((1, w), lambda i: (0, 0)),
            pl.BlockSpec((1, w), lambda i: (0, 0)),
        ],
        out_shape=[
            jax.ShapeDtypeStruct((s, w), F32),
            jax.ShapeDtypeStruct((w, w), BF16),
            jax.ShapeDtypeStruct((1, w), F32),
            jax.ShapeDtypeStruct((1, w), F32),
        ],
        scratch_shapes=[pltpu.VMEM((w, w), F32)],
        compiler_params=_params(1),
        name="glu_bwd",
    )


def _ssm_bwd_state(dy, v2, al, xprev):
    s = dy.shape[0]
    nt, nc, ns = xprev.shape
    hs = ns // 2

    def body(dy_ref, v_ref, al_ref, xp_ref, dys_ref, g_ref, dal_ref, dxp, gs):
        for q in range(PAIRS):
            dys_ref[q, :, 0:LANES] = dy_ref[pl.ds(2 * q, nc, stride=CHUNK), :].astype(BF16)
            dys_ref[q, :, LANES : 2 * LANES] = dy_ref[pl.ds(2 * q + 1, nc, stride=CHUNK), :].astype(BF16)
        acc = _dot(dys_ref[0], v_ref[0])
        for r in range(1, PAIRS):
            acc = acc + _dot(dys_ref[r], v_ref[r])
        dxp[...] = acc
        ar = al_ref[0:1, :]
        ai = al_ref[1:2, :]
        zero = jnp.zeros((1, hs), F32)
        gs[pl.ds(nc - 1, 1), 0:hs] = zero
        gs[pl.ds(nc - 1, 1), hs:ns] = zero

        def step(n, carry):
            gr, gi = carry
            c = nc - 2 - n
            nr = dxp[pl.ds(c + 1, 1), 0:hs] + ar * gr + ai * gi
            ni = dxp[pl.ds(c + 1, 1), hs:ns] + ar * gi - ai * gr
            gs[pl.ds(c, 1), 0:hs] = nr
            gs[pl.ds(c, 1), hs:ns] = ni
            return nr, ni

        lax.fori_loop(0, nc - 1, step, (zero, zero))
        gv = gs[...]
        xv = xp_ref[...]
        gr, gi = gv[:, 0:hs], gv[:, hs:ns]
        xr, xi = xv[:, 0:hs], xv[:, hs:ns]
        dal_ref[0:1, :] = jnp.sum(gr * xr + gi * xi, axis=0, keepdims=True)
        dal_ref[1:2, :] = jnp.sum(gi * xr - gr * xi, axis=0, keepdims=True)
        g_ref[...] = gv.astype(BF16)

    return _call(
        body,
        grid=(nt,),
        in_specs=[
            pl.BlockSpec((s, LANES), lambda t: (0, t)),
            pl.BlockSpec((None, PAIRS, 2 * LANES, ns), lambda t: (t, 0, 0, 0)),
            pl.BlockSpec((None, 2, hs), lambda t: (t, 0, 0)),
            pl.BlockSpec((None, nc, ns), lambda t: (t, 0, 0)),
        ],
        out_specs=[
            pl.BlockSpec((None, PAIRS, nc, 2 * LANES), lambda t: (t, 0, 0, 0)),
            pl.BlockSpec((None, nc, ns), lambda t: (t, 0, 0)),
            pl.BlockSpec((None, 2, hs), lambda t: (t, 0, 0)),
        ],
        out_shape=[
            jax.ShapeDtypeStruct((nt, PAIRS, nc, 2 * LANES), BF16),
            jax.ShapeDtypeStruct((nt, nc, ns), BF16),
            jax.ShapeDtypeStruct((nt, 2, hs), F32),
        ],
        scratch_shapes=[pltpu.VMEM((nc, ns), F32), pltpu.VMEM((nc, ns), F32)],
        compiler_params=_params(1),
        name="ssm_bwd_state",
    )(dy, v2, al, xprev)


def _ssm_bwd_main(us, dys, k2, w2, xprev, gst, consts, s, width, exs):
    nt, _, nc, _ = us.shape
    ns = xprev.shape[-1]
    rep_k, mask_k, rep_s, mask_s = consts
    nk = 2 * LANES
    gg = GROUPS_PER_TILE
    rows_of = [(gi, st, slice((st * gg + gi) * SSM_GROUP, (st * gg + gi + 1) * SSM_GROUP))
               for st in range(2) for gi in range(gg)]

    def body(us_ref, dys_ref, k_ref, w_ref, xp_ref, g_ref, rk_ref, mk_ref, rs_ref, ms_ref,
             du_ref, dk_ref, dw_ref, dv_ref, acc, du2, duf, dkf):
        q = pl.program_id(1)

        @pl.when(q == 0)
        def _():
            dkf[...] = jnp.zeros_like(dkf)

        usq = us_ref[q]
        gb = g_ref[...]
        dw = _gather_blocks(_dot_tn(usq, gb), ms_ref[...], rs_ref[...])
        dv = _gather_blocks(_dot_tn(dys_ref[q], xp_ref[...].astype(BF16)), ms_ref[...], rs_ref[...])
        for gi, st, rows in rows_of:
            dw_ref[gi, st] = dw[rows]
            dv_ref[gi, st] = dv[rows]
        acc[...] = _dot_nt(gb, w_ref[...])
        for m in range(PAIRS):

            @pl.when(q + m < PAIRS)
            def _():
                dyr = dys_ref[jnp.minimum(q + m, PAIRS - 1)]
                acc[...] += _dot_nt(dyr, k_ref[m])
                dkf[m] += _dot_tn(usq, dyr)

        du2[q] = acc[...]

        @pl.when(q == PAIRS - 1)
        def _():
            for p in range(PAIRS):
                duf[pl.ds(2 * p, nc, stride=CHUNK), :] = du2[p, :, 0:LANES]
                duf[pl.ds(2 * p + 1, nc, stride=CHUNK), :] = du2[p, :, LANES : 2 * LANES]
                dk = _gather_blocks(dkf[p], mk_ref[...], rk_ref[...])
                for gi, st, rows in rows_of:
                    dk_ref[gi, p, st] = dk[rows]
            du_ref[...] = duf[...].astype(BF16)

    def per_tile(rows, cols):
        return pl.BlockSpec((None, PAIRS, rows, cols), lambda t, q: (t, 0, 0, 0))

    def per_pair(rows, cols):
        return pl.BlockSpec((None, None, rows, cols), lambda t, q: (t, q, 0, 0))

    def whole(a):
        return pl.BlockSpec(a.shape, lambda t, q: (0, 0))

    return _call_riding(
        body,
        exs,
        args=(us, dys, k2, w2, xprev, gst, rep_k, mask_k, rep_s, mask_s),
        grid=(nt, PAIRS),
        in_specs=[
            per_tile(nc, nk),
            per_tile(nc, nk),
            per_tile(nk, nk),
            per_pair(nk, ns),
            pl.BlockSpec((None, nc, ns), lambda t, q: (t, 0, 0)),
            pl.BlockSpec((None, nc, ns), lambda t, q: (t, 0, 0)),
            whole(rep_k),
            whole(mask_k),
            whole(rep_s),
            whole(mask_s),
        ],
        out_specs=[
            pl.BlockSpec((s, LANES), lambda t, q: (0, t)),
            pl.BlockSpec((gg, PAIRS, 2, SSM_GROUP, LANES), lambda t, q: (t, 0, 0, 0, 0)),
            pl.BlockSpec((gg, None, 2, SSM_GROUP, LANES), lambda t, q: (t, q, 0, 0, 0)),
            pl.BlockSpec((gg, None, 2, SSM_GROUP, LANES), lambda t, q: (t, q, 0, 0, 0)),
        ],
        out_shape=[
            jax.ShapeDtypeStruct((s, width), BF16),
            jax.ShapeDtypeStruct((nt * gg, PAIRS, 2, SSM_GROUP, LANES), F32),
            jax.ShapeDtypeStruct((nt * gg, PAIRS, 2, SSM_GROUP, LANES), F32),
            jax.ShapeDtypeStruct((nt * gg, PAIRS, 2, SSM_GROUP, LANES), F32),
        ],
        scratch_shapes=[
            pltpu.VMEM((nc, nk), F32),
            pltpu.VMEM((PAIRS, nc, nk), F32),
            pltpu.VMEM((s, LANES), F32),
            pltpu.VMEM((PAIRS, nk, nk), F32),
        ],
        compiler_params=_params(2),
        name="ssm_bwd_main",
    )


def _sgu_bwd(z, dmix, dz, lng, lnb, wt, wtt, biasb, go, rb, exs):
    s = z.shape[0]
    w = lng.shape[-1]
    heads = wt.shape[0]
    ncol = (LANES, LANES)

    def body(zu_ref, zv_ref, dm_ref, dz_any, lng_ref, lnb_ref, wt_ref, wtt_ref, b_ref, go_ref,
             dz_ref, dw_ref, db_ref, dlg_ref, dlb_ref, dgo_ref, dzv):
        del dz_any
        i = pl.program_id(0)
        p = pl.program_id(1)

        @pl.when(jnp.logical_and(i == 0, p == 0))
        def _():
            dw_ref[...] = jnp.zeros_like(dw_ref)
            db_ref[...] = jnp.zeros_like(db_ref)
            dlg_ref[...] = jnp.zeros_like(dlg_ref)
            dlb_ref[...] = jnp.zeros_like(dlb_ref)
            dgo_ref[...] = jnp.zeros_like(dgo_ref)

        @pl.when(p == 0)
        def _():
            lng_v = lng_ref[...]
            for ck in range(rb // SGU_CHUNK):
                rows = slice(ck * SGU_CHUNK, (ck + 1) * SGU_CHUNK)
                zu = zu_ref[rows, :]
                zv = zv_ref[rows, :]
                u, vhat, rstd, vb, mix, u_grad, v_grad = _sgu_parts(zu, zv, lng_v, lnb_ref[...], wt_ref, b_ref)
                y = u * mix
                dy, dgt = _rms_bwd(dm_ref[rows, :], y, _rms(y), go_ref[...])
                dgo_ref[...] += jnp.sum(dgt, axis=0, keepdims=True)
                du = dy * mix
                dmx = dy * u
                dmb = dmx.astype(BF16)
                dvl = []
                for h in range(heads):
                    cols = slice(h * LANES, (h + 1) * LANES)
                    db_ref[h] += jnp.broadcast_to(jnp.sum(dmx[:, cols], axis=-1, keepdims=True), ncol)
                    dw_ref[h] += _dot_nt(dmb[:, cols], vb[:, cols])
                    dvl.append(_dot(wtt_ref[h], dmb[:, cols]))
                dvln = jnp.concatenate(dvl, axis=1)
                dlg_ref[...] += jnp.sum(dvln * vhat, axis=0, keepdims=True)
                dlb_ref[...] += jnp.sum(dvln, axis=0, keepdims=True)
                dvh = dvln * lng_v
                dv = rstd * (
                    dvh
                    - jnp.mean(dvh, axis=-1, keepdims=True)
                    - vhat * jnp.mean(dvh * vhat, axis=-1, keepdims=True)
                )
                dz_ref[rows, :] = (du * u_grad).astype(BF16)
                dzv[rows, :] = (dv * v_grad).astype(BF16)

        @pl.when(p == 1)
        def _():
            dz_ref[...] = dzv[...]

    return _call_riding(
        body,
        exs,
        args=(z, z, dmix, dz, lng, lnb, wt, wtt, biasb, go),
        grid=(s // rb, 2),
        in_specs=[
            pl.BlockSpec((rb, w), lambda i, p: (i, 1)),
            pl.BlockSpec((rb, w), lambda i, p: (i, 2)),
            pl.BlockSpec((rb, w), lambda i, p: (i, 1)),
            pl.BlockSpec(memory_space=pl.ANY),
            pl.BlockSpec((1, w), lambda i, p: (0, 0)),
            pl.BlockSpec((1, w), lambda i, p: (0, 0)),
            pl.BlockSpec((heads, SGU_CHUNK, SGU_CHUNK), lambda i, p: (0, 0, 0)),
            pl.BlockSpec((heads, SGU_CHUNK, SGU_CHUNK), lambda i, p: (0, 0, 0)),
            pl.BlockSpec((heads, SGU_CHUNK, LANES), lambda i, p: (0, 0, 0)),
            pl.BlockSpec((1, w), lambda i, p: (0, 0)),
        ],
        out_specs=[
            pl.BlockSpec((rb, w), lambda i, p: (i, 1 + p)),
            pl.BlockSpec((heads, SGU_CHUNK, SGU_CHUNK), lambda i, p: (0, 0, 0)),
            pl.BlockSpec((heads, SGU_CHUNK, LANES), lambda i, p: (0, 0, 0)),
            pl.BlockSpec((1, w), lambda i, p: (0, 0)),
            pl.BlockSpec((1, w), lambda i, p: (0, 0)),
            pl.BlockSpec((1, w), lambda i, p: (0, 0)),
        ],
        out_shape=[
            jax.ShapeDtypeStruct(dz.shape, dz.dtype),
            jax.ShapeDtypeStruct((heads, SGU_CHUNK, SGU_CHUNK), F32),
            jax.ShapeDtypeStruct((heads, SGU_CHUNK, LANES), F32),
            jax.ShapeDtypeStruct((1, w), F32),
            jax.ShapeDtypeStruct((1, w), F32),
            jax.ShapeDtypeStruct((1, w), F32),
        ],
        scratch_shapes=[pltpu.VMEM((rb, w), BF16)],
        input_output_aliases={3: 0},
        compiler_params=_params(2),
        name="sgu_bwd",
    )


def _in_proj_bwd(dz, w4, x, dres, g, tm, exs, tiles, into, name):
    s, d = x.shape
    nj, _, cw = w4.shape
    first, stop = tiles

    def body(dz_ref, w_ref, x_ref, dr_ref, g_ref, *rest):
        dx_ref, dg_ref, acc = rest[-3:]
        i = pl.program_id(0)
        j = pl.program_id(1)

        @pl.when(j == 0)
        def _():
            acc[...] = jnp.zeros_like(acc)

        @pl.when(jnp.logical_and(i == 0, j == 0))
        def _():
            dg_ref[...] = jnp.zeros_like(dg_ref)

        acc[...] += _dot_nt(dz_ref[...], w_ref[...])

        @pl.when(j == nj - 1)
        def _():
            xv = x_ref[...]
            dx, dgt = _rms_bwd(acc[...], xv, _rms(xv), g_ref[...])
            dx_ref[...] = dr_ref[...] + dx
            dg_ref[...] += jnp.sum(dgt, axis=0, keepdims=True)

    row = pl.BlockSpec((tm, d), lambda i, j: (i + first, 0))
    vec = pl.BlockSpec((1, d), lambda i, j: (0, 0))
    kept = [] if into is None else [into]
    return _call_riding(
        body,
        exs,
        args=(dz, w4, x, dres, g, *kept),
        scratch_shapes=[pltpu.VMEM((tm, d), F32)],
        grid=(stop - first, nj),
        in_specs=[
            pl.BlockSpec((tm, cw), lambda i, j: (i + first, j)),
            pl.BlockSpec((None, d, cw), lambda i, j: (j, 0, 0)),
            row,
            row,
            vec,
        ]
        + [_ANY] * len(kept),
        out_specs=[row, vec],
        out_shape=[jax.ShapeDtypeStruct((s, d), F32), jax.ShapeDtypeStruct((1, d), F32)],
        input_output_aliases={5: 0} if kept else {},
        compiler_params=_params(2),
        name=name,
    )


_ANY = pl.BlockSpec(memory_space=pl.ANY)


def _position():
    x, y, c = lax.axis_index("x"), lax.axis_index("y"), lax.axis_index("c")
    return x, y, c, [(1 - x, y), (x, 1 - y), (1 - x, 1 - y)]


def _remote(src, dst, send_sems, recv_sems, k, to):
    return pltpu.make_async_remote_copy(
        src_ref=src, dst_ref=dst, send_sem=send_sems.at[k], recv_sem=recv_sems.at[k], device_id=to, device_id_type=MESH
    )


class _Riding:
    def __init__(self, srcs, out_shapes, n_sems, start, finish, continues=False):
        self.srcs, self.out_shapes, self.n_sems, self.start, self.finish = srcs, out_shapes, n_sems, start, finish
        self.continues = continues


def _call_riding(body, exs, *, grid, in_specs, out_specs, out_shape, scratch_shapes, args, **kw):
    n_in, n_out, n_scr = len(in_specs), len(out_specs), len(scratch_shapes)
    spec = dict(grid=grid, in_specs=list(in_specs), out_specs=list(out_specs), out_shape=list(out_shape))
    if not exs:
        return _call(body, scratch_shapes=list(scratch_shapes), **spec, **kw)(*args), []
    srcs = [a for ex in exs for a in ex.srcs]
    lands = [a for ex in exs for a in ex.out_shapes]
    xi, xo = len(srcs), len(lands)

    def fused(*refs):
        cin, xin = refs[:n_in], refs[n_in : n_in + xi]
        o = n_in + xi
        cout, xout = refs[o : o + n_out], refs[o + n_out : o + n_out + xo]
        scr = refs[o + n_out + xo :]
        ids = [pl.program_id(a) for a in range(len(grid))]
        first = functools.reduce(jnp.logical_and, [i == 0 for i in ids])
        last = functools.reduce(jnp.logical_and, [i == n - 1 for i, n in zip(ids, grid)])

        def each(half):
            si = so = 0
            for e, ex in enumerate(exs):
                ni, no = len(ex.srcs), len(ex.out_shapes)
                sems = scr[n_scr + 2 * e : n_scr + 2 * e + 2]
                getattr(ex, half)(xin[si : si + ni], xout[so : so + no], *sems)
                si, so = si + ni, so + no

        @pl.when(first)
        def _():
            each("start")

        body(*cin, *cout, *scr[:n_scr])

        @pl.when(last)
        def _():
            each("finish")

    sems = [pltpu.SemaphoreType.DMA((ex.n_sems,)) for ex in exs for _ in range(2)]
    aliases = dict(kw.pop("input_output_aliases", {}))
    si = so = 0
    for ex in exs:
        if ex.continues:
            aliases[n_in + si + len(ex.srcs) - 1] = n_out + so
        si, so = si + len(ex.srcs), so + len(ex.out_shapes)
    kw["input_output_aliases"] = aliases
    spec["in_specs"] += [_ANY] * xi
    spec["out_specs"] += [_ANY] * xo
    spec["out_shape"] += lands
    outs = _call(fused, scratch_shapes=list(scratch_shapes) + sems, **spec, **kw)(*args, *srcs)
    return outs[:n_out], outs[n_out:]


def _gather_one(shard, peers=(0, 1, 2), into=None):
    def own_copies(ins, outs, send, recv):
        x, y, c, chips = _position()
        me = 2 * x + y
        sib = (x, y, 1 - c)
        cps = [] if into is not None else [_remote(ins[0].at[c], outs[0].at[me, c], send, recv, 0, sib)]
        cps += [_remote(ins[0].at[c], outs[0].at[me, c], send, recv, 1 + k, (*chips[k], c)) for k in peers]
        return cps, (c, me, sib, chips)

    def start(ins, outs, send, recv):
        for cp in own_copies(ins, outs, send, recv)[0]:
            cp.start()

    def finish(ins, outs, send, recv):
        cps, (c, me, sib, chips) = own_copies(ins, outs, send, recv)
        for k in peers:
            px, py = chips[k]
            landed = outs[0].at[2 * px + py, c]
            _remote(ins[0].at[c], landed, send, recv, 1 + k, (px, py, c)).wait_recv()
            cps.append(_remote(landed, landed, send, recv, 4 + k, sib))
            cps[-1].start()
        if into is None:
            _remote(ins[0].at[1 - c], outs[0].at[me, 1 - c], send, recv, 0, sib).wait_recv()
        for k in peers:
            px, py = chips[k]
            theirs = outs[0].at[2 * px + py, 1 - c]
            _remote(theirs, theirs, send, recv, 4 + k, sib).wait_recv()
        for cp in cps:
            cp.wait_send()

    land = jax.ShapeDtypeStruct((N_CHIPS,) + shard.shape, shard.dtype)
    return _Riding([shard] + ([] if into is None else [into]), [land], 7, start, finish, continues=into is not None)


def _to_owner_one(p, peers=(0, 1, 2), into=None):
    def copies(ins, outs, send, recv):
        x, y, c, chips = _position()
        return [_remote(ins[0].at[2 * chips[k][0] + chips[k][1]], outs[0].at[k], send, recv, k, (*chips[k], c)) for k in peers]

    def start(ins, outs, send, recv):
        for cp in copies(ins, outs, send, recv):
            cp.start()

    def finish(ins, outs, send, recv):
        cps = copies(ins, outs, send, recv)
        for cp in cps:
            cp.wait_recv()
        for cp in cps:
            cp.wait_send()

    land = jax.ShapeDtypeStruct((3,) + p.shape[1:], p.dtype)
    return _Riding([p] + ([] if into is None else [into]), [land], 3, start, finish, continues=into is not None)


def _to_sibling_one(g):
    def copies(ins, outs, send, recv):
        x, y, c, _ = _position()
        return [_remote(ins[0].at[j, 1 - c], outs[0].at[j], send, recv, j, (x, y, 1 - c)) for j in range(N_CHIPS)]

    def start(ins, outs, send, recv):
        for cp in copies(ins, outs, send, recv):
            cp.start()

    def finish(ins, outs, send, recv):
        cps = copies(ins, outs, send, recv)
        for cp in cps:
            cp.wait_recv()
        for cp in cps:
            cp.wait_send()

    return _Riding([g], [jax.ShapeDtypeStruct((g.shape[0],) + g.shape[2:], g.dtype)], N_CHIPS, start, finish)


def _gather_chips(arrs, name):
    n = len(arrs)

    def body(*refs):
        ins, outs = refs[:n], refs[n : 2 * n]
        send_sems, recv_sems, local_sems = refs[2 * n :]
        x, y, c, chips = _position()
        me = 2 * x + y
        local = [pltpu.make_async_copy(ins[i], outs[i].at[me], local_sems.at[i]) for i in range(n)]
        for cp in local:
            cp.start()
        sends = []
        for k, (px, py) in enumerate(chips):
            for i in range(n):
                sends.append(_remote(ins[i], outs[i].at[me], send_sems, recv_sems, k * n + i, (px, py, c)))
                sends[-1].start()
        for k, (px, py) in enumerate(chips):
            for i in range(n):
                _remote(ins[i], outs[i].at[2 * px + py], send_sems, recv_sems, k * n + i, (px, py, c)).wait_recv()
        for cp in sends:
            cp.wait_send()
        for cp in local:
            cp.wait()

    return _call(
        body,
        in_specs=[_ANY] * n,
        out_specs=[_ANY] * n,
        out_shape=[jax.ShapeDtypeStruct((N_CHIPS,) + a.shape, a.dtype) for a in arrs],
        scratch_shapes=[
            pltpu.SemaphoreType.DMA((3 * n,)),
            pltpu.SemaphoreType.DMA((3 * n,)),
            pltpu.SemaphoreType.DMA((n,)),
        ],
        name=name,
    )(*arrs)


def _pair_exchange(gs, name):
    n = len(gs)

    def body(*refs):
        ins, outs = refs[:n], refs[n : 2 * n]
        send_sems, recv_sems = refs[2 * n :]
        x, y, c, _ = _position()
        sib = (x, y, 1 - c)
        cps = []
        for i in range(n):
            for j in range(N_CHIPS):
                cps.append(_remote(ins[i].at[j, 1 - c], outs[i].at[j], send_sems, recv_sems, i * N_CHIPS + j, sib))
                cps[-1].start()
        for cp in cps:
            cp.wait_recv()
        for cp in cps:
            cp.wait_send()

    return _call(
        body,
        in_specs=[_ANY] * n,
        out_specs=[_ANY] * n,
        out_shape=[jax.ShapeDtypeStruct((g.shape[0],) + g.shape[2:], g.dtype) for g in gs],
        scratch_shapes=[pltpu.SemaphoreType.DMA((n * N_CHIPS,)), pltpu.SemaphoreType.DMA((n * N_CHIPS,))],
        name=name,
    )(*gs)


def _place_own(shard, gathered, pos, name):
    _, hr, cols = shard.shape
    tr = _row_tile(hr, 512)

    def body(pos_ref, s_ref, g_any, o_ref):
        del pos_ref, g_any
        o_ref[...] = s_ref[...]

    return _call(
        body,
        grid_spec=pltpu.PrefetchScalarGridSpec(
            num_scalar_prefetch=1,
            grid=(hr // tr,),
            in_specs=[pl.BlockSpec((None, tr, cols), lambda i, p: (p[1], i, 0)), _ANY],
            out_specs=pl.BlockSpec((None, None, tr, cols), lambda i, p: (p[0], p[1], i, 0)),
        ),
        out_shape=jax.ShapeDtypeStruct(gathered.shape, gathered.dtype),
        input_output_aliases={2: 0},
        compiler_params=_params(1),
        name=name,
    )(pos, shard, gathered)


def _chip_exchange(ps, name):
    n = len(ps)

    def body(*refs):
        ins, outs = refs[:n], refs[n : 2 * n]
        send_sems, recv_sems = refs[2 * n :]
        x, y, c, chips = _position()
        sends = []
        for k, (px, py) in enumerate(chips):
            for i in range(n):
                sends.append(_remote(ins[i].at[2 * px + py], outs[i].at[k], send_sems, recv_sems, k * n + i, (px, py, c)))
                sends[-1].start()
        for cp in sends:
            cp.wait_recv()
        for cp in sends:
            cp.wait_send()

    return _call(
        body,
        in_specs=[_ANY] * n,
        out_specs=[_ANY] * n,
        out_shape=[jax.ShapeDtypeStruct((3,) + p.shape[1:], p.dtype) for p in ps],
        scratch_shapes=[pltpu.SemaphoreType.DMA((3 * n,)), pltpu.SemaphoreType.DMA((3 * n,))],
        name=name,
    )(*ps)


def _pair_share(fs, name):
    n = len(fs)

    def body(*refs):
        ins, outs = refs[:n], refs[n : 2 * n]
        send_sems, recv_sems = refs[2 * n :]
        x, y, c, _ = _position()
        sib = (x, y, 1 - c)
        sends = [_remote(ins[i].at[c], outs[i].at[c], send_sems, recv_sems, i, sib) for i in range(n)]
        for cp in sends:
            cp.start()
        for i in range(n):
            _remote(ins[i].at[1 - c], outs[i].at[1 - c], send_sems, recv_sems, i, sib).wait_recv()
        for cp in sends:
            cp.wait_send()

    return _call(
        body,
        in_specs=[_ANY] * n,
        out_specs=[_ANY] * n,
        out_shape=[jax.ShapeDtypeStruct(f.shape, f.dtype) for f in fs],
        input_output_aliases={i: i for i in range(n)},
        scratch_shapes=[pltpu.SemaphoreType.DMA((n,)), pltpu.SemaphoreType.DMA((n,))],
        name=name,
    )(*fs)


def _pair_sum(g, r, core, name):
    nj, _, hr, cols = g.shape
    tr = _row_tile(hr, 256)

    def body(c_ref, g_ref, r_ref, o_ref):
        del c_ref
        o_ref[...] = (g_ref[...].astype(F32) + r_ref[...].astype(F32)).astype(o_ref.dtype)

    return _call(
        body,
        grid_spec=pltpu.PrefetchScalarGridSpec(
            num_scalar_prefetch=1,
            grid=(nj, hr // tr),
            in_specs=[
                pl.BlockSpec((None, None, tr, cols), lambda j, i, c: (j, c[0], i, 0)),
                pl.BlockSpec((None, tr, cols), lambda j, i, c: (j, i, 0)),
            ],
            out_specs=pl.BlockSpec((None, tr, cols), lambda j, i, c: (j, i, 0)),
        ),
        out_shape=jax.ShapeDtypeStruct((nj, hr, cols), g.dtype),
        compiler_params=_params(2),
        name=name,
    )(core, g, r)


def _chip_sum(p, q, pos, name):
    nq, hr, cols = q.shape
    tr = _row_tile(hr, 256)

    def body(pos_ref, p_ref, *refs):
        del pos_ref
        o_ref = refs[nq]
        tot = p_ref[...].astype(F32)
        for k in range(nq):
            tot = tot + refs[k][...].astype(F32)
        o_ref[...] = tot

    return _call(
        body,
        grid_spec=pltpu.PrefetchScalarGridSpec(
            num_scalar_prefetch=1,
            grid=(hr // tr,),
            in_specs=[pl.BlockSpec((None, tr, cols), lambda i, s: (s[0], i, 0))]
            + [pl.BlockSpec((None, tr, cols), functools.partial(lambda k, i, s: (k, i, 0), k)) for k in range(nq)],
            out_specs=pl.BlockSpec((None, tr, cols), lambda i, s: (s[1], i, 0)),
        ),
        out_shape=jax.ShapeDtypeStruct((2, hr, cols), F32),
        compiler_params=_params(1),
        name=name,
    )(pos, p, *([q] * nq))


def _adamw(w, g, m, v, name):
    rows, cols = w.shape
    tr = _row_tile(rows, max(8, (2**18) // cols), mult=8)
    c1 = 1.0 - ADAM_B1**ADAM_STEP
    c2 = 1.0 - ADAM_B2**ADAM_STEP

    def body(w_ref, g_ref, m_ref, v_ref, gk_ref, d_ref, nm_ref, nv_ref):
        gv = g_ref[...]
        gk_ref[...] = gv
        nm = ADAM_B1 * m_ref[...] + (1.0 - ADAM_B1) * gv
        nv = ADAM_B2 * v_ref[...] + (1.0 - ADAM_B2) * (gv * gv)
        nm_ref[...] = nm
        nv_ref[...] = nv
        d_ref[...] = -ADAM_LR * ((nm / c1) / (jnp.sqrt(nv / c2) + ADAM_EPS) + ADAM_WD * w_ref[...])

    spec = pl.BlockSpec((tr, cols), lambda i: (i, 0))
    sds = jax.ShapeDtypeStruct((rows, cols), F32)
    return _call(
        body,
        grid=(rows // tr,),
        in_specs=[spec] * 4,
        out_specs=[spec] * 4,
        out_shape=[sds] * 4,
        compiler_params=_params(1),
        name=name,
    )(w, g, m, v)


_TILE_ELEMS = 8 * LANES
_FLAT_ROW_MULT = 8 * 2 * N_CHIPS


def _flat_rows(shape):
    n = math.prod(shape)
    return (n + _TILE_ELEMS - 1) // _TILE_ELEMS * 8


def _pack(arrs):
    parts = []
    for a in arrs:
        rows = _flat_rows(a.shape)
        flat = a.reshape(-1).astype(F32)
        flat = jnp.pad(flat, (0, rows * LANES - flat.shape[0]))
        parts.append(flat.reshape(rows, LANES))
    total = sum(p.shape[0] for p in parts)
    pad = -total % _FLAT_ROW_MULT
    if pad:
        parts.append(jnp.zeros((pad, LANES), F32))
    return jnp.concatenate(parts, axis=0)


def _unpack(flat, shapes):
    out, row = [], 0
    for shp in shapes:
        rows = _flat_rows(shp)
        out.append(flat[row : row + rows].reshape(-1)[: math.prod(shp)].reshape(shp))
        row += rows
    return out


def kernel(x, norm_mix_g, w_in, ssm_a_re, ssm_a_im, ssm_b_re, ssm_b_im, ssm_c_re, ssm_c_im, ssm_d, ssm_log_dt, ssm_glu_w, ssm_glu_b, sgu_ln_g, sgu_ln_b, sgu_w, sgu_b, out_norm_ssm_g, out_norm_sgu_g, w_out, norm_mlp_g, w_up, w_down, norm_final_g, loss_target, m_norm_mix_g, m_w_in, m_ssm_a_re, m_ssm_a_im, m_ssm_b_re, m_ssm_b_im, m_ssm_c_re, m_ssm_c_im, m_ssm_d, m_ssm_log_dt, m_ssm_glu_w, m_ssm_glu_b, m_sgu_ln_g, m_sgu_ln_b, m_sgu_w, m_sgu_b, m_out_norm_ssm_g, m_out_norm_sgu_g, m_w_out, m_norm_mlp_g, m_w_up, m_w_down, m_norm_final_g, v_norm_mix_g, v_w_in, v_ssm_a_re, v_ssm_a_im, v_ssm_b_re, v_ssm_b_im, v_ssm_c_re, v_ssm_c_im, v_ssm_d, v_ssm_log_dt, v_ssm_glu_w, v_ssm_glu_b, v_sgu_ln_g, v_sgu_ln_b, v_sgu_w, v_sgu_b, v_out_norm_ssm_g, v_out_norm_sgu_g, v_w_out, v_norm_mlp_g, v_w_up, v_w_down, v_norm_final_g):
    weights = dict(norm_mix_g=norm_mix_g, w_in=w_in, ssm_a_re=ssm_a_re, ssm_a_im=ssm_a_im, ssm_b_re=ssm_b_re, ssm_b_im=ssm_b_im, ssm_c_re=ssm_c_re, ssm_c_im=ssm_c_im, ssm_d=ssm_d, ssm_log_dt=ssm_log_dt, ssm_glu_w=ssm_glu_w, ssm_glu_b=ssm_glu_b, sgu_ln_g=sgu_ln_g, sgu_ln_b=sgu_ln_b, sgu_w=sgu_w, sgu_b=sgu_b, out_norm_ssm_g=out_norm_ssm_g, out_norm_sgu_g=out_norm_sgu_g, w_out=w_out, norm_mlp_g=norm_mlp_g, w_up=w_up, w_down=w_down, norm_final_g=norm_final_g)
    mom_m = dict(norm_mix_g=m_norm_mix_g, w_in=m_w_in, ssm_a_re=m_ssm_a_re, ssm_a_im=m_ssm_a_im, ssm_b_re=m_ssm_b_re, ssm_b_im=m_ssm_b_im, ssm_c_re=m_ssm_c_re, ssm_c_im=m_ssm_c_im, ssm_d=m_ssm_d, ssm_log_dt=m_ssm_log_dt, ssm_glu_w=m_ssm_glu_w, ssm_glu_b=m_ssm_glu_b, sgu_ln_g=m_sgu_ln_g, sgu_ln_b=m_sgu_ln_b, sgu_w=m_sgu_w, sgu_b=m_sgu_b, out_norm_ssm_g=m_out_norm_ssm_g, out_norm_sgu_g=m_out_norm_sgu_g, w_out=m_w_out, norm_mlp_g=m_norm_mlp_g, w_up=m_w_up, w_down=m_w_down, norm_final_g=m_norm_final_g)
    mom_v = dict(norm_mix_g=v_norm_mix_g, w_in=v_w_in, ssm_a_re=v_ssm_a_re, ssm_a_im=v_ssm_a_im, ssm_b_re=v_ssm_b_re, ssm_b_im=v_ssm_b_im, ssm_c_re=v_ssm_c_re, ssm_c_im=v_ssm_c_im, ssm_d=v_ssm_d, ssm_log_dt=v_ssm_log_dt, ssm_glu_w=v_ssm_glu_w, ssm_glu_b=v_ssm_glu_b, sgu_ln_g=v_sgu_ln_g, sgu_ln_b=v_sgu_ln_b, sgu_w=v_sgu_w, sgu_b=v_sgu_b, out_norm_ssm_g=v_out_norm_ssm_g, out_norm_sgu_g=v_out_norm_sgu_g, w_out=v_w_out, norm_mlp_g=v_norm_mlp_g, w_up=v_w_up, w_down=v_w_down, norm_final_g=v_norm_final_g)
    names = list(weights)
    large = ["w_in", "ssm_glu_w", "w_out", "w_up", "w_down"]
    small = [n for n in names if n not in large]

    s, d = x.shape[1], x.shape[2]
    xs = x.reshape(s, d)
    target = loss_target.reshape(s, d)
    width = ssm_glu_w.shape[-1]
    ff = w_down.shape[1] * N_CHIPS
    tm = min(512, s)
    core = lax.axis_index("c").astype(jnp.int32).reshape(1)
    chip = (2 * lax.axis_index("x") + lax.axis_index("y")).astype(jnp.int32).reshape(1)
    pos = jnp.concatenate([chip, core])

    shards = [w[0].astype(BF16).reshape(2, w.shape[1] // 2, w.shape[2]) for w in (w_in, ssm_glu_w, w_out, w_up, w_down)]
    ssm_args = (ssm_a_re[0], ssm_a_im[0], ssm_b_re[0], ssm_b_im[0], ssm_c_re[0], ssm_c_im[0], ssm_d[0], ssm_log_dt[0])
    (k2c, w2c, v2c, al), ssm_vjp = jax.vjp(_ssm_mats, *ssm_args)
    consts = _spread_consts(ssm_b_re.shape[-1], ssm_a_re.shape[-1])
    (k2b, w2b, v2b), (w_in_g,) = _ssm_spread(k2c, w2c, v2c, consts, [_gather_one(shards[0])])
    w_in4 = _place_own(shards[0], w_in_g, pos, "place_w_in").reshape(N_CHIPS, d, w_in.shape[2])
    causal = jnp.tril(jnp.ones((SGU_CHUNK, SGU_CHUNK), dtype=bool))
    wt = jnp.where(causal[None], sgu_w[0], 0.0)
    wtb = wt.astype(BF16)
    wttb = jnp.swapaxes(wt, 1, 2).astype(BF16)
    heads = sgu_w.shape[1]
    biasb = jnp.broadcast_to(sgu_b[0][:, :, None], (heads, SGU_CHUNK, LANES))

    tm_big = min(1024, s)
    (z, h1b), (glu_g, out_g, up_g) = _in_proj(
        xs, norm_mix_g, w_in4, tm_big,
        [_gather_one(shards[1]), _gather_one(shards[2]), _gather_one(shards[3], peers=(0, 1))])
    wg_full = _place_own(shards[1], glu_g, pos, "place_glu_w").reshape(width, width)
    w_out_full = _place_own(shards[2], out_g, pos, "place_w_out").reshape(d, d)
    (y_pre, xprev, us), (up_g,) = _ssm_fwd(z, k2b, w2b, v2b, al, [_gather_one(shards[3], peers=(2,), into=up_g)])
    up4 = _place_own(shards[3], up_g, pos, "place_w_up").reshape(N_CHIPS, d, w_up.shape[2])
    mixed = _glu_fwd(y_pre, wg_full, ssm_glu_b, out_norm_ssm_g, d, tm)
    mixed = _sgu_fwd(z, mixed, sgu_ln_g, sgu_ln_b, wtb, biasb, out_norm_sgu_g, tm)
    x1 = _out_proj(xs, mixed, w_out_full, tm)
    tf = min(1024, up4.shape[-1])
    (up, h2b), (down_g,) = _mlp_up(x1, norm_mlp_g, up4, tm_big, tf, [_gather_one(shards[4])])
    w_down_full = _place_own(shards[4], down_g, pos, "place_w_down").reshape(ff, d)
    loss_part, dx2, dx2b, d_norm_final = _mlp_down_loss(
        x1, up, w_down_full, target, norm_final_g.reshape(1, d), tm, tf)

    dup, dh2 = _mlp_bwd(dx2b, up, up4, w_down_full, tm_big, min(512, up4.shape[-1]))
    dx1, dx1b, d_norm_mlp = _norm_bwd(dh2, x1, dx2, norm_mlp_g, min(256, s), "norm_mlp_bwd")
    hr_big = d // 2
    tkk = min(1024, s)
    g_down = _grad_w(
        up, dx2b, (N_CHIPS, 2, hr_big, d),
        lambda i, j: (i // (2 * (hr_big // min(1024, hr_big))), (i // (hr_big // min(1024, hr_big))) % 2,
                      i % (hr_big // min(1024, hr_big)), j),
        min(1024, hr_big), d, tkk, True, "grad_w_down")
    tn_up = min(2048, up4.shape[-1])
    per_up = up4.shape[-1] // tn_up
    g_up, sib_down = _grad_w(
        h2b, dup, (N_CHIPS, 2, hr_big, up4.shape[-1]),
        lambda i, j: (j // per_up, i // (hr_big // min(1024, hr_big)), i % (hr_big // min(1024, hr_big)), j % per_up),
        min(1024, hr_big), tn_up, tkk, False, "grad_w_up", exs=[_to_sibling_one(g_down)])
    pair_down = _pair_sum(g_down, sib_down, core, "pair_sum_w_down")
    dmix = _out_proj_bwd(dx1b, w_out_full, tm)
    hr_out = d // (2 * N_CHIPS)
    g_out, sib_up = _grad_w(
        mixed, dx1b, (1, 1, d, d), lambda i, j: (0, 0, i, j), min(1024, d), d, tkk, False, "grad_w_out",
        exs=[_to_sibling_one(g_up)])
    g_out = g_out.reshape(N_CHIPS, 2, hr_out, d)
    pair_up = _pair_sum(g_up, sib_up, core, "pair_sum_w_up")
    (dy_pre, g_glu, d_glu_b, d_norm_ssm), (sib_out,) = _glu_bwd(
        y_pre, dmix, wg_full, ssm_glu_b, out_norm_ssm_g, tm, [_to_sibling_one(g_out)])
    pair_out = _pair_sum(g_out, sib_out, core, "pair_sum_w_out")
    dys, gst, d_al = _ssm_bwd_state(dy_pre, v2b, al, xprev)
    (dz, d_k2, d_w2, d_v2), (chips_down,) = _ssm_bwd_main(
        us, dys, k2b, w2b, xprev, gst, consts, s, z.shape[1], [_to_owner_one(pair_down)])
    (dz, d_wt, d_bias, d_ln_g, d_ln_b, d_norm_sgu), (chips_up,) = _sgu_bwd(
        z, dmix, dz, sgu_ln_g, sgu_ln_b, wtb, wttb, biasb, out_norm_sgu_g, tm,
        [_to_owner_one(pair_up, peers=(0, 1))])
    cw_in = w_in4.shape[-1]
    g_in, chips_out, chips_up = _grad_w(
        h1b, dz, (N_CHIPS, 2, hr_big, cw_in),
        lambda i, j: (j, i // (hr_big // min(1024, hr_big)), i % (hr_big // min(1024, hr_big)), 0),
        min(1024, hr_big), 2 * cw_in, tkk, False, "grad_w_in",
        exs=[_to_owner_one(pair_out), _to_owner_one(pair_up, peers=(2,), into=chips_up)], chunks=2)
    (sib_in,) = _pair_exchange([g_in], "w_in_to_sibling")
    pair_in = _pair_sum(g_in, sib_in, core, "pair_sum_w_in")
    tm_x = min(512, s // 2)
    n_x = s // tm_x
    (gx_half, dg_a), (chips_in,) = _in_proj_bwd(
        dz, w_in4, xs, dx1, norm_mix_g, tm_x, [_to_owner_one(pair_in)], (0, n_x // 2), None, "in_proj_bwd_a")
    (grad_x, dg_b), _ = _in_proj_bwd(
        dz, w_in4, xs, dx1, norm_mix_g, tm_x, [], (n_x // 2, n_x), gx_half, "in_proj_bwd_b")
    d_norm_mix = dg_a + dg_b

    d_ssm = ssm_vjp((d_k2, d_w2, d_v2, d_al))
    small_grads = dict(
        norm_mix_g=d_norm_mix, ssm_a_re=d_ssm[0], ssm_a_im=d_ssm[1], ssm_b_re=d_ssm[2], ssm_b_im=d_ssm[3],
        ssm_c_re=d_ssm[4], ssm_c_im=d_ssm[5], ssm_d=d_ssm[6], ssm_log_dt=d_ssm[7], ssm_glu_b=d_glu_b,
        sgu_ln_g=d_ln_g, sgu_ln_b=d_ln_b, sgu_w=jnp.where(causal[None], d_wt, 0.0), sgu_b=d_bias[:, :, 0],
        out_norm_ssm_g=d_norm_ssm, out_norm_sgu_g=d_norm_sgu, norm_mlp_g=d_norm_mlp, norm_final_g=d_norm_final)
    flat = _pack([small_grads[n] for n in small])
    hr_small = flat.shape[0] // (2 * N_CHIPS)
    g_small = flat.reshape(N_CHIPS, 2, hr_small, LANES)

    hr_glu = width // (2 * N_CHIPS)
    grads = [g_glu.reshape(N_CHIPS, 2, hr_glu, width), g_small]
    tags = ["glu_w", "small"]
    from_sib = _pair_exchange(grads, "grads_to_sibling")
    pair = [_pair_sum(g, r, core, "pair_sum_" + t) for g, r, t in zip(grads, from_sib, tags)]
    from_chips = _chip_exchange(pair, "grads_to_owner")
    pair = [pair_in, pair[0], pair_out, pair_up, pair_down, pair[1]]
    from_chips = [chips_in, from_chips[0], chips_out, chips_up, chips_down, from_chips[1]]
    tags = ["w_in", "glu_w", "w_out", "w_up", "w_down", "small"]
    halves = [_chip_sum(p, q, pos, "chip_sum_" + t) for p, q, t in zip(pair, from_chips, tags)]
    owned = _pair_share(halves, "grads_to_both_cores")
    (small_all,) = _gather_chips([owned[5]], "gather_small_grads")
    small_flat = small_all.reshape(flat.shape)

    grad_out, delta_out, m_out, v_out = {}, {}, {}, {}
    for n, g in zip(large, owned[:5]):
        shp = weights[n].shape
        g2, dl, nm, nv = _adamw(
            weights[n][0], g.reshape(shp[1], shp[2]), mom_m[n][0], mom_v[n][0], "adamw_" + n)
        grad_out[n], delta_out[n], m_out[n], v_out[n] = g2.reshape(shp), dl.reshape(shp), nm.reshape(shp), nv.reshape(shp)
    shapes = [weights[n].shape for n in small]
    g2, dl, nm, nv = _adamw(
        _pack([weights[n] for n in small]), small_flat, _pack([mom_m[n] for n in small]),
        _pack([mom_v[n] for n in small]), "adamw_small")
    for n, g, a, b, c in zip(small, _unpack(g2, shapes), _unpack(dl, shapes), _unpack(nm, shapes), _unpack(nv, shapes)):
        grad_out[n], delta_out[n], m_out[n], v_out[n] = g, a, b, c

    loss = lax.psum(loss_part[0, 0], ("x", "y", "c"))
    return (loss, grad_x.reshape(x.shape), *[grad_out[n] for n in names], *[delta_out[n] for n in names],
            *[m_out[n] for n in names], *[v_out[n] for n in names])
```

```python
import functools
import math

import numpy as np
import jax
import jax.numpy as jnp
from jax import lax
from jax.experimental import pallas as pl
from jax.experimental.pallas import tpu as pltpu

F32 = jnp.float32
BF16 = jnp.bfloat16
MESH = pl.DeviceIdType.MESH
HIGHEST = lax.Precision.HIGHEST

EPS = 1e-6
ADAM_LR = 0.001
ADAM_B1 = 0.9
ADAM_B2 = 0.999
ADAM_EPS = 1e-08
ADAM_WD = 0.01
ADAM_STEP = 10

N_CHIPS = 4
LANES = 128
SSM_GROUP = 16
SSM_STATE = 64
GROUPS_PER_TILE = LANES // SSM_GROUP
CHUNK = 16
PAIRS = CHUNK // 2
SGU_CHUNK = 128
VMEM_LIMIT = 56 * 2**20


def _params(n_axes, vmem=VMEM_LIMIT):
    return pltpu.CompilerParams(dimension_semantics=("arbitrary",) * n_axes, vmem_limit_bytes=vmem)


def _call(body, **kw):
    return pl.pallas_call(body, **kw)


def _dot(a, b):
    return jnp.dot(a, b, preferred_element_type=F32)


def _dot_nt(a, b):
    return lax.dot_general(a, b, (((1,), (1,)), ((), ())), preferred_element_type=F32)


def _dot_tn(a, b):
    return lax.dot_general(a, b, (((0,), (0,)), ((), ())), preferred_element_type=F32)


_GELU_K = math.sqrt(2.0 / math.pi)
_GELU_C = 0.044715


def _gelu_both(x):
    x2 = x * x
    t = jnp.tanh(x * (_GELU_K + (_GELU_K * _GELU_C) * x2))
    hx = 0.5 * x
    onep = 1.0 + t
    return hx * onep, 0.5 * onep + hx * (1.0 - t * t) * (_GELU_K + (3.0 * _GELU_K * _GELU_C) * x2)


def _gelu(x):
    return _gelu_both(x)[0]


def _sigmoid(x):
    return 1.0 / (1.0 + jnp.exp(-x))


def _rms(x):
    return lax.rsqrt(jnp.mean(x * x, axis=-1, keepdims=True) + EPS)


def _rms_bwd(dy, x, r, g):
    a = dy * g
    dx = r * a - x * (r * r * r) * jnp.mean(a * x, axis=-1, keepdims=True)
    return dx, dy * x * r


def _row_tile(rows, target, mult=16):
    for t in range(min(rows, target), 0, -1):
        if rows % t == 0 and t % mult == 0:
            return t
    return rows


def _ssm_mats(a_re, a_im, b_re, b_im, c_re, c_im, d, log_dt):
    g, p = a_re.shape
    h = b_re.shape[-1]
    nt = g // GROUPS_PER_TILE
    dt = jnp.exp(log_dt)[:, None]
    lr, li = a_re * dt, a_im * dt

    def apow(l, lr, li):
        mag = jnp.exp(lr * l)
        return mag * jnp.cos(li * l), mag * jnp.sin(li * l)

    ar, ai = apow(1.0, lr, li)
    den = a_re * a_re + a_im * a_im
    qr = ((ar - 1.0) * a_re + ai * a_im) / den
    qi = (ai * a_re - (ar - 1.0) * a_im) / den
    bt_re, bt_im = jnp.swapaxes(b_re, 1, 2), jnp.swapaxes(b_im, 1, 2)
    bbr = qr[:, None, :] * bt_re - qi[:, None, :] * bt_im
    bbi = qr[:, None, :] * bt_im + qi[:, None, :] * bt_re
    b_same = jnp.concatenate([bbr, bbi], axis=-1)[:, None]
    b_swap = jnp.concatenate([bbi, bbr], axis=-1)[:, None]
    ls = jnp.arange(CHUNK + 1, dtype=F32)[None, :, None]
    pr, pi = apow(ls, lr[:, None, :], li[:, None, :])
    p_same = jnp.concatenate([pr, pr], axis=-1)[:, :, None, :]
    p_sign = jnp.concatenate([-pi, pi], axis=-1)[:, :, None, :]
    tj = p_same[:, :CHUNK] * b_same + p_sign[:, :CHUNK] * b_swap
    c_conj = jnp.concatenate([c_re, -c_im], axis=-1)
    k = jnp.einsum("goq,gliq->glio", c_conj, tj, precision=HIGHEST)
    k = k.at[:, 0].add(d[:, :, None] * jnp.eye(h, dtype=F32))

    zero = jnp.zeros_like(k[:, :1])
    kz = jnp.concatenate([zero, zero, k], axis=1).reshape(g, PAIRS + 1, 2, h, h)
    even, odd = kz[:, :, 0], kz[:, :, 1]
    fill = jnp.zeros((g, PAIRS, h, LANES - 2 * h), F32)
    row0 = jnp.concatenate([even[:, 1:], odd[:, 1:], fill], axis=-1)
    row1 = jnp.concatenate([odd[:, :-1], even[:, 1:], fill], axis=-1)
    k2c = jnp.stack([row0, row1], axis=2)
    w2c = tj[:, ::-1].reshape(g, PAIRS, 2, h, 2 * p)
    c_cross = jnp.concatenate([-c_im, -c_re], axis=-1)
    p_imag = jnp.concatenate([pi, pi], axis=-1)[:, :, None, :]
    v2c = (c_conj[:, None] * p_same[:, 1:] + c_cross[:, None] * p_imag[:, 1:]).reshape(g, PAIRS, 2, h, 2 * p)
    al = jnp.stack([pr[:, CHUNK].reshape(nt, -1), pi[:, CHUNK].reshape(nt, -1)], axis=1)
    return k2c, w2c, v2c, al


def _spread_consts(h, p):
    gg = GROUPS_PER_TILE
    row_g = (np.arange(2 * gg * h) // h) % gg
    colk = np.arange(2 * gg * h)
    rep_k = np.zeros((LANES, 2 * gg * h), np.float32)
    rep_k[(colk // (gg * h)) * h + colk % h, colk] = 1.0
    mask_k = (row_g[:, None] == ((colk // h) % gg)[None, :]).astype(np.float32)
    cols = np.arange(2 * gg * p)
    rep_s = np.zeros((2 * p, 2 * gg * p), np.float32)
    rep_s[(cols // (gg * p)) * p + cols % p, cols] = 1.0
    mask_s = (row_g[:, None] == ((cols // p) % gg)[None, :]).astype(np.float32)
    return tuple(jnp.asarray(a, BF16) for a in (rep_k, mask_k, rep_s, mask_s))


def _ssm_spread(k2c, w2c, v2c, consts, exs):
    h = k2c.shape[3]
    nt = k2c.shape[0] // GROUPS_PER_TILE
    rep_k, mask_k, rep_s, mask_s = consts
    nk, ns = rep_k.shape[1], rep_s.shape[1]

    def body(k_ref, w_ref, v_ref, rk_ref, mk_ref, rs_ref, ms_ref, ko_ref, wo_ref, vo_ref):
        for q in range(PAIRS):
            for c_ref, o_ref, r_ref, m_ref in ((k_ref, ko_ref, rk_ref, mk_ref), (w_ref, wo_ref, rs_ref, ms_ref),
                                               (v_ref, vo_ref, rs_ref, ms_ref)):
                rows = jnp.concatenate(
                    [c_ref[gi, q, st] for st in range(2) for gi in range(GROUPS_PER_TILE)], axis=0)
                o_ref[q] = (_dot(rows.astype(BF16), r_ref[...]) * m_ref[...]).astype(BF16)

    compact = pl.BlockSpec((GROUPS_PER_TILE, PAIRS, 2, h, LANES), lambda t: (t, 0, 0, 0, 0))

    def tile(cols):
        return pl.BlockSpec((None, PAIRS, nk, cols), lambda t: (t, 0, 0, 0))

    def whole(a):
        return pl.BlockSpec(a.shape, lambda t: (0, 0))

    return _call_riding(
        body,
        exs,
        args=(k2c, w2c, v2c, rep_k, mask_k, rep_s, mask_s),
        scratch_shapes=[],
        grid=(nt,),
        in_specs=[compact, compact, compact, whole(rep_k), whole(mask_k), whole(rep_s), whole(mask_s)],
        out_specs=[tile(nk), tile(ns), tile(ns)],
        out_shape=[
            jax.ShapeDtypeStruct((nt, PAIRS, nk, nk), BF16),
            jax.ShapeDtypeStruct((nt, PAIRS, nk, ns), BF16),
            jax.ShapeDtypeStruct((nt, PAIRS, nk, ns), BF16),
        ],
        compiler_params=_params(1),
        name="ssm_spread",
    )


def _gather_blocks(full, mask, rep):
    return _dot_nt((full * mask).astype(BF16), rep)


def _in_proj(x, g, w4, tm, exs):
    s, d = x.shape
    nj, _, cw = w4.shape

    def body(x_ref, g_ref, w_ref, z_ref, h_ref):
        @pl.when(pl.program_id(1) == 0)
        def _():
            xv = x_ref[...]
            h_ref[...] = (xv * _rms(xv) * g_ref[...]).astype(BF16)

        z_ref[...] = _dot(h_ref[...], w_ref[...])

    return _call_riding(
        body,
        exs,
        scratch_shapes=[],
        args=(x, g, w4),
        grid=(s // tm, nj),
        in_specs=[
            pl.BlockSpec((tm, d), lambda i, j: (i, 0)),
            pl.BlockSpec((1, d), lambda i, j: (0, 0)),
            pl.BlockSpec((None, d, cw), lambda i, j: (j, 0, 0)),
        ],
        out_specs=[pl.BlockSpec((tm, cw), lambda i, j: (i, j)), pl.BlockSpec((tm, d), lambda i, j: (i, 0))],
        out_shape=[jax.ShapeDtypeStruct((s, nj * cw), F32), jax.ShapeDtypeStruct((s, d), BF16)],
        compiler_params=_params(2),
        name="in_proj",
    )


def _ssm_fwd(z, k2, w2, v2, al, exs):
    s = z.shape[0]
    nt = k2.shape[0]
    nc = s // CHUNK
    ns = w2.shape[-1]
    hs = ns // 2

    def body(u_ref, k_ref, w_ref, v_ref, al_ref, y_ref, xp_ref, us_ref, xloc):
        for q in range(PAIRS):
            us_ref[q, :, 0:LANES] = u_ref[pl.ds(2 * q, nc, stride=CHUNK), :].astype(BF16)
            us_ref[q, :, LANES : 2 * LANES] = u_ref[pl.ds(2 * q + 1, nc, stride=CHUNK), :].astype(BF16)
        acc = _dot(us_ref[0], w_ref[0])
        for q in range(1, PAIRS):
            acc = acc + _dot(us_ref[q], w_ref[q])
        xloc[...] = acc
        ar = al_ref[0:1, :]
        ai = al_ref[1:2, :]

        def step(c, carry):
            xr, xi = carry
            xp_ref[pl.ds(c, 1), 0:hs] = xr
            xp_ref[pl.ds(c, 1), hs:ns] = xi
            lr = xloc[pl.ds(c, 1), 0:hs]
            li = xloc[pl.ds(c, 1), hs:ns]
            return ar * xr - ai * xi + lr, ar * xi + ai * xr + li

        zero = jnp.zeros((1, hs), F32)
        lax.fori_loop(0, nc, step, (zero, zero))
        xpb = xp_ref[...].astype(BF16)
        for r in range(PAIRS):
            acc = _dot_nt(xpb, v_ref[r])
            for q in range(r + 1):
                acc = acc + _dot(us_ref[q], k_ref[r - q])
            y_ref[pl.ds(2 * r, nc, stride=CHUNK), :] = acc[:, 0:LANES]
            y_ref[pl.ds(2 * r + 1, nc, stride=CHUNK), :] = acc[:, LANES : 2 * LANES]

    return _call_riding(
        body,
        exs,
        args=(z, k2, w2, v2, al),
        grid=(nt,),
        in_specs=[
            pl.BlockSpec((s, LANES), lambda t: (0, t)),
            pl.BlockSpec((None, PAIRS, 2 * LANES, 2 * LANES), lambda t: (t, 0, 0, 0)),
            pl.BlockSpec((None, PAIRS, 2 * LANES, ns), lambda t: (t, 0, 0, 0)),
            pl.BlockSpec((None, PAIRS, 2 * LANES, ns), lambda t: (t, 0, 0, 0)),
            pl.BlockSpec((None, 2, hs), lambda t: (t, 0, 0)),
        ],
        out_specs=[
            pl.BlockSpec((s, LANES), lambda t: (0, t)),
            pl.BlockSpec((None, nc, ns), lambda t: (t, 0, 0)),
            pl.BlockSpec((None, PAIRS, nc, 2 * LANES), lambda t: (t, 0, 0, 0)),
        ],
        out_shape=[
            jax.ShapeDtypeStruct((s, nt * LANES), F32),
            jax.ShapeDtypeStruct((nt, nc, ns), F32),
            jax.ShapeDtypeStruct((nt, PAIRS, nc, 2 * LANES), BF16),
        ],
        scratch_shapes=[pltpu.VMEM((nc, ns), F32)],
        compiler_params=_params(1),
        name="ssm_fwd",
    )


def _glu_fwd(y_pre, wg, bg, go, d_model, tm):
    s, w = y_pre.shape

    def body(y_ref, wg_ref, bg_ref, go_ref, o_ref):
        yg = _gelu(y_ref[...])
        gate = _sigmoid(_dot(yg.astype(BF16), wg_ref[...]) + bg_ref[...])
        ys = yg * gate
        o_ref[...] = (ys * _rms(ys) * go_ref[...]).astype(BF16)

    return _call(
        body,
        grid=(s // tm,),
        in_specs=[
            pl.BlockSpec((tm, w), lambda i: (i, 0)),
            pl.BlockSpec((w, w), lambda i: (0, 0)),
            pl.BlockSpec((1, w), lambda i: (0, 0)),
            pl.BlockSpec((1, w), lambda i: (0, 0)),
        ],
        out_specs=pl.BlockSpec((tm, w), lambda i: (i, 0)),
        out_shape=jax.ShapeDtypeStruct((s, d_model), BF16),
        compiler_params=_params(1),
        name="glu_fwd",
    )(y_pre, wg, bg, go)


def _sgu_parts(zu, zv, lng, lnb, wt_ref, bias_ref):
    u, u_grad = _gelu_both(zu)
    v, v_grad = _gelu_both(zv)
    mu = jnp.mean(v, axis=-1, keepdims=True)
    vc = v - mu
    rstd = lax.rsqrt(jnp.mean(vc * vc, axis=-1, keepdims=True) + EPS)
    vhat = vc * rstd
    vb = (vhat * lng + lnb).astype(BF16)
    heads = wt_ref.shape[0]
    mix = jnp.concatenate(
        [_dot(wt_ref[h], vb[:, h * LANES : (h + 1) * LANES]) + bias_ref[h] for h in range(heads)], axis=1
    )
    return u, vhat, rstd, vb, mix, u_grad, v_grad


def _sgu_fwd(z, mixed, lng, lnb, wt, biasb, go, rb):
    s = z.shape[0]
    w = lng.shape[-1]
    heads = wt.shape[0]

    def body(zu_ref, zv_ref, m_any, lng_ref, lnb_ref, wt_ref, b_ref, go_ref, o_ref):
        del m_any
        for ck in range(rb // SGU_CHUNK):
            rows = slice(ck * SGU_CHUNK, (ck + 1) * SGU_CHUNK)
            u, _, _, _, mix, _, _ = _sgu_parts(zu_ref[rows, :], zv_ref[rows, :], lng_ref[...], lnb_ref[...], wt_ref, b_ref)
            y = u * mix
            o_ref[rows, :] = (y * _rms(y) * go_ref[...]).astype(BF16)

    return _call(
        body,
        grid=(s // rb,),
        in_specs=[
            pl.BlockSpec((rb, w), lambda i: (i, 1)),
            pl.BlockSpec((rb, w), lambda i: (i, 2)),
            pl.BlockSpec(memory_space=pl.ANY),
            pl.BlockSpec((1, w), lambda i: (0, 0)),
            pl.BlockSpec((1, w), lambda i: (0, 0)),
            pl.BlockSpec((heads, SGU_CHUNK, SGU_CHUNK), lambda i: (0, 0, 0)),
            pl.BlockSpec((heads, SGU_CHUNK, LANES), lambda i: (0, 0, 0)),
            pl.BlockSpec((1, w), lambda i: (0, 0)),
        ],
        out_specs=pl.BlockSpec((rb, w), lambda i: (i, 1)),
        out_shape=jax.ShapeDtypeStruct(mixed.shape, mixed.dtype),
        input_output_aliases={2: 0},
        compiler_params=_params(1),
        name="sgu_fwd",
    )(z, z, mixed, lng, lnb, wt, biasb, go)


def _out_proj(x, mixed, w_out, tm):
    s, d = x.shape

    def body(x_ref, m_ref, w_ref, o_ref):
        o_ref[...] = x_ref[...] + _dot(m_ref[...], w_ref[...])

    return _call(
        body,
        grid=(s // tm,),
        in_specs=[
            pl.BlockSpec((tm, d), lambda i: (i, 0)),
            pl.BlockSpec((tm, d), lambda i: (i, 0)),
            pl.BlockSpec((d, d), lambda i: (0, 0)),
        ],
        out_specs=pl.BlockSpec((tm, d), lambda i: (i, 0)),
        out_shape=jax.ShapeDtypeStruct((s, d), F32),
        compiler_params=_params(1),
        name="out_proj",
    )(x, mixed, w_out)


def _mlp_up(x1, g, w_up4, tm, tf, exs):
    s, d = x1.shape
    nj, _, cw = w_up4.shape
    per = cw // tf

    def body(x_ref, g_ref, wu_ref, up_ref, h_ref):
        @pl.when(pl.program_id(1) == 0)
        def _():
            xv = x_ref[...]
            h_ref[...] = (xv * _rms(xv) * g_ref[...]).astype(BF16)

        up_ref[...] = _dot(h_ref[...], wu_ref[...]).astype(BF16)

    return _call_riding(
        body,
        exs,
        scratch_shapes=[],
        args=(x1, g, w_up4),
        grid=(s // tm, nj * per),
        in_specs=[
            pl.BlockSpec((tm, d), lambda i, f: (i, 0)),
            pl.BlockSpec((1, d), lambda i, f: (0, 0)),
            pl.BlockSpec((None, d, tf), lambda i, f: (f // per, 0, f % per)),
        ],
        out_specs=[pl.BlockSpec((tm, tf), lambda i, f: (i, f)), pl.BlockSpec((tm, d), lambda i, f: (i, 0))],
        out_shape=[jax.ShapeDtypeStruct((s, nj * cw), BF16), jax.ShapeDtypeStruct((s, d), BF16)],
        compiler_params=_params(2),
        name="mlp_up",
    )


def _mlp_down_loss(x1, up, w_down, target, g, tm, tf):
    s, d = x1.shape
    ff = w_down.shape[0]
    nf = ff // tf

    def body(x_ref, up_ref, wd_ref, t_ref, g_ref, loss_ref, dx_ref, dxb_ref, dg_ref, acc):
        i = pl.program_id(0)
        f = pl.program_id(1)

        @pl.when(f == 0)
        def _():
            acc[...] = x_ref[...]

        @pl.when(jnp.logical_and(i == 0, f == 0))
        def _():
            loss_ref[...] = jnp.zeros_like(loss_ref)
            dg_ref[...] = jnp.zeros_like(dg_ref)

        a = jnp.maximum(up_ref[...].astype(F32), 0.0)
        acc[...] += _dot((a * a).astype(BF16), wd_ref[...])

        @pl.when(f == nf - 1)
        def _():
            xv = acc[...]
            r = _rms(xv)
            gv = g_ref[...]
            diff = xv * r * gv - t_ref[...]
            loss_ref[...] += 0.5 * jnp.sum(jnp.mean(diff * diff, axis=-1, keepdims=True), axis=0, keepdims=True)
            dx, dgt = _rms_bwd(diff * (1.0 / d), xv, r, gv)
            dx_ref[...] = dx
            dxb_ref[...] = dx.astype(BF16)
            dg_ref[...] += jnp.sum(dgt, axis=0, keepdims=True)

    row = pl.BlockSpec((tm, d), lambda i, f: (i, 0))
    vec = pl.BlockSpec((1, d), lambda i, f: (0, 0))
    return _call(
        body,
        grid=(s // tm, nf),
        in_specs=[
            row,
            pl.BlockSpec((tm, tf), lambda i, f: (i, f)),
            pl.BlockSpec((tf, d), lambda i, f: (f, 0)),
            row,
            vec,
        ],
        out_specs=[pl.BlockSpec((1, 1), lambda i, f: (0, 0)), row, row, vec],
        out_shape=[
            jax.ShapeDtypeStruct((1, 1), F32),
            jax.ShapeDtypeStruct((s, d), F32),
            jax.ShapeDtypeStruct((s, d), BF16),
            jax.ShapeDtypeStruct((1, d), F32),
        ],
        scratch_shapes=[pltpu.VMEM((tm, d), F32)],
        compiler_params=_params(2),
        name="mlp_down_loss",
    )(x1, up, w_down, target, g)


def _mlp_bwd(dx2b, up, w_up4, w_down, tm, tf):
    s, d = dx2b.shape
    ff = w_down.shape[0]
    cw = w_up4.shape[-1]
    per = cw // tf

    def body(dx_ref, up_ref, wu_ref, wd_ref, dup_ref, dh_ref):
        @pl.when(pl.program_id(1) == 0)
        def _():
            dh_ref[...] = jnp.zeros_like(dh_ref)

        dact = _dot_nt(dx_ref[...], wd_ref[...])
        dupb = (dact * (2.0 * jnp.maximum(up_ref[...].astype(F32), 0.0))).astype(BF16)
        dup_ref[...] = dupb
        dh_ref[...] += _dot_nt(dupb, wu_ref[...])

    return _call(
        body,
        grid=(s // tm, ff // tf),
        in_specs=[
            pl.BlockSpec((tm, d), lambda i, f: (i, 0)),
            pl.BlockSpec((tm, tf), lambda i, f: (i, f)),
            pl.BlockSpec((None, d, tf), lambda i, f: (f // per, 0, f % per)),
            pl.BlockSpec((tf, d), lambda i, f: (f, 0)),
        ],
        out_specs=[pl.BlockSpec((tm, tf), lambda i, f: (i, f)), pl.BlockSpec((tm, d), lambda i, f: (i, 0))],
        out_shape=[jax.ShapeDtypeStruct((s, ff), BF16), jax.ShapeDtypeStruct((s, d), F32)],
        compiler_params=_params(2),
        name="mlp_bwd",
    )(dx2b, up, w_up4, w_down)


def _norm_bwd(dh, x, dres, g, tm, name):
    s, d = x.shape

    def body(dh_ref, x_ref, dr_ref, g_ref, dx_ref, dxb_ref, dg_ref):
        @pl.when(pl.program_id(0) == 0)
        def _():
            dg_ref[...] = jnp.zeros_like(dg_ref)

        xv = x_ref[...]
        dx, dgt = _rms_bwd(dh_ref[...], xv, _rms(xv), g_ref[...])
        tot = dr_ref[...] + dx
        dx_ref[...] = tot
        dxb_ref[...] = tot.astype(BF16)
        dg_ref[...] += jnp.sum(dgt, axis=0, keepdims=True)

    row = pl.BlockSpec((tm, d), lambda i: (i, 0))
    vec = pl.BlockSpec((1, d), lambda i: (0, 0))
    return _call(
        body,
        grid=(s // tm,),
        in_specs=[row, row, row, vec],
        out_specs=[row, row, vec],
        out_shape=[
            jax.ShapeDtypeStruct((s, d), F32),
            jax.ShapeDtypeStruct((s, d), BF16),
            jax.ShapeDtypeStruct((1, d), F32),
        ],
        compiler_params=_params(1),
        name=name,
    )(dh, x, dres, g)


def _grad_w(a, b, out_dims, index_map, tm, tn, tk, relu2, name, exs=(), chunks=1):
    t, m = a.shape
    n = b.shape[1]
    nk = t // tk
    cw = tn // chunks

    def body(a_ref, b_ref, o_ref, acc):
        k = pl.program_id(2)

        @pl.when(k == 0)
        def _():
            acc[...] = jnp.zeros_like(acc)

        av = a_ref[...]
        if relu2:
            r = jnp.maximum(av.astype(F32), 0.0)
            av = (r * r).astype(BF16)
        acc[...] += _dot_tn(av, b_ref[...])

        @pl.when(k == nk - 1)
        def _():
            for c in range(chunks):
                o_ref[c] = acc[:, c * cw : (c + 1) * cw].astype(BF16)

    spec = dict(
        grid=(m // tm, n // tn, nk),
        in_specs=[pl.BlockSpec((tk, tm), lambda i, j, k: (k, i)), pl.BlockSpec((tk, tn), lambda i, j, k: (k, j))],
        out_specs=[pl.BlockSpec((chunks, None, tm, cw), lambda i, j, k: index_map(i, j))],
        out_shape=[jax.ShapeDtypeStruct(out_dims, BF16)],
        scratch_shapes=[pltpu.VMEM((tm, tn), F32)],
        compiler_params=_params(3),
        name=name,
    )
    (g,), landed = _call_riding(body, exs, args=(a, b), **spec)
    return (g, *landed) if exs else g


def _out_proj_bwd(dx1b, w_out, tm):
    s, d = dx1b.shape

    def body(dx_ref, w_ref, o_ref):
        o_ref[...] = _dot_nt(dx_ref[...], w_ref[...])

    return _call(
        body,
        grid=(s // tm,),
        in_specs=[pl.BlockSpec((tm, d), lambda i: (i, 0)), pl.BlockSpec((d, d), lambda i: (0, 0))],
        out_specs=pl.BlockSpec((tm, d), lambda i: (i, 0)),
        out_shape=jax.ShapeDtypeStruct((s, d), F32),
        compiler_params=_params(1),
        name="out_proj_bwd",
    )(dx1b, w_out)


def _glu_bwd(y_pre, dmix, wg, bg, go, tm, exs):
    s, w = y_pre.shape
    n = s // tm

    def body(y_ref, dm_ref, wg_ref, bg_ref, go_ref, dy_ref, dwg_ref, dbg_ref, dgo_ref, acc):
        i = pl.program_id(0)

        @pl.when(i == 0)
        def _():
            acc[...] = jnp.zeros_like(acc)
            dbg_ref[...] = jnp.zeros_like(dbg_ref)
            dgo_ref[...] = jnp.zeros_like(dgo_ref)

        yp = y_ref[...]
        yg, yg_grad = _gelu_both(yp)
        ygb = yg.astype(BF16)
        gate = _sigmoid(_dot(ygb, wg_ref[...]) + bg_ref[...])
        ys = yg * gate
        dys, dgt = _rms_bwd(dm_ref[...], ys, _rms(ys), go_ref[...])
        dgo_ref[...] += jnp.sum(dgt, axis=0, keepdims=True)
        dpre = dys * yg * gate * (1.0 - gate)
        dbg_ref[...] += jnp.sum(dpre, axis=0, keepdims=True)
        dpb = dpre.astype(BF16)
        dyg = dys * gate + _dot_nt(dpb, wg_ref[...])
        dy_ref[...] = dyg * yg_grad
        acc[...] += _dot_tn(ygb, dpb)

        @pl.when(i == n - 1)
        def _():
            dwg_ref[...] = acc[...].astype(BF16)

    return _call_riding(
        body,
        exs,
        args=(y_pre, dmix, wg, bg, go),
        grid=(n,),
        in_specs=[
            pl.BlockSpec((tm, w), lambda i: (i, 0)),
            pl.BlockSpec((tm, w), lambda i: (i, 0)),
            pl.BlockSpec((w, w), lambda i: (0, 0)),
            pl.BlockSpec((1, w), lambda i: (0, 0)),
            pl.BlockSpec((1, w), lambda i: (0, 0)),
        ],
        out_specs=[
            pl.BlockSpec((tm, w), lambda i: (i, 0)),
            pl.BlockSpec((w, w), lambda i: (0, 0)),
            pl.BlockSpec((1, w), lambda i: (0, 0)),
            pl.BlockSpec((1, w), lambda i: (0, 0)),
        ],
        out_shape=[
            jax.ShapeDtypeStruct((s, w), F32),
            jax.ShapeDtypeStruct((w, w), BF16),
            jax.ShapeDtypeStruct((1, w), F32),
            jax.ShapeDtypeStruct((1, w), F32),
        ],
        scratch_shapes=[pltpu.VMEM((w, w), F32)],
        compiler_params=_params(1),
        name="glu_bwd",
    )


def _ssm_bwd_state(dy, v2, al, xprev):
    s = dy.shape[0]
    nt, nc, ns = xprev.shape
    hs = ns // 2

    def body(dy_ref, v_ref, al_ref, xp_ref, dys_ref, g_ref, dal_ref, dxp, gs):
        for q in range(PAIRS):
            dys_ref[q, :, 0:LANES] = dy_ref[pl.ds(2 * q, nc, stride=CHUNK), :].astype(BF16)
            dys_ref[q, :, LANES : 2 * LANES] = dy_ref[pl.ds(2 * q + 1, nc, stride=CHUNK), :].astype(BF16)
        acc = _dot(dys_ref[0], v_ref[0])
        for r in range(1, PAIRS):
            acc = acc + _dot(dys_ref[r], v_ref[r])
        dxp[...] = acc
        ar = al_ref[0:1, :]
        ai = al_ref[1:2, :]
        zero = jnp.zeros((1, hs), F32)
        gs[pl.ds(nc - 1, 1), 0:hs] = zero
        gs[pl.ds(nc - 1, 1), hs:ns] = zero

        def step(n, carry):
            gr, gi = carry
            c = nc - 2 - n
            nr = dxp[pl.ds(c + 1, 1), 0:hs] + ar * gr + ai * gi
            ni = dxp[pl.ds(c + 1, 1), hs:ns] + ar * gi - ai * gr
            gs[pl.ds(c, 1), 0:hs] = nr
            gs[pl.ds(c, 1), hs:ns] = ni
            return nr, ni

        lax.fori_loop(0, nc - 1, step, (zero, zero))
        gv = gs[...]
        xv = xp_ref[...]
        gr, gi = gv[:, 0:hs], gv[:, hs:ns]
        xr, xi = xv[:, 0:hs], xv[:, hs:ns]
        dal_ref[0:1, :] = jnp.sum(gr * xr + gi * xi, axis=0, keepdims=True)
        dal_ref[1:2, :] = jnp.sum(gi * xr - gr * xi, axis=0, keepdims=True)
        g_ref[...] = gv.astype(BF16)

    return _call(
        body,
        grid=(nt,),
        in_specs=[
            pl.BlockSpec((s, LANES), lambda t: (0, t)),
            pl.BlockSpec((None, PAIRS, 2 * LANES, ns), lambda t: (t, 0, 0, 0)),
            pl.BlockSpec((None, 2, hs), lambda t: (t, 0, 0)),
            pl.BlockSpec((None, nc, ns), lambda t: (t, 0, 0)),
        ],
        out_specs=[
            pl.BlockSpec((None, PAIRS, nc, 2 * LANES), lambda t: (t, 0, 0, 0)),
            pl.BlockSpec((None, nc, ns), lambda t: (t, 0, 0)),
            pl.BlockSpec((None, 2, hs), lambda t: (t, 0, 0)),
        ],
        out_shape=[
            jax.ShapeDtypeStruct((nt, PAIRS, nc, 2 * LANES), BF16),
            jax.ShapeDtypeStruct((nt, nc, ns), BF16),
            jax.ShapeDtypeStruct((nt, 2, hs), F32),
        ],
        scratch_shapes=[pltpu.VMEM((nc, ns), F32), pltpu.VMEM((nc, ns), F32)],
        compiler_params=_params(1),
        name="ssm_bwd_state",
    )(dy, v2, al, xprev)


def _ssm_bwd_main(us, dys, k2, w2, xprev, gst, consts, s, width, exs):
    nt, _, nc, _ = us.shape
    ns = xprev.shape[-1]
    rep_k, mask_k, rep_s, mask_s = consts
    nk = 2 * LANES
    gg = GROUPS_PER_TILE
    rows_of = [(gi, st, slice((st * gg + gi) * SSM_GROUP, (st * gg + gi + 1) * SSM_GROUP))
               for st in range(2) for gi in range(gg)]

    def body(us_ref, dys_ref, k_ref, w_ref, xp_ref, g_ref, rk_ref, mk_ref, rs_ref, ms_ref,
             du_ref, dk_ref, dw_ref, dv_ref, acc, du2, duf, dkf):
        q = pl.program_id(1)

        @pl.when(q == 0)
        def _():
            dkf[...] = jnp.zeros_like(dkf)

        usq = us_ref[q]
        gb = g_ref[...]
        dw = _gather_blocks(_dot_tn(usq, gb), ms_ref[...], rs_ref[...])
        dv = _gather_blocks(_dot_tn(dys_ref[q], xp_ref[...].astype(BF16)), ms_ref[...], rs_ref[...])
        for gi, st, rows in rows_of:
            dw_ref[gi, st] = dw[rows]
            dv_ref[gi, st] = dv[rows]
        acc[...] = _dot_nt(gb, w_ref[...])
        for m in range(PAIRS):

            @pl.when(q + m < PAIRS)
            def _():
                dyr = dys_ref[jnp.minimum(q + m, PAIRS - 1)]
                acc[...] += _dot_nt(dyr, k_ref[m])
                dkf[m] += _dot_tn(usq, dyr)

        du2[q] = acc[...]

        @pl.when(q == PAIRS - 1)
        def _():
            for p in range(PAIRS):
                duf[pl.ds(2 * p, nc, stride=CHUNK), :] = du2[p, :, 0:LANES]
                duf[pl.ds(2 * p + 1, nc, stride=CHUNK), :] = du2[p, :, LANES : 2 * LANES]
                dk = _gather_blocks(dkf[p], mk_ref[...], rk_ref[...])
                for gi, st, rows in rows_of:
                    dk_ref[gi, p, st] = dk[rows]
            du_ref[...] = duf[...].astype(BF16)

    def per_tile(rows, cols):
        return pl.BlockSpec((None, PAIRS, rows, cols), lambda t, q: (t, 0, 0, 0))

    def per_pair(rows, cols):
        return pl.BlockSpec((None, None, rows, cols), lambda t, q: (t, q, 0, 0))

    def whole(a):
        return pl.BlockSpec(a.shape, lambda t, q: (0, 0))

    return _call_riding(
        body,
        exs,
        args=(us, dys, k2, w2, xprev, gst, rep_k, mask_k, rep_s, mask_s),
        grid=(nt, PAIRS),
        in_specs=[
            per_tile(nc, nk),
            per_tile(nc, nk),
            per_tile(nk, nk),
            per_pair(nk, ns),
            pl.BlockSpec((None, nc, ns), lambda t, q: (t, 0, 0)),
            pl.BlockSpec((None, nc, ns), lambda t, q: (t, 0, 0)),
            whole(rep_k),
            whole(mask_k),
            whole(rep_s),
            whole(mask_s),
        ],
        out_specs=[
            pl.BlockSpec((s, LANES), lambda t, q: (0, t)),
            pl.BlockSpec((gg, PAIRS, 2, SSM_GROUP, LANES), lambda t, q: (t, 0, 0, 0, 0)),
            pl.BlockSpec((gg, None, 2, SSM_GROUP, LANES), lambda t, q: (t, q, 0, 0, 0)),
            pl.BlockSpec((gg, None, 2, SSM_GROUP, LANES), lambda t, q: (t, q, 0, 0, 0)),
        ],
        out_shape=[
            jax.ShapeDtypeStruct((s, width), BF16),
            jax.ShapeDtypeStruct((nt * gg, PAIRS, 2, SSM_GROUP, LANES), F32),
            jax.ShapeDtypeStruct((nt * gg, PAIRS, 2, SSM_GROUP, LANES), F32),
            jax.ShapeDtypeStruct((nt * gg, PAIRS, 2, SSM_GROUP, LANES), F32),
        ],
        scratch_shapes=[
            pltpu.VMEM((nc, nk), F32),
            pltpu.VMEM((PAIRS, nc, nk), F32),
            pltpu.VMEM((s, LANES), F32),
            pltpu.VMEM((PAIRS, nk, nk), F32),
        ],
        compiler_params=_params(2),
        name="ssm_bwd_main",
    )


def _sgu_bwd(z, dmix, dz, lng, lnb, wt, wtt, biasb, go, rb, exs):
    s = z.shape[0]
    w = lng.shape[-1]
    heads = wt.shape[0]
    ncol = (LANES, LANES)

    def body(zu_ref, zv_ref, dm_ref, dz_any, lng_ref, lnb_ref, wt_ref, wtt_ref, b_ref, go_ref,
             dz_ref, dw_ref, db_ref, dlg_ref, dlb_ref, dgo_ref, dzv):
        del dz_any
        i = pl.program_id(0)
        p = pl.program_id(1)

        @pl.when(jnp.logical_and(i == 0, p == 0))
        def _():
            dw_ref[...] = jnp.zeros_like(dw_ref)
            db_ref[...] = jnp.zeros_like(db_ref)
            dlg_ref[...] = jnp.zeros_like(dlg_ref)
            dlb_ref[...] = jnp.zeros_like(dlb_ref)
            dgo_ref[...] = jnp.zeros_like(dgo_ref)

        @pl.when(p == 0)
        def _():
            lng_v = lng_ref[...]
            for ck in range(rb // SGU_CHUNK):
                rows = slice(ck * SGU_CHUNK, (ck + 1) * SGU_CHUNK)
                zu = zu_ref[rows, :]
                zv = zv_ref[rows, :]
                u, vhat, rstd, vb, mix, u_grad, v_grad = _sgu_parts(zu, zv, lng_v, lnb_ref[...], wt_ref, b_ref)
                y = u * mix
                dy, dgt = _rms_bwd(dm_ref[rows, :], y, _rms(y), go_ref[...])
                dgo_ref[...] += jnp.sum(dgt, axis=0, keepdims=True)
                du = dy * mix
                dmx = dy * u
                dmb = dmx.astype(BF16)
                dvl = []
                for h in range(heads):
                    cols = slice(h * LANES, (h + 1) * LANES)
                    db_ref[h] += jnp.broadcast_to(jnp.sum(dmx[:, cols], axis=-1, keepdims=True), ncol)
                    dw_ref[h] += _dot_nt(dmb[:, cols], vb[:, cols])
                    dvl.append(_dot(wtt_ref[h], dmb[:, cols]))
                dvln = jnp.concatenate(dvl, axis=1)
                dlg_ref[...] += jnp.sum(dvln * vhat, axis=0, keepdims=True)
                dlb_ref[...] += jnp.sum(dvln, axis=0, keepdims=True)
                dvh = dvln * lng_v
                dv = rstd * (
                    dvh
                    - jnp.mean(dvh, axis=-1, keepdims=True)
                    - vhat * jnp.mean(dvh * vhat, axis=-1, keepdims=True)
                )
                dz_ref[rows, :] = (du * u_grad).astype(BF16)
                dzv[rows, :] = (dv * v_grad).astype(BF16)

        @pl.when(p == 1)
        def _():
            dz_ref[...] = dzv[...]

    return _call_riding(
        body,
        exs,
        args=(z, z, dmix, dz, lng, lnb, wt, wtt, biasb, go),
        grid=(s // rb, 2),
        in_specs=[
            pl.BlockSpec((rb, w), lambda i, p: (i, 1)),
            pl.BlockSpec((rb, w), lambda i, p: (i, 2)),
            pl.BlockSpec((rb, w), lambda i, p: (i, 1)),
            pl.BlockSpec(memory_space=pl.ANY),
            pl.BlockSpec((1, w), lambda i, p: (0, 0)),
            pl.BlockSpec((1, w), lambda i, p: (0, 0)),
            pl.BlockSpec((heads, SGU_CHUNK, SGU_CHUNK), lambda i, p: (0, 0, 0)),
            pl.BlockSpec((heads, SGU_CHUNK, SGU_CHUNK), lambda i, p: (0, 0, 0)),
            pl.BlockSpec((heads, SGU_CHUNK, LANES), lambda i, p: (0, 0, 0)),
            pl.BlockSpec((1, w), lambda i, p: (0, 0)),
        ],
        out_specs=[
            pl.BlockSpec((rb, w), lambda i, p: (i, 1 + p)),
            pl.BlockSpec((heads, SGU_CHUNK, SGU_CHUNK), lambda i, p: (0, 0, 0)),
            pl.BlockSpec((heads, SGU_CHUNK, LANES), lambda i, p: (0, 0, 0)),
            pl.BlockSpec((1, w), lambda i, p: (0, 0)),
            pl.BlockSpec((1, w), lambda i, p: (0, 0)),
            pl.BlockSpec((1, w), lambda i, p: (0, 0)),
        ],
        out_shape=[
            jax.ShapeDtypeStruct(dz.shape, dz.dtype),
            jax.ShapeDtypeStruct((heads, SGU_CHUNK, SGU_CHUNK), F32),
            jax.ShapeDtypeStruct((heads, SGU_CHUNK, LANES), F32),
            jax.ShapeDtypeStruct((1, w), F32),
            jax.ShapeDtypeStruct((1, w), F32),
            jax.ShapeDtypeStruct((1, w), F32),
        ],
        scratch_shapes=[pltpu.VMEM((rb, w), BF16)],
        input_output_aliases={3: 0},
        compiler_params=_params(2),
        name="sgu_bwd",
    )


def _in_proj_bwd(dz, w4, x, dres, g, tm, exs, tiles, into, name):
    s, d = x.shape
    nj, _, cw = w4.shape
    first, stop = tiles

    def body(dz_ref, w_ref, x_ref, dr_ref, g_ref, *rest):
        dx_ref, dg_ref, acc = rest[-3:]
        i = pl.program_id(0)
        j = pl.program_id(1)

        @pl.when(j == 0)
        def _():
            acc[...] = jnp.zeros_like(acc)

        @pl.when(jnp.logical_and(i == 0, j == 0))
        def _():
            dg_ref[...] = jnp.zeros_like(dg_ref)

        acc[...] += _dot_nt(dz_ref[...], w_ref[...])

        @pl.when(j == nj - 1)
        def _():
            xv = x_ref[...]
            dx, dgt = _rms_bwd(acc[...], xv, _rms(xv), g_ref[...])
            dx_ref[...] = dr_ref[...] + dx
            dg_ref[...] += jnp.sum(dgt, axis=0, keepdims=True)

    row = pl.BlockSpec((tm, d), lambda i, j: (i + first, 0))
    vec = pl.BlockSpec((1, d), lambda i, j: (0, 0))
    kept = [] if into is None else [into]
    return _call_riding(
        body,
        exs,
        args=(dz, w4, x, dres, g, *kept),
        scratch_shapes=[pltpu.VMEM((tm, d), F32)],
        grid=(stop - first, nj),
        in_specs=[
            pl.BlockSpec((tm, cw), lambda i, j: (i + first, j)),
            pl.BlockSpec((None, d, cw), lambda i, j: (j, 0, 0)),
            row,
            row,
            vec,
        ]
        + [_ANY] * len(kept),
        out_specs=[row, vec],
        out_shape=[jax.ShapeDtypeStruct((s, d), F32), jax.ShapeDtypeStruct((1, d), F32)],
        input_output_aliases={5: 0} if kept else {},
        compiler_params=_params(2),
        name=name,
    )


_ANY = pl.BlockSpec(memory_space=pl.ANY)


def _position():
    x, y, c = lax.axis_index("x"), lax.axis_index("y"), lax.axis_index("c")
    return x, y, c, [(1 - x, y), (x, 1 - y), (1 - x, 1 - y)]


def _remote(src, dst, send_sems, recv_sems, k, to):
    return pltpu.make_async_remote_copy(
        src_ref=src, dst_ref=dst, send_sem=send_sems.at[k], recv_sem=recv_sems.at[k], device_id=to, device_id_type=MESH
    )


class _Riding:
    def __init__(self, srcs, out_shapes, n_sems, start, finish, kept=()):
        self.srcs, self.out_shapes, self.n_sems, self.start, self.finish = srcs, out_shapes, n_sems, start, finish
        self.kept = kept


def _call_riding(body, exs, *, grid, in_specs, out_specs, out_shape, scratch_shapes, args, **kw):
    n_in, n_out, n_scr = len(in_specs), len(out_specs), len(scratch_shapes)
    spec = dict(grid=grid, in_specs=list(in_specs), out_specs=list(out_specs), out_shape=list(out_shape))
    if not exs:
        return _call(body, scratch_shapes=list(scratch_shapes), **spec, **kw)(*args), []
    srcs = [a for ex in exs for a in ex.srcs]
    lands = [a for ex in exs for a in ex.out_shapes]
    xi, xo = len(srcs), len(lands)

    def fused(*refs):
        cin, xin = refs[:n_in], refs[n_in : n_in + xi]
        o = n_in + xi
        cout, xout = refs[o : o + n_out], refs[o + n_out : o + n_out + xo]
        scr = refs[o + n_out + xo :]
        ids = [pl.program_id(a) for a in range(len(grid))]
        first = functools.reduce(jnp.logical_and, [i == 0 for i in ids])
        last = functools.reduce(jnp.logical_and, [i == n - 1 for i, n in zip(ids, grid)])

        def each(half):
            si = so = 0
            for e, ex in enumerate(exs):
                ni, no = len(ex.srcs), len(ex.out_shapes)
                sems = scr[n_scr + 2 * e : n_scr + 2 * e + 2]
                getattr(ex, half)(xin[si : si + ni], xout[so : so + no], *sems)
                si, so = si + ni, so + no

        @pl.when(first)
        def _():
            each("start")

        body(*cin, *cout, *scr[:n_scr])

        @pl.when(last)
        def _():
            each("finish")

    sems = [pltpu.SemaphoreType.DMA((ex.n_sems,)) for ex in exs for _ in range(2)]
    aliases = dict(kw.pop("input_output_aliases", {}))
    si = so = 0
    for ex in exs:
        for a, b in ex.kept:
            aliases[n_in + si + a] = n_out + so + b
        si, so = si + len(ex.srcs), so + len(ex.out_shapes)
    kw["input_output_aliases"] = aliases
    spec["in_specs"] += [_ANY] * xi
    spec["out_specs"] += [_ANY] * xo
    spec["out_shape"] += lands
    outs = _call(fused, scratch_shapes=list(scratch_shapes) + sems, **spec, **kw)(*args, *srcs)
    return outs[:n_out], outs[n_out:]


def _gather_one(shard, peers=(0, 1, 2), into=None):
    def own_copies(ins, outs, send, recv):
        x, y, c, chips = _position()
        me = 2 * x + y
        sib = (x, y, 1 - c)
        cps = [] if into is not None else [_remote(ins[0].at[c], outs[0].at[me, c], send, recv, 0, sib)]
        cps += [_remote(ins[0].at[c], outs[0].at[me, c], send, recv, 1 + k, (*chips[k], c)) for k in peers]
        return cps, (c, me, sib, chips)

    def start(ins, outs, send, recv):
        for cp in own_copies(ins, outs, send, recv)[0]:
            cp.start()

    def finish(ins, outs, send, recv):
        cps, (c, me, sib, chips) = own_copies(ins, outs, send, recv)
        for k in peers:
            px, py = chips[k]
            landed = outs[0].at[2 * px + py, c]
            _remote(ins[0].at[c], landed, send, recv, 1 + k, (px, py, c)).wait_recv()
            cps.append(_remote(landed, landed, send, recv, 4 + k, sib))
            cps[-1].start()
        if into is None:
            _remote(ins[0].at[1 - c], outs[0].at[me, 1 - c], send, recv, 0, sib).wait_recv()
        for k in peers:
            px, py = chips[k]
            theirs = outs[0].at[2 * px + py, 1 - c]
            _remote(theirs, theirs, send, recv, 4 + k, sib).wait_recv()
        for cp in cps:
            cp.wait_send()

    land = jax.ShapeDtypeStruct((N_CHIPS,) + shard.shape, shard.dtype)
    return _Riding([shard] + ([] if into is None else [into]), [land], 7, start, finish, [] if into is None else [(1, 0)])


def _to_owner_one(p, peers=(0, 1, 2), into=None):
    def copies(ins, outs, send, recv):
        x, y, c, chips = _position()
        return [_remote(ins[0].at[2 * chips[k][0] + chips[k][1]], outs[0].at[k], send, recv, k, (*chips[k], c)) for k in peers]

    def start(ins, outs, send, recv):
        for cp in copies(ins, outs, send, recv):
            cp.start()

    def finish(ins, outs, send, recv):
        cps = copies(ins, outs, send, recv)
        for cp in cps:
            cp.wait_recv()
        for cp in cps:
            cp.wait_send()

    land = jax.ShapeDtypeStruct((3,) + p.shape[1:], p.dtype)
    return _Riding([p] + ([] if into is None else [into]), [land], 3, start, finish, [] if into is None else [(1, 0)])


def _to_sibling_one(g):
    def copies(ins, outs, send, recv):
        x, y, c, _ = _position()
        return [_remote(ins[0].at[j, 1 - c], outs[0].at[j], send, recv, j, (x, y, 1 - c)) for j in range(N_CHIPS)]

    def start(ins, outs, send, recv):
        for cp in copies(ins, outs, send, recv):
            cp.start()

    def finish(ins, outs, send, recv):
        cps = copies(ins, outs, send, recv)
        for cp in cps:
            cp.wait_recv()
        for cp in cps:
            cp.wait_send()

    return _Riding([g], [jax.ShapeDtypeStruct((g.shape[0],) + g.shape[2:], g.dtype)], N_CHIPS, start, finish)


def _riding(copies, srcs, lands, n_sems, kept=()):
    def start(ins, outs, send, recv):
        for cp in copies(ins, outs, send, recv):
            cp.start()

    def finish(ins, outs, send, recv):
        cps = copies(ins, outs, send, recv)
        for cp in cps:
            cp.wait_recv()
        for cp in cps:
            cp.wait_send()

    return _Riding(srcs, lands, n_sems, start, finish, kept)


def _to_owner_many(ps):
    n = len(ps)

    def copies(ins, outs, send, recv):
        x, y, c, chips = _position()
        return [_remote(ins[i].at[2 * px + py], outs[i].at[k], send, recv, k * n + i, (px, py, c))
                for k, (px, py) in enumerate(chips) for i in range(n)]

    return _riding(copies, list(ps), [jax.ShapeDtypeStruct((3,) + p.shape[1:], p.dtype) for p in ps], 3 * n)


def _share_many(fs):
    n = len(fs)

    def copies(ins, outs, send, recv):
        x, y, c, _ = _position()
        return [_remote(ins[i].at[c], outs[i].at[c], send, recv, i, (x, y, 1 - c)) for i in range(n)]

    def finish(ins, outs, send, recv):
        x, y, c, _ = _position()
        for i in range(n):
            _remote(ins[i].at[1 - c], outs[i].at[1 - c], send, recv, i, (x, y, 1 - c)).wait_recv()
        for cp in copies(ins, outs, send, recv):
            cp.wait_send()

    rider = _riding(copies, list(fs), [jax.ShapeDtypeStruct(f.shape, f.dtype) for f in fs], n, [(i, i) for i in range(n)])
    rider.finish = finish
    return rider


def _gather_chips_one(a):
    def copies(ins, outs, send, recv):
        x, y, c, chips = _position()
        return [_remote(ins[0], outs[0].at[2 * x + y], send, recv, k, (px, py, c)) for k, (px, py) in enumerate(chips)]

    def start(ins, outs, send, recv):
        x, y, c, _ = _position()
        pltpu.make_async_copy(ins[0], outs[0].at[2 * x + y], send.at[3]).start()
        for cp in copies(ins, outs, send, recv):
            cp.start()

    def finish(ins, outs, send, recv):
        x, y, c, chips = _position()
        for k, (px, py) in enumerate(chips):
            _remote(ins[0], outs[0].at[2 * px + py], send, recv, k, (px, py, c)).wait_recv()
        for cp in copies(ins, outs, send, recv):
            cp.wait_send()
        pltpu.make_async_copy(ins[0], outs[0].at[2 * x + y], send.at[3]).wait()

    return _Riding([a], [jax.ShapeDtypeStruct((N_CHIPS,) + a.shape, a.dtype)], 4, start, finish)


def _pair_exchange(gs, name):
    n = len(gs)

    def body(*refs):
        ins, outs = refs[:n], refs[n : 2 * n]
        send_sems, recv_sems = refs[2 * n :]
        x, y, c, _ = _position()
        sib = (x, y, 1 - c)
        cps = []
        for i in range(n):
            for j in range(N_CHIPS):
                cps.append(_remote(ins[i].at[j, 1 - c], outs[i].at[j], send_sems, recv_sems, i * N_CHIPS + j, sib))
                cps[-1].start()
        for cp in cps:
            cp.wait_recv()
        for cp in cps:
            cp.wait_send()

    return _call(
        body,
        in_specs=[_ANY] * n,
        out_specs=[_ANY] * n,
        out_shape=[jax.ShapeDtypeStruct((g.shape[0],) + g.shape[2:], g.dtype) for g in gs],
        scratch_shapes=[pltpu.SemaphoreType.DMA((n * N_CHIPS,)), pltpu.SemaphoreType.DMA((n * N_CHIPS,))],
        name=name,
    )(*gs)


def _place_own(shard, gathered, pos, name):
    _, hr, cols = shard.shape
    tr = _row_tile(hr, 512)

    def body(pos_ref, s_ref, g_any, o_ref):
        del pos_ref, g_any
        o_ref[...] = s_ref[...]

    return _call(
        body,
        grid_spec=pltpu.PrefetchScalarGridSpec(
            num_scalar_prefetch=1,
            grid=(hr // tr,),
            in_specs=[pl.BlockSpec((None, tr, cols), lambda i, p: (p[1], i, 0)), _ANY],
            out_specs=pl.BlockSpec((None, None, tr, cols), lambda i, p: (p[0], p[1], i, 0)),
        ),
        out_shape=jax.ShapeDtypeStruct(gathered.shape, gathered.dtype),
        input_output_aliases={2: 0},
        compiler_params=_params(1),
        name=name,
    )(pos, shard, gathered)


def _pair_share(fs, name):
    n = len(fs)

    def body(*refs):
        ins, outs = refs[:n], refs[n : 2 * n]
        send_sems, recv_sems = refs[2 * n :]
        x, y, c, _ = _position()
        sib = (x, y, 1 - c)
        sends = [_remote(ins[i].at[c], outs[i].at[c], send_sems, recv_sems, i, sib) for i in range(n)]
        for cp in sends:
            cp.start()
        for i in range(n):
            _remote(ins[i].at[1 - c], outs[i].at[1 - c], send_sems, recv_sems, i, sib).wait_recv()
        for cp in sends:
            cp.wait_send()

    return _call(
        body,
        in_specs=[_ANY] * n,
        out_specs=[_ANY] * n,
        out_shape=[jax.ShapeDtypeStruct(f.shape, f.dtype) for f in fs],
        input_output_aliases={i: i for i in range(n)},
        scratch_shapes=[pltpu.SemaphoreType.DMA((n,)), pltpu.SemaphoreType.DMA((n,))],
        name=name,
    )(*fs)


def _pair_sum(g, r, core, name):
    nj, _, hr, cols = g.shape
    tr = _row_tile(hr, 256)

    def body(c_ref, g_ref, r_ref, o_ref):
        del c_ref
        o_ref[...] = (g_ref[...].astype(F32) + r_ref[...].astype(F32)).astype(o_ref.dtype)

    return _call(
        body,
        grid_spec=pltpu.PrefetchScalarGridSpec(
            num_scalar_prefetch=1,
            grid=(nj, hr // tr),
            in_specs=[
                pl.BlockSpec((None, None, tr, cols), lambda j, i, c: (j, c[0], i, 0)),
                pl.BlockSpec((None, tr, cols), lambda j, i, c: (j, i, 0)),
            ],
            out_specs=pl.BlockSpec((None, tr, cols), lambda j, i, c: (j, i, 0)),
        ),
        out_shape=jax.ShapeDtypeStruct((nj, hr, cols), g.dtype),
        compiler_params=_params(2),
        name=name,
    )(core, g, r)


def _chip_sum(p, q, pos, name):
    nq, hr, cols = q.shape
    tr = _row_tile(hr, 256)

    def body(pos_ref, p_ref, *refs):
        del pos_ref
        o_ref = refs[nq]
        tot = p_ref[...].astype(F32)
        for k in range(nq):
            tot = tot + refs[k][...].astype(F32)
        o_ref[...] = tot

    return _call(
        body,
        grid_spec=pltpu.PrefetchScalarGridSpec(
            num_scalar_prefetch=1,
            grid=(hr // tr,),
            in_specs=[pl.BlockSpec((None, tr, cols), lambda i, s: (s[0], i, 0))]
            + [pl.BlockSpec((None, tr, cols), functools.partial(lambda k, i, s: (k, i, 0), k)) for k in range(nq)],
            out_specs=pl.BlockSpec((None, tr, cols), lambda i, s: (s[1], i, 0)),
        ),
        out_shape=jax.ShapeDtypeStruct((2, hr, cols), F32),
        compiler_params=_params(1),
        name=name,
    )(pos, p, *([q] * nq))


def _adamw(w, g, m, v, name, exs=()):
    rows, cols = w.shape
    tr = _row_tile(rows, max(8, (2**18) // cols), mult=8)
    c1 = 1.0 - ADAM_B1**ADAM_STEP
    c2 = 1.0 - ADAM_B2**ADAM_STEP

    def body(w_ref, g_ref, m_ref, v_ref, gk_ref, d_ref, nm_ref, nv_ref):
        gv = g_ref[...]
        gk_ref[...] = gv
        nm = ADAM_B1 * m_ref[...] + (1.0 - ADAM_B1) * gv
        nv = ADAM_B2 * v_ref[...] + (1.0 - ADAM_B2) * (gv * gv)
        nm_ref[...] = nm
        nv_ref[...] = nv
        d_ref[...] = -ADAM_LR * ((nm / c1) / (jnp.sqrt(nv / c2) + ADAM_EPS) + ADAM_WD * w_ref[...])

    spec = pl.BlockSpec((tr, cols), lambda i: (i, 0))
    sds = jax.ShapeDtypeStruct((rows, cols), F32)
    outs, landed = _call_riding(
        body,
        exs,
        args=(w, g, m, v),
        scratch_shapes=[],
        grid=(rows // tr,),
        in_specs=[spec] * 4,
        out_specs=[spec] * 4,
        out_shape=[sds] * 4,
        compiler_params=_params(1),
        name=name,
    )
    return (*outs, *landed)


_TILE_ELEMS = 8 * LANES
_FLAT_ROW_MULT = 8 * 2 * N_CHIPS


def _flat_rows(shape):
    n = math.prod(shape)
    return (n + _TILE_ELEMS - 1) // _TILE_ELEMS * 8


def _pack(arrs):
    parts = []
    for a in arrs:
        rows = _flat_rows(a.shape)
        flat = a.reshape(-1).astype(F32)
        flat = jnp.pad(flat, (0, rows * LANES - flat.shape[0]))
        parts.append(flat.reshape(rows, LANES))
    total = sum(p.shape[0] for p in parts)
    pad = -total % _FLAT_ROW_MULT
    if pad:
        parts.append(jnp.zeros((pad, LANES), F32))
    return jnp.concatenate(parts, axis=0)


def _unpack(flat, shapes):
    out, row = [], 0
    for shp in shapes:
        rows = _flat_rows(shp)
        out.append(flat[row : row + rows].reshape(-1)[: math.prod(shp)].reshape(shp))
        row += rows
    return out


def kernel(x, norm_mix_g, w_in, ssm_a_re, ssm_a_im, ssm_b_re, ssm_b_im, ssm_c_re, ssm_c_im, ssm_d, ssm_log_dt, ssm_glu_w, ssm_glu_b, sgu_ln_g, sgu_ln_b, sgu_w, sgu_b, out_norm_ssm_g, out_norm_sgu_g, w_out, norm_mlp_g, w_up, w_down, norm_final_g, loss_target, m_norm_mix_g, m_w_in, m_ssm_a_re, m_ssm_a_im, m_ssm_b_re, m_ssm_b_im, m_ssm_c_re, m_ssm_c_im, m_ssm_d, m_ssm_log_dt, m_ssm_glu_w, m_ssm_glu_b, m_sgu_ln_g, m_sgu_ln_b, m_sgu_w, m_sgu_b, m_out_norm_ssm_g, m_out_norm_sgu_g, m_w_out, m_norm_mlp_g, m_w_up, m_w_down, m_norm_final_g, v_norm_mix_g, v_w_in, v_ssm_a_re, v_ssm_a_im, v_ssm_b_re, v_ssm_b_im, v_ssm_c_re, v_ssm_c_im, v_ssm_d, v_ssm_log_dt, v_ssm_glu_w, v_ssm_glu_b, v_sgu_ln_g, v_sgu_ln_b, v_sgu_w, v_sgu_b, v_out_norm_ssm_g, v_out_norm_sgu_g, v_w_out, v_norm_mlp_g, v_w_up, v_w_down, v_norm_final_g):
    weights = dict(norm_mix_g=norm_mix_g, w_in=w_in, ssm_a_re=ssm_a_re, ssm_a_im=ssm_a_im, ssm_b_re=ssm_b_re, ssm_b_im=ssm_b_im, ssm_c_re=ssm_c_re, ssm_c_im=ssm_c_im, ssm_d=ssm_d, ssm_log_dt=ssm_log_dt, ssm_glu_w=ssm_glu_w, ssm_glu_b=ssm_glu_b, sgu_ln_g=sgu_ln_g, sgu_ln_b=sgu_ln_b, sgu_w=sgu_w, sgu_b=sgu_b, out_norm_ssm_g=out_norm_ssm_g, out_norm_sgu_g=out_norm_sgu_g, w_out=w_out, norm_mlp_g=norm_mlp_g, w_up=w_up, w_down=w_down, norm_final_g=norm_final_g)
    mom_m = dict(norm_mix_g=m_norm_mix_g, w_in=m_w_in, ssm_a_re=m_ssm_a_re, ssm_a_im=m_ssm_a_im, ssm_b_re=m_ssm_b_re, ssm_b_im=m_ssm_b_im, ssm_c_re=m_ssm_c_re, ssm_c_im=m_ssm_c_im, ssm_d=m_ssm_d, ssm_log_dt=m_ssm_log_dt, ssm_glu_w=m_ssm_glu_w, ssm_glu_b=m_ssm_glu_b, sgu_ln_g=m_sgu_ln_g, sgu_ln_b=m_sgu_ln_b, sgu_w=m_sgu_w, sgu_b=m_sgu_b, out_norm_ssm_g=m_out_norm_ssm_g, out_norm_sgu_g=m_out_norm_sgu_g, w_out=m_w_out, norm_mlp_g=m_norm_mlp_g, w_up=m_w_up, w_down=m_w_down, norm_final_g=m_norm_final_g)
    mom_v = dict(norm_mix_g=v_norm_mix_g, w_in=v_w_in, ssm_a_re=v_ssm_a_re, ssm_a_im=v_ssm_a_im, ssm_b_re=v_ssm_b_re, ssm_b_im=v_ssm_b_im, ssm_c_re=v_ssm_c_re, ssm_c_im=v_ssm_c_im, ssm_d=v_ssm_d, ssm_log_dt=v_ssm_log_dt, ssm_glu_w=v_ssm_glu_w, ssm_glu_b=v_ssm_glu_b, sgu_ln_g=v_sgu_ln_g, sgu_ln_b=v_sgu_ln_b, sgu_w=v_sgu_w, sgu_b=v_sgu_b, out_norm_ssm_g=v_out_norm_ssm_g, out_norm_sgu_g=v_out_norm_sgu_g, w_out=v_w_out, norm_mlp_g=v_norm_mlp_g, w_up=v_w_up, w_down=v_w_down, norm_final_g=v_norm_final_g)
    names = list(weights)
    large = ["w_in", "ssm_glu_w", "w_out", "w_up", "w_down"]
    small = [n for n in names if n not in large]

    s, d = x.shape[1], x.shape[2]
    xs = x.reshape(s, d)
    target = loss_target.reshape(s, d)
    width = ssm_glu_w.shape[-1]
    ff = w_down.shape[1] * N_CHIPS
    tm = min(512, s)
    core = lax.axis_index("c").astype(jnp.int32).reshape(1)
    chip = (2 * lax.axis_index("x") + lax.axis_index("y")).astype(jnp.int32).reshape(1)
    pos = jnp.concatenate([chip, core])

    shards = [w[0].astype(BF16).reshape(2, w.shape[1] // 2, w.shape[2]) for w in (w_in, ssm_glu_w, w_out, w_up, w_down)]
    ssm_args = (ssm_a_re[0], ssm_a_im[0], ssm_b_re[0], ssm_b_im[0], ssm_c_re[0], ssm_c_im[0], ssm_d[0], ssm_log_dt[0])
    (k2c, w2c, v2c, al), ssm_vjp = jax.vjp(_ssm_mats, *ssm_args)
    consts = _spread_consts(ssm_b_re.shape[-1], ssm_a_re.shape[-1])
    (k2b, w2b, v2b), (w_in_g,) = _ssm_spread(k2c, w2c, v2c, consts, [_gather_one(shards[0])])
    w_in4 = _place_own(shards[0], w_in_g, pos, "place_w_in").reshape(N_CHIPS, d, w_in.shape[2])
    causal = jnp.tril(jnp.ones((SGU_CHUNK, SGU_CHUNK), dtype=bool))
    wt = jnp.where(causal[None], sgu_w[0], 0.0)
    wtb = wt.astype(BF16)
    wttb = jnp.swapaxes(wt, 1, 2).astype(BF16)
    heads = sgu_w.shape[1]
    biasb = jnp.broadcast_to(sgu_b[0][:, :, None], (heads, SGU_CHUNK, LANES))

    tm_big = min(1024, s)
    (z, h1b), (glu_g, out_g, up_g) = _in_proj(
        xs, norm_mix_g, w_in4, tm_big,
        [_gather_one(shards[1]), _gather_one(shards[2]), _gather_one(shards[3], peers=(0, 1))])
    wg_full = _place_own(shards[1], glu_g, pos, "place_glu_w").reshape(width, width)
    w_out_full = _place_own(shards[2], out_g, pos, "place_w_out").reshape(d, d)
    (y_pre, xprev, us), (up_g,) = _ssm_fwd(z, k2b, w2b, v2b, al, [_gather_one(shards[3], peers=(2,), into=up_g)])
    up4 = _place_own(shards[3], up_g, pos, "place_w_up").reshape(N_CHIPS, d, w_up.shape[2])
    mixed = _glu_fwd(y_pre, wg_full, ssm_glu_b, out_norm_ssm_g, d, tm)
    mixed = _sgu_fwd(z, mixed, sgu_ln_g, sgu_ln_b, wtb, biasb, out_norm_sgu_g, tm)
    x1 = _out_proj(xs, mixed, w_out_full, tm)
    tf = min(1024, up4.shape[-1])
    (up, h2b), (down_g,) = _mlp_up(x1, norm_mlp_g, up4, tm_big, tf, [_gather_one(shards[4])])
    w_down_full = _place_own(shards[4], down_g, pos, "place_w_down").reshape(ff, d)
    loss_part, dx2, dx2b, d_norm_final = _mlp_down_loss(
        x1, up, w_down_full, target, norm_final_g.reshape(1, d), tm, tf)

    dup, dh2 = _mlp_bwd(dx2b, up, up4, w_down_full, tm_big, min(512, up4.shape[-1]))
    dx1, dx1b, d_norm_mlp = _norm_bwd(dh2, x1, dx2, norm_mlp_g, min(256, s), "norm_mlp_bwd")
    hr_big = d // 2
    tkk = min(1024, s)
    g_down = _grad_w(
        up, dx2b, (N_CHIPS, 2, hr_big, d),
        lambda i, j: (i // (2 * (hr_big // min(1024, hr_big))), (i // (hr_big // min(1024, hr_big))) % 2,
                      i % (hr_big // min(1024, hr_big)), j),
        min(1024, hr_big), d, tkk, True, "grad_w_down")
    tn_up = min(2048, up4.shape[-1])
    per_up = up4.shape[-1] // tn_up
    g_up, sib_down = _grad_w(
        h2b, dup, (N_CHIPS, 2, hr_big, up4.shape[-1]),
        lambda i, j: (j // per_up, i // (hr_big // min(1024, hr_big)), i % (hr_big // min(1024, hr_big)), j % per_up),
        min(1024, hr_big), tn_up, tkk, False, "grad_w_up", exs=[_to_sibling_one(g_down)])
    pair_down = _pair_sum(g_down, sib_down, core, "pair_sum_w_down")
    dmix = _out_proj_bwd(dx1b, w_out_full, tm)
    hr_out = d // (2 * N_CHIPS)
    g_out, sib_up = _grad_w(
        mixed, dx1b, (1, 1, d, d), lambda i, j: (0, 0, i, j), min(1024, d), d, tkk, False, "grad_w_out",
        exs=[_to_sibling_one(g_up)])
    g_out = g_out.reshape(N_CHIPS, 2, hr_out, d)
    pair_up = _pair_sum(g_up, sib_up, core, "pair_sum_w_up")
    (dy_pre, g_glu, d_glu_b, d_norm_ssm), (sib_out,) = _glu_bwd(
        y_pre, dmix, wg_full, ssm_glu_b, out_norm_ssm_g, tm, [_to_sibling_one(g_out)])
    pair_out = _pair_sum(g_out, sib_out, core, "pair_sum_w_out")
    dys, gst, d_al = _ssm_bwd_state(dy_pre, v2b, al, xprev)
    (dz, d_k2, d_w2, d_v2), (chips_down,) = _ssm_bwd_main(
        us, dys, k2b, w2b, xprev, gst, consts, s, z.shape[1], [_to_owner_one(pair_down)])
    (dz, d_wt, d_bias, d_ln_g, d_ln_b, d_norm_sgu), (chips_up,) = _sgu_bwd(
        z, dmix, dz, sgu_ln_g, sgu_ln_b, wtb, wttb, biasb, out_norm_sgu_g, tm,
        [_to_owner_one(pair_up, peers=(0, 1))])
    cw_in = w_in4.shape[-1]
    g_in, chips_out, chips_up = _grad_w(
        h1b, dz, (N_CHIPS, 2, hr_big, cw_in),
        lambda i, j: (j, i // (hr_big // min(1024, hr_big)), i % (hr_big // min(1024, hr_big)), 0),
        min(1024, hr_big), 2 * cw_in, tkk, False, "grad_w_in",
        exs=[_to_owner_one(pair_out), _to_owner_one(pair_up, peers=(2,), into=chips_up)], chunks=2)
    (sib_in,) = _pair_exchange([g_in], "w_in_to_sibling")
    pair_in = _pair_sum(g_in, sib_in, core, "pair_sum_w_in")
    tm_x = min(512, s // 2)
    n_x = s // tm_x
    (gx_half, dg_a), (chips_in,) = _in_proj_bwd(
        dz, w_in4, xs, dx1, norm_mix_g, tm_x, [_to_owner_one(pair_in)], (0, n_x // 2), None, "in_proj_bwd_a")
    (grad_x, dg_b), _ = _in_proj_bwd(
        dz, w_in4, xs, dx1, norm_mix_g, tm_x, [], (n_x // 2, n_x), gx_half, "in_proj_bwd_b")
    d_norm_mix = dg_a + dg_b

    d_ssm = ssm_vjp((d_k2, d_w2, d_v2, d_al))
    small_grads = dict(
        norm_mix_g=d_norm_mix, ssm_a_re=d_ssm[0], ssm_a_im=d_ssm[1], ssm_b_re=d_ssm[2], ssm_b_im=d_ssm[3],
        ssm_c_re=d_ssm[4], ssm_c_im=d_ssm[5], ssm_d=d_ssm[6], ssm_log_dt=d_ssm[7], ssm_glu_b=d_glu_b,
        sgu_ln_g=d_ln_g, sgu_ln_b=d_ln_b, sgu_w=jnp.where(causal[None], d_wt, 0.0), sgu_b=d_bias[:, :, 0],
        out_norm_ssm_g=d_norm_ssm, out_norm_sgu_g=d_norm_sgu, norm_mlp_g=d_norm_mlp, norm_final_g=d_norm_final)
    flat = _pack([small_grads[n] for n in small])
    hr_small = flat.shape[0] // (2 * N_CHIPS)
    g_small = flat.reshape(N_CHIPS, 2, hr_small, LANES)

    hr_glu = width // (2 * N_CHIPS)
    grads = [g_glu.reshape(N_CHIPS, 2, hr_glu, width), g_small]
    tags = ["glu_w", "small"]
    from_sib = _pair_exchange(grads, "grads_to_sibling")
    pair_glu, pair_small = [_pair_sum(g, r, core, "pair_sum_" + t) for g, r, t in zip(grads, from_sib, tags)]
    big = dict(w_in=(pair_in, chips_in), w_out=(pair_out, chips_out), w_up=(pair_up, chips_up), w_down=(pair_down, chips_down))
    halves = {n: _chip_sum(p, q, pos, "chip_sum_" + n) for n, (p, q) in big.items()}
    owned = dict(zip(big, _pair_share(list(halves.values()), "grads_to_both_cores")))

    grad_out, delta_out, m_out, v_out = {}, {}, {}, {}

    def update(n, g, exs=()):
        shp = weights[n].shape
        g2, dl, nm, nv, *landed = _adamw(
            weights[n][0], g.reshape(shp[1], shp[2]), mom_m[n][0], mom_v[n][0], "adamw_" + n, exs)
        grad_out[n], delta_out[n], m_out[n], v_out[n] = g2.reshape(shp), dl.reshape(shp), nm.reshape(shp), nv.reshape(shp)
        return landed

    chips_glu, chips_small = update("w_down", owned["w_down"], [_to_owner_many([pair_glu, pair_small])])
    half_glu = _chip_sum(pair_glu, chips_glu, pos, "chip_sum_glu_w")
    half_small = _chip_sum(pair_small, chips_small, pos, "chip_sum_small")
    own_glu, own_small = update("w_up", owned["w_up"], [_share_many([half_glu, half_small])])
    (small_all,) = update("w_in", owned["w_in"], [_gather_chips_one(own_small)])
    update("w_out", owned["w_out"])
    update("ssm_glu_w", own_glu)
    small_flat = small_all.reshape(flat.shape)
    shapes = [weights[n].shape for n in small]
    g2, dl, nm, nv = _adamw(
        _pack([weights[n] for n in small]), small_flat, _pack([mom_m[n] for n in small]),
        _pack([mom_v[n] for n in small]), "adamw_small")
    for n, g, a, b, c in zip(small, _unpack(g2, shapes), _unpack(dl, shapes), _unpack(nm, shapes), _unpack(nv, shapes)):
        grad_out[n], delta_out[n], m_out[n], v_out[n] = g, a, b, c

    loss = lax.psum(loss_part[0, 0], ("x", "y", "c"))
    return (loss, grad_x.reshape(x.shape), *[grad_out[n] for n in names], *[delta_out[n] for n in names],
            *[m_out[n] for n in names], *[v_out[n] for n in names])
```

```python
import functools
import math

import numpy as np
import jax
import jax.numpy as jnp
from jax import lax
from jax.experimental import pallas as pl
from jax.experimental.pallas import tpu as pltpu

F32 = jnp.float32
BF16 = jnp.bfloat16
MESH = pl.DeviceIdType.MESH
HIGHEST = lax.Precision.HIGHEST

EPS = 1e-6
ADAM_LR = 0.001
ADAM_B1 = 0.9
ADAM_B2 = 0.999
ADAM_EPS = 1e-08
ADAM_WD = 0.01
ADAM_STEP = 10

N_CHIPS = 4
LANES = 128
SSM_GROUP = 16
SSM_STATE = 64
GROUPS_PER_TILE = LANES // SSM_GROUP
CHUNK = 16
PAIRS = CHUNK // 2
SGU_CHUNK = 128
VMEM_LIMIT = 56 * 2**20


def _params(n_axes, vmem=VMEM_LIMIT):
    return pltpu.CompilerParams(dimension_semantics=("arbitrary",) * n_axes, vmem_limit_bytes=vmem)


def _call(body, **kw):
    return pl.pallas_call(body, **kw)


def _dot(a, b):
    return jnp.dot(a, b, preferred_element_type=F32)


def _dot_nt(a, b):
    return lax.dot_general(a, b, (((1,), (1,)), ((), ())), preferred_element_type=F32)


def _dot_tn(a, b):
    return lax.dot_general(a, b, (((0,), (0,)), ((), ())), preferred_element_type=F32)


_GELU_K = math.sqrt(2.0 / math.pi)
_GELU_C = 0.044715


def _gelu_both(x):
    x2 = x * x
    t = jnp.tanh(x * (_GELU_K + (_GELU_K * _GELU_C) * x2))
    hx = 0.5 * x
    onep = 1.0 + t
    return hx * onep, 0.5 * onep + hx * (1.0 - t * t) * (_GELU_K + (3.0 * _GELU_K * _GELU_C) * x2)


def _gelu(x):
    return _gelu_both(x)[0]


def _sigmoid(x):
    return 1.0 / (1.0 + jnp.exp(-x))


def _rms(x):
    return lax.rsqrt(jnp.mean(x * x, axis=-1, keepdims=True) + EPS)


def _rms_bwd(dy, x, r, g):
    a = dy * g
    dx = r * a - x * (r * r * r) * jnp.mean(a * x, axis=-1, keepdims=True)
    return dx, dy * x * r


def _row_tile(rows, target, mult=16):
    for t in range(min(rows, target), 0, -1):
        if rows % t == 0 and t % mult == 0:
            return t
    return rows


def _ssm_mats(a_re, a_im, b_re, b_im, c_re, c_im, d, log_dt):
    g, p = a_re.shape
    h = b_re.shape[-1]
    nt = g // GROUPS_PER_TILE
    dt = jnp.exp(log_dt)[:, None]
    lr, li = a_re * dt, a_im * dt

    def apow(l, lr, li):
        mag = jnp.exp(lr * l)
        return mag * jnp.cos(li * l), mag * jnp.sin(li * l)

    ar, ai = apow(1.0, lr, li)
    den = a_re * a_re + a_im * a_im
    qr = ((ar - 1.0) * a_re + ai * a_im) / den
    qi = (ai * a_re - (ar - 1.0) * a_im) / den
    bt_re, bt_im = jnp.swapaxes(b_re, 1, 2), jnp.swapaxes(b_im, 1, 2)
    bbr = qr[:, None, :] * bt_re - qi[:, None, :] * bt_im
    bbi = qr[:, None, :] * bt_im + qi[:, None, :] * bt_re
    b_same = jnp.concatenate([bbr, bbi], axis=-1)[:, None]
    b_swap = jnp.concatenate([bbi, bbr], axis=-1)[:, None]
    ls = jnp.arange(CHUNK + 1, dtype=F32)[None, :, None]
    pr, pi = apow(ls, lr[:, None, :], li[:, None, :])
    p_same = jnp.concatenate([pr, pr], axis=-1)[:, :, None, :]
    p_sign = jnp.concatenate([-pi, pi], axis=-1)[:, :, None, :]
    tj = p_same[:, :CHUNK] * b_same + p_sign[:, :CHUNK] * b_swap
    c_conj = jnp.concatenate([c_re, -c_im], axis=-1)
    k = jnp.einsum("goq,gliq->glio", c_conj, tj, precision=HIGHEST)
    k = k.at[:, 0].add(d[:, :, None] * jnp.eye(h, dtype=F32))

    zero = jnp.zeros_like(k[:, :1])
    kz = jnp.concatenate([zero, zero, k], axis=1).reshape(g, PAIRS + 1, 2, h, h)
    even, odd = kz[:, :, 0], kz[:, :, 1]
    fill = jnp.zeros((g, PAIRS, h, LANES - 2 * h), F32)
    row0 = jnp.concatenate([even[:, 1:], odd[:, 1:], fill], axis=-1)
    row1 = jnp.concatenate([odd[:, :-1], even[:, 1:], fill], axis=-1)
    k2c = jnp.stack([row0, row1], axis=2)
    w2c = tj[:, ::-1].reshape(g, PAIRS, 2, h, 2 * p)
    c_cross = jnp.concatenate([-c_im, -c_re], axis=-1)
    p_imag = jnp.concatenate([pi, pi], axis=-1)[:, :, None, :]
    v2c = (c_conj[:, None] * p_same[:, 1:] + c_cross[:, None] * p_imag[:, 1:]).reshape(g, PAIRS, 2, h, 2 * p)
    al = jnp.stack([pr[:, CHUNK].reshape(nt, -1), pi[:, CHUNK].reshape(nt, -1)], axis=1)
    return k2c, w2c, v2c, al


def _spread_consts(h, p):
    gg = GROUPS_PER_TILE
    row_g = (np.arange(2 * gg * h) // h) % gg
    colk = np.arange(2 * gg * h)
    rep_k = np.zeros((LANES, 2 * gg * h), np.float32)
    rep_k[(colk // (gg * h)) * h + colk % h, colk] = 1.0
    mask_k = (row_g[:, None] == ((colk // h) % gg)[None, :]).astype(np.float32)
    cols = np.arange(2 * gg * p)
    rep_s = np.zeros((2 * p, 2 * gg * p), np.float32)
    rep_s[(cols // (gg * p)) * p + cols % p, cols] = 1.0
    mask_s = (row_g[:, None] == ((cols // p) % gg)[None, :]).astype(np.float32)
    return tuple(jnp.asarray(a, BF16) for a in (rep_k, mask_k, rep_s, mask_s))


def _ssm_spread(k2c, w2c, v2c, consts, exs):
    h = k2c.shape[3]
    nt = k2c.shape[0] // GROUPS_PER_TILE
    rep_k, mask_k, rep_s, mask_s = consts
    nk, ns = rep_k.shape[1], rep_s.shape[1]

    def body(k_ref, w_ref, v_ref, rk_ref, mk_ref, rs_ref, ms_ref, ko_ref, wo_ref, vo_ref):
        for q in range(PAIRS):
            for c_ref, o_ref, r_ref, m_ref in ((k_ref, ko_ref, rk_ref, mk_ref), (w_ref, wo_ref, rs_ref, ms_ref),
                                               (v_ref, vo_ref, rs_ref, ms_ref)):
                rows = jnp.concatenate(
                    [c_ref[gi, q, st] for st in range(2) for gi in range(GROUPS_PER_TILE)], axis=0)
                o_ref[q] = (_dot(rows.astype(BF16), r_ref[...]) * m_ref[...]).astype(BF16)

    compact = pl.BlockSpec((GROUPS_PER_TILE, PAIRS, 2, h, LANES), lambda t: (t, 0, 0, 0, 0))

    def tile(cols):
        return pl.BlockSpec((None, PAIRS, nk, cols), lambda t: (t, 0, 0, 0))

    def whole(a):
        return pl.BlockSpec(a.shape, lambda t: (0, 0))

    return _call_riding(
        body,
        exs,
        args=(k2c, w2c, v2c, rep_k, mask_k, rep_s, mask_s),
        scratch_shapes=[],
        grid=(nt,),
        in_specs=[compact, compact, compact, whole(rep_k), whole(mask_k), whole(rep_s), whole(mask_s)],
        out_specs=[tile(nk), tile(ns), tile(ns)],
        out_shape=[
            jax.ShapeDtypeStruct((nt, PAIRS, nk, nk), BF16),
            jax.ShapeDtypeStruct((nt, PAIRS, nk, ns), BF16),
            jax.ShapeDtypeStruct((nt, PAIRS, nk, ns), BF16),
        ],
        compiler_params=_params(1),
        name="ssm_spread",
    )


def _gather_blocks(full, mask, rep):
    return _dot_nt((full * mask).astype(BF16), rep)


def _in_proj(x, g, w4, tm, exs):
    s, d = x.shape
    nj, _, cw = w4.shape

    def body(x_ref, g_ref, w_ref, z_ref, h_ref):
        @pl.when(pl.program_id(1) == 0)
        def _():
            xv = x_ref[...]
            h_ref[...] = (xv * _rms(xv) * g_ref[...]).astype(BF16)

        z_ref[...] = _dot(h_ref[...], w_ref[...])

    return _call_riding(
        body,
        exs,
        scratch_shapes=[],
        args=(x, g, w4),
        grid=(s // tm, nj),
        in_specs=[
            pl.BlockSpec((tm, d), lambda i, j: (i, 0)),
            pl.BlockSpec((1, d), lambda i, j: (0, 0)),
            pl.BlockSpec((None, d, cw), lambda i, j: (j, 0, 0)),
        ],
        out_specs=[pl.BlockSpec((tm, cw), lambda i, j: (i, j)), pl.BlockSpec((tm, d), lambda i, j: (i, 0))],
        out_shape=[jax.ShapeDtypeStruct((s, nj * cw), F32), jax.ShapeDtypeStruct((s, d), BF16)],
        compiler_params=_params(2),
        name="in_proj",
    )


def _ssm_fwd(z, k2, w2, v2, al, exs):
    s = z.shape[0]
    nt = k2.shape[0]
    nc = s // CHUNK
    ns = w2.shape[-1]
    hs = ns // 2

    def body(u_ref, k_ref, w_ref, v_ref, al_ref, y_ref, xp_ref, us_ref, xloc):
        for q in range(PAIRS):
            us_ref[q, :, 0:LANES] = u_ref[pl.ds(2 * q, nc, stride=CHUNK), :].astype(BF16)
            us_ref[q, :, LANES : 2 * LANES] = u_ref[pl.ds(2 * q + 1, nc, stride=CHUNK), :].astype(BF16)
        acc = _dot(us_ref[0], w_ref[0])
        for q in range(1, PAIRS):
            acc = acc + _dot(us_ref[q], w_ref[q])
        xloc[...] = acc
        ar = al_ref[0:1, :]
        ai = al_ref[1:2, :]

        def step(c, carry):
            xr, xi = carry
            xp_ref[pl.ds(c, 1), 0:hs] = xr
            xp_ref[pl.ds(c, 1), hs:ns] = xi
            lr = xloc[pl.ds(c, 1), 0:hs]
            li = xloc[pl.ds(c, 1), hs:ns]
            return ar * xr - ai * xi + lr, ar * xi + ai * xr + li

        zero = jnp.zeros((1, hs), F32)
        lax.fori_loop(0, nc, step, (zero, zero))
        xpb = xp_ref[...].astype(BF16)
        for r in range(PAIRS):
            acc = _dot_nt(xpb, v_ref[r])
            for q in range(r + 1):
                acc = acc + _dot(us_ref[q], k_ref[r - q])
            y_ref[pl.ds(2 * r, nc, stride=CHUNK), :] = acc[:, 0:LANES]
            y_ref[pl.ds(2 * r + 1, nc, stride=CHUNK), :] = acc[:, LANES : 2 * LANES]

    return _call_riding(
        body,
        exs,
        args=(z, k2, w2, v2, al),
        grid=(nt,),
        in_specs=[
            pl.BlockSpec((s, LANES), lambda t: (0, t)),
            pl.BlockSpec((None, PAIRS, 2 * LANES, 2 * LANES), lambda t: (t, 0, 0, 0)),
            pl.BlockSpec((None, PAIRS, 2 * LANES, ns), lambda t: (t, 0, 0, 0)),
            pl.BlockSpec((None, PAIRS, 2 * LANES, ns), lambda t: (t, 0, 0, 0)),
            pl.BlockSpec((None, 2, hs), lambda t: (t, 0, 0)),
        ],
        out_specs=[
            pl.BlockSpec((s, LANES), lambda t: (0, t)),
            pl.BlockSpec((None, nc, ns), lambda t: (t, 0, 0)),
            pl.BlockSpec((None, PAIRS, nc, 2 * LANES), lambda t: (t, 0, 0, 0)),
        ],
        out_shape=[
            jax.ShapeDtypeStruct((s, nt * LANES), F32),
            jax.ShapeDtypeStruct((nt, nc, ns), F32),
            jax.ShapeDtypeStruct((nt, PAIRS, nc, 2 * LANES), BF16),
        ],
        scratch_shapes=[pltpu.VMEM((nc, ns), F32)],
        compiler_params=_params(1),
        name="ssm_fwd",
    )


def _glu_fwd(y_pre, wg, bg, go, d_model, tm):
    s, w = y_pre.shape

    def body(y_ref, wg_ref, bg_ref, go_ref, o_ref):
        yg = _gelu(y_ref[...])
        gate = _sigmoid(_dot(yg.astype(BF16), wg_ref[...]) + bg_ref[...])
        ys = yg * gate
        o_ref[...] = (ys * _rms(ys) * go_ref[...]).astype(BF16)

    return _call(
        body,
        grid=(s // tm,),
        in_specs=[
            pl.BlockSpec((tm, w), lambda i: (i, 0)),
            pl.BlockSpec((w, w), lambda i: (0, 0)),
            pl.BlockSpec((1, w), lambda i: (0, 0)),
            pl.BlockSpec((1, w), lambda i: (0, 0)),
        ],
        out_specs=pl.BlockSpec((tm, w), lambda i: (i, 0)),
        out_shape=jax.ShapeDtypeStruct((s, d_model), BF16),
        compiler_params=_params(1),
        name="glu_fwd",
    )(y_pre, wg, bg, go)


def _sgu_parts(zu, zv, lng, lnb, wt_ref, bias_ref):
    u, u_grad = _gelu_both(zu)
    v, v_grad = _gelu_both(zv)
    mu = jnp.mean(v, axis=-1, keepdims=True)
    vc = v - mu
    rstd = lax.rsqrt(jnp.mean(vc * vc, axis=-1, keepdims=True) + EPS)
    vhat = vc * rstd
    vb = (vhat * lng + lnb).astype(BF16)
    heads = wt_ref.shape[0]
    mix = jnp.concatenate(
        [_dot(wt_ref[h], vb[:, h * LANES : (h + 1) * LANES]) + bias_ref[h] for h in range(heads)], axis=1
    )
    return u, vhat, rstd, vb, mix, u_grad, v_grad


def _sgu_fwd(z, mixed, lng, lnb, wt, biasb, go, rb):
    s = z.shape[0]
    w = lng.shape[-1]
    heads = wt.shape[0]

    def body(zu_ref, zv_ref, m_any, lng_ref, lnb_ref, wt_ref, b_ref, go_ref, o_ref):
        del m_any
        for ck in range(rb // SGU_CHUNK):
            rows = slice(ck * SGU_CHUNK, (ck + 1) * SGU_CHUNK)
            u, _, _, _, mix, _, _ = _sgu_parts(zu_ref[rows, :], zv_ref[rows, :], lng_ref[...], lnb_ref[...], wt_ref, b_ref)
            y = u * mix
            o_ref[rows, :] = (y * _rms(y) * go_ref[...]).astype(BF16)

    return _call(
        body,
        grid=(s // rb,),
        in_specs=[
            pl.BlockSpec((rb, w), lambda i: (i, 1)),
            pl.BlockSpec((rb, w), lambda i: (i, 2)),
            pl.BlockSpec(memory_space=pl.ANY),
            pl.BlockSpec((1, w), lambda i: (0, 0)),
            pl.BlockSpec((1, w), lambda i: (0, 0)),
            pl.BlockSpec((heads, SGU_CHUNK, SGU_CHUNK), lambda i: (0, 0, 0)),
            pl.BlockSpec((heads, SGU_CHUNK, LANES), lambda i: (0, 0, 0)),
            pl.BlockSpec((1, w), lambda i: (0, 0)),
        ],
        out_specs=pl.BlockSpec((rb, w), lambda i: (i, 1)),
        out_shape=jax.ShapeDtypeStruct(mixed.shape, mixed.dtype),
        input_output_aliases={2: 0},
        compiler_params=_params(1),
        name="sgu_fwd",
    )(z, z, mixed, lng, lnb, wt, biasb, go)


def _out_proj(x, mixed, w_out, g, tm):
    s, d = x.shape

    def body(x_ref, m_ref, w_ref, g_ref, o_ref, h_ref):
        x1 = x_ref[...] + _dot(m_ref[...], w_ref[...])
        o_ref[...] = x1
        h_ref[...] = (x1 * _rms(x1) * g_ref[...]).astype(BF16)

    row = pl.BlockSpec((tm, d), lambda i: (i, 0))
    return _call(
        body,
        grid=(s // tm,),
        in_specs=[row, row, pl.BlockSpec((d, d), lambda i: (0, 0)), pl.BlockSpec((1, d), lambda i: (0, 0))],
        out_specs=[row, row],
        out_shape=[jax.ShapeDtypeStruct((s, d), F32), jax.ShapeDtypeStruct((s, d), BF16)],
        compiler_params=_params(1),
        name="out_proj",
    )(x, mixed, w_out, g)


def _mlp_up(h, w_up4, tm, tf, exs):
    s, d = h.shape
    nj, _, cw = w_up4.shape
    per = cw // tf

    def body(h_ref, wu_ref, up_ref):
        up_ref[...] = _dot(h_ref[...], wu_ref[...]).astype(BF16)

    return _call_riding(
        body,
        exs,
        scratch_shapes=[],
        args=(h, w_up4),
        grid=(s // tm, nj * per),
        in_specs=[
            pl.BlockSpec((tm, d), lambda i, f: (i, 0)),
            pl.BlockSpec((None, d, tf), lambda i, f: (f // per, 0, f % per)),
        ],
        out_specs=[pl.BlockSpec((tm, tf), lambda i, f: (i, f))],
        out_shape=[jax.ShapeDtypeStruct((s, nj * cw), BF16)],
        compiler_params=_params(2),
        name="mlp_up",
    )


def _mlp_down_loss(x1, up, w_down, target, g, tm, tf):
    s, d = x1.shape
    ff = w_down.shape[0]
    nf = ff // tf

    def body(x_ref, up_ref, wd_ref, t_ref, g_ref, loss_ref, dx_ref, dxb_ref, dg_ref, acc):
        i = pl.program_id(0)
        f = pl.program_id(1)

        @pl.when(f == 0)
        def _():
            acc[...] = x_ref[...]

        @pl.when(jnp.logical_and(i == 0, f == 0))
        def _():
            loss_ref[...] = jnp.zeros_like(loss_ref)
            dg_ref[...] = jnp.zeros_like(dg_ref)

        a = jnp.maximum(up_ref[...].astype(F32), 0.0)
        acc[...] += _dot((a * a).astype(BF16), wd_ref[...])

        @pl.when(f == nf - 1)
        def _():
            xv = acc[...]
            r = _rms(xv)
            gv = g_ref[...]
            diff = xv * r * gv - t_ref[...]
            loss_ref[...] += 0.5 * jnp.sum(jnp.mean(diff * diff, axis=-1, keepdims=True), axis=0, keepdims=True)
            dx, dgt = _rms_bwd(diff * (1.0 / d), xv, r, gv)
            dx_ref[...] = dx
            dxb_ref[...] = dx.astype(BF16)
            dg_ref[...] += jnp.sum(dgt, axis=0, keepdims=True)

    row = pl.BlockSpec((tm, d), lambda i, f: (i, 0))
    vec = pl.BlockSpec((1, d), lambda i, f: (0, 0))
    return _call(
        body,
        grid=(s // tm, nf),
        in_specs=[
            row,
            pl.BlockSpec((tm, tf), lambda i, f: (i, f)),
            pl.BlockSpec((tf, d), lambda i, f: (f, 0)),
            row,
            vec,
        ],
        out_specs=[pl.BlockSpec((1, 1), lambda i, f: (0, 0)), row, row, vec],
        out_shape=[
            jax.ShapeDtypeStruct((1, 1), F32),
            jax.ShapeDtypeStruct((s, d), F32),
            jax.ShapeDtypeStruct((s, d), BF16),
            jax.ShapeDtypeStruct((1, d), F32),
        ],
        scratch_shapes=[pltpu.VMEM((tm, d), F32)],
        compiler_params=_params(2),
        name="mlp_down_loss",
    )(x1, up, w_down, target, g)


def _mlp_bwd(dx2b, up, w_up4, w_down, tm, tf):
    s, d = dx2b.shape
    ff = w_down.shape[0]
    cw = w_up4.shape[-1]
    per = cw // tf

    def body(dx_ref, up_ref, wu_ref, wd_ref, dup_ref, dh_ref):
        @pl.when(pl.program_id(1) == 0)
        def _():
            dh_ref[...] = jnp.zeros_like(dh_ref)

        dact = _dot_nt(dx_ref[...], wd_ref[...])
        dupb = (dact * (2.0 * jnp.maximum(up_ref[...].astype(F32), 0.0))).astype(BF16)
        dup_ref[...] = dupb
        dh_ref[...] += _dot_nt(dupb, wu_ref[...])

    return _call(
        body,
        grid=(s // tm, ff // tf),
        in_specs=[
            pl.BlockSpec((tm, d), lambda i, f: (i, 0)),
            pl.BlockSpec((tm, tf), lambda i, f: (i, f)),
            pl.BlockSpec((None, d, tf), lambda i, f: (f // per, 0, f % per)),
            pl.BlockSpec((tf, d), lambda i, f: (f, 0)),
        ],
        out_specs=[pl.BlockSpec((tm, tf), lambda i, f: (i, f)), pl.BlockSpec((tm, d), lambda i, f: (i, 0))],
        out_shape=[jax.ShapeDtypeStruct((s, ff), BF16), jax.ShapeDtypeStruct((s, d), F32)],
        compiler_params=_params(2),
        name="mlp_bwd",
    )(dx2b, up, w_up4, w_down)


def _norm_bwd(dh, x, dres, g, tm, name):
    s, d = x.shape

    def body(dh_ref, x_ref, dr_ref, g_ref, dx_ref, dxb_ref, dg_ref):
        @pl.when(pl.program_id(0) == 0)
        def _():
            dg_ref[...] = jnp.zeros_like(dg_ref)

        xv = x_ref[...]
        dx, dgt = _rms_bwd(dh_ref[...], xv, _rms(xv), g_ref[...])
        tot = dr_ref[...] + dx
        dx_ref[...] = tot
        dxb_ref[...] = tot.astype(BF16)
        dg_ref[...] += jnp.sum(dgt, axis=0, keepdims=True)

    row = pl.BlockSpec((tm, d), lambda i: (i, 0))
    vec = pl.BlockSpec((1, d), lambda i: (0, 0))
    return _call(
        body,
        grid=(s // tm,),
        in_specs=[row, row, row, vec],
        out_specs=[row, row, vec],
        out_shape=[
            jax.ShapeDtypeStruct((s, d), F32),
            jax.ShapeDtypeStruct((s, d), BF16),
            jax.ShapeDtypeStruct((1, d), F32),
        ],
        compiler_params=_params(1),
        name=name,
    )(dh, x, dres, g)


def _grad_w(a, b, out_dims, index_map, tm, tn, tk, relu2, name, exs=(), chunks=1):
    t, m = a.shape
    n = b.shape[1]
    nk = t // tk
    cw = tn // chunks

    def body(a_ref, b_ref, o_ref, acc):
        k = pl.program_id(2)

        @pl.when(k == 0)
        def _():
            acc[...] = jnp.zeros_like(acc)

        av = a_ref[...]
        if relu2:
            r = jnp.maximum(av.astype(F32), 0.0)
            av = (r * r).astype(BF16)
        acc[...] += _dot_tn(av, b_ref[...])

        @pl.when(k == nk - 1)
        def _():
            for c in range(chunks):
                o_ref[c] = acc[:, c * cw : (c + 1) * cw].astype(BF16)

    spec = dict(
        grid=(m // tm, n // tn, nk),
        in_specs=[pl.BlockSpec((tk, tm), lambda i, j, k: (k, i)), pl.BlockSpec((tk, tn), lambda i, j, k: (k, j))],
        out_specs=[pl.BlockSpec((chunks, None, tm, cw), lambda i, j, k: index_map(i, j))],
        out_shape=[jax.ShapeDtypeStruct(out_dims, BF16)],
        scratch_shapes=[pltpu.VMEM((tm, tn), F32)],
        compiler_params=_params(3),
        name=name,
    )
    (g,), landed = _call_riding(body, exs, args=(a, b), **spec)
    return (g, *landed) if exs else g


def _out_proj_bwd(dx1b, w_out, tm):
    s, d = dx1b.shape

    def body(dx_ref, w_ref, o_ref):
        o_ref[...] = _dot_nt(dx_ref[...], w_ref[...])

    return _call(
        body,
        grid=(s // tm,),
        in_specs=[pl.BlockSpec((tm, d), lambda i: (i, 0)), pl.BlockSpec((d, d), lambda i: (0, 0))],
        out_specs=pl.BlockSpec((tm, d), lambda i: (i, 0)),
        out_shape=jax.ShapeDtypeStruct((s, d), F32),
        compiler_params=_params(1),
        name="out_proj_bwd",
    )(dx1b, w_out)


def _glu_bwd(y_pre, dmix, wg, bg, go, tm, exs):
    s, w = y_pre.shape
    n = s // tm

    def body(y_ref, dm_ref, wg_ref, bg_ref, go_ref, dy_ref, dwg_ref, dbg_ref, dgo_ref, acc):
        i = pl.program_id(0)

        @pl.when(i == 0)
        def _():
            acc[...] = jnp.zeros_like(acc)
            dbg_ref[...] = jnp.zeros_like(dbg_ref)
            dgo_ref[...] = jnp.zeros_like(dgo_ref)

        yp = y_ref[...]
        yg, yg_grad = _gelu_both(yp)
        ygb = yg.astype(BF16)
        gate = _sigmoid(_dot(ygb, wg_ref[...]) + bg_ref[...])
        ys = yg * gate
        dys, dgt = _rms_bwd(dm_ref[...], ys, _rms(ys), go_ref[...])
        dgo_ref[...] += jnp.sum(dgt, axis=0, keepdims=True)
        dpre = dys * yg * gate * (1.0 - gate)
        dbg_ref[...] += jnp.sum(dpre, axis=0, keepdims=True)
        dpb = dpre.astype(BF16)
        dyg = dys * gate + _dot_nt(dpb, wg_ref[...])
        dy_ref[...] = dyg * yg_grad
        acc[...] += _dot_tn(ygb, dpb)

        @pl.when(i == n - 1)
        def _():
            dwg_ref[...] = acc[...].astype(BF16)

    return _call_riding(
        body,
        exs,
        args=(y_pre, dmix, wg, bg, go),
        grid=(n,),
        in_specs=[
            pl.BlockSpec((tm, w), lambda i: (i, 0)),
            pl.BlockSpec((tm, w), lambda i: (i, 0)),
            pl.BlockSpec((w, w), lambda i: (0, 0)),
            pl.BlockSpec((1, w), lambda i: (0, 0)),
            pl.BlockSpec((1, w), lambda i: (0, 0)),
        ],
        out_specs=[
            pl.BlockSpec((tm, w), lambda i: (i, 0)),
            pl.BlockSpec((w, w), lambda i: (0, 0)),
            pl.BlockSpec((1, w), lambda i: (0, 0)),
            pl.BlockSpec((1, w), lambda i: (0, 0)),
        ],
        out_shape=[
            jax.ShapeDtypeStruct((s, w), F32),
            jax.ShapeDtypeStruct((w, w), BF16),
            jax.ShapeDtypeStruct((1, w), F32),
            jax.ShapeDtypeStruct((1, w), F32),
        ],
        scratch_shapes=[pltpu.VMEM((w, w), F32)],
        compiler_params=_params(1),
        name="glu_bwd",
    )


def _ssm_bwd_state(dy, v2, al, xprev):
    s = dy.shape[0]
    nt, nc, ns = xprev.shape
    hs = ns // 2

    def body(dy_ref, v_ref, al_ref, xp_ref, dys_ref, g_ref, dal_ref, dxp, gs):
        for q in range(PAIRS):
            dys_ref[q, :, 0:LANES] = dy_ref[pl.ds(2 * q, nc, stride=CHUNK), :].astype(BF16)
            dys_ref[q, :, LANES : 2 * LANES] = dy_ref[pl.ds(2 * q + 1, nc, stride=CHUNK), :].astype(BF16)
        acc = _dot(dys_ref[0], v_ref[0])
        for r in range(1, PAIRS):
            acc = acc + _dot(dys_ref[r], v_ref[r])
        dxp[...] = acc
        ar = al_ref[0:1, :]
        ai = al_ref[1:2, :]
        zero = jnp.zeros((1, hs), F32)
        gs[pl.ds(nc - 1, 1), 0:hs] = zero
        gs[pl.ds(nc - 1, 1), hs:ns] = zero

        def step(n, carry):
            gr, gi = carry
            c = nc - 2 - n
            nr = dxp[pl.ds(c + 1, 1), 0:hs] + ar * gr + ai * gi
            ni = dxp[pl.ds(c + 1, 1), hs:ns] + ar * gi - ai * gr
            gs[pl.ds(c, 1), 0:hs] = nr
            gs[pl.ds(c, 1), hs:ns] = ni
            return nr, ni

        lax.fori_loop(0, nc - 1, step, (zero, zero))
        gv = gs[...]
        xv = xp_ref[...]
        gr, gi = gv[:, 0:hs], gv[:, hs:ns]
        xr, xi = xv[:, 0:hs], xv[:, hs:ns]
        dal_ref[0:1, :] = jnp.sum(gr * xr + gi * xi, axis=0, keepdims=True)
        dal_ref[1:2, :] = jnp.sum(gi * xr - gr * xi, axis=0, keepdims=True)
        g_ref[...] = gv.astype(BF16)

    return _call(
        body,
        grid=(nt,),
        in_specs=[
            pl.BlockSpec((s, LANES), lambda t: (0, t)),
            pl.BlockSpec((None, PAIRS, 2 * LANES, ns), lambda t: (t, 0, 0, 0)),
            pl.BlockSpec((None, 2, hs), lambda t: (t, 0, 0)),
            pl.BlockSpec((None, nc, ns), lambda t: (t, 0, 0)),
        ],
        out_specs=[
            pl.BlockSpec((None, PAIRS, nc, 2 * LANES), lambda t: (t, 0, 0, 0)),
            pl.BlockSpec((None, nc, ns), lambda t: (t, 0, 0)),
            pl.BlockSpec((None, 2, hs), lambda t: (t, 0, 0)),
        ],
        out_shape=[
            jax.ShapeDtypeStruct((nt, PAIRS, nc, 2 * LANES), BF16),
            jax.ShapeDtypeStruct((nt, nc, ns), BF16),
            jax.ShapeDtypeStruct((nt, 2, hs), F32),
        ],
        scratch_shapes=[pltpu.VMEM((nc, ns), F32), pltpu.VMEM((nc, ns), F32)],
        compiler_params=_params(1),
        name="ssm_bwd_state",
    )(dy, v2, al, xprev)


def _ssm_bwd_main(us, dys, k2, w2, xprev, gst, consts, s, width, exs):
    nt, _, nc, _ = us.shape
    ns = xprev.shape[-1]
    rep_k, mask_k, rep_s, mask_s = consts
    nk = 2 * LANES
    gg = GROUPS_PER_TILE
    rows_of = [(gi, st, slice((st * gg + gi) * SSM_GROUP, (st * gg + gi + 1) * SSM_GROUP))
               for st in range(2) for gi in range(gg)]

    def body(us_ref, dys_ref, k_ref, w_ref, xp_ref, g_ref, rk_ref, mk_ref, rs_ref, ms_ref,
             du_ref, dk_ref, dw_ref, dv_ref, acc, du2, duf, dkf):
        q = pl.program_id(1)

        @pl.when(q == 0)
        def _():
            dkf[...] = jnp.zeros_like(dkf)

        usq = us_ref[q]
        gb = g_ref[...]
        dw = _gather_blocks(_dot_tn(usq, gb), ms_ref[...], rs_ref[...])
        dv = _gather_blocks(_dot_tn(dys_ref[q], xp_ref[...].astype(BF16)), ms_ref[...], rs_ref[...])
        for gi, st, rows in rows_of:
            dw_ref[gi, st] = dw[rows]
            dv_ref[gi, st] = dv[rows]
        acc[...] = _dot_nt(gb, w_ref[...])
        for m in range(PAIRS):

            @pl.when(q + m < PAIRS)
            def _():
                dyr = dys_ref[jnp.minimum(q + m, PAIRS - 1)]
                acc[...] += _dot_nt(dyr, k_ref[m])
                dkf[m] += _dot_tn(usq, dyr)

        du2[q] = acc[...]

        @pl.when(q == PAIRS - 1)
        def _():
            for p in range(PAIRS):
                duf[pl.ds(2 * p, nc, stride=CHUNK), :] = du2[p, :, 0:LANES]
                duf[pl.ds(2 * p + 1, nc, stride=CHUNK), :] = du2[p, :, LANES : 2 * LANES]
                dk = _gather_blocks(dkf[p], mk_ref[...], rk_ref[...])
                for gi, st, rows in rows_of:
                    dk_ref[gi, p, st] = dk[rows]
            du_ref[...] = duf[...].astype(BF16)

    def per_tile(rows, cols):
        return pl.BlockSpec((None, PAIRS, rows, cols), lambda t, q: (t, 0, 0, 0))

    def per_pair(rows, cols):
        return pl.BlockSpec((None, None, rows, cols), lambda t, q: (t, q, 0, 0))

    def whole(a):
        return pl.BlockSpec(a.shape, lambda t, q: (0, 0))

    return _call_riding(
        body,
        exs,
        args=(us, dys, k2, w2, xprev, gst, rep_k, mask_k, rep_s, mask_s),
        grid=(nt, PAIRS),
        in_specs=[
            per_tile(nc, nk),
            per_tile(nc, nk),
            per_tile(nk, nk),
            per_pair(nk, ns),
            pl.BlockSpec((None, nc, ns), lambda t, q: (t, 0, 0)),
            pl.BlockSpec((None, nc, ns), lambda t, q: (t, 0, 0)),
            whole(rep_k),
            whole(mask_k),
            whole(rep_s),
            whole(mask_s),
        ],
        out_specs=[
            pl.BlockSpec((s, LANES), lambda t, q: (0, t)),
            pl.BlockSpec((gg, PAIRS, 2, SSM_GROUP, LANES), lambda t, q: (t, 0, 0, 0, 0)),
            pl.BlockSpec((gg, None, 2, SSM_GROUP, LANES), lambda t, q: (t, q, 0, 0, 0)),
            pl.BlockSpec((gg, None, 2, SSM_GROUP, LANES), lambda t, q: (t, q, 0, 0, 0)),
        ],
        out_shape=[
            jax.ShapeDtypeStruct((s, width), BF16),
            jax.ShapeDtypeStruct((nt * gg, PAIRS, 2, SSM_GROUP, LANES), F32),
            jax.ShapeDtypeStruct((nt * gg, PAIRS, 2, SSM_GROUP, LANES), F32),
            jax.ShapeDtypeStruct((nt * gg, PAIRS, 2, SSM_GROUP, LANES), F32),
        ],
        scratch_shapes=[
            pltpu.VMEM((nc, nk), F32),
            pltpu.VMEM((PAIRS, nc, nk), F32),
            pltpu.VMEM((s, LANES), F32),
            pltpu.VMEM((PAIRS, nk, nk), F32),
        ],
        compiler_params=_params(2),
        name="ssm_bwd_main",
    )


def _sgu_bwd(z, dmix, dz, lng, lnb, wt, wtt, biasb, go, rb, exs):
    s = z.shape[0]
    w = lng.shape[-1]
    heads = wt.shape[0]
    ncol = (LANES, LANES)

    def body(zu_ref, zv_ref, dm_ref, dz_any, lng_ref, lnb_ref, wt_ref, wtt_ref, b_ref, go_ref,
             dz_ref, dw_ref, db_ref, dlg_ref, dlb_ref, dgo_ref, dzv):
        del dz_any
        i = pl.program_id(0)
        p = pl.program_id(1)

        @pl.when(jnp.logical_and(i == 0, p == 0))
        def _():
            dw_ref[...] = jnp.zeros_like(dw_ref)
            db_ref[...] = jnp.zeros_like(db_ref)
            dlg_ref[...] = jnp.zeros_like(dlg_ref)
            dlb_ref[...] = jnp.zeros_like(dlb_ref)
            dgo_ref[...] = jnp.zeros_like(dgo_ref)

        @pl.when(p == 0)
        def _():
            lng_v = lng_ref[...]
            for ck in range(rb // SGU_CHUNK):
                rows = slice(ck * SGU_CHUNK, (ck + 1) * SGU_CHUNK)
                zu = zu_ref[rows, :]
                zv = zv_ref[rows, :]
                u, vhat, rstd, vb, mix, u_grad, v_grad = _sgu_parts(zu, zv, lng_v, lnb_ref[...], wt_ref, b_ref)
                y = u * mix
                dy, dgt = _rms_bwd(dm_ref[rows, :], y, _rms(y), go_ref[...])
                dgo_ref[...] += jnp.sum(dgt, axis=0, keepdims=True)
                du = dy * mix
                dmx = dy * u
                dmb = dmx.astype(BF16)
                dvl = []
                for h in range(heads):
                    cols = slice(h * LANES, (h + 1) * LANES)
                    db_ref[h] += jnp.broadcast_to(jnp.sum(dmx[:, cols], axis=-1, keepdims=True), ncol)
                    dw_ref[h] += _dot_nt(dmb[:, cols], vb[:, cols])
                    dvl.append(_dot(wtt_ref[h], dmb[:, cols]))
                dvln = jnp.concatenate(dvl, axis=1)
                dlg_ref[...] += jnp.sum(dvln * vhat, axis=0, keepdims=True)
                dlb_ref[...] += jnp.sum(dvln, axis=0, keepdims=True)
                dvh = dvln * lng_v
                dv = rstd * (
                    dvh
                    - jnp.mean(dvh, axis=-1, keepdims=True)
                    - vhat * jnp.mean(dvh * vhat, axis=-1, keepdims=True)
                )
                dz_ref[rows, :] = (du * u_grad).astype(BF16)
                dzv[rows, :] = (dv * v_grad).astype(BF16)

        @pl.when(p == 1)
        def _():
            dz_ref[...] = dzv[...]

    return _call_riding(
        body,
        exs,
        args=(z, z, dmix, dz, lng, lnb, wt, wtt, biasb, go),
        grid=(s // rb, 2),
        in_specs=[
            pl.BlockSpec((rb, w), lambda i, p: (i, 1)),
            pl.BlockSpec((rb, w), lambda i, p: (i, 2)),
            pl.BlockSpec((rb, w), lambda i, p: (i, 1)),
            pl.BlockSpec(memory_space=pl.ANY),
            pl.BlockSpec((1, w), lambda i, p: (0, 0)),
            pl.BlockSpec((1, w), lambda i, p: (0, 0)),
            pl.BlockSpec((heads, SGU_CHUNK, SGU_CHUNK), lambda i, p: (0, 0, 0)),
            pl.BlockSpec((heads, SGU_CHUNK, SGU_CHUNK), lambda i, p: (0, 0, 0)),
            pl.BlockSpec((heads, SGU_CHUNK, LANES), lambda i, p: (0, 0, 0)),
            pl.BlockSpec((1, w), lambda i, p: (0, 0)),
        ],
        out_specs=[
            pl.BlockSpec((rb, w), lambda i, p: (i, 1 + p)),
            pl.BlockSpec((heads, SGU_CHUNK, SGU_CHUNK), lambda i, p: (0, 0, 0)),
            pl.BlockSpec((heads, SGU_CHUNK, LANES), lambda i, p: (0, 0, 0)),
            pl.BlockSpec((1, w), lambda i, p: (0, 0)),
            pl.BlockSpec((1, w), lambda i, p: (0, 0)),
            pl.BlockSpec((1, w), lambda i, p: (0, 0)),
        ],
        out_shape=[
            jax.ShapeDtypeStruct(dz.shape, dz.dtype),
            jax.ShapeDtypeStruct((heads, SGU_CHUNK, SGU_CHUNK), F32),
            jax.ShapeDtypeStruct((heads, SGU_CHUNK, LANES), F32),
            jax.ShapeDtypeStruct((1, w), F32),
            jax.ShapeDtypeStruct((1, w), F32),
            jax.ShapeDtypeStruct((1, w), F32),
        ],
        scratch_shapes=[pltpu.VMEM((rb, w), BF16)],
        input_output_aliases={3: 0},
        compiler_params=_params(2),
        name="sgu_bwd",
    )


def _in_proj_bwd(dz, w4, x, dres, g, tm, exs, tiles, into, name):
    s, d = x.shape
    nj, _, cw = w4.shape
    first, stop = tiles

    def body(dz_ref, w_ref, x_ref, dr_ref, g_ref, *rest):
        dx_ref, dg_ref, acc = rest[-3:]
        i = pl.program_id(0)
        j = pl.program_id(1)

        @pl.when(j == 0)
        def _():
            acc[...] = jnp.zeros_like(acc)

        @pl.when(jnp.logical_and(i == 0, j == 0))
        def _():
            dg_ref[...] = jnp.zeros_like(dg_ref)

        acc[...] += _dot_nt(dz_ref[...], w_ref[...])

        @pl.when(j == nj - 1)
        def _():
            xv = x_ref[...]
            dx, dgt = _rms_bwd(acc[...], xv, _rms(xv), g_ref[...])
            dx_ref[...] = dr_ref[...] + dx
            dg_ref[...] += jnp.sum(dgt, axis=0, keepdims=True)

    row = pl.BlockSpec((tm, d), lambda i, j: (i + first, 0))
    vec = pl.BlockSpec((1, d), lambda i, j: (0, 0))
    kept = [] if into is None else [into]
    return _call_riding(
        body,
        exs,
        args=(dz, w4, x, dres, g, *kept),
        scratch_shapes=[pltpu.VMEM((tm, d), F32)],
        grid=(stop - first, nj),
        in_specs=[
            pl.BlockSpec((tm, cw), lambda i, j: (i + first, j)),
            pl.BlockSpec((None, d, cw), lambda i, j: (j, 0, 0)),
            row,
            row,
            vec,
        ]
        + [_ANY] * len(kept),
        out_specs=[row, vec],
        out_shape=[jax.ShapeDtypeStruct((s, d), F32), jax.ShapeDtypeStruct((1, d), F32)],
        input_output_aliases={5: 0} if kept else {},
        compiler_params=_params(2),
        name=name,
    )


_ANY = pl.BlockSpec(memory_space=pl.ANY)


def _position():
    x, y, c = lax.axis_index("x"), lax.axis_index("y"), lax.axis_index("c")
    return x, y, c, [(1 - x, y), (x, 1 - y), (1 - x, 1 - y)]


def _remote(src, dst, send_sems, recv_sems, k, to):
    return pltpu.make_async_remote_copy(
        src_ref=src, dst_ref=dst, send_sem=send_sems.at[k], recv_sem=recv_sems.at[k], device_id=to, device_id_type=MESH
    )


class _Riding:
    def __init__(self, srcs, out_shapes, n_sems, start, finish, kept=()):
        self.srcs, self.out_shapes, self.n_sems, self.start, self.finish = srcs, out_shapes, n_sems, start, finish
        self.kept = kept


def _call_riding(body, exs, *, grid, in_specs, out_specs, out_shape, scratch_shapes, args, **kw):
    n_in, n_out, n_scr = len(in_specs), len(out_specs), len(scratch_shapes)
    spec = dict(grid=grid, in_specs=list(in_specs), out_specs=list(out_specs), out_shape=list(out_shape))
    if not exs:
        return _call(body, scratch_shapes=list(scratch_shapes), **spec, **kw)(*args), []
    srcs = [a for ex in exs for a in ex.srcs]
    lands = [a for ex in exs for a in ex.out_shapes]
    xi, xo = len(srcs), len(lands)

    def fused(*refs):
        cin, xin = refs[:n_in], refs[n_in : n_in + xi]
        o = n_in + xi
        cout, xout = refs[o : o + n_out], refs[o + n_out : o + n_out + xo]
        scr = refs[o + n_out + xo :]
        ids = [pl.program_id(a) for a in range(len(grid))]
        first = functools.reduce(jnp.logical_and, [i == 0 for i in ids])
        last = functools.reduce(jnp.logical_and, [i == n - 1 for i, n in zip(ids, grid)])

        def each(half):
            si = so = 0
            for e, ex in enumerate(exs):
                ni, no = len(ex.srcs), len(ex.out_shapes)
                sems = scr[n_scr + 2 * e : n_scr + 2 * e + 2]
                getattr(ex, half)(xin[si : si + ni], xout[so : so + no], *sems)
                si, so = si + ni, so + no

        @pl.when(first)
        def _():
            each("start")

        body(*cin, *cout, *scr[:n_scr])

        @pl.when(last)
        def _():
            each("finish")

    sems = [pltpu.SemaphoreType.DMA((ex.n_sems,)) for ex in exs for _ in range(2)]
    aliases = dict(kw.pop("input_output_aliases", {}))
    si = so = 0
    for ex in exs:
        for a, b in ex.kept:
            aliases[n_in + si + a] = n_out + so + b
        si, so = si + len(ex.srcs), so + len(ex.out_shapes)
    kw["input_output_aliases"] = aliases
    spec["in_specs"] += [_ANY] * xi
    spec["out_specs"] += [_ANY] * xo
    spec["out_shape"] += lands
    outs = _call(fused, scratch_shapes=list(scratch_shapes) + sems, **spec, **kw)(*args, *srcs)
    return outs[:n_out], outs[n_out:]


def _gather_one(shard, peers=(0, 1, 2), into=None):
    def own_copies(ins, outs, send, recv):
        x, y, c, chips = _position()
        me = 2 * x + y
        sib = (x, y, 1 - c)
        cps = [] if into is not None else [_remote(ins[0].at[c], outs[0].at[me, c], send, recv, 0, sib)]
        cps += [_remote(ins[0].at[c], outs[0].at[me, c], send, recv, 1 + k, (*chips[k], c)) for k in peers]
        return cps, (c, me, sib, chips)

    def start(ins, outs, send, recv):
        for cp in own_copies(ins, outs, send, recv)[0]:
            cp.start()

    def finish(ins, outs, send, recv):
        cps, (c, me, sib, chips) = own_copies(ins, outs, send, recv)
        for k in peers:
            px, py = chips[k]
            landed = outs[0].at[2 * px + py, c]
            _remote(ins[0].at[c], landed, send, recv, 1 + k, (px, py, c)).wait_recv()
            cps.append(_remote(landed, landed, send, recv, 4 + k, sib))
            cps[-1].start()
        if into is None:
            _remote(ins[0].at[1 - c], outs[0].at[me, 1 - c], send, recv, 0, sib).wait_recv()
        for k in peers:
            px, py = chips[k]
            theirs = outs[0].at[2 * px + py, 1 - c]
            _remote(theirs, theirs, send, recv, 4 + k, sib).wait_recv()
        for cp in cps:
            cp.wait_send()

    land = jax.ShapeDtypeStruct((N_CHIPS,) + shard.shape, shard.dtype)
    return _Riding([shard] + ([] if into is None else [into]), [land], 7, start, finish, [] if into is None else [(1, 0)])


def _to_owner_one(p, peers=(0, 1, 2), into=None):
    def copies(ins, outs, send, recv):
        x, y, c, chips = _position()
        return [_remote(ins[0].at[2 * chips[k][0] + chips[k][1]], outs[0].at[k], send, recv, k, (*chips[k], c)) for k in peers]

    def start(ins, outs, send, recv):
        for cp in copies(ins, outs, send, recv):
            cp.start()

    def finish(ins, outs, send, recv):
        cps = copies(ins, outs, send, recv)
        for cp in cps:
            cp.wait_recv()
        for cp in cps:
            cp.wait_send()

    land = jax.ShapeDtypeStruct((3,) + p.shape[1:], p.dtype)
    return _Riding([p] + ([] if into is None else [into]), [land], 3, start, finish, [] if into is None else [(1, 0)])


def _to_sibling_one(g):
    def copies(ins, outs, send, recv):
        x, y, c, _ = _position()
        return [_remote(ins[0].at[j, 1 - c], outs[0].at[j], send, recv, j, (x, y, 1 - c)) for j in range(N_CHIPS)]

    def start(ins, outs, send, recv):
        for cp in copies(ins, outs, send, recv):
            cp.start()

    def finish(ins, outs, send, recv):
        cps = copies(ins, outs, send, recv)
        for cp in cps:
            cp.wait_recv()
        for cp in cps:
            cp.wait_send()

    return _Riding([g], [jax.ShapeDtypeStruct((g.shape[0],) + g.shape[2:], g.dtype)], N_CHIPS, start, finish)


def _gather_chips(arrs, name):
    n = len(arrs)

    def body(*refs):
        ins, outs = refs[:n], refs[n : 2 * n]
        send_sems, recv_sems, local_sems = refs[2 * n :]
        x, y, c, chips = _position()
        me = 2 * x + y
        local = [pltpu.make_async_copy(ins[i], outs[i].at[me], local_sems.at[i]) for i in range(n)]
        for cp in local:
            cp.start()
        sends = []
        for k, (px, py) in enumerate(chips):
            for i in range(n):
                sends.append(_remote(ins[i], outs[i].at[me], send_sems, recv_sems, k * n + i, (px, py, c)))
                sends[-1].start()
        for k, (px, py) in enumerate(chips):
            for i in range(n):
                _remote(ins[i], outs[i].at[2 * px + py], send_sems, recv_sems, k * n + i, (px, py, c)).wait_recv()
        for cp in sends:
            cp.wait_send()
        for cp in local:
            cp.wait()

    return _call(
        body,
        in_specs=[_ANY] * n,
        out_specs=[_ANY] * n,
        out_shape=[jax.ShapeDtypeStruct((N_CHIPS,) + a.shape, a.dtype) for a in arrs],
        scratch_shapes=[
            pltpu.SemaphoreType.DMA((3 * n,)),
            pltpu.SemaphoreType.DMA((3 * n,)),
            pltpu.SemaphoreType.DMA((n,)),
        ],
        name=name,
    )(*arrs)


def _chip_exchange(ps, name):
    n = len(ps)

    def body(*refs):
        ins, outs = refs[:n], refs[n : 2 * n]
        send_sems, recv_sems = refs[2 * n :]
        x, y, c, chips = _position()
        sends = []
        for k, (px, py) in enumerate(chips):
            for i in range(n):
                sends.append(_remote(ins[i].at[2 * px + py], outs[i].at[k], send_sems, recv_sems, k * n + i, (px, py, c)))
                sends[-1].start()
        for cp in sends:
            cp.wait_recv()
        for cp in sends:
            cp.wait_send()

    return _call(
        body,
        in_specs=[_ANY] * n,
        out_specs=[_ANY] * n,
        out_shape=[jax.ShapeDtypeStruct((3,) + p.shape[1:], p.dtype) for p in ps],
        scratch_shapes=[pltpu.SemaphoreType.DMA((3 * n,)), pltpu.SemaphoreType.DMA((3 * n,))],
        name=name,
    )(*ps)


def _pair_exchange(gs, name):
    n = len(gs)

    def body(*refs):
        ins, outs = refs[:n], refs[n : 2 * n]
        send_sems, recv_sems = refs[2 * n :]
        x, y, c, _ = _position()
        sib = (x, y, 1 - c)
        cps = []
        for i in range(n):
            for j in range(N_CHIPS):
                cps.append(_remote(ins[i].at[j, 1 - c], outs[i].at[j], send_sems, recv_sems, i * N_CHIPS + j, sib))
                cps[-1].start()
        for cp in cps:
            cp.wait_recv()
        for cp in cps:
            cp.wait_send()

    return _call(
        body,
        in_specs=[_ANY] * n,
        out_specs=[_ANY] * n,
        out_shape=[jax.ShapeDtypeStruct((g.shape[0],) + g.shape[2:], g.dtype) for g in gs],
        scratch_shapes=[pltpu.SemaphoreType.DMA((n * N_CHIPS,)), pltpu.SemaphoreType.DMA((n * N_CHIPS,))],
        name=name,
    )(*gs)


def _place_own(shard, gathered, pos, name):
    _, hr, cols = shard.shape
    tr = _row_tile(hr, 512)

    def body(pos_ref, s_ref, g_any, o_ref):
        del pos_ref, g_any
        o_ref[...] = s_ref[...]

    return _call(
        body,
        grid_spec=pltpu.PrefetchScalarGridSpec(
            num_scalar_prefetch=1,
            grid=(hr // tr,),
            in_specs=[pl.BlockSpec((None, tr, cols), lambda i, p: (p[1], i, 0)), _ANY],
            out_specs=pl.BlockSpec((None, None, tr, cols), lambda i, p: (p[0], p[1], i, 0)),
        ),
        out_shape=jax.ShapeDtypeStruct(gathered.shape, gathered.dtype),
        input_output_aliases={2: 0},
        compiler_params=_params(1),
        name=name,
    )(pos, shard, gathered)


def _pair_share(fs, name):
    n = len(fs)

    def body(*refs):
        ins, outs = refs[:n], refs[n : 2 * n]
        send_sems, recv_sems = refs[2 * n :]
        x, y, c, _ = _position()
        sib = (x, y, 1 - c)
        sends = [_remote(ins[i].at[c], outs[i].at[c], send_sems, recv_sems, i, sib) for i in range(n)]
        for cp in sends:
            cp.start()
        for i in range(n):
            _remote(ins[i].at[1 - c], outs[i].at[1 - c], send_sems, recv_sems, i, sib).wait_recv()
        for cp in sends:
            cp.wait_send()

    return _call(
        body,
        in_specs=[_ANY] * n,
        out_specs=[_ANY] * n,
        out_shape=[jax.ShapeDtypeStruct(f.shape, f.dtype) for f in fs],
        input_output_aliases={i: i for i in range(n)},
        scratch_shapes=[pltpu.SemaphoreType.DMA((n,)), pltpu.SemaphoreType.DMA((n,))],
        name=name,
    )(*fs)


def _pair_sum(g, r, core, name):
    nj, _, hr, cols = g.shape
    tr = _row_tile(hr, 256)

    def body(c_ref, g_ref, r_ref, o_ref):
        del c_ref
        o_ref[...] = (g_ref[...].astype(F32) + r_ref[...].astype(F32)).astype(o_ref.dtype)

    return _call(
        body,
        grid_spec=pltpu.PrefetchScalarGridSpec(
            num_scalar_prefetch=1,
            grid=(nj, hr // tr),
            in_specs=[
                pl.BlockSpec((None, None, tr, cols), lambda j, i, c: (j, c[0], i, 0)),
                pl.BlockSpec((None, tr, cols), lambda j, i, c: (j, i, 0)),
            ],
            out_specs=pl.BlockSpec((None, tr, cols), lambda j, i, c: (j, i, 0)),
        ),
        out_shape=jax.ShapeDtypeStruct((nj, hr, cols), g.dtype),
        compiler_params=_params(2),
        name=name,
    )(core, g, r)


def _chip_sum(p, q, pos, name):
    nq, hr, cols = q.shape
    tr = _row_tile(hr, 256)

    def body(pos_ref, p_ref, *refs):
        del pos_ref
        o_ref = refs[nq]
        tot = p_ref[...].astype(F32)
        for k in range(nq):
            tot = tot + refs[k][...].astype(F32)
        o_ref[...] = tot

    return _call(
        body,
        grid_spec=pltpu.PrefetchScalarGridSpec(
            num_scalar_prefetch=1,
            grid=(hr // tr,),
            in_specs=[pl.BlockSpec((None, tr, cols), lambda i, s: (s[0], i, 0))]
            + [pl.BlockSpec((None, tr, cols), functools.partial(lambda k, i, s: (k, i, 0), k)) for k in range(nq)],
            out_specs=pl.BlockSpec((None, tr, cols), lambda i, s: (s[1], i, 0)),
        ),
        out_shape=jax.ShapeDtypeStruct((2, hr, cols), F32),
        compiler_params=_params(1),
        name=name,
    )(pos, p, *([q] * nq))


def _adamw(w, g, m, v, name):
    rows, cols = w.shape
    tr = _row_tile(rows, max(8, (2**18) // cols), mult=8)
    c1 = 1.0 - ADAM_B1**ADAM_STEP
    c2 = 1.0 - ADAM_B2**ADAM_STEP

    def body(w_ref, g_ref, m_ref, v_ref, gk_ref, d_ref, nm_ref, nv_ref):
        gv = g_ref[...]
        gk_ref[...] = gv
        nm = ADAM_B1 * m_ref[...] + (1.0 - ADAM_B1) * gv
        nv = ADAM_B2 * v_ref[...] + (1.0 - ADAM_B2) * (gv * gv)
        nm_ref[...] = nm
        nv_ref[...] = nv
        d_ref[...] = -ADAM_LR * ((nm / c1) / (jnp.sqrt(nv / c2) + ADAM_EPS) + ADAM_WD * w_ref[...])

    spec = pl.BlockSpec((tr, cols), lambda i: (i, 0))
    sds = jax.ShapeDtypeStruct((rows, cols), F32)
    return _call(
        body,
        grid=(rows // tr,),
        in_specs=[spec] * 4,
        out_specs=[spec] * 4,
        out_shape=[sds] * 4,
        compiler_params=_params(1),
        name=name,
    )(w, g, m, v)


_TILE_ELEMS = 8 * LANES
_FLAT_ROW_MULT = 8 * 2 * N_CHIPS


def _flat_rows(shape):
    n = math.prod(shape)
    return (n + _TILE_ELEMS - 1) // _TILE_ELEMS * 8


def _pack(arrs):
    parts = []
    for a in arrs:
        rows = _flat_rows(a.shape)
        flat = a.reshape(-1).astype(F32)
        flat = jnp.pad(flat, (0, rows * LANES - flat.shape[0]))
        parts.append(flat.reshape(rows, LANES))
    total = sum(p.shape[0] for p in parts)
    pad = -total % _FLAT_ROW_MULT
    if pad:
        parts.append(jnp.zeros((pad, LANES), F32))
    return jnp.concatenate(parts, axis=0)


def _unpack(flat, shapes):
    out, row = [], 0
    for shp in shapes:
        rows = _flat_rows(shp)
        out.append(flat[row : row + rows].reshape(-1)[: math.prod(shp)].reshape(shp))
        row += rows
    return out


def kernel(x, norm_mix_g, w_in, ssm_a_re, ssm_a_im, ssm_b_re, ssm_b_im, ssm_c_re, ssm_c_im, ssm_d, ssm_log_dt, ssm_glu_w, ssm_glu_b, sgu_ln_g, sgu_ln_b, sgu_w, sgu_b, out_norm_ssm_g, out_norm_sgu_g, w_out, norm_mlp_g, w_up, w_down, norm_final_g, loss_target, m_norm_mix_g, m_w_in, m_ssm_a_re, m_ssm_a_im, m_ssm_b_re, m_ssm_b_im, m_ssm_c_re, m_ssm_c_im, m_ssm_d, m_ssm_log_dt, m_ssm_glu_w, m_ssm_glu_b, m_sgu_ln_g, m_sgu_ln_b, m_sgu_w, m_sgu_b, m_out_norm_ssm_g, m_out_norm_sgu_g, m_w_out, m_norm_mlp_g, m_w_up, m_w_down, m_norm_final_g, v_norm_mix_g, v_w_in, v_ssm_a_re, v_ssm_a_im, v_ssm_b_re, v_ssm_b_im, v_ssm_c_re, v_ssm_c_im, v_ssm_d, v_ssm_log_dt, v_ssm_glu_w, v_ssm_glu_b, v_sgu_ln_g, v_sgu_ln_b, v_sgu_w, v_sgu_b, v_out_norm_ssm_g, v_out_norm_sgu_g, v_w_out, v_norm_mlp_g, v_w_up, v_w_down, v_norm_final_g):
    weights = dict(norm_mix_g=norm_mix_g, w_in=w_in, ssm_a_re=ssm_a_re, ssm_a_im=ssm_a_im, ssm_b_re=ssm_b_re, ssm_b_im=ssm_b_im, ssm_c_re=ssm_c_re, ssm_c_im=ssm_c_im, ssm_d=ssm_d, ssm_log_dt=ssm_log_dt, ssm_glu_w=ssm_glu_w, ssm_glu_b=ssm_glu_b, sgu_ln_g=sgu_ln_g, sgu_ln_b=sgu_ln_b, sgu_w=sgu_w, sgu_b=sgu_b, out_norm_ssm_g=out_norm_ssm_g, out_norm_sgu_g=out_norm_sgu_g, w_out=w_out, norm_mlp_g=norm_mlp_g, w_up=w_up, w_down=w_down, norm_final_g=norm_final_g)
    mom_m = dict(norm_mix_g=m_norm_mix_g, w_in=m_w_in, ssm_a_re=m_ssm_a_re, ssm_a_im=m_ssm_a_im, ssm_b_re=m_ssm_b_re, ssm_b_im=m_ssm_b_im, ssm_c_re=m_ssm_c_re, ssm_c_im=m_ssm_c_im, ssm_d=m_ssm_d, ssm_log_dt=m_ssm_log_dt, ssm_glu_w=m_ssm_glu_w, ssm_glu_b=m_ssm_glu_b, sgu_ln_g=m_sgu_ln_g, sgu_ln_b=m_sgu_ln_b, sgu_w=m_sgu_w, sgu_b=m_sgu_b, out_norm_ssm_g=m_out_norm_ssm_g, out_norm_sgu_g=m_out_norm_sgu_g, w_out=m_w_out, norm_mlp_g=m_norm_mlp_g, w_up=m_w_up, w_down=m_w_down, norm_final_g=m_norm_final_g)
    mom_v = dict(norm_mix_g=v_norm_mix_g, w_in=v_w_in, ssm_a_re=v_ssm_a_re, ssm_a_im=v_ssm_a_im, ssm_b_re=v_ssm_b_re, ssm_b_im=v_ssm_b_im, ssm_c_re=v_ssm_c_re, ssm_c_im=v_ssm_c_im, ssm_d=v_ssm_d, ssm_log_dt=v_ssm_log_dt, ssm_glu_w=v_ssm_glu_w, ssm_glu_b=v_ssm_glu_b, sgu_ln_g=v_sgu_ln_g, sgu_ln_b=v_sgu_ln_b, sgu_w=v_sgu_w, sgu_b=v_sgu_b, out_norm_ssm_g=v_out_norm_ssm_g, out_norm_sgu_g=v_out_norm_sgu_g, w_out=v_w_out, norm_mlp_g=v_norm_mlp_g, w_up=v_w_up, w_down=v_w_down, norm_final_g=v_norm_final_g)
    names = list(weights)
    large = ["w_in", "ssm_glu_w", "w_out", "w_up", "w_down"]
    small = [n for n in names if n not in large]

    s, d = x.shape[1], x.shape[2]
    xs = x.reshape(s, d)
    target = loss_target.reshape(s, d)
    width = ssm_glu_w.shape[-1]
    ff = w_down.shape[1] * N_CHIPS
    tm = min(512, s)
    core = lax.axis_index("c").astype(jnp.int32).reshape(1)
    chip = (2 * lax.axis_index("x") + lax.axis_index("y")).astype(jnp.int32).reshape(1)
    pos = jnp.concatenate([chip, core])

    shards = [w[0].astype(BF16).reshape(2, w.shape[1] // 2, w.shape[2]) for w in (w_in, ssm_glu_w, w_out, w_up, w_down)]
    ssm_args = (ssm_a_re[0], ssm_a_im[0], ssm_b_re[0], ssm_b_im[0], ssm_c_re[0], ssm_c_im[0], ssm_d[0], ssm_log_dt[0])
    (k2c, w2c, v2c, al), ssm_vjp = jax.vjp(_ssm_mats, *ssm_args)
    consts = _spread_consts(ssm_b_re.shape[-1], ssm_a_re.shape[-1])
    (k2b, w2b, v2b), (w_in_g,) = _ssm_spread(k2c, w2c, v2c, consts, [_gather_one(shards[0])])
    w_in4 = _place_own(shards[0], w_in_g, pos, "place_w_in").reshape(N_CHIPS, d, w_in.shape[2])
    causal = jnp.tril(jnp.ones((SGU_CHUNK, SGU_CHUNK), dtype=bool))
    wt = jnp.where(causal[None], sgu_w[0], 0.0)
    wtb = wt.astype(BF16)
    wttb = jnp.swapaxes(wt, 1, 2).astype(BF16)
    heads = sgu_w.shape[1]
    biasb = jnp.broadcast_to(sgu_b[0][:, :, None], (heads, SGU_CHUNK, LANES))

    tm_big = min(1024, s)
    (z, h1b), (glu_g, out_g, up_g) = _in_proj(
        xs, norm_mix_g, w_in4, tm_big,
        [_gather_one(shards[1]), _gather_one(shards[2]), _gather_one(shards[3], peers=(0, 1))])
    wg_full = _place_own(shards[1], glu_g, pos, "place_glu_w").reshape(width, width)
    w_out_full = _place_own(shards[2], out_g, pos, "place_w_out").reshape(d, d)
    (y_pre, xprev, us), (up_g,) = _ssm_fwd(z, k2b, w2b, v2b, al, [_gather_one(shards[3], peers=(2,), into=up_g)])
    up4 = _place_own(shards[3], up_g, pos, "place_w_up").reshape(N_CHIPS, d, w_up.shape[2])
    mixed = _glu_fwd(y_pre, wg_full, ssm_glu_b, out_norm_ssm_g, d, tm)
    mixed = _sgu_fwd(z, mixed, sgu_ln_g, sgu_ln_b, wtb, biasb, out_norm_sgu_g, tm)
    x1, h2b = _out_proj(xs, mixed, w_out_full, norm_mlp_g, tm)
    tf = min(1024, up4.shape[-1])
    (up,), (down_g,) = _mlp_up(h2b, up4, tm_big, min(2048, up4.shape[-1]), [_gather_one(shards[4])])
    w_down_full = _place_own(shards[4], down_g, pos, "place_w_down").reshape(ff, d)
    loss_part, dx2, dx2b, d_norm_final = _mlp_down_loss(
        x1, up, w_down_full, target, norm_final_g.reshape(1, d), tm, tf)

    dup, dh2 = _mlp_bwd(dx2b, up, up4, w_down_full, tm_big, min(512, up4.shape[-1]))
    dx1, dx1b, d_norm_mlp = _norm_bwd(dh2, x1, dx2, norm_mlp_g, min(256, s), "norm_mlp_bwd")
    hr_big = d // 2
    tkk = min(1024, s)
    g_down = _grad_w(
        up, dx2b, (N_CHIPS, 2, hr_big, d),
        lambda i, j: (i // (2 * (hr_big // min(1024, hr_big))), (i // (hr_big // min(1024, hr_big))) % 2,
                      i % (hr_big // min(1024, hr_big)), j),
        min(1024, hr_big), d, tkk, True, "grad_w_down")
    tn_up = min(2048, up4.shape[-1])
    per_up = up4.shape[-1] // tn_up
    g_up, sib_down = _grad_w(
        h2b, dup, (N_CHIPS, 2, hr_big, up4.shape[-1]),
        lambda i, j: (j // per_up, i // (hr_big // min(1024, hr_big)), i % (hr_big // min(1024, hr_big)), j % per_up),
        min(1024, hr_big), tn_up, tkk, False, "grad_w_up", exs=[_to_sibling_one(g_down)])
    pair_down = _pair_sum(g_down, sib_down, core, "pair_sum_w_down")
    dmix = _out_proj_bwd(dx1b, w_out_full, tm)
    hr_out = d // (2 * N_CHIPS)
    g_out, sib_up = _grad_w(
        mixed, dx1b, (1, 1, d, d), lambda i, j: (0, 0, i, j), min(1024, d), d, tkk, False, "grad_w_out",
        exs=[_to_sibling_one(g_up)])
    g_out = g_out.reshape(N_CHIPS, 2, hr_out, d)
    pair_up = _pair_sum(g_up, sib_up, core, "pair_sum_w_up")
    (dy_pre, g_glu, d_glu_b, d_norm_ssm), (sib_out,) = _glu_bwd(
        y_pre, dmix, wg_full, ssm_glu_b, out_norm_ssm_g, tm, [_to_sibling_one(g_out)])
    pair_out = _pair_sum(g_out, sib_out, core, "pair_sum_w_out")
    dys, gst, d_al = _ssm_bwd_state(dy_pre, v2b, al, xprev)
    (dz, d_k2, d_w2, d_v2), (chips_down,) = _ssm_bwd_main(
        us, dys, k2b, w2b, xprev, gst, consts, s, z.shape[1], [_to_owner_one(pair_down)])
    (dz, d_wt, d_bias, d_ln_g, d_ln_b, d_norm_sgu), (chips_up,) = _sgu_bwd(
        z, dmix, dz, sgu_ln_g, sgu_ln_b, wtb, wttb, biasb, out_norm_sgu_g, tm,
        [_to_owner_one(pair_up, peers=(0, 1))])
    cw_in = w_in4.shape[-1]
    g_in, chips_out, chips_up = _grad_w(
        h1b, dz, (N_CHIPS, 2, hr_big, cw_in),
        lambda i, j: (j, i // (hr_big // min(1024, hr_big)), i % (hr_big // min(1024, hr_big)), 0),
        min(1024, hr_big), 2 * cw_in, tkk, False, "grad_w_in",
        exs=[_to_owner_one(pair_out), _to_owner_one(pair_up, peers=(2,), into=chips_up)], chunks=2)
    (sib_in,) = _pair_exchange([g_in], "w_in_to_sibling")
    pair_in = _pair_sum(g_in, sib_in, core, "pair_sum_w_in")
    tm_x = min(512, s // 2)
    n_x = s // tm_x
    (gx_half, dg_a), (chips_in,) = _in_proj_bwd(
        dz, w_in4, xs, dx1, norm_mix_g, tm_x, [_to_owner_one(pair_in)], (0, n_x // 2), None, "in_proj_bwd_a")
    (grad_x, dg_b), _ = _in_proj_bwd(
        dz, w_in4, xs, dx1, norm_mix_g, tm_x, [], (n_x // 2, n_x), gx_half, "in_proj_bwd_b")
    d_norm_mix = dg_a + dg_b

    d_ssm = ssm_vjp((d_k2, d_w2, d_v2, d_al))
    small_grads = dict(
        norm_mix_g=d_norm_mix, ssm_a_re=d_ssm[0], ssm_a_im=d_ssm[1], ssm_b_re=d_ssm[2], ssm_b_im=d_ssm[3],
        ssm_c_re=d_ssm[4], ssm_c_im=d_ssm[5], ssm_d=d_ssm[6], ssm_log_dt=d_ssm[7], ssm_glu_b=d_glu_b,
        sgu_ln_g=d_ln_g, sgu_ln_b=d_ln_b, sgu_w=jnp.where(causal[None], d_wt, 0.0), sgu_b=d_bias[:, :, 0],
        out_norm_ssm_g=d_norm_ssm, out_norm_sgu_g=d_norm_sgu, norm_mlp_g=d_norm_mlp, norm_final_g=d_norm_final)
    flat = _pack([small_grads[n] for n in small])
    hr_small = flat.shape[0] // (2 * N_CHIPS)
    g_small = flat.reshape(N_CHIPS, 2, hr_small, LANES)

    hr_glu = width // (2 * N_CHIPS)
    grads = [g_glu.reshape(N_CHIPS, 2, hr_glu, width), g_small]
    tags = ["glu_w", "small"]
    from_sib = _pair_exchange(grads, "grads_to_sibling")
    pair = [_pair_sum(g, r, core, "pair_sum_" + t) for g, r, t in zip(grads, from_sib, tags)]
    from_chips = _chip_exchange(pair, "grads_to_owner")
    pair = [pair_in, pair[0], pair_out, pair_up, pair_down, pair[1]]
    from_chips = [chips_in, from_chips[0], chips_out, chips_up, chips_down, from_chips[1]]
    tags = ["w_in", "glu_w", "w_out", "w_up", "w_down", "small"]
    halves = [_chip_sum(p, q, pos, "chip_sum_" + t) for p, q, t in zip(pair, from_chips, tags)]
    owned = _pair_share(halves, "grads_to_both_cores")
    (small_all,) = _gather_chips([owned[5]], "gather_small_grads")
    small_flat = small_all.reshape(flat.shape)

    grad_out, delta_out, m_out, v_out = {}, {}, {}, {}
    for n, g in zip(large, owned[:5]):
        shp = weights[n].shape
        g2, dl, nm, nv = _adamw(
            weights[n][0], g.reshape(shp[1], shp[2]), mom_m[n][0], mom_v[n][0], "adamw_" + n)
        grad_out[n], delta_out[n], m_out[n], v_out[n] = g2.reshape(shp), dl.reshape(shp), nm.reshape(shp), nv.reshape(shp)
    shapes = [weights[n].shape for n in small]
    g2, dl, nm, nv = _adamw(
        _pack([weights[n] for n in small]), small_flat, _pack([mom_m[n] for n in small]),
        _pack([mom_v[n] for n in small]), "adamw_small")
    for n, g, a, b, c in zip(small, _unpack(g2, shapes), _unpack(dl, shapes), _unpack(nm, shapes), _unpack(nv, shapes)):
        grad_out[n], delta_out[n], m_out[n], v_out[n] = g, a, b, c

    loss = lax.psum(loss_part[0, 0], ("x", "y", "c"))
    return (loss, grad_x.reshape(x.shape), *[grad_out[n] for n in names], *[delta_out[n] for n in names],
            *[m_out[n] for n in names], *[v_out[n] for n in names])
```

```python
import functools
import math

import numpy as np
import jax
import jax.numpy as jnp
from jax import lax
from jax.experimental import pallas as pl
from jax.experimental.pallas import tpu as pltpu

F32 = jnp.float32
BF16 = jnp.bfloat16
MESH = pl.DeviceIdType.MESH
HIGHEST = lax.Precision.HIGHEST

EPS = 1e-6
ADAM_LR = 0.001
ADAM_B1 = 0.9
ADAM_B2 = 0.999
ADAM_EPS = 1e-08
ADAM_WD = 0.01
ADAM_STEP = 10

N_CHIPS = 4
LANES = 128
SSM_GROUP = 16
SSM_STATE = 64
GROUPS_PER_TILE = LANES // SSM_GROUP
CHUNK = 16
PAIRS = CHUNK // 2
SGU_CHUNK = 128
VMEM_LIMIT = 56 * 2**20


def _params(n_axes, vmem=VMEM_LIMIT):
    return pltpu.CompilerParams(dimension_semantics=("arbitrary",) * n_axes, vmem_limit_bytes=vmem)


def _call(body, **kw):
    return pl.pallas_call(body, **kw)


def _dot(a, b):
    return jnp.dot(a, b, preferred_element_type=F32)


def _dot_nt(a, b):
    return lax.dot_general(a, b, (((1,), (1,)), ((), ())), preferred_element_type=F32)


def _dot_tn(a, b):
    return lax.dot_general(a, b, (((0,), (0,)), ((), ())), preferred_element_type=F32)


_GELU_K = math.sqrt(2.0 / math.pi)
_GELU_C = 0.044715


def _gelu_both(x):
    x2 = x * x
    t = jnp.tanh(x * (_GELU_K + (_GELU_K * _GELU_C) * x2))
    hx = 0.5 * x
    onep = 1.0 + t
    return hx * onep, 0.5 * onep + hx * (1.0 - t * t) * (_GELU_K + (3.0 * _GELU_K * _GELU_C) * x2)


def _gelu(x):
    return _gelu_both(x)[0]


def _sigmoid(x):
    return 1.0 / (1.0 + jnp.exp(-x))


def _rms(x):
    return lax.rsqrt(jnp.mean(x * x, axis=-1, keepdims=True) + EPS)


def _rms_bwd(dy, x, r, g):
    a = dy * g
    dx = r * a - x * (r * r * r) * jnp.mean(a * x, axis=-1, keepdims=True)
    return dx, dy * x * r


def _row_tile(rows, target, mult=16):
    for t in range(min(rows, target), 0, -1):
        if rows % t == 0 and t % mult == 0:
            return t
    return rows


def _ssm_mats(a_re, a_im, b_re, b_im, c_re, c_im, d, log_dt):
    g, p = a_re.shape
    h = b_re.shape[-1]
    nt = g // GROUPS_PER_TILE
    dt = jnp.exp(log_dt)[:, None]
    lr, li = a_re * dt, a_im * dt

    def apow(l, lr, li):
        mag = jnp.exp(lr * l)
        return mag * jnp.cos(li * l), mag * jnp.sin(li * l)

    ar, ai = apow(1.0, lr, li)
    den = a_re * a_re + a_im * a_im
    qr = ((ar - 1.0) * a_re + ai * a_im) / den
    qi = (ai * a_re - (ar - 1.0) * a_im) / den
    bt_re, bt_im = jnp.swapaxes(b_re, 1, 2), jnp.swapaxes(b_im, 1, 2)
    bbr = qr[:, None, :] * bt_re - qi[:, None, :] * bt_im
    bbi = qr[:, None, :] * bt_im + qi[:, None, :] * bt_re
    b_same = jnp.concatenate([bbr, bbi], axis=-1)[:, None]
    b_swap = jnp.concatenate([bbi, bbr], axis=-1)[:, None]
    ls = jnp.arange(CHUNK + 1, dtype=F32)[None, :, None]
    pr, pi = apow(ls, lr[:, None, :], li[:, None, :])
    p_same = jnp.concatenate([pr, pr], axis=-1)[:, :, None, :]
    p_sign = jnp.concatenate([-pi, pi], axis=-1)[:, :, None, :]
    tj = p_same[:, :CHUNK] * b_same + p_sign[:, :CHUNK] * b_swap
    c_conj = jnp.concatenate([c_re, -c_im], axis=-1)
    k = jnp.einsum("goq,gliq->glio", c_conj, tj, precision=HIGHEST)
    k = k.at[:, 0].add(d[:, :, None] * jnp.eye(h, dtype=F32))

    zero = jnp.zeros_like(k[:, :1])
    kz = jnp.concatenate([zero, zero, k], axis=1).reshape(g, PAIRS + 1, 2, h, h)
    even, odd = kz[:, :, 0], kz[:, :, 1]
    fill = jnp.zeros((g, PAIRS, h, LANES - 2 * h), F32)
    row0 = jnp.concatenate([even[:, 1:], odd[:, 1:], fill], axis=-1)
    row1 = jnp.concatenate([odd[:, :-1], even[:, 1:], fill], axis=-1)
    k2c = jnp.stack([row0, row1], axis=2)
    w2c = tj[:, ::-1].reshape(g, PAIRS, 2, h, 2 * p)
    c_cross = jnp.concatenate([-c_im, -c_re], axis=-1)
    p_imag = jnp.concatenate([pi, pi], axis=-1)[:, :, None, :]
    v2c = (c_conj[:, None] * p_same[:, 1:] + c_cross[:, None] * p_imag[:, 1:]).reshape(g, PAIRS, 2, h, 2 * p)
    al = jnp.stack([pr[:, CHUNK].reshape(nt, -1), pi[:, CHUNK].reshape(nt, -1)], axis=1)
    return k2c, w2c, v2c, al


def _spread_consts(h, p):
    gg = GROUPS_PER_TILE
    row_g = (np.arange(2 * gg * h) // h) % gg
    colk = np.arange(2 * gg * h)
    rep_k = np.zeros((LANES, 2 * gg * h), np.float32)
    rep_k[(colk // (gg * h)) * h + colk % h, colk] = 1.0
    mask_k = (row_g[:, None] == ((colk // h) % gg)[None, :]).astype(np.float32)
    cols = np.arange(2 * gg * p)
    rep_s = np.zeros((2 * p, 2 * gg * p), np.float32)
    rep_s[(cols // (gg * p)) * p + cols % p, cols] = 1.0
    mask_s = (row_g[:, None] == ((cols // p) % gg)[None, :]).astype(np.float32)
    return tuple(jnp.asarray(a, BF16) for a in (rep_k, mask_k, rep_s, mask_s))


def _ssm_spread(k2c, w2c, v2c, consts, exs):
    h = k2c.shape[3]
    nt = k2c.shape[0] // GROUPS_PER_TILE
    rep_k, mask_k, rep_s, mask_s = consts
    nk, ns = rep_k.shape[1], rep_s.shape[1]

    def body(k_ref, w_ref, v_ref, rk_ref, mk_ref, rs_ref, ms_ref, ko_ref, wo_ref, vo_ref):
        for q in range(PAIRS):
            for c_ref, o_ref, r_ref, m_ref in ((k_ref, ko_ref, rk_ref, mk_ref), (w_ref, wo_ref, rs_ref, ms_ref),
                                               (v_ref, vo_ref, rs_ref, ms_ref)):
                rows = jnp.concatenate(
                    [c_ref[gi, q, st] for st in range(2) for gi in range(GROUPS_PER_TILE)], axis=0)
                o_ref[q] = (_dot(rows.astype(BF16), r_ref[...]) * m_ref[...]).astype(BF16)

    compact = pl.BlockSpec((GROUPS_PER_TILE, PAIRS, 2, h, LANES), lambda t: (t, 0, 0, 0, 0))

    def tile(cols):
        return pl.BlockSpec((None, PAIRS, nk, cols), lambda t: (t, 0, 0, 0))

    def whole(a):
        return pl.BlockSpec(a.shape, lambda t: (0, 0))

    return _call_riding(
        body,
        exs,
        args=(k2c, w2c, v2c, rep_k, mask_k, rep_s, mask_s),
        scratch_shapes=[],
        grid=(nt,),
        in_specs=[compact, compact, compact, whole(rep_k), whole(mask_k), whole(rep_s), whole(mask_s)],
        out_specs=[tile(nk), tile(ns), tile(ns)],
        out_shape=[
            jax.ShapeDtypeStruct((nt, PAIRS, nk, nk), BF16),
            jax.ShapeDtypeStruct((nt, PAIRS, nk, ns), BF16),
            jax.ShapeDtypeStruct((nt, PAIRS, nk, ns), BF16),
        ],
        compiler_params=_params(1),
        name="ssm_spread",
    )


def _gather_blocks(full, mask, rep):
    return _dot_nt((full * mask).astype(BF16), rep)


def _in_proj(x, g, w4, tm, exs):
    s, d = x.shape
    nj, _, cw = w4.shape

    def body(x_ref, g_ref, w_ref, z_ref, h_ref):
        @pl.when(pl.program_id(1) == 0)
        def _():
            xv = x_ref[...]
            h_ref[...] = (xv * _rms(xv) * g_ref[...]).astype(BF16)

        z_ref[...] = _dot(h_ref[...], w_ref[...])

    return _call_riding(
        body,
        exs,
        scratch_shapes=[],
        args=(x, g, w4),
        grid=(s // tm, nj),
        in_specs=[
            pl.BlockSpec((tm, d), lambda i, j: (i, 0)),
            pl.BlockSpec((1, d), lambda i, j: (0, 0)),
            pl.BlockSpec((None, d, cw), lambda i, j: (j, 0, 0)),
        ],
        out_specs=[pl.BlockSpec((tm, cw), lambda i, j: (i, j)), pl.BlockSpec((tm, d), lambda i, j: (i, 0))],
        out_shape=[jax.ShapeDtypeStruct((s, nj * cw), F32), jax.ShapeDtypeStruct((s, d), BF16)],
        compiler_params=_params(2),
        name="in_proj",
    )


def _ssm_fwd(z, k2, w2, v2, al, exs):
    s = z.shape[0]
    nt = k2.shape[0]
    nc = s // CHUNK
    ns = w2.shape[-1]
    hs = ns // 2

    def body(u_ref, k_ref, w_ref, v_ref, al_ref, y_ref, xp_ref, us_ref, xloc):
        for q in range(PAIRS):
            us_ref[q, :, 0:LANES] = u_ref[pl.ds(2 * q, nc, stride=CHUNK), :].astype(BF16)
            us_ref[q, :, LANES : 2 * LANES] = u_ref[pl.ds(2 * q + 1, nc, stride=CHUNK), :].astype(BF16)
        acc = _dot(us_ref[0], w_ref[0])
        for q in range(1, PAIRS):
            acc = acc + _dot(us_ref[q], w_ref[q])
        xloc[...] = acc
        ar = al_ref[0:1, :]
        ai = al_ref[1:2, :]

        def step(c, carry):
            xr, xi = carry
            xp_ref[pl.ds(c, 1), 0:hs] = xr
            xp_ref[pl.ds(c, 1), hs:ns] = xi
            lr = xloc[pl.ds(c, 1), 0:hs]
            li = xloc[pl.ds(c, 1), hs:ns]
            return ar * xr - ai * xi + lr, ar * xi + ai * xr + li

        zero = jnp.zeros((1, hs), F32)
        lax.fori_loop(0, nc, step, (zero, zero))
        xpb = xp_ref[...].astype(BF16)
        for r in range(PAIRS):
            acc = _dot_nt(xpb, v_ref[r])
            for q in range(r + 1):
                acc = acc + _dot(us_ref[q], k_ref[r - q])
            y_ref[pl.ds(2 * r, nc, stride=CHUNK), :] = acc[:, 0:LANES]
            y_ref[pl.ds(2 * r + 1, nc, stride=CHUNK), :] = acc[:, LANES : 2 * LANES]

    return _call_riding(
        body,
        exs,
        args=(z, k2, w2, v2, al),
        grid=(nt,),
        in_specs=[
            pl.BlockSpec((s, LANES), lambda t: (0, t)),
            pl.BlockSpec((None, PAIRS, 2 * LANES, 2 * LANES), lambda t: (t, 0, 0, 0)),
            pl.BlockSpec((None, PAIRS, 2 * LANES, ns), lambda t: (t, 0, 0, 0)),
            pl.BlockSpec((None, PAIRS, 2 * LANES, ns), lambda t: (t, 0, 0, 0)),
            pl.BlockSpec((None, 2, hs), lambda t: (t, 0, 0)),
        ],
        out_specs=[
            pl.BlockSpec((s, LANES), lambda t: (0, t)),
            pl.BlockSpec((None, nc, ns), lambda t: (t, 0, 0)),
            pl.BlockSpec((None, PAIRS, nc, 2 * LANES), lambda t: (t, 0, 0, 0)),
        ],
        out_shape=[
            jax.ShapeDtypeStruct((s, nt * LANES), F32),
            jax.ShapeDtypeStruct((nt, nc, ns), F32),
            jax.ShapeDtypeStruct((nt, PAIRS, nc, 2 * LANES), BF16),
        ],
        scratch_shapes=[pltpu.VMEM((nc, ns), F32)],
        compiler_params=_params(1),
        name="ssm_fwd",
    )


def _glu_fwd(y_pre, wg, bg, go, d_model, tm):
    s, w = y_pre.shape

    def body(y_ref, wg_ref, bg_ref, go_ref, o_ref):
        yg = _gelu(y_ref[...])
        gate = _sigmoid(_dot(yg.astype(BF16), wg_ref[...]) + bg_ref[...])
        ys = yg * gate
        o_ref[...] = (ys * _rms(ys) * go_ref[...]).astype(BF16)

    return _call(
        body,
        grid=(s // tm,),
        in_specs=[
            pl.BlockSpec((tm, w), lambda i: (i, 0)),
            pl.BlockSpec((w, w), lambda i: (0, 0)),
            pl.BlockSpec((1, w), lambda i: (0, 0)),
            pl.BlockSpec((1, w), lambda i: (0, 0)),
        ],
        out_specs=pl.BlockSpec((tm, w), lambda i: (i, 0)),
        out_shape=jax.ShapeDtypeStruct((s, d_model), BF16),
        compiler_params=_params(1),
        name="glu_fwd",
    )(y_pre, wg, bg, go)


def _sgu_parts(zu, zv, lng, lnb, wt_ref, bias_ref):
    u, u_grad = _gelu_both(zu)
    v, v_grad = _gelu_both(zv)
    mu = jnp.mean(v, axis=-1, keepdims=True)
    vc = v - mu
    rstd = lax.rsqrt(jnp.mean(vc * vc, axis=-1, keepdims=True) + EPS)
    vhat = vc * rstd
    vb = (vhat * lng + lnb).astype(BF16)
    heads = wt_ref.shape[0]
    mix = jnp.concatenate(
        [_dot(wt_ref[h], vb[:, h * LANES : (h + 1) * LANES]) + bias_ref[h] for h in range(heads)], axis=1
    )
    return u, vhat, rstd, vb, mix, u_grad, v_grad


def _sgu_fwd(z, mixed, lng, lnb, wt, biasb, go, rb):
    s = z.shape[0]
    w = lng.shape[-1]
    heads = wt.shape[0]

    def body(zu_ref, zv_ref, m_any, lng_ref, lnb_ref, wt_ref, b_ref, go_ref, o_ref):
        del m_any
        for ck in range(rb // SGU_CHUNK):
            rows = slice(ck * SGU_CHUNK, (ck + 1) * SGU_CHUNK)
            u, _, _, _, mix, _, _ = _sgu_parts(zu_ref[rows, :], zv_ref[rows, :], lng_ref[...], lnb_ref[...], wt_ref, b_ref)
            y = u * mix
            o_ref[rows, :] = (y * _rms(y) * go_ref[...]).astype(BF16)

    return _call(
        body,
        grid=(s // rb,),
        in_specs=[
            pl.BlockSpec((rb, w), lambda i: (i, 1)),
            pl.BlockSpec((rb, w), lambda i: (i, 2)),
            pl.BlockSpec(memory_space=pl.ANY),
            pl.BlockSpec((1, w), lambda i: (0, 0)),
            pl.BlockSpec((1, w), lambda i: (0, 0)),
            pl.BlockSpec((heads, SGU_CHUNK, SGU_CHUNK), lambda i: (0, 0, 0)),
            pl.BlockSpec((heads, SGU_CHUNK, LANES), lambda i: (0, 0, 0)),
            pl.BlockSpec((1, w), lambda i: (0, 0)),
        ],
        out_specs=pl.BlockSpec((rb, w), lambda i: (i, 1)),
        out_shape=jax.ShapeDtypeStruct(mixed.shape, mixed.dtype),
        input_output_aliases={2: 0},
        compiler_params=_params(1),
        name="sgu_fwd",
    )(z, z, mixed, lng, lnb, wt, biasb, go)


def _out_proj(x, mixed, w_out, g, tm):
    s, d = x.shape

    def body(x_ref, m_ref, w_ref, g_ref, o_ref, h_ref):
        x1 = x_ref[...] + _dot(m_ref[...], w_ref[...])
        o_ref[...] = x1
        h_ref[...] = (x1 * _rms(x1) * g_ref[...]).astype(BF16)

    row = pl.BlockSpec((tm, d), lambda i: (i, 0))
    return _call(
        body,
        grid=(s // tm,),
        in_specs=[row, row, pl.BlockSpec((d, d), lambda i: (0, 0)), pl.BlockSpec((1, d), lambda i: (0, 0))],
        out_specs=[row, row],
        out_shape=[jax.ShapeDtypeStruct((s, d), F32), jax.ShapeDtypeStruct((s, d), BF16)],
        compiler_params=_params(1),
        name="out_proj",
    )(x, mixed, w_out, g)


def _mlp_up(h, w_up4, tm, tf, exs):
    s, d = h.shape
    nj, _, cw = w_up4.shape
    per = cw // tf

    def body(h_ref, wu_ref, up_ref):
        up_ref[...] = _dot(h_ref[...], wu_ref[...]).astype(BF16)

    return _call_riding(
        body,
        exs,
        scratch_shapes=[],
        args=(h, w_up4),
        grid=(s // tm, nj * per),
        in_specs=[
            pl.BlockSpec((tm, d), lambda i, f: (i, 0)),
            pl.BlockSpec((None, d, tf), lambda i, f: (f // per, 0, f % per)),
        ],
        out_specs=[pl.BlockSpec((tm, tf), lambda i, f: (i, f))],
        out_shape=[jax.ShapeDtypeStruct((s, nj * cw), BF16)],
        compiler_params=_params(2),
        name="mlp_up",
    )


def _mlp_down_loss(x1, up, w_down, target, g, tm, tf):
    s, d = x1.shape
    ff = w_down.shape[0]
    nf = ff // tf

    def body(x_ref, up_ref, wd_ref, t_ref, g_ref, loss_ref, dx_ref, dxb_ref, dg_ref, acc):
        i = pl.program_id(0)
        f = pl.program_id(1)

        @pl.when(f == 0)
        def _():
            acc[...] = x_ref[...]

        @pl.when(jnp.logical_and(i == 0, f == 0))
        def _():
            loss_ref[...] = jnp.zeros_like(loss_ref)
            dg_ref[...] = jnp.zeros_like(dg_ref)

        a = jnp.maximum(up_ref[...].astype(F32), 0.0)
        acc[...] += _dot((a * a).astype(BF16), wd_ref[...])

        @pl.when(f == nf - 1)
        def _():
            xv = acc[...]
            r = _rms(xv)
            gv = g_ref[...]
            diff = xv * r * gv - t_ref[...]
            loss_ref[...] += 0.5 * jnp.sum(jnp.mean(diff * diff, axis=-1, keepdims=True), axis=0, keepdims=True)
            dx, dgt = _rms_bwd(diff * (1.0 / d), xv, r, gv)
            dx_ref[...] = dx
            dxb_ref[...] = dx.astype(BF16)
            dg_ref[...] += jnp.sum(dgt, axis=0, keepdims=True)

    row = pl.BlockSpec((tm, d), lambda i, f: (i, 0))
    vec = pl.BlockSpec((1, d), lambda i, f: (0, 0))
    return _call(
        body,
        grid=(s // tm, nf),
        in_specs=[
            row,
            pl.BlockSpec((tm, tf), lambda i, f: (i, f)),
            pl.BlockSpec((tf, d), lambda i, f: (f, 0)),
            row,
            vec,
        ],
        out_specs=[pl.BlockSpec((1, 1), lambda i, f: (0, 0)), row, row, vec],
        out_shape=[
            jax.ShapeDtypeStruct((1, 1), F32),
            jax.ShapeDtypeStruct((s, d), F32),
            jax.ShapeDtypeStruct((s, d), BF16),
            jax.ShapeDtypeStruct((1, d), F32),
        ],
        scratch_shapes=[pltpu.VMEM((tm, d), F32)],
        compiler_params=_params(2),
        name="mlp_down_loss",
    )(x1, up, w_down, target, g)


def _mlp_bwd(dx2b, up, w_up4, w_down, tm, tf):
    s, d = dx2b.shape
    ff = w_down.shape[0]
    cw = w_up4.shape[-1]
    per = cw // tf

    def body(dx_ref, up_ref, wu_ref, wd_ref, dup_ref, dh_ref):
        @pl.when(pl.program_id(1) == 0)
        def _():
            dh_ref[...] = jnp.zeros_like(dh_ref)

        dact = _dot_nt(dx_ref[...], wd_ref[...])
        dupb = (dact * (2.0 * jnp.maximum(up_ref[...].astype(F32), 0.0))).astype(BF16)
        dup_ref[...] = dupb
        dh_ref[...] += _dot_nt(dupb, wu_ref[...])

    return _call(
        body,
        grid=(s // tm, ff // tf),
        in_specs=[
            pl.BlockSpec((tm, d), lambda i, f: (i, 0)),
            pl.BlockSpec((tm, tf), lambda i, f: (i, f)),
            pl.BlockSpec((None, d, tf), lambda i, f: (f // per, 0, f % per)),
            pl.BlockSpec((tf, d), lambda i, f: (f, 0)),
        ],
        out_specs=[pl.BlockSpec((tm, tf), lambda i, f: (i, f)), pl.BlockSpec((tm, d), lambda i, f: (i, 0))],
        out_shape=[jax.ShapeDtypeStruct((s, ff), BF16), jax.ShapeDtypeStruct((s, d), F32)],
        compiler_params=_params(2),
        name="mlp_bwd",
    )(dx2b, up, w_up4, w_down)


def _norm_bwd(dh, x, dres, g, tm, name):
    s, d = x.shape

    def body(dh_ref, x_ref, dr_ref, g_ref, dx_ref, dxb_ref, dg_ref):
        @pl.when(pl.program_id(0) == 0)
        def _():
            dg_ref[...] = jnp.zeros_like(dg_ref)

        xv = x_ref[...]
        dx, dgt = _rms_bwd(dh_ref[...], xv, _rms(xv), g_ref[...])
        tot = dr_ref[...] + dx
        dx_ref[...] = tot
        dxb_ref[...] = tot.astype(BF16)
        dg_ref[...] += jnp.sum(dgt, axis=0, keepdims=True)

    row = pl.BlockSpec((tm, d), lambda i: (i, 0))
    vec = pl.BlockSpec((1, d), lambda i: (0, 0))
    return _call(
        body,
        grid=(s // tm,),
        in_specs=[row, row, row, vec],
        out_specs=[row, row, vec],
        out_shape=[
            jax.ShapeDtypeStruct((s, d), F32),
            jax.ShapeDtypeStruct((s, d), BF16),
            jax.ShapeDtypeStruct((1, d), F32),
        ],
        compiler_params=_params(1),
        name=name,
    )(dh, x, dres, g)


def _grad_w(a, b, out_dims, index_map, tm, tn, tk, relu2, name, exs=(), chunks=1):
    t, m = a.shape
    n = b.shape[1]
    nk = t // tk
    cw = tn // chunks

    def body(a_ref, b_ref, o_ref, acc):
        k = pl.program_id(2)

        @pl.when(k == 0)
        def _():
            acc[...] = jnp.zeros_like(acc)

        av = a_ref[...]
        if relu2:
            r = jnp.maximum(av.astype(F32), 0.0)
            av = (r * r).astype(BF16)
        acc[...] += _dot_tn(av, b_ref[...])

        @pl.when(k == nk - 1)
        def _():
            for c in range(chunks):
                o_ref[c] = acc[:, c * cw : (c + 1) * cw].astype(BF16)

    spec = dict(
        grid=(m // tm, n // tn, nk),
        in_specs=[pl.BlockSpec((tk, tm), lambda i, j, k: (k, i)), pl.BlockSpec((tk, tn), lambda i, j, k: (k, j))],
        out_specs=[pl.BlockSpec((chunks, None, tm, cw), lambda i, j, k: index_map(i, j))],
        out_shape=[jax.ShapeDtypeStruct(out_dims, BF16)],
        scratch_shapes=[pltpu.VMEM((tm, tn), F32)],
        compiler_params=_params(3),
        name=name,
    )
    (g,), landed = _call_riding(body, exs, args=(a, b), **spec)
    return (g, *landed) if exs else g


def _out_proj_bwd(dx1b, w_out, tm):
    s, d = dx1b.shape

    def body(dx_ref, w_ref, o_ref):
        o_ref[...] = _dot_nt(dx_ref[...], w_ref[...])

    return _call(
        body,
        grid=(s // tm,),
        in_specs=[pl.BlockSpec((tm, d), lambda i: (i, 0)), pl.BlockSpec((d, d), lambda i: (0, 0))],
        out_specs=pl.BlockSpec((tm, d), lambda i: (i, 0)),
        out_shape=jax.ShapeDtypeStruct((s, d), F32),
        compiler_params=_params(1),
        name="out_proj_bwd",
    )(dx1b, w_out)


def _glu_bwd(y_pre, dmix, wg, bg, go, tm, exs):
    s, w = y_pre.shape
    n = s // tm

    def body(y_ref, dm_ref, wg_ref, bg_ref, go_ref, dy_ref, dwg_ref, dbg_ref, dgo_ref, acc):
        i = pl.program_id(0)

        @pl.when(i == 0)
        def _():
            acc[...] = jnp.zeros_like(acc)
            dbg_ref[...] = jnp.zeros_like(dbg_ref)
            dgo_ref[...] = jnp.zeros_like(dgo_ref)

        yp = y_ref[...]
        yg, yg_grad = _gelu_both(yp)
        ygb = yg.astype(BF16)
        gate = _sigmoid(_dot(ygb, wg_ref[...]) + bg_ref[...])
        ys = yg * gate
        dys, dgt = _rms_bwd(dm_ref[...], ys, _rms(ys), go_ref[...])
        dgo_ref[...] += jnp.sum(dgt, axis=0, keepdims=True)
        dpre = dys * yg * gate * (1.0 - gate)
        dbg_ref[...] += jnp.sum(dpre, axis=0, keepdims=True)
        dpb = dpre.astype(BF16)
        dyg = dys * gate + _dot_nt(dpb, wg_ref[...])
        dy_ref[...] = dyg * yg_grad
        acc[...] += _dot_tn(ygb, dpb)

        @pl.when(i == n - 1)
        def _():
            dwg_ref[...] = acc[...].astype(BF16)

    return _call_riding(
        body,
        exs,
        args=(y_pre, dmix, wg, bg, go),
        grid=(n,),
        in_specs=[
            pl.BlockSpec((tm, w), lambda i: (i, 0)),
            pl.BlockSpec((tm, w), lambda i: (i, 0)),
            pl.BlockSpec((w, w), lambda i: (0, 0)),
            pl.BlockSpec((1, w), lambda i: (0, 0)),
            pl.BlockSpec((1, w), lambda i: (0, 0)),
        ],
        out_specs=[
            pl.BlockSpec((tm, w), lambda i: (i, 0)),
            pl.BlockSpec((w, w), lambda i: (0, 0)),
            pl.BlockSpec((1, w), lambda i: (0, 0)),
            pl.BlockSpec((1, w), lambda i: (0, 0)),
        ],
        out_shape=[
            jax.ShapeDtypeStruct((s, w), F32),
            jax.ShapeDtypeStruct((w, w), BF16),
            jax.ShapeDtypeStruct((1, w), F32),
            jax.ShapeDtypeStruct((1, w), F32),
        ],
        scratch_shapes=[pltpu.VMEM((w, w), F32)],
        compiler_params=_params(1),
        name="glu_bwd",
    )


def _ssm_bwd_state(dy, v2, al, xprev):
    s = dy.shape[0]
    nt, nc, ns = xprev.shape
    hs = ns // 2

    def body(dy_ref, v_ref, al_ref, xp_ref, dys_ref, g_ref, dal_ref, dxp, gs):
        for q in range(PAIRS):
            dys_ref[q, :, 0:LANES] = dy_ref[pl.ds(2 * q, nc, stride=CHUNK), :].astype(BF16)
            dys_ref[q, :, LANES : 2 * LANES] = dy_ref[pl.ds(2 * q + 1, nc, stride=CHUNK), :].astype(BF16)
        acc = _dot(dys_ref[0], v_ref[0])
        for r in range(1, PAIRS):
            acc = acc + _dot(dys_ref[r], v_ref[r])
        dxp[...] = acc
        ar = al_ref[0:1, :]
        ai = al_ref[1:2, :]
        zero = jnp.zeros((1, hs), F32)
        gs[pl.ds(nc - 1, 1), 0:hs] = zero
        gs[pl.ds(nc - 1, 1), hs:ns] = zero

        def step(n, carry):
            gr, gi = carry
            c = nc - 2 - n
            nr = dxp[pl.ds(c + 1, 1), 0:hs] + ar * gr + ai * gi
            ni = dxp[pl.ds(c + 1, 1), hs:ns] + ar * gi - ai * gr
            gs[pl.ds(c, 1), 0:hs] = nr
            gs[pl.ds(c, 1), hs:ns] = ni
            return nr, ni

        lax.fori_loop(0, nc - 1, step, (zero, zero))
        gv = gs[...]
        xv = xp_ref[...]
        gr, gi = gv[:, 0:hs], gv[:, hs:ns]
        xr, xi = xv[:, 0:hs], xv[:, hs:ns]
        dal_ref[0:1, :] = jnp.sum(gr * xr + gi * xi, axis=0, keepdims=True)
        dal_ref[1:2, :] = jnp.sum(gi * xr - gr * xi, axis=0, keepdims=True)
        g_ref[...] = gv.astype(BF16)

    return _call(
        body,
        grid=(nt,),
        in_specs=[
            pl.BlockSpec((s, LANES), lambda t: (0, t)),
            pl.BlockSpec((None, PAIRS, 2 * LANES, ns), lambda t: (t, 0, 0, 0)),
            pl.BlockSpec((None, 2, hs), lambda t: (t, 0, 0)),
            pl.BlockSpec((None, nc, ns), lambda t: (t, 0, 0)),
        ],
        out_specs=[
            pl.BlockSpec((None, PAIRS, nc, 2 * LANES), lambda t: (t, 0, 0, 0)),
            pl.BlockSpec((None, nc, ns), lambda t: (t, 0, 0)),
            pl.BlockSpec((None, 2, hs), lambda t: (t, 0, 0)),
        ],
        out_shape=[
            jax.ShapeDtypeStruct((nt, PAIRS, nc, 2 * LANES), BF16),
            jax.ShapeDtypeStruct((nt, nc, ns), BF16),
            jax.ShapeDtypeStruct((nt, 2, hs), F32),
        ],
        scratch_shapes=[pltpu.VMEM((nc, ns), F32), pltpu.VMEM((nc, ns), F32)],
        compiler_params=_params(1),
        name="ssm_bwd_state",
    )(dy, v2, al, xprev)


def _ssm_bwd_main(us, dys, k2, w2, xprev, gst, consts, s, width, exs):
    nt, _, nc, _ = us.shape
    ns = xprev.shape[-1]
    rep_k, mask_k, rep_s, mask_s = consts
    nk = 2 * LANES
    gg = GROUPS_PER_TILE
    rows_of = [(gi, st, slice((st * gg + gi) * SSM_GROUP, (st * gg + gi + 1) * SSM_GROUP))
               for st in range(2) for gi in range(gg)]

    def body(us_ref, dys_ref, k_ref, w_ref, xp_ref, g_ref, rk_ref, mk_ref, rs_ref, ms_ref,
             du_ref, dk_ref, dw_ref, dv_ref, acc, du2, duf, dkf):
        q = pl.program_id(1)

        @pl.when(q == 0)
        def _():
            dkf[...] = jnp.zeros_like(dkf)

        usq = us_ref[q]
        gb = g_ref[...]
        dw = _gather_blocks(_dot_tn(usq, gb), ms_ref[...], rs_ref[...])
        dv = _gather_blocks(_dot_tn(dys_ref[q], xp_ref[...].astype(BF16)), ms_ref[...], rs_ref[...])
        for gi, st, rows in rows_of:
            dw_ref[gi, st] = dw[rows]
            dv_ref[gi, st] = dv[rows]
        acc[...] = _dot_nt(gb, w_ref[...])
        for m in range(PAIRS):

            @pl.when(q + m < PAIRS)
            def _():
                dyr = dys_ref[jnp.minimum(q + m, PAIRS - 1)]
                acc[...] += _dot_nt(dyr, k_ref[m])
                dkf[m] += _dot_tn(usq, dyr)

        du2[q] = acc[...]

        @pl.when(q == PAIRS - 1)
        def _():
            for p in range(PAIRS):
                duf[pl.ds(2 * p, nc, stride=CHUNK), :] = du2[p, :, 0:LANES]
                duf[pl.ds(2 * p + 1, nc, stride=CHUNK), :] = du2[p, :, LANES : 2 * LANES]
                dk = _gather_blocks(dkf[p], mk_ref[...], rk_ref[...])
                for gi, st, rows in rows_of:
                    dk_ref[gi, p, st] = dk[rows]
            du_ref[...] = duf[...].astype(BF16)

    def per_tile(rows, cols):
        return pl.BlockSpec((None, PAIRS, rows, cols), lambda t, q: (t, 0, 0, 0))

    def per_pair(rows, cols):
        return pl.BlockSpec((None, None, rows, cols), lambda t, q: (t, q, 0, 0))

    def whole(a):
        return pl.BlockSpec(a.shape, lambda t, q: (0, 0))

    return _call_riding(
        body,
        exs,
        args=(us, dys, k2, w2, xprev, gst, rep_k, mask_k, rep_s, mask_s),
        grid=(nt, PAIRS),
        in_specs=[
            per_tile(nc, nk),
            per_tile(nc, nk),
            per_tile(nk, nk),
            per_pair(nk, ns),
            pl.BlockSpec((None, nc, ns), lambda t, q: (t, 0, 0)),
            pl.BlockSpec((None, nc, ns), lambda t, q: (t, 0, 0)),
            whole(rep_k),
            whole(mask_k),
            whole(rep_s),
            whole(mask_s),
        ],
        out_specs=[
            pl.BlockSpec((s, LANES), lambda t, q: (0, t)),
            pl.BlockSpec((gg, PAIRS, 2, SSM_GROUP, LANES), lambda t, q: (t, 0, 0, 0, 0)),
            pl.BlockSpec((gg, None, 2, SSM_GROUP, LANES), lambda t, q: (t, q, 0, 0, 0)),
            pl.BlockSpec((gg, None, 2, SSM_GROUP, LANES), lambda t, q: (t, q, 0, 0, 0)),
        ],
        out_shape=[
            jax.ShapeDtypeStruct((s, width), BF16),
            jax.ShapeDtypeStruct((nt * gg, PAIRS, 2, SSM_GROUP, LANES), F32),
            jax.ShapeDtypeStruct((nt * gg, PAIRS, 2, SSM_GROUP, LANES), F32),
            jax.ShapeDtypeStruct((nt * gg, PAIRS, 2, SSM_GROUP, LANES), F32),
        ],
        scratch_shapes=[
            pltpu.VMEM((nc, nk), F32),
            pltpu.VMEM((PAIRS, nc, nk), F32),
            pltpu.VMEM((s, LANES), F32),
            pltpu.VMEM((PAIRS, nk, nk), F32),
        ],
        compiler_params=_params(2),
        name="ssm_bwd_main",
    )


def _sgu_bwd(z, dmix, dz, lng, lnb, wt, wtt, biasb, go, rb, exs):
    s = z.shape[0]
    w = lng.shape[-1]
    heads = wt.shape[0]
    ncol = (LANES, LANES)

    def body(zu_ref, zv_ref, dm_ref, dz_any, lng_ref, lnb_ref, wt_ref, wtt_ref, b_ref, go_ref,
             dz_ref, dw_ref, db_ref, dlg_ref, dlb_ref, dgo_ref, dzv):
        del dz_any
        i = pl.program_id(0)
        p = pl.program_id(1)

        @pl.when(jnp.logical_and(i == 0, p == 0))
        def _():
            dw_ref[...] = jnp.zeros_like(dw_ref)
            db_ref[...] = jnp.zeros_like(db_ref)
            dlg_ref[...] = jnp.zeros_like(dlg_ref)
            dlb_ref[...] = jnp.zeros_like(dlb_ref)
            dgo_ref[...] = jnp.zeros_like(dgo_ref)

        @pl.when(p == 0)
        def _():
            lng_v = lng_ref[...]
            for ck in range(rb // SGU_CHUNK):
                rows = slice(ck * SGU_CHUNK, (ck + 1) * SGU_CHUNK)
                zu = zu_ref[rows, :]
                zv = zv_ref[rows, :]
                u, vhat, rstd, vb, mix, u_grad, v_grad = _sgu_parts(zu, zv, lng_v, lnb_ref[...], wt_ref, b_ref)
                y = u * mix
                dy, dgt = _rms_bwd(dm_ref[rows, :], y, _rms(y), go_ref[...])
                dgo_ref[...] += jnp.sum(dgt, axis=0, keepdims=True)
                du = dy * mix
                dmx = dy * u
                dmb = dmx.astype(BF16)
                dvl = []
                for h in range(heads):
                    cols = slice(h * LANES, (h + 1) * LANES)
                    db_ref[h] += jnp.broadcast_to(jnp.sum(dmx[:, cols], axis=-1, keepdims=True), ncol)
                    dw_ref[h] += _dot_nt(dmb[:, cols], vb[:, cols])
                    dvl.append(_dot(wtt_ref[h], dmb[:, cols]))
                dvln = jnp.concatenate(dvl, axis=1)
                dlg_ref[...] += jnp.sum(dvln * vhat, axis=0, keepdims=True)
                dlb_ref[...] += jnp.sum(dvln, axis=0, keepdims=True)
                dvh = dvln * lng_v
                dv = rstd * (
                    dvh
                    - jnp.mean(dvh, axis=-1, keepdims=True)
                    - vhat * jnp.mean(dvh * vhat, axis=-1, keepdims=True)
                )
                dz_ref[rows, :] = (du * u_grad).astype(BF16)
                dzv[rows, :] = (dv * v_grad).astype(BF16)

        @pl.when(p == 1)
        def _():
            dz_ref[...] = dzv[...]

    return _call_riding(
        body,
        exs,
        args=(z, z, dmix, dz, lng, lnb, wt, wtt, biasb, go),
        grid=(s // rb, 2),
        in_specs=[
            pl.BlockSpec((rb, w), lambda i, p: (i, 1)),
            pl.BlockSpec((rb, w), lambda i, p: (i, 2)),
            pl.BlockSpec((rb, w), lambda i, p: (i, 1)),
            pl.BlockSpec(memory_space=pl.ANY),
            pl.BlockSpec((1, w), lambda i, p: (0, 0)),
            pl.BlockSpec((1, w), lambda i, p: (0, 0)),
            pl.BlockSpec((heads, SGU_CHUNK, SGU_CHUNK), lambda i, p: (0, 0, 0)),
            pl.BlockSpec((heads, SGU_CHUNK, SGU_CHUNK), lambda i, p: (0, 0, 0)),
            pl.BlockSpec((heads, SGU_CHUNK, LANES), lambda i, p: (0, 0, 0)),
            pl.BlockSpec((1, w), lambda i, p: (0, 0)),
        ],
        out_specs=[
            pl.BlockSpec((rb, w), lambda i, p: (i, 1 + p)),
            pl.BlockSpec((heads, SGU_CHUNK, SGU_CHUNK), lambda i, p: (0, 0, 0)),
            pl.BlockSpec((heads, SGU_CHUNK, LANES), lambda i, p: (0, 0, 0)),
            pl.BlockSpec((1, w), lambda i, p: (0, 0)),
            pl.BlockSpec((1, w), lambda i, p: (0, 0)),
            pl.BlockSpec((1, w), lambda i, p: (0, 0)),
        ],
        out_shape=[
            jax.ShapeDtypeStruct(dz.shape, dz.dtype),
            jax.ShapeDtypeStruct((heads, SGU_CHUNK, SGU_CHUNK), F32),
            jax.ShapeDtypeStruct((heads, SGU_CHUNK, LANES), F32),
            jax.ShapeDtypeStruct((1, w), F32),
            jax.ShapeDtypeStruct((1, w), F32),
            jax.ShapeDtypeStruct((1, w), F32),
        ],
        scratch_shapes=[pltpu.VMEM((rb, w), BF16)],
        input_output_aliases={3: 0},
        compiler_params=_params(2),
        name="sgu_bwd",
    )


def _in_proj_bwd(dz, w4, x, dres, g, tm, exs, tiles, into, name):
    s, d = x.shape
    nj, _, cw = w4.shape
    first, stop = tiles

    def body(dz_ref, w_ref, x_ref, dr_ref, g_ref, *rest):
        dx_ref, dg_ref, acc = rest[-3:]
        i = pl.program_id(0)
        j = pl.program_id(1)

        @pl.when(j == 0)
        def _():
            acc[...] = jnp.zeros_like(acc)

        @pl.when(jnp.logical_and(i == 0, j == 0))
        def _():
            dg_ref[...] = jnp.zeros_like(dg_ref)

        acc[...] += _dot_nt(dz_ref[...], w_ref[...])

        @pl.when(j == nj - 1)
        def _():
            xv = x_ref[...]
            dx, dgt = _rms_bwd(acc[...], xv, _rms(xv), g_ref[...])
            dx_ref[...] = dr_ref[...] + dx
            dg_ref[...] += jnp.sum(dgt, axis=0, keepdims=True)

    row = pl.BlockSpec((tm, d), lambda i, j: (i + first, 0))
    vec = pl.BlockSpec((1, d), lambda i, j: (0, 0))
    kept = [] if into is None else [into]
    return _call_riding(
        body,
        exs,
        args=(dz, w4, x, dres, g, *kept),
        scratch_shapes=[pltpu.VMEM((tm, d), F32)],
        grid=(stop - first, nj),
        in_specs=[
            pl.BlockSpec((tm, cw), lambda i, j: (i + first, j)),
            pl.BlockSpec((None, d, cw), lambda i, j: (j, 0, 0)),
            row,
            row,
            vec,
        ]
        + [_ANY] * len(kept),
        out_specs=[row, vec],
        out_shape=[jax.ShapeDtypeStruct((s, d), F32), jax.ShapeDtypeStruct((1, d), F32)],
        input_output_aliases={5: 0} if kept else {},
        compiler_params=_params(2),
        name=name,
    )


_ANY = pl.BlockSpec(memory_space=pl.ANY)


def _position():
    x, y, c = lax.axis_index("x"), lax.axis_index("y"), lax.axis_index("c")
    return x, y, c, [(1 - x, y), (x, 1 - y), (1 - x, 1 - y)]


def _remote(src, dst, send_sems, recv_sems, k, to):
    return pltpu.make_async_remote_copy(
        src_ref=src, dst_ref=dst, send_sem=send_sems.at[k], recv_sem=recv_sems.at[k], device_id=to, device_id_type=MESH
    )


class _Riding:
    def __init__(self, srcs, out_shapes, n_sems, start, finish, kept=()):
        self.srcs, self.out_shapes, self.n_sems, self.start, self.finish = srcs, out_shapes, n_sems, start, finish
        self.kept = kept
        self.relay = None


def _call_riding(body, exs, *, grid, in_specs, out_specs, out_shape, scratch_shapes, args, **kw):
    n_in, n_out, n_scr = len(in_specs), len(out_specs), len(scratch_shapes)
    spec = dict(grid=grid, in_specs=list(in_specs), out_specs=list(out_specs), out_shape=list(out_shape))
    if not exs:
        return _call(body, scratch_shapes=list(scratch_shapes), **spec, **kw)(*args), []
    srcs = [a for ex in exs for a in ex.srcs]
    lands = [a for ex in exs for a in ex.out_shapes]
    xi, xo = len(srcs), len(lands)

    def fused(*refs):
        cin, xin = refs[:n_in], refs[n_in : n_in + xi]
        o = n_in + xi
        cout, xout = refs[o : o + n_out], refs[o + n_out : o + n_out + xo]
        scr = refs[o + n_out + xo :]
        ids = [pl.program_id(a) for a in range(len(grid))]
        first = functools.reduce(jnp.logical_and, [i == 0 for i in ids])
        last = functools.reduce(jnp.logical_and, [i == n - 1 for i, n in zip(ids, grid)])

        step = functools.reduce(lambda acc, a: acc * grid[a] + ids[a], range(1, len(grid)), ids[0])

        def each(half):
            si = so = 0
            for e, ex in enumerate(exs):
                ni, no = len(ex.srcs), len(ex.out_shapes)
                sems = scr[n_scr + 2 * e : n_scr + 2 * e + 2]
                if getattr(ex, half) is not None:
                    getattr(ex, half)(xin[si : si + ni], xout[so : so + no], *sems)
                si, so = si + ni, so + no

        @pl.when(first)
        def _():
            each("start")

        @pl.when(step == (3 * math.prod(grid)) // 4)
        def _():
            each("relay")

        body(*cin, *cout, *scr[:n_scr])

        @pl.when(last)
        def _():
            each("finish")

    sems = [pltpu.SemaphoreType.DMA((ex.n_sems,)) for ex in exs for _ in range(2)]
    aliases = dict(kw.pop("input_output_aliases", {}))
    si = so = 0
    for ex in exs:
        for a, b in ex.kept:
            aliases[n_in + si + a] = n_out + so + b
        si, so = si + len(ex.srcs), so + len(ex.out_shapes)
    kw["input_output_aliases"] = aliases
    spec["in_specs"] += [_ANY] * xi
    spec["out_specs"] += [_ANY] * xo
    spec["out_shape"] += lands
    outs = _call(fused, scratch_shapes=list(scratch_shapes) + sems, **spec, **kw)(*args, *srcs)
    return outs[:n_out], outs[n_out:]


def _gather_one(shard, peers=(0, 1, 2), into=None):
    def own_copies(ins, outs, send, recv):
        x, y, c, chips = _position()
        me = 2 * x + y
        sib = (x, y, 1 - c)
        cps = [] if into is not None else [_remote(ins[0].at[c], outs[0].at[me, c], send, recv, 0, sib)]
        cps += [_remote(ins[0].at[c], outs[0].at[me, c], send, recv, 1 + k, (*chips[k], c)) for k in peers]
        return cps, (c, me, sib, chips)

    def start(ins, outs, send, recv):
        for cp in own_copies(ins, outs, send, recv)[0]:
            cp.start()

    def passed_on(outs, send, recv, c, sib, chips):
        return [_remote(outs[0].at[2 * chips[k][0] + chips[k][1], c], outs[0].at[2 * chips[k][0] + chips[k][1], c],
                        send, recv, 4 + k, sib) for k in peers]

    def relay(ins, outs, send, recv):
        x, y, c, chips = _position()
        for k, cp in zip(peers, passed_on(outs, send, recv, c, (x, y, 1 - c), chips)):
            px, py = chips[k]
            _remote(ins[0].at[c], outs[0].at[2 * px + py, c], send, recv, 1 + k, (px, py, c)).wait_recv()
            cp.start()

    def finish(ins, outs, send, recv):
        cps, (c, me, sib, chips) = own_copies(ins, outs, send, recv)
        if into is None:
            _remote(ins[0].at[1 - c], outs[0].at[me, 1 - c], send, recv, 0, sib).wait_recv()
        for k in peers:
            px, py = chips[k]
            theirs = outs[0].at[2 * px + py, 1 - c]
            _remote(theirs, theirs, send, recv, 4 + k, sib).wait_recv()
        for cp in cps + passed_on(outs, send, recv, c, sib, chips):
            cp.wait_send()

    land = jax.ShapeDtypeStruct((N_CHIPS,) + shard.shape, shard.dtype)
    rider = _Riding([shard] + ([] if into is None else [into]), [land], 7, start, finish, [] if into is None else [(1, 0)])
    rider.relay = relay
    return rider


def _to_owner_one(p, peers=(0, 1, 2), into=None):
    def copies(ins, outs, send, recv):
        x, y, c, chips = _position()
        return [_remote(ins[0].at[2 * chips[k][0] + chips[k][1]], outs[0].at[k], send, recv, k, (*chips[k], c)) for k in peers]

    def start(ins, outs, send, recv):
        for cp in copies(ins, outs, send, recv):
            cp.start()

    def finish(ins, outs, send, recv):
        cps = copies(ins, outs, send, recv)
        for cp in cps:
            cp.wait_recv()
        for cp in cps:
            cp.wait_send()

    land = jax.ShapeDtypeStruct((3,) + p.shape[1:], p.dtype)
    return _Riding([p] + ([] if into is None else [into]), [land], 3, start, finish, [] if into is None else [(1, 0)])


def _to_sibling_one(g):
    def copies(ins, outs, send, recv):
        x, y, c, _ = _position()
        return [_remote(ins[0].at[j, 1 - c], outs[0].at[j], send, recv, j, (x, y, 1 - c)) for j in range(N_CHIPS)]

    def start(ins, outs, send, recv):
        for cp in copies(ins, outs, send, recv):
            cp.start()

    def finish(ins, outs, send, recv):
        cps = copies(ins, outs, send, recv)
        for cp in cps:
            cp.wait_recv()
        for cp in cps:
            cp.wait_send()

    return _Riding([g], [jax.ShapeDtypeStruct((g.shape[0],) + g.shape[2:], g.dtype)], N_CHIPS, start, finish)


def _gather_chips(arrs, name):
    n = len(arrs)

    def body(*refs):
        ins, outs = refs[:n], refs[n : 2 * n]
        send_sems, recv_sems, local_sems = refs[2 * n :]
        x, y, c, chips = _position()
        me = 2 * x + y
        local = [pltpu.make_async_copy(ins[i], outs[i].at[me], local_sems.at[i]) for i in range(n)]
        for cp in local:
            cp.start()
        sends = []
        for k, (px, py) in enumerate(chips):
            for i in range(n):
                sends.append(_remote(ins[i], outs[i].at[me], send_sems, recv_sems, k * n + i, (px, py, c)))
                sends[-1].start()
        for k, (px, py) in enumerate(chips):
            for i in range(n):
                _remote(ins[i], outs[i].at[2 * px + py], send_sems, recv_sems, k * n + i, (px, py, c)).wait_recv()
        for cp in sends:
            cp.wait_send()
        for cp in local:
            cp.wait()

    return _call(
        body,
        in_specs=[_ANY] * n,
        out_specs=[_ANY] * n,
        out_shape=[jax.ShapeDtypeStruct((N_CHIPS,) + a.shape, a.dtype) for a in arrs],
        scratch_shapes=[
            pltpu.SemaphoreType.DMA((3 * n,)),
            pltpu.SemaphoreType.DMA((3 * n,)),
            pltpu.SemaphoreType.DMA((n,)),
        ],
        name=name,
    )(*arrs)


def _chip_exchange(ps, name):
    n = len(ps)

    def body(*refs):
        ins, outs = refs[:n], refs[n : 2 * n]
        send_sems, recv_sems = refs[2 * n :]
        x, y, c, chips = _position()
        sends = []
        for k, (px, py) in enumerate(chips):
            for i in range(n):
                sends.append(_remote(ins[i].at[2 * px + py], outs[i].at[k], send_sems, recv_sems, k * n + i, (px, py, c)))
                sends[-1].start()
        for cp in sends:
            cp.wait_recv()
        for cp in sends:
            cp.wait_send()

    return _call(
        body,
        in_specs=[_ANY] * n,
        out_specs=[_ANY] * n,
        out_shape=[jax.ShapeDtypeStruct((3,) + p.shape[1:], p.dtype) for p in ps],
        scratch_shapes=[pltpu.SemaphoreType.DMA((3 * n,)), pltpu.SemaphoreType.DMA((3 * n,))],
        name=name,
    )(*ps)


def _pair_exchange(gs, name):
    n = len(gs)

    def body(*refs):
        ins, outs = refs[:n], refs[n : 2 * n]
        send_sems, recv_sems = refs[2 * n :]
        x, y, c, _ = _position()
        sib = (x, y, 1 - c)
        cps = []
        for i in range(n):
            for j in range(N_CHIPS):
                cps.append(_remote(ins[i].at[j, 1 - c], outs[i].at[j], send_sems, recv_sems, i * N_CHIPS + j, sib))
                cps[-1].start()
        for cp in cps:
            cp.wait_recv()
        for cp in cps:
            cp.wait_send()

    return _call(
        body,
        in_specs=[_ANY] * n,
        out_specs=[_ANY] * n,
        out_shape=[jax.ShapeDtypeStruct((g.shape[0],) + g.shape[2:], g.dtype) for g in gs],
        scratch_shapes=[pltpu.SemaphoreType.DMA((n * N_CHIPS,)), pltpu.SemaphoreType.DMA((n * N_CHIPS,))],
        name=name,
    )(*gs)


def _place_own(shard, gathered, pos, name):
    _, hr, cols = shard.shape
    tr = _row_tile(hr, 512)

    def body(pos_ref, s_ref, g_any, o_ref):
        del pos_ref, g_any
        o_ref[...] = s_ref[...]

    return _call(
        body,
        grid_spec=pltpu.PrefetchScalarGridSpec(
            num_scalar_prefetch=1,
            grid=(hr // tr,),
            in_specs=[pl.BlockSpec((None, tr, cols), lambda i, p: (p[1], i, 0)), _ANY],
            out_specs=pl.BlockSpec((None, None, tr, cols), lambda i, p: (p[0], p[1], i, 0)),
        ),
        out_shape=jax.ShapeDtypeStruct(gathered.shape, gathered.dtype),
        input_output_aliases={2: 0},
        compiler_params=_params(1),
        name=name,
    )(pos, shard, gathered)


def _pair_share(fs, name):
    n = len(fs)

    def body(*refs):
        ins, outs = refs[:n], refs[n : 2 * n]
        send_sems, recv_sems = refs[2 * n :]
        x, y, c, _ = _position()
        sib = (x, y, 1 - c)
        sends = [_remote(ins[i].at[c], outs[i].at[c], send_sems, recv_sems, i, sib) for i in range(n)]
        for cp in sends:
            cp.start()
        for i in range(n):
            _remote(ins[i].at[1 - c], outs[i].at[1 - c], send_sems, recv_sems, i, sib).wait_recv()
        for cp in sends:
            cp.wait_send()

    return _call(
        body,
        in_specs=[_ANY] * n,
        out_specs=[_ANY] * n,
        out_shape=[jax.ShapeDtypeStruct(f.shape, f.dtype) for f in fs],
        input_output_aliases={i: i for i in range(n)},
        scratch_shapes=[pltpu.SemaphoreType.DMA((n,)), pltpu.SemaphoreType.DMA((n,))],
        name=name,
    )(*fs)


def _pair_sum(g, r, core, name):
    nj, _, hr, cols = g.shape
    tr = _row_tile(hr, 256)

    def body(c_ref, g_ref, r_ref, o_ref):
        del c_ref
        o_ref[...] = (g_ref[...].astype(F32) + r_ref[...].astype(F32)).astype(o_ref.dtype)

    return _call(
        body,
        grid_spec=pltpu.PrefetchScalarGridSpec(
            num_scalar_prefetch=1,
            grid=(nj, hr // tr),
            in_specs=[
                pl.BlockSpec((None, None, tr, cols), lambda j, i, c: (j, c[0], i, 0)),
                pl.BlockSpec((None, tr, cols), lambda j, i, c: (j, i, 0)),
            ],
            out_specs=pl.BlockSpec((None, tr, cols), lambda j, i, c: (j, i, 0)),
        ),
        out_shape=jax.ShapeDtypeStruct((nj, hr, cols), g.dtype),
        compiler_params=_params(2),
        name=name,
    )(core, g, r)


def _chip_sum(p, q, pos, name):
    nq, hr, cols = q.shape
    tr = _row_tile(hr, 256)

    def body(pos_ref, p_ref, *refs):
        del pos_ref
        o_ref = refs[nq]
        tot = p_ref[...].astype(F32)
        for k in range(nq):
            tot = tot + refs[k][...].astype(F32)
        o_ref[...] = tot

    return _call(
        body,
        grid_spec=pltpu.PrefetchScalarGridSpec(
            num_scalar_prefetch=1,
            grid=(hr // tr,),
            in_specs=[pl.BlockSpec((None, tr, cols), lambda i, s: (s[0], i, 0))]
            + [pl.BlockSpec((None, tr, cols), functools.partial(lambda k, i, s: (k, i, 0), k)) for k in range(nq)],
            out_specs=pl.BlockSpec((None, tr, cols), lambda i, s: (s[1], i, 0)),
        ),
        out_shape=jax.ShapeDtypeStruct((2, hr, cols), F32),
        compiler_params=_params(1),
        name=name,
    )(pos, p, *([q] * nq))


def _adamw(w, g, m, v, name):
    rows, cols = w.shape
    tr = _row_tile(rows, max(8, (2**18) // cols), mult=8)
    c1 = 1.0 - ADAM_B1**ADAM_STEP
    c2 = 1.0 - ADAM_B2**ADAM_STEP

    def body(w_ref, g_ref, m_ref, v_ref, gk_ref, d_ref, nm_ref, nv_ref):
        gv = g_ref[...]
        gk_ref[...] = gv
        nm = ADAM_B1 * m_ref[...] + (1.0 - ADAM_B1) * gv
        nv = ADAM_B2 * v_ref[...] + (1.0 - ADAM_B2) * (gv * gv)
        nm_ref[...] = nm
        nv_ref[...] = nv
        d_ref[...] = -ADAM_LR * ((nm / c1) / (jnp.sqrt(nv / c2) + ADAM_EPS) + ADAM_WD * w_ref[...])

    spec = pl.BlockSpec((tr, cols), lambda i: (i, 0))
    sds = jax.ShapeDtypeStruct((rows, cols), F32)
    return _call(
        body,
        grid=(rows // tr,),
        in_specs=[spec] * 4,
        out_specs=[spec] * 4,
        out_shape=[sds] * 4,
        compiler_params=_params(1),
        name=name,
    )(w, g, m, v)


_TILE_ELEMS = 8 * LANES
_FLAT_ROW_MULT = 8 * 2 * N_CHIPS


def _flat_rows(shape):
    n = math.prod(shape)
    return (n + _TILE_ELEMS - 1) // _TILE_ELEMS * 8


def _pack(arrs):
    parts = []
    for a in arrs:
        rows = _flat_rows(a.shape)
        flat = a.reshape(-1).astype(F32)
        flat = jnp.pad(flat, (0, rows * LANES - flat.shape[0]))
        parts.append(flat.reshape(rows, LANES))
    total = sum(p.shape[0] for p in parts)
    pad = -total % _FLAT_ROW_MULT
    if pad:
        parts.append(jnp.zeros((pad, LANES), F32))
    return jnp.concatenate(parts, axis=0)


def _unpack(flat, shapes):
    out, row = [], 0
    for shp in shapes:
        rows = _flat_rows(shp)
        out.append(flat[row : row + rows].reshape(-1)[: math.prod(shp)].reshape(shp))
        row += rows
    return out


def kernel(x, norm_mix_g, w_in, ssm_a_re, ssm_a_im, ssm_b_re, ssm_b_im, ssm_c_re, ssm_c_im, ssm_d, ssm_log_dt, ssm_glu_w, ssm_glu_b, sgu_ln_g, sgu_ln_b, sgu_w, sgu_b, out_norm_ssm_g, out_norm_sgu_g, w_out, norm_mlp_g, w_up, w_down, norm_final_g, loss_target, m_norm_mix_g, m_w_in, m_ssm_a_re, m_ssm_a_im, m_ssm_b_re, m_ssm_b_im, m_ssm_c_re, m_ssm_c_im, m_ssm_d, m_ssm_log_dt, m_ssm_glu_w, m_ssm_glu_b, m_sgu_ln_g, m_sgu_ln_b, m_sgu_w, m_sgu_b, m_out_norm_ssm_g, m_out_norm_sgu_g, m_w_out, m_norm_mlp_g, m_w_up, m_w_down, m_norm_final_g, v_norm_mix_g, v_w_in, v_ssm_a_re, v_ssm_a_im, v_ssm_b_re, v_ssm_b_im, v_ssm_c_re, v_ssm_c_im, v_ssm_d, v_ssm_log_dt, v_ssm_glu_w, v_ssm_glu_b, v_sgu_ln_g, v_sgu_ln_b, v_sgu_w, v_sgu_b, v_out_norm_ssm_g, v_out_norm_sgu_g, v_w_out, v_norm_mlp_g, v_w_up, v_w_down, v_norm_final_g):
    weights = dict(norm_mix_g=norm_mix_g, w_in=w_in, ssm_a_re=ssm_a_re, ssm_a_im=ssm_a_im, ssm_b_re=ssm_b_re, ssm_b_im=ssm_b_im, ssm_c_re=ssm_c_re, ssm_c_im=ssm_c_im, ssm_d=ssm_d, ssm_log_dt=ssm_log_dt, ssm_glu_w=ssm_glu_w, ssm_glu_b=ssm_glu_b, sgu_ln_g=sgu_ln_g, sgu_ln_b=sgu_ln_b, sgu_w=sgu_w, sgu_b=sgu_b, out_norm_ssm_g=out_norm_ssm_g, out_norm_sgu_g=out_norm_sgu_g, w_out=w_out, norm_mlp_g=norm_mlp_g, w_up=w_up, w_down=w_down, norm_final_g=norm_final_g)
    mom_m = dict(norm_mix_g=m_norm_mix_g, w_in=m_w_in, ssm_a_re=m_ssm_a_re, ssm_a_im=m_ssm_a_im, ssm_b_re=m_ssm_b_re, ssm_b_im=m_ssm_b_im, ssm_c_re=m_ssm_c_re, ssm_c_im=m_ssm_c_im, ssm_d=m_ssm_d, ssm_log_dt=m_ssm_log_dt, ssm_glu_w=m_ssm_glu_w, ssm_glu_b=m_ssm_glu_b, sgu_ln_g=m_sgu_ln_g, sgu_ln_b=m_sgu_ln_b, sgu_w=m_sgu_w, sgu_b=m_sgu_b, out_norm_ssm_g=m_out_norm_ssm_g, out_norm_sgu_g=m_out_norm_sgu_g, w_out=m_w_out, norm_mlp_g=m_norm_mlp_g, w_up=m_w_up, w_down=m_w_down, norm_final_g=m_norm_final_g)
    mom_v = dict(norm_mix_g=v_norm_mix_g, w_in=v_w_in, ssm_a_re=v_ssm_a_re, ssm_a_im=v_ssm_a_im, ssm_b_re=v_ssm_b_re, ssm_b_im=v_ssm_b_im, ssm_c_re=v_ssm_c_re, ssm_c_im=v_ssm_c_im, ssm_d=v_ssm_d, ssm_log_dt=v_ssm_log_dt, ssm_glu_w=v_ssm_glu_w, ssm_glu_b=v_ssm_glu_b, sgu_ln_g=v_sgu_ln_g, sgu_ln_b=v_sgu_ln_b, sgu_w=v_sgu_w, sgu_b=v_sgu_b, out_norm_ssm_g=v_out_norm_ssm_g, out_norm_sgu_g=v_out_norm_sgu_g, w_out=v_w_out, norm_mlp_g=v_norm_mlp_g, w_up=v_w_up, w_down=v_w_down, norm_final_g=v_norm_final_g)
    names = list(weights)
    large = ["w_in", "ssm_glu_w", "w_out", "w_up", "w_down"]
    small = [n for n in names if n not in large]

    s, d = x.shape[1], x.shape[2]
    xs = x.reshape(s, d)
    target = loss_target.reshape(s, d)
    width = ssm_glu_w.shape[-1]
    ff = w_down.shape[1] * N_CHIPS
    tm = min(512, s)
    core = lax.axis_index("c").astype(jnp.int32).reshape(1)
    chip = (2 * lax.axis_index("x") + lax.axis_index("y")).astype(jnp.int32).reshape(1)
    pos = jnp.concatenate([chip, core])

    shards = [w[0].astype(BF16).reshape(2, w.shape[1] // 2, w.shape[2]) for w in (w_in, ssm_glu_w, w_out, w_up, w_down)]
    ssm_args = (ssm_a_re[0], ssm_a_im[0], ssm_b_re[0], ssm_b_im[0], ssm_c_re[0], ssm_c_im[0], ssm_d[0], ssm_log_dt[0])
    (k2c, w2c, v2c, al), ssm_vjp = jax.vjp(_ssm_mats, *ssm_args)
    consts = _spread_consts(ssm_b_re.shape[-1], ssm_a_re.shape[-1])
    (k2b, w2b, v2b), (w_in_g,) = _ssm_spread(k2c, w2c, v2c, consts, [_gather_one(shards[0])])
    w_in4 = _place_own(shards[0], w_in_g, pos, "place_w_in").reshape(N_CHIPS, d, w_in.shape[2])
    causal = jnp.tril(jnp.ones((SGU_CHUNK, SGU_CHUNK), dtype=bool))
    wt = jnp.where(causal[None], sgu_w[0], 0.0)
    wtb = wt.astype(BF16)
    wttb = jnp.swapaxes(wt, 1, 2).astype(BF16)
    heads = sgu_w.shape[1]
    biasb = jnp.broadcast_to(sgu_b[0][:, :, None], (heads, SGU_CHUNK, LANES))

    tm_big = min(1024, s)
    (z, h1b), (glu_g, out_g, up_g) = _in_proj(
        xs, norm_mix_g, w_in4, tm_big,
        [_gather_one(shards[1]), _gather_one(shards[2]), _gather_one(shards[3], peers=(0, 1))])
    wg_full = _place_own(shards[1], glu_g, pos, "place_glu_w").reshape(width, width)
    w_out_full = _place_own(shards[2], out_g, pos, "place_w_out").reshape(d, d)
    (y_pre, xprev, us), (up_g,) = _ssm_fwd(z, k2b, w2b, v2b, al, [_gather_one(shards[3], peers=(2,), into=up_g)])
    up4 = _place_own(shards[3], up_g, pos, "place_w_up").reshape(N_CHIPS, d, w_up.shape[2])
    mixed = _glu_fwd(y_pre, wg_full, ssm_glu_b, out_norm_ssm_g, d, tm)
    mixed = _sgu_fwd(z, mixed, sgu_ln_g, sgu_ln_b, wtb, biasb, out_norm_sgu_g, tm)
    x1, h2b = _out_proj(xs, mixed, w_out_full, norm_mlp_g, tm)
    tf = min(1024, up4.shape[-1])
    (up,), (down_g,) = _mlp_up(h2b, up4, tm_big, min(2048, up4.shape[-1]), [_gather_one(shards[4])])
    w_down_full = _place_own(shards[4], down_g, pos, "place_w_down").reshape(ff, d)
    loss_part, dx2, dx2b, d_norm_final = _mlp_down_loss(
        x1, up, w_down_full, target, norm_final_g.reshape(1, d), tm, tf)

    dup, dh2 = _mlp_bwd(dx2b, up, up4, w_down_full, tm_big, min(1024, up4.shape[-1]))
    dx1, dx1b, d_norm_mlp = _norm_bwd(dh2, x1, dx2, norm_mlp_g, min(256, s), "norm_mlp_bwd")
    hr_big = d // 2
    tkk = min(1024, s)
    g_down = _grad_w(
        up, dx2b, (N_CHIPS, 2, hr_big, d),
        lambda i, j: (i // (2 * (hr_big // min(1024, hr_big))), (i // (hr_big // min(1024, hr_big))) % 2,
                      i % (hr_big // min(1024, hr_big)), j),
        min(1024, hr_big), d, tkk, True, "grad_w_down")
    tn_up = min(2048, up4.shape[-1])
    per_up = up4.shape[-1] // tn_up
    g_up, sib_down = _grad_w(
        h2b, dup, (N_CHIPS, 2, hr_big, up4.shape[-1]),
        lambda i, j: (j // per_up, i // (hr_big // min(1024, hr_big)), i % (hr_big // min(1024, hr_big)), j % per_up),
        min(1024, hr_big), tn_up, tkk, False, "grad_w_up", exs=[_to_sibling_one(g_down)])
    pair_down = _pair_sum(g_down, sib_down, core, "pair_sum_w_down")
    dmix = _out_proj_bwd(dx1b, w_out_full, tm)
    hr_out = d // (2 * N_CHIPS)
    g_out, sib_up = _grad_w(
        mixed, dx1b, (1, 1, d, d), lambda i, j: (0, 0, i, j), min(1024, d), d, tkk, False, "grad_w_out",
        exs=[_to_sibling_one(g_up)])
    g_out = g_out.reshape(N_CHIPS, 2, hr_out, d)
    pair_up = _pair_sum(g_up, sib_up, core, "pair_sum_w_up")
    (dy_pre, g_glu, d_glu_b, d_norm_ssm), (sib_out,) = _glu_bwd(
        y_pre, dmix, wg_full, ssm_glu_b, out_norm_ssm_g, tm, [_to_sibling_one(g_out)])
    pair_out = _pair_sum(g_out, sib_out, core, "pair_sum_w_out")
    dys, gst, d_al = _ssm_bwd_state(dy_pre, v2b, al, xprev)
    (dz, d_k2, d_w2, d_v2), (chips_down,) = _ssm_bwd_main(
        us, dys, k2b, w2b, xprev, gst, consts, s, z.shape[1], [_to_owner_one(pair_down)])
    (dz, d_wt, d_bias, d_ln_g, d_ln_b, d_norm_sgu), (chips_up,) = _sgu_bwd(
        z, dmix, dz, sgu_ln_g, sgu_ln_b, wtb, wttb, biasb, out_norm_sgu_g, tm,
        [_to_owner_one(pair_up, peers=(0, 1))])
    cw_in = w_in4.shape[-1]
    g_in, chips_out, chips_up = _grad_w(
        h1b, dz, (N_CHIPS, 2, hr_big, cw_in),
        lambda i, j: (j, i // (hr_big // min(1024, hr_big)), i % (hr_big // min(1024, hr_big)), 0),
        min(1024, hr_big), 2 * cw_in, tkk, False, "grad_w_in",
        exs=[_to_owner_one(pair_out), _to_owner_one(pair_up, peers=(2,), into=chips_up)], chunks=2)
    (sib_in,) = _pair_exchange([g_in], "w_in_to_sibling")
    pair_in = _pair_sum(g_in, sib_in, core, "pair_sum_w_in")
    tm_x = min(512, s // 2)
    n_x = s // tm_x
    (gx_half, dg_a), (chips_in,) = _in_proj_bwd(
        dz, w_in4, xs, dx1, norm_mix_g, tm_x, [_to_owner_one(pair_in)], (0, n_x // 2), None, "in_proj_bwd_a")
    (grad_x, dg_b), _ = _in_proj_bwd(
        dz, w_in4, xs, dx1, norm_mix_g, tm_x, [], (n_x // 2, n_x), gx_half, "in_proj_bwd_b")
    d_norm_mix = dg_a + dg_b

    d_ssm = ssm_vjp((d_k2, d_w2, d_v2, d_al))
    small_grads = dict(
        norm_mix_g=d_norm_mix, ssm_a_re=d_ssm[0], ssm_a_im=d_ssm[1], ssm_b_re=d_ssm[2], ssm_b_im=d_ssm[3],
        ssm_c_re=d_ssm[4], ssm_c_im=d_ssm[5], ssm_d=d_ssm[6], ssm_log_dt=d_ssm[7], ssm_glu_b=d_glu_b,
        sgu_ln_g=d_ln_g, sgu_ln_b=d_ln_b, sgu_w=jnp.where(causal[None], d_wt, 0.0), sgu_b=d_bias[:, :, 0],
        out_norm_ssm_g=d_norm_ssm, out_norm_sgu_g=d_norm_sgu, norm_mlp_g=d_norm_mlp, norm_final_g=d_norm_final)
    flat = _pack([small_grads[n] for n in small])
    hr_small = flat.shape[0] // (2 * N_CHIPS)
    g_small = flat.reshape(N_CHIPS, 2, hr_small, LANES)

    hr_glu = width // (2 * N_CHIPS)
    grads = [g_glu.reshape(N_CHIPS, 2, hr_glu, width), g_small]
    tags = ["glu_w", "small"]
    from_sib = _pair_exchange(grads, "grads_to_sibling")
    pair = [_pair_sum(g, r, core, "pair_sum_" + t) for g, r, t in zip(grads, from_sib, tags)]
    from_chips = _chip_exchange(pair, "grads_to_owner")
    pair = [pair_in, pair[0], pair_out, pair_up, pair_down, pair[1]]
    from_chips = [chips_in, from_chips[0], chips_out, chips_up, chips_down, from_chips[1]]
    tags = ["w_in", "glu_w", "w_out", "w_up", "w_down", "small"]
    halves = [_chip_sum(p, q, pos, "chip_sum_" + t) for p, q, t in zip(pair, from_chips, tags)]
    owned = _pair_share(halves, "grads_to_both_cores")
    (small_all,) = _gather_chips([owned[5]], "gather_small_grads")
    small_flat = small_all.reshape(flat.shape)

    grad_out, delta_out, m_out, v_out = {}, {}, {}, {}
    for n, g in zip(large, owned[:5]):
        shp = weights[n].shape
        g2, dl, nm, nv = _adamw(
            weights[n][0], g.reshape(shp[1], shp[2]), mom_m[n][0], mom_v[n][0], "adamw_" + n)
        grad_out[n], delta_out[n], m_out[n], v_out[n] = g2.reshape(shp), dl.reshape(shp), nm.reshape(shp), nv.reshape(shp)
    shapes = [weights[n].shape for n in small]
    g2, dl, nm, nv = _adamw(
        _pack([weights[n] for n in small]), small_flat, _pack([mom_m[n] for n in small]),
        _pack([mom_v[n] for n in small]), "adamw_small")
    for n, g, a, b, c in zip(small, _unpack(g2, shapes), _unpack(dl, shapes), _unpack(nm, shapes), _unpack(nv, shapes)):
        grad_out[n], delta_out[n], m_out[n], v_out[n] = g, a, b, c

    loss = lax.psum(loss_part[0, 0], ("x", "y", "c"))
    return (loss, grad_x.reshape(x.shape), *[grad_out[n] for n in names], *[delta_out[n] for n in names],
            *[m_out[n] for n in names], *[v_out[n] for n in names])
```

```python
import functools
import math

import numpy as np
import jax
import jax.numpy as jnp
from jax import lax
from jax.experimental import pallas as pl
from jax.experimental.pallas import tpu as pltpu

F32 = jnp.float32
BF16 = jnp.bfloat16
MESH = pl.DeviceIdType.MESH
HIGHEST = lax.Precision.HIGHEST

EPS = 1e-6
ADAM_LR = 0.001
ADAM_B1 = 0.9
ADAM_B2 = 0.999
ADAM_EPS = 1e-08
ADAM_WD = 0.01
ADAM_STEP = 10

N_CHIPS = 4
LANES = 128
SSM_GROUP = 16
SSM_STATE = 64
GROUPS_PER_TILE = LANES // SSM_GROUP
CHUNK = 16
PAIRS = CHUNK // 2
SGU_CHUNK = 128
VMEM_LIMIT = 56 * 2**20


def _params(n_axes, vmem=VMEM_LIMIT):
    return pltpu.CompilerParams(dimension_semantics=("arbitrary",) * n_axes, vmem_limit_bytes=vmem)


def _call(body, **kw):
    return pl.pallas_call(body, **kw)


def _dot(a, b):
    return jnp.dot(a, b, preferred_element_type=F32)


def _dot_nt(a, b):
    return lax.dot_general(a, b, (((1,), (1,)), ((), ())), preferred_element_type=F32)


def _dot_tn(a, b):
    return lax.dot_general(a, b, (((0,), (0,)), ((), ())), preferred_element_type=F32)


_GELU_K = math.sqrt(2.0 / math.pi)
_GELU_C = 0.044715


def _gelu_both(x):
    x2 = x * x
    t = jnp.tanh(x * (_GELU_K + (_GELU_K * _GELU_C) * x2))
    hx = 0.5 * x
    onep = 1.0 + t
    return hx * onep, 0.5 * onep + hx * (1.0 - t * t) * (_GELU_K + (3.0 * _GELU_K * _GELU_C) * x2)


def _gelu(x):
    return _gelu_both(x)[0]


def _sigmoid(x):
    return 1.0 / (1.0 + jnp.exp(-x))


def _rms(x):
    return lax.rsqrt(jnp.mean(x * x, axis=-1, keepdims=True) + EPS)


def _rms_bwd(dy, x, r, g):
    a = dy * g
    dx = r * a - x * (r * r * r) * jnp.mean(a * x, axis=-1, keepdims=True)
    return dx, dy * x * r


def _row_tile(rows, target, mult=16):
    for t in range(min(rows, target), 0, -1):
        if rows % t == 0 and t % mult == 0:
            return t
    return rows


def _ssm_mats(a_re, a_im, b_re, b_im, c_re, c_im, d, log_dt):
    g, p = a_re.shape
    h = b_re.shape[-1]
    nt = g // GROUPS_PER_TILE
    dt = jnp.exp(log_dt)[:, None]
    lr, li = a_re * dt, a_im * dt

    def apow(l, lr, li):
        mag = jnp.exp(lr * l)
        return mag * jnp.cos(li * l), mag * jnp.sin(li * l)

    ar, ai = apow(1.0, lr, li)
    den = a_re * a_re + a_im * a_im
    qr = ((ar - 1.0) * a_re + ai * a_im) / den
    qi = (ai * a_re - (ar - 1.0) * a_im) / den
    bt_re, bt_im = jnp.swapaxes(b_re, 1, 2), jnp.swapaxes(b_im, 1, 2)
    bbr = qr[:, None, :] * bt_re - qi[:, None, :] * bt_im
    bbi = qr[:, None, :] * bt_im + qi[:, None, :] * bt_re
    b_same = jnp.concatenate([bbr, bbi], axis=-1)[:, None]
    b_swap = jnp.concatenate([bbi, bbr], axis=-1)[:, None]
    ls = jnp.arange(CHUNK + 1, dtype=F32)[None, :, None]
    pr, pi = apow(ls, lr[:, None, :], li[:, None, :])
    p_same = jnp.concatenate([pr, pr], axis=-1)[:, :, None, :]
    p_sign = jnp.concatenate([-pi, pi], axis=-1)[:, :, None, :]
    tj = p_same[:, :CHUNK] * b_same + p_sign[:, :CHUNK] * b_swap
    c_conj = jnp.concatenate([c_re, -c_im], axis=-1)
    k = jnp.einsum("goq,gliq->glio", c_conj, tj, precision=HIGHEST)
    k = k.at[:, 0].add(d[:, :, None] * jnp.eye(h, dtype=F32))

    zero = jnp.zeros_like(k[:, :1])
    kz = jnp.concatenate([zero, zero, k], axis=1).reshape(g, PAIRS + 1, 2, h, h)
    even, odd = kz[:, :, 0], kz[:, :, 1]
    fill = jnp.zeros((g, PAIRS, h, LANES - 2 * h), F32)
    row0 = jnp.concatenate([even[:, 1:], odd[:, 1:], fill], axis=-1)
    row1 = jnp.concatenate([odd[:, :-1], even[:, 1:], fill], axis=-1)
    k2c = jnp.stack([row0, row1], axis=2)
    w2c = tj[:, ::-1].reshape(g, PAIRS, 2, h, 2 * p)
    c_cross = jnp.concatenate([-c_im, -c_re], axis=-1)
    p_imag = jnp.concatenate([pi, pi], axis=-1)[:, :, None, :]
    v2c = (c_conj[:, None] * p_same[:, 1:] + c_cross[:, None] * p_imag[:, 1:]).reshape(g, PAIRS, 2, h, 2 * p)
    al = jnp.stack([pr[:, CHUNK].reshape(nt, -1), pi[:, CHUNK].reshape(nt, -1)], axis=1)
    return k2c, w2c, v2c, al


def _spread_consts(h, p):
    gg = GROUPS_PER_TILE
    row_g = (np.arange(2 * gg * h) // h) % gg
    colk = np.arange(2 * gg * h)
    rep_k = np.zeros((LANES, 2 * gg * h), np.float32)
    rep_k[(colk // (gg * h)) * h + colk % h, colk] = 1.0
    mask_k = (row_g[:, None] == ((colk // h) % gg)[None, :]).astype(np.float32)
    cols = np.arange(2 * gg * p)
    rep_s = np.zeros((2 * p, 2 * gg * p), np.float32)
    rep_s[(cols // (gg * p)) * p + cols % p, cols] = 1.0
    mask_s = (row_g[:, None] == ((cols // p) % gg)[None, :]).astype(np.float32)
    return tuple(jnp.asarray(a, BF16) for a in (rep_k, mask_k, rep_s, mask_s))


def _ssm_spread(k2c, w2c, v2c, consts, exs):
    h = k2c.shape[3]
    nt = k2c.shape[0] // GROUPS_PER_TILE
    rep_k, mask_k, rep_s, mask_s = consts
    nk, ns = rep_k.shape[1], rep_s.shape[1]

    def body(k_ref, w_ref, v_ref, rk_ref, mk_ref, rs_ref, ms_ref, ko_ref, wo_ref, vo_ref):
        for q in range(PAIRS):
            for c_ref, o_ref, r_ref, m_ref in ((k_ref, ko_ref, rk_ref, mk_ref), (w_ref, wo_ref, rs_ref, ms_ref),
                                               (v_ref, vo_ref, rs_ref, ms_ref)):
                rows = jnp.concatenate(
                    [c_ref[gi, q, st] for st in range(2) for gi in range(GROUPS_PER_TILE)], axis=0)
                o_ref[q] = (_dot(rows.astype(BF16), r_ref[...]) * m_ref[...]).astype(BF16)

    compact = pl.BlockSpec((GROUPS_PER_TILE, PAIRS, 2, h, LANES), lambda t: (t, 0, 0, 0, 0))

    def tile(cols):
        return pl.BlockSpec((None, PAIRS, nk, cols), lambda t: (t, 0, 0, 0))

    def whole(a):
        return pl.BlockSpec(a.shape, lambda t: (0, 0))

    return _call_riding(
        body,
        exs,
        args=(k2c, w2c, v2c, rep_k, mask_k, rep_s, mask_s),
        scratch_shapes=[],
        grid=(nt,),
        in_specs=[compact, compact, compact, whole(rep_k), whole(mask_k), whole(rep_s), whole(mask_s)],
        out_specs=[tile(nk), tile(ns), tile(ns)],
        out_shape=[
            jax.ShapeDtypeStruct((nt, PAIRS, nk, nk), BF16),
            jax.ShapeDtypeStruct((nt, PAIRS, nk, ns), BF16),
            jax.ShapeDtypeStruct((nt, PAIRS, nk, ns), BF16),
        ],
        compiler_params=_params(1),
        name="ssm_spread",
    )


def _gather_blocks(full, mask, rep):
    return _dot_nt((full * mask).astype(BF16), rep)


def _in_proj(x, g, w4, tm, exs):
    s, d = x.shape
    nj, _, cw = w4.shape

    def body(x_ref, g_ref, w_ref, z_ref, h_ref):
        @pl.when(pl.program_id(1) == 0)
        def _():
            xv = x_ref[...]
            h_ref[...] = (xv * _rms(xv) * g_ref[...]).astype(BF16)

        z_ref[...] = _dot(h_ref[...], w_ref[...])

    return _call_riding(
        body,
        exs,
        scratch_shapes=[],
        args=(x, g, w4),
        grid=(s // tm, nj),
        in_specs=[
            pl.BlockSpec((tm, d), lambda i, j: (i, 0)),
            pl.BlockSpec((1, d), lambda i, j: (0, 0)),
            pl.BlockSpec((None, d, cw), lambda i, j: (j, 0, 0)),
        ],
        out_specs=[pl.BlockSpec((tm, cw), lambda i, j: (i, j)), pl.BlockSpec((tm, d), lambda i, j: (i, 0))],
        out_shape=[jax.ShapeDtypeStruct((s, nj * cw), F32), jax.ShapeDtypeStruct((s, d), BF16)],
        compiler_params=_params(2),
        name="in_proj",
    )


def _ssm_fwd(z, k2, w2, v2, al, exs):
    s = z.shape[0]
    nt = k2.shape[0]
    nc = s // CHUNK
    ns = w2.shape[-1]
    hs = ns // 2

    def body(u_ref, k_ref, w_ref, v_ref, al_ref, y_ref, xp_ref, us_ref, xloc):
        for q in range(PAIRS):
            us_ref[q, :, 0:LANES] = u_ref[pl.ds(2 * q, nc, stride=CHUNK), :].astype(BF16)
            us_ref[q, :, LANES : 2 * LANES] = u_ref[pl.ds(2 * q + 1, nc, stride=CHUNK), :].astype(BF16)
        acc = _dot(us_ref[0], w_ref[0])
        for q in range(1, PAIRS):
            acc = acc + _dot(us_ref[q], w_ref[q])
        xloc[...] = acc
        ar = al_ref[0:1, :]
        ai = al_ref[1:2, :]

        def step(c, carry):
            xr, xi = carry
            xp_ref[pl.ds(c, 1), 0:hs] = xr
            xp_ref[pl.ds(c, 1), hs:ns] = xi
            lr = xloc[pl.ds(c, 1), 0:hs]
            li = xloc[pl.ds(c, 1), hs:ns]
            return ar * xr - ai * xi + lr, ar * xi + ai * xr + li

        zero = jnp.zeros((1, hs), F32)
        lax.fori_loop(0, nc, step, (zero, zero))
        xpb = xp_ref[...].astype(BF16)
        for r in range(PAIRS):
            acc = _dot_nt(xpb, v_ref[r])
            for q in range(r + 1):
                acc = acc + _dot(us_ref[q], k_ref[r - q])
            y_ref[pl.ds(2 * r, nc, stride=CHUNK), :] = acc[:, 0:LANES]
            y_ref[pl.ds(2 * r + 1, nc, stride=CHUNK), :] = acc[:, LANES : 2 * LANES]

    return _call_riding(
        body,
        exs,
        args=(z, k2, w2, v2, al),
        grid=(nt,),
        in_specs=[
            pl.BlockSpec((s, LANES), lambda t: (0, t)),
            pl.BlockSpec((None, PAIRS, 2 * LANES, 2 * LANES), lambda t: (t, 0, 0, 0)),
            pl.BlockSpec((None, PAIRS, 2 * LANES, ns), lambda t: (t, 0, 0, 0)),
            pl.BlockSpec((None, PAIRS, 2 * LANES, ns), lambda t: (t, 0, 0, 0)),
            pl.BlockSpec((None, 2, hs), lambda t: (t, 0, 0)),
        ],
        out_specs=[
            pl.BlockSpec((s, LANES), lambda t: (0, t)),
            pl.BlockSpec((None, nc, ns), lambda t: (t, 0, 0)),
            pl.BlockSpec((None, PAIRS, nc, 2 * LANES), lambda t: (t, 0, 0, 0)),
        ],
        out_shape=[
            jax.ShapeDtypeStruct((s, nt * LANES), F32),
            jax.ShapeDtypeStruct((nt, nc, ns), F32),
            jax.ShapeDtypeStruct((nt, PAIRS, nc, 2 * LANES), BF16),
        ],
        scratch_shapes=[pltpu.VMEM((nc, ns), F32)],
        compiler_params=_params(1),
        name="ssm_fwd",
    )


def _glu_fwd(y_pre, wg, bg, go, d_model, tm):
    s, w = y_pre.shape

    def body(y_ref, wg_ref, bg_ref, go_ref, o_ref):
        yg = _gelu(y_ref[...])
        gate = _sigmoid(_dot(yg.astype(BF16), wg_ref[...]) + bg_ref[...])
        ys = yg * gate
        o_ref[...] = (ys * _rms(ys) * go_ref[...]).astype(BF16)

    return _call(
        body,
        grid=(s // tm,),
        in_specs=[
            pl.BlockSpec((tm, w), lambda i: (i, 0)),
            pl.BlockSpec((w, w), lambda i: (0, 0)),
            pl.BlockSpec((1, w), lambda i: (0, 0)),
            pl.BlockSpec((1, w), lambda i: (0, 0)),
        ],
        out_specs=pl.BlockSpec((tm, w), lambda i: (i, 0)),
        out_shape=jax.ShapeDtypeStruct((s, d_model), BF16),
        compiler_params=_params(1),
        name="glu_fwd",
    )(y_pre, wg, bg, go)


def _sgu_parts(zu, zv, lng, lnb, wt_ref, bias_ref):
    u, u_grad = _gelu_both(zu)
    v, v_grad = _gelu_both(zv)
    mu = jnp.mean(v, axis=-1, keepdims=True)
    vc = v - mu
    rstd = lax.rsqrt(jnp.mean(vc * vc, axis=-1, keepdims=True) + EPS)
    vhat = vc * rstd
    vb = (vhat * lng + lnb).astype(BF16)
    heads = wt_ref.shape[0]
    mix = jnp.concatenate(
        [_dot(wt_ref[h], vb[:, h * LANES : (h + 1) * LANES]) + bias_ref[h] for h in range(heads)], axis=1
    )
    return u, vhat, rstd, vb, mix, u_grad, v_grad


def _sgu_fwd(z, mixed, lng, lnb, wt, biasb, go, rb):
    s = z.shape[0]
    w = lng.shape[-1]
    heads = wt.shape[0]

    def body(zu_ref, zv_ref, m_any, lng_ref, lnb_ref, wt_ref, b_ref, go_ref, o_ref):
        del m_any
        for ck in range(rb // SGU_CHUNK):
            rows = slice(ck * SGU_CHUNK, (ck + 1) * SGU_CHUNK)
            u, _, _, _, mix, _, _ = _sgu_parts(zu_ref[rows, :], zv_ref[rows, :], lng_ref[...], lnb_ref[...], wt_ref, b_ref)
            y = u * mix
            o_ref[rows, :] = (y * _rms(y) * go_ref[...]).astype(BF16)

    return _call(
        body,
        grid=(s // rb,),
        in_specs=[
            pl.BlockSpec((rb, w), lambda i: (i, 1)),
            pl.BlockSpec((rb, w), lambda i: (i, 2)),
            pl.BlockSpec(memory_space=pl.ANY),
            pl.BlockSpec((1, w), lambda i: (0, 0)),
            pl.BlockSpec((1, w), lambda i: (0, 0)),
            pl.BlockSpec((heads, SGU_CHUNK, SGU_CHUNK), lambda i: (0, 0, 0)),
            pl.BlockSpec((heads, SGU_CHUNK, LANES), lambda i: (0, 0, 0)),
            pl.BlockSpec((1, w), lambda i: (0, 0)),
        ],
        out_specs=pl.BlockSpec((rb, w), lambda i: (i, 1)),
        out_shape=jax.ShapeDtypeStruct(mixed.shape, mixed.dtype),
        input_output_aliases={2: 0},
        compiler_params=_params(1),
        name="sgu_fwd",
    )(z, z, mixed, lng, lnb, wt, biasb, go)


def _out_proj(x, mixed, w_out, g, tm):
    s, d = x.shape

    def body(x_ref, m_ref, w_ref, g_ref, o_ref, h_ref):
        x1 = x_ref[...] + _dot(m_ref[...], w_ref[...])
        o_ref[...] = x1
        h_ref[...] = (x1 * _rms(x1) * g_ref[...]).astype(BF16)

    row = pl.BlockSpec((tm, d), lambda i: (i, 0))
    return _call(
        body,
        grid=(s // tm,),
        in_specs=[row, row, pl.BlockSpec((d, d), lambda i: (0, 0)), pl.BlockSpec((1, d), lambda i: (0, 0))],
        out_specs=[row, row],
        out_shape=[jax.ShapeDtypeStruct((s, d), F32), jax.ShapeDtypeStruct((s, d), BF16)],
        compiler_params=_params(1),
        name="out_proj",
    )(x, mixed, w_out, g)


def _mlp_up(h, w_up4, tm, tf, exs):
    s, d = h.shape
    nj, _, cw = w_up4.shape
    per = cw // tf

    def body(h_ref, wu_ref, up_ref):
        up_ref[...] = _dot(h_ref[...], wu_ref[...]).astype(BF16)

    return _call_riding(
        body,
        exs,
        scratch_shapes=[],
        args=(h, w_up4),
        grid=(s // tm, nj * per),
        in_specs=[
            pl.BlockSpec((tm, d), lambda i, f: (i, 0)),
            pl.BlockSpec((None, d, tf), lambda i, f: (f // per, 0, f % per)),
        ],
        out_specs=[pl.BlockSpec((tm, tf), lambda i, f: (i, f))],
        out_shape=[jax.ShapeDtypeStruct((s, nj * cw), BF16)],
        compiler_params=_params(2),
        name="mlp_up",
    )


def _mlp_down_loss(x1, up, w_down, target, g, tm, tf):
    s, d = x1.shape
    ff = w_down.shape[0]
    nf = ff // tf

    def body(x_ref, up_ref, wd_ref, t_ref, g_ref, loss_ref, dx_ref, dxb_ref, dg_ref, acc):
        i = pl.program_id(0)
        f = pl.program_id(1)

        @pl.when(f == 0)
        def _():
            acc[...] = x_ref[...]

        @pl.when(jnp.logical_and(i == 0, f == 0))
        def _():
            loss_ref[...] = jnp.zeros_like(loss_ref)
            dg_ref[...] = jnp.zeros_like(dg_ref)

        a = jnp.maximum(up_ref[...].astype(F32), 0.0)
        acc[...] += _dot((a * a).astype(BF16), wd_ref[...])

        @pl.when(f == nf - 1)
        def _():
            xv = acc[...]
            r = _rms(xv)
            gv = g_ref[...]
            diff = xv * r * gv - t_ref[...]
            loss_ref[...] += 0.5 * jnp.sum(jnp.mean(diff * diff, axis=-1, keepdims=True), axis=0, keepdims=True)
            dx, dgt = _rms_bwd(diff * (1.0 / d), xv, r, gv)
            dx_ref[...] = dx
            dxb_ref[...] = dx.astype(BF16)
            dg_ref[...] += jnp.sum(dgt, axis=0, keepdims=True)

    row = pl.BlockSpec((tm, d), lambda i, f: (i, 0))
    vec = pl.BlockSpec((1, d), lambda i, f: (0, 0))
    return _call(
        body,
        grid=(s // tm, nf),
        in_specs=[
            row,
            pl.BlockSpec((tm, tf), lambda i, f: (i, f)),
            pl.BlockSpec((tf, d), lambda i, f: (f, 0)),
            row,
            vec,
        ],
        out_specs=[pl.BlockSpec((1, 1), lambda i, f: (0, 0)), row, row, vec],
        out_shape=[
            jax.ShapeDtypeStruct((1, 1), F32),
            jax.ShapeDtypeStruct((s, d), F32),
            jax.ShapeDtypeStruct((s, d), BF16),
            jax.ShapeDtypeStruct((1, d), F32),
        ],
        scratch_shapes=[pltpu.VMEM((tm, d), F32)],
        compiler_params=_params(2),
        name="mlp_down_loss",
    )(x1, up, w_down, target, g)


def _mlp_bwd(dx2b, up, w_up4, w_down, tm, tf):
    s, d = dx2b.shape
    ff = w_down.shape[0]
    cw = w_up4.shape[-1]
    per = cw // tf

    def body(dx_ref, up_ref, wu_ref, wd_ref, dup_ref, dh_ref):
        @pl.when(pl.program_id(1) == 0)
        def _():
            dh_ref[...] = jnp.zeros_like(dh_ref)

        dact = _dot_nt(dx_ref[...], wd_ref[...])
        dupb = (dact * (2.0 * jnp.maximum(up_ref[...].astype(F32), 0.0))).astype(BF16)
        dup_ref[...] = dupb
        dh_ref[...] += _dot_nt(dupb, wu_ref[...])

    return _call(
        body,
        grid=(s // tm, ff // tf),
        in_specs=[
            pl.BlockSpec((tm, d), lambda i, f: (i, 0)),
            pl.BlockSpec((tm, tf), lambda i, f: (i, f)),
            pl.BlockSpec((None, d, tf), lambda i, f: (f // per, 0, f % per)),
            pl.BlockSpec((tf, d), lambda i, f: (f, 0)),
        ],
        out_specs=[pl.BlockSpec((tm, tf), lambda i, f: (i, f)), pl.BlockSpec((tm, d), lambda i, f: (i, 0))],
        out_shape=[jax.ShapeDtypeStruct((s, ff), BF16), jax.ShapeDtypeStruct((s, d), F32)],
        compiler_params=_params(2),
        name="mlp_bwd",
    )(dx2b, up, w_up4, w_down)


def _norm_bwd(dh, x, dres, g, tm, name):
    s, d = x.shape

    def body(dh_ref, x_ref, dr_ref, g_ref, dx_ref, dxb_ref, dg_ref):
        @pl.when(pl.program_id(0) == 0)
        def _():
            dg_ref[...] = jnp.zeros_like(dg_ref)

        xv = x_ref[...]
        dx, dgt = _rms_bwd(dh_ref[...], xv, _rms(xv), g_ref[...])
        tot = dr_ref[...] + dx
        dx_ref[...] = tot
        dxb_ref[...] = tot.astype(BF16)
        dg_ref[...] += jnp.sum(dgt, axis=0, keepdims=True)

    row = pl.BlockSpec((tm, d), lambda i: (i, 0))
    vec = pl.BlockSpec((1, d), lambda i: (0, 0))
    return _call(
        body,
        grid=(s // tm,),
        in_specs=[row, row, row, vec],
        out_specs=[row, row, vec],
        out_shape=[
            jax.ShapeDtypeStruct((s, d), F32),
            jax.ShapeDtypeStruct((s, d), BF16),
            jax.ShapeDtypeStruct((1, d), F32),
        ],
        compiler_params=_params(1),
        name=name,
    )(dh, x, dres, g)


def _grad_w(a, b, out_dims, index_map, tm, tn, tk, relu2, name, exs=(), chunks=1):
    t, m = a.shape
    n = b.shape[1]
    nk = t // tk
    cw = tn // chunks

    def body(a_ref, b_ref, o_ref, acc):
        k = pl.program_id(2)

        @pl.when(k == 0)
        def _():
            acc[...] = jnp.zeros_like(acc)

        av = a_ref[...]
        if relu2:
            r = jnp.maximum(av.astype(F32), 0.0)
            av = (r * r).astype(BF16)
        acc[...] += _dot_tn(av, b_ref[...])

        @pl.when(k == nk - 1)
        def _():
            for c in range(chunks):
                o_ref[c] = acc[:, c * cw : (c + 1) * cw].astype(BF16)

    spec = dict(
        grid=(m // tm, n // tn, nk),
        in_specs=[pl.BlockSpec((tk, tm), lambda i, j, k: (k, i)), pl.BlockSpec((tk, tn), lambda i, j, k: (k, j))],
        out_specs=[pl.BlockSpec((chunks, None, tm, cw), lambda i, j, k: index_map(i, j))],
        out_shape=[jax.ShapeDtypeStruct(out_dims, BF16)],
        scratch_shapes=[pltpu.VMEM((tm, tn), F32)],
        compiler_params=_params(3),
        name=name,
    )
    (g,), landed = _call_riding(body, exs, args=(a, b), **spec)
    return (g, *landed) if exs else g


def _out_proj_bwd(dx1b, w_out, tm):
    s, d = dx1b.shape

    def body(dx_ref, w_ref, o_ref):
        o_ref[...] = _dot_nt(dx_ref[...], w_ref[...])

    return _call(
        body,
        grid=(s // tm,),
        in_specs=[pl.BlockSpec((tm, d), lambda i: (i, 0)), pl.BlockSpec((d, d), lambda i: (0, 0))],
        out_specs=pl.BlockSpec((tm, d), lambda i: (i, 0)),
        out_shape=jax.ShapeDtypeStruct((s, d), F32),
        compiler_params=_params(1),
        name="out_proj_bwd",
    )(dx1b, w_out)


def _glu_bwd(y_pre, dmix, wg, bg, go, tm, exs):
    s, w = y_pre.shape
    n = s // tm

    def body(y_ref, dm_ref, wg_ref, bg_ref, go_ref, dy_ref, dwg_ref, dbg_ref, dgo_ref, acc):
        i = pl.program_id(0)

        @pl.when(i == 0)
        def _():
            acc[...] = jnp.zeros_like(acc)
            dbg_ref[...] = jnp.zeros_like(dbg_ref)
            dgo_ref[...] = jnp.zeros_like(dgo_ref)

        yp = y_ref[...]
        yg, yg_grad = _gelu_both(yp)
        ygb = yg.astype(BF16)
        gate = _sigmoid(_dot(ygb, wg_ref[...]) + bg_ref[...])
        ys = yg * gate
        dys, dgt = _rms_bwd(dm_ref[...], ys, _rms(ys), go_ref[...])
        dgo_ref[...] += jnp.sum(dgt, axis=0, keepdims=True)
        dpre = dys * yg * gate * (1.0 - gate)
        dbg_ref[...] += jnp.sum(dpre, axis=0, keepdims=True)
        dpb = dpre.astype(BF16)
        dyg = dys * gate + _dot_nt(dpb, wg_ref[...])
        dy_ref[...] = dyg * yg_grad
        acc[...] += _dot_tn(ygb, dpb)

        @pl.when(i == n - 1)
        def _():
            dwg_ref[...] = acc[...].astype(BF16)

    return _call_riding(
        body,
        exs,
        args=(y_pre, dmix, wg, bg, go),
        grid=(n,),
        in_specs=[
            pl.BlockSpec((tm, w), lambda i: (i, 0)),
            pl.BlockSpec((tm, w), lambda i: (i, 0)),
            pl.BlockSpec((w, w), lambda i: (0, 0)),
            pl.BlockSpec((1, w), lambda i: (0, 0)),
            pl.BlockSpec((1, w), lambda i: (0, 0)),
        ],
        out_specs=[
            pl.BlockSpec((tm, w), lambda i: (i, 0)),
            pl.BlockSpec((w, w), lambda i: (0, 0)),
            pl.BlockSpec((1, w), lambda i: (0, 0)),
            pl.BlockSpec((1, w), lambda i: (0, 0)),
        ],
        out_shape=[
            jax.ShapeDtypeStruct((s, w), F32),
            jax.ShapeDtypeStruct((w, w), BF16),
            jax.ShapeDtypeStruct((1, w), F32),
            jax.ShapeDtypeStruct((1, w), F32),
        ],
        scratch_shapes=[pltpu.VMEM((w, w), F32)],
        compiler_params=_params(1),
        name="glu_bwd",
    )


def _ssm_bwd_state(dy, v2, al, xprev):
    s = dy.shape[0]
    nt, nc, ns = xprev.shape
    hs = ns // 2

    def body(dy_ref, v_ref, al_ref, xp_ref, dys_ref, g_ref, dal_ref, dxp, gs):
        for q in range(PAIRS):
            dys_ref[q, :, 0:LANES] = dy_ref[pl.ds(2 * q, nc, stride=CHUNK), :].astype(BF16)
            dys_ref[q, :, LANES : 2 * LANES] = dy_ref[pl.ds(2 * q + 1, nc, stride=CHUNK), :].astype(BF16)
        acc = _dot(dys_ref[0], v_ref[0])
        for r in range(1, PAIRS):
            acc = acc + _dot(dys_ref[r], v_ref[r])
        dxp[...] = acc
        ar = al_ref[0:1, :]
        ai = al_ref[1:2, :]
        zero = jnp.zeros((1, hs), F32)
        gs[pl.ds(nc - 1, 1), 0:hs] = zero
        gs[pl.ds(nc - 1, 1), hs:ns] = zero

        def step(n, carry):
            gr, gi = carry
            c = nc - 2 - n
            nr = dxp[pl.ds(c + 1, 1), 0:hs] + ar * gr + ai * gi
            ni = dxp[pl.ds(c + 1, 1), hs:ns] + ar * gi - ai * gr
            gs[pl.ds(c, 1), 0:hs] = nr
            gs[pl.ds(c, 1), hs:ns] = ni
            return nr, ni

        lax.fori_loop(0, nc - 1, step, (zero, zero))
        gv = gs[...]
        xv = xp_ref[...]
        gr, gi = gv[:, 0:hs], gv[:, hs:ns]
        xr, xi = xv[:, 0:hs], xv[:, hs:ns]
        dal_ref[0:1, :] = jnp.sum(gr * xr + gi * xi, axis=0, keepdims=True)
        dal_ref[1:2, :] = jnp.sum(gi * xr - gr * xi, axis=0, keepdims=True)
        g_ref[...] = gv.astype(BF16)

    return _call(
        body,
        grid=(nt,),
        in_specs=[
            pl.BlockSpec((s, LANES), lambda t: (0, t)),
            pl.BlockSpec((None, PAIRS, 2 * LANES, ns), lambda t: (t, 0, 0, 0)),
            pl.BlockSpec((None, 2, hs), lambda t: (t, 0, 0)),
            pl.BlockSpec((None, nc, ns), lambda t: (t, 0, 0)),
        ],
        out_specs=[
            pl.BlockSpec((None, PAIRS, nc, 2 * LANES), lambda t: (t, 0, 0, 0)),
            pl.BlockSpec((None, nc, ns), lambda t: (t, 0, 0)),
            pl.BlockSpec((None, 2, hs), lambda t: (t, 0, 0)),
        ],
        out_shape=[
            jax.ShapeDtypeStruct((nt, PAIRS, nc, 2 * LANES), BF16),
            jax.ShapeDtypeStruct((nt, nc, ns), BF16),
            jax.ShapeDtypeStruct((nt, 2, hs), F32),
        ],
        scratch_shapes=[pltpu.VMEM((nc, ns), F32), pltpu.VMEM((nc, ns), F32)],
        compiler_params=_params(1),
        name="ssm_bwd_state",
    )(dy, v2, al, xprev)


def _ssm_bwd_main(us, dys, k2, w2, xprev, gst, consts, s, width, exs):
    nt, _, nc, _ = us.shape
    ns = xprev.shape[-1]
    rep_k, mask_k, rep_s, mask_s = consts
    nk = 2 * LANES
    gg = GROUPS_PER_TILE
    rows_of = [(gi, st, slice((st * gg + gi) * SSM_GROUP, (st * gg + gi + 1) * SSM_GROUP))
               for st in range(2) for gi in range(gg)]

    def body(us_ref, dys_ref, k_ref, w_ref, xp_ref, g_ref, rk_ref, mk_ref, rs_ref, ms_ref,
             du_ref, dk_ref, dw_ref, dv_ref, acc, du2, duf, dkf):
        q = pl.program_id(1)

        @pl.when(q == 0)
        def _():
            dkf[...] = jnp.zeros_like(dkf)

        usq = us_ref[q]
        gb = g_ref[...]
        dw = _gather_blocks(_dot_tn(usq, gb), ms_ref[...], rs_ref[...])
        dv = _gather_blocks(_dot_tn(dys_ref[q], xp_ref[...].astype(BF16)), ms_ref[...], rs_ref[...])
        for gi, st, rows in rows_of:
            dw_ref[gi, st] = dw[rows]
            dv_ref[gi, st] = dv[rows]
        acc[...] = _dot_nt(gb, w_ref[...])
        for m in range(PAIRS):

            @pl.when(q + m < PAIRS)
            def _():
                dyr = dys_ref[jnp.minimum(q + m, PAIRS - 1)]
                acc[...] += _dot_nt(dyr, k_ref[m])
                dkf[m] += _dot_tn(usq, dyr)

        du2[q] = acc[...]

        @pl.when(q == PAIRS - 1)
        def _():
            for p in range(PAIRS):
                duf[pl.ds(2 * p, nc, stride=CHUNK), :] = du2[p, :, 0:LANES]
                duf[pl.ds(2 * p + 1, nc, stride=CHUNK), :] = du2[p, :, LANES : 2 * LANES]
                dk = _gather_blocks(dkf[p], mk_ref[...], rk_ref[...])
                for gi, st, rows in rows_of:
                    dk_ref[gi, p, st] = dk[rows]
            du_ref[...] = duf[...].astype(BF16)

    def per_tile(rows, cols):
        return pl.BlockSpec((None, PAIRS, rows, cols), lambda t, q: (t, 0, 0, 0))

    def per_pair(rows, cols):
        return pl.BlockSpec((None, None, rows, cols), lambda t, q: (t, q, 0, 0))

    def whole(a):
        return pl.BlockSpec(a.shape, lambda t, q: (0, 0))

    return _call_riding(
        body,
        exs,
        args=(us, dys, k2, w2, xprev, gst, rep_k, mask_k, rep_s, mask_s),
        grid=(nt, PAIRS),
        in_specs=[
            per_tile(nc, nk),
            per_tile(nc, nk),
            per_tile(nk, nk),
            per_pair(nk, ns),
            pl.BlockSpec((None, nc, ns), lambda t, q: (t, 0, 0)),
            pl.BlockSpec((None, nc, ns), lambda t, q: (t, 0, 0)),
            whole(rep_k),
            whole(mask_k),
            whole(rep_s),
            whole(mask_s),
        ],
        out_specs=[
            pl.BlockSpec((s, LANES), lambda t, q: (0, t)),
            pl.BlockSpec((gg, PAIRS, 2, SSM_GROUP, LANES), lambda t, q: (t, 0, 0, 0, 0)),
            pl.BlockSpec((gg, None, 2, SSM_GROUP, LANES), lambda t, q: (t, q, 0, 0, 0)),
            pl.BlockSpec((gg, None, 2, SSM_GROUP, LANES), lambda t, q: (t, q, 0, 0, 0)),
        ],
        out_shape=[
            jax.ShapeDtypeStruct((s, width), BF16),
            jax.ShapeDtypeStruct((nt * gg, PAIRS, 2, SSM_GROUP, LANES), F32),
            jax.ShapeDtypeStruct((nt * gg, PAIRS, 2, SSM_GROUP, LANES), F32),
            jax.ShapeDtypeStruct((nt * gg, PAIRS, 2, SSM_GROUP, LANES), F32),
        ],
        scratch_shapes=[
            pltpu.VMEM((nc, nk), F32),
            pltpu.VMEM((PAIRS, nc, nk), F32),
            pltpu.VMEM((s, LANES), F32),
            pltpu.VMEM((PAIRS, nk, nk), F32),
        ],
        compiler_params=_params(2),
        name="ssm_bwd_main",
    )


def _sgu_bwd(z, dmix, dz, lng, lnb, wt, wtt, biasb, go, rb, exs):
    s = z.shape[0]
    w = lng.shape[-1]
    heads = wt.shape[0]
    ncol = (LANES, LANES)

    def body(zu_ref, zv_ref, dm_ref, dz_any, lng_ref, lnb_ref, wt_ref, wtt_ref, b_ref, go_ref,
             dz_ref, dw_ref, db_ref, dlg_ref, dlb_ref, dgo_ref, dzv):
        del dz_any
        i = pl.program_id(0)
        p = pl.program_id(1)

        @pl.when(jnp.logical_and(i == 0, p == 0))
        def _():
            dw_ref[...] = jnp.zeros_like(dw_ref)
            db_ref[...] = jnp.zeros_like(db_ref)
            dlg_ref[...] = jnp.zeros_like(dlg_ref)
            dlb_ref[...] = jnp.zeros_like(dlb_ref)
            dgo_ref[...] = jnp.zeros_like(dgo_ref)

        @pl.when(p == 0)
        def _():
            lng_v = lng_ref[...]
            for ck in range(rb // SGU_CHUNK):
                rows = slice(ck * SGU_CHUNK, (ck + 1) * SGU_CHUNK)
                zu = zu_ref[rows, :]
                zv = zv_ref[rows, :]
                u, vhat, rstd, vb, mix, u_grad, v_grad = _sgu_parts(zu, zv, lng_v, lnb_ref[...], wt_ref, b_ref)
                y = u * mix
                dy, dgt = _rms_bwd(dm_ref[rows, :], y, _rms(y), go_ref[...])
                dgo_ref[...] += jnp.sum(dgt, axis=0, keepdims=True)
                du = dy * mix
                dmx = dy * u
                dmb = dmx.astype(BF16)
                dvl = []
                for h in range(heads):
                    cols = slice(h * LANES, (h + 1) * LANES)
                    db_ref[h] += jnp.broadcast_to(jnp.sum(dmx[:, cols], axis=-1, keepdims=True), ncol)
                    dw_ref[h] += _dot_nt(dmb[:, cols], vb[:, cols])
                    dvl.append(_dot(wtt_ref[h], dmb[:, cols]))
                dvln = jnp.concatenate(dvl, axis=1)
                dlg_ref[...] += jnp.sum(dvln * vhat, axis=0, keepdims=True)
                dlb_ref[...] += jnp.sum(dvln, axis=0, keepdims=True)
                dvh = dvln * lng_v
                dv = rstd * (
                    dvh
                    - jnp.mean(dvh, axis=-1, keepdims=True)
                    - vhat * jnp.mean(dvh * vhat, axis=-1, keepdims=True)
                )
                dz_ref[rows, :] = (du * u_grad).astype(BF16)
                dzv[rows, :] = (dv * v_grad).astype(BF16)

        @pl.when(p == 1)
        def _():
            dz_ref[...] = dzv[...]

    return _call_riding(
        body,
        exs,
        args=(z, z, dmix, dz, lng, lnb, wt, wtt, biasb, go),
        grid=(s // rb, 2),
        in_specs=[
            pl.BlockSpec((rb, w), lambda i, p: (i, 1)),
            pl.BlockSpec((rb, w), lambda i, p: (i, 2)),
            pl.BlockSpec((rb, w), lambda i, p: (i, 1)),
            pl.BlockSpec(memory_space=pl.ANY),
            pl.BlockSpec((1, w), lambda i, p: (0, 0)),
            pl.BlockSpec((1, w), lambda i, p: (0, 0)),
            pl.BlockSpec((heads, SGU_CHUNK, SGU_CHUNK), lambda i, p: (0, 0, 0)),
            pl.BlockSpec((heads, SGU_CHUNK, SGU_CHUNK), lambda i, p: (0, 0, 0)),
            pl.BlockSpec((heads, SGU_CHUNK, LANES), lambda i, p: (0, 0, 0)),
            pl.BlockSpec((1, w), lambda i, p: (0, 0)),
        ],
        out_specs=[
            pl.BlockSpec((rb, w), lambda i, p: (i, 1 + p)),
            pl.BlockSpec((heads, SGU_CHUNK, SGU_CHUNK), lambda i, p: (0, 0, 0)),
            pl.BlockSpec((heads, SGU_CHUNK, LANES), lambda i, p: (0, 0, 0)),
            pl.BlockSpec((1, w), lambda i, p: (0, 0)),
            pl.BlockSpec((1, w), lambda i, p: (0, 0)),
            pl.BlockSpec((1, w), lambda i, p: (0, 0)),
        ],
        out_shape=[
            jax.ShapeDtypeStruct(dz.shape, dz.dtype),
            jax.ShapeDtypeStruct((heads, SGU_CHUNK, SGU_CHUNK), F32),
            jax.ShapeDtypeStruct((heads, SGU_CHUNK, LANES), F32),
            jax.ShapeDtypeStruct((1, w), F32),
            jax.ShapeDtypeStruct((1, w), F32),
            jax.ShapeDtypeStruct((1, w), F32),
        ],
        scratch_shapes=[pltpu.VMEM((rb, w), BF16)],
        input_output_aliases={3: 0},
        compiler_params=_params(2),
        name="sgu_bwd",
    )


def _in_proj_bwd(dz, w4, x, dres, g, tm, exs, tiles, into, name):
    s, d = x.shape
    nj, _, cw = w4.shape
    first, stop = tiles

    def body(dz_ref, w_ref, x_ref, dr_ref, g_ref, *rest):
        dx_ref, dg_ref, acc = rest[-3:]
        i = pl.program_id(0)
        j = pl.program_id(1)

        @pl.when(j == 0)
        def _():
            acc[...] = jnp.zeros_like(acc)

        @pl.when(jnp.logical_and(i == 0, j == 0))
        def _():
            dg_ref[...] = jnp.zeros_like(dg_ref)

        acc[...] += _dot_nt(dz_ref[...], w_ref[...])

        @pl.when(j == nj - 1)
        def _():
            xv = x_ref[...]
            dx, dgt = _rms_bwd(acc[...], xv, _rms(xv), g_ref[...])
            dx_ref[...] = dr_ref[...] + dx
            dg_ref[...] += jnp.sum(dgt, axis=0, keepdims=True)

    row = pl.BlockSpec((tm, d), lambda i, j: (i + first, 0))
    vec = pl.BlockSpec((1, d), lambda i, j: (0, 0))
    kept = [] if into is None else [into]
    return _call_riding(
        body,
        exs,
        args=(dz, w4, x, dres, g, *kept),
        scratch_shapes=[pltpu.VMEM((tm, d), F32)],
        grid=(stop - first, nj),
        in_specs=[
            pl.BlockSpec((tm, cw), lambda i, j: (i + first, j)),
            pl.BlockSpec((None, d, cw), lambda i, j: (j, 0, 0)),
            row,
            row,
            vec,
        ]
        + [_ANY] * len(kept),
        out_specs=[row, vec],
        out_shape=[jax.ShapeDtypeStruct((s, d), F32), jax.ShapeDtypeStruct((1, d), F32)],
        input_output_aliases={5: 0} if kept else {},
        compiler_params=_params(2),
        name=name,
    )


_ANY = pl.BlockSpec(memory_space=pl.ANY)


def _position():
    x, y, c = lax.axis_index("x"), lax.axis_index("y"), lax.axis_index("c")
    return x, y, c, [(1 - x, y), (x, 1 - y), (1 - x, 1 - y)]


def _remote(src, dst, send_sems, recv_sems, k, to):
    return pltpu.make_async_remote_copy(
        src_ref=src, dst_ref=dst, send_sem=send_sems.at[k], recv_sem=recv_sems.at[k], device_id=to, device_id_type=MESH
    )


class _Riding:
    def __init__(self, srcs, out_shapes, n_sems, start, finish, kept=()):
        self.srcs, self.out_shapes, self.n_sems, self.start, self.finish = srcs, out_shapes, n_sems, start, finish
        self.kept = kept
        self.relay = None


def _call_riding(body, exs, *, grid, in_specs, out_specs, out_shape, scratch_shapes, args, **kw):
    n_in, n_out, n_scr = len(in_specs), len(out_specs), len(scratch_shapes)
    spec = dict(grid=grid, in_specs=list(in_specs), out_specs=list(out_specs), out_shape=list(out_shape))
    if not exs:
        return _call(body, scratch_shapes=list(scratch_shapes), **spec, **kw)(*args), []
    srcs = [a for ex in exs for a in ex.srcs]
    lands = [a for ex in exs for a in ex.out_shapes]
    xi, xo = len(srcs), len(lands)

    def fused(*refs):
        cin, xin = refs[:n_in], refs[n_in : n_in + xi]
        o = n_in + xi
        cout, xout = refs[o : o + n_out], refs[o + n_out : o + n_out + xo]
        scr = refs[o + n_out + xo :]
        ids = [pl.program_id(a) for a in range(len(grid))]
        first = functools.reduce(jnp.logical_and, [i == 0 for i in ids])
        last = functools.reduce(jnp.logical_and, [i == n - 1 for i, n in zip(ids, grid)])

        step = functools.reduce(lambda acc, a: acc * grid[a] + ids[a], range(1, len(grid)), ids[0])

        def each(half):
            si = so = 0
            for e, ex in enumerate(exs):
                ni, no = len(ex.srcs), len(ex.out_shapes)
                sems = scr[n_scr + 2 * e : n_scr + 2 * e + 2]
                if getattr(ex, half) is not None:
                    getattr(ex, half)(xin[si : si + ni], xout[so : so + no], *sems)
                si, so = si + ni, so + no

        @pl.when(first)
        def _():
            each("start")

        @pl.when(step == (3 * math.prod(grid)) // 4)
        def _():
            each("relay")

        body(*cin, *cout, *scr[:n_scr])

        @pl.when(last)
        def _():
            each("finish")

    sems = [pltpu.SemaphoreType.DMA((ex.n_sems,)) for ex in exs for _ in range(2)]
    aliases = dict(kw.pop("input_output_aliases", {}))
    si = so = 0
    for ex in exs:
        for a, b in ex.kept:
            aliases[n_in + si + a] = n_out + so + b
        si, so = si + len(ex.srcs), so + len(ex.out_shapes)
    kw["input_output_aliases"] = aliases
    spec["in_specs"] += [_ANY] * xi
    spec["out_specs"] += [_ANY] * xo
    spec["out_shape"] += lands
    outs = _call(fused, scratch_shapes=list(scratch_shapes) + sems, **spec, **kw)(*args, *srcs)
    return outs[:n_out], outs[n_out:]


def _gather_one(shard, peers=(0, 1, 2), into=None, early_relay=False):
    def own_copies(ins, outs, send, recv):
        x, y, c, chips = _position()
        me = 2 * x + y
        sib = (x, y, 1 - c)
        cps = [] if into is not None else [_remote(ins[0].at[c], outs[0].at[me, c], send, recv, 0, sib)]
        cps += [_remote(ins[0].at[c], outs[0].at[me, c], send, recv, 1 + k, (*chips[k], c)) for k in peers]
        return cps, (c, me, sib, chips)

    def start(ins, outs, send, recv):
        for cp in own_copies(ins, outs, send, recv)[0]:
            cp.start()

    def passed_on(outs, send, recv, c, sib, chips):
        return [_remote(outs[0].at[2 * chips[k][0] + chips[k][1], c], outs[0].at[2 * chips[k][0] + chips[k][1], c],
                        send, recv, 4 + k, sib) for k in peers]

    def relay(ins, outs, send, recv):
        x, y, c, chips = _position()
        for k, cp in zip(peers, passed_on(outs, send, recv, c, (x, y, 1 - c), chips)):
            px, py = chips[k]
            _remote(ins[0].at[c], outs[0].at[2 * px + py, c], send, recv, 1 + k, (px, py, c)).wait_recv()
            cp.start()

    def finish(ins, outs, send, recv):
        if not early_relay:
            relay(ins, outs, send, recv)
        cps, (c, me, sib, chips) = own_copies(ins, outs, send, recv)
        if into is None:
            _remote(ins[0].at[1 - c], outs[0].at[me, 1 - c], send, recv, 0, sib).wait_recv()
        for k in peers:
            px, py = chips[k]
            theirs = outs[0].at[2 * px + py, 1 - c]
            _remote(theirs, theirs, send, recv, 4 + k, sib).wait_recv()
        for cp in cps + passed_on(outs, send, recv, c, sib, chips):
            cp.wait_send()

    land = jax.ShapeDtypeStruct((N_CHIPS,) + shard.shape, shard.dtype)
    rider = _Riding([shard] + ([] if into is None else [into]), [land], 7, start, finish, [] if into is None else [(1, 0)])
    rider.relay = relay if early_relay else None
    return rider


def _to_owner_one(p, peers=(0, 1, 2), into=None):
    def copies(ins, outs, send, recv):
        x, y, c, chips = _position()
        return [_remote(ins[0].at[2 * chips[k][0] + chips[k][1]], outs[0].at[k], send, recv, k, (*chips[k], c)) for k in peers]

    def start(ins, outs, send, recv):
        for cp in copies(ins, outs, send, recv):
            cp.start()

    def finish(ins, outs, send, recv):
        cps = copies(ins, outs, send, recv)
        for cp in cps:
            cp.wait_recv()
        for cp in cps:
            cp.wait_send()

    land = jax.ShapeDtypeStruct((3,) + p.shape[1:], p.dtype)
    return _Riding([p] + ([] if into is None else [into]), [land], 3, start, finish, [] if into is None else [(1, 0)])


def _to_sibling_one(g):
    def copies(ins, outs, send, recv):
        x, y, c, _ = _position()
        return [_remote(ins[0].at[j, 1 - c], outs[0].at[j], send, recv, j, (x, y, 1 - c)) for j in range(N_CHIPS)]

    def start(ins, outs, send, recv):
        for cp in copies(ins, outs, send, recv):
            cp.start()

    def finish(ins, outs, send, recv):
        cps = copies(ins, outs, send, recv)
        for cp in cps:
            cp.wait_recv()
        for cp in cps:
            cp.wait_send()

    return _Riding([g], [jax.ShapeDtypeStruct((g.shape[0],) + g.shape[2:], g.dtype)], N_CHIPS, start, finish)


def _gather_chips(arrs, name):
    n = len(arrs)

    def body(*refs):
        ins, outs = refs[:n], refs[n : 2 * n]
        send_sems, recv_sems, local_sems = refs[2 * n :]
        x, y, c, chips = _position()
        me = 2 * x + y
        local = [pltpu.make_async_copy(ins[i], outs[i].at[me], local_sems.at[i]) for i in range(n)]
        for cp in local:
            cp.start()
        sends = []
        for k, (px, py) in enumerate(chips):
            for i in range(n):
                sends.append(_remote(ins[i], outs[i].at[me], send_sems, recv_sems, k * n + i, (px, py, c)))
                sends[-1].start()
        for k, (px, py) in enumerate(chips):
            for i in range(n):
                _remote(ins[i], outs[i].at[2 * px + py], send_sems, recv_sems, k * n + i, (px, py, c)).wait_recv()
        for cp in sends:
            cp.wait_send()
        for cp in local:
            cp.wait()

    return _call(
        body,
        in_specs=[_ANY] * n,
        out_specs=[_ANY] * n,
        out_shape=[jax.ShapeDtypeStruct((N_CHIPS,) + a.shape, a.dtype) for a in arrs],
        scratch_shapes=[
            pltpu.SemaphoreType.DMA((3 * n,)),
            pltpu.SemaphoreType.DMA((3 * n,)),
            pltpu.SemaphoreType.DMA((n,)),
        ],
        name=name,
    )(*arrs)


def _chip_exchange(ps, name):
    n = len(ps)

    def body(*refs):
        ins, outs = refs[:n], refs[n : 2 * n]
        send_sems, recv_sems = refs[2 * n :]
        x, y, c, chips = _position()
        sends = []
        for k, (px, py) in enumerate(chips):
            for i in range(n):
                sends.append(_remote(ins[i].at[2 * px + py], outs[i].at[k], send_sems, recv_sems, k * n + i, (px, py, c)))
                sends[-1].start()
        for cp in sends:
            cp.wait_recv()
        for cp in sends:
            cp.wait_send()

    return _call(
        body,
        in_specs=[_ANY] * n,
        out_specs=[_ANY] * n,
        out_shape=[jax.ShapeDtypeStruct((3,) + p.shape[1:], p.dtype) for p in ps],
        scratch_shapes=[pltpu.SemaphoreType.DMA((3 * n,)), pltpu.SemaphoreType.DMA((3 * n,))],
        name=name,
    )(*ps)


def _pair_exchange(gs, name):
    n = len(gs)

    def body(*refs):
        ins, outs = refs[:n], refs[n : 2 * n]
        send_sems, recv_sems = refs[2 * n :]
        x, y, c, _ = _position()
        sib = (x, y, 1 - c)
        cps = []
        for i in range(n):
            for j in range(N_CHIPS):
                cps.append(_remote(ins[i].at[j, 1 - c], outs[i].at[j], send_sems, recv_sems, i * N_CHIPS + j, sib))
                cps[-1].start()
        for cp in cps:
            cp.wait_recv()
        for cp in cps:
            cp.wait_send()

    return _call(
        body,
        in_specs=[_ANY] * n,
        out_specs=[_ANY] * n,
        out_shape=[jax.ShapeDtypeStruct((g.shape[0],) + g.shape[2:], g.dtype) for g in gs],
        scratch_shapes=[pltpu.SemaphoreType.DMA((n * N_CHIPS,)), pltpu.SemaphoreType.DMA((n * N_CHIPS,))],
        name=name,
    )(*gs)


def _place_own(shard, gathered, pos, name):
    _, hr, cols = shard.shape
    tr = _row_tile(hr, 512)

    def body(pos_ref, s_ref, g_any, o_ref):
        del pos_ref, g_any
        o_ref[...] = s_ref[...]

    return _call(
        body,
        grid_spec=pltpu.PrefetchScalarGridSpec(
            num_scalar_prefetch=1,
            grid=(hr // tr,),
            in_specs=[pl.BlockSpec((None, tr, cols), lambda i, p: (p[1], i, 0)), _ANY],
            out_specs=pl.BlockSpec((None, None, tr, cols), lambda i, p: (p[0], p[1], i, 0)),
        ),
        out_shape=jax.ShapeDtypeStruct(gathered.shape, gathered.dtype),
        input_output_aliases={2: 0},
        compiler_params=_params(1),
        name=name,
    )(pos, shard, gathered)


def _pair_share(fs, name):
    n = len(fs)

    def body(*refs):
        ins, outs = refs[:n], refs[n : 2 * n]
        send_sems, recv_sems = refs[2 * n :]
        x, y, c, _ = _position()
        sib = (x, y, 1 - c)
        sends = [_remote(ins[i].at[c], outs[i].at[c], send_sems, recv_sems, i, sib) for i in range(n)]
        for cp in sends:
            cp.start()
        for i in range(n):
            _remote(ins[i].at[1 - c], outs[i].at[1 - c], send_sems, recv_sems, i, sib).wait_recv()
        for cp in sends:
            cp.wait_send()

    return _call(
        body,
        in_specs=[_ANY] * n,
        out_specs=[_ANY] * n,
        out_shape=[jax.ShapeDtypeStruct(f.shape, f.dtype) for f in fs],
        input_output_aliases={i: i for i in range(n)},
        scratch_shapes=[pltpu.SemaphoreType.DMA((n,)), pltpu.SemaphoreType.DMA((n,))],
        name=name,
    )(*fs)


def _pair_sum(g, r, core, name):
    nj, _, hr, cols = g.shape
    tr = _row_tile(hr, 256)

    def body(c_ref, g_ref, r_ref, o_ref):
        del c_ref
        o_ref[...] = (g_ref[...].astype(F32) + r_ref[...].astype(F32)).astype(o_ref.dtype)

    return _call(
        body,
        grid_spec=pltpu.PrefetchScalarGridSpec(
            num_scalar_prefetch=1,
            grid=(nj, hr // tr),
            in_specs=[
                pl.BlockSpec((None, None, tr, cols), lambda j, i, c: (j, c[0], i, 0)),
                pl.BlockSpec((None, tr, cols), lambda j, i, c: (j, i, 0)),
            ],
            out_specs=pl.BlockSpec((None, tr, cols), lambda j, i, c: (j, i, 0)),
        ),
        out_shape=jax.ShapeDtypeStruct((nj, hr, cols), g.dtype),
        compiler_params=_params(2),
        name=name,
    )(core, g, r)


def _chip_sum(p, q, pos, name):
    nq, hr, cols = q.shape
    tr = _row_tile(hr, 256)

    def body(pos_ref, p_ref, *refs):
        del pos_ref
        o_ref = refs[nq]
        tot = p_ref[...].astype(F32)
        for k in range(nq):
            tot = tot + refs[k][...].astype(F32)
        o_ref[...] = tot

    return _call(
        body,
        grid_spec=pltpu.PrefetchScalarGridSpec(
            num_scalar_prefetch=1,
            grid=(hr // tr,),
            in_specs=[pl.BlockSpec((None, tr, cols), lambda i, s: (s[0], i, 0))]
            + [pl.BlockSpec((None, tr, cols), functools.partial(lambda k, i, s: (k, i, 0), k)) for k in range(nq)],
            out_specs=pl.BlockSpec((None, tr, cols), lambda i, s: (s[1], i, 0)),
        ),
        out_shape=jax.ShapeDtypeStruct((2, hr, cols), F32),
        compiler_params=_params(1),
        name=name,
    )(pos, p, *([q] * nq))


def _adamw(w, g, m, v, name):
    rows, cols = w.shape
    tr = _row_tile(rows, max(8, (2**18) // cols), mult=8)
    c1 = 1.0 - ADAM_B1**ADAM_STEP
    c2 = 1.0 - ADAM_B2**ADAM_STEP

    def body(w_ref, g_ref, m_ref, v_ref, gk_ref, d_ref, nm_ref, nv_ref):
        gv = g_ref[...]
        gk_ref[...] = gv
        nm = ADAM_B1 * m_ref[...] + (1.0 - ADAM_B1) * gv
        nv = ADAM_B2 * v_ref[...] + (1.0 - ADAM_B2) * (gv * gv)
        nm_ref[...] = nm
        nv_ref[...] = nv
        d_ref[...] = -ADAM_LR * ((nm / c1) / (jnp.sqrt(nv / c2) + ADAM_EPS) + ADAM_WD * w_ref[...])

    spec = pl.BlockSpec((tr, cols), lambda i: (i, 0))
    sds = jax.ShapeDtypeStruct((rows, cols), F32)
    return _call(
        body,
        grid=(rows // tr,),
        in_specs=[spec] * 4,
        out_specs=[spec] * 4,
        out_shape=[sds] * 4,
        compiler_params=_params(1),
        name=name,
    )(w, g, m, v)


_TILE_ELEMS = 8 * LANES
_FLAT_ROW_MULT = 8 * 2 * N_CHIPS


def _flat_rows(shape):
    n = math.prod(shape)
    return (n + _TILE_ELEMS - 1) // _TILE_ELEMS * 8


def _pack(arrs):
    parts = []
    for a in arrs:
        rows = _flat_rows(a.shape)
        flat = a.reshape(-1).astype(F32)
        flat = jnp.pad(flat, (0, rows * LANES - flat.shape[0]))
        parts.append(flat.reshape(rows, LANES))
    total = sum(p.shape[0] for p in parts)
    pad = -total % _FLAT_ROW_MULT
    if pad:
        parts.append(jnp.zeros((pad, LANES), F32))
    return jnp.concatenate(parts, axis=0)


def _unpack(flat, shapes):
    out, row = [], 0
    for shp in shapes:
        rows = _flat_rows(shp)
        out.append(flat[row : row + rows].reshape(-1)[: math.prod(shp)].reshape(shp))
        row += rows
    return out


def kernel(x, norm_mix_g, w_in, ssm_a_re, ssm_a_im, ssm_b_re, ssm_b_im, ssm_c_re, ssm_c_im, ssm_d, ssm_log_dt, ssm_glu_w, ssm_glu_b, sgu_ln_g, sgu_ln_b, sgu_w, sgu_b, out_norm_ssm_g, out_norm_sgu_g, w_out, norm_mlp_g, w_up, w_down, norm_final_g, loss_target, m_norm_mix_g, m_w_in, m_ssm_a_re, m_ssm_a_im, m_ssm_b_re, m_ssm_b_im, m_ssm_c_re, m_ssm_c_im, m_ssm_d, m_ssm_log_dt, m_ssm_glu_w, m_ssm_glu_b, m_sgu_ln_g, m_sgu_ln_b, m_sgu_w, m_sgu_b, m_out_norm_ssm_g, m_out_norm_sgu_g, m_w_out, m_norm_mlp_g, m_w_up, m_w_down, m_norm_final_g, v_norm_mix_g, v_w_in, v_ssm_a_re, v_ssm_a_im, v_ssm_b_re, v_ssm_b_im, v_ssm_c_re, v_ssm_c_im, v_ssm_d, v_ssm_log_dt, v_ssm_glu_w, v_ssm_glu_b, v_sgu_ln_g, v_sgu_ln_b, v_sgu_w, v_sgu_b, v_out_norm_ssm_g, v_out_norm_sgu_g, v_w_out, v_norm_mlp_g, v_w_up, v_w_down, v_norm_final_g):
    weights = dict(norm_mix_g=norm_mix_g, w_in=w_in, ssm_a_re=ssm_a_re, ssm_a_im=ssm_a_im, ssm_b_re=ssm_b_re, ssm_b_im=ssm_b_im, ssm_c_re=ssm_c_re, ssm_c_im=ssm_c_im, ssm_d=ssm_d, ssm_log_dt=ssm_log_dt, ssm_glu_w=ssm_glu_w, ssm_glu_b=ssm_glu_b, sgu_ln_g=sgu_ln_g, sgu_ln_b=sgu_ln_b, sgu_w=sgu_w, sgu_b=sgu_b, out_norm_ssm_g=out_norm_ssm_g, out_norm_sgu_g=out_norm_sgu_g, w_out=w_out, norm_mlp_g=norm_mlp_g, w_up=w_up, w_down=w_down, norm_final_g=norm_final_g)
    mom_m = dict(norm_mix_g=m_norm_mix_g, w_in=m_w_in, ssm_a_re=m_ssm_a_re, ssm_a_im=m_ssm_a_im, ssm_b_re=m_ssm_b_re, ssm_b_im=m_ssm_b_im, ssm_c_re=m_ssm_c_re, ssm_c_im=m_ssm_c_im, ssm_d=m_ssm_d, ssm_log_dt=m_ssm_log_dt, ssm_glu_w=m_ssm_glu_w, ssm_glu_b=m_ssm_glu_b, sgu_ln_g=m_sgu_ln_g, sgu_ln_b=m_sgu_ln_b, sgu_w=m_sgu_w, sgu_b=m_sgu_b, out_norm_ssm_g=m_out_norm_ssm_g, out_norm_sgu_g=m_out_norm_sgu_g, w_out=m_w_out, norm_mlp_g=m_norm_mlp_g, w_up=m_w_up, w_down=m_w_down, norm_final_g=m_norm_final_g)
    mom_v = dict(norm_mix_g=v_norm_mix_g, w_in=v_w_in, ssm_a_re=v_ssm_a_re, ssm_a_im=v_ssm_a_im, ssm_b_re=v_ssm_b_re, ssm_b_im=v_ssm_b_im, ssm_c_re=v_ssm_c_re, ssm_c_im=v_ssm_c_im, ssm_d=v_ssm_d, ssm_log_dt=v_ssm_log_dt, ssm_glu_w=v_ssm_glu_w, ssm_glu_b=v_ssm_glu_b, sgu_ln_g=v_sgu_ln_g, sgu_ln_b=v_sgu_ln_b, sgu_w=v_sgu_w, sgu_b=v_sgu_b, out_norm_ssm_g=v_out_norm_ssm_g, out_norm_sgu_g=v_out_norm_sgu_g, w_out=v_w_out, norm_mlp_g=v_norm_mlp_g, w_up=v_w_up, w_down=v_w_down, norm_final_g=v_norm_final_g)
    names = list(weights)
    large = ["w_in", "ssm_glu_w", "w_out", "w_up", "w_down"]
    small = [n for n in names if n not in large]

    s, d = x.shape[1], x.shape[2]
    xs = x.reshape(s, d)
    target = loss_target.reshape(s, d)
    width = ssm_glu_w.shape[-1]
    ff = w_down.shape[1] * N_CHIPS
    tm = min(512, s)
    core = lax.axis_index("c").astype(jnp.int32).reshape(1)
    chip = (2 * lax.axis_index("x") + lax.axis_index("y")).astype(jnp.int32).reshape(1)
    pos = jnp.concatenate([chip, core])

    shards = [w[0].astype(BF16).reshape(2, w.shape[1] // 2, w.shape[2]) for w in (w_in, ssm_glu_w, w_out, w_up, w_down)]
    ssm_args = (ssm_a_re[0], ssm_a_im[0], ssm_b_re[0], ssm_b_im[0], ssm_c_re[0], ssm_c_im[0], ssm_d[0], ssm_log_dt[0])
    (k2c, w2c, v2c, al), ssm_vjp = jax.vjp(_ssm_mats, *ssm_args)
    consts = _spread_consts(ssm_b_re.shape[-1], ssm_a_re.shape[-1])
    (k2b, w2b, v2b), (w_in_g,) = _ssm_spread(k2c, w2c, v2c, consts, [_gather_one(shards[0])])
    w_in4 = _place_own(shards[0], w_in_g, pos, "place_w_in").reshape(N_CHIPS, d, w_in.shape[2])
    causal = jnp.tril(jnp.ones((SGU_CHUNK, SGU_CHUNK), dtype=bool))
    wt = jnp.where(causal[None], sgu_w[0], 0.0)
    wtb = wt.astype(BF16)
    wttb = jnp.swapaxes(wt, 1, 2).astype(BF16)
    heads = sgu_w.shape[1]
    biasb = jnp.broadcast_to(sgu_b[0][:, :, None], (heads, SGU_CHUNK, LANES))

    tm_big = min(1024, s)
    (z, h1b), (glu_g, out_g, up_g) = _in_proj(
        xs, norm_mix_g, w_in4, tm_big,
        [_gather_one(shards[1]), _gather_one(shards[2]), _gather_one(shards[3], peers=(0, 1))])
    wg_full = _place_own(shards[1], glu_g, pos, "place_glu_w").reshape(width, width)
    w_out_full = _place_own(shards[2], out_g, pos, "place_w_out").reshape(d, d)
    (y_pre, xprev, us), (up_g,) = _ssm_fwd(z, k2b, w2b, v2b, al, [_gather_one(shards[3], peers=(2,), into=up_g)])
    up4 = _place_own(shards[3], up_g, pos, "place_w_up").reshape(N_CHIPS, d, w_up.shape[2])
    mixed = _glu_fwd(y_pre, wg_full, ssm_glu_b, out_norm_ssm_g, d, tm)
    mixed = _sgu_fwd(z, mixed, sgu_ln_g, sgu_ln_b, wtb, biasb, out_norm_sgu_g, tm)
    x1, h2b = _out_proj(xs, mixed, w_out_full, norm_mlp_g, tm)
    tf = min(1024, up4.shape[-1])
    (up,), (down_g,) = _mlp_up(
        h2b, up4, tm_big, min(2048, up4.shape[-1]), [_gather_one(shards[4], early_relay=True)])
    w_down_full = _place_own(shards[4], down_g, pos, "place_w_down").reshape(ff, d)
    loss_part, dx2, dx2b, d_norm_final = _mlp_down_loss(
        x1, up, w_down_full, target, norm_final_g.reshape(1, d), tm, tf)

    dup, dh2 = _mlp_bwd(dx2b, up, up4, w_down_full, tm_big, min(1024, up4.shape[-1]))
    dx1, dx1b, d_norm_mlp = _norm_bwd(dh2, x1, dx2, norm_mlp_g, min(256, s), "norm_mlp_bwd")
    hr_big = d // 2
    tkk = min(1024, s)
    g_down = _grad_w(
        up, dx2b, (N_CHIPS, 2, hr_big, d),
        lambda i, j: (i // (2 * (hr_big // min(1024, hr_big))), (i // (hr_big // min(1024, hr_big))) % 2,
                      i % (hr_big // min(1024, hr_big)), j),
        min(1024, hr_big), d, tkk, True, "grad_w_down")
    tn_up = min(2048, up4.shape[-1])
    per_up = up4.shape[-1] // tn_up
    g_up, sib_down = _grad_w(
        h2b, dup, (N_CHIPS, 2, hr_big, up4.shape[-1]),
        lambda i, j: (j // per_up, i // (hr_big // min(1024, hr_big)), i % (hr_big // min(1024, hr_big)), j % per_up),
        min(1024, hr_big), tn_up, tkk, False, "grad_w_up", exs=[_to_sibling_one(g_down)])
    pair_down = _pair_sum(g_down, sib_down, core, "pair_sum_w_down")
    dmix = _out_proj_bwd(dx1b, w_out_full, tm)
    hr_out = d // (2 * N_CHIPS)
    g_out, sib_up = _grad_w(
        mixed, dx1b, (1, 1, d, d), lambda i, j: (0, 0, i, j), min(1024, d), d, tkk, False, "grad_w_out",
        exs=[_to_sibling_one(g_up)])
    g_out = g_out.reshape(N_CHIPS, 2, hr_out, d)
    pair_up = _pair_sum(g_up, sib_up, core, "pair_sum_w_up")
    (dy_pre, g_glu, d_glu_b, d_norm_ssm), (sib_out,) = _glu_bwd(
        y_pre, dmix, wg_full, ssm_glu_b, out_norm_ssm_g, tm, [_to_sibling_one(g_out)])
    pair_out = _pair_sum(g_out, sib_out, core, "pair_sum_w_out")
    dys, gst, d_al = _ssm_bwd_state(dy_pre, v2b, al, xprev)
    (dz, d_k2, d_w2, d_v2), (chips_down,) = _ssm_bwd_main(
        us, dys, k2b, w2b, xprev, gst, consts, s, z.shape[1], [_to_owner_one(pair_down)])
    (dz, d_wt, d_bias, d_ln_g, d_ln_b, d_norm_sgu), (chips_up,) = _sgu_bwd(
        z, dmix, dz, sgu_ln_g, sgu_ln_b, wtb, wttb, biasb, out_norm_sgu_g, tm,
        [_to_owner_one(pair_up, peers=(0, 1))])
    cw_in = w_in4.shape[-1]
    g_in, chips_out, chips_up = _grad_w(
        h1b, dz, (N_CHIPS, 2, hr_big, cw_in),
        lambda i, j: (j, i // (hr_big // min(1024, hr_big)), i % (hr_big // min(1024, hr_big)), 0),
        min(1024, hr_big), 2 * cw_in, tkk, False, "grad_w_in",
        exs=[_to_owner_one(pair_out), _to_owner_one(pair_up, peers=(2,), into=chips_up)], chunks=2)
    (sib_in,) = _pair_exchange([g_in], "w_in_to_sibling")
    pair_in = _pair_sum(g_in, sib_in, core, "pair_sum_w_in")
    tm_x = min(512, s // 2)
    n_x = s // tm_x
    (gx_half, dg_a), (chips_in,) = _in_proj_bwd(
        dz, w_in4, xs, dx1, norm_mix_g, tm_x, [_to_owner_one(pair_in)], (0, n_x // 2), None, "in_proj_bwd_a")
    (grad_x, dg_b), _ = _in_proj_bwd(
        dz, w_in4, xs, dx1, norm_mix_g, tm_x, [], (n_x // 2, n_x), gx_half, "in_proj_bwd_b")
    d_norm_mix = dg_a + dg_b

    d_ssm = ssm_vjp((d_k2, d_w2, d_v2, d_al))
    small_grads = dict(
        norm_mix_g=d_norm_mix, ssm_a_re=d_ssm[0], ssm_a_im=d_ssm[1], ssm_b_re=d_ssm[2], ssm_b_im=d_ssm[3],
        ssm_c_re=d_ssm[4], ssm_c_im=d_ssm[5], ssm_d=d_ssm[6], ssm_log_dt=d_ssm[7], ssm_glu_b=d_glu_b,
        sgu_ln_g=d_ln_g, sgu_ln_b=d_ln_b, sgu_w=jnp.where(causal[None], d_wt, 0.0), sgu_b=d_bias[:, :, 0],
        out_norm_ssm_g=d_norm_ssm, out_norm_sgu_g=d_norm_sgu, norm_mlp_g=d_norm_mlp, norm_final_g=d_norm_final)
    flat = _pack([small_grads[n] for n in small])
    hr_small = flat.shape[0] // (2 * N_CHIPS)
    g_small = flat.reshape(N_CHIPS, 2, hr_small, LANES)

    hr_glu = width // (2 * N_CHIPS)
    grads = [g_glu.reshape(N_CHIPS, 2, hr_glu, width), g_small]
    tags = ["glu_w", "small"]
    from_sib = _pair_exchange(grads, "grads_to_sibling")
    pair = [_pair_sum(g, r, core, "pair_sum_" + t) for g, r, t in zip(grads, from_sib, tags)]
    from_chips = _chip_exchange(pair, "grads_to_owner")
    pair = [pair_in, pair[0], pair_out, pair_up, pair_down, pair[1]]
    from_chips = [chips_in, from_chips[0], chips_out, chips_up, chips_down, from_chips[1]]
    tags = ["w_in", "glu_w", "w_out", "w_up", "w_down", "small"]
    halves = [_chip_sum(p, q, pos, "chip_sum_" + t) for p, q, t in zip(pair, from_chips, tags)]
    owned = _pair_share(halves, "grads_to_both_cores")
    (small_all,) = _gather_chips([owned[5]], "gather_small_grads")
    small_flat = small_all.reshape(flat.shape)

    grad_out, delta_out, m_out, v_out = {}, {}, {}, {}
    for n, g in zip(large, owned[:5]):
        shp = weights[n].shape
        g2, dl, nm, nv = _adamw(
            weights[n][0], g.reshape(shp[1], shp[2]), mom_m[n][0], mom_v[n][0], "adamw_" + n)
        grad_out[n], delta_out[n], m_out[n], v_out[n] = g2.reshape(shp), dl.reshape(shp), nm.reshape(shp), nv.reshape(shp)
    shapes = [weights[n].shape for n in small]
    g2, dl, nm, nv = _adamw(
        _pack([weights[n] for n in small]), small_flat, _pack([mom_m[n] for n in small]),
        _pack([mom_v[n] for n in small]), "adamw_small")
    for n, g, a, b, c in zip(small, _unpack(g2, shapes), _unpack(dl, shapes), _unpack(nm, shapes), _unpack(nv, shapes)):
        grad_out[n], delta_out[n], m_out[n], v_out[n] = g, a, b, c

    loss = lax.psum(loss_part[0, 0], ("x", "y", "c"))
    return (loss, grad_x.reshape(x.shape), *[grad_out[n] for n in names], *[delta_out[n] for n in names],
            *[m_out[n] for n in names], *[v_out[n] for n in names])
```

```python
import functools
import math

import numpy as np
import jax
import jax.numpy as jnp
from jax import lax
from jax.experimental import pallas as pl
from jax.experimental.pallas import tpu as pltpu

F32 = jnp.float32
BF16 = jnp.bfloat16
MESH = pl.DeviceIdType.MESH
HIGHEST = lax.Precision.HIGHEST

EPS = 1e-6
ADAM_LR = 0.001
ADAM_B1 = 0.9
ADAM_B2 = 0.999
ADAM_EPS = 1e-08
ADAM_WD = 0.01
ADAM_STEP = 10

N_CHIPS = 4
LANES = 128
SSM_GROUP = 16
SSM_STATE = 64
GROUPS_PER_TILE = LANES // SSM_GROUP
CHUNK = 16
PAIRS = CHUNK // 2
SGU_CHUNK = 128
VMEM_LIMIT = 56 * 2**20


def _params(n_axes, vmem=VMEM_LIMIT):
    return pltpu.CompilerParams(dimension_semantics=("arbitrary",) * n_axes, vmem_limit_bytes=vmem)


def _call(body, **kw):
    return pl.pallas_call(body, **kw)


def _dot(a, b):
    return jnp.dot(a, b, preferred_element_type=F32)


def _dot_nt(a, b):
    return lax.dot_general(a, b, (((1,), (1,)), ((), ())), preferred_element_type=F32)


def _dot_tn(a, b):
    return lax.dot_general(a, b, (((0,), (0,)), ((), ())), preferred_element_type=F32)


_GELU_K = math.sqrt(2.0 / math.pi)
_GELU_C = 0.044715


def _gelu_both(x):
    x2 = x * x
    t = jnp.tanh(x * (_GELU_K + (_GELU_K * _GELU_C) * x2))
    hx = 0.5 * x
    onep = 1.0 + t
    return hx * onep, 0.5 * onep + hx * (1.0 - t * t) * (_GELU_K + (3.0 * _GELU_K * _GELU_C) * x2)


def _gelu(x):
    return _gelu_both(x)[0]


def _sigmoid(x):
    return 1.0 / (1.0 + jnp.exp(-x))


def _rms(x):
    return lax.rsqrt(jnp.mean(x * x, axis=-1, keepdims=True) + EPS)


def _rms_bwd(dy, x, r, g):
    a = dy * g
    dx = r * a - x * (r * r * r) * jnp.mean(a * x, axis=-1, keepdims=True)
    return dx, dy * x * r


def _row_tile(rows, target, mult=16):
    for t in range(min(rows, target), 0, -1):
        if rows % t == 0 and t % mult == 0:
            return t
    return rows


def _ssm_mats(a_re, a_im, b_re, b_im, c_re, c_im, d, log_dt):
    g, p = a_re.shape
    h = b_re.shape[-1]
    nt = g // GROUPS_PER_TILE
    dt = jnp.exp(log_dt)[:, None]
    lr, li = a_re * dt, a_im * dt

    def apow(l, lr, li):
        mag = jnp.exp(lr * l)
        return mag * jnp.cos(li * l), mag * jnp.sin(li * l)

    ar, ai = apow(1.0, lr, li)
    den = a_re * a_re + a_im * a_im
    qr = ((ar - 1.0) * a_re + ai * a_im) / den
    qi = (ai * a_re - (ar - 1.0) * a_im) / den
    bt_re, bt_im = jnp.swapaxes(b_re, 1, 2), jnp.swapaxes(b_im, 1, 2)
    bbr = qr[:, None, :] * bt_re - qi[:, None, :] * bt_im
    bbi = qr[:, None, :] * bt_im + qi[:, None, :] * bt_re
    b_same = jnp.concatenate([bbr, bbi], axis=-1)[:, None]
    b_swap = jnp.concatenate([bbi, bbr], axis=-1)[:, None]
    ls = jnp.arange(CHUNK + 1, dtype=F32)[None, :, None]
    pr, pi = apow(ls, lr[:, None, :], li[:, None, :])
    p_same = jnp.concatenate([pr, pr], axis=-1)[:, :, None, :]
    p_sign = jnp.concatenate([-pi, pi], axis=-1)[:, :, None, :]
    tj = p_same[:, :CHUNK] * b_same + p_sign[:, :CHUNK] * b_swap
    c_conj = jnp.concatenate([c_re, -c_im], axis=-1)
    k = jnp.einsum("goq,gliq->glio", c_conj, tj, precision=HIGHEST)
    k = k.at[:, 0].add(d[:, :, None] * jnp.eye(h, dtype=F32))

    zero = jnp.zeros_like(k[:, :1])
    kz = jnp.concatenate([zero, zero, k], axis=1).reshape(g, PAIRS + 1, 2, h, h)
    even, odd = kz[:, :, 0], kz[:, :, 1]
    fill = jnp.zeros((g, PAIRS, h, LANES - 2 * h), F32)
    row0 = jnp.concatenate([even[:, 1:], odd[:, 1:], fill], axis=-1)
    row1 = jnp.concatenate([odd[:, :-1], even[:, 1:], fill], axis=-1)
    k2c = jnp.stack([row0, row1], axis=2)
    w2c = tj[:, ::-1].reshape(g, PAIRS, 2, h, 2 * p)
    c_cross = jnp.concatenate([-c_im, -c_re], axis=-1)
    p_imag = jnp.concatenate([pi, pi], axis=-1)[:, :, None, :]
    v2c = (c_conj[:, None] * p_same[:, 1:] + c_cross[:, None] * p_imag[:, 1:]).reshape(g, PAIRS, 2, h, 2 * p)
    al = jnp.stack([pr[:, CHUNK].reshape(nt, -1), pi[:, CHUNK].reshape(nt, -1)], axis=1)
    return k2c, w2c, v2c, al


def _spread_consts(h, p):
    gg = GROUPS_PER_TILE
    row_g = (np.arange(2 * gg * h) // h) % gg
    colk = np.arange(2 * gg * h)
    rep_k = np.zeros((LANES, 2 * gg * h), np.float32)
    rep_k[(colk // (gg * h)) * h + colk % h, colk] = 1.0
    mask_k = (row_g[:, None] == ((colk // h) % gg)[None, :]).astype(np.float32)
    cols = np.arange(2 * gg * p)
    rep_s = np.zeros((2 * p, 2 * gg * p), np.float32)
    rep_s[(cols // (gg * p)) * p + cols % p, cols] = 1.0
    mask_s = (row_g[:, None] == ((cols // p) % gg)[None, :]).astype(np.float32)
    return tuple(jnp.asarray(a, BF16) for a in (rep_k, mask_k, rep_s, mask_s))


def _ssm_spread(k2c, w2c, v2c, consts, exs):
    h = k2c.shape[3]
    nt = k2c.shape[0] // GROUPS_PER_TILE
    rep_k, mask_k, rep_s, mask_s = consts
    nk, ns = rep_k.shape[1], rep_s.shape[1]

    def body(k_ref, w_ref, v_ref, rk_ref, mk_ref, rs_ref, ms_ref, ko_ref, wo_ref, vo_ref):
        for q in range(PAIRS):
            for c_ref, o_ref, r_ref, m_ref in ((k_ref, ko_ref, rk_ref, mk_ref), (w_ref, wo_ref, rs_ref, ms_ref),
                                               (v_ref, vo_ref, rs_ref, ms_ref)):
                rows = jnp.concatenate(
                    [c_ref[gi, q, st] for st in range(2) for gi in range(GROUPS_PER_TILE)], axis=0)
                o_ref[q] = (_dot(rows.astype(BF16), r_ref[...]) * m_ref[...]).astype(BF16)

    compact = pl.BlockSpec((GROUPS_PER_TILE, PAIRS, 2, h, LANES), lambda t: (t, 0, 0, 0, 0))

    def tile(cols):
        return pl.BlockSpec((None, PAIRS, nk, cols), lambda t: (t, 0, 0, 0))

    def whole(a):
        return pl.BlockSpec(a.shape, lambda t: (0, 0))

    return _call_riding(
        body,
        exs,
        args=(k2c, w2c, v2c, rep_k, mask_k, rep_s, mask_s),
        scratch_shapes=[],
        grid=(nt,),
        in_specs=[compact, compact, compact, whole(rep_k), whole(mask_k), whole(rep_s), whole(mask_s)],
        out_specs=[tile(nk), tile(ns), tile(ns)],
        out_shape=[
            jax.ShapeDtypeStruct((nt, PAIRS, nk, nk), BF16),
            jax.ShapeDtypeStruct((nt, PAIRS, nk, ns), BF16),
            jax.ShapeDtypeStruct((nt, PAIRS, nk, ns), BF16),
        ],
        compiler_params=_params(1),
        name="ssm_spread",
    )


def _gather_blocks(full, mask, rep):
    return _dot_nt((full * mask).astype(BF16), rep)


def _in_proj(x, g, w4, tm, exs):
    s, d = x.shape
    nj, _, cw = w4.shape

    def body(x_ref, g_ref, w_ref, z_ref, h_ref):
        @pl.when(pl.program_id(1) == 0)
        def _():
            xv = x_ref[...]
            h_ref[...] = (xv * _rms(xv) * g_ref[...]).astype(BF16)

        z_ref[...] = _dot(h_ref[...], w_ref[...])

    return _call_riding(
        body,
        exs,
        scratch_shapes=[],
        args=(x, g, w4),
        grid=(s // tm, nj),
        in_specs=[
            pl.BlockSpec((tm, d), lambda i, j: (i, 0)),
            pl.BlockSpec((1, d), lambda i, j: (0, 0)),
            pl.BlockSpec((None, d, cw), lambda i, j: (j, 0, 0)),
        ],
        out_specs=[pl.BlockSpec((tm, cw), lambda i, j: (i, j)), pl.BlockSpec((tm, d), lambda i, j: (i, 0))],
        out_shape=[jax.ShapeDtypeStruct((s, nj * cw), F32), jax.ShapeDtypeStruct((s, d), BF16)],
        compiler_params=_params(2),
        name="in_proj",
    )


def _ssm_fwd(z, k2, w2, v2, al, exs):
    s = z.shape[0]
    nt = k2.shape[0]
    nc = s // CHUNK
    ns = w2.shape[-1]
    hs = ns // 2

    def body(u_ref, k_ref, w_ref, v_ref, al_ref, y_ref, xp_ref, us_ref, xloc):
        for q in range(PAIRS):
            us_ref[q, :, 0:LANES] = u_ref[pl.ds(2 * q, nc, stride=CHUNK), :].astype(BF16)
            us_ref[q, :, LANES : 2 * LANES] = u_ref[pl.ds(2 * q + 1, nc, stride=CHUNK), :].astype(BF16)
        acc = _dot(us_ref[0], w_ref[0])
        for q in range(1, PAIRS):
            acc = acc + _dot(us_ref[q], w_ref[q])
        xloc[...] = acc
        ar = al_ref[0:1, :]
        ai = al_ref[1:2, :]

        def step(c, carry):
            xr, xi = carry
            xp_ref[pl.ds(c, 1), 0:hs] = xr
            xp_ref[pl.ds(c, 1), hs:ns] = xi
            lr = xloc[pl.ds(c, 1), 0:hs]
            li = xloc[pl.ds(c, 1), hs:ns]
            return ar * xr - ai * xi + lr, ar * xi + ai * xr + li

        zero = jnp.zeros((1, hs), F32)
        lax.fori_loop(0, nc, step, (zero, zero))
        xpb = xp_ref[...].astype(BF16)
        for r in range(PAIRS):
            acc = _dot_nt(xpb, v_ref[r])
            for q in range(r + 1):
                acc = acc + _dot(us_ref[q], k_ref[r - q])
            y_ref[pl.ds(2 * r, nc, stride=CHUNK), :] = acc[:, 0:LANES]
            y_ref[pl.ds(2 * r + 1, nc, stride=CHUNK), :] = acc[:, LANES : 2 * LANES]

    return _call_riding(
        body,
        exs,
        args=(z, k2, w2, v2, al),
        grid=(nt,),
        in_specs=[
            pl.BlockSpec((s, LANES), lambda t: (0, t)),
            pl.BlockSpec((None, PAIRS, 2 * LANES, 2 * LANES), lambda t: (t, 0, 0, 0)),
            pl.BlockSpec((None, PAIRS, 2 * LANES, ns), lambda t: (t, 0, 0, 0)),
            pl.BlockSpec((None, PAIRS, 2 * LANES, ns), lambda t: (t, 0, 0, 0)),
            pl.BlockSpec((None, 2, hs), lambda t: (t, 0, 0)),
        ],
        out_specs=[
            pl.BlockSpec((s, LANES), lambda t: (0, t)),
            pl.BlockSpec((None, nc, ns), lambda t: (t, 0, 0)),
            pl.BlockSpec((None, PAIRS, nc, 2 * LANES), lambda t: (t, 0, 0, 0)),
        ],
        out_shape=[
            jax.ShapeDtypeStruct((s, nt * LANES), F32),
            jax.ShapeDtypeStruct((nt, nc, ns), F32),
            jax.ShapeDtypeStruct((nt, PAIRS, nc, 2 * LANES), BF16),
        ],
        scratch_shapes=[pltpu.VMEM((nc, ns), F32)],
        compiler_params=_params(1),
        name="ssm_fwd",
    )


def _glu_fwd(y_pre, wg, bg, go, d_model, tm):
    s, w = y_pre.shape

    def body(y_ref, wg_ref, bg_ref, go_ref, o_ref):
        yg = _gelu(y_ref[...])
        gate = _sigmoid(_dot(yg.astype(BF16), wg_ref[...]) + bg_ref[...])
        ys = yg * gate
        o_ref[...] = (ys * _rms(ys) * go_ref[...]).astype(BF16)

    return _call(
        body,
        grid=(s // tm,),
        in_specs=[
            pl.BlockSpec((tm, w), lambda i: (i, 0)),
            pl.BlockSpec((w, w), lambda i: (0, 0)),
            pl.BlockSpec((1, w), lambda i: (0, 0)),
            pl.BlockSpec((1, w), lambda i: (0, 0)),
        ],
        out_specs=pl.BlockSpec((tm, w), lambda i: (i, 0)),
        out_shape=jax.ShapeDtypeStruct((s, d_model), BF16),
        compiler_params=_params(1),
        name="glu_fwd",
    )(y_pre, wg, bg, go)


def _sgu_parts(zu, zv, lng, lnb, wt_ref, bias_ref):
    u, u_grad = _gelu_both(zu)
    v, v_grad = _gelu_both(zv)
    mu = jnp.mean(v, axis=-1, keepdims=True)
    vc = v - mu
    rstd = lax.rsqrt(jnp.mean(vc * vc, axis=-1, keepdims=True) + EPS)
    vhat = vc * rstd
    vb = (vhat * lng + lnb).astype(BF16)
    heads = wt_ref.shape[0]
    mix = jnp.concatenate(
        [_dot(wt_ref[h], vb[:, h * LANES : (h + 1) * LANES]) + bias_ref[h] for h in range(heads)], axis=1
    )
    return u, vhat, rstd, vb, mix, u_grad, v_grad


def _sgu_fwd(z, mixed, lng, lnb, wt, biasb, go, rb):
    s = z.shape[0]
    w = lng.shape[-1]
    heads = wt.shape[0]

    def body(zu_ref, zv_ref, m_any, lng_ref, lnb_ref, wt_ref, b_ref, go_ref, o_ref):
        del m_any
        for ck in range(rb // SGU_CHUNK):
            rows = slice(ck * SGU_CHUNK, (ck + 1) * SGU_CHUNK)
            u, _, _, _, mix, _, _ = _sgu_parts(zu_ref[rows, :], zv_ref[rows, :], lng_ref[...], lnb_ref[...], wt_ref, b_ref)
            y = u * mix
            o_ref[rows, :] = (y * _rms(y) * go_ref[...]).astype(BF16)

    return _call(
        body,
        grid=(s // rb,),
        in_specs=[
            pl.BlockSpec((rb, w), lambda i: (i, 1)),
            pl.BlockSpec((rb, w), lambda i: (i, 2)),
            pl.BlockSpec(memory_space=pl.ANY),
            pl.BlockSpec((1, w), lambda i: (0, 0)),
            pl.BlockSpec((1, w), lambda i: (0, 0)),
            pl.BlockSpec((heads, SGU_CHUNK, SGU_CHUNK), lambda i: (0, 0, 0)),
            pl.BlockSpec((heads, SGU_CHUNK, LANES), lambda i: (0, 0, 0)),
            pl.BlockSpec((1, w), lambda i: (0, 0)),
        ],
        out_specs=pl.BlockSpec((rb, w), lambda i: (i, 1)),
        out_shape=jax.ShapeDtypeStruct(mixed.shape, mixed.dtype),
        input_output_aliases={2: 0},
        compiler_params=_params(1),
        name="sgu_fwd",
    )(z, z, mixed, lng, lnb, wt, biasb, go)


def _out_proj(x, mixed, w_out, g, tm):
    s, d = x.shape

    def body(x_ref, m_ref, w_ref, g_ref, o_ref, h_ref):
        x1 = x_ref[...] + _dot(m_ref[...], w_ref[...])
        o_ref[...] = x1
        h_ref[...] = (x1 * _rms(x1) * g_ref[...]).astype(BF16)

    row = pl.BlockSpec((tm, d), lambda i: (i, 0))
    return _call(
        body,
        grid=(s // tm,),
        in_specs=[row, row, pl.BlockSpec((d, d), lambda i: (0, 0)), pl.BlockSpec((1, d), lambda i: (0, 0))],
        out_specs=[row, row],
        out_shape=[jax.ShapeDtypeStruct((s, d), F32), jax.ShapeDtypeStruct((s, d), BF16)],
        compiler_params=_params(1),
        name="out_proj",
    )(x, mixed, w_out, g)


def _mlp_up(h, w_up4, tm, tf, exs):
    s, d = h.shape
    nj, _, cw = w_up4.shape
    per = cw // tf

    def body(h_ref, wu_ref, up_ref):
        up_ref[...] = _dot(h_ref[...], wu_ref[...]).astype(BF16)

    return _call_riding(
        body,
        exs,
        scratch_shapes=[],
        args=(h, w_up4),
        grid=(s // tm, nj * per),
        in_specs=[
            pl.BlockSpec((tm, d), lambda i, f: (i, 0)),
            pl.BlockSpec((None, d, tf), lambda i, f: (f // per, 0, f % per)),
        ],
        out_specs=[pl.BlockSpec((tm, tf), lambda i, f: (i, f))],
        out_shape=[jax.ShapeDtypeStruct((s, nj * cw), BF16)],
        compiler_params=_params(2),
        name="mlp_up",
    )


def _mlp_down_loss(x1, up, w_down, target, g, tm, tf):
    s, d = x1.shape
    ff = w_down.shape[0]
    nf = ff // tf

    def body(x_ref, up_ref, wd_ref, t_ref, g_ref, loss_ref, dx_ref, dxb_ref, dg_ref, acc):
        i = pl.program_id(0)
        f = pl.program_id(1)

        @pl.when(f == 0)
        def _():
            acc[...] = x_ref[...]

        @pl.when(jnp.logical_and(i == 0, f == 0))
        def _():
            loss_ref[...] = jnp.zeros_like(loss_ref)
            dg_ref[...] = jnp.zeros_like(dg_ref)

        a = jnp.maximum(up_ref[...].astype(F32), 0.0)
        acc[...] += _dot((a * a).astype(BF16), wd_ref[...])

        @pl.when(f == nf - 1)
        def _():
            xv = acc[...]
            r = _rms(xv)
            gv = g_ref[...]
            diff = xv * r * gv - t_ref[...]
            loss_ref[...] += 0.5 * jnp.sum(jnp.mean(diff * diff, axis=-1, keepdims=True), axis=0, keepdims=True)
            dx, dgt = _rms_bwd(diff * (1.0 / d), xv, r, gv)
            dx_ref[...] = dx
            dxb_ref[...] = dx.astype(BF16)
            dg_ref[...] += jnp.sum(dgt, axis=0, keepdims=True)

    row = pl.BlockSpec((tm, d), lambda i, f: (i, 0))
    vec = pl.BlockSpec((1, d), lambda i, f: (0, 0))
    return _call(
        body,
        grid=(s // tm, nf),
        in_specs=[
            row,
            pl.BlockSpec((tm, tf), lambda i, f: (i, f)),
            pl.BlockSpec((tf, d), lambda i, f: (f, 0)),
            row,
            vec,
        ],
        out_specs=[pl.BlockSpec((1, 1), lambda i, f: (0, 0)), row, row, vec],
        out_shape=[
            jax.ShapeDtypeStruct((1, 1), F32),
            jax.ShapeDtypeStruct((s, d), F32),
            jax.ShapeDtypeStruct((s, d), BF16),
            jax.ShapeDtypeStruct((1, d), F32),
        ],
        scratch_shapes=[pltpu.VMEM((tm, d), F32)],
        compiler_params=_params(2),
        name="mlp_down_loss",
    )(x1, up, w_down, target, g)


def _mlp_bwd(dx2b, up, w_up4, w_down, tm, tf):
    s, d = dx2b.shape
    ff = w_down.shape[0]
    cw = w_up4.shape[-1]
    per = cw // tf

    def body(dx_ref, up_ref, wu_ref, wd_ref, dup_ref, dh_ref):
        @pl.when(pl.program_id(1) == 0)
        def _():
            dh_ref[...] = jnp.zeros_like(dh_ref)

        dact = _dot_nt(dx_ref[...], wd_ref[...])
        dupb = (dact * (2.0 * jnp.maximum(up_ref[...].astype(F32), 0.0))).astype(BF16)
        dup_ref[...] = dupb
        dh_ref[...] += _dot_nt(dupb, wu_ref[...])

    return _call(
        body,
        grid=(s // tm, ff // tf),
        in_specs=[
            pl.BlockSpec((tm, d), lambda i, f: (i, 0)),
            pl.BlockSpec((tm, tf), lambda i, f: (i, f)),
            pl.BlockSpec((None, d, tf), lambda i, f: (f // per, 0, f % per)),
            pl.BlockSpec((tf, d), lambda i, f: (f, 0)),
        ],
        out_specs=[pl.BlockSpec((tm, tf), lambda i, f: (i, f)), pl.BlockSpec((tm, d), lambda i, f: (i, 0))],
        out_shape=[jax.ShapeDtypeStruct((s, ff), BF16), jax.ShapeDtypeStruct((s, d), F32)],
        compiler_params=_params(2),
        name="mlp_bwd",
    )(dx2b, up, w_up4, w_down)


def _norm_bwd(dh, x, dres, g, tm, name):
    s, d = x.shape

    def body(dh_ref, x_ref, dr_ref, g_ref, dx_ref, dxb_ref, dg_ref):
        @pl.when(pl.program_id(0) == 0)
        def _():
            dg_ref[...] = jnp.zeros_like(dg_ref)

        xv = x_ref[...]
        dx, dgt = _rms_bwd(dh_ref[...], xv, _rms(xv), g_ref[...])
        tot = dr_ref[...] + dx
        dx_ref[...] = tot
        dxb_ref[...] = tot.astype(BF16)
        dg_ref[...] += jnp.sum(dgt, axis=0, keepdims=True)

    row = pl.BlockSpec((tm, d), lambda i: (i, 0))
    vec = pl.BlockSpec((1, d), lambda i: (0, 0))
    return _call(
        body,
        grid=(s // tm,),
        in_specs=[row, row, row, vec],
        out_specs=[row, row, vec],
        out_shape=[
            jax.ShapeDtypeStruct((s, d), F32),
            jax.ShapeDtypeStruct((s, d), BF16),
            jax.ShapeDtypeStruct((1, d), F32),
        ],
        compiler_params=_params(1),
        name=name,
    )(dh, x, dres, g)


def _grad_w(a, b, out_dims, index_map, tm, tn, tk, relu2, name, exs=(), chunks=1):
    t, m = a.shape
    n = b.shape[1]
    nk = t // tk
    cw = tn // chunks

    def body(a_ref, b_ref, o_ref, acc):
        k = pl.program_id(2)

        @pl.when(k == 0)
        def _():
            acc[...] = jnp.zeros_like(acc)

        av = a_ref[...]
        if relu2:
            r = jnp.maximum(av.astype(F32), 0.0)
            av = (r * r).astype(BF16)
        acc[...] += _dot_tn(av, b_ref[...])

        @pl.when(k == nk - 1)
        def _():
            for c in range(chunks):
                o_ref[c] = acc[:, c * cw : (c + 1) * cw].astype(BF16)

    spec = dict(
        grid=(m // tm, n // tn, nk),
        in_specs=[pl.BlockSpec((tk, tm), lambda i, j, k: (k, i)), pl.BlockSpec((tk, tn), lambda i, j, k: (k, j))],
        out_specs=[pl.BlockSpec((chunks, None, tm, cw), lambda i, j, k: index_map(i, j))],
        out_shape=[jax.ShapeDtypeStruct(out_dims, BF16)],
        scratch_shapes=[pltpu.VMEM((tm, tn), F32)],
        compiler_params=_params(3),
        name=name,
    )
    (g,), landed = _call_riding(body, exs, args=(a, b), **spec)
    return (g, *landed) if exs else g


def _out_proj_bwd(dx1b, w_out, tm):
    s, d = dx1b.shape

    def body(dx_ref, w_ref, o_ref):
        o_ref[...] = _dot_nt(dx_ref[...], w_ref[...])

    return _call(
        body,
        grid=(s // tm,),
        in_specs=[pl.BlockSpec((tm, d), lambda i: (i, 0)), pl.BlockSpec((d, d), lambda i: (0, 0))],
        out_specs=pl.BlockSpec((tm, d), lambda i: (i, 0)),
        out_shape=jax.ShapeDtypeStruct((s, d), F32),
        compiler_params=_params(1),
        name="out_proj_bwd",
    )(dx1b, w_out)


def _glu_bwd(y_pre, dmix, wg, bg, go, tm, exs):
    s, w = y_pre.shape
    n = s // tm

    def body(y_ref, dm_ref, wg_ref, bg_ref, go_ref, dy_ref, dwg_ref, dbg_ref, dgo_ref, acc):
        i = pl.program_id(0)

        @pl.when(i == 0)
        def _():
            acc[...] = jnp.zeros_like(acc)
            dbg_ref[...] = jnp.zeros_like(dbg_ref)
            dgo_ref[...] = jnp.zeros_like(dgo_ref)

        yp = y_ref[...]
        yg, yg_grad = _gelu_both(yp)
        ygb = yg.astype(BF16)
        gate = _sigmoid(_dot(ygb, wg_ref[...]) + bg_ref[...])
        ys = yg * gate
        dys, dgt = _rms_bwd(dm_ref[...], ys, _rms(ys), go_ref[...])
        dgo_ref[...] += jnp.sum(dgt, axis=0, keepdims=True)
        dpre = dys * yg * gate * (1.0 - gate)
        dbg_ref[...] += jnp.sum(dpre, axis=0, keepdims=True)
        dpb = dpre.astype(BF16)
        dyg = dys * gate + _dot_nt(dpb, wg_ref[...])
        dy_ref[...] = dyg * yg_grad
        acc[...] += _dot_tn(ygb, dpb)

        @pl.when(i == n - 1)
        def _():
            dwg_ref[...] = acc[...].astype(BF16)

    return _call_riding(
        body,
        exs,
        args=(y_pre, dmix, wg, bg, go),
        grid=(n,),
        in_specs=[
            pl.BlockSpec((tm, w), lambda i: (i, 0)),
            pl.BlockSpec((tm, w), lambda i: (i, 0)),
            pl.BlockSpec((w, w), lambda i: (0, 0)),
            pl.BlockSpec((1, w), lambda i: (0, 0)),
            pl.BlockSpec((1, w), lambda i: (0, 0)),
        ],
        out_specs=[
            pl.BlockSpec((tm, w), lambda i: (i, 0)),
            pl.BlockSpec((w, w), lambda i: (0, 0)),
            pl.BlockSpec((1, w), lambda i: (0, 0)),
            pl.BlockSpec((1, w), lambda i: (0, 0)),
        ],
        out_shape=[
            jax.ShapeDtypeStruct((s, w), F32),
            jax.ShapeDtypeStruct((w, w), BF16),
            jax.ShapeDtypeStruct((1, w), F32),
            jax.ShapeDtypeStruct((1, w), F32),
        ],
        scratch_shapes=[pltpu.VMEM((w, w), F32)],
        compiler_params=_params(1),
        name="glu_bwd",
    )


def _ssm_bwd_state(dy, v2, al, xprev):
    s = dy.shape[0]
    nt, nc, ns = xprev.shape
    hs = ns // 2

    def body(dy_ref, v_ref, al_ref, xp_ref, dys_ref, g_ref, dal_ref, dxp, gs):
        for q in range(PAIRS):
            dys_ref[q, :, 0:LANES] = dy_ref[pl.ds(2 * q, nc, stride=CHUNK), :].astype(BF16)
            dys_ref[q, :, LANES : 2 * LANES] = dy_ref[pl.ds(2 * q + 1, nc, stride=CHUNK), :].astype(BF16)
        acc = _dot(dys_ref[0], v_ref[0])
        for r in range(1, PAIRS):
            acc = acc + _dot(dys_ref[r], v_ref[r])
        dxp[...] = acc
        ar = al_ref[0:1, :]
        ai = al_ref[1:2, :]
        zero = jnp.zeros((1, hs), F32)
        gs[pl.ds(nc - 1, 1), 0:hs] = zero
        gs[pl.ds(nc - 1, 1), hs:ns] = zero

        def step(n, carry):
            gr, gi = carry
            c = nc - 2 - n
            nr = dxp[pl.ds(c + 1, 1), 0:hs] + ar * gr + ai * gi
            ni = dxp[pl.ds(c + 1, 1), hs:ns] + ar * gi - ai * gr
            gs[pl.ds(c, 1), 0:hs] = nr
            gs[pl.ds(c, 1), hs:ns] = ni
            return nr, ni

        lax.fori_loop(0, nc - 1, step, (zero, zero))
        gv = gs[...]
        xv = xp_ref[...]
        gr, gi = gv[:, 0:hs], gv[:, hs:ns]
        xr, xi = xv[:, 0:hs], xv[:, hs:ns]
        dal_ref[0:1, :] = jnp.sum(gr * xr + gi * xi, axis=0, keepdims=True)
        dal_ref[1:2, :] = jnp.sum(gi * xr - gr * xi, axis=0, keepdims=True)
        g_ref[...] = gv.astype(BF16)

    return _call(
        body,
        grid=(nt,),
        in_specs=[
            pl.BlockSpec((s, LANES), lambda t: (0, t)),
            pl.BlockSpec((None, PAIRS, 2 * LANES, ns), lambda t: (t, 0, 0, 0)),
            pl.BlockSpec((None, 2, hs), lambda t: (t, 0, 0)),
            pl.BlockSpec((None, nc, ns), lambda t: (t, 0, 0)),
        ],
        out_specs=[
            pl.BlockSpec((None, PAIRS, nc, 2 * LANES), lambda t: (t, 0, 0, 0)),
            pl.BlockSpec((None, nc, ns), lambda t: (t, 0, 0)),
            pl.BlockSpec((None, 2, hs), lambda t: (t, 0, 0)),
        ],
        out_shape=[
            jax.ShapeDtypeStruct((nt, PAIRS, nc, 2 * LANES), BF16),
            jax.ShapeDtypeStruct((nt, nc, ns), BF16),
            jax.ShapeDtypeStruct((nt, 2, hs), F32),
        ],
        scratch_shapes=[pltpu.VMEM((nc, ns), F32), pltpu.VMEM((nc, ns), F32)],
        compiler_params=_params(1),
        name="ssm_bwd_state",
    )(dy, v2, al, xprev)


def _ssm_bwd_main(us, dys, k2, w2, xprev, gst, consts, s, width, exs):
    nt, _, nc, _ = us.shape
    ns = xprev.shape[-1]
    rep_k, mask_k, rep_s, mask_s = consts
    nk = 2 * LANES
    gg = GROUPS_PER_TILE
    rows_of = [(gi, st, slice((st * gg + gi) * SSM_GROUP, (st * gg + gi + 1) * SSM_GROUP))
               for st in range(2) for gi in range(gg)]

    def body(us_ref, dys_ref, k_ref, w_ref, xp_ref, g_ref, rk_ref, mk_ref, rs_ref, ms_ref,
             du_ref, dk_ref, dw_ref, dv_ref, acc, du2, duf, dkf):
        q = pl.program_id(1)

        @pl.when(q == 0)
        def _():
            dkf[...] = jnp.zeros_like(dkf)

        usq = us_ref[q]
        gb = g_ref[...]
        dw = _gather_blocks(_dot_tn(usq, gb), ms_ref[...], rs_ref[...])
        dv = _gather_blocks(_dot_tn(dys_ref[q], xp_ref[...].astype(BF16)), ms_ref[...], rs_ref[...])
        for gi, st, rows in rows_of:
            dw_ref[gi, st] = dw[rows]
            dv_ref[gi, st] = dv[rows]
        acc[...] = _dot_nt(gb, w_ref[...])
        for m in range(PAIRS):

            @pl.when(q + m < PAIRS)
            def _():
                dyr = dys_ref[jnp.minimum(q + m, PAIRS - 1)]
                acc[...] += _dot_nt(dyr, k_ref[m])
                dkf[m] += _dot_tn(usq, dyr)

        du2[q] = acc[...]

        @pl.when(q == PAIRS - 1)
        def _():
            for p in range(PAIRS):
                duf[pl.ds(2 * p, nc, stride=CHUNK), :] = du2[p, :, 0:LANES]
                duf[pl.ds(2 * p + 1, nc, stride=CHUNK), :] = du2[p, :, LANES : 2 * LANES]
                dk = _gather_blocks(dkf[p], mk_ref[...], rk_ref[...])
                for gi, st, rows in rows_of:
                    dk_ref[gi, p, st] = dk[rows]
            du_ref[...] = duf[...].astype(BF16)

    def per_tile(rows, cols):
        return pl.BlockSpec((None, PAIRS, rows, cols), lambda t, q: (t, 0, 0, 0))

    def per_pair(rows, cols):
        return pl.BlockSpec((None, None, rows, cols), lambda t, q: (t, q, 0, 0))

    def whole(a):
        return pl.BlockSpec(a.shape, lambda t, q: (0, 0))

    return _call_riding(
        body,
        exs,
        args=(us, dys, k2, w2, xprev, gst, rep_k, mask_k, rep_s, mask_s),
        grid=(nt, PAIRS),
        in_specs=[
            per_tile(nc, nk),
            per_tile(nc, nk),
            per_tile(nk, nk),
            per_pair(nk, ns),
            pl.BlockSpec((None, nc, ns), lambda t, q: (t, 0, 0)),
            pl.BlockSpec((None, nc, ns), lambda t, q: (t, 0, 0)),
            whole(rep_k),
            whole(mask_k),
            whole(rep_s),
            whole(mask_s),
        ],
        out_specs=[
            pl.BlockSpec((s, LANES), lambda t, q: (0, t)),
            pl.BlockSpec((gg, PAIRS, 2, SSM_GROUP, LANES), lambda t, q: (t, 0, 0, 0, 0)),
            pl.BlockSpec((gg, None, 2, SSM_GROUP, LANES), lambda t, q: (t, q, 0, 0, 0)),
            pl.BlockSpec((gg, None, 2, SSM_GROUP, LANES), lambda t, q: (t, q, 0, 0, 0)),
        ],
        out_shape=[
            jax.ShapeDtypeStruct((s, width), BF16),
            jax.ShapeDtypeStruct((nt * gg, PAIRS, 2, SSM_GROUP, LANES), F32),
            jax.ShapeDtypeStruct((nt * gg, PAIRS, 2, SSM_GROUP, LANES), F32),
            jax.ShapeDtypeStruct((nt * gg, PAIRS, 2, SSM_GROUP, LANES), F32),
        ],
        scratch_shapes=[
            pltpu.VMEM((nc, nk), F32),
            pltpu.VMEM((PAIRS, nc, nk), F32),
            pltpu.VMEM((s, LANES), F32),
            pltpu.VMEM((PAIRS, nk, nk), F32),
        ],
        compiler_params=_params(2),
        name="ssm_bwd_main",
    )


def _sgu_bwd(z, dmix, dz, lng, lnb, wt, wtt, biasb, go, rb, exs):
    s = z.shape[0]
    w = lng.shape[-1]
    heads = wt.shape[0]
    ncol = (LANES, LANES)

    def body(zu_ref, zv_ref, dm_ref, dz_any, lng_ref, lnb_ref, wt_ref, wtt_ref, b_ref, go_ref,
             dz_ref, dw_ref, db_ref, dlg_ref, dlb_ref, dgo_ref, dzv):
        del dz_any
        i = pl.program_id(0)
        p = pl.program_id(1)

        @pl.when(jnp.logical_and(i == 0, p == 0))
        def _():
            dw_ref[...] = jnp.zeros_like(dw_ref)
            db_ref[...] = jnp.zeros_like(db_ref)
            dlg_ref[...] = jnp.zeros_like(dlg_ref)
            dlb_ref[...] = jnp.zeros_like(dlb_ref)
            dgo_ref[...] = jnp.zeros_like(dgo_ref)

        @pl.when(p == 0)
        def _():
            lng_v = lng_ref[...]
            for ck in range(rb // SGU_CHUNK):
                rows = slice(ck * SGU_CHUNK, (ck + 1) * SGU_CHUNK)
                zu = zu_ref[rows, :]
                zv = zv_ref[rows, :]
                u, vhat, rstd, vb, mix, u_grad, v_grad = _sgu_parts(zu, zv, lng_v, lnb_ref[...], wt_ref, b_ref)
                y = u * mix
                dy, dgt = _rms_bwd(dm_ref[rows, :], y, _rms(y), go_ref[...])
                dgo_ref[...] += jnp.sum(dgt, axis=0, keepdims=True)
                du = dy * mix
                dmx = dy * u
                dmb = dmx.astype(BF16)
                dvl = []
                for h in range(heads):
                    cols = slice(h * LANES, (h + 1) * LANES)
                    db_ref[h] += jnp.broadcast_to(jnp.sum(dmx[:, cols], axis=-1, keepdims=True), ncol)
                    dw_ref[h] += _dot_nt(dmb[:, cols], vb[:, cols])
                    dvl.append(_dot(wtt_ref[h], dmb[:, cols]))
                dvln = jnp.concatenate(dvl, axis=1)
                dlg_ref[...] += jnp.sum(dvln * vhat, axis=0, keepdims=True)
                dlb_ref[...] += jnp.sum(dvln, axis=0, keepdims=True)
                dvh = dvln * lng_v
                dv = rstd * (
                    dvh
                    - jnp.mean(dvh, axis=-1, keepdims=True)
                    - vhat * jnp.mean(dvh * vhat, axis=-1, keepdims=True)
                )
                dz_ref[rows, :] = (du * u_grad).astype(BF16)
                dzv[rows, :] = (dv * v_grad).astype(BF16)

        @pl.when(p == 1)
        def _():
            dz_ref[...] = dzv[...]

    return _call_riding(
        body,
        exs,
        args=(z, z, dmix, dz, lng, lnb, wt, wtt, biasb, go),
        grid=(s // rb, 2),
        in_specs=[
            pl.BlockSpec((rb, w), lambda i, p: (i, 1)),
            pl.BlockSpec((rb, w), lambda i, p: (i, 2)),
            pl.BlockSpec((rb, w), lambda i, p: (i, 1)),
            pl.BlockSpec(memory_space=pl.ANY),
            pl.BlockSpec((1, w), lambda i, p: (0, 0)),
            pl.BlockSpec((1, w), lambda i, p: (0, 0)),
            pl.BlockSpec((heads, SGU_CHUNK, SGU_CHUNK), lambda i, p: (0, 0, 0)),
            pl.BlockSpec((heads, SGU_CHUNK, SGU_CHUNK), lambda i, p: (0, 0, 0)),
            pl.BlockSpec((heads, SGU_CHUNK, LANES), lambda i, p: (0, 0, 0)),
            pl.BlockSpec((1, w), lambda i, p: (0, 0)),
        ],
        out_specs=[
            pl.BlockSpec((rb, w), lambda i, p: (i, 1 + p)),
            pl.BlockSpec((heads, SGU_CHUNK, SGU_CHUNK), lambda i, p: (0, 0, 0)),
            pl.BlockSpec((heads, SGU_CHUNK, LANES), lambda i, p: (0, 0, 0)),
            pl.BlockSpec((1, w), lambda i, p: (0, 0)),
            pl.BlockSpec((1, w), lambda i, p: (0, 0)),
            pl.BlockSpec((1, w), lambda i, p: (0, 0)),
        ],
        out_shape=[
            jax.ShapeDtypeStruct(dz.shape, dz.dtype),
            jax.ShapeDtypeStruct((heads, SGU_CHUNK, SGU_CHUNK), F32),
            jax.ShapeDtypeStruct((heads, SGU_CHUNK, LANES), F32),
            jax.ShapeDtypeStruct((1, w), F32),
            jax.ShapeDtypeStruct((1, w), F32),
            jax.ShapeDtypeStruct((1, w), F32),
        ],
        scratch_shapes=[pltpu.VMEM((rb, w), BF16)],
        input_output_aliases={3: 0},
        compiler_params=_params(2),
        name="sgu_bwd",
    )


def _in_proj_bwd(dz, w4, x, dres, g, tm, exs, tiles, into, name):
    s, d = x.shape
    nj, _, cw = w4.shape
    first, stop = tiles

    def body(dz_ref, w_ref, x_ref, dr_ref, g_ref, *rest):
        dx_ref, dg_ref, acc = rest[-3:]
        i = pl.program_id(0)
        j = pl.program_id(1)

        @pl.when(j == 0)
        def _():
            acc[...] = jnp.zeros_like(acc)

        @pl.when(jnp.logical_and(i == 0, j == 0))
        def _():
            dg_ref[...] = jnp.zeros_like(dg_ref)

        acc[...] += _dot_nt(dz_ref[...], w_ref[...])

        @pl.when(j == nj - 1)
        def _():
            xv = x_ref[...]
            dx, dgt = _rms_bwd(acc[...], xv, _rms(xv), g_ref[...])
            dx_ref[...] = dr_ref[...] + dx
            dg_ref[...] += jnp.sum(dgt, axis=0, keepdims=True)

    row = pl.BlockSpec((tm, d), lambda i, j: (i + first, 0))
    vec = pl.BlockSpec((1, d), lambda i, j: (0, 0))
    kept = [] if into is None else [into]
    return _call_riding(
        body,
        exs,
        args=(dz, w4, x, dres, g, *kept),
        scratch_shapes=[pltpu.VMEM((tm, d), F32)],
        grid=(stop - first, nj),
        in_specs=[
            pl.BlockSpec((tm, cw), lambda i, j: (i + first, j)),
            pl.BlockSpec((None, d, cw), lambda i, j: (j, 0, 0)),
            row,
            row,
            vec,
        ]
        + [_ANY] * len(kept),
        out_specs=[row, vec],
        out_shape=[jax.ShapeDtypeStruct((s, d), F32), jax.ShapeDtypeStruct((1, d), F32)],
        input_output_aliases={5: 0} if kept else {},
        compiler_params=_params(2),
        name=name,
    )


_ANY = pl.BlockSpec(memory_space=pl.ANY)


def _position():
    x, y, c = lax.axis_index("x"), lax.axis_index("y"), lax.axis_index("c")
    return x, y, c, [(1 - x, y), (x, 1 - y), (1 - x, 1 - y)]


def _remote(src, dst, send_sems, recv_sems, k, to):
    return pltpu.make_async_remote_copy(
        src_ref=src, dst_ref=dst, send_sem=send_sems.at[k], recv_sem=recv_sems.at[k], device_id=to, device_id_type=MESH
    )


class _Riding:
    def __init__(self, srcs, out_shapes, n_sems, start, finish, kept=()):
        self.srcs, self.out_shapes, self.n_sems, self.start, self.finish = srcs, out_shapes, n_sems, start, finish
        self.kept = kept
        self.relay = None


def _call_riding(body, exs, *, grid, in_specs, out_specs, out_shape, scratch_shapes, args, **kw):
    n_in, n_out, n_scr = len(in_specs), len(out_specs), len(scratch_shapes)
    spec = dict(grid=grid, in_specs=list(in_specs), out_specs=list(out_specs), out_shape=list(out_shape))
    if not exs:
        return _call(body, scratch_shapes=list(scratch_shapes), **spec, **kw)(*args), []
    srcs = [a for ex in exs for a in ex.srcs]
    lands = [a for ex in exs for a in ex.out_shapes]
    xi, xo = len(srcs), len(lands)

    def fused(*refs):
        cin, xin = refs[:n_in], refs[n_in : n_in + xi]
        o = n_in + xi
        cout, xout = refs[o : o + n_out], refs[o + n_out : o + n_out + xo]
        scr = refs[o + n_out + xo :]
        ids = [pl.program_id(a) for a in range(len(grid))]
        first = functools.reduce(jnp.logical_and, [i == 0 for i in ids])
        last = functools.reduce(jnp.logical_and, [i == n - 1 for i, n in zip(ids, grid)])

        step = functools.reduce(lambda acc, a: acc * grid[a] + ids[a], range(1, len(grid)), ids[0])

        def each(half):
            si = so = 0
            for e, ex in enumerate(exs):
                ni, no = len(ex.srcs), len(ex.out_shapes)
                sems = scr[n_scr + 2 * e : n_scr + 2 * e + 2]
                if getattr(ex, half) is not None:
                    getattr(ex, half)(xin[si : si + ni], xout[so : so + no], *sems)
                si, so = si + ni, so + no

        @pl.when(first)
        def _():
            each("start")

        @pl.when(step == (3 * math.prod(grid)) // 4)
        def _():
            each("relay")

        body(*cin, *cout, *scr[:n_scr])

        @pl.when(last)
        def _():
            each("finish")

    sems = [pltpu.SemaphoreType.DMA((ex.n_sems,)) for ex in exs for _ in range(2)]
    aliases = dict(kw.pop("input_output_aliases", {}))
    si = so = 0
    for ex in exs:
        for a, b in ex.kept:
            aliases[n_in + si + a] = n_out + so + b
        si, so = si + len(ex.srcs), so + len(ex.out_shapes)
    kw["input_output_aliases"] = aliases
    spec["in_specs"] += [_ANY] * xi
    spec["out_specs"] += [_ANY] * xo
    spec["out_shape"] += lands
    outs = _call(fused, scratch_shapes=list(scratch_shapes) + sems, **spec, **kw)(*args, *srcs)
    return outs[:n_out], outs[n_out:]


def _gather_one(shard, peers=(0, 1, 2), into=None, early_relay=False):
    def own_copies(ins, outs, send, recv):
        x, y, c, chips = _position()
        me = 2 * x + y
        sib = (x, y, 1 - c)
        cps = [] if into is not None else [_remote(ins[0].at[c], outs[0].at[me, c], send, recv, 0, sib)]
        cps += [_remote(ins[0].at[c], outs[0].at[me, c], send, recv, 1 + k, (*chips[k], c)) for k in peers]
        return cps, (c, me, sib, chips)

    def start(ins, outs, send, recv):
        for cp in own_copies(ins, outs, send, recv)[0]:
            cp.start()

    def passed_on(outs, send, recv, c, sib, chips):
        return [_remote(outs[0].at[2 * chips[k][0] + chips[k][1], c], outs[0].at[2 * chips[k][0] + chips[k][1], c],
                        send, recv, 4 + k, sib) for k in peers]

    def relay(ins, outs, send, recv):
        x, y, c, chips = _position()
        for k, cp in zip(peers, passed_on(outs, send, recv, c, (x, y, 1 - c), chips)):
            px, py = chips[k]
            _remote(ins[0].at[c], outs[0].at[2 * px + py, c], send, recv, 1 + k, (px, py, c)).wait_recv()
            cp.start()

    def finish(ins, outs, send, recv):
        if not early_relay:
            relay(ins, outs, send, recv)
        cps, (c, me, sib, chips) = own_copies(ins, outs, send, recv)
        if into is None:
            _remote(ins[0].at[1 - c], outs[0].at[me, 1 - c], send, recv, 0, sib).wait_recv()
        for k in peers:
            px, py = chips[k]
            theirs = outs[0].at[2 * px + py, 1 - c]
            _remote(theirs, theirs, send, recv, 4 + k, sib).wait_recv()
        for cp in cps + passed_on(outs, send, recv, c, sib, chips):
            cp.wait_send()

    land = jax.ShapeDtypeStruct((N_CHIPS,) + shard.shape, shard.dtype)
    rider = _Riding([shard] + ([] if into is None else [into]), [land], 7, start, finish, [] if into is None else [(1, 0)])
    rider.relay = relay if early_relay else None
    return rider


def _to_owner_one(p, peers=(0, 1, 2), into=None):
    def copies(ins, outs, send, recv):
        x, y, c, chips = _position()
        return [_remote(ins[0].at[2 * chips[k][0] + chips[k][1]], outs[0].at[k], send, recv, k, (*chips[k], c)) for k in peers]

    def start(ins, outs, send, recv):
        for cp in copies(ins, outs, send, recv):
            cp.start()

    def finish(ins, outs, send, recv):
        cps = copies(ins, outs, send, recv)
        for cp in cps:
            cp.wait_recv()
        for cp in cps:
            cp.wait_send()

    land = jax.ShapeDtypeStruct((3,) + p.shape[1:], p.dtype)
    return _Riding([p] + ([] if into is None else [into]), [land], 3, start, finish, [] if into is None else [(1, 0)])


def _to_sibling_one(g):
    def copies(ins, outs, send, recv):
        x, y, c, _ = _position()
        return [_remote(ins[0].at[j, 1 - c], outs[0].at[j], send, recv, j, (x, y, 1 - c)) for j in range(N_CHIPS)]

    def start(ins, outs, send, recv):
        for cp in copies(ins, outs, send, recv):
            cp.start()

    def finish(ins, outs, send, recv):
        cps = copies(ins, outs, send, recv)
        for cp in cps:
            cp.wait_recv()
        for cp in cps:
            cp.wait_send()

    return _Riding([g], [jax.ShapeDtypeStruct((g.shape[0],) + g.shape[2:], g.dtype)], N_CHIPS, start, finish)


def _gather_chips(arrs, name):
    n = len(arrs)

    def body(*refs):
        ins, outs = refs[:n], refs[n : 2 * n]
        send_sems, recv_sems, local_sems = refs[2 * n :]
        x, y, c, chips = _position()
        me = 2 * x + y
        local = [pltpu.make_async_copy(ins[i], outs[i].at[me], local_sems.at[i]) for i in range(n)]
        for cp in local:
            cp.start()
        sends = []
        for k, (px, py) in enumerate(chips):
            for i in range(n):
                sends.append(_remote(ins[i], outs[i].at[me], send_sems, recv_sems, k * n + i, (px, py, c)))
                sends[-1].start()
        for k, (px, py) in enumerate(chips):
            for i in range(n):
                _remote(ins[i], outs[i].at[2 * px + py], send_sems, recv_sems, k * n + i, (px, py, c)).wait_recv()
        for cp in sends:
            cp.wait_send()
        for cp in local:
            cp.wait()

    return _call(
        body,
        in_specs=[_ANY] * n,
        out_specs=[_ANY] * n,
        out_shape=[jax.ShapeDtypeStruct((N_CHIPS,) + a.shape, a.dtype) for a in arrs],
        scratch_shapes=[
            pltpu.SemaphoreType.DMA((3 * n,)),
            pltpu.SemaphoreType.DMA((3 * n,)),
            pltpu.SemaphoreType.DMA((n,)),
        ],
        name=name,
    )(*arrs)


def _chip_exchange(ps, name):
    n = len(ps)

    def body(*refs):
        ins, outs = refs[:n], refs[n : 2 * n]
        send_sems, recv_sems = refs[2 * n :]
        x, y, c, chips = _position()
        sends = []
        for k, (px, py) in enumerate(chips):
            for i in range(n):
                sends.append(_remote(ins[i].at[2 * px + py], outs[i].at[k], send_sems, recv_sems, k * n + i, (px, py, c)))
                sends[-1].start()
        for cp in sends:
            cp.wait_recv()
        for cp in sends:
            cp.wait_send()

    return _call(
        body,
        in_specs=[_ANY] * n,
        out_specs=[_ANY] * n,
        out_shape=[jax.ShapeDtypeStruct((3,) + p.shape[1:], p.dtype) for p in ps],
        scratch_shapes=[pltpu.SemaphoreType.DMA((3 * n,)), pltpu.SemaphoreType.DMA((3 * n,))],
        name=name,
    )(*ps)


def _pair_exchange(gs, name):
    n = len(gs)

    def body(*refs):
        ins, outs = refs[:n], refs[n : 2 * n]
        send_sems, recv_sems = refs[2 * n :]
        x, y, c, _ = _position()
        sib = (x, y, 1 - c)
        cps = []
        for i in range(n):
            for j in range(N_CHIPS):
                cps.append(_remote(ins[i].at[j, 1 - c], outs[i].at[j], send_sems, recv_sems, i * N_CHIPS + j, sib))
                cps[-1].start()
        for cp in cps:
            cp.wait_recv()
        for cp in cps:
            cp.wait_send()

    return _call(
        body,
        in_specs=[_ANY] * n,
        out_specs=[_ANY] * n,
        out_shape=[jax.ShapeDtypeStruct((g.shape[0],) + g.shape[2:], g.dtype) for g in gs],
        scratch_shapes=[pltpu.SemaphoreType.DMA((n * N_CHIPS,)), pltpu.SemaphoreType.DMA((n * N_CHIPS,))],
        name=name,
    )(*gs)


def _place_own(shard, gathered, pos, name):
    _, hr, cols = shard.shape
    tr = _row_tile(hr, 512)

    def body(pos_ref, s_ref, g_any, o_ref):
        del pos_ref, g_any
        o_ref[...] = s_ref[...]

    return _call(
        body,
        grid_spec=pltpu.PrefetchScalarGridSpec(
            num_scalar_prefetch=1,
            grid=(hr // tr,),
            in_specs=[pl.BlockSpec((None, tr, cols), lambda i, p: (p[1], i, 0)), _ANY],
            out_specs=pl.BlockSpec((None, None, tr, cols), lambda i, p: (p[0], p[1], i, 0)),
        ),
        out_shape=jax.ShapeDtypeStruct(gathered.shape, gathered.dtype),
        input_output_aliases={2: 0},
        compiler_params=_params(1),
        name=name,
    )(pos, shard, gathered)


def _pair_share(fs, name):
    n = len(fs)

    def body(*refs):
        ins, outs = refs[:n], refs[n : 2 * n]
        send_sems, recv_sems = refs[2 * n :]
        x, y, c, _ = _position()
        sib = (x, y, 1 - c)
        sends = [_remote(ins[i].at[c], outs[i].at[c], send_sems, recv_sems, i, sib) for i in range(n)]
        for cp in sends:
            cp.start()
        for i in range(n):
            _remote(ins[i].at[1 - c], outs[i].at[1 - c], send_sems, recv_sems, i, sib).wait_recv()
        for cp in sends:
            cp.wait_send()

    return _call(
        body,
        in_specs=[_ANY] * n,
        out_specs=[_ANY] * n,
        out_shape=[jax.ShapeDtypeStruct(f.shape, f.dtype) for f in fs],
        input_output_aliases={i: i for i in range(n)},
        scratch_shapes=[pltpu.SemaphoreType.DMA((n,)), pltpu.SemaphoreType.DMA((n,))],
        name=name,
    )(*fs)


def _pair_sum(g, r, core, name):
    nj, _, hr, cols = g.shape
    tr = _row_tile(hr, 256)

    def body(c_ref, g_ref, r_ref, o_ref):
        del c_ref
        o_ref[...] = (g_ref[...].astype(F32) + r_ref[...].astype(F32)).astype(o_ref.dtype)

    return _call(
        body,
        grid_spec=pltpu.PrefetchScalarGridSpec(
            num_scalar_prefetch=1,
            grid=(nj, hr // tr),
            in_specs=[
                pl.BlockSpec((None, None, tr, cols), lambda j, i, c: (j, c[0], i, 0)),
                pl.BlockSpec((None, tr, cols), lambda j, i, c: (j, i, 0)),
            ],
            out_specs=pl.BlockSpec((None, tr, cols), lambda j, i, c: (j, i, 0)),
        ),
        out_shape=jax.ShapeDtypeStruct((nj, hr, cols), g.dtype),
        compiler_params=_params(2),
        name=name,
    )(core, g, r)


def _chip_sum(p, q, pos, name):
    nq, hr, cols = q.shape
    tr = _row_tile(hr, 256)

    def body(pos_ref, p_ref, *refs):
        del pos_ref
        o_ref = refs[nq]
        tot = p_ref[...].astype(F32)
        for k in range(nq):
            tot = tot + refs[k][...].astype(F32)
        o_ref[...] = tot

    return _call(
        body,
        grid_spec=pltpu.PrefetchScalarGridSpec(
            num_scalar_prefetch=1,
            grid=(hr // tr,),
            in_specs=[pl.BlockSpec((None, tr, cols), lambda i, s: (s[0], i, 0))]
            + [pl.BlockSpec((None, tr, cols), functools.partial(lambda k, i, s: (k, i, 0), k)) for k in range(nq)],
            out_specs=pl.BlockSpec((None, tr, cols), lambda i, s: (s[1], i, 0)),
        ),
        out_shape=jax.ShapeDtypeStruct((2, hr, cols), F32),
        compiler_params=_params(1),
        name=name,
    )(pos, p, *([q] * nq))


def _adamw(w, g, m, v, name):
    rows, cols = w.shape
    tr = _row_tile(rows, max(8, (2**18) // cols), mult=8)
    c1 = 1.0 - ADAM_B1**ADAM_STEP
    c2 = 1.0 - ADAM_B2**ADAM_STEP

    def body(w_ref, g_ref, m_ref, v_ref, gk_ref, d_ref, nm_ref, nv_ref):
        gv = g_ref[...]
        gk_ref[...] = gv
        nm = ADAM_B1 * m_ref[...] + (1.0 - ADAM_B1) * gv
        nv = ADAM_B2 * v_ref[...] + (1.0 - ADAM_B2) * (gv * gv)
        nm_ref[...] = nm
        nv_ref[...] = nv
        d_ref[...] = -ADAM_LR * ((nm / c1) / (jnp.sqrt(nv / c2) + ADAM_EPS) + ADAM_WD * w_ref[...])

    spec = pl.BlockSpec((tr, cols), lambda i: (i, 0))
    sds = jax.ShapeDtypeStruct((rows, cols), F32)
    return _call(
        body,
        grid=(rows // tr,),
        in_specs=[spec] * 4,
        out_specs=[spec] * 4,
        out_shape=[sds] * 4,
        compiler_params=_params(1),
        name=name,
    )(w, g, m, v)


_TILE_ELEMS = 8 * LANES
_FLAT_ROW_MULT = 8 * 2 * N_CHIPS


def _flat_rows(shape):
    n = math.prod(shape)
    return (n + _TILE_ELEMS - 1) // _TILE_ELEMS * 8


def _pack(arrs):
    parts = []
    for a in arrs:
        rows = _flat_rows(a.shape)
        flat = a.reshape(-1).astype(F32)
        flat = jnp.pad(flat, (0, rows * LANES - flat.shape[0]))
        parts.append(flat.reshape(rows, LANES))
    total = sum(p.shape[0] for p in parts)
    pad = -total % _FLAT_ROW_MULT
    if pad:
        parts.append(jnp.zeros((pad, LANES), F32))
    return jnp.concatenate(parts, axis=0)


def _unpack(flat, shapes):
    out, row = [], 0
    for shp in shapes:
        rows = _flat_rows(shp)
        out.append(flat[row : row + rows].reshape(-1)[: math.prod(shp)].reshape(shp))
        row += rows
    return out


def kernel(x, norm_mix_g, w_in, ssm_a_re, ssm_a_im, ssm_b_re, ssm_b_im, ssm_c_re, ssm_c_im, ssm_d, ssm_log_dt, ssm_glu_w, ssm_glu_b, sgu_ln_g, sgu_ln_b, sgu_w, sgu_b, out_norm_ssm_g, out_norm_sgu_g, w_out, norm_mlp_g, w_up, w_down, norm_final_g, loss_target, m_norm_mix_g, m_w_in, m_ssm_a_re, m_ssm_a_im, m_ssm_b_re, m_ssm_b_im, m_ssm_c_re, m_ssm_c_im, m_ssm_d, m_ssm_log_dt, m_ssm_glu_w, m_ssm_glu_b, m_sgu_ln_g, m_sgu_ln_b, m_sgu_w, m_sgu_b, m_out_norm_ssm_g, m_out_norm_sgu_g, m_w_out, m_norm_mlp_g, m_w_up, m_w_down, m_norm_final_g, v_norm_mix_g, v_w_in, v_ssm_a_re, v_ssm_a_im, v_ssm_b_re, v_ssm_b_im, v_ssm_c_re, v_ssm_c_im, v_ssm_d, v_ssm_log_dt, v_ssm_glu_w, v_ssm_glu_b, v_sgu_ln_g, v_sgu_ln_b, v_sgu_w, v_sgu_b, v_out_norm_ssm_g, v_out_norm_sgu_g, v_w_out, v_norm_mlp_g, v_w_up, v_w_down, v_norm_final_g):
    weights = dict(norm_mix_g=norm_mix_g, w_in=w_in, ssm_a_re=ssm_a_re, ssm_a_im=ssm_a_im, ssm_b_re=ssm_b_re, ssm_b_im=ssm_b_im, ssm_c_re=ssm_c_re, ssm_c_im=ssm_c_im, ssm_d=ssm_d, ssm_log_dt=ssm_log_dt, ssm_glu_w=ssm_glu_w, ssm_glu_b=ssm_glu_b, sgu_ln_g=sgu_ln_g, sgu_ln_b=sgu_ln_b, sgu_w=sgu_w, sgu_b=sgu_b, out_norm_ssm_g=out_norm_ssm_g, out_norm_sgu_g=out_norm_sgu_g, w_out=w_out, norm_mlp_g=norm_mlp_g, w_up=w_up, w_down=w_down, norm_final_g=norm_final_g)
    mom_m = dict(norm_mix_g=m_norm_mix_g, w_in=m_w_in, ssm_a_re=m_ssm_a_re, ssm_a_im=m_ssm_a_im, ssm_b_re=m_ssm_b_re, ssm_b_im=m_ssm_b_im, ssm_c_re=m_ssm_c_re, ssm_c_im=m_ssm_c_im, ssm_d=m_ssm_d, ssm_log_dt=m_ssm_log_dt, ssm_glu_w=m_ssm_glu_w, ssm_glu_b=m_ssm_glu_b, sgu_ln_g=m_sgu_ln_g, sgu_ln_b=m_sgu_ln_b, sgu_w=m_sgu_w, sgu_b=m_sgu_b, out_norm_ssm_g=m_out_norm_ssm_g, out_norm_sgu_g=m_out_norm_sgu_g, w_out=m_w_out, norm_mlp_g=m_norm_mlp_g, w_up=m_w_up, w_down=m_w_down, norm_final_g=m_norm_final_g)
    mom_v = dict(norm_mix_g=v_norm_mix_g, w_in=v_w_in, ssm_a_re=v_ssm_a_re, ssm_a_im=v_ssm_a_im, ssm_b_re=v_ssm_b_re, ssm_b_im=v_ssm_b_im, ssm_c_re=v_ssm_c_re, ssm_c_im=v_ssm_c_im, ssm_d=v_ssm_d, ssm_log_dt=v_ssm_log_dt, ssm_glu_w=v_ssm_glu_w, ssm_glu_b=v_ssm_glu_b, sgu_ln_g=v_sgu_ln_g, sgu_ln_b=v_sgu_ln_b, sgu_w=v_sgu_w, sgu_b=v_sgu_b, out_norm_ssm_g=v_out_norm_ssm_g, out_norm_sgu_g=v_out_norm_sgu_g, w_out=v_w_out, norm_mlp_g=v_norm_mlp_g, w_up=v_w_up, w_down=v_w_down, norm_final_g=v_norm_final_g)
    names = list(weights)
    large = ["w_in", "ssm_glu_w", "w_out", "w_up", "w_down"]
    small = [n for n in names if n not in large]

    s, d = x.shape[1], x.shape[2]
    xs = x.reshape(s, d)
    target = loss_target.reshape(s, d)
    width = ssm_glu_w.shape[-1]
    ff = w_down.shape[1] * N_CHIPS
    tm = min(512, s)
    core = lax.axis_index("c").astype(jnp.int32).reshape(1)
    chip = (2 * lax.axis_index("x") + lax.axis_index("y")).astype(jnp.int32).reshape(1)
    pos = jnp.concatenate([chip, core])

    shards = [w[0].astype(BF16).reshape(2, w.shape[1] // 2, w.shape[2]) for w in (w_in, ssm_glu_w, w_out, w_up, w_down)]
    ssm_args = (ssm_a_re[0], ssm_a_im[0], ssm_b_re[0], ssm_b_im[0], ssm_c_re[0], ssm_c_im[0], ssm_d[0], ssm_log_dt[0])
    (k2c, w2c, v2c, al), ssm_vjp = jax.vjp(_ssm_mats, *ssm_args)
    consts = _spread_consts(ssm_b_re.shape[-1], ssm_a_re.shape[-1])
    (k2b, w2b, v2b), (w_in_g,) = _ssm_spread(k2c, w2c, v2c, consts, [_gather_one(shards[0])])
    w_in4 = _place_own(shards[0], w_in_g, pos, "place_w_in").reshape(N_CHIPS, d, w_in.shape[2])
    causal = jnp.tril(jnp.ones((SGU_CHUNK, SGU_CHUNK), dtype=bool))
    wt = jnp.where(causal[None], sgu_w[0], 0.0)
    wtb = wt.astype(BF16)
    wttb = jnp.swapaxes(wt, 1, 2).astype(BF16)
    heads = sgu_w.shape[1]
    biasb = jnp.broadcast_to(sgu_b[0][:, :, None], (heads, SGU_CHUNK, LANES))

    tm_big = min(1024, s)
    (z, h1b), (glu_g, out_g, up_g) = _in_proj(
        xs, norm_mix_g, w_in4, tm_big,
        [_gather_one(shards[1]), _gather_one(shards[2]), _gather_one(shards[3], peers=(0, 1))])
    wg_full = _place_own(shards[1], glu_g, pos, "place_glu_w").reshape(width, width)
    w_out_full = _place_own(shards[2], out_g, pos, "place_w_out").reshape(d, d)
    (y_pre, xprev, us), (up_g,) = _ssm_fwd(z, k2b, w2b, v2b, al, [_gather_one(shards[3], peers=(2,), into=up_g)])
    up4 = _place_own(shards[3], up_g, pos, "place_w_up").reshape(N_CHIPS, d, w_up.shape[2])
    mixed = _glu_fwd(y_pre, wg_full, ssm_glu_b, out_norm_ssm_g, d, tm)
    mixed = _sgu_fwd(z, mixed, sgu_ln_g, sgu_ln_b, wtb, biasb, out_norm_sgu_g, tm)
    x1, h2b = _out_proj(xs, mixed, w_out_full, norm_mlp_g, tm)
    tf = min(1024, up4.shape[-1])
    (up,), (down_g,) = _mlp_up(
        h2b, up4, tm_big, min(2048, up4.shape[-1]), [_gather_one(shards[4], early_relay=True)])
    w_down_full = _place_own(shards[4], down_g, pos, "place_w_down").reshape(ff, d)
    loss_part, dx2, dx2b, d_norm_final = _mlp_down_loss(
        x1, up, w_down_full, target, norm_final_g.reshape(1, d), tm, tf)

    dup, dh2 = _mlp_bwd(dx2b, up, up4, w_down_full, tm_big, min(1024, up4.shape[-1]))
    dx1, dx1b, d_norm_mlp = _norm_bwd(dh2, x1, dx2, norm_mlp_g, min(256, s), "norm_mlp_bwd")
    hr_big = d // 2
    tkk = min(2048, s)
    g_down = _grad_w(
        up, dx2b, (N_CHIPS, 2, hr_big, d),
        lambda i, j: (i // (2 * (hr_big // min(1024, hr_big))), (i // (hr_big // min(1024, hr_big))) % 2,
                      i % (hr_big // min(1024, hr_big)), j),
        min(1024, hr_big), d, tkk, True, "grad_w_down")
    tn_up = min(2048, up4.shape[-1])
    per_up = up4.shape[-1] // tn_up
    g_up, sib_down = _grad_w(
        h2b, dup, (N_CHIPS, 2, hr_big, up4.shape[-1]),
        lambda i, j: (j // per_up, i // (hr_big // min(1024, hr_big)), i % (hr_big // min(1024, hr_big)), j % per_up),
        min(1024, hr_big), tn_up, tkk, False, "grad_w_up", exs=[_to_sibling_one(g_down)])
    pair_down = _pair_sum(g_down, sib_down, core, "pair_sum_w_down")
    dmix = _out_proj_bwd(dx1b, w_out_full, tm)
    hr_out = d // (2 * N_CHIPS)
    g_out, sib_up = _grad_w(
        mixed, dx1b, (1, 1, d, d), lambda i, j: (0, 0, i, j), min(1024, d), d, tkk, False, "grad_w_out",
        exs=[_to_sibling_one(g_up)])
    g_out = g_out.reshape(N_CHIPS, 2, hr_out, d)
    pair_up = _pair_sum(g_up, sib_up, core, "pair_sum_w_up")
    (dy_pre, g_glu, d_glu_b, d_norm_ssm), (sib_out,) = _glu_bwd(
        y_pre, dmix, wg_full, ssm_glu_b, out_norm_ssm_g, tm, [_to_sibling_one(g_out)])
    pair_out = _pair_sum(g_out, sib_out, core, "pair_sum_w_out")
    dys, gst, d_al = _ssm_bwd_state(dy_pre, v2b, al, xprev)
    (dz, d_k2, d_w2, d_v2), (chips_down,) = _ssm_bwd_main(
        us, dys, k2b, w2b, xprev, gst, consts, s, z.shape[1], [_to_owner_one(pair_down)])
    (dz, d_wt, d_bias, d_ln_g, d_ln_b, d_norm_sgu), (chips_up,) = _sgu_bwd(
        z, dmix, dz, sgu_ln_g, sgu_ln_b, wtb, wttb, biasb, out_norm_sgu_g, tm,
        [_to_owner_one(pair_up, peers=(0, 1))])
    cw_in = w_in4.shape[-1]
    g_in, chips_out, chips_up = _grad_w(
        h1b, dz, (N_CHIPS, 2, hr_big, cw_in),
        lambda i, j: (j, i // (hr_big // min(1024, hr_big)), i % (hr_big // min(1024, hr_big)), 0),
        min(1024, hr_big), 2 * cw_in, tkk, False, "grad_w_in",
        exs=[_to_owner_one(pair_out), _to_owner_one(pair_up, peers=(2,), into=chips_up)], chunks=2)
    (sib_in,) = _pair_exchange([g_in], "w_in_to_sibling")
    pair_in = _pair_sum(g_in, sib_in, core, "pair_sum_w_in")
    tm_x = min(512, s // 2)
    n_x = s // tm_x
    (gx_half, dg_a), (chips_in,) = _in_proj_bwd(
        dz, w_in4, xs, dx1, norm_mix_g, tm_x, [_to_owner_one(pair_in)], (0, n_x // 2), None, "in_proj_bwd_a")
    (grad_x, dg_b), _ = _in_proj_bwd(
        dz, w_in4, xs, dx1, norm_mix_g, tm_x, [], (n_x // 2, n_x), gx_half, "in_proj_bwd_b")
    d_norm_mix = dg_a + dg_b

    d_ssm = ssm_vjp((d_k2, d_w2, d_v2, d_al))
    small_grads = dict(
        norm_mix_g=d_norm_mix, ssm_a_re=d_ssm[0], ssm_a_im=d_ssm[1], ssm_b_re=d_ssm[2], ssm_b_im=d_ssm[3],
        ssm_c_re=d_ssm[4], ssm_c_im=d_ssm[5], ssm_d=d_ssm[6], ssm_log_dt=d_ssm[7], ssm_glu_b=d_glu_b,
        sgu_ln_g=d_ln_g, sgu_ln_b=d_ln_b, sgu_w=jnp.where(causal[None], d_wt, 0.0), sgu_b=d_bias[:, :, 0],
        out_norm_ssm_g=d_norm_ssm, out_norm_sgu_g=d_norm_sgu, norm_mlp_g=d_norm_mlp, norm_final_g=d_norm_final)
    flat = _pack([small_grads[n] for n in small])
    hr_small = flat.shape[0] // (2 * N_CHIPS)
    g_small = flat.reshape(N_CHIPS, 2, hr_small, LANES)

    hr_glu = width // (2 * N_CHIPS)
    grads = [g_glu.reshape(N_CHIPS, 2, hr_glu, width), g_small]
    tags = ["glu_w", "small"]
    from_sib = _pair_exchange(grads, "grads_to_sibling")
    pair = [_pair_sum(g, r, core, "pair_sum_" + t) for g, r, t in zip(grads, from_sib, tags)]
    from_chips = _chip_exchange(pair, "grads_to_owner")
    pair = [pair_in, pair[0], pair_out, pair_up, pair_down, pair[1]]
    from_chips = [chips_in, from_chips[0], chips_out, chips_up, chips_down, from_chips[1]]
    tags = ["w_in", "glu_w", "w_out", "w_up", "w_down", "small"]
    halves = [_chip_sum(p, q, pos, "chip_sum_" + t) for p, q, t in zip(pair, from_chips, tags)]
    owned = _pair_share(halves, "grads_to_both_cores")
    (small_all,) = _gather_chips([owned[5]], "gather_small_grads")
    small_flat = small_all.reshape(flat.shape)

    grad_out, delta_out, m_out, v_out = {}, {}, {}, {}
    for n, g in zip(large, owned[:5]):
        shp = weights[n].shape
        g2, dl, nm, nv = _adamw(
            weights[n][0], g.reshape(shp[1], shp[2]), mom_m[n][0], mom_v[n][0], "adamw_" + n)
        grad_out[n], delta_out[n], m_out[n], v_out[n] = g2.reshape(shp), dl.reshape(shp), nm.reshape(shp), nv.reshape(shp)
    shapes = [weights[n].shape for n in small]
    g2, dl, nm, nv = _adamw(
        _pack([weights[n] for n in small]), small_flat, _pack([mom_m[n] for n in small]),
        _pack([mom_v[n] for n in small]), "adamw_small")
    for n, g, a, b, c in zip(small, _unpack(g2, shapes), _unpack(dl, shapes), _unpack(nm, shapes), _unpack(nv, shapes)):
        grad_out[n], delta_out[n], m_out[n], v_out[n] = g, a, b, c

    loss = lax.psum(loss_part[0, 0], ("x", "y", "c"))
    return (loss, grad_x.reshape(x.shape), *[grad_out[n] for n in names], *[delta_out[n] for n in names],
            *[m_out[n] for n in names], *[v_out[n] for n in names])
```

```python
import functools
import math

import numpy as np
import jax
import jax.numpy as jnp
from jax import lax
from jax.experimental import pallas as pl
from jax.experimental.pallas import tpu as pltpu

F32 = jnp.float32
BF16 = jnp.bfloat16
MESH = pl.DeviceIdType.MESH
HIGHEST = lax.Precision.HIGHEST

EPS = 1e-6
ADAM_LR = 0.001
ADAM_B1 = 0.9
ADAM_B2 = 0.999
ADAM_EPS = 1e-08
ADAM_WD = 0.01
ADAM_STEP = 10

N_CHIPS = 4
LANES = 128
SSM_GROUP = 16
SSM_STATE = 64
GROUPS_PER_TILE = LANES // SSM_GROUP
CHUNK = 16
PAIRS = CHUNK // 2
SGU_CHUNK = 128
VMEM_LIMIT = 56 * 2**20


def _params(n_axes, vmem=VMEM_LIMIT):
    return pltpu.CompilerParams(dimension_semantics=("arbitrary",) * n_axes, vmem_limit_bytes=vmem)


def _call(body, **kw):
    return pl.pallas_call(body, **kw)


def _dot(a, b):
    return jnp.dot(a, b, preferred_element_type=F32)


def _dot_nt(a, b):
    return lax.dot_general(a, b, (((1,), (1,)), ((), ())), preferred_element_type=F32)


def _dot_tn(a, b):
    return lax.dot_general(a, b, (((0,), (0,)), ((), ())), preferred_element_type=F32)


_GELU_K = math.sqrt(2.0 / math.pi)
_GELU_C = 0.044715


def _gelu_both(x):
    x2 = x * x
    t = jnp.tanh(x * (_GELU_K + (_GELU_K * _GELU_C) * x2))
    hx = 0.5 * x
    onep = 1.0 + t
    return hx * onep, 0.5 * onep + hx * (1.0 - t * t) * (_GELU_K + (3.0 * _GELU_K * _GELU_C) * x2)


def _gelu(x):
    return _gelu_both(x)[0]


def _sigmoid(x):
    return 1.0 / (1.0 + jnp.exp(-x))


def _rms(x):
    return lax.rsqrt(jnp.mean(x * x, axis=-1, keepdims=True) + EPS)


def _rms_bwd(dy, x, r, g):
    a = dy * g
    dx = r * a - x * (r * r * r) * jnp.mean(a * x, axis=-1, keepdims=True)
    return dx, dy * x * r


def _row_tile(rows, target, mult=16):
    for t in range(min(rows, target), 0, -1):
        if rows % t == 0 and t % mult == 0:
            return t
    return rows


def _ssm_mats(a_re, a_im, b_re, b_im, c_re, c_im, d, log_dt):
    g, p = a_re.shape
    h = b_re.shape[-1]
    nt = g // GROUPS_PER_TILE
    dt = jnp.exp(log_dt)[:, None]
    lr, li = a_re * dt, a_im * dt

    def apow(l, lr, li):
        mag = jnp.exp(lr * l)
        return mag * jnp.cos(li * l), mag * jnp.sin(li * l)

    ar, ai = apow(1.0, lr, li)
    den = a_re * a_re + a_im * a_im
    qr = ((ar - 1.0) * a_re + ai * a_im) / den
    qi = (ai * a_re - (ar - 1.0) * a_im) / den
    bt_re, bt_im = jnp.swapaxes(b_re, 1, 2), jnp.swapaxes(b_im, 1, 2)
    bbr = qr[:, None, :] * bt_re - qi[:, None, :] * bt_im
    bbi = qr[:, None, :] * bt_im + qi[:, None, :] * bt_re
    b_same = jnp.concatenate([bbr, bbi], axis=-1)[:, None]
    b_swap = jnp.concatenate([bbi, bbr], axis=-1)[:, None]
    ls = jnp.arange(CHUNK + 1, dtype=F32)[None, :, None]
    pr, pi = apow(ls, lr[:, None, :], li[:, None, :])
    p_same = jnp.concatenate([pr, pr], axis=-1)[:, :, None, :]
    p_sign = jnp.concatenate([-pi, pi], axis=-1)[:, :, None, :]
    tj = p_same[:, :CHUNK] * b_same + p_sign[:, :CHUNK] * b_swap
    c_conj = jnp.concatenate([c_re, -c_im], axis=-1)
    k = jnp.einsum("goq,gliq->glio", c_conj, tj, precision=HIGHEST)
    k = k.at[:, 0].add(d[:, :, None] * jnp.eye(h, dtype=F32))

    zero = jnp.zeros_like(k[:, :1])
    kz = jnp.concatenate([zero, zero, k], axis=1).reshape(g, PAIRS + 1, 2, h, h)
    even, odd = kz[:, :, 0], kz[:, :, 1]
    fill = jnp.zeros((g, PAIRS, h, LANES - 2 * h), F32)
    row0 = jnp.concatenate([even[:, 1:], odd[:, 1:], fill], axis=-1)
    row1 = jnp.concatenate([odd[:, :-1], even[:, 1:], fill], axis=-1)
    k2c = jnp.stack([row0, row1], axis=2)
    w2c = tj[:, ::-1].reshape(g, PAIRS, 2, h, 2 * p)
    c_cross = jnp.concatenate([-c_im, -c_re], axis=-1)
    p_imag = jnp.concatenate([pi, pi], axis=-1)[:, :, None, :]
    v2c = (c_conj[:, None] * p_same[:, 1:] + c_cross[:, None] * p_imag[:, 1:]).reshape(g, PAIRS, 2, h, 2 * p)
    al = jnp.stack([pr[:, CHUNK].reshape(nt, -1), pi[:, CHUNK].reshape(nt, -1)], axis=1)
    return k2c, w2c, v2c, al


def _spread_consts(h, p):
    gg = GROUPS_PER_TILE
    row_g = (np.arange(2 * gg * h) // h) % gg
    colk = np.arange(2 * gg * h)
    rep_k = np.zeros((LANES, 2 * gg * h), np.float32)
    rep_k[(colk // (gg * h)) * h + colk % h, colk] = 1.0
    mask_k = (row_g[:, None] == ((colk // h) % gg)[None, :]).astype(np.float32)
    cols = np.arange(2 * gg * p)
    rep_s = np.zeros((2 * p, 2 * gg * p), np.float32)
    rep_s[(cols // (gg * p)) * p + cols % p, cols] = 1.0
    mask_s = (row_g[:, None] == ((cols // p) % gg)[None, :]).astype(np.float32)
    return tuple(jnp.asarray(a, BF16) for a in (rep_k, mask_k, rep_s, mask_s))


def _ssm_spread(k2c, w2c, v2c, consts, exs):
    h = k2c.shape[3]
    nt = k2c.shape[0] // GROUPS_PER_TILE
    rep_k, mask_k, rep_s, mask_s = consts
    nk, ns = rep_k.shape[1], rep_s.shape[1]

    def body(k_ref, w_ref, v_ref, rk_ref, mk_ref, rs_ref, ms_ref, ko_ref, wo_ref, vo_ref):
        for q in range(PAIRS):
            for c_ref, o_ref, r_ref, m_ref in ((k_ref, ko_ref, rk_ref, mk_ref), (w_ref, wo_ref, rs_ref, ms_ref),
                                               (v_ref, vo_ref, rs_ref, ms_ref)):
                rows = jnp.concatenate(
                    [c_ref[gi, q, st] for st in range(2) for gi in range(GROUPS_PER_TILE)], axis=0)
                o_ref[q] = (_dot(rows.astype(BF16), r_ref[...]) * m_ref[...]).astype(BF16)

    compact = pl.BlockSpec((GROUPS_PER_TILE, PAIRS, 2, h, LANES), lambda t: (t, 0, 0, 0, 0))

    def tile(cols):
        return pl.BlockSpec((None, PAIRS, nk, cols), lambda t: (t, 0, 0, 0))

    def whole(a):
        return pl.BlockSpec(a.shape, lambda t: (0, 0))

    return _call_riding(
        body,
        exs,
        args=(k2c, w2c, v2c, rep_k, mask_k, rep_s, mask_s),
        scratch_shapes=[],
        grid=(nt,),
        in_specs=[compact, compact, compact, whole(rep_k), whole(mask_k), whole(rep_s), whole(mask_s)],
        out_specs=[tile(nk), tile(ns), tile(ns)],
        out_shape=[
            jax.ShapeDtypeStruct((nt, PAIRS, nk, nk), BF16),
            jax.ShapeDtypeStruct((nt, PAIRS, nk, ns), BF16),
            jax.ShapeDtypeStruct((nt, PAIRS, nk, ns), BF16),
        ],
        compiler_params=_params(1),
        name="ssm_spread",
    )


def _gather_blocks(full, mask, rep):
    return _dot_nt((full * mask).astype(BF16), rep)


def _in_proj(x, g, w4, tm, exs):
    s, d = x.shape
    nj, _, cw = w4.shape

    def body(x_ref, g_ref, w_ref, z_ref, h_ref):
        @pl.when(pl.program_id(1) == 0)
        def _():
            xv = x_ref[...]
            h_ref[...] = (xv * _rms(xv) * g_ref[...]).astype(BF16)

        z_ref[...] = _dot(h_ref[...], w_ref[...])

    return _call_riding(
        body,
        exs,
        scratch_shapes=[],
        args=(x, g, w4),
        grid=(s // tm, nj),
        in_specs=[
            pl.BlockSpec((tm, d), lambda i, j: (i, 0)),
            pl.BlockSpec((1, d), lambda i, j: (0, 0)),
            pl.BlockSpec((None, d, cw), lambda i, j: (j, 0, 0)),
        ],
        out_specs=[pl.BlockSpec((tm, cw), lambda i, j: (i, j)), pl.BlockSpec((tm, d), lambda i, j: (i, 0))],
        out_shape=[jax.ShapeDtypeStruct((s, nj * cw), F32), jax.ShapeDtypeStruct((s, d), BF16)],
        compiler_params=_params(2),
        name="in_proj",
    )


def _ssm_fwd(z, k2, w2, v2, al, exs):
    s = z.shape[0]
    nt = k2.shape[0]
    nc = s // CHUNK
    ns = w2.shape[-1]
    hs = ns // 2

    def body(u_ref, k_ref, w_ref, v_ref, al_ref, y_ref, xp_ref, us_ref, xloc):
        for q in range(PAIRS):
            us_ref[q, :, 0:LANES] = u_ref[pl.ds(2 * q, nc, stride=CHUNK), :].astype(BF16)
            us_ref[q, :, LANES : 2 * LANES] = u_ref[pl.ds(2 * q + 1, nc, stride=CHUNK), :].astype(BF16)
        acc = _dot(us_ref[0], w_ref[0])
        for q in range(1, PAIRS):
            acc = acc + _dot(us_ref[q], w_ref[q])
        xloc[...] = acc
        ar = al_ref[0:1, :]
        ai = al_ref[1:2, :]

        def step(c, carry):
            xr, xi = carry
            xp_ref[pl.ds(c, 1), 0:hs] = xr
            xp_ref[pl.ds(c, 1), hs:ns] = xi
            lr = xloc[pl.ds(c, 1), 0:hs]
            li = xloc[pl.ds(c, 1), hs:ns]
            return ar * xr - ai * xi + lr, ar * xi + ai * xr + li

        zero = jnp.zeros((1, hs), F32)
        lax.fori_loop(0, nc, step, (zero, zero))
        xpb = xp_ref[...].astype(BF16)
        for r in range(PAIRS):
            acc = _dot_nt(xpb, v_ref[r])
            for q in range(r + 1):
                acc = acc + _dot(us_ref[q], k_ref[r - q])
            y_ref[pl.ds(2 * r, nc, stride=CHUNK), :] = acc[:, 0:LANES]
            y_ref[pl.ds(2 * r + 1, nc, stride=CHUNK), :] = acc[:, LANES : 2 * LANES]

    return _call_riding(
        body,
        exs,
        args=(z, k2, w2, v2, al),
        grid=(nt,),
        in_specs=[
            pl.BlockSpec((s, LANES), lambda t: (0, t)),
            pl.BlockSpec((None, PAIRS, 2 * LANES, 2 * LANES), lambda t: (t, 0, 0, 0)),
            pl.BlockSpec((None, PAIRS, 2 * LANES, ns), lambda t: (t, 0, 0, 0)),
            pl.BlockSpec((None, PAIRS, 2 * LANES, ns), lambda t: (t, 0, 0, 0)),
            pl.BlockSpec((None, 2, hs), lambda t: (t, 0, 0)),
        ],
        out_specs=[
            pl.BlockSpec((s, LANES), lambda t: (0, t)),
            pl.BlockSpec((None, nc, ns), lambda t: (t, 0, 0)),
            pl.BlockSpec((None, PAIRS, nc, 2 * LANES), lambda t: (t, 0, 0, 0)),
        ],
        out_shape=[
            jax.ShapeDtypeStruct((s, nt * LANES), F32),
            jax.ShapeDtypeStruct((nt, nc, ns), F32),
            jax.ShapeDtypeStruct((nt, PAIRS, nc, 2 * LANES), BF16),
        ],
        scratch_shapes=[pltpu.VMEM((nc, ns), F32)],
        compiler_params=_params(1),
        name="ssm_fwd",
    )


def _glu_fwd(y_pre, wg, bg, go, d_model, tm):
    s, w = y_pre.shape

    def body(y_ref, wg_ref, bg_ref, go_ref, o_ref):
        yg = _gelu(y_ref[...])
        gate = _sigmoid(_dot(yg.astype(BF16), wg_ref[...]) + bg_ref[...])
        ys = yg * gate
        o_ref[...] = (ys * _rms(ys) * go_ref[...]).astype(BF16)

    return _call(
        body,
        grid=(s // tm,),
        in_specs=[
            pl.BlockSpec((tm, w), lambda i: (i, 0)),
            pl.BlockSpec((w, w), lambda i: (0, 0)),
            pl.BlockSpec((1, w), lambda i: (0, 0)),
            pl.BlockSpec((1, w), lambda i: (0, 0)),
        ],
        out_specs=pl.BlockSpec((tm, w), lambda i: (i, 0)),
        out_shape=jax.ShapeDtypeStruct((s, d_model), BF16),
        compiler_params=_params(1),
        name="glu_fwd",
    )(y_pre, wg, bg, go)


def _sgu_parts(zu, zv, lng, lnb, wt_ref, bias_ref):
    u, u_grad = _gelu_both(zu)
    v, v_grad = _gelu_both(zv)
    mu = jnp.mean(v, axis=-1, keepdims=True)
    vc = v - mu
    rstd = lax.rsqrt(jnp.mean(vc * vc, axis=-1, keepdims=True) + EPS)
    vhat = vc * rstd
    vb = (vhat * lng + lnb).astype(BF16)
    heads = wt_ref.shape[0]
    mix = jnp.concatenate(
        [_dot(wt_ref[h], vb[:, h * LANES : (h + 1) * LANES]) + bias_ref[h] for h in range(heads)], axis=1
    )
    return u, vhat, rstd, vb, mix, u_grad, v_grad


def _sgu_fwd(z, mixed, lng, lnb, wt, biasb, go, rb):
    s = z.shape[0]
    w = lng.shape[-1]
    heads = wt.shape[0]

    def body(zu_ref, zv_ref, m_any, lng_ref, lnb_ref, wt_ref, b_ref, go_ref, o_ref):
        del m_any
        for ck in range(rb // SGU_CHUNK):
            rows = slice(ck * SGU_CHUNK, (ck + 1) * SGU_CHUNK)
            u, _, _, _, mix, _, _ = _sgu_parts(zu_ref[rows, :], zv_ref[rows, :], lng_ref[...], lnb_ref[...], wt_ref, b_ref)
            y = u * mix
            o_ref[rows, :] = (y * _rms(y) * go_ref[...]).astype(BF16)

    return _call(
        body,
        grid=(s // rb,),
        in_specs=[
            pl.BlockSpec((rb, w), lambda i: (i, 1)),
            pl.BlockSpec((rb, w), lambda i: (i, 2)),
            pl.BlockSpec(memory_space=pl.ANY),
            pl.BlockSpec((1, w), lambda i: (0, 0)),
            pl.BlockSpec((1, w), lambda i: (0, 0)),
            pl.BlockSpec((heads, SGU_CHUNK, SGU_CHUNK), lambda i: (0, 0, 0)),
            pl.BlockSpec((heads, SGU_CHUNK, LANES), lambda i: (0, 0, 0)),
            pl.BlockSpec((1, w), lambda i: (0, 0)),
        ],
        out_specs=pl.BlockSpec((rb, w), lambda i: (i, 1)),
        out_shape=jax.ShapeDtypeStruct(mixed.shape, mixed.dtype),
        input_output_aliases={2: 0},
        compiler_params=_params(1),
        name="sgu_fwd",
    )(z, z, mixed, lng, lnb, wt, biasb, go)


def _out_proj(x, mixed, w_out, g, tm):
    s, d = x.shape

    def body(x_ref, m_ref, w_ref, g_ref, o_ref, h_ref):
        x1 = x_ref[...] + _dot(m_ref[...], w_ref[...])
        o_ref[...] = x1
        h_ref[...] = (x1 * _rms(x1) * g_ref[...]).astype(BF16)

    row = pl.BlockSpec((tm, d), lambda i: (i, 0))
    return _call(
        body,
        grid=(s // tm,),
        in_specs=[row, row, pl.BlockSpec((d, d), lambda i: (0, 0)), pl.BlockSpec((1, d), lambda i: (0, 0))],
        out_specs=[row, row],
        out_shape=[jax.ShapeDtypeStruct((s, d), F32), jax.ShapeDtypeStruct((s, d), BF16)],
        compiler_params=_params(1),
        name="out_proj",
    )(x, mixed, w_out, g)


def _mlp_up(h, w_up4, tm, tf, exs):
    s, d = h.shape
    nj, _, cw = w_up4.shape
    per = cw // tf

    def body(h_ref, wu_ref, up_ref):
        up_ref[...] = _dot(h_ref[...], wu_ref[...]).astype(BF16)

    return _call_riding(
        body,
        exs,
        scratch_shapes=[],
        args=(h, w_up4),
        grid=(s // tm, nj * per),
        in_specs=[
            pl.BlockSpec((tm, d), lambda i, f: (i, 0)),
            pl.BlockSpec((None, d, tf), lambda i, f: (f // per, 0, f % per)),
        ],
        out_specs=[pl.BlockSpec((tm, tf), lambda i, f: (i, f))],
        out_shape=[jax.ShapeDtypeStruct((s, nj * cw), BF16)],
        compiler_params=_params(2),
        name="mlp_up",
    )


def _mlp_down_loss(x1, up, w_down, target, g, tm, tf):
    s, d = x1.shape
    ff = w_down.shape[0]
    nf = ff // tf

    def body(x_ref, up_ref, wd_ref, t_ref, g_ref, loss_ref, dx_ref, dxb_ref, dg_ref, acc):
        i = pl.program_id(0)
        f = pl.program_id(1)

        @pl.when(f == 0)
        def _():
            acc[...] = x_ref[...]

        @pl.when(jnp.logical_and(i == 0, f == 0))
        def _():
            loss_ref[...] = jnp.zeros_like(loss_ref)
            dg_ref[...] = jnp.zeros_like(dg_ref)

        a = jnp.maximum(up_ref[...].astype(F32), 0.0)
        acc[...] += _dot((a * a).astype(BF16), wd_ref[...])

        @pl.when(f == nf - 1)
        def _():
            xv = acc[...]
            r = _rms(xv)
            gv = g_ref[...]
            diff = xv * r * gv - t_ref[...]
            loss_ref[...] += 0.5 * jnp.sum(jnp.mean(diff * diff, axis=-1, keepdims=True), axis=0, keepdims=True)
            dx, dgt = _rms_bwd(diff * (1.0 / d), xv, r, gv)
            dx_ref[...] = dx
            dxb_ref[...] = dx.astype(BF16)
            dg_ref[...] += jnp.sum(dgt, axis=0, keepdims=True)

    row = pl.BlockSpec((tm, d), lambda i, f: (i, 0))
    vec = pl.BlockSpec((1, d), lambda i, f: (0, 0))
    return _call(
        body,
        grid=(s // tm, nf),
        in_specs=[
            row,
            pl.BlockSpec((tm, tf), lambda i, f: (i, f)),
            pl.BlockSpec((tf, d), lambda i, f: (f, 0)),
            row,
            vec,
        ],
        out_specs=[pl.BlockSpec((1, 1), lambda i, f: (0, 0)), row, row, vec],
        out_shape=[
            jax.ShapeDtypeStruct((1, 1), F32),
            jax.ShapeDtypeStruct((s, d), F32),
            jax.ShapeDtypeStruct((s, d), BF16),
            jax.ShapeDtypeStruct((1, d), F32),
        ],
        scratch_shapes=[pltpu.VMEM((tm, d), F32)],
        compiler_params=_params(2),
        name="mlp_down_loss",
    )(x1, up, w_down, target, g)


def _mlp_bwd(dx2b, up, w_up4, w_down, tm, tf):
    s, d = dx2b.shape
    ff = w_down.shape[0]
    cw = w_up4.shape[-1]
    per = cw // tf

    def body(dx_ref, up_ref, wu_ref, wd_ref, dup_ref, dh_ref):
        @pl.when(pl.program_id(1) == 0)
        def _():
            dh_ref[...] = jnp.zeros_like(dh_ref)

        dact = _dot_nt(dx_ref[...], wd_ref[...])
        dupb = (dact * (2.0 * jnp.maximum(up_ref[...].astype(F32), 0.0))).astype(BF16)
        dup_ref[...] = dupb
        dh_ref[...] += _dot_nt(dupb, wu_ref[...])

    return _call(
        body,
        grid=(s // tm, ff // tf),
        in_specs=[
            pl.BlockSpec((tm, d), lambda i, f: (i, 0)),
            pl.BlockSpec((tm, tf), lambda i, f: (i, f)),
            pl.BlockSpec((None, d, tf), lambda i, f: (f // per, 0, f % per)),
            pl.BlockSpec((tf, d), lambda i, f: (f, 0)),
        ],
        out_specs=[pl.BlockSpec((tm, tf), lambda i, f: (i, f)), pl.BlockSpec((tm, d), lambda i, f: (i, 0))],
        out_shape=[jax.ShapeDtypeStruct((s, ff), BF16), jax.ShapeDtypeStruct((s, d), F32)],
        compiler_params=_params(2),
        name="mlp_bwd",
    )(dx2b, up, w_up4, w_down)


def _norm_bwd(dh, x, dres, g, tm, name):
    s, d = x.shape

    def body(dh_ref, x_ref, dr_ref, g_ref, dx_ref, dxb_ref, dg_ref):
        @pl.when(pl.program_id(0) == 0)
        def _():
            dg_ref[...] = jnp.zeros_like(dg_ref)

        xv = x_ref[...]
        dx, dgt = _rms_bwd(dh_ref[...], xv, _rms(xv), g_ref[...])
        tot = dr_ref[...] + dx
        dx_ref[...] = tot
        dxb_ref[...] = tot.astype(BF16)
        dg_ref[...] += jnp.sum(dgt, axis=0, keepdims=True)

    row = pl.BlockSpec((tm, d), lambda i: (i, 0))
    vec = pl.BlockSpec((1, d), lambda i: (0, 0))
    return _call(
        body,
        grid=(s // tm,),
        in_specs=[row, row, row, vec],
        out_specs=[row, row, vec],
        out_shape=[
            jax.ShapeDtypeStruct((s, d), F32),
            jax.ShapeDtypeStruct((s, d), BF16),
            jax.ShapeDtypeStruct((1, d), F32),
        ],
        compiler_params=_params(1),
        name=name,
    )(dh, x, dres, g)


def _grad_w(a, b, out_dims, index_map, tm, tn, tk, relu2, name, exs=(), chunks=1):
    t, m = a.shape
    n = b.shape[1]
    nk = t // tk
    cw = tn // chunks

    def body(a_ref, b_ref, o_ref, acc):
        k = pl.program_id(2)

        @pl.when(k == 0)
        def _():
            acc[...] = jnp.zeros_like(acc)

        av = a_ref[...]
        if relu2:
            r = jnp.maximum(av.astype(F32), 0.0)
            av = (r * r).astype(BF16)
        acc[...] += _dot_tn(av, b_ref[...])

        @pl.when(k == nk - 1)
        def _():
            for c in range(chunks):
                o_ref[c] = acc[:, c * cw : (c + 1) * cw].astype(BF16)

    spec = dict(
        grid=(m // tm, n // tn, nk),
        in_specs=[pl.BlockSpec((tk, tm), lambda i, j, k: (k, i)), pl.BlockSpec((tk, tn), lambda i, j, k: (k, j))],
        out_specs=[pl.BlockSpec((chunks, None, tm, cw), lambda i, j, k: index_map(i, j))],
        out_shape=[jax.ShapeDtypeStruct(out_dims, BF16)],
        scratch_shapes=[pltpu.VMEM((tm, tn), F32)],
        compiler_params=_params(3),
        name=name,
    )
    (g,), landed = _call_riding(body, exs, args=(a, b), **spec)
    return (g, *landed) if exs else g


def _out_proj_bwd(dx1b, w_out, tm):
    s, d = dx1b.shape

    def body(dx_ref, w_ref, o_ref):
        o_ref[...] = _dot_nt(dx_ref[...], w_ref[...])

    return _call(
        body,
        grid=(s // tm,),
        in_specs=[pl.BlockSpec((tm, d), lambda i: (i, 0)), pl.BlockSpec((d, d), lambda i: (0, 0))],
        out_specs=pl.BlockSpec((tm, d), lambda i: (i, 0)),
        out_shape=jax.ShapeDtypeStruct((s, d), F32),
        compiler_params=_params(1),
        name="out_proj_bwd",
    )(dx1b, w_out)


def _glu_bwd(y_pre, dmix, wg, bg, go, tm, exs):
    s, w = y_pre.shape
    n = s // tm

    def body(y_ref, dm_ref, wg_ref, bg_ref, go_ref, dy_ref, dwg_ref, dbg_ref, dgo_ref, acc):
        i = pl.program_id(0)

        @pl.when(i == 0)
        def _():
            acc[...] = jnp.zeros_like(acc)
            dbg_ref[...] = jnp.zeros_like(dbg_ref)
            dgo_ref[...] = jnp.zeros_like(dgo_ref)

        yp = y_ref[...]
        yg, yg_grad = _gelu_both(yp)
        ygb = yg.astype(BF16)
        gate = _sigmoid(_dot(ygb, wg_ref[...]) + bg_ref[...])
        ys = yg * gate
        dys, dgt = _rms_bwd(dm_ref[...], ys, _rms(ys), go_ref[...])
        dgo_ref[...] += jnp.sum(dgt, axis=0, keepdims=True)
        dpre = dys * yg * gate * (1.0 - gate)
        dbg_ref[...] += jnp.sum(dpre, axis=0, keepdims=True)
        dpb = dpre.astype(BF16)
        dyg = dys * gate + _dot_nt(dpb, wg_ref[...])
        dy_ref[...] = dyg * yg_grad
        acc[...] += _dot_tn(ygb, dpb)

        @pl.when(i == n - 1)
        def _():
            dwg_ref[...] = acc[...].astype(BF16)

    return _call_riding(
        body,
        exs,
        args=(y_pre, dmix, wg, bg, go),
        grid=(n,),
        in_specs=[
            pl.BlockSpec((tm, w), lambda i: (i, 0)),
            pl.BlockSpec((tm, w), lambda i: (i, 0)),
            pl.BlockSpec((w, w), lambda i: (0, 0)),
            pl.BlockSpec((1, w), lambda i: (0, 0)),
            pl.BlockSpec((1, w), lambda i: (0, 0)),
        ],
        out_specs=[
            pl.BlockSpec((tm, w), lambda i: (i, 0)),
            pl.BlockSpec((w, w), lambda i: (0, 0)),
            pl.BlockSpec((1, w), lambda i: (0, 0)),
            pl.BlockSpec((1, w), lambda i: (0, 0)),
        ],
        out_shape=[
            jax.ShapeDtypeStruct((s, w), F32),
            jax.ShapeDtypeStruct((w, w), BF16),
            jax.ShapeDtypeStruct((1, w), F32),
            jax.ShapeDtypeStruct((1, w), F32),
        ],
        scratch_shapes=[pltpu.VMEM((w, w), F32)],
        compiler_params=_params(1),
        name="glu_bwd",
    )


def _ssm_bwd_state(dy, v2, al, xprev):
    s = dy.shape[0]
    nt, nc, ns = xprev.shape
    hs = ns // 2

    def body(dy_ref, v_ref, al_ref, xp_ref, dys_ref, g_ref, dal_ref, dxp, gs):
        for q in range(PAIRS):
            dys_ref[q, :, 0:LANES] = dy_ref[pl.ds(2 * q, nc, stride=CHUNK), :].astype(BF16)
            dys_ref[q, :, LANES : 2 * LANES] = dy_ref[pl.ds(2 * q + 1, nc, stride=CHUNK), :].astype(BF16)
        acc = _dot(dys_ref[0], v_ref[0])
        for r in range(1, PAIRS):
            acc = acc + _dot(dys_ref[r], v_ref[r])
        dxp[...] = acc
        ar = al_ref[0:1, :]
        ai = al_ref[1:2, :]
        zero = jnp.zeros((1, hs), F32)
        gs[pl.ds(nc - 1, 1), 0:hs] = zero
        gs[pl.ds(nc - 1, 1), hs:ns] = zero

        def step(n, carry):
            gr, gi = carry
            c = nc - 2 - n
            nr = dxp[pl.ds(c + 1, 1), 0:hs] + ar * gr + ai * gi
            ni = dxp[pl.ds(c + 1, 1), hs:ns] + ar * gi - ai * gr
            gs[pl.ds(c, 1), 0:hs] = nr
            gs[pl.ds(c, 1), hs:ns] = ni
            return nr, ni

        lax.fori_loop(0, nc - 1, step, (zero, zero))
        gv = gs[...]
        xv = xp_ref[...]
        gr, gi = gv[:, 0:hs], gv[:, hs:ns]
        xr, xi = xv[:, 0:hs], xv[:, hs:ns]
        dal_ref[0:1, :] = jnp.sum(gr * xr + gi * xi, axis=0, keepdims=True)
        dal_ref[1:2, :] = jnp.sum(gi * xr - gr * xi, axis=0, keepdims=True)
        g_ref[...] = gv.astype(BF16)

    return _call(
        body,
        grid=(nt,),
        in_specs=[
            pl.BlockSpec((s, LANES), lambda t: (0, t)),
            pl.BlockSpec((None, PAIRS, 2 * LANES, ns), lambda t: (t, 0, 0, 0)),
            pl.BlockSpec((None, 2, hs), lambda t: (t, 0, 0)),
            pl.BlockSpec((None, nc, ns), lambda t: (t, 0, 0)),
        ],
        out_specs=[
            pl.BlockSpec((None, PAIRS, nc, 2 * LANES), lambda t: (t, 0, 0, 0)),
            pl.BlockSpec((None, nc, ns), lambda t: (t, 0, 0)),
            pl.BlockSpec((None, 2, hs), lambda t: (t, 0, 0)),
        ],
        out_shape=[
            jax.ShapeDtypeStruct((nt, PAIRS, nc, 2 * LANES), BF16),
            jax.ShapeDtypeStruct((nt, nc, ns), BF16),
            jax.ShapeDtypeStruct((nt, 2, hs), F32),
        ],
        scratch_shapes=[pltpu.VMEM((nc, ns), F32), pltpu.VMEM((nc, ns), F32)],
        compiler_params=_params(1),
        name="ssm_bwd_state",
    )(dy, v2, al, xprev)


def _ssm_bwd_main(us, dys, k2, w2, xprev, gst, consts, s, width, exs):
    nt, _, nc, _ = us.shape
    ns = xprev.shape[-1]
    rep_k, mask_k, rep_s, mask_s = consts
    nk = 2 * LANES
    gg = GROUPS_PER_TILE
    rows_of = [(gi, st, slice((st * gg + gi) * SSM_GROUP, (st * gg + gi + 1) * SSM_GROUP))
               for st in range(2) for gi in range(gg)]

    def body(us_ref, dys_ref, k_ref, w_ref, xp_ref, g_ref, rk_ref, mk_ref, rs_ref, ms_ref,
             du_ref, dk_ref, dw_ref, dv_ref, acc, du2, duf, dkf):
        q = pl.program_id(1)

        @pl.when(q == 0)
        def _():
            dkf[...] = jnp.zeros_like(dkf)

        usq = us_ref[q]
        gb = g_ref[...]
        dw = _gather_blocks(_dot_tn(usq, gb), ms_ref[...], rs_ref[...])
        dv = _gather_blocks(_dot_tn(dys_ref[q], xp_ref[...].astype(BF16)), ms_ref[...], rs_ref[...])
        for gi, st, rows in rows_of:
            dw_ref[gi, st] = dw[rows]
            dv_ref[gi, st] = dv[rows]
        acc[...] = _dot_nt(gb, w_ref[...])
        for m in range(PAIRS):

            @pl.when(q + m < PAIRS)
            def _():
                dyr = dys_ref[jnp.minimum(q + m, PAIRS - 1)]
                acc[...] += _dot_nt(dyr, k_ref[m])
                dkf[m] += _dot_tn(usq, dyr)

        du2[q] = acc[...]

        @pl.when(q == PAIRS - 1)
        def _():
            for p in range(PAIRS):
                duf[pl.ds(2 * p, nc, stride=CHUNK), :] = du2[p, :, 0:LANES]
                duf[pl.ds(2 * p + 1, nc, stride=CHUNK), :] = du2[p, :, LANES : 2 * LANES]
                dk = _gather_blocks(dkf[p], mk_ref[...], rk_ref[...])
                for gi, st, rows in rows_of:
                    dk_ref[gi, p, st] = dk[rows]
            du_ref[...] = duf[...].astype(BF16)

    def per_tile(rows, cols):
        return pl.BlockSpec((None, PAIRS, rows, cols), lambda t, q: (t, 0, 0, 0))

    def per_pair(rows, cols):
        return pl.BlockSpec((None, None, rows, cols), lambda t, q: (t, q, 0, 0))

    def whole(a):
        return pl.BlockSpec(a.shape, lambda t, q: (0, 0))

    return _call_riding(
        body,
        exs,
        args=(us, dys, k2, w2, xprev, gst, rep_k, mask_k, rep_s, mask_s),
        grid=(nt, PAIRS),
        in_specs=[
            per_tile(nc, nk),
            per_tile(nc, nk),
            per_tile(nk, nk),
            per_pair(nk, ns),
            pl.BlockSpec((None, nc, ns), lambda t, q: (t, 0, 0)),
            pl.BlockSpec((None, nc, ns), lambda t, q: (t, 0, 0)),
            whole(rep_k),
            whole(mask_k),
            whole(rep_s),
            whole(mask_s),
        ],
        out_specs=[
            pl.BlockSpec((s, LANES), lambda t, q: (0, t)),
            pl.BlockSpec((gg, PAIRS, 2, SSM_GROUP, LANES), lambda t, q: (t, 0, 0, 0, 0)),
            pl.BlockSpec((gg, None, 2, SSM_GROUP, LANES), lambda t, q: (t, q, 0, 0, 0)),
            pl.BlockSpec((gg, None, 2, SSM_GROUP, LANES), lambda t, q: (t, q, 0, 0, 0)),
        ],
        out_shape=[
            jax.ShapeDtypeStruct((s, width), BF16),
            jax.ShapeDtypeStruct((nt * gg, PAIRS, 2, SSM_GROUP, LANES), F32),
            jax.ShapeDtypeStruct((nt * gg, PAIRS, 2, SSM_GROUP, LANES), F32),
            jax.ShapeDtypeStruct((nt * gg, PAIRS, 2, SSM_GROUP, LANES), F32),
        ],
        scratch_shapes=[
            pltpu.VMEM((nc, nk), F32),
            pltpu.VMEM((PAIRS, nc, nk), F32),
            pltpu.VMEM((s, LANES), F32),
            pltpu.VMEM((PAIRS, nk, nk), F32),
        ],
        compiler_params=_params(2),
        name="ssm_bwd_main",
    )


def _sgu_bwd(z, dmix, dz, lng, lnb, wt, wtt, biasb, go, rb, exs):
    s = z.shape[0]
    w = lng.shape[-1]
    heads = wt.shape[0]
    ncol = (LANES, LANES)

    def body(zu_ref, zv_ref, dm_ref, dz_any, lng_ref, lnb_ref, wt_ref, wtt_ref, b_ref, go_ref,
             dz_ref, dw_ref, db_ref, dlg_ref, dlb_ref, dgo_ref, dzv):
        del dz_any
        i = pl.program_id(0)
        p = pl.program_id(1)

        @pl.when(jnp.logical_and(i == 0, p == 0))
        def _():
            dw_ref[...] = jnp.zeros_like(dw_ref)
            db_ref[...] = jnp.zeros_like(db_ref)
            dlg_ref[...] = jnp.zeros_like(dlg_ref)
            dlb_ref[...] = jnp.zeros_like(dlb_ref)
            dgo_ref[...] = jnp.zeros_like(dgo_ref)

        @pl.when(p == 0)
        def _():
            lng_v = lng_ref[...]
            for ck in range(rb // SGU_CHUNK):
                rows = slice(ck * SGU_CHUNK, (ck + 1) * SGU_CHUNK)
                zu = zu_ref[rows, :]
                zv = zv_ref[rows, :]
                u, vhat, rstd, vb, mix, u_grad, v_grad = _sgu_parts(zu, zv, lng_v, lnb_ref[...], wt_ref, b_ref)
                y = u * mix
                dy, dgt = _rms_bwd(dm_ref[rows, :], y, _rms(y), go_ref[...])
                dgo_ref[...] += jnp.sum(dgt, axis=0, keepdims=True)
                du = dy * mix
                dmx = dy * u
                dmb = dmx.astype(BF16)
                dvl = []
                for h in range(heads):
                    cols = slice(h * LANES, (h + 1) * LANES)
                    db_ref[h] += jnp.broadcast_to(jnp.sum(dmx[:, cols], axis=-1, keepdims=True), ncol)
                    dw_ref[h] += _dot_nt(dmb[:, cols], vb[:, cols])
                    dvl.append(_dot(wtt_ref[h], dmb[:, cols]))
                dvln = jnp.concatenate(dvl, axis=1)
                dlg_ref[...] += jnp.sum(dvln * vhat, axis=0, keepdims=True)
                dlb_ref[...] += jnp.sum(dvln, axis=0, keepdims=True)
                dvh = dvln * lng_v
                dv = rstd * (
                    dvh
                    - jnp.mean(dvh, axis=-1, keepdims=True)
                    - vhat * jnp.mean(dvh * vhat, axis=-1, keepdims=True)
                )
                dz_ref[rows, :] = (du * u_grad).astype(BF16)
                dzv[rows, :] = (dv * v_grad).astype(BF16)

        @pl.when(p == 1)
        def _():
            dz_ref[...] = dzv[...]

    return _call_riding(
        body,
        exs,
        args=(z, z, dmix, dz, lng, lnb, wt, wtt, biasb, go),
        grid=(s // rb, 2),
        in_specs=[
            pl.BlockSpec((rb, w), lambda i, p: (i, 1)),
            pl.BlockSpec((rb, w), lambda i, p: (i, 2)),
            pl.BlockSpec((rb, w), lambda i, p: (i, 1)),
            pl.BlockSpec(memory_space=pl.ANY),
            pl.BlockSpec((1, w), lambda i, p: (0, 0)),
            pl.BlockSpec((1, w), lambda i, p: (0, 0)),
            pl.BlockSpec((heads, SGU_CHUNK, SGU_CHUNK), lambda i, p: (0, 0, 0)),
            pl.BlockSpec((heads, SGU_CHUNK, SGU_CHUNK), lambda i, p: (0, 0, 0)),
            pl.BlockSpec((heads, SGU_CHUNK, LANES), lambda i, p: (0, 0, 0)),
            pl.BlockSpec((1, w), lambda i, p: (0, 0)),
        ],
        out_specs=[
            pl.BlockSpec((rb, w), lambda i, p: (i, 1 + p)),
            pl.BlockSpec((heads, SGU_CHUNK, SGU_CHUNK), lambda i, p: (0, 0, 0)),
            pl.BlockSpec((heads, SGU_CHUNK, LANES), lambda i, p: (0, 0, 0)),
            pl.BlockSpec((1, w), lambda i, p: (0, 0)),
            pl.BlockSpec((1, w), lambda i, p: (0, 0)),
            pl.BlockSpec((1, w), lambda i, p: (0, 0)),
        ],
        out_shape=[
            jax.ShapeDtypeStruct(dz.shape, dz.dtype),
            jax.ShapeDtypeStruct((heads, SGU_CHUNK, SGU_CHUNK), F32),
            jax.ShapeDtypeStruct((heads, SGU_CHUNK, LANES), F32),
            jax.ShapeDtypeStruct((1, w), F32),
            jax.ShapeDtypeStruct((1, w), F32),
            jax.ShapeDtypeStruct((1, w), F32),
        ],
        scratch_shapes=[pltpu.VMEM((rb, w), BF16)],
        input_output_aliases={3: 0},
        compiler_params=_params(2),
        name="sgu_bwd",
    )


def _in_proj_bwd(dz, w4, x, dres, g, tm, exs, tiles, into, name):
    s, d = x.shape
    nj, _, cw = w4.shape
    first, stop = tiles

    def body(dz_ref, w_ref, x_ref, dr_ref, g_ref, *rest):
        dx_ref, dg_ref, acc = rest[-3:]
        i = pl.program_id(0)
        j = pl.program_id(1)

        @pl.when(j == 0)
        def _():
            acc[...] = jnp.zeros_like(acc)

        @pl.when(jnp.logical_and(i == 0, j == 0))
        def _():
            dg_ref[...] = jnp.zeros_like(dg_ref)

        acc[...] += _dot_nt(dz_ref[...], w_ref[...])

        @pl.when(j == nj - 1)
        def _():
            xv = x_ref[...]
            dx, dgt = _rms_bwd(acc[...], xv, _rms(xv), g_ref[...])
            dx_ref[...] = dr_ref[...] + dx
            dg_ref[...] += jnp.sum(dgt, axis=0, keepdims=True)

    row = pl.BlockSpec((tm, d), lambda i, j: (i + first, 0))
    vec = pl.BlockSpec((1, d), lambda i, j: (0, 0))
    kept = [] if into is None else [into]
    return _call_riding(
        body,
        exs,
        args=(dz, w4, x, dres, g, *kept),
        scratch_shapes=[pltpu.VMEM((tm, d), F32)],
        grid=(stop - first, nj),
        in_specs=[
            pl.BlockSpec((tm, cw), lambda i, j: (i + first, j)),
            pl.BlockSpec((None, d, cw), lambda i, j: (j, 0, 0)),
            row,
            row,
            vec,
        ]
        + [_ANY] * len(kept),
        out_specs=[row, vec],
        out_shape=[jax.ShapeDtypeStruct((s, d), F32), jax.ShapeDtypeStruct((1, d), F32)],
        input_output_aliases={5: 0} if kept else {},
        compiler_params=_params(2),
        name=name,
    )


_ANY = pl.BlockSpec(memory_space=pl.ANY)


def _position():
    x, y, c = lax.axis_index("x"), lax.axis_index("y"), lax.axis_index("c")
    return x, y, c, [(1 - x, y), (x, 1 - y), (1 - x, 1 - y)]


def _remote(src, dst, send_sems, recv_sems, k, to):
    return pltpu.make_async_remote_copy(
        src_ref=src, dst_ref=dst, send_sem=send_sems.at[k], recv_sem=recv_sems.at[k], device_id=to, device_id_type=MESH
    )


class _Riding:
    def __init__(self, srcs, out_shapes, n_sems, start, finish, kept=()):
        self.srcs, self.out_shapes, self.n_sems, self.start, self.finish = srcs, out_shapes, n_sems, start, finish
        self.kept = kept
        self.relay = None


def _call_riding(body, exs, *, grid, in_specs, out_specs, out_shape, scratch_shapes, args, **kw):
    n_in, n_out, n_scr = len(in_specs), len(out_specs), len(scratch_shapes)
    spec = dict(grid=grid, in_specs=list(in_specs), out_specs=list(out_specs), out_shape=list(out_shape))
    if not exs:
        return _call(body, scratch_shapes=list(scratch_shapes), **spec, **kw)(*args), []
    srcs = [a for ex in exs for a in ex.srcs]
    lands = [a for ex in exs for a in ex.out_shapes]
    xi, xo = len(srcs), len(lands)

    def fused(*refs):
        cin, xin = refs[:n_in], refs[n_in : n_in + xi]
        o = n_in + xi
        cout, xout = refs[o : o + n_out], refs[o + n_out : o + n_out + xo]
        scr = refs[o + n_out + xo :]
        ids = [pl.program_id(a) for a in range(len(grid))]
        first = functools.reduce(jnp.logical_and, [i == 0 for i in ids])
        last = functools.reduce(jnp.logical_and, [i == n - 1 for i, n in zip(ids, grid)])

        step = functools.reduce(lambda acc, a: acc * grid[a] + ids[a], range(1, len(grid)), ids[0])

        def each(half):
            si = so = 0
            for e, ex in enumerate(exs):
                ni, no = len(ex.srcs), len(ex.out_shapes)
                sems = scr[n_scr + 2 * e : n_scr + 2 * e + 2]
                if getattr(ex, half) is not None:
                    getattr(ex, half)(xin[si : si + ni], xout[so : so + no], *sems)
                si, so = si + ni, so + no

        @pl.when(first)
        def _():
            each("start")

        @pl.when(step == (3 * math.prod(grid)) // 4)
        def _():
            each("relay")

        body(*cin, *cout, *scr[:n_scr])

        @pl.when(last)
        def _():
            each("finish")

    sems = [pltpu.SemaphoreType.DMA((ex.n_sems,)) for ex in exs for _ in range(2)]
    aliases = dict(kw.pop("input_output_aliases", {}))
    si = so = 0
    for ex in exs:
        for a, b in ex.kept:
            aliases[n_in + si + a] = n_out + so + b
        si, so = si + len(ex.srcs), so + len(ex.out_shapes)
    kw["input_output_aliases"] = aliases
    spec["in_specs"] += [_ANY] * xi
    spec["out_specs"] += [_ANY] * xo
    spec["out_shape"] += lands
    outs = _call(fused, scratch_shapes=list(scratch_shapes) + sems, **spec, **kw)(*args, *srcs)
    return outs[:n_out], outs[n_out:]


def _gather_one(shard, peers=(0, 1, 2), into=None, early_relay=False):
    def own_copies(ins, outs, send, recv):
        x, y, c, chips = _position()
        me = 2 * x + y
        sib = (x, y, 1 - c)
        cps = [] if into is not None else [_remote(ins[0].at[c], outs[0].at[me, c], send, recv, 0, sib)]
        cps += [_remote(ins[0].at[c], outs[0].at[me, c], send, recv, 1 + k, (*chips[k], c)) for k in peers]
        return cps, (c, me, sib, chips)

    def start(ins, outs, send, recv):
        for cp in own_copies(ins, outs, send, recv)[0]:
            cp.start()

    def passed_on(outs, send, recv, c, sib, chips):
        return [_remote(outs[0].at[2 * chips[k][0] + chips[k][1], c], outs[0].at[2 * chips[k][0] + chips[k][1], c],
                        send, recv, 4 + k, sib) for k in peers]

    def relay(ins, outs, send, recv):
        x, y, c, chips = _position()
        for k, cp in zip(peers, passed_on(outs, send, recv, c, (x, y, 1 - c), chips)):
            px, py = chips[k]
            _remote(ins[0].at[c], outs[0].at[2 * px + py, c], send, recv, 1 + k, (px, py, c)).wait_recv()
            cp.start()

    def finish(ins, outs, send, recv):
        if not early_relay:
            relay(ins, outs, send, recv)
        cps, (c, me, sib, chips) = own_copies(ins, outs, send, recv)
        if into is None:
            _remote(ins[0].at[1 - c], outs[0].at[me, 1 - c], send, recv, 0, sib).wait_recv()
        for k in peers:
            px, py = chips[k]
            theirs = outs[0].at[2 * px + py, 1 - c]
            _remote(theirs, theirs, send, recv, 4 + k, sib).wait_recv()
        for cp in cps + passed_on(outs, send, recv, c, sib, chips):
            cp.wait_send()

    land = jax.ShapeDtypeStruct((N_CHIPS,) + shard.shape, shard.dtype)
    rider = _Riding([shard] + ([] if into is None else [into]), [land], 7, start, finish, [] if into is None else [(1, 0)])
    rider.relay = relay if early_relay else None
    return rider


def _to_owner_one(p, peers=(0, 1, 2), into=None):
    def copies(ins, outs, send, recv):
        x, y, c, chips = _position()
        return [_remote(ins[0].at[2 * chips[k][0] + chips[k][1]], outs[0].at[k], send, recv, k, (*chips[k], c)) for k in peers]

    def start(ins, outs, send, recv):
        for cp in copies(ins, outs, send, recv):
            cp.start()

    def finish(ins, outs, send, recv):
        cps = copies(ins, outs, send, recv)
        for cp in cps:
            cp.wait_recv()
        for cp in cps:
            cp.wait_send()

    land = jax.ShapeDtypeStruct((3,) + p.shape[1:], p.dtype)
    return _Riding([p] + ([] if into is None else [into]), [land], 3, start, finish, [] if into is None else [(1, 0)])


def _to_sibling_one(g):
    def copies(ins, outs, send, recv):
        x, y, c, _ = _position()
        return [_remote(ins[0].at[j, 1 - c], outs[0].at[j], send, recv, j, (x, y, 1 - c)) for j in range(N_CHIPS)]

    def start(ins, outs, send, recv):
        for cp in copies(ins, outs, send, recv):
            cp.start()

    def finish(ins, outs, send, recv):
        cps = copies(ins, outs, send, recv)
        for cp in cps:
            cp.wait_recv()
        for cp in cps:
            cp.wait_send()

    return _Riding([g], [jax.ShapeDtypeStruct((g.shape[0],) + g.shape[2:], g.dtype)], N_CHIPS, start, finish)


def _gather_chips(arrs, name):
    n = len(arrs)

    def body(*refs):
        ins, outs = refs[:n], refs[n : 2 * n]
        send_sems, recv_sems, local_sems = refs[2 * n :]
        x, y, c, chips = _position()
        me = 2 * x + y
        local = [pltpu.make_async_copy(ins[i], outs[i].at[me], local_sems.at[i]) for i in range(n)]
        for cp in local:
            cp.start()
        sends = []
        for k, (px, py) in enumerate(chips):
            for i in range(n):
                sends.append(_remote(ins[i], outs[i].at[me], send_sems, recv_sems, k * n + i, (px, py, c)))
                sends[-1].start()
        for k, (px, py) in enumerate(chips):
            for i in range(n):
                _remote(ins[i], outs[i].at[2 * px + py], send_sems, recv_sems, k * n + i, (px, py, c)).wait_recv()
        for cp in sends:
            cp.wait_send()
        for cp in local:
            cp.wait()

    return _call(
        body,
        in_specs=[_ANY] * n,
        out_specs=[_ANY] * n,
        out_shape=[jax.ShapeDtypeStruct((N_CHIPS,) + a.shape, a.dtype) for a in arrs],
        scratch_shapes=[
            pltpu.SemaphoreType.DMA((3 * n,)),
            pltpu.SemaphoreType.DMA((3 * n,)),
            pltpu.SemaphoreType.DMA((n,)),
        ],
        name=name,
    )(*arrs)


def _chip_exchange(ps, name):
    n = len(ps)

    def body(*refs):
        ins, outs = refs[:n], refs[n : 2 * n]
        send_sems, recv_sems = refs[2 * n :]
        x, y, c, chips = _position()
        sends = []
        for k, (px, py) in enumerate(chips):
            for i in range(n):
                sends.append(_remote(ins[i].at[2 * px + py], outs[i].at[k], send_sems, recv_sems, k * n + i, (px, py, c)))
                sends[-1].start()
        for cp in sends:
            cp.wait_recv()
        for cp in sends:
            cp.wait_send()

    return _call(
        body,
        in_specs=[_ANY] * n,
        out_specs=[_ANY] * n,
        out_shape=[jax.ShapeDtypeStruct((3,) + p.shape[1:], p.dtype) for p in ps],
        scratch_shapes=[pltpu.SemaphoreType.DMA((3 * n,)), pltpu.SemaphoreType.DMA((3 * n,))],
        name=name,
    )(*ps)


def _pair_exchange(gs, name):
    n = len(gs)

    def body(*refs):
        ins, outs = refs[:n], refs[n : 2 * n]
        send_sems, recv_sems = refs[2 * n :]
        x, y, c, _ = _position()
        sib = (x, y, 1 - c)
        cps = []
        for i in range(n):
            for j in range(N_CHIPS):
                cps.append(_remote(ins[i].at[j, 1 - c], outs[i].at[j], send_sems, recv_sems, i * N_CHIPS + j, sib))
                cps[-1].start()
        for cp in cps:
            cp.wait_recv()
        for cp in cps:
            cp.wait_send()

    return _call(
        body,
        in_specs=[_ANY] * n,
        out_specs=[_ANY] * n,
        out_shape=[jax.ShapeDtypeStruct((g.shape[0],) + g.shape[2:], g.dtype) for g in gs],
        scratch_shapes=[pltpu.SemaphoreType.DMA((n * N_CHIPS,)), pltpu.SemaphoreType.DMA((n * N_CHIPS,))],
        name=name,
    )(*gs)


def _place_own(shard, gathered, pos, name):
    _, hr, cols = shard.shape
    tr = _row_tile(hr, 512)

    def body(pos_ref, s_ref, g_any, o_ref):
        del pos_ref, g_any
        o_ref[...] = s_ref[...]

    return _call(
        body,
        grid_spec=pltpu.PrefetchScalarGridSpec(
            num_scalar_prefetch=1,
            grid=(hr // tr,),
            in_specs=[pl.BlockSpec((None, tr, cols), lambda i, p: (p[1], i, 0)), _ANY],
            out_specs=pl.BlockSpec((None, None, tr, cols), lambda i, p: (p[0], p[1], i, 0)),
        ),
        out_shape=jax.ShapeDtypeStruct(gathered.shape, gathered.dtype),
        input_output_aliases={2: 0},
        compiler_params=_params(1),
        name=name,
    )(pos, shard, gathered)


def _pair_share(fs, name):
    n = len(fs)

    def body(*refs):
        ins, outs = refs[:n], refs[n : 2 * n]
        send_sems, recv_sems = refs[2 * n :]
        x, y, c, _ = _position()
        sib = (x, y, 1 - c)
        sends = [_remote(ins[i].at[c], outs[i].at[c], send_sems, recv_sems, i, sib) for i in range(n)]
        for cp in sends:
            cp.start()
        for i in range(n):
            _remote(ins[i].at[1 - c], outs[i].at[1 - c], send_sems, recv_sems, i, sib).wait_recv()
        for cp in sends:
            cp.wait_send()

    return _call(
        body,
        in_specs=[_ANY] * n,
        out_specs=[_ANY] * n,
        out_shape=[jax.ShapeDtypeStruct(f.shape, f.dtype) for f in fs],
        input_output_aliases={i: i for i in range(n)},
        scratch_shapes=[pltpu.SemaphoreType.DMA((n,)), pltpu.SemaphoreType.DMA((n,))],
        name=name,
    )(*fs)


def _pair_sum(g, r, core, name):
    nj, _, hr, cols = g.shape
    tr = _row_tile(hr, 256)

    def body(c_ref, g_ref, r_ref, o_ref):
        del c_ref
        o_ref[...] = (g_ref[...].astype(F32) + r_ref[...].astype(F32)).astype(o_ref.dtype)

    return _call(
        body,
        grid_spec=pltpu.PrefetchScalarGridSpec(
            num_scalar_prefetch=1,
            grid=(nj, hr // tr),
            in_specs=[
                pl.BlockSpec((None, None, tr, cols), lambda j, i, c: (j, c[0], i, 0)),
                pl.BlockSpec((None, tr, cols), lambda j, i, c: (j, i, 0)),
            ],
            out_specs=pl.BlockSpec((None, tr, cols), lambda j, i, c: (j, i, 0)),
        ),
        out_shape=jax.ShapeDtypeStruct((nj, hr, cols), g.dtype),
        compiler_params=_params(2),
        name=name,
    )(core, g, r)


def _chip_sum(p, q, pos, name):
    nq, hr, cols = q.shape
    tr = _row_tile(hr, 256)

    def body(pos_ref, p_ref, *refs):
        del pos_ref
        o_ref = refs[nq]
        tot = p_ref[...].astype(F32)
        for k in range(nq):
            tot = tot + refs[k][...].astype(F32)
        o_ref[...] = tot

    return _call(
        body,
        grid_spec=pltpu.PrefetchScalarGridSpec(
            num_scalar_prefetch=1,
            grid=(hr // tr,),
            in_specs=[pl.BlockSpec((None, tr, cols), lambda i, s: (s[0], i, 0))]
            + [pl.BlockSpec((None, tr, cols), functools.partial(lambda k, i, s: (k, i, 0), k)) for k in range(nq)],
            out_specs=pl.BlockSpec((None, tr, cols), lambda i, s: (s[1], i, 0)),
        ),
        out_shape=jax.ShapeDtypeStruct((2, hr, cols), F32),
        compiler_params=_params(1),
        name=name,
    )(pos, p, *([q] * nq))


def _adamw(w, g, m, v, name):
    rows, cols = w.shape
    tr = _row_tile(rows, max(8, (2**18) // cols), mult=8)
    c1 = 1.0 - ADAM_B1**ADAM_STEP
    c2 = 1.0 - ADAM_B2**ADAM_STEP

    def body(w_ref, g_ref, m_ref, v_ref, gk_ref, d_ref, nm_ref, nv_ref):
        gv = g_ref[...]
        gk_ref[...] = gv
        nm = ADAM_B1 * m_ref[...] + (1.0 - ADAM_B1) * gv
        nv = ADAM_B2 * v_ref[...] + (1.0 - ADAM_B2) * (gv * gv)
        nm_ref[...] = nm
        nv_ref[...] = nv
        d_ref[...] = -ADAM_LR * ((nm / c1) / (jnp.sqrt(nv / c2) + ADAM_EPS) + ADAM_WD * w_ref[...])

    spec = pl.BlockSpec((tr, cols), lambda i: (i, 0))
    sds = jax.ShapeDtypeStruct((rows, cols), F32)
    return _call(
        body,
        grid=(rows // tr,),
        in_specs=[spec] * 4,
        out_specs=[spec] * 4,
        out_shape=[sds] * 4,
        compiler_params=_params(1),
        name=name,
    )(w, g, m, v)


_TILE_ELEMS = 8 * LANES
_FLAT_ROW_MULT = 8 * 2 * N_CHIPS


def _flat_rows(shape):
    n = math.prod(shape)
    return (n + _TILE_ELEMS - 1) // _TILE_ELEMS * 8


def _pack(arrs):
    parts = []
    for a in arrs:
        rows = _flat_rows(a.shape)
        flat = a.reshape(-1).astype(F32)
        flat = jnp.pad(flat, (0, rows * LANES - flat.shape[0]))
        parts.append(flat.reshape(rows, LANES))
    total = sum(p.shape[0] for p in parts)
    pad = -total % _FLAT_ROW_MULT
    if pad:
        parts.append(jnp.zeros((pad, LANES), F32))
    return jnp.concatenate(parts, axis=0)


def _unpack(flat, shapes):
    out, row = [], 0
    for shp in shapes:
        rows = _flat_rows(shp)
        out.append(flat[row : row + rows].reshape(-1)[: math.prod(shp)].reshape(shp))
        row += rows
    return out


def kernel(x, norm_mix_g, w_in, ssm_a_re, ssm_a_im, ssm_b_re, ssm_b_im, ssm_c_re, ssm_c_im, ssm_d, ssm_log_dt, ssm_glu_w, ssm_glu_b, sgu_ln_g, sgu_ln_b, sgu_w, sgu_b, out_norm_ssm_g, out_norm_sgu_g, w_out, norm_mlp_g, w_up, w_down, norm_final_g, loss_target, m_norm_mix_g, m_w_in, m_ssm_a_re, m_ssm_a_im, m_ssm_b_re, m_ssm_b_im, m_ssm_c_re, m_ssm_c_im, m_ssm_d, m_ssm_log_dt, m_ssm_glu_w, m_ssm_glu_b, m_sgu_ln_g, m_sgu_ln_b, m_sgu_w, m_sgu_b, m_out_norm_ssm_g, m_out_norm_sgu_g, m_w_out, m_norm_mlp_g, m_w_up, m_w_down, m_norm_final_g, v_norm_mix_g, v_w_in, v_ssm_a_re, v_ssm_a_im, v_ssm_b_re, v_ssm_b_im, v_ssm_c_re, v_ssm_c_im, v_ssm_d, v_ssm_log_dt, v_ssm_glu_w, v_ssm_glu_b, v_sgu_ln_g, v_sgu_ln_b, v_sgu_w, v_sgu_b, v_out_norm_ssm_g, v_out_norm_sgu_g, v_w_out, v_norm_mlp_g, v_w_up, v_w_down, v_norm_final_g):
    weights = dict(norm_mix_g=norm_mix_g, w_in=w_in, ssm_a_re=ssm_a_re, ssm_a_im=ssm_a_im, ssm_b_re=ssm_b_re, ssm_b_im=ssm_b_im, ssm_c_re=ssm_c_re, ssm_c_im=ssm_c_im, ssm_d=ssm_d, ssm_log_dt=ssm_log_dt, ssm_glu_w=ssm_glu_w, ssm_glu_b=ssm_glu_b, sgu_ln_g=sgu_ln_g, sgu_ln_b=sgu_ln_b, sgu_w=sgu_w, sgu_b=sgu_b, out_norm_ssm_g=out_norm_ssm_g, out_norm_sgu_g=out_norm_sgu_g, w_out=w_out, norm_mlp_g=norm_mlp_g, w_up=w_up, w_down=w_down, norm_final_g=norm_final_g)
    mom_m = dict(norm_mix_g=m_norm_mix_g, w_in=m_w_in, ssm_a_re=m_ssm_a_re, ssm_a_im=m_ssm_a_im, ssm_b_re=m_ssm_b_re, ssm_b_im=m_ssm_b_im, ssm_c_re=m_ssm_c_re, ssm_c_im=m_ssm_c_im, ssm_d=m_ssm_d, ssm_log_dt=m_ssm_log_dt, ssm_glu_w=m_ssm_glu_w, ssm_glu_b=m_ssm_glu_b, sgu_ln_g=m_sgu_ln_g, sgu_ln_b=m_sgu_ln_b, sgu_w=m_sgu_w, sgu_b=m_sgu_b, out_norm_ssm_g=m_out_norm_ssm_g, out_norm_sgu_g=m_out_norm_sgu_g, w_out=m_w_out, norm_mlp_g=m_norm_mlp_g, w_up=m_w_up, w_down=m_w_down, norm_final_g=m_norm_final_g)
    mom_v = dict(norm_mix_g=v_norm_mix_g, w_in=v_w_in, ssm_a_re=v_ssm_a_re, ssm_a_im=v_ssm_a_im, ssm_b_re=v_ssm_b_re, ssm_b_im=v_ssm_b_im, ssm_c_re=v_ssm_c_re, ssm_c_im=v_ssm_c_im, ssm_d=v_ssm_d, ssm_log_dt=v_ssm_log_dt, ssm_glu_w=v_ssm_glu_w, ssm_glu_b=v_ssm_glu_b, sgu_ln_g=v_sgu_ln_g, sgu_ln_b=v_sgu_ln_b, sgu_w=v_sgu_w, sgu_b=v_sgu_b, out_norm_ssm_g=v_out_norm_ssm_g, out_norm_sgu_g=v_out_norm_sgu_g, w_out=v_w_out, norm_mlp_g=v_norm_mlp_g, w_up=v_w_up, w_down=v_w_down, norm_final_g=v_norm_final_g)
    names = list(weights)
    large = ["w_in", "ssm_glu_w", "w_out", "w_up", "w_down"]
    small = [n for n in names if n not in large]

    s, d = x.shape[1], x.shape[2]
    xs = x.reshape(s, d)
    target = loss_target.reshape(s, d)
    width = ssm_glu_w.shape[-1]
    ff = w_down.shape[1] * N_CHIPS
    tm = min(512, s)
    core = lax.axis_index("c").astype(jnp.int32).reshape(1)
    chip = (2 * lax.axis_index("x") + lax.axis_index("y")).astype(jnp.int32).reshape(1)
    pos = jnp.concatenate([chip, core])

    shards = [w[0].astype(BF16).reshape(2, w.shape[1] // 2, w.shape[2]) for w in (w_in, ssm_glu_w, w_out, w_up, w_down)]
    ssm_args = (ssm_a_re[0], ssm_a_im[0], ssm_b_re[0], ssm_b_im[0], ssm_c_re[0], ssm_c_im[0], ssm_d[0], ssm_log_dt[0])
    (k2c, w2c, v2c, al), ssm_vjp = jax.vjp(_ssm_mats, *ssm_args)
    consts = _spread_consts(ssm_b_re.shape[-1], ssm_a_re.shape[-1])
    (k2b, w2b, v2b), (w_in_g,) = _ssm_spread(k2c, w2c, v2c, consts, [_gather_one(shards[0])])
    w_in4 = _place_own(shards[0], w_in_g, pos, "place_w_in").reshape(N_CHIPS, d, w_in.shape[2])
    causal = jnp.tril(jnp.ones((SGU_CHUNK, SGU_CHUNK), dtype=bool))
    wt = jnp.where(causal[None], sgu_w[0], 0.0)
    wtb = wt.astype(BF16)
    wttb = jnp.swapaxes(wt, 1, 2).astype(BF16)
    heads = sgu_w.shape[1]
    biasb = jnp.broadcast_to(sgu_b[0][:, :, None], (heads, SGU_CHUNK, LANES))

    tm_big = min(1024, s)
    (z, h1b), (glu_g, out_g, up_g) = _in_proj(
        xs, norm_mix_g, w_in4, tm_big,
        [_gather_one(shards[1]), _gather_one(shards[2]), _gather_one(shards[3], peers=(0, 1))])
    wg_full = _place_own(shards[1], glu_g, pos, "place_glu_w").reshape(width, width)
    w_out_full = _place_own(shards[2], out_g, pos, "place_w_out").reshape(d, d)
    (y_pre, xprev, us), (up_g,) = _ssm_fwd(z, k2b, w2b, v2b, al, [_gather_one(shards[3], peers=(2,), into=up_g)])
    up4 = _place_own(shards[3], up_g, pos, "place_w_up").reshape(N_CHIPS, d, w_up.shape[2])
    mixed = _glu_fwd(y_pre, wg_full, ssm_glu_b, out_norm_ssm_g, d, tm)
    mixed = _sgu_fwd(z, mixed, sgu_ln_g, sgu_ln_b, wtb, biasb, out_norm_sgu_g, tm)
    x1, h2b = _out_proj(xs, mixed, w_out_full, norm_mlp_g, tm)
    tf = min(1024, up4.shape[-1])
    (up,), (down_g,) = _mlp_up(
        h2b, up4, tm_big, min(2048, up4.shape[-1]), [_gather_one(shards[4], early_relay=True)])
    w_down_full = _place_own(shards[4], down_g, pos, "place_w_down").reshape(ff, d)
    loss_part, dx2, dx2b, d_norm_final = _mlp_down_loss(
        x1, up, w_down_full, target, norm_final_g.reshape(1, d), tm, tf)

    dup, dh2 = _mlp_bwd(dx2b, up, up4, w_down_full, tm_big, min(1024, up4.shape[-1]))
    dx1, dx1b, d_norm_mlp = _norm_bwd(dh2, x1, dx2, norm_mlp_g, min(256, s), "norm_mlp_bwd")
    hr_big = d // 2
    tkk = min(2048, s)
    g_down = _grad_w(
        up, dx2b, (N_CHIPS, 2, hr_big, d),
        lambda i, j: (i // (2 * (hr_big // min(1024, hr_big))), (i // (hr_big // min(1024, hr_big))) % 2,
                      i % (hr_big // min(1024, hr_big)), j),
        min(1024, hr_big), d, tkk, True, "grad_w_down")
    tn_up = min(2048, up4.shape[-1])
    per_up = up4.shape[-1] // tn_up
    g_up, sib_down = _grad_w(
        h2b, dup, (N_CHIPS, 2, hr_big, up4.shape[-1]),
        lambda i, j: (j // per_up, i // (hr_big // min(1024, hr_big)), i % (hr_big // min(1024, hr_big)), j % per_up),
        min(1024, hr_big), tn_up, tkk, False, "grad_w_up", exs=[_to_sibling_one(g_down)])
    pair_down = _pair_sum(g_down, sib_down, core, "pair_sum_w_down")
    dmix = _out_proj_bwd(dx1b, w_out_full, tm)
    hr_out = d // (2 * N_CHIPS)
    g_out, sib_up = _grad_w(
        mixed, dx1b, (1, 1, d, d), lambda i, j: (0, 0, i, j), min(1024, d), d, tkk, False, "grad_w_out",
        exs=[_to_sibling_one(g_up)])
    g_out = g_out.reshape(N_CHIPS, 2, hr_out, d)
    pair_up = _pair_sum(g_up, sib_up, core, "pair_sum_w_up")
    (dy_pre, g_glu, d_glu_b, d_norm_ssm), (sib_out,) = _glu_bwd(
        y_pre, dmix, wg_full, ssm_glu_b, out_norm_ssm_g, tm, [_to_sibling_one(g_out)])
    pair_out = _pair_sum(g_out, sib_out, core, "pair_sum_w_out")
    dys, gst, d_al = _ssm_bwd_state(dy_pre, v2b, al, xprev)
    (dz, d_k2, d_w2, d_v2), (chips_down,) = _ssm_bwd_main(
        us, dys, k2b, w2b, xprev, gst, consts, s, z.shape[1], [_to_owner_one(pair_down)])
    (dz, d_wt, d_bias, d_ln_g, d_ln_b, d_norm_sgu), (chips_up,) = _sgu_bwd(
        z, dmix, dz, sgu_ln_g, sgu_ln_b, wtb, wttb, biasb, out_norm_sgu_g, tm,
        [_to_owner_one(pair_up, peers=(0, 1))])
    cw_in = w_in4.shape[-1]
    g_in, chips_up = _grad_w(
        h1b, dz, (N_CHIPS, 2, hr_big, cw_in),
        lambda i, j: (j, i // (hr_big // min(1024, hr_big)), i % (hr_big // min(1024, hr_big)), 0),
        min(1024, hr_big), 2 * cw_in, tkk, False, "grad_w_in",
        exs=[_to_owner_one(pair_up, peers=(2,), into=chips_up)], chunks=2)
    (sib_in,) = _pair_exchange([g_in], "w_in_to_sibling")
    pair_in = _pair_sum(g_in, sib_in, core, "pair_sum_w_in")
    tm_x = min(512, s // 2)
    n_x = s // tm_x
    (gx_half, dg_a), (chips_in,) = _in_proj_bwd(
        dz, w_in4, xs, dx1, norm_mix_g, tm_x, [_to_owner_one(pair_in)], (0, n_x // 2), None, "in_proj_bwd_a")
    (grad_x, dg_b), (chips_out,) = _in_proj_bwd(
        dz, w_in4, xs, dx1, norm_mix_g, tm_x, [_to_owner_one(pair_out)], (n_x // 2, n_x), gx_half, "in_proj_bwd_b")
    d_norm_mix = dg_a + dg_b

    d_ssm = ssm_vjp((d_k2, d_w2, d_v2, d_al))
    small_grads = dict(
        norm_mix_g=d_norm_mix, ssm_a_re=d_ssm[0], ssm_a_im=d_ssm[1], ssm_b_re=d_ssm[2], ssm_b_im=d_ssm[3],
        ssm_c_re=d_ssm[4], ssm_c_im=d_ssm[5], ssm_d=d_ssm[6], ssm_log_dt=d_ssm[7], ssm_glu_b=d_glu_b,
        sgu_ln_g=d_ln_g, sgu_ln_b=d_ln_b, sgu_w=jnp.where(causal[None], d_wt, 0.0), sgu_b=d_bias[:, :, 0],
        out_norm_ssm_g=d_norm_ssm, out_norm_sgu_g=d_norm_sgu, norm_mlp_g=d_norm_mlp, norm_final_g=d_norm_final)
    flat = _pack([small_grads[n] for n in small])
    hr_small = flat.shape[0] // (2 * N_CHIPS)
    g_small = flat.reshape(N_CHIPS, 2, hr_small, LANES)

    hr_glu = width // (2 * N_CHIPS)
    grads = [g_glu.reshape(N_CHIPS, 2, hr_glu, width), g_small]
    tags = ["glu_w", "small"]
    from_sib = _pair_exchange(grads, "grads_to_sibling")
    pair = [_pair_sum(g, r, core, "pair_sum_" + t) for g, r, t in zip(grads, from_sib, tags)]
    from_chips = _chip_exchange(pair, "grads_to_owner")
    pair = [pair_in, pair[0], pair_out, pair_up, pair_down, pair[1]]
    from_chips = [chips_in, from_chips[0], chips_out, chips_up, chips_down, from_chips[1]]
    tags = ["w_in", "glu_w", "w_out", "w_up", "w_down", "small"]
    halves = [_chip_sum(p, q, pos, "chip_sum_" + t) for p, q, t in zip(pair, from_chips, tags)]
    owned = _pair_share(halves, "grads_to_both_cores")
    (small_all,) = _gather_chips([owned[5]], "gather_small_grads")
    small_flat = small_all.reshape(flat.shape)

    grad_out, delta_out, m_out, v_out = {}, {}, {}, {}
    for n, g in zip(large, owned[:5]):
        shp = weights[n].shape
        g2, dl, nm, nv = _adamw(
            weights[n][0], g.reshape(shp[1], shp[2]), mom_m[n][0], mom_v[n][0], "adamw_" + n)
        grad_out[n], delta_out[n], m_out[n], v_out[n] = g2.reshape(shp), dl.reshape(shp), nm.reshape(shp), nv.reshape(shp)
    shapes = [weights[n].shape for n in small]
    g2, dl, nm, nv = _adamw(
        _pack([weights[n] for n in small]), small_flat, _pack([mom_m[n] for n in small]),
        _pack([mom_v[n] for n in small]), "adamw_small")
    for n, g, a, b, c in zip(small, _unpack(g2, shapes), _unpack(dl, shapes), _unpack(nm, shapes), _unpack(nv, shapes)):
        grad_out[n], delta_out[n], m_out[n], v_out[n] = g, a, b, c

    loss = lax.psum(loss_part[0, 0], ("x", "y", "c"))
    return (loss, grad_x.reshape(x.shape), *[grad_out[n] for n in names], *[delta_out[n] for n in names],
            *[m_out[n] for n in names], *[v_out[n] for n in names])
```

```python
import functools
import math

import numpy as np
import jax
import jax.numpy as jnp
from jax import lax
from jax.experimental import pallas as pl
from jax.experimental.pallas import tpu as pltpu

F32 = jnp.float32
BF16 = jnp.bfloat16
MESH = pl.DeviceIdType.MESH
HIGHEST = lax.Precision.HIGHEST

EPS = 1e-6
ADAM_LR = 0.001
ADAM_B1 = 0.9
ADAM_B2 = 0.999
ADAM_EPS = 1e-08
ADAM_WD = 0.01
ADAM_STEP = 10

N_CHIPS = 4
LANES = 128
SSM_GROUP = 16
SSM_STATE = 64
GROUPS_PER_TILE = LANES // SSM_GROUP
CHUNK = 16
PAIRS = CHUNK // 2
SGU_CHUNK = 128
VMEM_LIMIT = 56 * 2**20


def _params(n_axes, vmem=VMEM_LIMIT):
    return pltpu.CompilerParams(dimension_semantics=("arbitrary",) * n_axes, vmem_limit_bytes=vmem)


def _call(body, **kw):
    return pl.pallas_call(body, **kw)


def _dot(a, b):
    return jnp.dot(a, b, preferred_element_type=F32)


def _dot_nt(a, b):
    return lax.dot_general(a, b, (((1,), (1,)), ((), ())), preferred_element_type=F32)


def _dot_tn(a, b):
    return lax.dot_general(a, b, (((0,), (0,)), ((), ())), preferred_element_type=F32)


_GELU_K = math.sqrt(2.0 / math.pi)
_GELU_C = 0.044715


def _gelu_both(x):
    x2 = x * x
    t = jnp.tanh(x * (_GELU_K + (_GELU_K * _GELU_C) * x2))
    hx = 0.5 * x
    onep = 1.0 + t
    return hx * onep, 0.5 * onep + hx * (1.0 - t * t) * (_GELU_K + (3.0 * _GELU_K * _GELU_C) * x2)


def _gelu(x):
    return _gelu_both(x)[0]


def _sigmoid(x):
    return 1.0 / (1.0 + jnp.exp(-x))


def _rms(x):
    return lax.rsqrt(jnp.mean(x * x, axis=-1, keepdims=True) + EPS)


def _rms_bwd(dy, x, r, g):
    a = dy * g
    dx = r * a - x * (r * r * r) * jnp.mean(a * x, axis=-1, keepdims=True)
    return dx, dy * x * r


def _row_tile(rows, target, mult=16):
    for t in range(min(rows, target), 0, -1):
        if rows % t == 0 and t % mult == 0:
            return t
    return rows


def _ssm_mats(a_re, a_im, b_re, b_im, c_re, c_im, d, log_dt):
    g, p = a_re.shape
    h = b_re.shape[-1]
    nt = g // GROUPS_PER_TILE
    dt = jnp.exp(log_dt)[:, None]
    lr, li = a_re * dt, a_im * dt

    def apow(l, lr, li):
        mag = jnp.exp(lr * l)
        return mag * jnp.cos(li * l), mag * jnp.sin(li * l)

    ar, ai = apow(1.0, lr, li)
    den = a_re * a_re + a_im * a_im
    qr = ((ar - 1.0) * a_re + ai * a_im) / den
    qi = (ai * a_re - (ar - 1.0) * a_im) / den
    bt_re, bt_im = jnp.swapaxes(b_re, 1, 2), jnp.swapaxes(b_im, 1, 2)
    bbr = qr[:, None, :] * bt_re - qi[:, None, :] * bt_im
    bbi = qr[:, None, :] * bt_im + qi[:, None, :] * bt_re
    b_same = jnp.concatenate([bbr, bbi], axis=-1)[:, None]
    b_swap = jnp.concatenate([bbi, bbr], axis=-1)[:, None]
    ls = jnp.arange(CHUNK + 1, dtype=F32)[None, :, None]
    pr, pi = apow(ls, lr[:, None, :], li[:, None, :])
    p_same = jnp.concatenate([pr, pr], axis=-1)[:, :, None, :]
    p_sign = jnp.concatenate([-pi, pi], axis=-1)[:, :, None, :]
    tj = p_same[:, :CHUNK] * b_same + p_sign[:, :CHUNK] * b_swap
    c_conj = jnp.concatenate([c_re, -c_im], axis=-1)
    k = jnp.einsum("goq,gliq->glio", c_conj, tj, precision=HIGHEST)
    k = k.at[:, 0].add(d[:, :, None] * jnp.eye(h, dtype=F32))

    zero = jnp.zeros_like(k[:, :1])
    kz = jnp.concatenate([zero, zero, k], axis=1).reshape(g, PAIRS + 1, 2, h, h)
    even, odd = kz[:, :, 0], kz[:, :, 1]
    fill = jnp.zeros((g, PAIRS, h, LANES - 2 * h), F32)
    row0 = jnp.concatenate([even[:, 1:], odd[:, 1:], fill], axis=-1)
    row1 = jnp.concatenate([odd[:, :-1], even[:, 1:], fill], axis=-1)
    k2c = jnp.stack([row0, row1], axis=2)
    w2c = tj[:, ::-1].reshape(g, PAIRS, 2, h, 2 * p)
    c_cross = jnp.concatenate([-c_im, -c_re], axis=-1)
    p_imag = jnp.concatenate([pi, pi], axis=-1)[:, :, None, :]
    v2c = (c_conj[:, None] * p_same[:, 1:] + c_cross[:, None] * p_imag[:, 1:]).reshape(g, PAIRS, 2, h, 2 * p)
    al = jnp.stack([pr[:, CHUNK].reshape(nt, -1), pi[:, CHUNK].reshape(nt, -1)], axis=1)
    return k2c, w2c, v2c, al


def _spread_consts(h, p):
    gg = GROUPS_PER_TILE
    row_g = (np.arange(2 * gg * h) // h) % gg
    colk = np.arange(2 * gg * h)
    rep_k = np.zeros((LANES, 2 * gg * h), np.float32)
    rep_k[(colk // (gg * h)) * h + colk % h, colk] = 1.0
    mask_k = (row_g[:, None] == ((colk // h) % gg)[None, :]).astype(np.float32)
    cols = np.arange(2 * gg * p)
    rep_s = np.zeros((2 * p, 2 * gg * p), np.float32)
    rep_s[(cols // (gg * p)) * p + cols % p, cols] = 1.0
    mask_s = (row_g[:, None] == ((cols // p) % gg)[None, :]).astype(np.float32)
    return tuple(jnp.asarray(a, BF16) for a in (rep_k, mask_k, rep_s, mask_s))


def _ssm_spread(k2c, w2c, v2c, consts, exs):
    h = k2c.shape[3]
    nt = k2c.shape[0] // GROUPS_PER_TILE
    rep_k, mask_k, rep_s, mask_s = consts
    nk, ns = rep_k.shape[1], rep_s.shape[1]

    def body(k_ref, w_ref, v_ref, rk_ref, mk_ref, rs_ref, ms_ref, ko_ref, wo_ref, vo_ref):
        for q in range(PAIRS):
            for c_ref, o_ref, r_ref, m_ref in ((k_ref, ko_ref, rk_ref, mk_ref), (w_ref, wo_ref, rs_ref, ms_ref),
                                               (v_ref, vo_ref, rs_ref, ms_ref)):
                rows = jnp.concatenate(
                    [c_ref[gi, q, st] for st in range(2) for gi in range(GROUPS_PER_TILE)], axis=0)
                o_ref[q] = (_dot(rows.astype(BF16), r_ref[...]) * m_ref[...]).astype(BF16)

    compact = pl.BlockSpec((GROUPS_PER_TILE, PAIRS, 2, h, LANES), lambda t: (t, 0, 0, 0, 0))

    def tile(cols):
        return pl.BlockSpec((None, PAIRS, nk, cols), lambda t: (t, 0, 0, 0))

    def whole(a):
        return pl.BlockSpec(a.shape, lambda t: (0, 0))

    return _call_riding(
        body,
        exs,
        args=(k2c, w2c, v2c, rep_k, mask_k, rep_s, mask_s),
        scratch_shapes=[],
        grid=(nt,),
        in_specs=[compact, compact, compact, whole(rep_k), whole(mask_k), whole(rep_s), whole(mask_s)],
        out_specs=[tile(nk), tile(ns), tile(ns)],
        out_shape=[
            jax.ShapeDtypeStruct((nt, PAIRS, nk, nk), BF16),
            jax.ShapeDtypeStruct((nt, PAIRS, nk, ns), BF16),
            jax.ShapeDtypeStruct((nt, PAIRS, nk, ns), BF16),
        ],
        compiler_params=_params(1),
        name="ssm_spread",
    )


def _gather_blocks(full, mask, rep):
    return _dot_nt((full * mask).astype(BF16), rep)


def _in_proj(x, g, w4, tm, exs):
    s, d = x.shape
    nj, _, cw = w4.shape

    def body(x_ref, g_ref, w_ref, z_ref, h_ref):
        @pl.when(pl.program_id(1) == 0)
        def _():
            xv = x_ref[...]
            h_ref[...] = (xv * _rms(xv) * g_ref[...]).astype(BF16)

        z_ref[...] = _dot(h_ref[...], w_ref[...])

    return _call_riding(
        body,
        exs,
        scratch_shapes=[],
        args=(x, g, w4),
        grid=(s // tm, nj),
        in_specs=[
            pl.BlockSpec((tm, d), lambda i, j: (i, 0)),
            pl.BlockSpec((1, d), lambda i, j: (0, 0)),
            pl.BlockSpec((None, d, cw), lambda i, j: (j, 0, 0)),
        ],
        out_specs=[pl.BlockSpec((tm, cw), lambda i, j: (i, j)), pl.BlockSpec((tm, d), lambda i, j: (i, 0))],
        out_shape=[jax.ShapeDtypeStruct((s, nj * cw), F32), jax.ShapeDtypeStruct((s, d), BF16)],
        compiler_params=_params(2),
        name="in_proj",
    )


def _ssm_fwd(z, k2, w2, v2, al, exs):
    s = z.shape[0]
    nt = k2.shape[0]
    nc = s // CHUNK
    ns = w2.shape[-1]
    hs = ns // 2

    def body(u_ref, k_ref, w_ref, v_ref, al_ref, y_ref, xp_ref, us_ref, xloc):
        for q in range(PAIRS):
            us_ref[q, :, 0:LANES] = u_ref[pl.ds(2 * q, nc, stride=CHUNK), :].astype(BF16)
            us_ref[q, :, LANES : 2 * LANES] = u_ref[pl.ds(2 * q + 1, nc, stride=CHUNK), :].astype(BF16)
        acc = _dot(us_ref[0], w_ref[0])
        for q in range(1, PAIRS):
            acc = acc + _dot(us_ref[q], w_ref[q])
        xloc[...] = acc
        ar = al_ref[0:1, :]
        ai = al_ref[1:2, :]

        def step(c, carry):
            xr, xi = carry
            xp_ref[pl.ds(c, 1), 0:hs] = xr
            xp_ref[pl.ds(c, 1), hs:ns] = xi
            lr = xloc[pl.ds(c, 1), 0:hs]
            li = xloc[pl.ds(c, 1), hs:ns]
            return ar * xr - ai * xi + lr, ar * xi + ai * xr + li

        zero = jnp.zeros((1, hs), F32)
        lax.fori_loop(0, nc, step, (zero, zero))
        xpb = xp_ref[...].astype(BF16)
        for r in range(PAIRS):
            acc = _dot_nt(xpb, v_ref[r])
            for q in range(r + 1):
                acc = acc + _dot(us_ref[q], k_ref[r - q])
            y_ref[pl.ds(2 * r, nc, stride=CHUNK), :] = acc[:, 0:LANES]
            y_ref[pl.ds(2 * r + 1, nc, stride=CHUNK), :] = acc[:, LANES : 2 * LANES]

    return _call_riding(
        body,
        exs,
        args=(z, k2, w2, v2, al),
        grid=(nt,),
        in_specs=[
            pl.BlockSpec((s, LANES), lambda t: (0, t)),
            pl.BlockSpec((None, PAIRS, 2 * LANES, 2 * LANES), lambda t: (t, 0, 0, 0)),
            pl.BlockSpec((None, PAIRS, 2 * LANES, ns), lambda t: (t, 0, 0, 0)),
            pl.BlockSpec((None, PAIRS, 2 * LANES, ns), lambda t: (t, 0, 0, 0)),
            pl.BlockSpec((None, 2, hs), lambda t: (t, 0, 0)),
        ],
        out_specs=[
            pl.BlockSpec((s, LANES), lambda t: (0, t)),
            pl.BlockSpec((None, nc, ns), lambda t: (t, 0, 0)),
            pl.BlockSpec((None, PAIRS, nc, 2 * LANES), lambda t: (t, 0, 0, 0)),
        ],
        out_shape=[
            jax.ShapeDtypeStruct((s, nt * LANES), F32),
            jax.ShapeDtypeStruct((nt, nc, ns), F32),
            jax.ShapeDtypeStruct((nt, PAIRS, nc, 2 * LANES), BF16),
        ],
        scratch_shapes=[pltpu.VMEM((nc, ns), F32)],
        compiler_params=_params(1),
        name="ssm_fwd",
    )


def _glu_fwd(y_pre, wg, bg, go, d_model, tm):
    s, w = y_pre.shape

    def body(y_ref, wg_ref, bg_ref, go_ref, o_ref):
        yg = _gelu(y_ref[...])
        gate = _sigmoid(_dot(yg.astype(BF16), wg_ref[...]) + bg_ref[...])
        ys = yg * gate
        o_ref[...] = (ys * _rms(ys) * go_ref[...]).astype(BF16)

    return _call(
        body,
        grid=(s // tm,),
        in_specs=[
            pl.BlockSpec((tm, w), lambda i: (i, 0)),
            pl.BlockSpec((w, w), lambda i: (0, 0)),
            pl.BlockSpec((1, w), lambda i: (0, 0)),
            pl.BlockSpec((1, w), lambda i: (0, 0)),
        ],
        out_specs=pl.BlockSpec((tm, w), lambda i: (i, 0)),
        out_shape=jax.ShapeDtypeStruct((s, d_model), BF16),
        compiler_params=_params(1),
        name="glu_fwd",
    )(y_pre, wg, bg, go)


def _sgu_parts(zu, zv, lng, lnb, wt_ref, bias_ref):
    u, u_grad = _gelu_both(zu)
    v, v_grad = _gelu_both(zv)
    mu = jnp.mean(v, axis=-1, keepdims=True)
    vc = v - mu
    rstd = lax.rsqrt(jnp.mean(vc * vc, axis=-1, keepdims=True) + EPS)
    vhat = vc * rstd
    vb = (vhat * lng + lnb).astype(BF16)
    heads = wt_ref.shape[0]
    mix = jnp.concatenate(
        [_dot(wt_ref[h], vb[:, h * LANES : (h + 1) * LANES]) + bias_ref[h] for h in range(heads)], axis=1
    )
    return u, vhat, rstd, vb, mix, u_grad, v_grad


def _sgu_fwd(z, mixed, lng, lnb, wt, biasb, go, rb):
    s = z.shape[0]
    w = lng.shape[-1]
    heads = wt.shape[0]

    def body(zu_ref, zv_ref, m_any, lng_ref, lnb_ref, wt_ref, b_ref, go_ref, o_ref):
        del m_any
        for ck in range(rb // SGU_CHUNK):
            rows = slice(ck * SGU_CHUNK, (ck + 1) * SGU_CHUNK)
            u, _, _, _, mix, _, _ = _sgu_parts(zu_ref[rows, :], zv_ref[rows, :], lng_ref[...], lnb_ref[...], wt_ref, b_ref)
            y = u * mix
            o_ref[rows, :] = (y * _rms(y) * go_ref[...]).astype(BF16)

    return _call(
        body,
        grid=(s // rb,),
        in_specs=[
            pl.BlockSpec((rb, w), lambda i: (i, 1)),
            pl.BlockSpec((rb, w), lambda i: (i, 2)),
            pl.BlockSpec(memory_space=pl.ANY),
            pl.BlockSpec((1, w), lambda i: (0, 0)),
            pl.BlockSpec((1, w), lambda i: (0, 0)),
            pl.BlockSpec((heads, SGU_CHUNK, SGU_CHUNK), lambda i: (0, 0, 0)),
            pl.BlockSpec((heads, SGU_CHUNK, LANES), lambda i: (0, 0, 0)),
            pl.BlockSpec((1, w), lambda i: (0, 0)),
        ],
        out_specs=pl.BlockSpec((rb, w), lambda i: (i, 1)),
        out_shape=jax.ShapeDtypeStruct(mixed.shape, mixed.dtype),
        input_output_aliases={2: 0},
        compiler_params=_params(1),
        name="sgu_fwd",
    )(z, z, mixed, lng, lnb, wt, biasb, go)


def _out_proj(x, mixed, w_out, g, tm):
    s, d = x.shape

    def body(x_ref, m_ref, w_ref, g_ref, o_ref, h_ref):
        x1 = x_ref[...] + _dot(m_ref[...], w_ref[...])
        o_ref[...] = x1
        h_ref[...] = (x1 * _rms(x1) * g_ref[...]).astype(BF16)

    row = pl.BlockSpec((tm, d), lambda i: (i, 0))
    return _call(
        body,
        grid=(s // tm,),
        in_specs=[row, row, pl.BlockSpec((d, d), lambda i: (0, 0)), pl.BlockSpec((1, d), lambda i: (0, 0))],
        out_specs=[row, row],
        out_shape=[jax.ShapeDtypeStruct((s, d), F32), jax.ShapeDtypeStruct((s, d), BF16)],
        compiler_params=_params(1),
        name="out_proj",
    )(x, mixed, w_out, g)


def _mlp_up(h, w_up4, tm, tf, exs):
    s, d = h.shape
    nj, _, cw = w_up4.shape
    per = cw // tf

    def body(h_ref, wu_ref, up_ref):
        up_ref[...] = _dot(h_ref[...], wu_ref[...]).astype(BF16)

    return _call_riding(
        body,
        exs,
        scratch_shapes=[],
        args=(h, w_up4),
        grid=(s // tm, nj * per),
        in_specs=[
            pl.BlockSpec((tm, d), lambda i, f: (i, 0)),
            pl.BlockSpec((None, d, tf), lambda i, f: (f // per, 0, f % per)),
        ],
        out_specs=[pl.BlockSpec((tm, tf), lambda i, f: (i, f))],
        out_shape=[jax.ShapeDtypeStruct((s, nj * cw), BF16)],
        compiler_params=_params(2),
        name="mlp_up",
    )


def _mlp_down_loss(x1, up, w_down, target, g, tm, tf):
    s, d = x1.shape
    ff = w_down.shape[0]
    nf = ff // tf

    def body(x_ref, up_ref, wd_ref, t_ref, g_ref, loss_ref, dx_ref, dxb_ref, dg_ref, acc):
        i = pl.program_id(0)
        f = pl.program_id(1)

        @pl.when(f == 0)
        def _():
            acc[...] = x_ref[...]

        @pl.when(jnp.logical_and(i == 0, f == 0))
        def _():
            loss_ref[...] = jnp.zeros_like(loss_ref)
            dg_ref[...] = jnp.zeros_like(dg_ref)

        a = jnp.maximum(up_ref[...].astype(F32), 0.0)
        acc[...] += _dot((a * a).astype(BF16), wd_ref[...])

        @pl.when(f == nf - 1)
        def _():
            xv = acc[...]
            r = _rms(xv)
            gv = g_ref[...]
            diff = xv * r * gv - t_ref[...]
            loss_ref[...] += 0.5 * jnp.sum(jnp.mean(diff * diff, axis=-1, keepdims=True), axis=0, keepdims=True)
            dx, dgt = _rms_bwd(diff * (1.0 / d), xv, r, gv)
            dx_ref[...] = dx
            dxb_ref[...] = dx.astype(BF16)
            dg_ref[...] += jnp.sum(dgt, axis=0, keepdims=True)

    row = pl.BlockSpec((tm, d), lambda i, f: (i, 0))
    vec = pl.BlockSpec((1, d), lambda i, f: (0, 0))
    return _call(
        body,
        grid=(s // tm, nf),
        in_specs=[
            row,
            pl.BlockSpec((tm, tf), lambda i, f: (i, f)),
            pl.BlockSpec((tf, d), lambda i, f: (f, 0)),
            row,
            vec,
        ],
        out_specs=[pl.BlockSpec((1, 1), lambda i, f: (0, 0)), row, row, vec],
        out_shape=[
            jax.ShapeDtypeStruct((1, 1), F32),
            jax.ShapeDtypeStruct((s, d), F32),
            jax.ShapeDtypeStruct((s, d), BF16),
            jax.ShapeDtypeStruct((1, d), F32),
        ],
        scratch_shapes=[pltpu.VMEM((tm, d), F32)],
        compiler_params=_params(2),
        name="mlp_down_loss",
    )(x1, up, w_down, target, g)


def _mlp_bwd(dx2b, up, w_up4, w_down, tm, tf):
    s, d = dx2b.shape
    ff = w_down.shape[0]
    cw = w_up4.shape[-1]
    per = cw // tf

    def body(dx_ref, up_ref, wu_ref, wd_ref, dup_ref, dh_ref):
        @pl.when(pl.program_id(1) == 0)
        def _():
            dh_ref[...] = jnp.zeros_like(dh_ref)

        dact = _dot_nt(dx_ref[...], wd_ref[...])
        dupb = (dact * (2.0 * jnp.maximum(up_ref[...].astype(F32), 0.0))).astype(BF16)
        dup_ref[...] = dupb
        dh_ref[...] += _dot_nt(dupb, wu_ref[...])

    return _call(
        body,
        grid=(s // tm, ff // tf),
        in_specs=[
            pl.BlockSpec((tm, d), lambda i, f: (i, 0)),
            pl.BlockSpec((tm, tf), lambda i, f: (i, f)),
            pl.BlockSpec((None, d, tf), lambda i, f: (f // per, 0, f % per)),
            pl.BlockSpec((tf, d), lambda i, f: (f, 0)),
        ],
        out_specs=[pl.BlockSpec((tm, tf), lambda i, f: (i, f)), pl.BlockSpec((tm, d), lambda i, f: (i, 0))],
        out_shape=[jax.ShapeDtypeStruct((s, ff), BF16), jax.ShapeDtypeStruct((s, d), F32)],
        compiler_params=_params(2),
        name="mlp_bwd",
    )(dx2b, up, w_up4, w_down)


def _grad_w(a, b, out_dims, index_map, tm, tn, tk, relu2, name, exs=(), chunks=1):
    t, m = a.shape
    n = b.shape[1]
    nk = t // tk
    cw = tn // chunks

    def body(a_ref, b_ref, o_ref, acc):
        k = pl.program_id(2)

        @pl.when(k == 0)
        def _():
            acc[...] = jnp.zeros_like(acc)

        av = a_ref[...]
        if relu2:
            r = jnp.maximum(av.astype(F32), 0.0)
            av = (r * r).astype(BF16)
        acc[...] += _dot_tn(av, b_ref[...])

        @pl.when(k == nk - 1)
        def _():
            for c in range(chunks):
                o_ref[c] = acc[:, c * cw : (c + 1) * cw].astype(BF16)

    spec = dict(
        grid=(m // tm, n // tn, nk),
        in_specs=[pl.BlockSpec((tk, tm), lambda i, j, k: (k, i)), pl.BlockSpec((tk, tn), lambda i, j, k: (k, j))],
        out_specs=[pl.BlockSpec((chunks, None, tm, cw), lambda i, j, k: index_map(i, j))],
        out_shape=[jax.ShapeDtypeStruct(out_dims, BF16)],
        scratch_shapes=[pltpu.VMEM((tm, tn), F32)],
        compiler_params=_params(3),
        name=name,
    )
    (g,), landed = _call_riding(body, exs, args=(a, b), **spec)
    return (g, *landed) if exs else g


def _norm_out_proj_bwd(dh, x, dres, g, w_out, tm):
    s, d = x.shape

    def body(dh_ref, x_ref, dr_ref, g_ref, w_ref, dx_ref, dxb_ref, dg_ref, dm_ref):
        @pl.when(pl.program_id(0) == 0)
        def _():
            dg_ref[...] = jnp.zeros_like(dg_ref)

        xv = x_ref[...]
        dx, dgt = _rms_bwd(dh_ref[...], xv, _rms(xv), g_ref[...])
        tot = dr_ref[...] + dx
        dx_ref[...] = tot
        totb = tot.astype(BF16)
        dxb_ref[...] = totb
        dg_ref[...] += jnp.sum(dgt, axis=0, keepdims=True)
        dm_ref[...] = _dot_nt(totb, w_ref[...])

    row = pl.BlockSpec((tm, d), lambda i: (i, 0))
    vec = pl.BlockSpec((1, d), lambda i: (0, 0))
    return _call(
        body,
        grid=(s // tm,),
        in_specs=[row, row, row, vec, pl.BlockSpec((d, d), lambda i: (0, 0), pipeline_mode=pl.Buffered(1))],
        out_specs=[row, row, vec, row],
        out_shape=[
            jax.ShapeDtypeStruct((s, d), F32),
            jax.ShapeDtypeStruct((s, d), BF16),
            jax.ShapeDtypeStruct((1, d), F32),
            jax.ShapeDtypeStruct((s, d), F32),
        ],
        compiler_params=_params(1, 60 * 2**20),
        name="norm_out_proj_bwd",
    )(dh, x, dres, g, w_out)


def _glu_bwd(y_pre, dmix, wg, bg, go, tm, exs):
    s, w = y_pre.shape
    n = s // tm

    def body(y_ref, dm_ref, wg_ref, bg_ref, go_ref, dy_ref, dwg_ref, dbg_ref, dgo_ref, acc):
        i = pl.program_id(0)

        @pl.when(i == 0)
        def _():
            acc[...] = jnp.zeros_like(acc)
            dbg_ref[...] = jnp.zeros_like(dbg_ref)
            dgo_ref[...] = jnp.zeros_like(dgo_ref)

        yp = y_ref[...]
        yg, yg_grad = _gelu_both(yp)
        ygb = yg.astype(BF16)
        gate = _sigmoid(_dot(ygb, wg_ref[...]) + bg_ref[...])
        ys = yg * gate
        dys, dgt = _rms_bwd(dm_ref[...], ys, _rms(ys), go_ref[...])
        dgo_ref[...] += jnp.sum(dgt, axis=0, keepdims=True)
        dpre = dys * yg * gate * (1.0 - gate)
        dbg_ref[...] += jnp.sum(dpre, axis=0, keepdims=True)
        dpb = dpre.astype(BF16)
        dyg = dys * gate + _dot_nt(dpb, wg_ref[...])
        dy_ref[...] = dyg * yg_grad
        acc[...] += _dot_tn(ygb, dpb)

        @pl.when(i == n - 1)
        def _():
            dwg_ref[...] = acc[...].astype(BF16)

    return _call_riding(
        body,
        exs,
        args=(y_pre, dmix, wg, bg, go),
        grid=(n,),
        in_specs=[
            pl.BlockSpec((tm, w), lambda i: (i, 0)),
            pl.BlockSpec((tm, w), lambda i: (i, 0)),
            pl.BlockSpec((w, w), lambda i: (0, 0)),
            pl.BlockSpec((1, w), lambda i: (0, 0)),
            pl.BlockSpec((1, w), lambda i: (0, 0)),
        ],
        out_specs=[
            pl.BlockSpec((tm, w), lambda i: (i, 0)),
            pl.BlockSpec((w, w), lambda i: (0, 0)),
            pl.BlockSpec((1, w), lambda i: (0, 0)),
            pl.BlockSpec((1, w), lambda i: (0, 0)),
        ],
        out_shape=[
            jax.ShapeDtypeStruct((s, w), F32),
            jax.ShapeDtypeStruct((w, w), BF16),
            jax.ShapeDtypeStruct((1, w), F32),
            jax.ShapeDtypeStruct((1, w), F32),
        ],
        scratch_shapes=[pltpu.VMEM((w, w), F32)],
        compiler_params=_params(1),
        name="glu_bwd",
    )


def _ssm_bwd_state(dy, v2, al, xprev):
    s = dy.shape[0]
    nt, nc, ns = xprev.shape
    hs = ns // 2

    def body(dy_ref, v_ref, al_ref, xp_ref, dys_ref, g_ref, dal_ref, dxp, gs):
        for q in range(PAIRS):
            dys_ref[q, :, 0:LANES] = dy_ref[pl.ds(2 * q, nc, stride=CHUNK), :].astype(BF16)
            dys_ref[q, :, LANES : 2 * LANES] = dy_ref[pl.ds(2 * q + 1, nc, stride=CHUNK), :].astype(BF16)
        acc = _dot(dys_ref[0], v_ref[0])
        for r in range(1, PAIRS):
            acc = acc + _dot(dys_ref[r], v_ref[r])
        dxp[...] = acc
        ar = al_ref[0:1, :]
        ai = al_ref[1:2, :]
        zero = jnp.zeros((1, hs), F32)
        gs[pl.ds(nc - 1, 1), 0:hs] = zero
        gs[pl.ds(nc - 1, 1), hs:ns] = zero

        def step(n, carry):
            gr, gi = carry
            c = nc - 2 - n
            nr = dxp[pl.ds(c + 1, 1), 0:hs] + ar * gr + ai * gi
            ni = dxp[pl.ds(c + 1, 1), hs:ns] + ar * gi - ai * gr
            gs[pl.ds(c, 1), 0:hs] = nr
            gs[pl.ds(c, 1), hs:ns] = ni
            return nr, ni

        lax.fori_loop(0, nc - 1, step, (zero, zero))
        gv = gs[...]
        xv = xp_ref[...]
        gr, gi = gv[:, 0:hs], gv[:, hs:ns]
        xr, xi = xv[:, 0:hs], xv[:, hs:ns]
        dal_ref[0:1, :] = jnp.sum(gr * xr + gi * xi, axis=0, keepdims=True)
        dal_ref[1:2, :] = jnp.sum(gi * xr - gr * xi, axis=0, keepdims=True)
        g_ref[...] = gv.astype(BF16)

    return _call(
        body,
        grid=(nt,),
        in_specs=[
            pl.BlockSpec((s, LANES), lambda t: (0, t)),
            pl.BlockSpec((None, PAIRS, 2 * LANES, ns), lambda t: (t, 0, 0, 0)),
            pl.BlockSpec((None, 2, hs), lambda t: (t, 0, 0)),
            pl.BlockSpec((None, nc, ns), lambda t: (t, 0, 0)),
        ],
        out_specs=[
            pl.BlockSpec((None, PAIRS, nc, 2 * LANES), lambda t: (t, 0, 0, 0)),
            pl.BlockSpec((None, nc, ns), lambda t: (t, 0, 0)),
            pl.BlockSpec((None, 2, hs), lambda t: (t, 0, 0)),
        ],
        out_shape=[
            jax.ShapeDtypeStruct((nt, PAIRS, nc, 2 * LANES), BF16),
            jax.ShapeDtypeStruct((nt, nc, ns), BF16),
            jax.ShapeDtypeStruct((nt, 2, hs), F32),
        ],
        scratch_shapes=[pltpu.VMEM((nc, ns), F32), pltpu.VMEM((nc, ns), F32)],
        compiler_params=_params(1),
        name="ssm_bwd_state",
    )(dy, v2, al, xprev)


def _ssm_bwd_main(us, dys, k2, w2, xprev, gst, consts, s, width, exs):
    nt, _, nc, _ = us.shape
    ns = xprev.shape[-1]
    rep_k, mask_k, rep_s, mask_s = consts
    nk = 2 * LANES
    gg = GROUPS_PER_TILE
    rows_of = [(gi, st, slice((st * gg + gi) * SSM_GROUP, (st * gg + gi + 1) * SSM_GROUP))
               for st in range(2) for gi in range(gg)]

    def body(us_ref, dys_ref, k_ref, w_ref, xp_ref, g_ref, rk_ref, mk_ref, rs_ref, ms_ref,
             du_ref, dk_ref, dw_ref, dv_ref, acc, du2, duf, dkf):
        q = pl.program_id(1)

        @pl.when(q == 0)
        def _():
            dkf[...] = jnp.zeros_like(dkf)

        usq = us_ref[q]
        gb = g_ref[...]
        dw = _gather_blocks(_dot_tn(usq, gb), ms_ref[...], rs_ref[...])
        dv = _gather_blocks(_dot_tn(dys_ref[q], xp_ref[...].astype(BF16)), ms_ref[...], rs_ref[...])
        for gi, st, rows in rows_of:
            dw_ref[gi, st] = dw[rows]
            dv_ref[gi, st] = dv[rows]
        acc[...] = _dot_nt(gb, w_ref[...])
        for m in range(PAIRS):

            @pl.when(q + m < PAIRS)
            def _():
                dyr = dys_ref[jnp.minimum(q + m, PAIRS - 1)]
                acc[...] += _dot_nt(dyr, k_ref[m])
                dkf[m] += _dot_tn(usq, dyr)

        du2[q] = acc[...]

        @pl.when(q == PAIRS - 1)
        def _():
            for p in range(PAIRS):
                duf[pl.ds(2 * p, nc, stride=CHUNK), :] = du2[p, :, 0:LANES]
                duf[pl.ds(2 * p + 1, nc, stride=CHUNK), :] = du2[p, :, LANES : 2 * LANES]
                dk = _gather_blocks(dkf[p], mk_ref[...], rk_ref[...])
                for gi, st, rows in rows_of:
                    dk_ref[gi, p, st] = dk[rows]
            du_ref[...] = duf[...].astype(BF16)

    def per_tile(rows, cols):
        return pl.BlockSpec((None, PAIRS, rows, cols), lambda t, q: (t, 0, 0, 0))

    def per_pair(rows, cols):
        return pl.BlockSpec((None, None, rows, cols), lambda t, q: (t, q, 0, 0))

    def whole(a):
        return pl.BlockSpec(a.shape, lambda t, q: (0, 0))

    return _call_riding(
        body,
        exs,
        args=(us, dys, k2, w2, xprev, gst, rep_k, mask_k, rep_s, mask_s),
        grid=(nt, PAIRS),
        in_specs=[
            per_tile(nc, nk),
            per_tile(nc, nk),
            per_tile(nk, nk),
            per_pair(nk, ns),
            pl.BlockSpec((None, nc, ns), lambda t, q: (t, 0, 0)),
            pl.BlockSpec((None, nc, ns), lambda t, q: (t, 0, 0)),
            whole(rep_k),
            whole(mask_k),
            whole(rep_s),
            whole(mask_s),
        ],
        out_specs=[
            pl.BlockSpec((s, LANES), lambda t, q: (0, t)),
            pl.BlockSpec((gg, PAIRS, 2, SSM_GROUP, LANES), lambda t, q: (t, 0, 0, 0, 0)),
            pl.BlockSpec((gg, None, 2, SSM_GROUP, LANES), lambda t, q: (t, q, 0, 0, 0)),
            pl.BlockSpec((gg, None, 2, SSM_GROUP, LANES), lambda t, q: (t, q, 0, 0, 0)),
        ],
        out_shape=[
            jax.ShapeDtypeStruct((s, width), BF16),
            jax.ShapeDtypeStruct((nt * gg, PAIRS, 2, SSM_GROUP, LANES), F32),
            jax.ShapeDtypeStruct((nt * gg, PAIRS, 2, SSM_GROUP, LANES), F32),
            jax.ShapeDtypeStruct((nt * gg, PAIRS, 2, SSM_GROUP, LANES), F32),
        ],
        scratch_shapes=[
            pltpu.VMEM((nc, nk), F32),
            pltpu.VMEM((PAIRS, nc, nk), F32),
            pltpu.VMEM((s, LANES), F32),
            pltpu.VMEM((PAIRS, nk, nk), F32),
        ],
        compiler_params=_params(2),
        name="ssm_bwd_main",
    )


def _sgu_bwd(z, dmix, dz, lng, lnb, wt, wtt, biasb, go, rb, exs):
    s = z.shape[0]
    w = lng.shape[-1]
    heads = wt.shape[0]
    ncol = (LANES, LANES)

    def body(zu_ref, zv_ref, dm_ref, dz_any, lng_ref, lnb_ref, wt_ref, wtt_ref, b_ref, go_ref,
             dz_ref, dw_ref, db_ref, dlg_ref, dlb_ref, dgo_ref, dzv):
        del dz_any
        i = pl.program_id(0)
        p = pl.program_id(1)

        @pl.when(jnp.logical_and(i == 0, p == 0))
        def _():
            dw_ref[...] = jnp.zeros_like(dw_ref)
            db_ref[...] = jnp.zeros_like(db_ref)
            dlg_ref[...] = jnp.zeros_like(dlg_ref)
            dlb_ref[...] = jnp.zeros_like(dlb_ref)
            dgo_ref[...] = jnp.zeros_like(dgo_ref)

        @pl.when(p == 0)
        def _():
            lng_v = lng_ref[...]
            for ck in range(rb // SGU_CHUNK):
                rows = slice(ck * SGU_CHUNK, (ck + 1) * SGU_CHUNK)
                zu = zu_ref[rows, :]
                zv = zv_ref[rows, :]
                u, vhat, rstd, vb, mix, u_grad, v_grad = _sgu_parts(zu, zv, lng_v, lnb_ref[...], wt_ref, b_ref)
                y = u * mix
                dy, dgt = _rms_bwd(dm_ref[rows, :], y, _rms(y), go_ref[...])
                dgo_ref[...] += jnp.sum(dgt, axis=0, keepdims=True)
                du = dy * mix
                dmx = dy * u
                dmb = dmx.astype(BF16)
                dvl = []
                for h in range(heads):
                    cols = slice(h * LANES, (h + 1) * LANES)
                    db_ref[h] += jnp.broadcast_to(jnp.sum(dmx[:, cols], axis=-1, keepdims=True), ncol)
                    dw_ref[h] += _dot_nt(dmb[:, cols], vb[:, cols])
                    dvl.append(_dot(wtt_ref[h], dmb[:, cols]))
                dvln = jnp.concatenate(dvl, axis=1)
                dlg_ref[...] += jnp.sum(dvln * vhat, axis=0, keepdims=True)
                dlb_ref[...] += jnp.sum(dvln, axis=0, keepdims=True)
                dvh = dvln * lng_v
                dv = rstd * (
                    dvh
                    - jnp.mean(dvh, axis=-1, keepdims=True)
                    - vhat * jnp.mean(dvh * vhat, axis=-1, keepdims=True)
                )
                dz_ref[rows, :] = (du * u_grad).astype(BF16)
                dzv[rows, :] = (dv * v_grad).astype(BF16)

        @pl.when(p == 1)
        def _():
            dz_ref[...] = dzv[...]

    return _call_riding(
        body,
        exs,
        args=(z, z, dmix, dz, lng, lnb, wt, wtt, biasb, go),
        grid=(s // rb, 2),
        in_specs=[
            pl.BlockSpec((rb, w), lambda i, p: (i, 1)),
            pl.BlockSpec((rb, w), lambda i, p: (i, 2)),
            pl.BlockSpec((rb, w), lambda i, p: (i, 1)),
            pl.BlockSpec(memory_space=pl.ANY),
            pl.BlockSpec((1, w), lambda i, p: (0, 0)),
            pl.BlockSpec((1, w), lambda i, p: (0, 0)),
            pl.BlockSpec((heads, SGU_CHUNK, SGU_CHUNK), lambda i, p: (0, 0, 0)),
            pl.BlockSpec((heads, SGU_CHUNK, SGU_CHUNK), lambda i, p: (0, 0, 0)),
            pl.BlockSpec((heads, SGU_CHUNK, LANES), lambda i, p: (0, 0, 0)),
            pl.BlockSpec((1, w), lambda i, p: (0, 0)),
        ],
        out_specs=[
            pl.BlockSpec((rb, w), lambda i, p: (i, 1 + p)),
            pl.BlockSpec((heads, SGU_CHUNK, SGU_CHUNK), lambda i, p: (0, 0, 0)),
            pl.BlockSpec((heads, SGU_CHUNK, LANES), lambda i, p: (0, 0, 0)),
            pl.BlockSpec((1, w), lambda i, p: (0, 0)),
            pl.BlockSpec((1, w), lambda i, p: (0, 0)),
            pl.BlockSpec((1, w), lambda i, p: (0, 0)),
        ],
        out_shape=[
            jax.ShapeDtypeStruct(dz.shape, dz.dtype),
            jax.ShapeDtypeStruct((heads, SGU_CHUNK, SGU_CHUNK), F32),
            jax.ShapeDtypeStruct((heads, SGU_CHUNK, LANES), F32),
            jax.ShapeDtypeStruct((1, w), F32),
            jax.ShapeDtypeStruct((1, w), F32),
            jax.ShapeDtypeStruct((1, w), F32),
        ],
        scratch_shapes=[pltpu.VMEM((rb, w), BF16)],
        input_output_aliases={3: 0},
        compiler_params=_params(2),
        name="sgu_bwd",
    )


def _in_proj_bwd(dz, w4, x, dres, g, tm, exs, tiles, into, name):
    s, d = x.shape
    nj, _, cw = w4.shape
    first, stop = tiles

    def body(dz_ref, w_ref, x_ref, dr_ref, g_ref, *rest):
        dx_ref, dg_ref, acc = rest[-3:]
        i = pl.program_id(0)
        j = pl.program_id(1)

        @pl.when(j == 0)
        def _():
            acc[...] = jnp.zeros_like(acc)

        @pl.when(jnp.logical_and(i == 0, j == 0))
        def _():
            dg_ref[...] = jnp.zeros_like(dg_ref)

        acc[...] += _dot_nt(dz_ref[...], w_ref[...])

        @pl.when(j == nj - 1)
        def _():
            xv = x_ref[...]
            dx, dgt = _rms_bwd(acc[...], xv, _rms(xv), g_ref[...])
            dx_ref[...] = dr_ref[...] + dx
            dg_ref[...] += jnp.sum(dgt, axis=0, keepdims=True)

    row = pl.BlockSpec((tm, d), lambda i, j: (i + first, 0))
    vec = pl.BlockSpec((1, d), lambda i, j: (0, 0))
    kept = [] if into is None else [into]
    return _call_riding(
        body,
        exs,
        args=(dz, w4, x, dres, g, *kept),
        scratch_shapes=[pltpu.VMEM((tm, d), F32)],
        grid=(stop - first, nj),
        in_specs=[
            pl.BlockSpec((tm, cw), lambda i, j: (i + first, j)),
            pl.BlockSpec((None, d, cw), lambda i, j: (j, 0, 0)),
            row,
            row,
            vec,
        ]
        + [_ANY] * len(kept),
        out_specs=[row, vec],
        out_shape=[jax.ShapeDtypeStruct((s, d), F32), jax.ShapeDtypeStruct((1, d), F32)],
        input_output_aliases={5: 0} if kept else {},
        compiler_params=_params(2),
        name=name,
    )


_ANY = pl.BlockSpec(memory_space=pl.ANY)


def _position():
    x, y, c = lax.axis_index("x"), lax.axis_index("y"), lax.axis_index("c")
    return x, y, c, [(1 - x, y), (x, 1 - y), (1 - x, 1 - y)]


def _remote(src, dst, send_sems, recv_sems, k, to):
    return pltpu.make_async_remote_copy(
        src_ref=src, dst_ref=dst, send_sem=send_sems.at[k], recv_sem=recv_sems.at[k], device_id=to, device_id_type=MESH
    )


class _Riding:
    def __init__(self, srcs, out_shapes, n_sems, start, finish, kept=()):
        self.srcs, self.out_shapes, self.n_sems, self.start, self.finish = srcs, out_shapes, n_sems, start, finish
        self.kept = kept
        self.relay = None


def _call_riding(body, exs, *, grid, in_specs, out_specs, out_shape, scratch_shapes, args, **kw):
    n_in, n_out, n_scr = len(in_specs), len(out_specs), len(scratch_shapes)
    spec = dict(grid=grid, in_specs=list(in_specs), out_specs=list(out_specs), out_shape=list(out_shape))
    if not exs:
        return _call(body, scratch_shapes=list(scratch_shapes), **spec, **kw)(*args), []
    srcs = [a for ex in exs for a in ex.srcs]
    lands = [a for ex in exs for a in ex.out_shapes]
    xi, xo = len(srcs), len(lands)

    def fused(*refs):
        cin, xin = refs[:n_in], refs[n_in : n_in + xi]
        o = n_in + xi
        cout, xout = refs[o : o + n_out], refs[o + n_out : o + n_out + xo]
        scr = refs[o + n_out + xo :]
        ids = [pl.program_id(a) for a in range(len(grid))]
        first = functools.reduce(jnp.logical_and, [i == 0 for i in ids])
        last = functools.reduce(jnp.logical_and, [i == n - 1 for i, n in zip(ids, grid)])

        step = functools.reduce(lambda acc, a: acc * grid[a] + ids[a], range(1, len(grid)), ids[0])

        def each(half):
            si = so = 0
            for e, ex in enumerate(exs):
                ni, no = len(ex.srcs), len(ex.out_shapes)
                sems = scr[n_scr + 2 * e : n_scr + 2 * e + 2]
                if getattr(ex, half) is not None:
                    getattr(ex, half)(xin[si : si + ni], xout[so : so + no], *sems)
                si, so = si + ni, so + no

        @pl.when(first)
        def _():
            each("start")

        @pl.when(step == (3 * math.prod(grid)) // 4)
        def _():
            each("relay")

        body(*cin, *cout, *scr[:n_scr])

        @pl.when(last)
        def _():
            each("finish")

    sems = [pltpu.SemaphoreType.DMA((ex.n_sems,)) for ex in exs for _ in range(2)]
    aliases = dict(kw.pop("input_output_aliases", {}))
    si = so = 0
    for ex in exs:
        for a, b in ex.kept:
            aliases[n_in + si + a] = n_out + so + b
        si, so = si + len(ex.srcs), so + len(ex.out_shapes)
    kw["input_output_aliases"] = aliases
    spec["in_specs"] += [_ANY] * xi
    spec["out_specs"] += [_ANY] * xo
    spec["out_shape"] += lands
    outs = _call(fused, scratch_shapes=list(scratch_shapes) + sems, **spec, **kw)(*args, *srcs)
    return outs[:n_out], outs[n_out:]


def _gather_one(shard, peers=(0, 1, 2), into=None, early_relay=False):
    def own_copies(ins, outs, send, recv):
        x, y, c, chips = _position()
        me = 2 * x + y
        sib = (x, y, 1 - c)
        cps = [] if into is not None else [_remote(ins[0].at[c], outs[0].at[me, c], send, recv, 0, sib)]
        cps += [_remote(ins[0].at[c], outs[0].at[me, c], send, recv, 1 + k, (*chips[k], c)) for k in peers]
        return cps, (c, me, sib, chips)

    def start(ins, outs, send, recv):
        for cp in own_copies(ins, outs, send, recv)[0]:
            cp.start()

    def passed_on(outs, send, recv, c, sib, chips):
        return [_remote(outs[0].at[2 * chips[k][0] + chips[k][1], c], outs[0].at[2 * chips[k][0] + chips[k][1], c],
                        send, recv, 4 + k, sib) for k in peers]

    def relay(ins, outs, send, recv):
        x, y, c, chips = _position()
        for k, cp in zip(peers, passed_on(outs, send, recv, c, (x, y, 1 - c), chips)):
            px, py = chips[k]
            _remote(ins[0].at[c], outs[0].at[2 * px + py, c], send, recv, 1 + k, (px, py, c)).wait_recv()
            cp.start()

    def finish(ins, outs, send, recv):
        if not early_relay:
            relay(ins, outs, send, recv)
        cps, (c, me, sib, chips) = own_copies(ins, outs, send, recv)
        if into is None:
            _remote(ins[0].at[1 - c], outs[0].at[me, 1 - c], send, recv, 0, sib).wait_recv()
        for k in peers:
            px, py = chips[k]
            theirs = outs[0].at[2 * px + py, 1 - c]
            _remote(theirs, theirs, send, recv, 4 + k, sib).wait_recv()
        for cp in cps + passed_on(outs, send, recv, c, sib, chips):
            cp.wait_send()

    land = jax.ShapeDtypeStruct((N_CHIPS,) + shard.shape, shard.dtype)
    rider = _Riding([shard] + ([] if into is None else [into]), [land], 7, start, finish, [] if into is None else [(1, 0)])
    rider.relay = relay if early_relay else None
    return rider


def _to_owner_one(p, peers=(0, 1, 2), into=None):
    def copies(ins, outs, send, recv):
        x, y, c, chips = _position()
        return [_remote(ins[0].at[2 * chips[k][0] + chips[k][1]], outs[0].at[k], send, recv, k, (*chips[k], c)) for k in peers]

    def start(ins, outs, send, recv):
        for cp in copies(ins, outs, send, recv):
            cp.start()

    def finish(ins, outs, send, recv):
        cps = copies(ins, outs, send, recv)
        for cp in cps:
            cp.wait_recv()
        for cp in cps:
            cp.wait_send()

    land = jax.ShapeDtypeStruct((3,) + p.shape[1:], p.dtype)
    return _Riding([p] + ([] if into is None else [into]), [land], 3, start, finish, [] if into is None else [(1, 0)])


def _to_sibling_one(g):
    def copies(ins, outs, send, recv):
        x, y, c, _ = _position()
        return [_remote(ins[0].at[j, 1 - c], outs[0].at[j], send, recv, j, (x, y, 1 - c)) for j in range(N_CHIPS)]

    def start(ins, outs, send, recv):
        for cp in copies(ins, outs, send, recv):
            cp.start()

    def finish(ins, outs, send, recv):
        cps = copies(ins, outs, send, recv)
        for cp in cps:
            cp.wait_recv()
        for cp in cps:
            cp.wait_send()

    return _Riding([g], [jax.ShapeDtypeStruct((g.shape[0],) + g.shape[2:], g.dtype)], N_CHIPS, start, finish)


def _gather_chips(arrs, name):
    n = len(arrs)

    def body(*refs):
        ins, outs = refs[:n], refs[n : 2 * n]
        send_sems, recv_sems, local_sems = refs[2 * n :]
        x, y, c, chips = _position()
        me = 2 * x + y
        local = [pltpu.make_async_copy(ins[i], outs[i].at[me], local_sems.at[i]) for i in range(n)]
        for cp in local:
            cp.start()
        sends = []
        for k, (px, py) in enumerate(chips):
            for i in range(n):
                sends.append(_remote(ins[i], outs[i].at[me], send_sems, recv_sems, k * n + i, (px, py, c)))
                sends[-1].start()
        for k, (px, py) in enumerate(chips):
            for i in range(n):
                _remote(ins[i], outs[i].at[2 * px + py], send_sems, recv_sems, k * n + i, (px, py, c)).wait_recv()
        for cp in sends:
            cp.wait_send()
        for cp in local:
            cp.wait()

    return _call(
        body,
        in_specs=[_ANY] * n,
        out_specs=[_ANY] * n,
        out_shape=[jax.ShapeDtypeStruct((N_CHIPS,) + a.shape, a.dtype) for a in arrs],
        scratch_shapes=[
            pltpu.SemaphoreType.DMA((3 * n,)),
            pltpu.SemaphoreType.DMA((3 * n,)),
            pltpu.SemaphoreType.DMA((n,)),
        ],
        name=name,
    )(*arrs)


def _chip_exchange(ps, name):
    n = len(ps)

    def body(*refs):
        ins, outs = refs[:n], refs[n : 2 * n]
        send_sems, recv_sems = refs[2 * n :]
        x, y, c, chips = _position()
        sends = []
        for k, (px, py) in enumerate(chips):
            for i in range(n):
                sends.append(_remote(ins[i].at[2 * px + py], outs[i].at[k], send_sems, recv_sems, k * n + i, (px, py, c)))
                sends[-1].start()
        for cp in sends:
            cp.wait_recv()
        for cp in sends:
            cp.wait_send()

    return _call(
        body,
        in_specs=[_ANY] * n,
        out_specs=[_ANY] * n,
        out_shape=[jax.ShapeDtypeStruct((3,) + p.shape[1:], p.dtype) for p in ps],
        scratch_shapes=[pltpu.SemaphoreType.DMA((3 * n,)), pltpu.SemaphoreType.DMA((3 * n,))],
        name=name,
    )(*ps)


def _pair_exchange(gs, name):
    n = len(gs)

    def body(*refs):
        ins, outs = refs[:n], refs[n : 2 * n]
        send_sems, recv_sems = refs[2 * n :]
        x, y, c, _ = _position()
        sib = (x, y, 1 - c)
        cps = []
        for i in range(n):
            for j in range(N_CHIPS):
                cps.append(_remote(ins[i].at[j, 1 - c], outs[i].at[j], send_sems, recv_sems, i * N_CHIPS + j, sib))
                cps[-1].start()
        for cp in cps:
            cp.wait_recv()
        for cp in cps:
            cp.wait_send()

    return _call(
        body,
        in_specs=[_ANY] * n,
        out_specs=[_ANY] * n,
        out_shape=[jax.ShapeDtypeStruct((g.shape[0],) + g.shape[2:], g.dtype) for g in gs],
        scratch_shapes=[pltpu.SemaphoreType.DMA((n * N_CHIPS,)), pltpu.SemaphoreType.DMA((n * N_CHIPS,))],
        name=name,
    )(*gs)


def _place_own(shard, gathered, pos, name):
    _, hr, cols = shard.shape
    tr = _row_tile(hr, 512)

    def body(pos_ref, s_ref, g_any, o_ref):
        del pos_ref, g_any
        o_ref[...] = s_ref[...]

    return _call(
        body,
        grid_spec=pltpu.PrefetchScalarGridSpec(
            num_scalar_prefetch=1,
            grid=(hr // tr,),
            in_specs=[pl.BlockSpec((None, tr, cols), lambda i, p: (p[1], i, 0)), _ANY],
            out_specs=pl.BlockSpec((None, None, tr, cols), lambda i, p: (p[0], p[1], i, 0)),
        ),
        out_shape=jax.ShapeDtypeStruct(gathered.shape, gathered.dtype),
        input_output_aliases={2: 0},
        compiler_params=_params(1),
        name=name,
    )(pos, shard, gathered)


def _pair_share(fs, name):
    n = len(fs)

    def body(*refs):
        ins, outs = refs[:n], refs[n : 2 * n]
        send_sems, recv_sems = refs[2 * n :]
        x, y, c, _ = _position()
        sib = (x, y, 1 - c)
        sends = [_remote(ins[i].at[c], outs[i].at[c], send_sems, recv_sems, i, sib) for i in range(n)]
        for cp in sends:
            cp.start()
        for i in range(n):
            _remote(ins[i].at[1 - c], outs[i].at[1 - c], send_sems, recv_sems, i, sib).wait_recv()
        for cp in sends:
            cp.wait_send()

    return _call(
        body,
        in_specs=[_ANY] * n,
        out_specs=[_ANY] * n,
        out_shape=[jax.ShapeDtypeStruct(f.shape, f.dtype) for f in fs],
        input_output_aliases={i: i for i in range(n)},
        scratch_shapes=[pltpu.SemaphoreType.DMA((n,)), pltpu.SemaphoreType.DMA((n,))],
        name=name,
    )(*fs)


def _pair_sum(g, r, core, name):
    nj, _, hr, cols = g.shape
    tr = _row_tile(hr, 256)

    def body(c_ref, g_ref, r_ref, o_ref):
        del c_ref
        o_ref[...] = (g_ref[...].astype(F32) + r_ref[...].astype(F32)).astype(o_ref.dtype)

    return _call(
        body,
        grid_spec=pltpu.PrefetchScalarGridSpec(
            num_scalar_prefetch=1,
            grid=(nj, hr // tr),
            in_specs=[
                pl.BlockSpec((None, None, tr, cols), lambda j, i, c: (j, c[0], i, 0)),
                pl.BlockSpec((None, tr, cols), lambda j, i, c: (j, i, 0)),
            ],
            out_specs=pl.BlockSpec((None, tr, cols), lambda j, i, c: (j, i, 0)),
        ),
        out_shape=jax.ShapeDtypeStruct((nj, hr, cols), g.dtype),
        compiler_params=_params(2),
        name=name,
    )(core, g, r)


def _chip_sum(p, q, pos, name):
    nq, hr, cols = q.shape
    tr = _row_tile(hr, 256)

    def body(pos_ref, p_ref, *refs):
        del pos_ref
        o_ref = refs[nq]
        tot = p_ref[...].astype(F32)
        for k in range(nq):
            tot = tot + refs[k][...].astype(F32)
        o_ref[...] = tot

    return _call(
        body,
        grid_spec=pltpu.PrefetchScalarGridSpec(
            num_scalar_prefetch=1,
            grid=(hr // tr,),
            in_specs=[pl.BlockSpec((None, tr, cols), lambda i, s: (s[0], i, 0))]
            + [pl.BlockSpec((None, tr, cols), functools.partial(lambda k, i, s: (k, i, 0), k)) for k in range(nq)],
            out_specs=pl.BlockSpec((None, tr, cols), lambda i, s: (s[1], i, 0)),
        ),
        out_shape=jax.ShapeDtypeStruct((2, hr, cols), F32),
        compiler_params=_params(1),
        name=name,
    )(pos, p, *([q] * nq))


def _adamw(w, g, m, v, name):
    rows, cols = w.shape
    tr = _row_tile(rows, max(8, (2**18) // cols), mult=8)
    c1 = 1.0 - ADAM_B1**ADAM_STEP
    c2 = 1.0 - ADAM_B2**ADAM_STEP

    def body(w_ref, g_ref, m_ref, v_ref, gk_ref, d_ref, nm_ref, nv_ref):
        gv = g_ref[...]
        gk_ref[...] = gv
        nm = ADAM_B1 * m_ref[...] + (1.0 - ADAM_B1) * gv
        nv = ADAM_B2 * v_ref[...] + (1.0 - ADAM_B2) * (gv * gv)
        nm_ref[...] = nm
        nv_ref[...] = nv
        d_ref[...] = -ADAM_LR * ((nm / c1) / (jnp.sqrt(nv / c2) + ADAM_EPS) + ADAM_WD * w_ref[...])

    spec = pl.BlockSpec((tr, cols), lambda i: (i, 0))
    sds = jax.ShapeDtypeStruct((rows, cols), F32)
    return _call(
        body,
        grid=(rows // tr,),
        in_specs=[spec] * 4,
        out_specs=[spec] * 4,
        out_shape=[sds] * 4,
        compiler_params=_params(1),
        name=name,
    )(w, g, m, v)


_TILE_ELEMS = 8 * LANES
_FLAT_ROW_MULT = 8 * 2 * N_CHIPS


def _flat_rows(shape):
    n = math.prod(shape)
    return (n + _TILE_ELEMS - 1) // _TILE_ELEMS * 8


def _pack(arrs):
    parts = []
    for a in arrs:
        rows = _flat_rows(a.shape)
        flat = a.reshape(-1).astype(F32)
        flat = jnp.pad(flat, (0, rows * LANES - flat.shape[0]))
        parts.append(flat.reshape(rows, LANES))
    total = sum(p.shape[0] for p in parts)
    pad = -total % _FLAT_ROW_MULT
    if pad:
        parts.append(jnp.zeros((pad, LANES), F32))
    return jnp.concatenate(parts, axis=0)


def _unpack(flat, shapes):
    out, row = [], 0
    for shp in shapes:
        rows = _flat_rows(shp)
        out.append(flat[row : row + rows].reshape(-1)[: math.prod(shp)].reshape(shp))
        row += rows
    return out


def kernel(x, norm_mix_g, w_in, ssm_a_re, ssm_a_im, ssm_b_re, ssm_b_im, ssm_c_re, ssm_c_im, ssm_d, ssm_log_dt, ssm_glu_w, ssm_glu_b, sgu_ln_g, sgu_ln_b, sgu_w, sgu_b, out_norm_ssm_g, out_norm_sgu_g, w_out, norm_mlp_g, w_up, w_down, norm_final_g, loss_target, m_norm_mix_g, m_w_in, m_ssm_a_re, m_ssm_a_im, m_ssm_b_re, m_ssm_b_im, m_ssm_c_re, m_ssm_c_im, m_ssm_d, m_ssm_log_dt, m_ssm_glu_w, m_ssm_glu_b, m_sgu_ln_g, m_sgu_ln_b, m_sgu_w, m_sgu_b, m_out_norm_ssm_g, m_out_norm_sgu_g, m_w_out, m_norm_mlp_g, m_w_up, m_w_down, m_norm_final_g, v_norm_mix_g, v_w_in, v_ssm_a_re, v_ssm_a_im, v_ssm_b_re, v_ssm_b_im, v_ssm_c_re, v_ssm_c_im, v_ssm_d, v_ssm_log_dt, v_ssm_glu_w, v_ssm_glu_b, v_sgu_ln_g, v_sgu_ln_b, v_sgu_w, v_sgu_b, v_out_norm_ssm_g, v_out_norm_sgu_g, v_w_out, v_norm_mlp_g, v_w_up, v_w_down, v_norm_final_g):
    weights = dict(norm_mix_g=norm_mix_g, w_in=w_in, ssm_a_re=ssm_a_re, ssm_a_im=ssm_a_im, ssm_b_re=ssm_b_re, ssm_b_im=ssm_b_im, ssm_c_re=ssm_c_re, ssm_c_im=ssm_c_im, ssm_d=ssm_d, ssm_log_dt=ssm_log_dt, ssm_glu_w=ssm_glu_w, ssm_glu_b=ssm_glu_b, sgu_ln_g=sgu_ln_g, sgu_ln_b=sgu_ln_b, sgu_w=sgu_w, sgu_b=sgu_b, out_norm_ssm_g=out_norm_ssm_g, out_norm_sgu_g=out_norm_sgu_g, w_out=w_out, norm_mlp_g=norm_mlp_g, w_up=w_up, w_down=w_down, norm_final_g=norm_final_g)
    mom_m = dict(norm_mix_g=m_norm_mix_g, w_in=m_w_in, ssm_a_re=m_ssm_a_re, ssm_a_im=m_ssm_a_im, ssm_b_re=m_ssm_b_re, ssm_b_im=m_ssm_b_im, ssm_c_re=m_ssm_c_re, ssm_c_im=m_ssm_c_im, ssm_d=m_ssm_d, ssm_log_dt=m_ssm_log_dt, ssm_glu_w=m_ssm_glu_w, ssm_glu_b=m_ssm_glu_b, sgu_ln_g=m_sgu_ln_g, sgu_ln_b=m_sgu_ln_b, sgu_w=m_sgu_w, sgu_b=m_sgu_b, out_norm_ssm_g=m_out_norm_ssm_g, out_norm_sgu_g=m_out_norm_sgu_g, w_out=m_w_out, norm_mlp_g=m_norm_mlp_g, w_up=m_w_up, w_down=m_w_down, norm_final_g=m_norm_final_g)
    mom_v = dict(norm_mix_g=v_norm_mix_g, w_in=v_w_in, ssm_a_re=v_ssm_a_re, ssm_a_im=v_ssm_a_im, ssm_b_re=v_ssm_b_re, ssm_b_im=v_ssm_b_im, ssm_c_re=v_ssm_c_re, ssm_c_im=v_ssm_c_im, ssm_d=v_ssm_d, ssm_log_dt=v_ssm_log_dt, ssm_glu_w=v_ssm_glu_w, ssm_glu_b=v_ssm_glu_b, sgu_ln_g=v_sgu_ln_g, sgu_ln_b=v_sgu_ln_b, sgu_w=v_sgu_w, sgu_b=v_sgu_b, out_norm_ssm_g=v_out_norm_ssm_g, out_norm_sgu_g=v_out_norm_sgu_g, w_out=v_w_out, norm_mlp_g=v_norm_mlp_g, w_up=v_w_up, w_down=v_w_down, norm_final_g=v_norm_final_g)
    names = list(weights)
    large = ["w_in", "ssm_glu_w", "w_out", "w_up", "w_down"]
    small = [n for n in names if n not in large]

    s, d = x.shape[1], x.shape[2]
    xs = x.reshape(s, d)
    target = loss_target.reshape(s, d)
    width = ssm_glu_w.shape[-1]
    ff = w_down.shape[1] * N_CHIPS
    tm = min(512, s)
    core = lax.axis_index("c").astype(jnp.int32).reshape(1)
    chip = (2 * lax.axis_index("x") + lax.axis_index("y")).astype(jnp.int32).reshape(1)
    pos = jnp.concatenate([chip, core])

    shards = [w[0].astype(BF16).reshape(2, w.shape[1] // 2, w.shape[2]) for w in (w_in, ssm_glu_w, w_out, w_up, w_down)]
    ssm_args = (ssm_a_re[0], ssm_a_im[0], ssm_b_re[0], ssm_b_im[0], ssm_c_re[0], ssm_c_im[0], ssm_d[0], ssm_log_dt[0])
    (k2c, w2c, v2c, al), ssm_vjp = jax.vjp(_ssm_mats, *ssm_args)
    consts = _spread_consts(ssm_b_re.shape[-1], ssm_a_re.shape[-1])
    (k2b, w2b, v2b), (w_in_g,) = _ssm_spread(k2c, w2c, v2c, consts, [_gather_one(shards[0])])
    w_in4 = _place_own(shards[0], w_in_g, pos, "place_w_in").reshape(N_CHIPS, d, w_in.shape[2])
    causal = jnp.tril(jnp.ones((SGU_CHUNK, SGU_CHUNK), dtype=bool))
    wt = jnp.where(causal[None], sgu_w[0], 0.0)
    wtb = wt.astype(BF16)
    wttb = jnp.swapaxes(wt, 1, 2).astype(BF16)
    heads = sgu_w.shape[1]
    biasb = jnp.broadcast_to(sgu_b[0][:, :, None], (heads, SGU_CHUNK, LANES))

    tm_big = min(1024, s)
    (z, h1b), (glu_g, out_g, up_g) = _in_proj(
        xs, norm_mix_g, w_in4, tm_big,
        [_gather_one(shards[1]), _gather_one(shards[2]), _gather_one(shards[3], peers=(0, 1))])
    wg_full = _place_own(shards[1], glu_g, pos, "place_glu_w").reshape(width, width)
    w_out_full = _place_own(shards[2], out_g, pos, "place_w_out").reshape(d, d)
    (y_pre, xprev, us), (up_g,) = _ssm_fwd(z, k2b, w2b, v2b, al, [_gather_one(shards[3], peers=(2,), into=up_g)])
    up4 = _place_own(shards[3], up_g, pos, "place_w_up").reshape(N_CHIPS, d, w_up.shape[2])
    mixed = _glu_fwd(y_pre, wg_full, ssm_glu_b, out_norm_ssm_g, d, tm)
    mixed = _sgu_fwd(z, mixed, sgu_ln_g, sgu_ln_b, wtb, biasb, out_norm_sgu_g, tm)
    x1, h2b = _out_proj(xs, mixed, w_out_full, norm_mlp_g, tm)
    tf = min(1024, up4.shape[-1])
    (up,), (down_g,) = _mlp_up(
        h2b, up4, tm_big, min(2048, up4.shape[-1]), [_gather_one(shards[4], early_relay=True)])
    w_down_full = _place_own(shards[4], down_g, pos, "place_w_down").reshape(ff, d)
    loss_part, dx2, dx2b, d_norm_final = _mlp_down_loss(
        x1, up, w_down_full, target, norm_final_g.reshape(1, d), tm, tf)

    dup, dh2 = _mlp_bwd(dx2b, up, up4, w_down_full, tm_big, min(1024, up4.shape[-1]))
    dx1, dx1b, d_norm_mlp, dmix = _norm_out_proj_bwd(dh2, x1, dx2, norm_mlp_g, w_out_full, tm)
    hr_big = d // 2
    tkk = min(2048, s)
    g_down = _grad_w(
        up, dx2b, (N_CHIPS, 2, hr_big, d),
        lambda i, j: (i // (2 * (hr_big // min(1024, hr_big))), (i // (hr_big // min(1024, hr_big))) % 2,
                      i % (hr_big // min(1024, hr_big)), j),
        min(1024, hr_big), d, tkk, True, "grad_w_down")
    tn_up = min(2048, up4.shape[-1])
    per_up = up4.shape[-1] // tn_up
    g_up, sib_down = _grad_w(
        h2b, dup, (N_CHIPS, 2, hr_big, up4.shape[-1]),
        lambda i, j: (j // per_up, i // (hr_big // min(1024, hr_big)), i % (hr_big // min(1024, hr_big)), j % per_up),
        min(1024, hr_big), tn_up, tkk, False, "grad_w_up", exs=[_to_sibling_one(g_down)])
    pair_down = _pair_sum(g_down, sib_down, core, "pair_sum_w_down")
    hr_out = d // (2 * N_CHIPS)
    g_out, sib_up = _grad_w(
        mixed, dx1b, (1, 1, d, d), lambda i, j: (0, 0, i, j), min(1024, d), d, tkk, False, "grad_w_out",
        exs=[_to_sibling_one(g_up)])
    g_out = g_out.reshape(N_CHIPS, 2, hr_out, d)
    pair_up = _pair_sum(g_up, sib_up, core, "pair_sum_w_up")
    (dy_pre, g_glu, d_glu_b, d_norm_ssm), (sib_out,) = _glu_bwd(
        y_pre, dmix, wg_full, ssm_glu_b, out_norm_ssm_g, tm, [_to_sibling_one(g_out)])
    pair_out = _pair_sum(g_out, sib_out, core, "pair_sum_w_out")
    dys, gst, d_al = _ssm_bwd_state(dy_pre, v2b, al, xprev)
    (dz, d_k2, d_w2, d_v2), (chips_down,) = _ssm_bwd_main(
        us, dys, k2b, w2b, xprev, gst, consts, s, z.shape[1], [_to_owner_one(pair_down)])
    (dz, d_wt, d_bias, d_ln_g, d_ln_b, d_norm_sgu), (chips_up,) = _sgu_bwd(
        z, dmix, dz, sgu_ln_g, sgu_ln_b, wtb, wttb, biasb, out_norm_sgu_g, tm,
        [_to_owner_one(pair_up, peers=(0, 1))])
    cw_in = w_in4.shape[-1]
    g_in, chips_up = _grad_w(
        h1b, dz, (N_CHIPS, 2, hr_big, cw_in),
        lambda i, j: (j, i // (hr_big // min(1024, hr_big)), i % (hr_big // min(1024, hr_big)), 0),
        min(1024, hr_big), 2 * cw_in, tkk, False, "grad_w_in",
        exs=[_to_owner_one(pair_up, peers=(2,), into=chips_up)], chunks=2)
    (sib_in,) = _pair_exchange([g_in], "w_in_to_sibling")
    pair_in = _pair_sum(g_in, sib_in, core, "pair_sum_w_in")
    tm_x = min(512, s // 2)
    n_x = s // tm_x
    (gx_half, dg_a), (chips_in,) = _in_proj_bwd(
        dz, w_in4, xs, dx1, norm_mix_g, tm_x, [_to_owner_one(pair_in)], (0, n_x // 2), None, "in_proj_bwd_a")
    (grad_x, dg_b), (chips_out,) = _in_proj_bwd(
        dz, w_in4, xs, dx1, norm_mix_g, tm_x, [_to_owner_one(pair_out)], (n_x // 2, n_x), gx_half, "in_proj_bwd_b")
    d_norm_mix = dg_a + dg_b

    d_ssm = ssm_vjp((d_k2, d_w2, d_v2, d_al))
    small_grads = dict(
        norm_mix_g=d_norm_mix, ssm_a_re=d_ssm[0], ssm_a_im=d_ssm[1], ssm_b_re=d_ssm[2], ssm_b_im=d_ssm[3],
        ssm_c_re=d_ssm[4], ssm_c_im=d_ssm[5], ssm_d=d_ssm[6], ssm_log_dt=d_ssm[7], ssm_glu_b=d_glu_b,
        sgu_ln_g=d_ln_g, sgu_ln_b=d_ln_b, sgu_w=jnp.where(causal[None], d_wt, 0.0), sgu_b=d_bias[:, :, 0],
        out_norm_ssm_g=d_norm_ssm, out_norm_sgu_g=d_norm_sgu, norm_mlp_g=d_norm_mlp, norm_final_g=d_norm_final)
    flat = _pack([small_grads[n] for n in small])
    hr_small = flat.shape[0] // (2 * N_CHIPS)
    g_small = flat.reshape(N_CHIPS, 2, hr_small, LANES)

    hr_glu = width // (2 * N_CHIPS)
    grads = [g_glu.reshape(N_CHIPS, 2, hr_glu, width), g_small]
    tags = ["glu_w", "small"]
    from_sib = _pair_exchange(grads, "grads_to_sibling")
    pair = [_pair_sum(g, r, core, "pair_sum_" + t) for g, r, t in zip(grads, from_sib, tags)]
    from_chips = _chip_exchange(pair, "grads_to_owner")
    pair = [pair_in, pair[0], pair_out, pair_up, pair_down, pair[1]]
    from_chips = [chips_in, from_chips[0], chips_out, chips_up, chips_down, from_chips[1]]
    tags = ["w_in", "glu_w", "w_out", "w_up", "w_down", "small"]
    halves = [_chip_sum(p, q, pos, "chip_sum_" + t) for p, q, t in zip(pair, from_chips, tags)]
    owned = _pair_share(halves, "grads_to_both_cores")
    (small_all,) = _gather_chips([owned[5]], "gather_small_grads")
    small_flat = small_all.reshape(flat.shape)

    grad_out, delta_out, m_out, v_out = {}, {}, {}, {}
    for n, g in zip(large, owned[:5]):
        shp = weights[n].shape
        g2, dl, nm, nv = _adamw(
            weights[n][0], g.reshape(shp[1], shp[2]), mom_m[n][0], mom_v[n][0], "adamw_" + n)
        grad_out[n], delta_out[n], m_out[n], v_out[n] = g2.reshape(shp), dl.reshape(shp), nm.reshape(shp), nv.reshape(shp)
    shapes = [weights[n].shape for n in small]
    g2, dl, nm, nv = _adamw(
        _pack([weights[n] for n in small]), small_flat, _pack([mom_m[n] for n in small]),
        _pack([mom_v[n] for n in small]), "adamw_small")
    for n, g, a, b, c in zip(small, _unpack(g2, shapes), _unpack(dl, shapes), _unpack(nm, shapes), _unpack(nv, shapes)):
        grad_out[n], delta_out[n], m_out[n], v_out[n] = g, a, b, c

    loss = lax.psum(loss_part[0, 0], ("x", "y", "c"))
    return (loss, grad_x.reshape(x.shape), *[grad_out[n] for n in names], *[delta_out[n] for n in names],
            *[m_out[n] for n in names], *[v_out[n] for n in names])
```

```python
import functools
import math

import numpy as np
import jax
import jax.numpy as jnp
from jax import lax
from jax.experimental import pallas as pl
from jax.experimental.pallas import tpu as pltpu

F32 = jnp.float32
BF16 = jnp.bfloat16
MESH = pl.DeviceIdType.MESH
HIGHEST = lax.Precision.HIGHEST

EPS = 1e-6
ADAM_LR = 0.001
ADAM_B1 = 0.9
ADAM_B2 = 0.999
ADAM_EPS = 1e-08
ADAM_WD = 0.01
ADAM_STEP = 10

N_CHIPS = 4
LANES = 128
SSM_GROUP = 16
SSM_STATE = 64
GROUPS_PER_TILE = LANES // SSM_GROUP
CHUNK = 16
PAIRS = CHUNK // 2
SGU_CHUNK = 128
VMEM_LIMIT = 56 * 2**20


def _params(n_axes, vmem=VMEM_LIMIT):
    return pltpu.CompilerParams(dimension_semantics=("arbitrary",) * n_axes, vmem_limit_bytes=vmem)


def _call(body, **kw):
    return pl.pallas_call(body, **kw)


def _dot(a, b):
    return jnp.dot(a, b, preferred_element_type=F32)


def _dot_nt(a, b):
    return lax.dot_general(a, b, (((1,), (1,)), ((), ())), preferred_element_type=F32)


def _dot_tn(a, b):
    return lax.dot_general(a, b, (((0,), (0,)), ((), ())), preferred_element_type=F32)


_GELU_K = math.sqrt(2.0 / math.pi)
_GELU_C = 0.044715


def _gelu_both(x):
    x2 = x * x
    t = jnp.tanh(x * (_GELU_K + (_GELU_K * _GELU_C) * x2))
    hx = 0.5 * x
    onep = 1.0 + t
    return hx * onep, 0.5 * onep + hx * (1.0 - t * t) * (_GELU_K + (3.0 * _GELU_K * _GELU_C) * x2)


def _gelu(x):
    return _gelu_both(x)[0]


def _sigmoid(x):
    return 1.0 / (1.0 + jnp.exp(-x))


def _rms(x):
    return lax.rsqrt(jnp.mean(x * x, axis=-1, keepdims=True) + EPS)


def _rms_bwd(dy, x, r, g):
    a = dy * g
    dx = r * a - x * (r * r * r) * jnp.mean(a * x, axis=-1, keepdims=True)
    return dx, dy * x * r


def _row_tile(rows, target, mult=16):
    for t in range(min(rows, target), 0, -1):
        if rows % t == 0 and t % mult == 0:
            return t
    return rows


def _ssm_mats(a_re, a_im, b_re, b_im, c_re, c_im, d, log_dt):
    g, p = a_re.shape
    h = b_re.shape[-1]
    nt = g // GROUPS_PER_TILE
    dt = jnp.exp(log_dt)[:, None]
    lr, li = a_re * dt, a_im * dt

    def apow(l, lr, li):
        mag = jnp.exp(lr * l)
        return mag * jnp.cos(li * l), mag * jnp.sin(li * l)

    ar, ai = apow(1.0, lr, li)
    den = a_re * a_re + a_im * a_im
    qr = ((ar - 1.0) * a_re + ai * a_im) / den
    qi = (ai * a_re - (ar - 1.0) * a_im) / den
    bt_re, bt_im = jnp.swapaxes(b_re, 1, 2), jnp.swapaxes(b_im, 1, 2)
    bbr = qr[:, None, :] * bt_re - qi[:, None, :] * bt_im
    bbi = qr[:, None, :] * bt_im + qi[:, None, :] * bt_re
    b_same = jnp.concatenate([bbr, bbi], axis=-1)[:, None]
    b_swap = jnp.concatenate([bbi, bbr], axis=-1)[:, None]
    ls = jnp.arange(CHUNK + 1, dtype=F32)[None, :, None]
    pr, pi = apow(ls, lr[:, None, :], li[:, None, :])
    p_same = jnp.concatenate([pr, pr], axis=-1)[:, :, None, :]
    p_sign = jnp.concatenate([-pi, pi], axis=-1)[:, :, None, :]
    tj = p_same[:, :CHUNK] * b_same + p_sign[:, :CHUNK] * b_swap
    c_conj = jnp.concatenate([c_re, -c_im], axis=-1)
    k = jnp.einsum("goq,gliq->glio", c_conj, tj, precision=HIGHEST)
    k = k.at[:, 0].add(d[:, :, None] * jnp.eye(h, dtype=F32))

    zero = jnp.zeros_like(k[:, :1])
    kz = jnp.concatenate([zero, zero, k], axis=1).reshape(g, PAIRS + 1, 2, h, h)
    even, odd = kz[:, :, 0], kz[:, :, 1]
    fill = jnp.zeros((g, PAIRS, h, LANES - 2 * h), F32)
    row0 = jnp.concatenate([even[:, 1:], odd[:, 1:], fill], axis=-1)
    row1 = jnp.concatenate([odd[:, :-1], even[:, 1:], fill], axis=-1)
    k2c = jnp.stack([row0, row1], axis=2)
    w2c = tj[:, ::-1].reshape(g, PAIRS, 2, h, 2 * p)
    c_cross = jnp.concatenate([-c_im, -c_re], axis=-1)
    p_imag = jnp.concatenate([pi, pi], axis=-1)[:, :, None, :]
    v2c = (c_conj[:, None] * p_same[:, 1:] + c_cross[:, None] * p_imag[:, 1:]).reshape(g, PAIRS, 2, h, 2 * p)
    al = jnp.stack([pr[:, CHUNK].reshape(nt, -1), pi[:, CHUNK].reshape(nt, -1)], axis=1)
    return k2c, w2c, v2c, al


def _spread_consts(h, p):
    gg = GROUPS_PER_TILE
    row_g = (np.arange(2 * gg * h) // h) % gg
    colk = np.arange(2 * gg * h)
    rep_k = np.zeros((LANES, 2 * gg * h), np.float32)
    rep_k[(colk // (gg * h)) * h + colk % h, colk] = 1.0
    mask_k = (row_g[:, None] == ((colk // h) % gg)[None, :]).astype(np.float32)
    cols = np.arange(2 * gg * p)
    rep_s = np.zeros((2 * p, 2 * gg * p), np.float32)
    rep_s[(cols // (gg * p)) * p + cols % p, cols] = 1.0
    mask_s = (row_g[:, None] == ((cols // p) % gg)[None, :]).astype(np.float32)
    return tuple(jnp.asarray(a, BF16) for a in (rep_k, mask_k, rep_s, mask_s))


def _ssm_spread(k2c, w2c, v2c, consts, exs):
    h = k2c.shape[3]
    nt = k2c.shape[0] // GROUPS_PER_TILE
    rep_k, mask_k, rep_s, mask_s = consts
    nk, ns = rep_k.shape[1], rep_s.shape[1]

    def body(k_ref, w_ref, v_ref, rk_ref, mk_ref, rs_ref, ms_ref, ko_ref, wo_ref, vo_ref):
        for q in range(PAIRS):
            for c_ref, o_ref, r_ref, m_ref in ((k_ref, ko_ref, rk_ref, mk_ref), (w_ref, wo_ref, rs_ref, ms_ref),
                                               (v_ref, vo_ref, rs_ref, ms_ref)):
                rows = jnp.concatenate(
                    [c_ref[gi, q, st] for st in range(2) for gi in range(GROUPS_PER_TILE)], axis=0)
                o_ref[q] = (_dot(rows.astype(BF16), r_ref[...]) * m_ref[...]).astype(BF16)

    compact = pl.BlockSpec((GROUPS_PER_TILE, PAIRS, 2, h, LANES), lambda t: (t, 0, 0, 0, 0))

    def tile(cols):
        return pl.BlockSpec((None, PAIRS, nk, cols), lambda t: (t, 0, 0, 0))

    def whole(a):
        return pl.BlockSpec(a.shape, lambda t: (0, 0))

    return _call_riding(
        body,
        exs,
        args=(k2c, w2c, v2c, rep_k, mask_k, rep_s, mask_s),
        scratch_shapes=[],
        grid=(nt,),
        in_specs=[compact, compact, compact, whole(rep_k), whole(mask_k), whole(rep_s), whole(mask_s)],
        out_specs=[tile(nk), tile(ns), tile(ns)],
        out_shape=[
            jax.ShapeDtypeStruct((nt, PAIRS, nk, nk), BF16),
            jax.ShapeDtypeStruct((nt, PAIRS, nk, ns), BF16),
            jax.ShapeDtypeStruct((nt, PAIRS, nk, ns), BF16),
        ],
        compiler_params=_params(1),
        name="ssm_spread",
    )


def _gather_blocks(full, mask, rep):
    return _dot_nt((full * mask).astype(BF16), rep)


def _in_proj(x, g, w4, tm, exs):
    s, d = x.shape
    nj, _, cw = w4.shape

    def body(x_ref, g_ref, w_ref, z_ref, h_ref):
        @pl.when(pl.program_id(1) == 0)
        def _():
            xv = x_ref[...]
            h_ref[...] = (xv * _rms(xv) * g_ref[...]).astype(BF16)

        z_ref[...] = _dot(h_ref[...], w_ref[...])

    return _call_riding(
        body,
        exs,
        scratch_shapes=[],
        args=(x, g, w4),
        grid=(s // tm, nj),
        in_specs=[
            pl.BlockSpec((tm, d), lambda i, j: (i, 0)),
            pl.BlockSpec((1, d), lambda i, j: (0, 0)),
            pl.BlockSpec((None, d, cw), lambda i, j: (j, 0, 0)),
        ],
        out_specs=[pl.BlockSpec((tm, cw), lambda i, j: (i, j)), pl.BlockSpec((tm, d), lambda i, j: (i, 0))],
        out_shape=[jax.ShapeDtypeStruct((s, nj * cw), F32), jax.ShapeDtypeStruct((s, d), BF16)],
        compiler_params=_params(2),
        name="in_proj",
    )


def _ssm_fwd(z, k2, w2, v2, al, exs):
    s = z.shape[0]
    nt = k2.shape[0]
    nc = s // CHUNK
    ns = w2.shape[-1]
    hs = ns // 2

    def body(u_ref, k_ref, w_ref, v_ref, al_ref, y_ref, xp_ref, us_ref, xloc):
        for q in range(PAIRS):
            us_ref[q, :, 0:LANES] = u_ref[pl.ds(2 * q, nc, stride=CHUNK), :].astype(BF16)
            us_ref[q, :, LANES : 2 * LANES] = u_ref[pl.ds(2 * q + 1, nc, stride=CHUNK), :].astype(BF16)
        acc = _dot(us_ref[0], w_ref[0])
        for q in range(1, PAIRS):
            acc = acc + _dot(us_ref[q], w_ref[q])
        xloc[...] = acc
        ar = al_ref[0:1, :]
        ai = al_ref[1:2, :]

        def step(c, carry):
            xr, xi = carry
            xp_ref[pl.ds(c, 1), 0:hs] = xr
            xp_ref[pl.ds(c, 1), hs:ns] = xi
            lr = xloc[pl.ds(c, 1), 0:hs]
            li = xloc[pl.ds(c, 1), hs:ns]
            return ar * xr - ai * xi + lr, ar * xi + ai * xr + li

        zero = jnp.zeros((1, hs), F32)
        lax.fori_loop(0, nc, step, (zero, zero))
        xpb = xp_ref[...].astype(BF16)
        for r in range(PAIRS):
            acc = _dot_nt(xpb, v_ref[r])
            for q in range(r + 1):
                acc = acc + _dot(us_ref[q], k_ref[r - q])
            y_ref[pl.ds(2 * r, nc, stride=CHUNK), :] = acc[:, 0:LANES]
            y_ref[pl.ds(2 * r + 1, nc, stride=CHUNK), :] = acc[:, LANES : 2 * LANES]

    return _call_riding(
        body,
        exs,
        args=(z, k2, w2, v2, al),
        grid=(nt,),
        in_specs=[
            pl.BlockSpec((s, LANES), lambda t: (0, t)),
            pl.BlockSpec((None, PAIRS, 2 * LANES, 2 * LANES), lambda t: (t, 0, 0, 0)),
            pl.BlockSpec((None, PAIRS, 2 * LANES, ns), lambda t: (t, 0, 0, 0)),
            pl.BlockSpec((None, PAIRS, 2 * LANES, ns), lambda t: (t, 0, 0, 0)),
            pl.BlockSpec((None, 2, hs), lambda t: (t, 0, 0)),
        ],
        out_specs=[
            pl.BlockSpec((s, LANES), lambda t: (0, t)),
            pl.BlockSpec((None, nc, ns), lambda t: (t, 0, 0)),
            pl.BlockSpec((None, PAIRS, nc, 2 * LANES), lambda t: (t, 0, 0, 0)),
        ],
        out_shape=[
            jax.ShapeDtypeStruct((s, nt * LANES), F32),
            jax.ShapeDtypeStruct((nt, nc, ns), F32),
            jax.ShapeDtypeStruct((nt, PAIRS, nc, 2 * LANES), BF16),
        ],
        scratch_shapes=[pltpu.VMEM((nc, ns), F32)],
        compiler_params=_params(1),
        name="ssm_fwd",
    )


def _glu_fwd(y_pre, wg, bg, go, d_model, tm):
    s, w = y_pre.shape

    def body(y_ref, wg_ref, bg_ref, go_ref, o_ref):
        yg = _gelu(y_ref[...])
        gate = _sigmoid(_dot(yg.astype(BF16), wg_ref[...]) + bg_ref[...])
        ys = yg * gate
        o_ref[...] = (ys * _rms(ys) * go_ref[...]).astype(BF16)

    return _call(
        body,
        grid=(s // tm,),
        in_specs=[
            pl.BlockSpec((tm, w), lambda i: (i, 0)),
            pl.BlockSpec((w, w), lambda i: (0, 0)),
            pl.BlockSpec((1, w), lambda i: (0, 0)),
            pl.BlockSpec((1, w), lambda i: (0, 0)),
        ],
        out_specs=pl.BlockSpec((tm, w), lambda i: (i, 0)),
        out_shape=jax.ShapeDtypeStruct((s, d_model), BF16),
        compiler_params=_params(1),
        name="glu_fwd",
    )(y_pre, wg, bg, go)


def _sgu_parts(zu, zv, lng, lnb, wt_ref, bias_ref):
    u, u_grad = _gelu_both(zu)
    v, v_grad = _gelu_both(zv)
    mu = jnp.mean(v, axis=-1, keepdims=True)
    vc = v - mu
    rstd = lax.rsqrt(jnp.mean(vc * vc, axis=-1, keepdims=True) + EPS)
    vhat = vc * rstd
    vb = (vhat * lng + lnb).astype(BF16)
    heads = wt_ref.shape[0]
    mix = jnp.concatenate(
        [_dot(wt_ref[h], vb[:, h * LANES : (h + 1) * LANES]) + bias_ref[h] for h in range(heads)], axis=1
    )
    return u, vhat, rstd, vb, mix, u_grad, v_grad


def _sgu_fwd(z, mixed, lng, lnb, wt, biasb, go, rb):
    s = z.shape[0]
    w = lng.shape[-1]
    heads = wt.shape[0]

    def body(zu_ref, zv_ref, m_any, lng_ref, lnb_ref, wt_ref, b_ref, go_ref, o_ref):
        del m_any
        for ck in range(rb // SGU_CHUNK):
            rows = slice(ck * SGU_CHUNK, (ck + 1) * SGU_CHUNK)
            u, _, _, _, mix, _, _ = _sgu_parts(zu_ref[rows, :], zv_ref[rows, :], lng_ref[...], lnb_ref[...], wt_ref, b_ref)
            y = u * mix
            o_ref[rows, :] = (y * _rms(y) * go_ref[...]).astype(BF16)

    return _call(
        body,
        grid=(s // rb,),
        in_specs=[
            pl.BlockSpec((rb, w), lambda i: (i, 1)),
            pl.BlockSpec((rb, w), lambda i: (i, 2)),
            pl.BlockSpec(memory_space=pl.ANY),
            pl.BlockSpec((1, w), lambda i: (0, 0)),
            pl.BlockSpec((1, w), lambda i: (0, 0)),
            pl.BlockSpec((heads, SGU_CHUNK, SGU_CHUNK), lambda i: (0, 0, 0)),
            pl.BlockSpec((heads, SGU_CHUNK, LANES), lambda i: (0, 0, 0)),
            pl.BlockSpec((1, w), lambda i: (0, 0)),
        ],
        out_specs=pl.BlockSpec((rb, w), lambda i: (i, 1)),
        out_shape=jax.ShapeDtypeStruct(mixed.shape, mixed.dtype),
        input_output_aliases={2: 0},
        compiler_params=_params(1),
        name="sgu_fwd",
    )(z, z, mixed, lng, lnb, wt, biasb, go)


def _out_proj(x, mixed, w_out, g, tm):
    s, d = x.shape

    def body(x_ref, m_ref, w_ref, g_ref, o_ref, h_ref):
        x1 = x_ref[...] + _dot(m_ref[...], w_ref[...])
        o_ref[...] = x1
        h_ref[...] = (x1 * _rms(x1) * g_ref[...]).astype(BF16)

    row = pl.BlockSpec((tm, d), lambda i: (i, 0))
    return _call(
        body,
        grid=(s // tm,),
        in_specs=[row, row, pl.BlockSpec((d, d), lambda i: (0, 0)), pl.BlockSpec((1, d), lambda i: (0, 0))],
        out_specs=[row, row],
        out_shape=[jax.ShapeDtypeStruct((s, d), F32), jax.ShapeDtypeStruct((s, d), BF16)],
        compiler_params=_params(1),
        name="out_proj",
    )(x, mixed, w_out, g)


def _mlp_up(h, w_up4, tm, tf, exs):
    s, d = h.shape
    nj, _, cw = w_up4.shape
    per = cw // tf

    def body(h_ref, wu_ref, up_ref):
        up_ref[...] = _dot(h_ref[...], wu_ref[...]).astype(BF16)

    return _call_riding(
        body,
        exs,
        scratch_shapes=[],
        args=(h, w_up4),
        grid=(s // tm, nj * per),
        in_specs=[
            pl.BlockSpec((tm, d), lambda i, f: (i, 0)),
            pl.BlockSpec((None, d, tf), lambda i, f: (f // per, 0, f % per)),
        ],
        out_specs=[pl.BlockSpec((tm, tf), lambda i, f: (i, f))],
        out_shape=[jax.ShapeDtypeStruct((s, nj * cw), BF16)],
        compiler_params=_params(2),
        name="mlp_up",
    )


def _mlp_down_loss(x1, up, w_down, target, g, tm, tf):
    s, d = x1.shape
    ff = w_down.shape[0]
    nf = ff // tf

    def body(x_ref, up_ref, wd_ref, t_ref, g_ref, loss_ref, dx_ref, dxb_ref, dg_ref, acc):
        i = pl.program_id(0)
        f = pl.program_id(1)

        @pl.when(f == 0)
        def _():
            acc[...] = x_ref[...]

        @pl.when(jnp.logical_and(i == 0, f == 0))
        def _():
            loss_ref[...] = jnp.zeros_like(loss_ref)
            dg_ref[...] = jnp.zeros_like(dg_ref)

        a = jnp.maximum(up_ref[...].astype(F32), 0.0)
        acc[...] += _dot((a * a).astype(BF16), wd_ref[...])

        @pl.when(f == nf - 1)
        def _():
            xv = acc[...]
            r = _rms(xv)
            gv = g_ref[...]
            diff = xv * r * gv - t_ref[...]
            loss_ref[...] += 0.5 * jnp.sum(jnp.mean(diff * diff, axis=-1, keepdims=True), axis=0, keepdims=True)
            dx, dgt = _rms_bwd(diff * (1.0 / d), xv, r, gv)
            dx_ref[...] = dx
            dxb_ref[...] = dx.astype(BF16)
            dg_ref[...] += jnp.sum(dgt, axis=0, keepdims=True)

    row = pl.BlockSpec((tm, d), lambda i, f: (i, 0))
    once = pl.BlockSpec((tm, d), lambda i, f: (i, 0), pipeline_mode=pl.Buffered(1))
    vec = pl.BlockSpec((1, d), lambda i, f: (0, 0))
    return _call(
        body,
        grid=(s // tm, nf),
        in_specs=[
            once,
            pl.BlockSpec((tm, tf), lambda i, f: (i, f)),
            pl.BlockSpec((tf, d), lambda i, f: (f, 0)),
            once,
            vec,
        ],
        out_specs=[pl.BlockSpec((1, 1), lambda i, f: (0, 0)), row, row, vec],
        out_shape=[
            jax.ShapeDtypeStruct((1, 1), F32),
            jax.ShapeDtypeStruct((s, d), F32),
            jax.ShapeDtypeStruct((s, d), BF16),
            jax.ShapeDtypeStruct((1, d), F32),
        ],
        scratch_shapes=[pltpu.VMEM((tm, d), F32)],
        compiler_params=_params(2),
        name="mlp_down_loss",
    )(x1, up, w_down, target, g)


def _mlp_bwd(dx2b, up, w_up4, w_down, tm, tf):
    s, d = dx2b.shape
    ff = w_down.shape[0]
    cw = w_up4.shape[-1]
    per = cw // tf

    def body(dx_ref, up_ref, wu_ref, wd_ref, dup_ref, dh_ref):
        @pl.when(pl.program_id(1) == 0)
        def _():
            dh_ref[...] = jnp.zeros_like(dh_ref)

        dact = _dot_nt(dx_ref[...], wd_ref[...])
        dupb = (dact * (2.0 * jnp.maximum(up_ref[...].astype(F32), 0.0))).astype(BF16)
        dup_ref[...] = dupb
        dh_ref[...] += _dot_nt(dupb, wu_ref[...])

    return _call(
        body,
        grid=(s // tm, ff // tf),
        in_specs=[
            pl.BlockSpec((tm, d), lambda i, f: (i, 0)),
            pl.BlockSpec((tm, tf), lambda i, f: (i, f)),
            pl.BlockSpec((None, d, tf), lambda i, f: (f // per, 0, f % per)),
            pl.BlockSpec((tf, d), lambda i, f: (f, 0)),
        ],
        out_specs=[pl.BlockSpec((tm, tf), lambda i, f: (i, f)), pl.BlockSpec((tm, d), lambda i, f: (i, 0))],
        out_shape=[jax.ShapeDtypeStruct((s, ff), BF16), jax.ShapeDtypeStruct((s, d), F32)],
        compiler_params=_params(2),
        name="mlp_bwd",
    )(dx2b, up, w_up4, w_down)


def _grad_w(a, b, out_dims, index_map, tm, tn, tk, relu2, name, exs=(), chunks=1):
    t, m = a.shape
    n = b.shape[1]
    nk = t // tk
    cw = tn // chunks

    def body(a_ref, b_ref, o_ref, acc):
        k = pl.program_id(2)

        @pl.when(k == 0)
        def _():
            acc[...] = jnp.zeros_like(acc)

        av = a_ref[...]
        if relu2:
            r = jnp.maximum(av.astype(F32), 0.0)
            av = (r * r).astype(BF16)
        acc[...] += _dot_tn(av, b_ref[...])

        @pl.when(k == nk - 1)
        def _():
            for c in range(chunks):
                o_ref[c] = acc[:, c * cw : (c + 1) * cw].astype(BF16)

    spec = dict(
        grid=(m // tm, n // tn, nk),
        in_specs=[pl.BlockSpec((tk, tm), lambda i, j, k: (k, i)), pl.BlockSpec((tk, tn), lambda i, j, k: (k, j))],
        out_specs=[pl.BlockSpec((chunks, None, tm, cw), lambda i, j, k: index_map(i, j))],
        out_shape=[jax.ShapeDtypeStruct(out_dims, BF16)],
        scratch_shapes=[pltpu.VMEM((tm, tn), F32)],
        compiler_params=_params(3),
        name=name,
    )
    (g,), landed = _call_riding(body, exs, args=(a, b), **spec)
    return (g, *landed) if exs else g


def _norm_out_proj_bwd(dh, x, dres, g, w_out, tm):
    s, d = x.shape

    def body(dh_ref, x_ref, dr_ref, g_ref, w_ref, dx_ref, dxb_ref, dg_ref, dm_ref):
        @pl.when(pl.program_id(0) == 0)
        def _():
            dg_ref[...] = jnp.zeros_like(dg_ref)

        xv = x_ref[...]
        dx, dgt = _rms_bwd(dh_ref[...], xv, _rms(xv), g_ref[...])
        tot = dr_ref[...] + dx
        dx_ref[...] = tot
        totb = tot.astype(BF16)
        dxb_ref[...] = totb
        dg_ref[...] += jnp.sum(dgt, axis=0, keepdims=True)
        dm_ref[...] = _dot_nt(totb, w_ref[...])

    row = pl.BlockSpec((tm, d), lambda i: (i, 0))
    vec = pl.BlockSpec((1, d), lambda i: (0, 0))
    return _call(
        body,
        grid=(s // tm,),
        in_specs=[row, row, row, vec, pl.BlockSpec((d, d), lambda i: (0, 0), pipeline_mode=pl.Buffered(1))],
        out_specs=[row, row, vec, row],
        out_shape=[
            jax.ShapeDtypeStruct((s, d), F32),
            jax.ShapeDtypeStruct((s, d), BF16),
            jax.ShapeDtypeStruct((1, d), F32),
            jax.ShapeDtypeStruct((s, d), F32),
        ],
        compiler_params=_params(1, 60 * 2**20),
        name="norm_out_proj_bwd",
    )(dh, x, dres, g, w_out)


def _glu_bwd(y_pre, dmix, wg, bg, go, tm, exs):
    s, w = y_pre.shape
    n = s // tm

    def body(y_ref, dm_ref, wg_ref, bg_ref, go_ref, dy_ref, dwg_ref, dbg_ref, dgo_ref, acc):
        i = pl.program_id(0)

        @pl.when(i == 0)
        def _():
            acc[...] = jnp.zeros_like(acc)
            dbg_ref[...] = jnp.zeros_like(dbg_ref)
            dgo_ref[...] = jnp.zeros_like(dgo_ref)

        yp = y_ref[...]
        yg, yg_grad = _gelu_both(yp)
        ygb = yg.astype(BF16)
        gate = _sigmoid(_dot(ygb, wg_ref[...]) + bg_ref[...])
        ys = yg * gate
        dys, dgt = _rms_bwd(dm_ref[...], ys, _rms(ys), go_ref[...])
        dgo_ref[...] += jnp.sum(dgt, axis=0, keepdims=True)
        dpre = dys * yg * gate * (1.0 - gate)
        dbg_ref[...] += jnp.sum(dpre, axis=0, keepdims=True)
        dpb = dpre.astype(BF16)
        dyg = dys * gate + _dot_nt(dpb, wg_ref[...])
        dy_ref[...] = dyg * yg_grad
        acc[...] += _dot_tn(ygb, dpb)

        @pl.when(i == n - 1)
        def _():
            dwg_ref[...] = acc[...].astype(BF16)

    return _call_riding(
        body,
        exs,
        args=(y_pre, dmix, wg, bg, go),
        grid=(n,),
        in_specs=[
            pl.BlockSpec((tm, w), lambda i: (i, 0)),
            pl.BlockSpec((tm, w), lambda i: (i, 0)),
            pl.BlockSpec((w, w), lambda i: (0, 0)),
            pl.BlockSpec((1, w), lambda i: (0, 0)),
            pl.BlockSpec((1, w), lambda i: (0, 0)),
        ],
        out_specs=[
            pl.BlockSpec((tm, w), lambda i: (i, 0)),
            pl.BlockSpec((w, w), lambda i: (0, 0)),
            pl.BlockSpec((1, w), lambda i: (0, 0)),
            pl.BlockSpec((1, w), lambda i: (0, 0)),
        ],
        out_shape=[
            jax.ShapeDtypeStruct((s, w), F32),
            jax.ShapeDtypeStruct((w, w), BF16),
            jax.ShapeDtypeStruct((1, w), F32),
            jax.ShapeDtypeStruct((1, w), F32),
        ],
        scratch_shapes=[pltpu.VMEM((w, w), F32)],
        compiler_params=_params(1),
        name="glu_bwd",
    )


def _ssm_bwd_state(dy, v2, al, xprev):
    s = dy.shape[0]
    nt, nc, ns = xprev.shape
    hs = ns // 2

    def body(dy_ref, v_ref, al_ref, xp_ref, dys_ref, g_ref, dal_ref, dxp, gs):
        for q in range(PAIRS):
            dys_ref[q, :, 0:LANES] = dy_ref[pl.ds(2 * q, nc, stride=CHUNK), :].astype(BF16)
            dys_ref[q, :, LANES : 2 * LANES] = dy_ref[pl.ds(2 * q + 1, nc, stride=CHUNK), :].astype(BF16)
        acc = _dot(dys_ref[0], v_ref[0])
        for r in range(1, PAIRS):
            acc = acc + _dot(dys_ref[r], v_ref[r])
        dxp[...] = acc
        ar = al_ref[0:1, :]
        ai = al_ref[1:2, :]
        zero = jnp.zeros((1, hs), F32)
        gs[pl.ds(nc - 1, 1), 0:hs] = zero
        gs[pl.ds(nc - 1, 1), hs:ns] = zero

        def step(n, carry):
            gr, gi = carry
            c = nc - 2 - n
            nr = dxp[pl.ds(c + 1, 1), 0:hs] + ar * gr + ai * gi
            ni = dxp[pl.ds(c + 1, 1), hs:ns] + ar * gi - ai * gr
            gs[pl.ds(c, 1), 0:hs] = nr
            gs[pl.ds(c, 1), hs:ns] = ni
            return nr, ni

        lax.fori_loop(0, nc - 1, step, (zero, zero))
        gv = gs[...]
        xv = xp_ref[...]
        gr, gi = gv[:, 0:hs], gv[:, hs:ns]
        xr, xi = xv[:, 0:hs], xv[:, hs:ns]
        dal_ref[0:1, :] = jnp.sum(gr * xr + gi * xi, axis=0, keepdims=True)
        dal_ref[1:2, :] = jnp.sum(gi * xr - gr * xi, axis=0, keepdims=True)
        g_ref[...] = gv.astype(BF16)

    return _call(
        body,
        grid=(nt,),
        in_specs=[
            pl.BlockSpec((s, LANES), lambda t: (0, t)),
            pl.BlockSpec((None, PAIRS, 2 * LANES, ns), lambda t: (t, 0, 0, 0)),
            pl.BlockSpec((None, 2, hs), lambda t: (t, 0, 0)),
            pl.BlockSpec((None, nc, ns), lambda t: (t, 0, 0)),
        ],
        out_specs=[
            pl.BlockSpec((None, PAIRS, nc, 2 * LANES), lambda t: (t, 0, 0, 0)),
            pl.BlockSpec((None, nc, ns), lambda t: (t, 0, 0)),
            pl.BlockSpec((None, 2, hs), lambda t: (t, 0, 0)),
        ],
        out_shape=[
            jax.ShapeDtypeStruct((nt, PAIRS, nc, 2 * LANES), BF16),
            jax.ShapeDtypeStruct((nt, nc, ns), BF16),
            jax.ShapeDtypeStruct((nt, 2, hs), F32),
        ],
        scratch_shapes=[pltpu.VMEM((nc, ns), F32), pltpu.VMEM((nc, ns), F32)],
        compiler_params=_params(1),
        name="ssm_bwd_state",
    )(dy, v2, al, xprev)


def _ssm_bwd_main(us, dys, k2, w2, xprev, gst, consts, s, width, exs):
    nt, _, nc, _ = us.shape
    ns = xprev.shape[-1]
    rep_k, mask_k, rep_s, mask_s = consts
    nk = 2 * LANES
    gg = GROUPS_PER_TILE
    rows_of = [(gi, st, slice((st * gg + gi) * SSM_GROUP, (st * gg + gi + 1) * SSM_GROUP))
               for st in range(2) for gi in range(gg)]

    def body(us_ref, dys_ref, k_ref, w_ref, xp_ref, g_ref, rk_ref, mk_ref, rs_ref, ms_ref,
             du_ref, dk_ref, dw_ref, dv_ref, acc, du2, duf, dkf):
        q = pl.program_id(1)

        @pl.when(q == 0)
        def _():
            dkf[...] = jnp.zeros_like(dkf)

        usq = us_ref[q]
        gb = g_ref[...]
        dw = _gather_blocks(_dot_tn(usq, gb), ms_ref[...], rs_ref[...])
        dv = _gather_blocks(_dot_tn(dys_ref[q], xp_ref[...].astype(BF16)), ms_ref[...], rs_ref[...])
        for gi, st, rows in rows_of:
            dw_ref[gi, st] = dw[rows]
            dv_ref[gi, st] = dv[rows]
        acc[...] = _dot_nt(gb, w_ref[...])
        for m in range(PAIRS):

            @pl.when(q + m < PAIRS)
            def _():
                dyr = dys_ref[jnp.minimum(q + m, PAIRS - 1)]
                acc[...] += _dot_nt(dyr, k_ref[m])
                dkf[m] += _dot_tn(usq, dyr)

        du2[q] = acc[...]

        @pl.when(q == PAIRS - 1)
        def _():
            for p in range(PAIRS):
                duf[pl.ds(2 * p, nc, stride=CHUNK), :] = du2[p, :, 0:LANES]
                duf[pl.ds(2 * p + 1, nc, stride=CHUNK), :] = du2[p, :, LANES : 2 * LANES]
                dk = _gather_blocks(dkf[p], mk_ref[...], rk_ref[...])
                for gi, st, rows in rows_of:
                    dk_ref[gi, p, st] = dk[rows]
            du_ref[...] = duf[...].astype(BF16)

    def per_tile(rows, cols):
        return pl.BlockSpec((None, PAIRS, rows, cols), lambda t, q: (t, 0, 0, 0))

    def per_pair(rows, cols):
        return pl.BlockSpec((None, None, rows, cols), lambda t, q: (t, q, 0, 0))

    def whole(a):
        return pl.BlockSpec(a.shape, lambda t, q: (0, 0))

    return _call_riding(
        body,
        exs,
        args=(us, dys, k2, w2, xprev, gst, rep_k, mask_k, rep_s, mask_s),
        grid=(nt, PAIRS),
        in_specs=[
            per_tile(nc, nk),
            per_tile(nc, nk),
            per_tile(nk, nk),
            per_pair(nk, ns),
            pl.BlockSpec((None, nc, ns), lambda t, q: (t, 0, 0)),
            pl.BlockSpec((None, nc, ns), lambda t, q: (t, 0, 0)),
            whole(rep_k),
            whole(mask_k),
            whole(rep_s),
            whole(mask_s),
        ],
        out_specs=[
            pl.BlockSpec((s, LANES), lambda t, q: (0, t)),
            pl.BlockSpec((gg, PAIRS, 2, SSM_GROUP, LANES), lambda t, q: (t, 0, 0, 0, 0)),
            pl.BlockSpec((gg, None, 2, SSM_GROUP, LANES), lambda t, q: (t, q, 0, 0, 0)),
            pl.BlockSpec((gg, None, 2, SSM_GROUP, LANES), lambda t, q: (t, q, 0, 0, 0)),
        ],
        out_shape=[
            jax.ShapeDtypeStruct((s, width), BF16),
            jax.ShapeDtypeStruct((nt * gg, PAIRS, 2, SSM_GROUP, LANES), F32),
            jax.ShapeDtypeStruct((nt * gg, PAIRS, 2, SSM_GROUP, LANES), F32),
            jax.ShapeDtypeStruct((nt * gg, PAIRS, 2, SSM_GROUP, LANES), F32),
        ],
        scratch_shapes=[
            pltpu.VMEM((nc, nk), F32),
            pltpu.VMEM((PAIRS, nc, nk), F32),
            pltpu.VMEM((s, LANES), F32),
            pltpu.VMEM((PAIRS, nk, nk), F32),
        ],
        compiler_params=_params(2),
        name="ssm_bwd_main",
    )


def _sgu_bwd(z, dmix, dz, lng, lnb, wt, wtt, biasb, go, rb, exs):
    s = z.shape[0]
    w = lng.shape[-1]
    heads = wt.shape[0]
    ncol = (LANES, LANES)

    def body(zu_ref, zv_ref, dm_ref, dz_any, lng_ref, lnb_ref, wt_ref, wtt_ref, b_ref, go_ref,
             dz_ref, dw_ref, db_ref, dlg_ref, dlb_ref, dgo_ref, dzv):
        del dz_any
        i = pl.program_id(0)
        p = pl.program_id(1)

        @pl.when(jnp.logical_and(i == 0, p == 0))
        def _():
            dw_ref[...] = jnp.zeros_like(dw_ref)
            db_ref[...] = jnp.zeros_like(db_ref)
            dlg_ref[...] = jnp.zeros_like(dlg_ref)
            dlb_ref[...] = jnp.zeros_like(dlb_ref)
            dgo_ref[...] = jnp.zeros_like(dgo_ref)

        @pl.when(p == 0)
        def _():
            lng_v = lng_ref[...]
            for ck in range(rb // SGU_CHUNK):
                rows = slice(ck * SGU_CHUNK, (ck + 1) * SGU_CHUNK)
                zu = zu_ref[rows, :]
                zv = zv_ref[rows, :]
                u, vhat, rstd, vb, mix, u_grad, v_grad = _sgu_parts(zu, zv, lng_v, lnb_ref[...], wt_ref, b_ref)
                y = u * mix
                dy, dgt = _rms_bwd(dm_ref[rows, :], y, _rms(y), go_ref[...])
                dgo_ref[...] += jnp.sum(dgt, axis=0, keepdims=True)
                du = dy * mix
                dmx = dy * u
                dmb = dmx.astype(BF16)
                dvl = []
                for h in range(heads):
                    cols = slice(h * LANES, (h + 1) * LANES)
                    db_ref[h] += jnp.broadcast_to(jnp.sum(dmx[:, cols], axis=-1, keepdims=True), ncol)
                    dw_ref[h] += _dot_nt(dmb[:, cols], vb[:, cols])
                    dvl.append(_dot(wtt_ref[h], dmb[:, cols]))
                dvln = jnp.concatenate(dvl, axis=1)
                dlg_ref[...] += jnp.sum(dvln * vhat, axis=0, keepdims=True)
                dlb_ref[...] += jnp.sum(dvln, axis=0, keepdims=True)
                dvh = dvln * lng_v
                dv = rstd * (
                    dvh
                    - jnp.mean(dvh, axis=-1, keepdims=True)
                    - vhat * jnp.mean(dvh * vhat, axis=-1, keepdims=True)
                )
                dz_ref[rows, :] = (du * u_grad).astype(BF16)
                dzv[rows, :] = (dv * v_grad).astype(BF16)

        @pl.when(p == 1)
        def _():
            dz_ref[...] = dzv[...]

    return _call_riding(
        body,
        exs,
        args=(z, z, dmix, dz, lng, lnb, wt, wtt, biasb, go),
        grid=(s // rb, 2),
        in_specs=[
            pl.BlockSpec((rb, w), lambda i, p: (i, 1)),
            pl.BlockSpec((rb, w), lambda i, p: (i, 2)),
            pl.BlockSpec((rb, w), lambda i, p: (i, 1)),
            pl.BlockSpec(memory_space=pl.ANY),
            pl.BlockSpec((1, w), lambda i, p: (0, 0)),
            pl.BlockSpec((1, w), lambda i, p: (0, 0)),
            pl.BlockSpec((heads, SGU_CHUNK, SGU_CHUNK), lambda i, p: (0, 0, 0)),
            pl.BlockSpec((heads, SGU_CHUNK, SGU_CHUNK), lambda i, p: (0, 0, 0)),
            pl.BlockSpec((heads, SGU_CHUNK, LANES), lambda i, p: (0, 0, 0)),
            pl.BlockSpec((1, w), lambda i, p: (0, 0)),
        ],
        out_specs=[
            pl.BlockSpec((rb, w), lambda i, p: (i, 1 + p)),
            pl.BlockSpec((heads, SGU_CHUNK, SGU_CHUNK), lambda i, p: (0, 0, 0)),
            pl.BlockSpec((heads, SGU_CHUNK, LANES), lambda i, p: (0, 0, 0)),
            pl.BlockSpec((1, w), lambda i, p: (0, 0)),
            pl.BlockSpec((1, w), lambda i, p: (0, 0)),
            pl.BlockSpec((1, w), lambda i, p: (0, 0)),
        ],
        out_shape=[
            jax.ShapeDtypeStruct(dz.shape, dz.dtype),
            jax.ShapeDtypeStruct((heads, SGU_CHUNK, SGU_CHUNK), F32),
            jax.ShapeDtypeStruct((heads, SGU_CHUNK, LANES), F32),
            jax.ShapeDtypeStruct((1, w), F32),
            jax.ShapeDtypeStruct((1, w), F32),
            jax.ShapeDtypeStruct((1, w), F32),
        ],
        scratch_shapes=[pltpu.VMEM((rb, w), BF16)],
        input_output_aliases={3: 0},
        compiler_params=_params(2),
        name="sgu_bwd",
    )


def _in_proj_bwd(dz, w4, x, dres, g, tm, exs, tiles, into, name):
    s, d = x.shape
    nj, _, cw = w4.shape
    first, stop = tiles

    def body(dz_ref, w_ref, x_ref, dr_ref, g_ref, *rest):
        dx_ref, dg_ref, acc = rest[-3:]
        i = pl.program_id(0)
        j = pl.program_id(1)

        @pl.when(j == 0)
        def _():
            acc[...] = jnp.zeros_like(acc)

        @pl.when(jnp.logical_and(i == 0, j == 0))
        def _():
            dg_ref[...] = jnp.zeros_like(dg_ref)

        acc[...] += _dot_nt(dz_ref[...], w_ref[...])

        @pl.when(j == nj - 1)
        def _():
            xv = x_ref[...]
            dx, dgt = _rms_bwd(acc[...], xv, _rms(xv), g_ref[...])
            dx_ref[...] = dr_ref[...] + dx
            dg_ref[...] += jnp.sum(dgt, axis=0, keepdims=True)

    row = pl.BlockSpec((tm, d), lambda i, j: (i + first, 0))
    vec = pl.BlockSpec((1, d), lambda i, j: (0, 0))
    kept = [] if into is None else [into]
    return _call_riding(
        body,
        exs,
        args=(dz, w4, x, dres, g, *kept),
        scratch_shapes=[pltpu.VMEM((tm, d), F32)],
        grid=(stop - first, nj),
        in_specs=[
            pl.BlockSpec((tm, cw), lambda i, j: (i + first, j)),
            pl.BlockSpec((None, d, cw), lambda i, j: (j, 0, 0)),
            row,
            row,
            vec,
        ]
        + [_ANY] * len(kept),
        out_specs=[row, vec],
        out_shape=[jax.ShapeDtypeStruct((s, d), F32), jax.ShapeDtypeStruct((1, d), F32)],
        input_output_aliases={5: 0} if kept else {},
        compiler_params=_params(2),
        name=name,
    )


_ANY = pl.BlockSpec(memory_space=pl.ANY)


def _position():
    x, y, c = lax.axis_index("x"), lax.axis_index("y"), lax.axis_index("c")
    return x, y, c, [(1 - x, y), (x, 1 - y), (1 - x, 1 - y)]


def _remote(src, dst, send_sems, recv_sems, k, to):
    return pltpu.make_async_remote_copy(
        src_ref=src, dst_ref=dst, send_sem=send_sems.at[k], recv_sem=recv_sems.at[k], device_id=to, device_id_type=MESH
    )


class _Riding:
    def __init__(self, srcs, out_shapes, n_sems, start, finish, kept=()):
        self.srcs, self.out_shapes, self.n_sems, self.start, self.finish = srcs, out_shapes, n_sems, start, finish
        self.kept = kept
        self.relay = None


def _call_riding(body, exs, *, grid, in_specs, out_specs, out_shape, scratch_shapes, args, **kw):
    n_in, n_out, n_scr = len(in_specs), len(out_specs), len(scratch_shapes)
    spec = dict(grid=grid, in_specs=list(in_specs), out_specs=list(out_specs), out_shape=list(out_shape))
    if not exs:
        return _call(body, scratch_shapes=list(scratch_shapes), **spec, **kw)(*args), []
    srcs = [a for ex in exs for a in ex.srcs]
    lands = [a for ex in exs for a in ex.out_shapes]
    xi, xo = len(srcs), len(lands)

    def fused(*refs):
        cin, xin = refs[:n_in], refs[n_in : n_in + xi]
        o = n_in + xi
        cout, xout = refs[o : o + n_out], refs[o + n_out : o + n_out + xo]
        scr = refs[o + n_out + xo :]
        ids = [pl.program_id(a) for a in range(len(grid))]
        first = functools.reduce(jnp.logical_and, [i == 0 for i in ids])
        last = functools.reduce(jnp.logical_and, [i == n - 1 for i, n in zip(ids, grid)])

        step = functools.reduce(lambda acc, a: acc * grid[a] + ids[a], range(1, len(grid)), ids[0])

        def each(half):
            si = so = 0
            for e, ex in enumerate(exs):
                ni, no = len(ex.srcs), len(ex.out_shapes)
                sems = scr[n_scr + 2 * e : n_scr + 2 * e + 2]
                if getattr(ex, half) is not None:
                    getattr(ex, half)(xin[si : si + ni], xout[so : so + no], *sems)
                si, so = si + ni, so + no

        @pl.when(first)
        def _():
            each("start")

        @pl.when(step == (3 * math.prod(grid)) // 4)
        def _():
            each("relay")

        body(*cin, *cout, *scr[:n_scr])

        @pl.when(last)
        def _():
            each("finish")

    sems = [pltpu.SemaphoreType.DMA((ex.n_sems,)) for ex in exs for _ in range(2)]
    aliases = dict(kw.pop("input_output_aliases", {}))
    si = so = 0
    for ex in exs:
        for a, b in ex.kept:
            aliases[n_in + si + a] = n_out + so + b
        si, so = si + len(ex.srcs), so + len(ex.out_shapes)
    kw["input_output_aliases"] = aliases
    spec["in_specs"] += [_ANY] * xi
    spec["out_specs"] += [_ANY] * xo
    spec["out_shape"] += lands
    outs = _call(fused, scratch_shapes=list(scratch_shapes) + sems, **spec, **kw)(*args, *srcs)
    return outs[:n_out], outs[n_out:]


def _gather_one(shard, peers=(0, 1, 2), into=None, early_relay=False):
    def own_copies(ins, outs, send, recv):
        x, y, c, chips = _position()
        me = 2 * x + y
        sib = (x, y, 1 - c)
        cps = [] if into is not None else [_remote(ins[0].at[c], outs[0].at[me, c], send, recv, 0, sib)]
        cps += [_remote(ins[0].at[c], outs[0].at[me, c], send, recv, 1 + k, (*chips[k], c)) for k in peers]
        return cps, (c, me, sib, chips)

    def start(ins, outs, send, recv):
        for cp in own_copies(ins, outs, send, recv)[0]:
            cp.start()

    def passed_on(outs, send, recv, c, sib, chips):
        return [_remote(outs[0].at[2 * chips[k][0] + chips[k][1], c], outs[0].at[2 * chips[k][0] + chips[k][1], c],
                        send, recv, 4 + k, sib) for k in peers]

    def relay(ins, outs, send, recv):
        x, y, c, chips = _position()
        for k, cp in zip(peers, passed_on(outs, send, recv, c, (x, y, 1 - c), chips)):
            px, py = chips[k]
            _remote(ins[0].at[c], outs[0].at[2 * px + py, c], send, recv, 1 + k, (px, py, c)).wait_recv()
            cp.start()

    def finish(ins, outs, send, recv):
        if not early_relay:
            relay(ins, outs, send, recv)
        cps, (c, me, sib, chips) = own_copies(ins, outs, send, recv)
        if into is None:
            _remote(ins[0].at[1 - c], outs[0].at[me, 1 - c], send, recv, 0, sib).wait_recv()
        for k in peers:
            px, py = chips[k]
            theirs = outs[0].at[2 * px + py, 1 - c]
            _remote(theirs, theirs, send, recv, 4 + k, sib).wait_recv()
        for cp in cps + passed_on(outs, send, recv, c, sib, chips):
            cp.wait_send()

    land = jax.ShapeDtypeStruct((N_CHIPS,) + shard.shape, shard.dtype)
    rider = _Riding([shard] + ([] if into is None else [into]), [land], 7, start, finish, [] if into is None else [(1, 0)])
    rider.relay = relay if early_relay else None
    return rider


def _to_owner_one(p, peers=(0, 1, 2), into=None):
    def copies(ins, outs, send, recv):
        x, y, c, chips = _position()
        return [_remote(ins[0].at[2 * chips[k][0] + chips[k][1]], outs[0].at[k], send, recv, k, (*chips[k], c)) for k in peers]

    def start(ins, outs, send, recv):
        for cp in copies(ins, outs, send, recv):
            cp.start()

    def finish(ins, outs, send, recv):
        cps = copies(ins, outs, send, recv)
        for cp in cps:
            cp.wait_recv()
        for cp in cps:
            cp.wait_send()

    land = jax.ShapeDtypeStruct((3,) + p.shape[1:], p.dtype)
    return _Riding([p] + ([] if into is None else [into]), [land], 3, start, finish, [] if into is None else [(1, 0)])


def _to_sibling_one(g):
    def copies(ins, outs, send, recv):
        x, y, c, _ = _position()
        return [_remote(ins[0].at[j, 1 - c], outs[0].at[j], send, recv, j, (x, y, 1 - c)) for j in range(N_CHIPS)]

    def start(ins, outs, send, recv):
        for cp in copies(ins, outs, send, recv):
            cp.start()

    def finish(ins, outs, send, recv):
        cps = copies(ins, outs, send, recv)
        for cp in cps:
            cp.wait_recv()
        for cp in cps:
            cp.wait_send()

    return _Riding([g], [jax.ShapeDtypeStruct((g.shape[0],) + g.shape[2:], g.dtype)], N_CHIPS, start, finish)


def _gather_chips(arrs, name):
    n = len(arrs)

    def body(*refs):
        ins, outs = refs[:n], refs[n : 2 * n]
        send_sems, recv_sems, local_sems = refs[2 * n :]
        x, y, c, chips = _position()
        me = 2 * x + y
        local = [pltpu.make_async_copy(ins[i], outs[i].at[me], local_sems.at[i]) for i in range(n)]
        for cp in local:
            cp.start()
        sends = []
        for k, (px, py) in enumerate(chips):
            for i in range(n):
                sends.append(_remote(ins[i], outs[i].at[me], send_sems, recv_sems, k * n + i, (px, py, c)))
                sends[-1].start()
        for k, (px, py) in enumerate(chips):
            for i in range(n):
                _remote(ins[i], outs[i].at[2 * px + py], send_sems, recv_sems, k * n + i, (px, py, c)).wait_recv()
        for cp in sends:
            cp.wait_send()
        for cp in local:
            cp.wait()

    return _call(
        body,
        in_specs=[_ANY] * n,
        out_specs=[_ANY] * n,
        out_shape=[jax.ShapeDtypeStruct((N_CHIPS,) + a.shape, a.dtype) for a in arrs],
        scratch_shapes=[
            pltpu.SemaphoreType.DMA((3 * n,)),
            pltpu.SemaphoreType.DMA((3 * n,)),
            pltpu.SemaphoreType.DMA((n,)),
        ],
        name=name,
    )(*arrs)


def _chip_exchange(ps, name):
    n = len(ps)

    def body(*refs):
        ins, outs = refs[:n], refs[n : 2 * n]
        send_sems, recv_sems = refs[2 * n :]
        x, y, c, chips = _position()
        sends = []
        for k, (px, py) in enumerate(chips):
            for i in range(n):
                sends.append(_remote(ins[i].at[2 * px + py], outs[i].at[k], send_sems, recv_sems, k * n + i, (px, py, c)))
                sends[-1].start()
        for cp in sends:
            cp.wait_recv()
        for cp in sends:
            cp.wait_send()

    return _call(
        body,
        in_specs=[_ANY] * n,
        out_specs=[_ANY] * n,
        out_shape=[jax.ShapeDtypeStruct((3,) + p.shape[1:], p.dtype) for p in ps],
        scratch_shapes=[pltpu.SemaphoreType.DMA((3 * n,)), pltpu.SemaphoreType.DMA((3 * n,))],
        name=name,
    )(*ps)


def _pair_exchange(gs, name):
    n = len(gs)

    def body(*refs):
        ins, outs = refs[:n], refs[n : 2 * n]
        send_sems, recv_sems = refs[2 * n :]
        x, y, c, _ = _position()
        sib = (x, y, 1 - c)
        cps = []
        for i in range(n):
            for j in range(N_CHIPS):
                cps.append(_remote(ins[i].at[j, 1 - c], outs[i].at[j], send_sems, recv_sems, i * N_CHIPS + j, sib))
                cps[-1].start()
        for cp in cps:
            cp.wait_recv()
        for cp in cps:
            cp.wait_send()

    return _call(
        body,
        in_specs=[_ANY] * n,
        out_specs=[_ANY] * n,
        out_shape=[jax.ShapeDtypeStruct((g.shape[0],) + g.shape[2:], g.dtype) for g in gs],
        scratch_shapes=[pltpu.SemaphoreType.DMA((n * N_CHIPS,)), pltpu.SemaphoreType.DMA((n * N_CHIPS,))],
        name=name,
    )(*gs)


def _place_own(shard, gathered, pos, name):
    _, hr, cols = shard.shape
    tr = _row_tile(hr, 512)

    def body(pos_ref, s_ref, g_any, o_ref):
        del pos_ref, g_any
        o_ref[...] = s_ref[...]

    return _call(
        body,
        grid_spec=pltpu.PrefetchScalarGridSpec(
            num_scalar_prefetch=1,
            grid=(hr // tr,),
            in_specs=[pl.BlockSpec((None, tr, cols), lambda i, p: (p[1], i, 0)), _ANY],
            out_specs=pl.BlockSpec((None, None, tr, cols), lambda i, p: (p[0], p[1], i, 0)),
        ),
        out_shape=jax.ShapeDtypeStruct(gathered.shape, gathered.dtype),
        input_output_aliases={2: 0},
        compiler_params=_params(1),
        name=name,
    )(pos, shard, gathered)


def _pair_share(fs, name):
    n = len(fs)

    def body(*refs):
        ins, outs = refs[:n], refs[n : 2 * n]
        send_sems, recv_sems = refs[2 * n :]
        x, y, c, _ = _position()
        sib = (x, y, 1 - c)
        sends = [_remote(ins[i].at[c], outs[i].at[c], send_sems, recv_sems, i, sib) for i in range(n)]
        for cp in sends:
            cp.start()
        for i in range(n):
            _remote(ins[i].at[1 - c], outs[i].at[1 - c], send_sems, recv_sems, i, sib).wait_recv()
        for cp in sends:
            cp.wait_send()

    return _call(
        body,
        in_specs=[_ANY] * n,
        out_specs=[_ANY] * n,
        out_shape=[jax.ShapeDtypeStruct(f.shape, f.dtype) for f in fs],
        input_output_aliases={i: i for i in range(n)},
        scratch_shapes=[pltpu.SemaphoreType.DMA((n,)), pltpu.SemaphoreType.DMA((n,))],
        name=name,
    )(*fs)


def _pair_sum(g, r, core, name):
    nj, _, hr, cols = g.shape
    tr = _row_tile(hr, 256)

    def body(c_ref, g_ref, r_ref, o_ref):
        del c_ref
        o_ref[...] = (g_ref[...].astype(F32) + r_ref[...].astype(F32)).astype(o_ref.dtype)

    return _call(
        body,
        grid_spec=pltpu.PrefetchScalarGridSpec(
            num_scalar_prefetch=1,
            grid=(nj, hr // tr),
            in_specs=[
                pl.BlockSpec((None, None, tr, cols), lambda j, i, c: (j, c[0], i, 0)),
                pl.BlockSpec((None, tr, cols), lambda j, i, c: (j, i, 0)),
            ],
            out_specs=pl.BlockSpec((None, tr, cols), lambda j, i, c: (j, i, 0)),
        ),
        out_shape=jax.ShapeDtypeStruct((nj, hr, cols), g.dtype),
        compiler_params=_params(2),
        name=name,
    )(core, g, r)


def _chip_sum(p, q, pos, name):
    nq, hr, cols = q.shape
    tr = _row_tile(hr, 256)

    def body(pos_ref, p_ref, *refs):
        del pos_ref
        o_ref = refs[nq]
        tot = p_ref[...].astype(F32)
        for k in range(nq):
            tot = tot + refs[k][...].astype(F32)
        o_ref[...] = tot

    return _call(
        body,
        grid_spec=pltpu.PrefetchScalarGridSpec(
            num_scalar_prefetch=1,
            grid=(hr // tr,),
            in_specs=[pl.BlockSpec((None, tr, cols), lambda i, s: (s[0], i, 0))]
            + [pl.BlockSpec((None, tr, cols), functools.partial(lambda k, i, s: (k, i, 0), k)) for k in range(nq)],
            out_specs=pl.BlockSpec((None, tr, cols), lambda i, s: (s[1], i, 0)),
        ),
        out_shape=jax.ShapeDtypeStruct((2, hr, cols), F32),
        compiler_params=_params(1),
        name=name,
    )(pos, p, *([q] * nq))


def _adamw(w, g, m, v, name):
    rows, cols = w.shape
    tr = _row_tile(rows, max(8, (2**18) // cols), mult=8)
    c1 = 1.0 - ADAM_B1**ADAM_STEP
    c2 = 1.0 - ADAM_B2**ADAM_STEP

    def body(w_ref, g_ref, m_ref, v_ref, gk_ref, d_ref, nm_ref, nv_ref):
        gv = g_ref[...]
        gk_ref[...] = gv
        nm = ADAM_B1 * m_ref[...] + (1.0 - ADAM_B1) * gv
        nv = ADAM_B2 * v_ref[...] + (1.0 - ADAM_B2) * (gv * gv)
        nm_ref[...] = nm
        nv_ref[...] = nv
        d_ref[...] = -ADAM_LR * ((nm / c1) / (jnp.sqrt(nv / c2) + ADAM_EPS) + ADAM_WD * w_ref[...])

    spec = pl.BlockSpec((tr, cols), lambda i: (i, 0))
    sds = jax.ShapeDtypeStruct((rows, cols), F32)
    return _call(
        body,
        grid=(rows // tr,),
        in_specs=[spec] * 4,
        out_specs=[spec] * 4,
        out_shape=[sds] * 4,
        compiler_params=_params(1),
        name=name,
    )(w, g, m, v)


_TILE_ELEMS = 8 * LANES
_FLAT_ROW_MULT = 8 * 2 * N_CHIPS


def _flat_rows(shape):
    n = math.prod(shape)
    return (n + _TILE_ELEMS - 1) // _TILE_ELEMS * 8


def _pack(arrs):
    parts = []
    for a in arrs:
        rows = _flat_rows(a.shape)
        flat = a.reshape(-1).astype(F32)
        flat = jnp.pad(flat, (0, rows * LANES - flat.shape[0]))
        parts.append(flat.reshape(rows, LANES))
    total = sum(p.shape[0] for p in parts)
    pad = -total % _FLAT_ROW_MULT
    if pad:
        parts.append(jnp.zeros((pad, LANES), F32))
    return jnp.concatenate(parts, axis=0)


def _unpack(flat, shapes):
    out, row = [], 0
    for shp in shapes:
        rows = _flat_rows(shp)
        out.append(flat[row : row + rows].reshape(-1)[: math.prod(shp)].reshape(shp))
        row += rows
    return out


def kernel(x, norm_mix_g, w_in, ssm_a_re, ssm_a_im, ssm_b_re, ssm_b_im, ssm_c_re, ssm_c_im, ssm_d, ssm_log_dt, ssm_glu_w, ssm_glu_b, sgu_ln_g, sgu_ln_b, sgu_w, sgu_b, out_norm_ssm_g, out_norm_sgu_g, w_out, norm_mlp_g, w_up, w_down, norm_final_g, loss_target, m_norm_mix_g, m_w_in, m_ssm_a_re, m_ssm_a_im, m_ssm_b_re, m_ssm_b_im, m_ssm_c_re, m_ssm_c_im, m_ssm_d, m_ssm_log_dt, m_ssm_glu_w, m_ssm_glu_b, m_sgu_ln_g, m_sgu_ln_b, m_sgu_w, m_sgu_b, m_out_norm_ssm_g, m_out_norm_sgu_g, m_w_out, m_norm_mlp_g, m_w_up, m_w_down, m_norm_final_g, v_norm_mix_g, v_w_in, v_ssm_a_re, v_ssm_a_im, v_ssm_b_re, v_ssm_b_im, v_ssm_c_re, v_ssm_c_im, v_ssm_d, v_ssm_log_dt, v_ssm_glu_w, v_ssm_glu_b, v_sgu_ln_g, v_sgu_ln_b, v_sgu_w, v_sgu_b, v_out_norm_ssm_g, v_out_norm_sgu_g, v_w_out, v_norm_mlp_g, v_w_up, v_w_down, v_norm_final_g):
    weights = dict(norm_mix_g=norm_mix_g, w_in=w_in, ssm_a_re=ssm_a_re, ssm_a_im=ssm_a_im, ssm_b_re=ssm_b_re, ssm_b_im=ssm_b_im, ssm_c_re=ssm_c_re, ssm_c_im=ssm_c_im, ssm_d=ssm_d, ssm_log_dt=ssm_log_dt, ssm_glu_w=ssm_glu_w, ssm_glu_b=ssm_glu_b, sgu_ln_g=sgu_ln_g, sgu_ln_b=sgu_ln_b, sgu_w=sgu_w, sgu_b=sgu_b, out_norm_ssm_g=out_norm_ssm_g, out_norm_sgu_g=out_norm_sgu_g, w_out=w_out, norm_mlp_g=norm_mlp_g, w_up=w_up, w_down=w_down, norm_final_g=norm_final_g)
    mom_m = dict(norm_mix_g=m_norm_mix_g, w_in=m_w_in, ssm_a_re=m_ssm_a_re, ssm_a_im=m_ssm_a_im, ssm_b_re=m_ssm_b_re, ssm_b_im=m_ssm_b_im, ssm_c_re=m_ssm_c_re, ssm_c_im=m_ssm_c_im, ssm_d=m_ssm_d, ssm_log_dt=m_ssm_log_dt, ssm_glu_w=m_ssm_glu_w, ssm_glu_b=m_ssm_glu_b, sgu_ln_g=m_sgu_ln_g, sgu_ln_b=m_sgu_ln_b, sgu_w=m_sgu_w, sgu_b=m_sgu_b, out_norm_ssm_g=m_out_norm_ssm_g, out_norm_sgu_g=m_out_norm_sgu_g, w_out=m_w_out, norm_mlp_g=m_norm_mlp_g, w_up=m_w_up, w_down=m_w_down, norm_final_g=m_norm_final_g)
    mom_v = dict(norm_mix_g=v_norm_mix_g, w_in=v_w_in, ssm_a_re=v_ssm_a_re, ssm_a_im=v_ssm_a_im, ssm_b_re=v_ssm_b_re, ssm_b_im=v_ssm_b_im, ssm_c_re=v_ssm_c_re, ssm_c_im=v_ssm_c_im, ssm_d=v_ssm_d, ssm_log_dt=v_ssm_log_dt, ssm_glu_w=v_ssm_glu_w, ssm_glu_b=v_ssm_glu_b, sgu_ln_g=v_sgu_ln_g, sgu_ln_b=v_sgu_ln_b, sgu_w=v_sgu_w, sgu_b=v_sgu_b, out_norm_ssm_g=v_out_norm_ssm_g, out_norm_sgu_g=v_out_norm_sgu_g, w_out=v_w_out, norm_mlp_g=v_norm_mlp_g, w_up=v_w_up, w_down=v_w_down, norm_final_g=v_norm_final_g)
    names = list(weights)
    large = ["w_in", "ssm_glu_w", "w_out", "w_up", "w_down"]
    small = [n for n in names if n not in large]

    s, d = x.shape[1], x.shape[2]
    xs = x.reshape(s, d)
    target = loss_target.reshape(s, d)
    width = ssm_glu_w.shape[-1]
    ff = w_down.shape[1] * N_CHIPS
    tm = min(512, s)
    core = lax.axis_index("c").astype(jnp.int32).reshape(1)
    chip = (2 * lax.axis_index("x") + lax.axis_index("y")).astype(jnp.int32).reshape(1)
    pos = jnp.concatenate([chip, core])

    shards = [w[0].astype(BF16).reshape(2, w.shape[1] // 2, w.shape[2]) for w in (w_in, ssm_glu_w, w_out, w_up, w_down)]
    ssm_args = (ssm_a_re[0], ssm_a_im[0], ssm_b_re[0], ssm_b_im[0], ssm_c_re[0], ssm_c_im[0], ssm_d[0], ssm_log_dt[0])
    (k2c, w2c, v2c, al), ssm_vjp = jax.vjp(_ssm_mats, *ssm_args)
    consts = _spread_consts(ssm_b_re.shape[-1], ssm_a_re.shape[-1])
    (k2b, w2b, v2b), (w_in_g,) = _ssm_spread(k2c, w2c, v2c, consts, [_gather_one(shards[0])])
    w_in4 = _place_own(shards[0], w_in_g, pos, "place_w_in").reshape(N_CHIPS, d, w_in.shape[2])
    causal = jnp.tril(jnp.ones((SGU_CHUNK, SGU_CHUNK), dtype=bool))
    wt = jnp.where(causal[None], sgu_w[0], 0.0)
    wtb = wt.astype(BF16)
    wttb = jnp.swapaxes(wt, 1, 2).astype(BF16)
    heads = sgu_w.shape[1]
    biasb = jnp.broadcast_to(sgu_b[0][:, :, None], (heads, SGU_CHUNK, LANES))

    tm_big = min(1024, s)
    (z, h1b), (glu_g, out_g, up_g) = _in_proj(
        xs, norm_mix_g, w_in4, tm_big,
        [_gather_one(shards[1]), _gather_one(shards[2]), _gather_one(shards[3], peers=(0, 1))])
    wg_full = _place_own(shards[1], glu_g, pos, "place_glu_w").reshape(width, width)
    w_out_full = _place_own(shards[2], out_g, pos, "place_w_out").reshape(d, d)
    (y_pre, xprev, us), (up_g,) = _ssm_fwd(z, k2b, w2b, v2b, al, [_gather_one(shards[3], peers=(2,), into=up_g)])
    up4 = _place_own(shards[3], up_g, pos, "place_w_up").reshape(N_CHIPS, d, w_up.shape[2])
    mixed = _glu_fwd(y_pre, wg_full, ssm_glu_b, out_norm_ssm_g, d, tm)
    mixed = _sgu_fwd(z, mixed, sgu_ln_g, sgu_ln_b, wtb, biasb, out_norm_sgu_g, tm)
    x1, h2b = _out_proj(xs, mixed, w_out_full, norm_mlp_g, tm)
    tf = min(1024, up4.shape[-1])
    (up,), (down_g,) = _mlp_up(
        h2b, up4, tm_big, min(2048, up4.shape[-1]), [_gather_one(shards[4], early_relay=True)])
    w_down_full = _place_own(shards[4], down_g, pos, "place_w_down").reshape(ff, d)
    loss_part, dx2, dx2b, d_norm_final = _mlp_down_loss(
        x1, up, w_down_full, target, norm_final_g.reshape(1, d), tm, min(2048, up4.shape[-1]))

    dup, dh2 = _mlp_bwd(dx2b, up, up4, w_down_full, tm_big, min(1024, up4.shape[-1]))
    dx1, dx1b, d_norm_mlp, dmix = _norm_out_proj_bwd(dh2, x1, dx2, norm_mlp_g, w_out_full, tm)
    hr_big = d // 2
    tkk = min(2048, s)
    g_down = _grad_w(
        up, dx2b, (N_CHIPS, 2, hr_big, d),
        lambda i, j: (i // (2 * (hr_big // min(1024, hr_big))), (i // (hr_big // min(1024, hr_big))) % 2,
                      i % (hr_big // min(1024, hr_big)), j),
        min(1024, hr_big), d, tkk, True, "grad_w_down")
    tn_up = min(2048, up4.shape[-1])
    per_up = up4.shape[-1] // tn_up
    g_up, sib_down = _grad_w(
        h2b, dup, (N_CHIPS, 2, hr_big, up4.shape[-1]),
        lambda i, j: (j // per_up, i // (hr_big // min(1024, hr_big)), i % (hr_big // min(1024, hr_big)), j % per_up),
        min(1024, hr_big), tn_up, tkk, False, "grad_w_up", exs=[_to_sibling_one(g_down)])
    pair_down = _pair_sum(g_down, sib_down, core, "pair_sum_w_down")
    hr_out = d // (2 * N_CHIPS)
    g_out, sib_up = _grad_w(
        mixed, dx1b, (1, 1, d, d), lambda i, j: (0, 0, i, j), min(1024, d), d, tkk, False, "grad_w_out",
        exs=[_to_sibling_one(g_up)])
    g_out = g_out.reshape(N_CHIPS, 2, hr_out, d)
    pair_up = _pair_sum(g_up, sib_up, core, "pair_sum_w_up")
    (dy_pre, g_glu, d_glu_b, d_norm_ssm), (sib_out,) = _glu_bwd(
        y_pre, dmix, wg_full, ssm_glu_b, out_norm_ssm_g, tm, [_to_sibling_one(g_out)])
    pair_out = _pair_sum(g_out, sib_out, core, "pair_sum_w_out")
    dys, gst, d_al = _ssm_bwd_state(dy_pre, v2b, al, xprev)
    (dz, d_k2, d_w2, d_v2), (chips_down,) = _ssm_bwd_main(
        us, dys, k2b, w2b, xprev, gst, consts, s, z.shape[1], [_to_owner_one(pair_down)])
    (dz, d_wt, d_bias, d_ln_g, d_ln_b, d_norm_sgu), (chips_up,) = _sgu_bwd(
        z, dmix, dz, sgu_ln_g, sgu_ln_b, wtb, wttb, biasb, out_norm_sgu_g, tm,
        [_to_owner_one(pair_up, peers=(0, 1))])
    cw_in = w_in4.shape[-1]
    g_in, chips_up = _grad_w(
        h1b, dz, (N_CHIPS, 2, hr_big, cw_in),
        lambda i, j: (j, i // (hr_big // min(1024, hr_big)), i % (hr_big // min(1024, hr_big)), 0),
        min(1024, hr_big), 2 * cw_in, tkk, False, "grad_w_in",
        exs=[_to_owner_one(pair_up, peers=(2,), into=chips_up)], chunks=2)
    (sib_in,) = _pair_exchange([g_in], "w_in_to_sibling")
    pair_in = _pair_sum(g_in, sib_in, core, "pair_sum_w_in")
    tm_x = min(512, s // 2)
    n_x = s // tm_x
    (gx_half, dg_a), (chips_in,) = _in_proj_bwd(
        dz, w_in4, xs, dx1, norm_mix_g, tm_x, [_to_owner_one(pair_in)], (0, n_x // 2), None, "in_proj_bwd_a")
    (grad_x, dg_b), (chips_out,) = _in_proj_bwd(
        dz, w_in4, xs, dx1, norm_mix_g, tm_x, [_to_owner_one(pair_out)], (n_x // 2, n_x), gx_half, "in_proj_bwd_b")
    d_norm_mix = dg_a + dg_b

    d_ssm = ssm_vjp((d_k2, d_w2, d_v2, d_al))
    small_grads = dict(
        norm_mix_g=d_norm_mix, ssm_a_re=d_ssm[0], ssm_a_im=d_ssm[1], ssm_b_re=d_ssm[2], ssm_b_im=d_ssm[3],
        ssm_c_re=d_ssm[4], ssm_c_im=d_ssm[5], ssm_d=d_ssm[6], ssm_log_dt=d_ssm[7], ssm_glu_b=d_glu_b,
        sgu_ln_g=d_ln_g, sgu_ln_b=d_ln_b, sgu_w=jnp.where(causal[None], d_wt, 0.0), sgu_b=d_bias[:, :, 0],
        out_norm_ssm_g=d_norm_ssm, out_norm_sgu_g=d_norm_sgu, norm_mlp_g=d_norm_mlp, norm_final_g=d_norm_final)
    flat = _pack([small_grads[n] for n in small])
    hr_small = flat.shape[0] // (2 * N_CHIPS)
    g_small = flat.reshape(N_CHIPS, 2, hr_small, LANES)

    hr_glu = width // (2 * N_CHIPS)
    grads = [g_glu.reshape(N_CHIPS, 2, hr_glu, width), g_small]
    tags = ["glu_w", "small"]
    from_sib = _pair_exchange(grads, "grads_to_sibling")
    pair = [_pair_sum(g, r, core, "pair_sum_" + t) for g, r, t in zip(grads, from_sib, tags)]
    from_chips = _chip_exchange(pair, "grads_to_owner")
    pair = [pair_in, pair[0], pair_out, pair_up, pair_down, pair[1]]
    from_chips = [chips_in, from_chips[0], chips_out, chips_up, chips_down, from_chips[1]]
    tags = ["w_in", "glu_w", "w_out", "w_up", "w_down", "small"]
    halves = [_chip_sum(p, q, pos, "chip_sum_" + t) for p, q, t in zip(pair, from_chips, tags)]
    owned = _pair_share(halves, "grads_to_both_cores")
    (small_all,) = _gather_chips([owned[5]], "gather_small_grads")
    small_flat = small_all.reshape(flat.shape)

    grad_out, delta_out, m_out, v_out = {}, {}, {}, {}
    for n, g in zip(large, owned[:5]):
        shp = weights[n].shape
        g2, dl, nm, nv = _adamw(
            weights[n][0], g.reshape(shp[1], shp[2]), mom_m[n][0], mom_v[n][0], "adamw_" + n)
        grad_out[n], delta_out[n], m_out[n], v_out[n] = g2.reshape(shp), dl.reshape(shp), nm.reshape(shp), nv.reshape(shp)
    shapes = [weights[n].shape for n in small]
    g2, dl, nm, nv = _adamw(
        _pack([weights[n] for n in small]), small_flat, _pack([mom_m[n] for n in small]),
        _pack([mom_v[n] for n in small]), "adamw_small")
    for n, g, a, b, c in zip(small, _unpack(g2, shapes), _unpack(dl, shapes), _unpack(nm, shapes), _unpack(nv, shapes)):
        grad_out[n], delta_out[n], m_out[n], v_out[n] = g, a, b, c

    loss = lax.psum(loss_part[0, 0], ("x", "y", "c"))
    return (loss, grad_x.reshape(x.shape), *[grad_out[n] for n in names], *[delta_out[n] for n in names],
            *[m_out[n] for n in names], *[v_out[n] for n in names])
```
